```python
import jax, jax.numpy as jnp
from jax import lax
import numpy as np

D_MODEL = 2048
BATCH = 1
SEQ = 8192
DEPTH = 1

CHUNK = 64
Q_BLOCK = 128
HEAD_DIM = 128
FOX_HEADS = 8
FORGET_BIAS_INIT = 2.0
DSA_HEADS = 8
DSA_KV_HEADS = 2
IDX_HEADS = 16
IDX_DIM = 64
TOPK_MAX = 256
ROPE_THETA = 500000.0
ROT_FRACTION_DEN = 4
MIX_A = FOX_HEADS * HEAD_DIM
MIX_B = DSA_HEADS * HEAD_DIM
IN_SIZES = (MIX_A, MIX_A, MIX_A, FOX_HEADS,
            MIX_B, DSA_KV_HEADS * HEAD_DIM, DSA_KV_HEADS * HEAD_DIM,
            IDX_HEADS * IDX_DIM, IDX_DIM, IDX_HEADS,
            D_MODEL, D_MODEL)
D_IN = sum(IN_SIZES)
N_EXPERTS = 32
TOP_K = 4
D_FF = D_MODEL
SWIGLU_ALPHA = 1.702
SWIGLU_LIMIT = 7.0
MOE_BLOCK = 128
LN_EPS = 1e-5

kernel_name = "hybrid_fox_dsa_moe_deepnorm"


def layer_norm(x, g, b):
    xf = x.astype(jnp.float32)
    mu = jnp.mean(xf, axis=-1, keepdims=True)
    var = jnp.mean(jnp.square(xf - mu), axis=-1, keepdims=True)
    y = (xf - mu) * lax.rsqrt(var + LN_EPS) * g.astype(jnp.float32) + b.astype(jnp.float32)
    return y.astype(x.dtype)


def rope_tables(seq, rot_dim):
    pos = jnp.arange(seq, dtype=jnp.float32)
    inv = jnp.power(ROPE_THETA, -jnp.arange(0, rot_dim, 2, dtype=jnp.float32) / rot_dim)
    ang = pos[:, None] * inv[None, :]
    return jnp.cos(ang), jnp.sin(ang)


def partial_rope(x, cos, sin):
    half = cos.shape[-1]
    xf = x.astype(jnp.float32)
    x1, x2, rest = xf[..., :half], xf[..., half:2 * half], xf[..., 2 * half:]
    c, s = cos[None, :, None, :], sin[None, :, None, :]
    out = jnp.concatenate([x1 * c - x2 * s, x2 * c + x1 * s, rest], axis=-1)
    return out.astype(x.dtype)


def to_blocks(a):
    b, s = a.shape[:2]
    a = a.reshape((b, s // Q_BLOCK, Q_BLOCK) + a.shape[2:])
    return jnp.moveaxis(a, 1, 0)


def from_blocks(a):
    a = jnp.moveaxis(a, 0, 1)
    return a.reshape((a.shape[0], a.shape[1] * a.shape[2]) + a.shape[3:])


def forgetting_attention(q, k, v, f_logit, b_forget):
    S, d = q.shape[1], q.shape[-1]
    scale = d ** -0.5
    log_f = jax.nn.log_sigmoid(f_logit.astype(jnp.float32) + b_forget.astype(jnp.float32))
    c = jnp.cumsum(log_f, axis=1)
    c_k = jnp.transpose(c, (0, 2, 1))
    pos = jnp.arange(S)

    def block(args):
        qb, cqb, tb = args
        logits = jnp.einsum('bthd,bshd->bhts', qb, k).astype(jnp.float32) * scale
        logits = logits + jnp.transpose(cqb, (0, 2, 1))[..., None] - c_k[:, :, None, :]
        mask = pos[None, :] <= tb[:, None]
        logits = jnp.where(mask[None, None], logits, -jnp.inf)
        p = jax.nn.softmax(logits, axis=-1).astype(v.dtype)
        return jnp.einsum('bhts,bshd->bthd', p, v)

    out = lax.map(block, (to_blocks(q), to_blocks(c), pos.reshape(-1, Q_BLOCK)))
    return from_blocks(out)


def indexed_sparse_attention(q, k, v, iq, ik, iw):
    B, S, Hq, d = q.shape
    Hkv = k.shape[2]
    G = Hq // Hkv
    n_sel = min(TOPK_MAX, S // 4)
    scale = d ** -0.5
    idx_scale = (IDX_HEADS ** -0.5) * (IDX_DIM ** -0.5)
    pos = jnp.arange(S)
    chunk_id = pos // CHUNK
    gather = jax.vmap(lambda arr, ids: arr[ids])

    def block(args):
        qb, iqb, iwb, tb = args
        cq = tb // CHUNK
        rel = jax.nn.relu(jnp.einsum('bthe,bse->bths', iqb, ik).astype(jnp.float32))
        score = jnp.einsum('bths,bth->bts', rel, iwb.astype(jnp.float32) * idx_scale)
        adm = chunk_id[None, :] <= cq[:, None]
        score = jnp.where(adm[None], score, -jnp.inf)
        _, sel = lax.top_k(score, n_sel)
        kg = gather(k, sel)
        vg = gather(v, sel)
        valid = chunk_id[sel] <= cq[None, :, None]
        qg = qb.reshape(B, Q_BLOCK, Hkv, G, d)
        logits = jnp.einsum('btjgd,btnjd->btjgn', qg, kg).astype(jnp.float32) * scale
        logits = jnp.where(valid[:, :, None, None, :], logits, -jnp.inf)
        p = jax.nn.softmax(logits, axis=-1).astype(v.dtype)
        o = jnp.einsum('btjgn,btnjd->btjgd', p, vg)
        return o.reshape(B, Q_BLOCK, Hq, d)

    out = lax.map(block, (to_blocks(q), to_blocks(iq), to_blocks(iw), pos.reshape(-1, Q_BLOCK)))
    return from_blocks(out)


def moe_ffn(h, w_router, b_router, w_gate_up, b_gate_up, w_down, b_down):
    B, S, D = h.shape
    N = B * S
    hf = h.reshape(N, D)
    logits = (hf @ w_router).astype(jnp.float32) + b_router.astype(jnp.float32)
    top_vals, top_idx = lax.top_k(logits, TOP_K)
    gates = jax.nn.softmax(top_vals, axis=-1)
    nk = N * TOP_K
    e_flat = top_idx.reshape(nk)
    tok_flat = jnp.arange(nk, dtype=jnp.int32) // TOP_K
    g_flat = gates.reshape(nk)
    order = jnp.argsort(e_flat)
    e_s, tok_s, g_s = e_flat[order], tok_flat[order], g_flat[order]
    counts = jnp.bincount(e_flat, length=N_EXPERTS)
    starts = jnp.cumsum(counts) - counts
    padded = (counts + MOE_BLOCK - 1) // MOE_BLOCK * MOE_BLOCK
    pend = jnp.cumsum(padded)
    pstart = pend - padded
    dest = pstart[e_s] + jnp.arange(nk) - starts[e_s]
    n_blocks = (nk + MOE_BLOCK - 1) // MOE_BLOCK + N_EXPERTS
    n_slots = n_blocks * MOE_BLOCK
    slot_tok = jnp.zeros((n_slots,), jnp.int32).at[dest].set(tok_s)
    slot_gate = jnp.zeros((n_slots,), jnp.float32).at[dest].set(g_s)
    block_expert = jnp.minimum(
        jnp.searchsorted(pend, jnp.arange(n_blocks) * MOE_BLOCK, side='right'), N_EXPERTS - 1)
    xs = hf[slot_tok].reshape(n_blocks, MOE_BLOCK, D)

    def expert_block(args):
        xb, e = args
        gu = xb @ w_gate_up[e] + b_gate_up[e]
        g, u = jnp.split(gu, 2, axis=-1)
        g = jnp.minimum(g, SWIGLU_LIMIT)
        u = jnp.clip(u, -SWIGLU_LIMIT, SWIGLU_LIMIT)
        act = g * jax.nn.sigmoid(SWIGLU_ALPHA * g) * (u + 1.0)
        return act @ w_down[e] + b_down[e]

    ys = lax.map(expert_block, (xs, block_expert)).reshape(n_slots, D)
    y = jnp.zeros((N, D), h.dtype).at[slot_tok].add((ys * slot_gate[:, None]).astype(h.dtype))
    return y.reshape(B, S, D)


def setup_inputs(seed: int = 0) -> dict:
    key = jax.random.key(seed)
    ks = jax.random.split(key, 16)
    beta = (8.0 * DEPTH) ** -0.25

    def nrm(k, shape, scale):
        return jax.random.normal(k, shape, jnp.float32) * scale

    return {
        "x": nrm(ks[0], (BATCH, SEQ, D_MODEL), 1.0),
        "w_in": nrm(ks[1], (DEPTH, D_MODEL, D_IN), D_MODEL ** -0.5),
        "b_forget": FORGET_BIAS_INIT + nrm(ks[2], (DEPTH, FOX_HEADS), 0.1),
        "w_branch_a": nrm(ks[3], (DEPTH, MIX_A, D_MODEL), beta * MIX_A ** -0.5),
        "w_branch_b": nrm(ks[4], (DEPTH, MIX_B, D_MODEL), beta * MIX_B ** -0.5),
        "w_out": nrm(ks[5], (DEPTH, D_MODEL, D_MODEL), beta * D_MODEL ** -0.5),
        "ln1_g": 1.0 + nrm(ks[6], (DEPTH, D_MODEL), 0.02),
        "ln1_b": nrm(ks[7], (DEPTH, D_MODEL), 0.02),
        "w_router": nrm(ks[8], (DEPTH, D_MODEL, N_EXPERTS), D_MODEL ** -0.5),
        "b_router": nrm(ks[9], (DEPTH, N_EXPERTS), 0.01),
        "w_gate_up": nrm(ks[10], (DEPTH, N_EXPERTS, D_MODEL, 2 * D_FF), D_MODEL ** -0.5),
        "b_gate_up": nrm(ks[11], (DEPTH, N_EXPERTS, 2 * D_FF), 0.01),
        "w_down": nrm(ks[12], (DEPTH, N_EXPERTS, D_FF, D_MODEL), beta * D_FF ** -0.5),
        "b_down": nrm(ks[13], (DEPTH, N_EXPERTS, D_MODEL), 0.01),
        "ln2_g": 1.0 + nrm(ks[14], (DEPTH, D_MODEL), 0.02),
        "ln2_b": nrm(ks[15], (DEPTH, D_MODEL), 0.02),
    }


def reference(x, w_in, b_forget, w_branch_a, w_branch_b, w_out, ln1_g, ln1_b,
              w_router, b_router, w_gate_up, b_gate_up, w_down, b_down, ln2_g, ln2_b):
    B, S, _ = x.shape
    alpha = (2.0 * DEPTH) ** 0.25
    split_points = np.cumsum(IN_SIZES)[:-1].tolist()
    cos, sin = rope_tables(S, HEAD_DIM // ROT_FRACTION_DEN)
    cos_i, sin_i = rope_tables(S, IDX_DIM // ROT_FRACTION_DEN)

    def heads(t, n):
        return t.reshape(B, S, n, -1)

    for l in range(DEPTH):
        p = x @ w_in[l]
        aq, ak, av, af, bq, bk, bv, iq, ik, iw, ga, gb = jnp.split(p, split_points, axis=-1)
        a_out = forgetting_attention(heads(aq, FOX_HEADS), heads(ak, FOX_HEADS),
                                     heads(av, FOX_HEADS), af, b_forget[l]).reshape(B, S, MIX_A)
        bq_r = partial_rope(heads(bq, DSA_HEADS), cos, sin)
        bk_r = partial_rope(heads(bk, DSA_KV_HEADS), cos, sin)
        iq_r = partial_rope(heads(iq, IDX_HEADS), cos_i, sin_i)
        ik_r = partial_rope(ik[:, :, None, :], cos_i, sin_i)[:, :, 0, :]
        b_out = indexed_sparse_attention(bq_r, bk_r, heads(bv, DSA_KV_HEADS),
                                         iq_r, ik_r, iw).reshape(B, S, MIX_B)
        merged = (jax.nn.sigmoid(ga) * (a_out @ w_branch_a[l])
                  + jax.nn.sigmoid(gb) * (b_out @ w_branch_b[l]))
        x = layer_norm(alpha * x + merged @ w_out[l], ln1_g[l], ln1_b[l])
        moe = moe_ffn(x, w_router[l], b_router[l], w_gate_up[l], b_gate_up[l],
                      w_down[l], b_down[l])
        x = layer_norm(alpha * x + moe, ln2_g[l], ln2_b[l])
    return x
```

```python
import functools

import numpy as np
import jax
import jax.numpy as jnp
from jax import lax
from jax.experimental import pallas as pl
from jax.experimental.pallas import tpu as pltpu

F32 = jnp.float32
BF16 = jnp.bfloat16
I32 = jnp.int32

D_MODEL = 2048
DEPTH = 1
CHUNK = 64
HEAD_DIM = 128
FOX_HEADS = 8
DSA_HEADS = 8
DSA_KV_HEADS = 2
DSA_GROUP = DSA_HEADS // DSA_KV_HEADS
IDX_HEADS = 16
IDX_DIM = 64
TOPK_MAX = 256
ROPE_THETA = 500000.0
ROT_FRACTION_DEN = 4
MIX_A = FOX_HEADS * HEAD_DIM
MIX_B = DSA_HEADS * HEAD_DIM
N_EXPERTS = 32
TOP_K = 4
D_FF = D_MODEL
SWIGLU_ALPHA = 1.702
SWIGLU_LIMIT = 7.0
LN_EPS = 1e-5

MIB = 1024 * 1024
NEG_BIG = -1e30
INT_MIN = -(2 ** 31)

COL_BQ = 0
COL_IQ = 1024
COL_AQ = 2048
COL_AK = 3072
COL_AV = 4096
COL_BK = 5120
COL_BV = 5376
COL_IK = 5632
N_PBF = 5888
PROJ_TN = 256
ROPE_HEAD_TILES = (0, 1, 2, 3, 20)
ROPE_IDX_TILES = (4, 5, 6, 7, 22)
COL_GA = 0
COL_GB = 2048
COL_SMALL = 4096
N_PG = 4224
PG_TN = 384

MOE_ROWS = 128


def _cparams(dims, vmem_mib):
    return pltpu.CompilerParams(dimension_semantics=dims, vmem_limit_bytes=vmem_mib * MIB)


def _tile_in(j, tiles):
    cond = j == tiles[0]
    for t in tiles[1:]:
        cond = cond | (j == t)
    return cond


def _proj_rope_kernel(x_ref, w_ref, tab_ref, o_ref, xb_ref):
    j = pl.program_id(1)

    @pl.when(j == 0)
    def _():
        xb_ref[...] = x_ref[...].astype(BF16)

    acc = jnp.dot(xb_ref[...], w_ref[...], preferred_element_type=F32)
    tn = acc.shape[1]
    is_head = _tile_in(j, ROPE_HEAD_TILES)
    is_idx = _tile_in(j, ROPE_IDX_TILES)

    def rope(shift):
        c = tab_ref[0, 0]
        s_prev = tab_ref[0, 1]
        s_next = tab_ref[0, 2]
        out = acc * c + pltpu.roll(acc, shift, 1) * s_prev + pltpu.roll(acc, tn - shift, 1) * s_next
        o_ref[...] = out.astype(o_ref.dtype)

    @pl.when(is_head)
    def _():
        rope(HEAD_DIM // ROT_FRACTION_DEN // 2)

    @pl.when(is_idx)
    def _():
        rope(IDX_DIM // ROT_FRACTION_DEN // 2)

    @pl.when(jnp.logical_not(is_head | is_idx))
    def _():
        o_ref[...] = acc.astype(o_ref.dtype)


def _proj_plain_kernel(x_ref, w_ref, o_ref, xb_ref):
    j = pl.program_id(1)

    @pl.when(j == 0)
    def _():
        xb_ref[...] = x_ref[...].astype(BF16)

    o_ref[...] = jnp.dot(xb_ref[...], w_ref[...], preferred_element_type=F32).astype(o_ref.dtype)


def _rope_group(j):
    return jnp.where(_tile_in(j, ROPE_IDX_TILES), 1, 0)


def _project(x2d, w_bf, tables, w_g):
    s, d = x2d.shape
    tm = min(1024, s)
    pbf = pl.pallas_call(
        _proj_rope_kernel,
        grid=(s // tm, N_PBF // PROJ_TN),
        in_specs=[
            pl.BlockSpec((tm, d), lambda i, j: (i, 0)),
            pl.BlockSpec((d, PROJ_TN), lambda i, j: (0, j)),
            pl.BlockSpec((1, 3, tm, PROJ_TN), lambda i, j: (_rope_group(j), 0, i, 0)),
        ],
        out_specs=pl.BlockSpec((tm, PROJ_TN), lambda i, j: (i, j)),
        out_shape=jax.ShapeDtypeStruct((s, N_PBF), BF16),
        scratch_shapes=[pltpu.VMEM((tm, d), BF16)],
        compiler_params=_cparams(("arbitrary", "arbitrary"), 48),
        name="proj_bf16",
    )(x2d, w_bf, tables)
    pg = pl.pallas_call(
        _proj_plain_kernel,
        grid=(s // tm, N_PG // PG_TN),
        in_specs=[
            pl.BlockSpec((tm, d), lambda i, j: (i, 0)),
            pl.BlockSpec((d, PG_TN), lambda i, j: (0, j)),
        ],
        out_specs=pl.BlockSpec((tm, PG_TN), lambda i, j: (i, j)),
        out_shape=jax.ShapeDtypeStruct((s, N_PG), F32),
        scratch_shapes=[pltpu.VMEM((tm, d), BF16)],
        compiler_params=_cparams(("arbitrary", "arbitrary"), 48),
        name="proj_f32",
    )(x2d, w_g)
    return pbf, pg


def _rope_tables(s):
    pos = jnp.arange(s, dtype=F32)

    def one(period):
        rot = period // ROT_FRACTION_DEN
        half = rot // 2
        inv = jnp.power(ROPE_THETA, -jnp.arange(0, rot, 2, dtype=F32) / rot)
        ang = pos[:, None] * inv[None, :]
        cos, sin = jnp.cos(ang), jnp.sin(ang)
        zero = jnp.zeros((s, period - rot), F32)
        c = jnp.concatenate([cos, cos, jnp.ones((s, period - rot), F32)], axis=1)
        s_prev = jnp.concatenate([jnp.zeros((s, half), F32), sin, zero], axis=1)
        s_next = jnp.concatenate([-sin, jnp.zeros((s, half), F32), zero], axis=1)
        reps = PROJ_TN // period
        return jnp.stack([jnp.tile(c, (1, reps)), jnp.tile(s_prev, (1, reps)), jnp.tile(s_next, (1, reps))])

    return jnp.stack([one(HEAD_DIM), one(IDX_DIM)])


def _cumsum_kernel(af_ref, bf_ref, c_ref, carry_ref):
    i = pl.program_id(0)

    @pl.when(i == 0)
    def _():
        carry_ref[...] = jnp.zeros_like(carry_ref)

    z = af_ref[...] + bf_ref[...]
    logf = jnp.minimum(z, 0.0) - jnp.log1p(jnp.exp(-jnp.abs(z)))
    t = z.shape[1]
    row = lax.broadcasted_iota(I32, (t, t), 0)
    col = lax.broadcasted_iota(I32, (t, t), 1)
    upper = (row <= col).astype(F32)
    c = jnp.dot(logf, upper, preferred_element_type=F32, precision=lax.Precision.HIGHEST) + carry_ref[...]
    c_ref[...] = c
    carry_ref[...] = c[:, t - 1:t]


def _forget_cumsum(af_t, b_forget):
    h, s = af_t.shape
    t = min(512, s)
    return pl.pallas_call(
        _cumsum_kernel,
        grid=(s // t,),
        in_specs=[pl.BlockSpec((h, t), lambda i: (0, i)), pl.BlockSpec((h, 1), lambda i: (0, 0))],
        out_specs=pl.BlockSpec((h, t), lambda i: (0, i)),
        out_shape=jax.ShapeDtypeStruct((h, s), F32),
        scratch_shapes=[pltpu.VMEM((h, 1), F32)],
        compiler_params=_cparams(("arbitrary",), 32),
        name="forget_cumsum",
    )(af_t, b_forget.reshape(h, 1).astype(F32))


def _fox_kernel(q_ref, k_ref, v_ref, c_ref, o_ref, *, tq, tk):
    qi = pl.program_id(1)
    q = q_ref[...]

    def tile(kt, carry, masked):
        m, l, acc = carry
        start = pl.multiple_of(kt * tk, tk)
        k = k_ref[pl.ds(start, tk), :]
        v = v_ref[pl.ds(start, tk), :]
        s = lax.dot_general(q, k, (((1,), (1,)), ((), ())), preferred_element_type=F32)
        s = s - c_ref[kt]
        if masked:
            row = qi * tq + lax.broadcasted_iota(I32, (tq, tk), 0)
            col = kt * tk + lax.broadcasted_iota(I32, (tq, tk), 1)
            s = jnp.where(col <= row, s, NEG_BIG)
        m_new = jnp.maximum(m, jnp.max(s, axis=1, keepdims=True))
        p = jnp.exp(s - m_new)
        alpha = jnp.exp(m - m_new)
        l = alpha * l + jnp.sum(p, axis=1, keepdims=True)
        acc = alpha * acc + jnp.dot(p.astype(BF16), v, preferred_element_type=F32)
        return m_new, l, acc

    d = q.shape[1]
    init = (jnp.full((tq, 1), NEG_BIG, F32), jnp.zeros((tq, 1), F32), jnp.zeros((tq, d), F32))
    n_full = (qi * tq) // tk
    carry = lax.fori_loop(0, n_full, lambda kt, c: tile(kt, c, False), init)
    n_diag = (tq + tk - 1) // tk
    for t in range(n_diag):
        carry = tile(n_full + t, carry, True)
    m, l, acc = carry
    o_ref[...] = (acc / l).astype(o_ref.dtype)


def _fox_attention(pbf, c_tiles, s, tq, tk):
    cb = HEAD_DIM
    return pl.pallas_call(
        functools.partial(_fox_kernel, tq=tq, tk=tk),
        grid=(FOX_HEADS, s // tq),
        in_specs=[
            pl.BlockSpec((tq, cb), lambda h, i: (i, COL_AQ // cb + h)),
            pl.BlockSpec((s, cb), lambda h, i: (0, COL_AK // cb + h)),
            pl.BlockSpec((s, cb), lambda h, i: (0, COL_AV // cb + h)),
            pl.BlockSpec((None, s // tk, 1, tk), lambda h, i: (h, 0, 0, 0)),
        ],
        out_specs=pl.BlockSpec((tq, cb), lambda h, i: (i, h)),
        out_shape=jax.ShapeDtypeStruct((s, MIX_A), BF16),
        compiler_params=_cparams(("arbitrary", "arbitrary"), 48),
        name="fox_attention",
    )(pbf, pbf, pbf, c_tiles)


def _dsa_kernel(bq_ref, iq_ref, iw_ref, ik_ref, bk_ref, bv_ref, o_ref,
                key_ref, qg_ref, m_ref, l_ref, acc_ref, *, tq, tk, n_sel):
    b = pl.program_id(0)
    n_tiles = (b * tq) // tk + 1
    row_g = b * tq + lax.broadcasted_iota(I32, (tq, tk), 0)
    adm_end = (row_g // CHUNK + 1) * CHUNK
    col_l = lax.broadcasted_iota(I32, (tq, tk), 1)

    idx_scale = (IDX_HEADS ** -0.5) * (IDX_DIM ** -0.5)
    iw = iw_ref[...][:, 8:8 + IDX_HEADS] * idx_scale
    iq = iq_ref[...]

    def score_tile(kt, _):
        start = pl.multiple_of(kt * tk, tk)
        ik = ik_ref[pl.ds(start, tk), :][:, :IDX_DIM]
        acc = jnp.zeros((tq, tk), F32)
        for h in range(IDX_HEADS):
            a = iq[:, h * IDX_DIM:(h + 1) * IDX_DIM]
            rel = lax.dot_general(a, ik, (((1,), (1,)), ((), ())), preferred_element_type=F32)
            acc = acc + jnp.maximum(rel, 0.0) * iw[:, h:h + 1]
        bits = pltpu.bitcast(acc, I32)
        key = jnp.where(bits >= 0, bits, bits ^ 0x7FFFFFFF)
        key = jnp.where(kt * tk + col_l < adm_end, key, INT_MIN)
        key_ref[kt] = key
        return 0

    lax.fori_loop(0, n_tiles, score_tile, 0)

    def count_ge(cand):
        def body(kt, part):
            ge = jnp.where(key_ref[kt] >= cand, 1, 0)
            for c in range(tk // 128):
                part = part + ge[:, c * 128:(c + 1) * 128]
            return part
        part = lax.fori_loop(0, n_tiles, body, jnp.zeros((tq, 128), I32))
        return jnp.sum(part, axis=1, keepdims=True)

    def bit_step(i, t_u):
        bit = lax.shift_left(jnp.int32(1), 31 - i)
        cand_u = t_u | bit
        cnt = count_ge(cand_u ^ INT_MIN)
        return jnp.where(cnt >= n_sel, cand_u, t_u)

    t_u = lax.fori_loop(0, 32, bit_step, jnp.zeros((tq, 1), I32))
    thr = jnp.maximum(t_u ^ INT_MIN, INT_MIN + 1)

    for j in range(DSA_KV_HEADS):
        for g in range(DSA_GROUP):
            hd = j * DSA_GROUP + g
            qg_ref[j, g * tq:(g + 1) * tq, :] = bq_ref[:, hd * HEAD_DIM:(hd + 1) * HEAD_DIM]
    m_ref[...] = jnp.full(m_ref.shape, NEG_BIG, F32)
    l_ref[...] = jnp.zeros(l_ref.shape, F32)
    acc_ref[...] = jnp.zeros(acc_ref.shape, F32)

    def attn_tile(kt, _):
        start = pl.multiple_of(kt * tk, tk)
        sel = key_ref[kt] >= thr
        for j in range(DSA_KV_HEADS):
            k = bk_ref[pl.ds(start, tk), :][:, j * HEAD_DIM:(j + 1) * HEAD_DIM]
            v = bv_ref[pl.ds(start, tk), :][:, j * HEAD_DIM:(j + 1) * HEAD_DIM]
            s = lax.dot_general(qg_ref[j], k, (((1,), (1,)), ((), ())), preferred_element_type=F32)
            s = jnp.where(sel[None], s.reshape(DSA_GROUP, tq, tk), NEG_BIG).reshape(DSA_GROUP * tq, tk)
            m_old = m_ref[j]
            m_new = jnp.maximum(m_old, jnp.max(s, axis=1, keepdims=True))
            p = jnp.exp(s - m_new)
            alpha = jnp.exp(m_old - m_new)
            l_ref[j] = alpha * l_ref[j] + jnp.sum(p, axis=1, keepdims=True)
            acc_ref[j] = alpha * acc_ref[j] + jnp.dot(p.astype(BF16), v, preferred_element_type=F32)
            m_ref[j] = m_new
        return 0

    lax.fori_loop(0, n_tiles, attn_tile, 0)

    for j in range(DSA_KV_HEADS):
        o = acc_ref[j] / l_ref[j]
        for g in range(DSA_GROUP):
            hd = j * DSA_GROUP + g
            o_ref[:, hd * HEAD_DIM:(hd + 1) * HEAD_DIM] = o[g * tq:(g + 1) * tq].astype(o_ref.dtype)


def _dsa_attention(pbf, pg, s, tq, tk):
    n_sel = min(TOPK_MAX, s // 4)
    kvw = DSA_KV_HEADS * HEAD_DIM
    rows = DSA_GROUP * tq
    return pl.pallas_call(
        functools.partial(_dsa_kernel, tq=tq, tk=tk, n_sel=n_sel),
        grid=(s // tq,),
        in_specs=[
            pl.BlockSpec((tq, MIX_B), lambda b: (b, COL_BQ // MIX_B)),
            pl.BlockSpec((tq, IDX_HEADS * IDX_DIM), lambda b: (b, COL_IQ // (IDX_HEADS * IDX_DIM))),
            pl.BlockSpec((tq, 128), lambda b: (b, COL_SMALL // 128)),
            pl.BlockSpec((s, 128), lambda b: (0, COL_IK // 128)),
            pl.BlockSpec((s, kvw), lambda b: (0, COL_BK // kvw)),
            pl.BlockSpec((s, kvw), lambda b: (0, COL_BV // kvw)),
        ],
        out_specs=pl.BlockSpec((tq, MIX_B), lambda b: (b, 0)),
        out_shape=jax.ShapeDtypeStruct((s, MIX_B), BF16),
        scratch_shapes=[
            pltpu.VMEM((s // tk, tq, tk), I32),
            pltpu.VMEM((DSA_KV_HEADS, rows, HEAD_DIM), BF16),
            pltpu.VMEM((DSA_KV_HEADS, rows, 1), F32),
            pltpu.VMEM((DSA_KV_HEADS, rows, 1), F32),
            pltpu.VMEM((DSA_KV_HEADS, rows, HEAD_DIM), F32),
        ],
        compiler_params=_cparams(("arbitrary",), 56),
        name="dsa_attention",
    )(pbf, pbf, pg, pbf, pbf, pbf)


def _layer_norm(z, g, b):
    mu = jnp.mean(z, axis=-1, keepdims=True)
    zc = z - mu
    var = jnp.mean(zc * zc, axis=-1, keepdims=True)
    return zc * lax.rsqrt(var + LN_EPS) * g + b


def _merge_kernel(a_ref, b_ref, ga_ref, gb_ref, x_ref, wa_ref, wb_ref, wo_ref, g_ref, beta_ref,
                  wr_ref, br_ref, h_ref, eidx_ref, gate_ref, pos_ref, cnt_ref, carry_ref, *, alpha):
    i = pl.program_id(0)

    @pl.when(i == 0)
    def _():
        carry_ref[...] = jnp.zeros_like(carry_ref)

    ma = jnp.dot(a_ref[...], wa_ref[...], preferred_element_type=F32)
    mb = jnp.dot(b_ref[...], wb_ref[...], preferred_element_type=F32)
    merged = jax.nn.sigmoid(ga_ref[...]) * ma + jax.nn.sigmoid(gb_ref[...]) * mb
    y = jnp.dot(merged.astype(BF16), wo_ref[...], preferred_element_type=F32)
    h = _layer_norm(alpha * x_ref[...] + y, g_ref[...], beta_ref[...])
    h_ref[...] = h

    logits = jnp.dot(h, wr_ref[...], preferred_element_type=F32, precision=lax.Precision.HIGHEST) + br_ref[...]
    tm, ne = logits.shape
    lane = lax.broadcasted_iota(I32, (tm, ne), 1)
    lane_k = lax.broadcasted_iota(I32, (tm, TOP_K), 1)
    work = logits
    vals, sels = [], []
    eidx = jnp.zeros((tm, TOP_K), I32)
    onehot = jnp.zeros((tm, ne), F32)
    for k in range(TOP_K):
        mv = jnp.max(work, axis=1, keepdims=True)
        idx = jnp.min(jnp.where(work == mv, lane, ne), axis=1, keepdims=True)
        sel = lane == idx
        vals.append(mv)
        sels.append(sel)
        eidx = jnp.where(lane_k == k, idx, eidx)
        onehot = onehot + jnp.where(sel, 1.0, 0.0)
        work = jnp.where(sel, -jnp.inf, work)
    exps = [jnp.exp(v - vals[0]) for v in vals]
    denom = exps[0] + exps[1] + exps[2] + exps[3]
    gates = jnp.zeros((tm, TOP_K), F32)
    for k in range(TOP_K):
        gates = jnp.where(lane_k == k, exps[k] / denom, gates)

    r_i = lax.broadcasted_iota(I32, (tm, tm), 0)
    c_i = lax.broadcasted_iota(I32, (tm, tm), 1)
    lower = jnp.where(c_i < r_i, 1.0, 0.0).astype(BF16)
    rank = carry_ref[...] + jnp.dot(lower, onehot.astype(BF16), preferred_element_type=F32)
    pos = jnp.zeros((tm, TOP_K), I32)
    for k in range(TOP_K):
        pk = jnp.sum(jnp.where(sels[k], rank, 0.0), axis=1, keepdims=True).astype(I32)
        pos = jnp.where(lane_k == k, pk, pos)
    carry_ref[...] = carry_ref[...] + jnp.sum(onehot, axis=0, keepdims=True)

    eidx_ref[...] = eidx
    gate_ref[...] = gates
    pos_ref[...] = pos
    cnt_ref[...] = carry_ref[...].astype(I32)


def _merge(a_out, b_out, pg, x2d, wa, wb, wo, ln_g, ln_b, w_router, b_router, alpha):
    s, d = x2d.shape
    tm = min(256, s)
    full = lambda shape: pl.BlockSpec(shape, lambda i: (0,) * len(shape))
    return pl.pallas_call(
        functools.partial(_merge_kernel, alpha=alpha),
        grid=(s // tm,),
        in_specs=[
            pl.BlockSpec((tm, MIX_A), lambda i: (i, 0)),
            pl.BlockSpec((tm, MIX_B), lambda i: (i, 0)),
            pl.BlockSpec((tm, d), lambda i: (i, COL_GA // d)),
            pl.BlockSpec((tm, d), lambda i: (i, COL_GB // d)),
            pl.BlockSpec((tm, d), lambda i: (i, 0)),
            full((MIX_A, d)), full((MIX_B, d)), full((d, d)),
            full((1, d)), full((1, d)), full((d, N_EXPERTS)), full((1, N_EXPERTS)),
        ],
        out_specs=[
            pl.BlockSpec((tm, d), lambda i: (i, 0)),
            pl.BlockSpec((tm, TOP_K), lambda i: (i, 0)),
            pl.BlockSpec((tm, TOP_K), lambda i: (i, 0)),
            pl.BlockSpec((tm, TOP_K), lambda i: (i, 0)),
            full((1, N_EXPERTS)),
        ],
        out_shape=[
            jax.ShapeDtypeStruct((s, d), F32),
            jax.ShapeDtypeStruct((s, TOP_K), I32),
            jax.ShapeDtypeStruct((s, TOP_K), F32),
            jax.ShapeDtypeStruct((s, TOP_K), I32),
            jax.ShapeDtypeStruct((1, N_EXPERTS), I32),
        ],
        scratch_shapes=[pltpu.VMEM((1, N_EXPERTS), F32)],
        compiler_params=_cparams(("arbitrary",), 56),
        name="merge_ln_router",
    )(a_out, b_out, pg, pg, x2d, wa, wb, wo, ln_g, ln_b, w_router, b_router)


def _row_copy(src, src_row, dst, dst_row, sem):
    return pltpu.make_async_copy(src.at[pl.ds(src_row, 1)], dst.at[pl.ds(dst_row, 1)], sem)


def _dispatch_kernel(dest_ref, h_ref, xs_in_ref, xs_ref, sem):
    del xs_in_ref
    n = dest_ref.shape[0]

    def start(t, _):
        _row_copy(h_ref, t // TOP_K, xs_ref, dest_ref[t], sem).start()
        return 0

    def wait(t, _):
        _row_copy(h_ref, t // TOP_K, xs_ref, dest_ref[t], sem).wait()
        return 0

    lax.fori_loop(0, n, start, 0)
    lax.fori_loop(0, n, wait, 0)


def _dispatch(h, dest_flat, n_slots):
    s, d = h.shape
    tm = min(128, s)
    xs0 = jnp.zeros((n_slots, d), F32)
    return pl.pallas_call(
        _dispatch_kernel,
        grid=(s // tm,),
        in_specs=[
            pl.BlockSpec((tm * TOP_K,), lambda i: (i,), memory_space=pltpu.SMEM),
            pl.BlockSpec((tm, d), lambda i: (i, 0)),
            pl.BlockSpec(memory_space=pl.ANY),
        ],
        out_specs=pl.BlockSpec(memory_space=pl.ANY),
        out_shape=jax.ShapeDtypeStruct((n_slots, d), F32),
        scratch_shapes=[pltpu.SemaphoreType.DMA(())],
        input_output_aliases={2: 0},
        compiler_params=_cparams(("arbitrary",), 32),
        name="moe_dispatch",
    )(dest_flat, h, xs0)


def _gate_up_kernel(be_ref, nv_ref, x_ref, wg_ref, wu_ref, bg_ref, bu_ref, act_ref, wg_sc, wu_sc):
    r = pl.program_id(1)
    prev = be_ref[jnp.maximum(r - 1, 0)]
    fresh = (r == 0) | (be_ref[r] != prev)

    @pl.when(fresh)
    def _():
        wg_sc[...] = wg_ref[...].astype(BF16)
        wu_sc[...] = wu_ref[...].astype(BF16)

    @pl.when(r < nv_ref[0])
    def _():
        x = x_ref[...].astype(BF16)
        g = jnp.dot(x, wg_sc[...], preferred_element_type=F32) + bg_ref[...]
        u = jnp.dot(x, wu_sc[...], preferred_element_type=F32) + bu_ref[...]
        g = jnp.minimum(g, SWIGLU_LIMIT)
        u = jnp.clip(u, -SWIGLU_LIMIT, SWIGLU_LIMIT)
        act_ref[...] = (g * jax.nn.sigmoid(SWIGLU_ALPHA * g) * (u + 1.0)).astype(act_ref.dtype)

    @pl.when(r >= nv_ref[0])
    def _():
        act_ref[...] = jnp.zeros_like(act_ref)


def _down_kernel(be_ref, nv_ref, a_ref, wd_ref, bd_ref, y_ref, wd_sc):
    r = pl.program_id(1)
    prev = be_ref[jnp.maximum(r - 1, 0)]
    fresh = (r == 0) | (be_ref[r] != prev)

    @pl.when(fresh)
    def _():
        wd_sc[...] = wd_ref[...].astype(BF16)

    @pl.when(r < nv_ref[0])
    def _():
        y_ref[...] = jnp.dot(a_ref[...], wd_sc[...], preferred_element_type=F32) + bd_ref[...]

    @pl.when(r >= nv_ref[0])
    def _():
        y_ref[...] = jnp.zeros_like(y_ref)


def _experts(xs, block_expert, n_valid, w_gate_up, b_gate_up, w_down, b_down):
    n_slots, d = xs.shape
    n_blocks = n_slots // MOE_ROWS
    tf = 512
    nf = D_FF // tf
    bgu = b_gate_up.reshape(N_EXPERTS, 1, 2 * D_FF)
    act = pl.pallas_call(
        _gate_up_kernel,
        grid_spec=pltpu.PrefetchScalarGridSpec(
            num_scalar_prefetch=2,
            grid=(nf, n_blocks),
            in_specs=[
                pl.BlockSpec((MOE_ROWS, d), lambda j, r, be, nv: (r, 0)),
                pl.BlockSpec((None, d, tf), lambda j, r, be, nv: (be[r], 0, j)),
                pl.BlockSpec((None, d, tf), lambda j, r, be, nv: (be[r], 0, nf + j)),
                pl.BlockSpec((None, 1, tf), lambda j, r, be, nv: (be[r], 0, j)),
                pl.BlockSpec((None, 1, tf), lambda j, r, be, nv: (be[r], 0, nf + j)),
            ],
            out_specs=pl.BlockSpec((MOE_ROWS, tf), lambda j, r, be, nv: (r, j)),
            scratch_shapes=[pltpu.VMEM((d, tf), BF16), pltpu.VMEM((d, tf), BF16)],
        ),
        out_shape=jax.ShapeDtypeStruct((n_slots, D_FF), BF16),
        compiler_params=_cparams(("arbitrary", "arbitrary"), 48),
        name="moe_gate_up",
    )(block_expert, n_valid, xs, w_gate_up, w_gate_up, bgu, bgu)
    tn = 1024
    nn = d // tn
    bd = b_down.reshape(N_EXPERTS, 1, d)
    ys = pl.pallas_call(
        _down_kernel,
        grid_spec=pltpu.PrefetchScalarGridSpec(
            num_scalar_prefetch=2,
            grid=(nn, n_blocks),
            in_specs=[
                pl.BlockSpec((MOE_ROWS, D_FF), lambda j, r, be, nv: (r, 0)),
                pl.BlockSpec((None, D_FF, tn), lambda j, r, be, nv: (be[r], 0, j)),
                pl.BlockSpec((None, 1, tn), lambda j, r, be, nv: (be[r], 0, j)),
            ],
            out_specs=pl.BlockSpec((MOE_ROWS, tn), lambda j, r, be, nv: (r, j)),
            scratch_shapes=[pltpu.VMEM((D_FF, tn), BF16)],
        ),
        out_shape=jax.ShapeDtypeStruct((n_slots, d), F32),
        compiler_params=_cparams(("arbitrary", "arbitrary"), 48),
        name="moe_down",
    )(block_expert, n_valid, act, w_down, bd)
    return ys


def _combine_kernel(dest_ref, gate_ref, h_ref, g_ref, beta_ref, ys_ref, o_ref, buf_ref, sem, *, alpha):
    n = dest_ref.shape[0]

    def start(t, _):
        pltpu.make_async_copy(ys_ref.at[pl.ds(dest_ref[t], 1)],
                              buf_ref.at[t % TOP_K, pl.ds(t // TOP_K, 1)], sem).start()
        return 0

    def wait(t, _):
        pltpu.make_async_copy(ys_ref.at[pl.ds(dest_ref[t], 1)],
                              buf_ref.at[t % TOP_K, pl.ds(t // TOP_K, 1)], sem).wait()
        return 0

    lax.fori_loop(0, n, start, 0)
    lax.fori_loop(0, n, wait, 0)
    gates = gate_ref[...]
    y = gates[:, 0:1] * buf_ref[0]
    for k in range(1, TOP_K):
        y = y + gates[:, k:k + 1] * buf_ref[k]
    o_ref[...] = _layer_norm(alpha * h_ref[...] + y, g_ref[...], beta_ref[...])


def _combine(ys, dest_flat, gates, h, ln_g, ln_b, alpha):
    s, d = h.shape
    tm = min(128, s)
    return pl.pallas_call(
        functools.partial(_combine_kernel, alpha=alpha),
        grid=(s // tm,),
        in_specs=[
            pl.BlockSpec((tm * TOP_K,), lambda i: (i,), memory_space=pltpu.SMEM),
            pl.BlockSpec((tm, TOP_K), lambda i: (i, 0)),
            pl.BlockSpec((tm, d), lambda i: (i, 0)),
            pl.BlockSpec((1, d), lambda i: (0, 0)),
            pl.BlockSpec((1, d), lambda i: (0, 0)),
            pl.BlockSpec(memory_space=pl.ANY),
        ],
        out_specs=pl.BlockSpec((tm, d), lambda i: (i, 0)),
        out_shape=jax.ShapeDtypeStruct((s, d), F32),
        scratch_shapes=[pltpu.VMEM((TOP_K, tm, d), F32), pltpu.SemaphoreType.DMA(())],
        compiler_params=_cparams(("arbitrary",), 32),
        name="moe_combine",
    )(dest_flat, gates, h, ln_g, ln_b, ys)


def _layer(x2d, w_in, b_forget, w_branch_a, w_branch_b, w_out, ln1_g, ln1_b, w_router, b_router,
           w_gate_up, b_gate_up, w_down, b_down, ln2_g, ln2_b, tables):
    s, d = x2d.shape
    alpha = (2.0 * DEPTH) ** 0.25
    scale = HEAD_DIM ** -0.5

    sizes = (MIX_A, MIX_A, MIX_A, FOX_HEADS, MIX_B, DSA_KV_HEADS * HEAD_DIM, DSA_KV_HEADS * HEAD_DIM,
             IDX_HEADS * IDX_DIM, IDX_DIM, IDX_HEADS, D_MODEL, D_MODEL)
    w_aq, w_ak, w_av, w_af, w_bq, w_bk, w_bv, w_iq, w_ik, w_iw, w_ga, w_gb = jnp.split(
        w_in, np.cumsum(sizes)[:-1].tolist(), axis=1)
    w_bf = jnp.concatenate(
        [w_bq * scale, w_iq, w_aq * scale, w_ak, w_av, w_bk, w_bv, w_ik,
         jnp.zeros((d, N_PBF - COL_IK - IDX_DIM), F32)], axis=1).astype(BF16)
    w_g = jnp.concatenate(
        [w_ga, w_gb, w_af, w_iw, jnp.zeros((d, N_PG - COL_SMALL - FOX_HEADS - IDX_HEADS), F32)],
        axis=1).astype(BF16)

    pbf, pg = _project(x2d, w_bf, tables, w_g)

    fox_tq = min(256, s)
    fox_tk = min(512, s)
    af_t = pg[:, COL_SMALL:COL_SMALL + FOX_HEADS].T
    c_t = _forget_cumsum(af_t, b_forget)
    c_tiles = c_t.reshape(FOX_HEADS, s // fox_tk, 1, fox_tk)
    a_out = _fox_attention(pbf, c_tiles, s, fox_tq, fox_tk)

    b_out = _dsa_attention(pbf, pg, s, 128, min(512, s))

    h, eidx, gates, pos, counts = _merge(
        a_out, b_out, pg, x2d, w_branch_a.astype(BF16), w_branch_b.astype(BF16), w_out.astype(BF16),
        ln1_g.reshape(1, d), ln1_b.reshape(1, d), w_router, b_router.reshape(1, N_EXPERTS), alpha)

    counts = counts.reshape(N_EXPERTS)
    padded = (counts + MOE_ROWS - 1) // MOE_ROWS * MOE_ROWS
    pend = jnp.cumsum(padded)
    pstart = pend - padded
    n_blocks = (s * TOP_K) // MOE_ROWS + N_EXPERTS
    n_slots = n_blocks * MOE_ROWS
    dest = (pstart[eidx] + pos).reshape(s * TOP_K).astype(I32)
    block_expert = jnp.minimum(
        jnp.searchsorted(pend, jnp.arange(n_blocks, dtype=I32) * MOE_ROWS, side='right'),
        N_EXPERTS - 1).astype(I32)
    n_valid = (pend[-1:] // MOE_ROWS).astype(I32)

    xs = _dispatch(h, dest, n_slots)
    ys = _experts(xs, block_expert, n_valid, w_gate_up, b_gate_up, w_down, b_down)
    return _combine(ys, dest, gates, h, ln2_g.reshape(1, d), ln2_b.reshape(1, d), alpha)


def kernel(x, w_in, b_forget, w_branch_a, w_branch_b, w_out, ln1_g, ln1_b, w_router, b_router,
           w_gate_up, b_gate_up, w_down, b_down, ln2_g, ln2_b):
    bsz, s, d = x.shape
    tables = _rope_tables(s)
    outs = []
    for bi in range(bsz):
        xb = x[bi]
        for l in range(DEPTH):
            xb = _layer(xb, w_in[l], b_forget[l], w_branch_a[l], w_branch_b[l], w_out[l], ln1_g[l], ln1_b[l],
                        w_router[l], b_router[l], w_gate_up[l], b_gate_up[l], w_down[l], b_down[l],
                        ln2_g[l], ln2_b[l], tables)
        outs.append(xb)
    return jnp.stack(outs)
```

```python
import functools

import numpy as np
import jax
import jax.numpy as jnp
from jax import lax
from jax.experimental import pallas as pl
from jax.experimental.pallas import tpu as pltpu

F32 = jnp.float32
BF16 = jnp.bfloat16
I32 = jnp.int32

D_MODEL = 2048
DEPTH = 1
CHUNK = 64
HEAD_DIM = 128
FOX_HEADS = 8
DSA_HEADS = 8
DSA_KV_HEADS = 2
DSA_GROUP = DSA_HEADS // DSA_KV_HEADS
IDX_HEADS = 16
IDX_DIM = 64
TOPK_MAX = 256
ROPE_THETA = 500000.0
ROT_FRACTION_DEN = 4
MIX_A = FOX_HEADS * HEAD_DIM
MIX_B = DSA_HEADS * HEAD_DIM
N_EXPERTS = 32
TOP_K = 4
D_FF = D_MODEL
SWIGLU_ALPHA = 1.702
SWIGLU_LIMIT = 7.0
LN_EPS = 1e-5

MIB = 1024 * 1024
NEG_BIG = -1e30
LOG2E = 1.4426950408889634
INT_MIN = -(2 ** 31)

COL_BQ = 0
COL_IQ = 1024
COL_AQ = 2048
COL_AK = 3072
COL_AV = 4096
COL_BK = 5120
COL_BV = 5376
COL_IK = 5632
N_PBF = 5888
PROJ_TN = 256
ROPE_HEAD_TILES = (0, 1, 2, 3, 20)
ROPE_IDX_TILES = (4, 5, 6, 7, 22)
COL_GA = 0
COL_GB = 2048
COL_SMALL = 4096
N_PG = 4224
PG_TN = 384

MOE_ROWS = 256


def _cparams(dims, vmem_mib):
    return pltpu.CompilerParams(dimension_semantics=dims, vmem_limit_bytes=vmem_mib * MIB)


def _tile_in(j, tiles):
    cond = j == tiles[0]
    for t in tiles[1:]:
        cond = cond | (j == t)
    return cond


def _proj_rope_kernel(x_ref, w_ref, tab_ref, o_ref, xb_ref):
    j = pl.program_id(1)

    @pl.when(j == 0)
    def _():
        xb_ref[...] = x_ref[...].astype(BF16)

    acc = jnp.dot(xb_ref[...], w_ref[...], preferred_element_type=F32)
    tn = acc.shape[1]
    is_head = _tile_in(j, ROPE_HEAD_TILES)
    is_idx = _tile_in(j, ROPE_IDX_TILES)

    def rope(shift):
        c = tab_ref[0, 0]
        s_prev = tab_ref[0, 1]
        s_next = tab_ref[0, 2]
        out = acc * c + pltpu.roll(acc, shift, 1) * s_prev + pltpu.roll(acc, tn - shift, 1) * s_next
        o_ref[...] = out.astype(o_ref.dtype)

    @pl.when(is_head)
    def _():
        rope(HEAD_DIM // ROT_FRACTION_DEN // 2)

    @pl.when(is_idx)
    def _():
        rope(IDX_DIM // ROT_FRACTION_DEN // 2)

    @pl.when(jnp.logical_not(is_head | is_idx))
    def _():
        o_ref[...] = acc.astype(o_ref.dtype)


def _proj_plain_kernel(x_ref, w_ref, o_ref, xb_ref):
    j = pl.program_id(1)

    @pl.when(j == 0)
    def _():
        xb_ref[...] = x_ref[...].astype(BF16)

    o_ref[...] = jnp.dot(xb_ref[...], w_ref[...], preferred_element_type=F32).astype(o_ref.dtype)


def _rope_group(j):
    return jnp.where(_tile_in(j, ROPE_IDX_TILES), 1, 0)


def _project(x2d, w_bf, tables, w_g):
    s, d = x2d.shape
    tm = min(1024, s)
    pbf = pl.pallas_call(
        _proj_rope_kernel,
        grid=(s // tm, N_PBF // PROJ_TN),
        in_specs=[
            pl.BlockSpec((tm, d), lambda i, j: (i, 0)),
            pl.BlockSpec((d, PROJ_TN), lambda i, j: (0, j)),
            pl.BlockSpec((1, 3, tm, PROJ_TN), lambda i, j: (_rope_group(j), 0, i, 0)),
        ],
        out_specs=pl.BlockSpec((tm, PROJ_TN), lambda i, j: (i, j)),
        out_shape=jax.ShapeDtypeStruct((s, N_PBF), BF16),
        scratch_shapes=[pltpu.VMEM((tm, d), BF16)],
        compiler_params=_cparams(("arbitrary", "arbitrary"), 48),
        name="proj_bf16",
    )(x2d, w_bf, tables)
    pg = pl.pallas_call(
        _proj_plain_kernel,
        grid=(s // tm, N_PG // PG_TN),
        in_specs=[
            pl.BlockSpec((tm, d), lambda i, j: (i, 0)),
            pl.BlockSpec((d, PG_TN), lambda i, j: (0, j)),
        ],
        out_specs=pl.BlockSpec((tm, PG_TN), lambda i, j: (i, j)),
        out_shape=jax.ShapeDtypeStruct((s, N_PG), F32),
        scratch_shapes=[pltpu.VMEM((tm, d), BF16)],
        compiler_params=_cparams(("arbitrary", "arbitrary"), 48),
        name="proj_f32",
    )(x2d, w_g)
    return pbf, pg


def _rope_tables(s):
    pos = jnp.arange(s, dtype=F32)

    def one(period):
        rot = period // ROT_FRACTION_DEN
        half = rot // 2
        inv = jnp.power(ROPE_THETA, -jnp.arange(0, rot, 2, dtype=F32) / rot)
        ang = pos[:, None] * inv[None, :]
        cos, sin = jnp.cos(ang), jnp.sin(ang)
        zero = jnp.zeros((s, period - rot), F32)
        c = jnp.concatenate([cos, cos, jnp.ones((s, period - rot), F32)], axis=1)
        s_prev = jnp.concatenate([jnp.zeros((s, half), F32), sin, zero], axis=1)
        s_next = jnp.concatenate([-sin, jnp.zeros((s, half), F32), zero], axis=1)
        reps = PROJ_TN // period
        return jnp.stack([jnp.tile(c, (1, reps)), jnp.tile(s_prev, (1, reps)), jnp.tile(s_next, (1, reps))])

    return jnp.stack([one(HEAD_DIM), one(IDX_DIM)])


def _cumsum_kernel(af_ref, bf_ref, c_ref, carry_ref):
    i = pl.program_id(0)

    @pl.when(i == 0)
    def _():
        carry_ref[...] = jnp.zeros_like(carry_ref)

    z = af_ref[...] + bf_ref[...]
    logf = jnp.minimum(z, 0.0) - jnp.log1p(jnp.exp(-jnp.abs(z)))
    t = z.shape[1]
    row = lax.broadcasted_iota(I32, (t, t), 0)
    col = lax.broadcasted_iota(I32, (t, t), 1)
    upper = (row <= col).astype(F32)
    c = jnp.dot(logf, upper, preferred_element_type=F32, precision=lax.Precision.HIGHEST) + carry_ref[...]
    c_ref[...] = c * LOG2E
    carry_ref[...] = c[:, t - 1:t]


def _forget_cumsum(af_t, b_forget):
    h, s = af_t.shape
    t = min(512, s)
    return pl.pallas_call(
        _cumsum_kernel,
        grid=(s // t,),
        in_specs=[pl.BlockSpec((h, t), lambda i: (0, i)), pl.BlockSpec((h, 1), lambda i: (0, 0))],
        out_specs=pl.BlockSpec((h, t), lambda i: (0, i)),
        out_shape=jax.ShapeDtypeStruct((h, s), F32),
        scratch_shapes=[pltpu.VMEM((h, 1), F32)],
        compiler_params=_cparams(("arbitrary",), 32),
        name="forget_cumsum",
    )(af_t, b_forget.reshape(h, 1).astype(F32))


def _softmax_init(m_ref, l_ref, acc_ref):
    m_ref[...] = jnp.full(m_ref.shape, NEG_BIG, F32)
    l_ref[...] = jnp.zeros(l_ref.shape, F32)
    acc_ref[...] = jnp.zeros(acc_ref.shape, F32)


def _softmax_tile(s, v, m_ref, l_ref, acc_ref, i):
    reps = s.shape[1] // 128
    m_old = m_ref[i]
    m_new = jnp.maximum(m_old, jnp.max(s, axis=1, keepdims=True))
    p = jnp.exp2(s - jnp.tile(m_new, (1, reps)))
    alpha = jnp.exp2(m_old - m_new)
    psum = p[:, :128]
    for c in range(1, reps):
        psum = psum + p[:, c * 128:(c + 1) * 128]
    l_ref[i] = alpha * l_ref[i] + psum
    acc_ref[i] = alpha * acc_ref[i] + jnp.dot(p.astype(BF16), v, preferred_element_type=F32)
    m_ref[i] = m_new


def _softmax_result(l_ref, acc_ref, i):
    return acc_ref[i] / jnp.sum(l_ref[i], axis=1, keepdims=True)


def _fox_kernel(q_ref, k_ref, v_ref, c_ref, o_ref, m_ref, l_ref, acc_ref, *, tq, tk, nh):
    qi = pl.program_id(1)
    _softmax_init(m_ref, l_ref, acc_ref)

    def tile(kt, masked):
        start = pl.multiple_of(kt * tk, tk)
        if masked:
            row = qi * tq + lax.broadcasted_iota(I32, (tq, tk), 0)
            col = kt * tk + lax.broadcasted_iota(I32, (tq, tk), 1)
            causal = col <= row
        for h in range(nh):
            hs = slice(h * HEAD_DIM, (h + 1) * HEAD_DIM)
            k = k_ref[pl.ds(start, tk), hs]
            v = v_ref[pl.ds(start, tk), hs]
            s = lax.dot_general(q_ref[:, hs], k, (((1,), (1,)), ((), ())), preferred_element_type=F32)
            s = s - c_ref[h, kt]
            if masked:
                s = jnp.where(causal, s, NEG_BIG)
            _softmax_tile(s, v, m_ref, l_ref, acc_ref, h)

    n_full = (qi * tq) // tk

    def body(kt, _):
        tile(kt, False)
        return 0

    lax.fori_loop(0, n_full, body, 0)
    for t in range((tq + tk - 1) // tk):
        tile(n_full + t, True)
    for h in range(nh):
        o_ref[:, h * HEAD_DIM:(h + 1) * HEAD_DIM] = _softmax_result(l_ref, acc_ref, h).astype(o_ref.dtype)


def _fox_attention(pbf, c_tiles, s, tq, tk, nh):
    cb = nh * HEAD_DIM
    return pl.pallas_call(
        functools.partial(_fox_kernel, tq=tq, tk=tk, nh=nh),
        grid=(FOX_HEADS // nh, s // tq),
        in_specs=[
            pl.BlockSpec((tq, cb), lambda g, i: (i, COL_AQ // cb + g)),
            pl.BlockSpec((s, cb), lambda g, i: (0, COL_AK // cb + g)),
            pl.BlockSpec((s, cb), lambda g, i: (0, COL_AV // cb + g)),
            pl.BlockSpec((nh, s // tk, 1, tk), lambda g, i: (g, 0, 0, 0)),
        ],
        out_specs=pl.BlockSpec((tq, cb), lambda g, i: (i, g)),
        out_shape=jax.ShapeDtypeStruct((s, MIX_A), BF16),
        scratch_shapes=[
            pltpu.VMEM((nh, tq, 128), F32),
            pltpu.VMEM((nh, tq, 128), F32),
            pltpu.VMEM((nh, tq, HEAD_DIM), F32),
        ],
        compiler_params=_cparams(("arbitrary", "arbitrary"), 56),
        name="fox_attention",
    )(pbf, pbf, pbf, c_tiles)


def _dsa_kernel(bq_ref, iq_ref, iw_ref, ik_ref, bk_ref, bv_ref, o_ref,
                key_ref, qg_ref, m_ref, l_ref, acc_ref, *, tq, tk, n_sel):
    b = pl.program_id(0)
    n_tiles = (b * tq) // tk + 1
    row_g = b * tq + lax.broadcasted_iota(I32, (tq, tk), 0)
    adm_end = (row_g // CHUNK + 1) * CHUNK
    col_l = lax.broadcasted_iota(I32, (tq, tk), 1)

    idx_scale = (IDX_HEADS ** -0.5) * (IDX_DIM ** -0.5)
    iw = iw_ref[...][:, 8:8 + IDX_HEADS] * idx_scale
    iq = iq_ref[...]

    def score_tile(kt, _):
        start = pl.multiple_of(kt * tk, tk)
        ik = ik_ref[pl.ds(start, tk), :][:, :IDX_DIM]
        acc = jnp.zeros((tq, tk), F32)
        for h in range(IDX_HEADS):
            a = iq[:, h * IDX_DIM:(h + 1) * IDX_DIM]
            rel = lax.dot_general(a, ik, (((1,), (1,)), ((), ())), preferred_element_type=F32)
            acc = acc + jnp.maximum(rel, 0.0) * iw[:, h:h + 1]
        key_ref[kt] = jnp.where(kt * tk + col_l < adm_end, acc, -jnp.inf)
        return 0

    lax.fori_loop(0, n_tiles, score_tile, 0)

    def float_of(code_u):
        code = code_u ^ INT_MIN
        return pltpu.bitcast(jnp.where(code >= 0, code, code ^ 0x7FFFFFFF), F32)

    def count_ge(cand):
        def body(kt, part):
            ge = jnp.where(key_ref[kt] >= cand, 1, 0)
            for c in range(tk // 128):
                part = part + ge[:, c * 128:(c + 1) * 128]
            return part
        part = lax.fori_loop(0, n_tiles, body, jnp.zeros((tq, 128), I32))
        return jnp.sum(part, axis=1, keepdims=True)

    def search_cond(carry):
        i, _, _, pending = carry
        return (i < 32) & (pending > 0)

    def search_step(carry):
        i, t_u, hit, _ = carry
        cand_u = t_u | lax.shift_left(jnp.int32(1), 31 - i)
        cnt = count_ge(float_of(cand_u))
        t_u = jnp.where(cnt >= n_sel, cand_u, t_u)
        hit = jnp.where(cnt == n_sel, 1, hit)
        return i + 1, t_u, hit, jnp.sum(1 - hit)

    zeros = jnp.zeros((tq, 1), I32)
    _, t_u, _, _ = lax.while_loop(search_cond, search_step, (jnp.int32(0), zeros, zeros, jnp.int32(tq)))
    thr = jnp.where(t_u == 0, jnp.finfo(F32).min, float_of(t_u))

    for j in range(DSA_KV_HEADS):
        for g in range(DSA_GROUP):
            hd = j * DSA_GROUP + g
            qg_ref[j, g * tq:(g + 1) * tq, :] = bq_ref[:, hd * HEAD_DIM:(hd + 1) * HEAD_DIM]
    _softmax_init(m_ref, l_ref, acc_ref)

    def attn_tile(kt, _):
        start = pl.multiple_of(kt * tk, tk)
        sel = key_ref[kt] >= thr
        for j in range(DSA_KV_HEADS):
            hs = slice(j * HEAD_DIM, (j + 1) * HEAD_DIM)
            k = bk_ref[pl.ds(start, tk), hs]
            v = bv_ref[pl.ds(start, tk), hs]
            s = lax.dot_general(qg_ref[j], k, (((1,), (1,)), ((), ())), preferred_element_type=F32)
            s = jnp.where(sel[None], s.reshape(DSA_GROUP, tq, tk), NEG_BIG).reshape(DSA_GROUP * tq, tk)
            _softmax_tile(s, v, m_ref, l_ref, acc_ref, j)
        return 0

    lax.fori_loop(0, n_tiles, attn_tile, 0)

    for j in range(DSA_KV_HEADS):
        o = _softmax_result(l_ref, acc_ref, j)
        for g in range(DSA_GROUP):
            hd = j * DSA_GROUP + g
            o_ref[:, hd * HEAD_DIM:(hd + 1) * HEAD_DIM] = o[g * tq:(g + 1) * tq].astype(o_ref.dtype)


def _dsa_attention(pbf, pg, s, tq, tk):
    n_sel = min(TOPK_MAX, s // 4)
    kvw = DSA_KV_HEADS * HEAD_DIM
    rows = DSA_GROUP * tq
    return pl.pallas_call(
        functools.partial(_dsa_kernel, tq=tq, tk=tk, n_sel=n_sel),
        grid=(s // tq,),
        in_specs=[
            pl.BlockSpec((tq, MIX_B), lambda b: (b, COL_BQ // MIX_B)),
            pl.BlockSpec((tq, IDX_HEADS * IDX_DIM), lambda b: (b, COL_IQ // (IDX_HEADS * IDX_DIM))),
            pl.BlockSpec((tq, 128), lambda b: (b, COL_SMALL // 128)),
            pl.BlockSpec((s, 128), lambda b: (0, COL_IK // 128)),
            pl.BlockSpec((s, kvw), lambda b: (0, COL_BK // kvw)),
            pl.BlockSpec((s, kvw), lambda b: (0, COL_BV // kvw)),
        ],
        out_specs=pl.BlockSpec((tq, MIX_B), lambda b: (b, 0)),
        out_shape=jax.ShapeDtypeStruct((s, MIX_B), BF16),
        scratch_shapes=[
            pltpu.VMEM((s // tk, tq, tk), F32),
            pltpu.VMEM((DSA_KV_HEADS, rows, HEAD_DIM), BF16),
            pltpu.VMEM((DSA_KV_HEADS, rows, 128), F32),
            pltpu.VMEM((DSA_KV_HEADS, rows, 128), F32),
            pltpu.VMEM((DSA_KV_HEADS, rows, HEAD_DIM), F32),
        ],
        compiler_params=_cparams(("arbitrary",), 56),
        name="dsa_attention",
    )(pbf, pbf, pg, pbf, pbf, pbf)


def _layer_norm(z, g, b):
    mu = jnp.mean(z, axis=-1, keepdims=True)
    zc = z - mu
    var = jnp.mean(zc * zc, axis=-1, keepdims=True)
    return zc * lax.rsqrt(var + LN_EPS) * g + b


def _merge_kernel(a_ref, b_ref, ga_ref, gb_ref, x_ref, wa_ref, wb_ref, wo_ref, g_ref, beta_ref,
                  wr_ref, br_ref, h_ref, eidx_ref, gate_ref, pos_ref, cnt_ref, carry_ref, *, alpha):
    i = pl.program_id(0)

    @pl.when(i == 0)
    def _():
        carry_ref[...] = jnp.zeros_like(carry_ref)

    ma = jnp.dot(a_ref[...], wa_ref[...], preferred_element_type=F32)
    mb = jnp.dot(b_ref[...], wb_ref[...], preferred_element_type=F32)
    merged = jax.nn.sigmoid(ga_ref[...]) * ma + jax.nn.sigmoid(gb_ref[...]) * mb
    y = jnp.dot(merged.astype(BF16), wo_ref[...], preferred_element_type=F32)
    h = _layer_norm(alpha * x_ref[...] + y, g_ref[...], beta_ref[...])
    h_ref[...] = h

    logits = jnp.dot(h, wr_ref[...], preferred_element_type=F32, precision=lax.Precision.HIGHEST) + br_ref[...]
    tm, ne = logits.shape
    lane = lax.broadcasted_iota(I32, (tm, ne), 1)
    lane_k = lax.broadcasted_iota(I32, (tm, TOP_K), 1)
    work = logits
    vals, sels = [], []
    eidx = jnp.zeros((tm, TOP_K), I32)
    onehot = jnp.zeros((tm, ne), F32)
    for k in range(TOP_K):
        mv = jnp.max(work, axis=1, keepdims=True)
        idx = jnp.min(jnp.where(work == mv, lane, ne), axis=1, keepdims=True)
        sel = lane == idx
        vals.append(mv)
        sels.append(sel)
        eidx = jnp.where(lane_k == k, idx, eidx)
        onehot = onehot + jnp.where(sel, 1.0, 0.0)
        work = jnp.where(sel, -jnp.inf, work)
    exps = [jnp.exp(v - vals[0]) for v in vals]
    denom = exps[0] + exps[1] + exps[2] + exps[3]
    gates = jnp.zeros((tm, TOP_K), F32)
    for k in range(TOP_K):
        gates = jnp.where(lane_k == k, exps[k] / denom, gates)

    r_i = lax.broadcasted_iota(I32, (tm, tm), 0)
    c_i = lax.broadcasted_iota(I32, (tm, tm), 1)
    lower = jnp.where(c_i < r_i, 1.0, 0.0).astype(BF16)
    rank = carry_ref[...] + jnp.dot(lower, onehot.astype(BF16), preferred_element_type=F32)
    pos = jnp.zeros((tm, TOP_K), I32)
    for k in range(TOP_K):
        pk = jnp.sum(jnp.where(sels[k], rank, 0.0), axis=1, keepdims=True).astype(I32)
        pos = jnp.where(lane_k == k, pk, pos)
    carry_ref[...] = carry_ref[...] + jnp.sum(onehot, axis=0, keepdims=True)

    eidx_ref[...] = eidx
    gate_ref[...] = gates
    pos_ref[...] = pos
    cnt_ref[...] = carry_ref[...].astype(I32)


def _merge(a_out, b_out, pg, x2d, wa, wb, wo, ln_g, ln_b, w_router, b_router, alpha):
    s, d = x2d.shape
    tm = min(256, s)
    full = lambda shape: pl.BlockSpec(shape, lambda i: (0,) * len(shape))
    return pl.pallas_call(
        functools.partial(_merge_kernel, alpha=alpha),
        grid=(s // tm,),
        in_specs=[
            pl.BlockSpec((tm, MIX_A), lambda i: (i, 0)),
            pl.BlockSpec((tm, MIX_B), lambda i: (i, 0)),
            pl.BlockSpec((tm, d), lambda i: (i, COL_GA // d)),
            pl.BlockSpec((tm, d), lambda i: (i, COL_GB // d)),
            pl.BlockSpec((tm, d), lambda i: (i, 0)),
            full((MIX_A, d)), full((MIX_B, d)), full((d, d)),
            full((1, d)), full((1, d)), full((d, N_EXPERTS)), full((1, N_EXPERTS)),
        ],
        out_specs=[
            pl.BlockSpec((tm, d), lambda i: (i, 0)),
            pl.BlockSpec((tm, TOP_K), lambda i: (i, 0)),
            pl.BlockSpec((tm, TOP_K), lambda i: (i, 0)),
            pl.BlockSpec((tm, TOP_K), lambda i: (i, 0)),
            full((1, N_EXPERTS)),
        ],
        out_shape=[
            jax.ShapeDtypeStruct((s, d), F32),
            jax.ShapeDtypeStruct((s, TOP_K), I32),
            jax.ShapeDtypeStruct((s, TOP_K), F32),
            jax.ShapeDtypeStruct((s, TOP_K), I32),
            jax.ShapeDtypeStruct((1, N_EXPERTS), I32),
        ],
        scratch_shapes=[pltpu.VMEM((1, N_EXPERTS), F32)],
        compiler_params=_cparams(("arbitrary",), 56),
        name="merge_ln_router",
    )(a_out, b_out, pg, pg, x2d, wa, wb, wo, ln_g, ln_b, w_router, b_router)


def _dispatch_kernel(dest_ref, h_ref, xs_in_ref, xs_ref, stage_ref, sem):
    del xs_in_ref
    i = pl.program_id(0)
    last = pl.num_programs(0) - 1
    slot = lax.rem(i, 2)
    n = dest_ref.shape[0]

    def drain(sl):
        pltpu.make_async_copy(xs_ref.at[pl.ds(0, n)], xs_ref.at[pl.ds(0, n)], sem.at[sl]).wait()

    @pl.when(i >= 2)
    def _():
        drain(slot)

    stage_ref[slot] = h_ref[...]

    def start(t, _):
        pltpu.make_async_copy(stage_ref.at[slot, pl.ds(t // TOP_K, 1)],
                              xs_ref.at[pl.ds(dest_ref[t], 1)], sem.at[slot]).start()
        return 0

    lax.fori_loop(0, n, start, 0, unroll=8)

    @pl.when(i == last)
    def _():
        drain(slot)

        @pl.when(i >= 1)
        def _():
            drain(1 - slot)


def _dispatch(h, dest_flat, n_slots):
    s, d = h.shape
    tm = min(128, s)
    xs0 = jnp.zeros((n_slots, d), h.dtype)
    return pl.pallas_call(
        _dispatch_kernel,
        grid=(s // tm,),
        in_specs=[
            pl.BlockSpec((tm * TOP_K,), lambda i: (i,), memory_space=pltpu.SMEM),
            pl.BlockSpec((tm, d), lambda i: (i, 0)),
            pl.BlockSpec(memory_space=pl.ANY),
        ],
        out_specs=pl.BlockSpec(memory_space=pl.ANY),
        out_shape=jax.ShapeDtypeStruct((n_slots, d), h.dtype),
        scratch_shapes=[pltpu.VMEM((2, tm, d), h.dtype), pltpu.SemaphoreType.DMA((2,))],
        input_output_aliases={2: 0},
        compiler_params=_cparams(("arbitrary",), 32),
        name="moe_dispatch",
    )(dest_flat, h, xs0)


def _gate_up_kernel(be_ref, nv_ref, x_ref, wg_ref, wu_ref, bg_ref, bu_ref, act_ref, wg_sc, wu_sc):
    r = pl.program_id(1)
    prev = be_ref[jnp.maximum(r - 1, 0)]
    fresh = (r == 0) | (be_ref[r] != prev)

    @pl.when(fresh)
    def _():
        wg_sc[...] = wg_ref[...].astype(BF16)
        wu_sc[...] = wu_ref[...].astype(BF16)

    @pl.when(r < nv_ref[0])
    def _():
        x = x_ref[...].astype(BF16)
        cw = 256
        for c in range(act_ref.shape[1] // cw):
            cs = slice(c * cw, (c + 1) * cw)
            g = jnp.dot(x, wg_sc[:, cs], preferred_element_type=F32) + bg_ref[:, cs]
            u = jnp.dot(x, wu_sc[:, cs], preferred_element_type=F32) + bu_ref[:, cs]
            g = jnp.minimum(g, SWIGLU_LIMIT)
            u = jnp.clip(u, -SWIGLU_LIMIT, SWIGLU_LIMIT)
            act_ref[:, cs] = (g * jax.nn.sigmoid(SWIGLU_ALPHA * g) * (u + 1.0)).astype(act_ref.dtype)

    @pl.when(r >= nv_ref[0])
    def _():
        act_ref[...] = jnp.zeros_like(act_ref)


def _down_kernel(be_ref, nv_ref, a_ref, wd_ref, bd_ref, y_ref, wd_sc):
    r = pl.program_id(1)
    prev = be_ref[jnp.maximum(r - 1, 0)]
    fresh = (r == 0) | (be_ref[r] != prev)

    @pl.when(fresh)
    def _():
        wd_sc[...] = wd_ref[...].astype(BF16)

    @pl.when(r < nv_ref[0])
    def _():
        y_ref[...] = jnp.dot(a_ref[...], wd_sc[...], preferred_element_type=F32) + bd_ref[...]

    @pl.when(r >= nv_ref[0])
    def _():
        y_ref[...] = jnp.zeros_like(y_ref)


def _experts(xs, block_expert, n_valid, w_gate_up, b_gate_up, w_down, b_down):
    n_slots, d = xs.shape
    n_blocks = n_slots // MOE_ROWS
    tf = 1024
    nf = D_FF // tf
    bgu = b_gate_up.reshape(N_EXPERTS, 1, 2 * D_FF)
    act = pl.pallas_call(
        _gate_up_kernel,
        grid_spec=pltpu.PrefetchScalarGridSpec(
            num_scalar_prefetch=2,
            grid=(nf, n_blocks),
            in_specs=[
                pl.BlockSpec((MOE_ROWS, d), lambda j, r, be, nv: (r, 0)),
                pl.BlockSpec((None, d, tf), lambda j, r, be, nv: (be[r], 0, j)),
                pl.BlockSpec((None, d, tf), lambda j, r, be, nv: (be[r], 0, nf + j)),
                pl.BlockSpec((None, 1, tf), lambda j, r, be, nv: (be[r], 0, j)),
                pl.BlockSpec((None, 1, tf), lambda j, r, be, nv: (be[r], 0, nf + j)),
            ],
            out_specs=pl.BlockSpec((MOE_ROWS, tf), lambda j, r, be, nv: (r, j)),
            scratch_shapes=[pltpu.VMEM((d, tf), BF16), pltpu.VMEM((d, tf), BF16)],
        ),
        out_shape=jax.ShapeDtypeStruct((n_slots, D_FF), BF16),
        compiler_params=_cparams(("arbitrary", "arbitrary"), 56),
        name="moe_gate_up",
    )(block_expert, n_valid, xs, w_gate_up, w_gate_up, bgu, bgu)
    tn = 2048
    nn = d // tn
    bd = b_down.reshape(N_EXPERTS, 1, d)
    ys = pl.pallas_call(
        _down_kernel,
        grid_spec=pltpu.PrefetchScalarGridSpec(
            num_scalar_prefetch=2,
            grid=(nn, n_blocks),
            in_specs=[
                pl.BlockSpec((MOE_ROWS, D_FF), lambda j, r, be, nv: (r, 0)),
                pl.BlockSpec((None, D_FF, tn), lambda j, r, be, nv: (be[r], 0, j)),
                pl.BlockSpec((None, 1, tn), lambda j, r, be, nv: (be[r], 0, j)),
            ],
            out_specs=pl.BlockSpec((MOE_ROWS, tn), lambda j, r, be, nv: (r, j)),
            scratch_shapes=[pltpu.VMEM((D_FF, tn), BF16)],
        ),
        out_shape=jax.ShapeDtypeStruct((n_slots, d), F32),
        compiler_params=_cparams(("arbitrary", "arbitrary"), 56),
        name="moe_down",
    )(block_expert, n_valid, act, w_down, bd)
    return ys


def _combine_kernel(dest_ref, dest_next_ref, gate_ref, h_ref, g_ref, beta_ref, ys_ref, o_ref, buf_ref, sem,
                    *, alpha):
    i = pl.program_id(0)
    last = pl.num_programs(0) - 1
    slot = lax.rem(i, 2)
    n = dest_ref.shape[0]

    def gather(dref, sl):
        def start(t, _):
            pltpu.make_async_copy(ys_ref.at[pl.ds(dref[t], 1)],
                                  buf_ref.at[sl, lax.rem(t, TOP_K), pl.ds(t // TOP_K, 1)], sem.at[sl]).start()
            return 0
        lax.fori_loop(0, n, start, 0, unroll=8)

    @pl.when(i == 0)
    def _():
        gather(dest_ref, slot)

    @pl.when(i < last)
    def _():
        gather(dest_next_ref, 1 - slot)

    pltpu.make_async_copy(buf_ref.at[slot], buf_ref.at[slot], sem.at[slot]).wait()
    gates = gate_ref[...]
    y = gates[:, 0:1] * buf_ref[slot, 0]
    for k in range(1, TOP_K):
        y = y + gates[:, k:k + 1] * buf_ref[slot, k]
    o_ref[...] = _layer_norm(alpha * h_ref[...] + y, g_ref[...], beta_ref[...])


def _combine(ys, dest_flat, gates, h, ln_g, ln_b, alpha):
    s, d = h.shape
    tm = min(128, s)
    n_steps = s // tm
    return pl.pallas_call(
        functools.partial(_combine_kernel, alpha=alpha),
        grid=(n_steps,),
        in_specs=[
            pl.BlockSpec((tm * TOP_K,), lambda i: (i,), memory_space=pltpu.SMEM),
            pl.BlockSpec((tm * TOP_K,), lambda i: (jnp.minimum(i + 1, n_steps - 1),), memory_space=pltpu.SMEM),
            pl.BlockSpec((tm, TOP_K), lambda i: (i, 0)),
            pl.BlockSpec((tm, d), lambda i: (i, 0)),
            pl.BlockSpec((1, d), lambda i: (0, 0)),
            pl.BlockSpec((1, d), lambda i: (0, 0)),
            pl.BlockSpec(memory_space=pl.ANY),
        ],
        out_specs=pl.BlockSpec((tm, d), lambda i: (i, 0)),
        out_shape=jax.ShapeDtypeStruct((s, d), F32),
        scratch_shapes=[pltpu.VMEM((2, TOP_K, tm, d), F32), pltpu.SemaphoreType.DMA((2,))],
        compiler_params=_cparams(("arbitrary",), 32),
        name="moe_combine",
    )(dest_flat, dest_flat, gates, h, ln_g, ln_b, ys)


def _layer(x2d, w_in, b_forget, w_branch_a, w_branch_b, w_out, ln1_g, ln1_b, w_router, b_router,
           w_gate_up, b_gate_up, w_down, b_down, ln2_g, ln2_b, tables):
    s, d = x2d.shape
    alpha = (2.0 * DEPTH) ** 0.25
    scale = HEAD_DIM ** -0.5 * LOG2E

    sizes = (MIX_A, MIX_A, MIX_A, FOX_HEADS, MIX_B, DSA_KV_HEADS * HEAD_DIM, DSA_KV_HEADS * HEAD_DIM,
             IDX_HEADS * IDX_DIM, IDX_DIM, IDX_HEADS, D_MODEL, D_MODEL)
    w_aq, w_ak, w_av, w_af, w_bq, w_bk, w_bv, w_iq, w_ik, w_iw, w_ga, w_gb = jnp.split(
        w_in, np.cumsum(sizes)[:-1].tolist(), axis=1)
    w_bf = jnp.concatenate(
        [w_bq * scale, w_iq, w_aq * scale, w_ak, w_av, w_bk, w_bv, w_ik,
         jnp.zeros((d, N_PBF - COL_IK - IDX_DIM), F32)], axis=1).astype(BF16)
    w_g = jnp.concatenate(
        [w_ga, w_gb, w_af, w_iw, jnp.zeros((d, N_PG - COL_SMALL - FOX_HEADS - IDX_HEADS), F32)],
        axis=1).astype(BF16)

    pbf, pg = _project(x2d, w_bf, tables, w_g)

    fox_tq = min(512, s)
    fox_tk = min(512, s)
    af_t = pg[:, COL_SMALL:COL_SMALL + FOX_HEADS].T
    c_t = _forget_cumsum(af_t, b_forget)
    c_tiles = c_t.reshape(FOX_HEADS, s // fox_tk, 1, fox_tk)
    a_out = _fox_attention(pbf, c_tiles, s, fox_tq, fox_tk, 4)

    b_out = _dsa_attention(pbf, pg, s, 128, min(512, s))

    h, eidx, gates, pos, counts = _merge(
        a_out, b_out, pg, x2d, w_branch_a.astype(BF16), w_branch_b.astype(BF16), w_out.astype(BF16),
        ln1_g.reshape(1, d), ln1_b.reshape(1, d), w_router, b_router.reshape(1, N_EXPERTS), alpha)

    counts = counts.reshape(N_EXPERTS)
    padded = (counts + MOE_ROWS - 1) // MOE_ROWS * MOE_ROWS
    pend = jnp.cumsum(padded)
    pstart = pend - padded
    n_blocks = (s * TOP_K) // MOE_ROWS + N_EXPERTS
    n_slots = n_blocks * MOE_ROWS
    experts = jnp.arange(N_EXPERTS, dtype=I32)
    start_of = jnp.sum(jnp.where(eidx[..., None] == experts, pstart.astype(I32), 0), axis=-1)
    dest = (start_of + pos).reshape(s * TOP_K).astype(I32)
    block_start = jnp.arange(n_blocks, dtype=I32) * MOE_ROWS
    block_expert = jnp.minimum(
        jnp.sum((pend[None, :] <= block_start[:, None]).astype(I32), axis=1), N_EXPERTS - 1).astype(I32)
    n_valid = (pend[-1:] // MOE_ROWS).astype(I32)

    xs = _dispatch(h, dest, n_slots)
    ys = _experts(xs, block_expert, n_valid, w_gate_up, b_gate_up, w_down, b_down)
    return _combine(ys, dest, gates, h, ln2_g.reshape(1, d), ln2_b.reshape(1, d), alpha)


def kernel(x, w_in, b_forget, w_branch_a, w_branch_b, w_out, ln1_g, ln1_b, w_router, b_router,
           w_gate_up, b_gate_up, w_down, b_down, ln2_g, ln2_b):
    bsz, s, d = x.shape
    tables = _rope_tables(s)
    outs = []
    for bi in range(bsz):
        xb = x[bi]
        for l in range(DEPTH):
            xb = _layer(xb, w_in[l], b_forget[l], w_branch_a[l], w_branch_b[l], w_out[l], ln1_g[l], ln1_b[l],
                        w_router[l], b_router[l], w_gate_up[l], b_gate_up[l], w_down[l], b_down[l],
                        ln2_g[l], ln2_b[l], tables)
        outs.append(xb)
    return jnp.stack(outs)
```

```python
import functools

import numpy as np
import jax
import jax.numpy as jnp
from jax import lax
from jax.experimental import pallas as pl
from jax.experimental.pallas import tpu as pltpu

F32 = jnp.float32
BF16 = jnp.bfloat16
I32 = jnp.int32

D_MODEL = 2048
DEPTH = 1
CHUNK = 64
HEAD_DIM = 128
FOX_HEADS = 8
DSA_HEADS = 8
DSA_KV_HEADS = 2
DSA_GROUP = DSA_HEADS // DSA_KV_HEADS
IDX_HEADS = 16
IDX_DIM = 64
TOPK_MAX = 256
ROPE_THETA = 500000.0
ROT_FRACTION_DEN = 4
MIX_A = FOX_HEADS * HEAD_DIM
MIX_B = DSA_HEADS * HEAD_DIM
N_EXPERTS = 32
TOP_K = 4
D_FF = D_MODEL
SWIGLU_ALPHA = 1.702
SWIGLU_LIMIT = 7.0
LN_EPS = 1e-5

MIB = 1024 * 1024
NEG_BIG = -1e30
LOG2E = 1.4426950408889634
INT_MIN = -(2 ** 31)

COL_BQ = 0
COL_IQ = 1024
COL_AQ = 2048
COL_AK = 3072
COL_AV = 4096
COL_BK = 5120
COL_BV = 5376
COL_IK = 5632
N_PBF = 5888
PROJ_TN = 256
ROPE_HEAD_TILES = (0, 1, 2, 3, 20)
ROPE_IDX_TILES = (4, 5, 6, 7, 22)
COL_GA = 0
COL_GB = 2048
COL_SMALL = 4096
N_PG = 4224
PG_TN = 384

MOE_CHUNK = 256
MOE_ALIGN = 16


def _cparams(dims, vmem_mib):
    return pltpu.CompilerParams(dimension_semantics=dims, vmem_limit_bytes=vmem_mib * MIB)


def _tile_in(j, tiles):
    cond = j == tiles[0]
    for t in tiles[1:]:
        cond = cond | (j == t)
    return cond


def _proj_rope_kernel(x_ref, w_ref, tab_ref, o_ref, xb_ref):
    j = pl.program_id(1)

    @pl.when(j == 0)
    def _():
        xb_ref[...] = x_ref[...].astype(BF16)

    acc = jnp.dot(xb_ref[...], w_ref[...], preferred_element_type=F32)
    tn = acc.shape[1]
    is_head = _tile_in(j, ROPE_HEAD_TILES)
    is_idx = _tile_in(j, ROPE_IDX_TILES)

    def rope(shift):
        c = tab_ref[0, 0]
        s_prev = tab_ref[0, 1]
        s_next = tab_ref[0, 2]
        out = acc * c + pltpu.roll(acc, shift, 1) * s_prev + pltpu.roll(acc, tn - shift, 1) * s_next
        o_ref[...] = out.astype(o_ref.dtype)

    @pl.when(is_head)
    def _():
        rope(HEAD_DIM // ROT_FRACTION_DEN // 2)

    @pl.when(is_idx)
    def _():
        rope(IDX_DIM // ROT_FRACTION_DEN // 2)

    @pl.when(jnp.logical_not(is_head | is_idx))
    def _():
        o_ref[...] = acc.astype(o_ref.dtype)


def _proj_plain_kernel(x_ref, w_ref, o_ref, xb_ref):
    j = pl.program_id(1)

    @pl.when(j == 0)
    def _():
        xb_ref[...] = x_ref[...].astype(BF16)

    o_ref[...] = jnp.dot(xb_ref[...], w_ref[...], preferred_element_type=F32).astype(o_ref.dtype)


def _rope_group(j):
    return jnp.where(_tile_in(j, ROPE_IDX_TILES), 1, 0)


def _project(x2d, w_bf, tables, w_g):
    s, d = x2d.shape
    tm = min(1024, s)
    pbf = pl.pallas_call(
        _proj_rope_kernel,
        grid=(s // tm, N_PBF // PROJ_TN),
        in_specs=[
            pl.BlockSpec((tm, d), lambda i, j: (i, 0)),
            pl.BlockSpec((d, PROJ_TN), lambda i, j: (0, j)),
            pl.BlockSpec((1, 3, tm, PROJ_TN), lambda i, j: (_rope_group(j), 0, i, 0)),
        ],
        out_specs=pl.BlockSpec((tm, PROJ_TN), lambda i, j: (i, j)),
        out_shape=jax.ShapeDtypeStruct((s, N_PBF), BF16),
        scratch_shapes=[pltpu.VMEM((tm, d), BF16)],
        compiler_params=_cparams(("arbitrary", "arbitrary"), 48),
        name="proj_bf16",
    )(x2d, w_bf, tables)
    pg = pl.pallas_call(
        _proj_plain_kernel,
        grid=(s // tm, N_PG // PG_TN),
        in_specs=[
            pl.BlockSpec((tm, d), lambda i, j: (i, 0)),
            pl.BlockSpec((d, PG_TN), lambda i, j: (0, j)),
        ],
        out_specs=pl.BlockSpec((tm, PG_TN), lambda i, j: (i, j)),
        out_shape=jax.ShapeDtypeStruct((s, N_PG), F32),
        scratch_shapes=[pltpu.VMEM((tm, d), BF16)],
        compiler_params=_cparams(("arbitrary", "arbitrary"), 48),
        name="proj_f32",
    )(x2d, w_g)
    return pbf, pg


def _rope_tables(s):
    pos = jnp.arange(s, dtype=F32)

    def one(period):
        rot = period // ROT_FRACTION_DEN
        half = rot // 2
        inv = jnp.power(ROPE_THETA, -jnp.arange(0, rot, 2, dtype=F32) / rot)
        ang = pos[:, None] * inv[None, :]
        cos, sin = jnp.cos(ang), jnp.sin(ang)
        zero = jnp.zeros((s, period - rot), F32)
        c = jnp.concatenate([cos, cos, jnp.ones((s, period - rot), F32)], axis=1)
        s_prev = jnp.concatenate([jnp.zeros((s, half), F32), sin, zero], axis=1)
        s_next = jnp.concatenate([-sin, jnp.zeros((s, half), F32), zero], axis=1)
        reps = PROJ_TN // period
        return jnp.stack([jnp.tile(c, (1, reps)), jnp.tile(s_prev, (1, reps)), jnp.tile(s_next, (1, reps))])

    return jnp.stack([one(HEAD_DIM), one(IDX_DIM)])


def _cumsum_kernel(af_ref, bf_ref, c_ref, carry_ref):
    i = pl.program_id(0)

    @pl.when(i == 0)
    def _():
        carry_ref[...] = jnp.zeros_like(carry_ref)

    z = af_ref[...] + bf_ref[...]
    logf = jnp.minimum(z, 0.0) - jnp.log1p(jnp.exp(-jnp.abs(z)))
    t = z.shape[1]
    row = lax.broadcasted_iota(I32, (t, t), 0)
    col = lax.broadcasted_iota(I32, (t, t), 1)
    upper = (row <= col).astype(F32)
    c = jnp.dot(logf, upper, preferred_element_type=F32, precision=lax.Precision.HIGHEST) + carry_ref[...]
    c_ref[...] = c * LOG2E
    carry_ref[...] = c[:, t - 1:t]


def _forget_cumsum(af_t, b_forget):
    h, s = af_t.shape
    t = min(512, s)
    return pl.pallas_call(
        _cumsum_kernel,
        grid=(s // t,),
        in_specs=[pl.BlockSpec((h, t), lambda i: (0, i)), pl.BlockSpec((h, 1), lambda i: (0, 0))],
        out_specs=pl.BlockSpec((h, t), lambda i: (0, i)),
        out_shape=jax.ShapeDtypeStruct((h, s), F32),
        scratch_shapes=[pltpu.VMEM((h, 1), F32)],
        compiler_params=_cparams(("arbitrary",), 32),
        name="forget_cumsum",
    )(af_t, b_forget.reshape(h, 1).astype(F32))


def _softmax_init(m_ref, l_ref, acc_ref):
    m_ref[...] = jnp.full(m_ref.shape, NEG_BIG, F32)
    l_ref[...] = jnp.zeros(l_ref.shape, F32)
    acc_ref[...] = jnp.zeros(acc_ref.shape, F32)


def _softmax_tile(s, v, m_ref, l_ref, acc_ref, i):
    reps = s.shape[1] // 128
    m_old = m_ref[i]
    m_new = jnp.maximum(m_old, jnp.max(s, axis=1, keepdims=True))
    p = jnp.exp2(s - jnp.tile(m_new, (1, reps)))
    alpha = jnp.exp2(m_old - m_new)
    psum = p[:, :128]
    for c in range(1, reps):
        psum = psum + p[:, c * 128:(c + 1) * 128]
    l_ref[i] = alpha * l_ref[i] + psum
    acc_ref[i] = alpha * acc_ref[i] + jnp.dot(p.astype(BF16), v, preferred_element_type=F32)
    m_ref[i] = m_new


def _softmax_result(l_ref, acc_ref, i):
    return acc_ref[i] / jnp.sum(l_ref[i], axis=1, keepdims=True)


def _fox_kernel(q_ref, k_ref, v_ref, c_ref, o_ref, m_ref, l_ref, acc_ref, *, tq, tk, nh):
    qi = pl.program_id(1)
    _softmax_init(m_ref, l_ref, acc_ref)

    def tile(kt, masked):
        start = pl.multiple_of(kt * tk, tk)
        if masked:
            row = qi * tq + lax.broadcasted_iota(I32, (tq, tk), 0)
            col = kt * tk + lax.broadcasted_iota(I32, (tq, tk), 1)
            causal = col <= row
        for h in range(nh):
            hs = slice(h * HEAD_DIM, (h + 1) * HEAD_DIM)
            k = k_ref[pl.ds(start, tk), hs]
            v = v_ref[pl.ds(start, tk), hs]
            s = lax.dot_general(q_ref[:, hs], k, (((1,), (1,)), ((), ())), preferred_element_type=F32)
            s = s - c_ref[h, kt]
            if masked:
                s = jnp.where(causal, s, NEG_BIG)
            _softmax_tile(s, v, m_ref, l_ref, acc_ref, h)

    n_full = (qi * tq) // tk

    def body(kt, _):
        tile(kt, False)
        return 0

    lax.fori_loop(0, n_full, body, 0)
    for t in range((tq + tk - 1) // tk):
        tile(n_full + t, True)
    for h in range(nh):
        o_ref[:, h * HEAD_DIM:(h + 1) * HEAD_DIM] = _softmax_result(l_ref, acc_ref, h).astype(o_ref.dtype)


def _fox_attention(pbf, c_tiles, s, tq, tk, nh):
    cb = nh * HEAD_DIM
    return pl.pallas_call(
        functools.partial(_fox_kernel, tq=tq, tk=tk, nh=nh),
        grid=(FOX_HEADS // nh, s // tq),
        in_specs=[
            pl.BlockSpec((tq, cb), lambda g, i: (i, COL_AQ // cb + g)),
            pl.BlockSpec((s, cb), lambda g, i: (0, COL_AK // cb + g)),
            pl.BlockSpec((s, cb), lambda g, i: (0, COL_AV // cb + g)),
            pl.BlockSpec((nh, s // tk, 1, tk), lambda g, i: (g, 0, 0, 0)),
        ],
        out_specs=pl.BlockSpec((tq, cb), lambda g, i: (i, g)),
        out_shape=jax.ShapeDtypeStruct((s, MIX_A), BF16),
        scratch_shapes=[
            pltpu.VMEM((nh, tq, 128), F32),
            pltpu.VMEM((nh, tq, 128), F32),
            pltpu.VMEM((nh, tq, HEAD_DIM), F32),
        ],
        compiler_params=_cparams(("arbitrary", "arbitrary"), 56),
        name="fox_attention",
    )(pbf, pbf, pbf, c_tiles)


def _dsa_kernel(bq_ref, iq_ref, iw_ref, ik_ref, bk_ref, bv_ref, o_ref,
                key_ref, qg_ref, m_ref, l_ref, acc_ref, *, tq, tk, n_sel):
    b = pl.program_id(0)
    n_tiles = (b * tq) // tk + 1
    row_g = b * tq + lax.broadcasted_iota(I32, (tq, tk), 0)
    adm_end = (row_g // CHUNK + 1) * CHUNK
    col_l = lax.broadcasted_iota(I32, (tq, tk), 1)

    idx_scale = (IDX_HEADS ** -0.5) * (IDX_DIM ** -0.5)
    iw = iw_ref[...][:, 8:8 + IDX_HEADS] * idx_scale
    iq = iq_ref[...]

    def score_tile(kt, _):
        start = pl.multiple_of(kt * tk, tk)
        ik = ik_ref[pl.ds(start, tk), :][:, :IDX_DIM]
        acc = jnp.zeros((tq, tk), F32)
        for h in range(IDX_HEADS):
            a = iq[:, h * IDX_DIM:(h + 1) * IDX_DIM]
            rel = lax.dot_general(a, ik, (((1,), (1,)), ((), ())), preferred_element_type=F32)
            acc = acc + jnp.maximum(rel, 0.0) * iw[:, h:h + 1]
        key_ref[kt] = jnp.where(kt * tk + col_l < adm_end, acc, -jnp.inf)
        return 0

    lax.fori_loop(0, n_tiles, score_tile, 0)

    def float_of(code_u):
        code = code_u ^ INT_MIN
        return pltpu.bitcast(jnp.where(code >= 0, code, code ^ 0x7FFFFFFF), F32)

    def count_ge(cand):
        def body(kt, part):
            ge = jnp.where(key_ref[kt] >= cand, 1, 0)
            for c in range(tk // 128):
                part = part + ge[:, c * 128:(c + 1) * 128]
            return part
        part = lax.fori_loop(0, n_tiles, body, jnp.zeros((tq, 128), I32))
        return jnp.sum(part, axis=1, keepdims=True)

    def search_cond(carry):
        i, _, _, pending = carry
        return (i < 32) & (pending > 0)

    def search_step(carry):
        i, t_u, hit, _ = carry
        cand_u = t_u | lax.shift_left(jnp.int32(1), 31 - i)
        cnt = count_ge(float_of(cand_u))
        t_u = jnp.where(cnt >= n_sel, cand_u, t_u)
        hit = jnp.where(cnt == n_sel, 1, hit)
        return i + 1, t_u, hit, jnp.sum(1 - hit)

    zeros = jnp.zeros((tq, 1), I32)
    _, t_u, _, _ = lax.while_loop(search_cond, search_step, (jnp.int32(0), zeros, zeros, jnp.int32(tq)))
    thr = jnp.where(t_u == 0, jnp.finfo(F32).min, float_of(t_u))

    for j in range(DSA_KV_HEADS):
        for g in range(DSA_GROUP):
            hd = j * DSA_GROUP + g
            qg_ref[j, g * tq:(g + 1) * tq, :] = bq_ref[:, hd * HEAD_DIM:(hd + 1) * HEAD_DIM]
    _softmax_init(m_ref, l_ref, acc_ref)

    def attn_tile(kt, _):
        start = pl.multiple_of(kt * tk, tk)
        sel = key_ref[kt] >= thr
        for j in range(DSA_KV_HEADS):
            hs = slice(j * HEAD_DIM, (j + 1) * HEAD_DIM)
            k = bk_ref[pl.ds(start, tk), hs]
            v = bv_ref[pl.ds(start, tk), hs]
            s = lax.dot_general(qg_ref[j], k, (((1,), (1,)), ((), ())), preferred_element_type=F32)
            s = jnp.where(sel[None], s.reshape(DSA_GROUP, tq, tk), NEG_BIG).reshape(DSA_GROUP * tq, tk)
            _softmax_tile(s, v, m_ref, l_ref, acc_ref, j)
        return 0

    lax.fori_loop(0, n_tiles, attn_tile, 0)

    for j in range(DSA_KV_HEADS):
        o = _softmax_result(l_ref, acc_ref, j)
        for g in range(DSA_GROUP):
            hd = j * DSA_GROUP + g
            o_ref[:, hd * HEAD_DIM:(hd + 1) * HEAD_DIM] = o[g * tq:(g + 1) * tq].astype(o_ref.dtype)


def _dsa_attention(pbf, pg, s, tq, tk):
    n_sel = min(TOPK_MAX, s // 4)
    kvw = DSA_KV_HEADS * HEAD_DIM
    rows = DSA_GROUP * tq
    return pl.pallas_call(
        functools.partial(_dsa_kernel, tq=tq, tk=tk, n_sel=n_sel),
        grid=(s // tq,),
        in_specs=[
            pl.BlockSpec((tq, MIX_B), lambda b: (b, COL_BQ // MIX_B)),
            pl.BlockSpec((tq, IDX_HEADS * IDX_DIM), lambda b: (b, COL_IQ // (IDX_HEADS * IDX_DIM))),
            pl.BlockSpec((tq, 128), lambda b: (b, COL_SMALL // 128)),
            pl.BlockSpec((s, 128), lambda b: (0, COL_IK // 128)),
            pl.BlockSpec((s, kvw), lambda b: (0, COL_BK // kvw)),
            pl.BlockSpec((s, kvw), lambda b: (0, COL_BV // kvw)),
        ],
        out_specs=pl.BlockSpec((tq, MIX_B), lambda b: (b, 0)),
        out_shape=jax.ShapeDtypeStruct((s, MIX_B), BF16),
        scratch_shapes=[
            pltpu.VMEM((s // tk, tq, tk), F32),
            pltpu.VMEM((DSA_KV_HEADS, rows, HEAD_DIM), BF16),
            pltpu.VMEM((DSA_KV_HEADS, rows, 128), F32),
            pltpu.VMEM((DSA_KV_HEADS, rows, 128), F32),
            pltpu.VMEM((DSA_KV_HEADS, rows, HEAD_DIM), F32),
        ],
        compiler_params=_cparams(("arbitrary",), 56),
        name="dsa_attention",
    )(pbf, pbf, pg, pbf, pbf, pbf)


def _layer_norm(z, g, b):
    mu = jnp.mean(z, axis=-1, keepdims=True)
    zc = z - mu
    var = jnp.mean(zc * zc, axis=-1, keepdims=True)
    return zc * lax.rsqrt(var + LN_EPS) * g + b


def _merge_kernel(a_ref, b_ref, ga_ref, gb_ref, x_ref, wa_ref, wb_ref, wo_ref, g_ref, beta_ref,
                  wr_ref, br_ref, h_ref, eidx_ref, gate_ref, pos_ref, cnt_ref, carry_ref, *, alpha):
    i = pl.program_id(0)

    @pl.when(i == 0)
    def _():
        carry_ref[...] = jnp.zeros_like(carry_ref)

    ma = jnp.dot(a_ref[...], wa_ref[...], preferred_element_type=F32)
    mb = jnp.dot(b_ref[...], wb_ref[...], preferred_element_type=F32)
    merged = jax.nn.sigmoid(ga_ref[...]) * ma + jax.nn.sigmoid(gb_ref[...]) * mb
    y = jnp.dot(merged.astype(BF16), wo_ref[...], preferred_element_type=F32)
    h = _layer_norm(alpha * x_ref[...] + y, g_ref[...], beta_ref[...])
    h_ref[...] = h

    logits = jnp.dot(h, wr_ref[...], preferred_element_type=F32, precision=lax.Precision.HIGHEST) + br_ref[...]
    tm, ne = logits.shape
    lane = lax.broadcasted_iota(I32, (tm, ne), 1)
    lane_k = lax.broadcasted_iota(I32, (tm, TOP_K), 1)
    work = logits
    vals, sels = [], []
    eidx = jnp.zeros((tm, TOP_K), I32)
    onehot = jnp.zeros((tm, ne), F32)
    for k in range(TOP_K):
        mv = jnp.max(work, axis=1, keepdims=True)
        idx = jnp.min(jnp.where(work == mv, lane, ne), axis=1, keepdims=True)
        sel = lane == idx
        vals.append(mv)
        sels.append(sel)
        eidx = jnp.where(lane_k == k, idx, eidx)
        onehot = onehot + jnp.where(sel, 1.0, 0.0)
        work = jnp.where(sel, -jnp.inf, work)
    exps = [jnp.exp(v - vals[0]) for v in vals]
    denom = exps[0] + exps[1] + exps[2] + exps[3]
    gates = jnp.zeros((tm, TOP_K), F32)
    for k in range(TOP_K):
        gates = jnp.where(lane_k == k, exps[k] / denom, gates)

    r_i = lax.broadcasted_iota(I32, (tm, tm), 0)
    c_i = lax.broadcasted_iota(I32, (tm, tm), 1)
    lower = jnp.where(c_i < r_i, 1.0, 0.0).astype(BF16)
    rank = carry_ref[...] + jnp.dot(lower, onehot.astype(BF16), preferred_element_type=F32)
    pos = jnp.zeros((tm, TOP_K), I32)
    for k in range(TOP_K):
        pk = jnp.sum(jnp.where(sels[k], rank, 0.0), axis=1, keepdims=True).astype(I32)
        pos = jnp.where(lane_k == k, pk, pos)
    carry_ref[...] = carry_ref[...] + jnp.sum(onehot, axis=0, keepdims=True)

    eidx_ref[...] = eidx
    gate_ref[...] = gates
    pos_ref[...] = pos
    cnt_ref[...] = carry_ref[...].astype(I32)


def _merge(a_out, b_out, pg, x2d, wa, wb, wo, ln_g, ln_b, w_router, b_router, alpha):
    s, d = x2d.shape
    tm = min(256, s)
    full = lambda shape: pl.BlockSpec(shape, lambda i: (0,) * len(shape))
    return pl.pallas_call(
        functools.partial(_merge_kernel, alpha=alpha),
        grid=(s // tm,),
        in_specs=[
            pl.BlockSpec((tm, MIX_A), lambda i: (i, 0)),
            pl.BlockSpec((tm, MIX_B), lambda i: (i, 0)),
            pl.BlockSpec((tm, d), lambda i: (i, COL_GA // d)),
            pl.BlockSpec((tm, d), lambda i: (i, COL_GB // d)),
            pl.BlockSpec((tm, d), lambda i: (i, 0)),
            full((MIX_A, d)), full((MIX_B, d)), full((d, d)),
            full((1, d)), full((1, d)), full((d, N_EXPERTS)), full((1, N_EXPERTS)),
        ],
        out_specs=[
            pl.BlockSpec((tm, d), lambda i: (i, 0)),
            pl.BlockSpec((tm, TOP_K), lambda i: (i, 0)),
            pl.BlockSpec((tm, TOP_K), lambda i: (i, 0)),
            pl.BlockSpec((tm, TOP_K), lambda i: (i, 0)),
            full((1, N_EXPERTS)),
        ],
        out_shape=[
            jax.ShapeDtypeStruct((s, d), F32),
            jax.ShapeDtypeStruct((s, TOP_K), I32),
            jax.ShapeDtypeStruct((s, TOP_K), F32),
            jax.ShapeDtypeStruct((s, TOP_K), I32),
            jax.ShapeDtypeStruct((1, N_EXPERTS), I32),
        ],
        scratch_shapes=[pltpu.VMEM((1, N_EXPERTS), F32)],
        compiler_params=_cparams(("arbitrary",), 56),
        name="merge_ln_router",
    )(a_out, b_out, pg, pg, x2d, wa, wb, wo, ln_g, ln_b, w_router, b_router)


def _dispatch_kernel(dest_ref, gap_start_ref, gap_len_ref, total_ref, h_ref, xs_ref, stage_ref, zero_ref,
                     sem, zsem, *, max_slack):
    i = pl.program_id(0)
    last = pl.num_programs(0) - 1
    slot = lax.rem(i, 2)
    n = dest_ref.shape[0]

    @pl.when(i == 0)
    def _():
        zero_ref[...] = jnp.zeros_like(zero_ref)

        def gap_copy(e, k):
            return pltpu.make_async_copy(zero_ref.at[pl.ds(0, 1)], xs_ref.at[pl.ds(gap_start_ref[e] + k, 1)], zsem)

        n_slots = xs_ref.shape[0]

        def slack_copy(k):
            row = pl.multiple_of(total_ref[0] + k * MOE_CHUNK, MOE_CHUNK)
            return pltpu.make_async_copy(zero_ref, xs_ref.at[pl.ds(row, MOE_CHUNK)], zsem)

        def slack(action):
            for k in range(max_slack):
                @pl.when(total_ref[0] + k * MOE_CHUNK < n_slots)
                def _():
                    action(slack_copy(k))

        def issue(e, _):
            lax.fori_loop(0, gap_len_ref[e], lambda k, c: (gap_copy(e, k).start(), c)[1], 0)
            return 0

        def settle(e, _):
            lax.fori_loop(0, gap_len_ref[e], lambda k, c: (gap_copy(e, k).wait(), c)[1], 0)
            return 0

        lax.fori_loop(0, N_EXPERTS, issue, 0)
        slack(lambda cp: cp.start())
        lax.fori_loop(0, N_EXPERTS, settle, 0)
        slack(lambda cp: cp.wait())

    def drain(sl):
        pltpu.make_async_copy(xs_ref.at[pl.ds(0, n)], xs_ref.at[pl.ds(0, n)], sem.at[sl]).wait()

    @pl.when(i >= 2)
    def _():
        drain(slot)

    stage_ref[slot] = h_ref[...]

    def start(t, _):
        pltpu.make_async_copy(stage_ref.at[slot, pl.ds(t // TOP_K, 1)],
                              xs_ref.at[pl.ds(dest_ref[t], 1)], sem.at[slot]).start()
        return 0

    lax.fori_loop(0, n, start, 0, unroll=8)

    @pl.when(i == last)
    def _():
        drain(slot)

        @pl.when(i >= 1)
        def _():
            drain(1 - slot)


def _dispatch(h, dest_flat, gap_start, gap_len, total, n_slots):
    s, d = h.shape
    tm = min(128, s)
    smem = pl.BlockSpec(memory_space=pltpu.SMEM)
    max_slack = (n_slots - s * TOP_K + MOE_CHUNK - 1) // MOE_CHUNK
    return pl.pallas_call(
        functools.partial(_dispatch_kernel, max_slack=max_slack),
        grid=(s // tm,),
        in_specs=[
            pl.BlockSpec((tm * TOP_K,), lambda i: (i,), memory_space=pltpu.SMEM),
            smem, smem, smem,
            pl.BlockSpec((tm, d), lambda i: (i, 0)),
        ],
        out_specs=pl.BlockSpec(memory_space=pl.ANY),
        out_shape=jax.ShapeDtypeStruct((n_slots, d), h.dtype),
        scratch_shapes=[pltpu.VMEM((2, tm, d), h.dtype), pltpu.VMEM((MOE_CHUNK, d), h.dtype),
                        pltpu.SemaphoreType.DMA((2,)), pltpu.SemaphoreType.DMA(())],
        compiler_params=_cparams(("arbitrary",), 32),
        name="moe_dispatch",
    )(dest_flat, gap_start, gap_len, total, h)


def _expert_rows_pipeline(n_chunks, in_copy, out_copy, compute):
    @pl.when(n_chunks > 0)
    def _():
        in_copy(0, 0).start()

        def body(c, _):
            slot = lax.rem(c, 2)
            in_copy(c, slot).wait()

            @pl.when(c + 1 < n_chunks)
            def _():
                in_copy(c + 1, 1 - slot).start()

            @pl.when(c >= 2)
            def _():
                out_copy(c - 2, slot).wait()

            compute(slot)
            out_copy(c, slot).start()
            return 0

        lax.fori_loop(0, n_chunks, body, 0)

        @pl.when(n_chunks >= 2)
        def _():
            out_copy(n_chunks - 2, lax.rem(n_chunks, 2)).wait()

        out_copy(n_chunks - 1, lax.rem(n_chunks - 1, 2)).wait()


def _chunk_rows(start_ref, e, c, n_slots):
    del n_slots
    return pl.ds(pl.multiple_of(start_ref[e] + c * MOE_CHUNK, MOE_ALIGN), MOE_CHUNK)


def _gate_up_kernel(start_ref, cnt_ref, x_hbm, wg_ref, wu_ref, bg_ref, bu_ref, act_hbm,
                    wg_sc, wu_sc, xbuf, obuf, sem_in, sem_out):
    j = pl.program_id(0)
    e = pl.program_id(1)
    n_slots = x_hbm.shape[0]
    n_chunks = (cnt_ref[e] + MOE_CHUNK - 1) // MOE_CHUNK

    @pl.when(n_chunks > 0)
    def _():
        wg_sc[...] = wg_ref[...].astype(BF16)
        wu_sc[...] = wu_ref[...].astype(BF16)

    def in_copy(c, slot):
        return pltpu.make_async_copy(x_hbm.at[_chunk_rows(start_ref, e, c, n_slots)], xbuf.at[slot], sem_in.at[slot])

    def out_copy(c, slot):
        return pltpu.make_async_copy(obuf.at[slot], act_hbm.at[j, _chunk_rows(start_ref, e, c, n_slots)],
                                     sem_out.at[slot])

    def compute(slot):
        x = xbuf[slot].astype(BF16)
        cw = 256
        for c in range(obuf.shape[2] // cw):
            cs = slice(c * cw, (c + 1) * cw)
            g = jnp.dot(x, wg_sc[:, cs], preferred_element_type=F32) + bg_ref[:, cs]
            u = jnp.dot(x, wu_sc[:, cs], preferred_element_type=F32) + bu_ref[:, cs]
            g = jnp.minimum(g, SWIGLU_LIMIT)
            u = jnp.clip(u, -SWIGLU_LIMIT, SWIGLU_LIMIT)
            obuf[slot, :, cs] = (g * jax.nn.sigmoid(SWIGLU_ALPHA * g) * (u + 1.0)).astype(obuf.dtype)

    _expert_rows_pipeline(n_chunks, in_copy, out_copy, compute)


def _down_kernel(start_ref, cnt_ref, a_hbm, wd_ref, bd_ref, y_hbm, wd_sc, abuf, obuf, sem_in, sem_out):
    e = pl.program_id(0)
    n_slots = y_hbm.shape[0]
    n_chunks = (cnt_ref[e] + MOE_CHUNK - 1) // MOE_CHUNK
    nf, _, tf = abuf.shape[1:]

    @pl.when(n_chunks > 0)
    def _():
        wd_sc[...] = wd_ref[...].astype(BF16)

    def in_copy(c, slot):
        return pltpu.make_async_copy(a_hbm.at[:, _chunk_rows(start_ref, e, c, n_slots)], abuf.at[slot],
                                     sem_in.at[slot])

    def out_copy(c, slot):
        return pltpu.make_async_copy(obuf.at[slot], y_hbm.at[_chunk_rows(start_ref, e, c, n_slots)],
                                     sem_out.at[slot])

    def compute(slot):
        cw = 512
        for c in range(obuf.shape[2] // cw):
            cs = slice(c * cw, (c + 1) * cw)
            y = bd_ref[:, cs]
            for k in range(nf):
                y = y + jnp.dot(abuf[slot, k], wd_sc[k * tf:(k + 1) * tf, cs], preferred_element_type=F32)
            obuf[slot, :, cs] = y

    _expert_rows_pipeline(n_chunks, in_copy, out_copy, compute)


def _experts(xs, starts, counts, w_gate_up, b_gate_up, w_down, b_down):
    n_slots, d = xs.shape
    tf = 1024
    nf = D_FF // tf
    any_spec = pl.BlockSpec(memory_space=pl.ANY)
    dma_sems = pltpu.SemaphoreType.DMA((2,))
    bgu = b_gate_up.reshape(N_EXPERTS, 1, 2 * D_FF)
    n_groups = starts.shape[0]
    wi = lambda e: jnp.minimum(e, N_EXPERTS - 1)
    act = pl.pallas_call(
        _gate_up_kernel,
        grid_spec=pltpu.PrefetchScalarGridSpec(
            num_scalar_prefetch=2,
            grid=(nf, n_groups),
            in_specs=[
                any_spec,
                pl.BlockSpec((None, d, tf), lambda j, e, st, ct: (wi(e), 0, j)),
                pl.BlockSpec((None, d, tf), lambda j, e, st, ct: (wi(e), 0, nf + j)),
                pl.BlockSpec((None, 1, tf), lambda j, e, st, ct: (wi(e), 0, j)),
                pl.BlockSpec((None, 1, tf), lambda j, e, st, ct: (wi(e), 0, nf + j)),
            ],
            out_specs=any_spec,
            scratch_shapes=[pltpu.VMEM((d, tf), BF16), pltpu.VMEM((d, tf), BF16),
                            pltpu.VMEM((2, MOE_CHUNK, d), xs.dtype), pltpu.VMEM((2, MOE_CHUNK, tf), BF16),
                            dma_sems, dma_sems],
        ),
        out_shape=jax.ShapeDtypeStruct((nf, n_slots, tf), BF16),
        compiler_params=_cparams(("arbitrary", "arbitrary"), 56),
        name="moe_gate_up",
    )(starts, counts, xs, w_gate_up, w_gate_up, bgu, bgu)
    bd = b_down.reshape(N_EXPERTS, 1, d)
    ys = pl.pallas_call(
        _down_kernel,
        grid_spec=pltpu.PrefetchScalarGridSpec(
            num_scalar_prefetch=2,
            grid=(n_groups,),
            in_specs=[
                any_spec,
                pl.BlockSpec((None, D_FF, d), lambda e, st, ct: (wi(e), 0, 0)),
                pl.BlockSpec((None, 1, d), lambda e, st, ct: (wi(e), 0, 0)),
            ],
            out_specs=any_spec,
            scratch_shapes=[pltpu.VMEM((D_FF, d), BF16),
                            pltpu.VMEM((2, nf, MOE_CHUNK, tf), BF16), pltpu.VMEM((2, MOE_CHUNK, d), F32),
                            dma_sems, dma_sems],
        ),
        out_shape=jax.ShapeDtypeStruct((n_slots, d), F32),
        compiler_params=_cparams(("arbitrary",), 56),
        name="moe_down",
    )(starts, counts, act, w_down, bd)
    return ys


def _combine_kernel(dest_ref, dest_next_ref, gate_ref, h_ref, g_ref, beta_ref, ys_ref, o_ref, buf_ref, sem,
                    *, alpha):
    i = pl.program_id(0)
    last = pl.num_programs(0) - 1
    slot = lax.rem(i, 2)
    n = dest_ref.shape[0]

    def gather(dref, sl):
        def start(t, _):
            pltpu.make_async_copy(ys_ref.at[pl.ds(dref[t], 1)],
                                  buf_ref.at[sl, lax.rem(t, TOP_K), pl.ds(t // TOP_K, 1)], sem.at[sl]).start()
            return 0
        lax.fori_loop(0, n, start, 0, unroll=8)

    @pl.when(i == 0)
    def _():
        gather(dest_ref, slot)

    @pl.when(i < last)
    def _():
        gather(dest_next_ref, 1 - slot)

    pltpu.make_async_copy(buf_ref.at[slot], buf_ref.at[slot], sem.at[slot]).wait()
    gates = gate_ref[...]
    y = gates[:, 0:1] * buf_ref[slot, 0]
    for k in range(1, TOP_K):
        y = y + gates[:, k:k + 1] * buf_ref[slot, k]
    o_ref[...] = _layer_norm(alpha * h_ref[...] + y, g_ref[...], beta_ref[...])


def _combine(ys, dest_flat, gates, h, ln_g, ln_b, alpha):
    s, d = h.shape
    tm = min(128, s)
    n_steps = s // tm
    return pl.pallas_call(
        functools.partial(_combine_kernel, alpha=alpha),
        grid=(n_steps,),
        in_specs=[
            pl.BlockSpec((tm * TOP_K,), lambda i: (i,), memory_space=pltpu.SMEM),
            pl.BlockSpec((tm * TOP_K,), lambda i: (jnp.minimum(i + 1, n_steps - 1),), memory_space=pltpu.SMEM),
            pl.BlockSpec((tm, TOP_K), lambda i: (i, 0)),
            pl.BlockSpec((tm, d), lambda i: (i, 0)),
            pl.BlockSpec((1, d), lambda i: (0, 0)),
            pl.BlockSpec((1, d), lambda i: (0, 0)),
            pl.BlockSpec(memory_space=pl.ANY),
        ],
        out_specs=pl.BlockSpec((tm, d), lambda i: (i, 0)),
        out_shape=jax.ShapeDtypeStruct((s, d), F32),
        scratch_shapes=[pltpu.VMEM((2, TOP_K, tm, d), F32), pltpu.SemaphoreType.DMA((2,))],
        compiler_params=_cparams(("arbitrary",), 32),
        name="moe_combine",
    )(dest_flat, dest_flat, gates, h, ln_g, ln_b, ys)


def _layer(x2d, w_in, b_forget, w_branch_a, w_branch_b, w_out, ln1_g, ln1_b, w_router, b_router,
           w_gate_up, b_gate_up, w_down, b_down, ln2_g, ln2_b, tables):
    s, d = x2d.shape
    alpha = (2.0 * DEPTH) ** 0.25
    scale = HEAD_DIM ** -0.5 * LOG2E

    sizes = (MIX_A, MIX_A, MIX_A, FOX_HEADS, MIX_B, DSA_KV_HEADS * HEAD_DIM, DSA_KV_HEADS * HEAD_DIM,
             IDX_HEADS * IDX_DIM, IDX_DIM, IDX_HEADS, D_MODEL, D_MODEL)
    w_aq, w_ak, w_av, w_af, w_bq, w_bk, w_bv, w_iq, w_ik, w_iw, w_ga, w_gb = jnp.split(
        w_in, np.cumsum(sizes)[:-1].tolist(), axis=1)
    w_bf = jnp.concatenate(
        [w_bq * scale, w_iq, w_aq * scale, w_ak, w_av, w_bk, w_bv, w_ik,
         jnp.zeros((d, N_PBF - COL_IK - IDX_DIM), F32)], axis=1).astype(BF16)
    w_g = jnp.concatenate(
        [w_ga, w_gb, w_af, w_iw, jnp.zeros((d, N_PG - COL_SMALL - FOX_HEADS - IDX_HEADS), F32)],
        axis=1).astype(BF16)

    pbf, pg = _project(x2d, w_bf, tables, w_g)

    fox_tq = min(512, s)
    fox_tk = min(512, s)
    af_t = pg[:, COL_SMALL:COL_SMALL + FOX_HEADS].T
    c_t = _forget_cumsum(af_t, b_forget)
    c_tiles = c_t.reshape(FOX_HEADS, s // fox_tk, 1, fox_tk)
    a_out = _fox_attention(pbf, c_tiles, s, fox_tq, fox_tk, 4)

    b_out = _dsa_attention(pbf, pg, s, 128, min(512, s))

    h, eidx, gates, pos, counts = _merge(
        a_out, b_out, pg, x2d, w_branch_a.astype(BF16), w_branch_b.astype(BF16), w_out.astype(BF16),
        ln1_g.reshape(1, d), ln1_b.reshape(1, d), w_router, b_router.reshape(1, N_EXPERTS), alpha)

    counts = counts.reshape(N_EXPERTS).astype(I32)
    aligned = (counts + MOE_ALIGN - 1) // MOE_ALIGN * MOE_ALIGN
    ends = jnp.cumsum(aligned).astype(I32)
    starts = ends - aligned
    bound = s * TOP_K + N_EXPERTS * MOE_ALIGN + 2 * MOE_CHUNK
    n_slots = (bound + MOE_CHUNK - 1) // MOE_CHUNK * MOE_CHUNK
    experts = jnp.arange(N_EXPERTS, dtype=I32)
    start_of = jnp.sum(jnp.where(eidx[..., None] == experts, starts, 0), axis=-1)
    dest = (start_of + pos).reshape(s * TOP_K).astype(I32)
    total = (ends[-1:] + MOE_CHUNK - 1) // MOE_CHUNK * MOE_CHUNK
    gap_len = aligned - counts + jnp.where(experts == N_EXPERTS - 1, total[0] - ends[-1], 0)
    group_starts = jnp.concatenate([starts, total])
    group_counts = jnp.concatenate([counts + jnp.where(experts == N_EXPERTS - 1, gap_len, 0), n_slots - total])

    xs = _dispatch(h, dest, starts + counts, gap_len, total, n_slots)
    ys = _experts(xs, group_starts, group_counts, w_gate_up, b_gate_up, w_down, b_down)
    return _combine(ys, dest, gates, h, ln2_g.reshape(1, d), ln2_b.reshape(1, d), alpha)


def kernel(x, w_in, b_forget, w_branch_a, w_branch_b, w_out, ln1_g, ln1_b, w_router, b_router,
           w_gate_up, b_gate_up, w_down, b_down, ln2_g, ln2_b):
    bsz, s, d = x.shape
    tables = _rope_tables(s)
    outs = []
    for bi in range(bsz):
        xb = x[bi]
        for l in range(DEPTH):
            xb = _layer(xb, w_in[l], b_forget[l], w_branch_a[l], w_branch_b[l], w_out[l], ln1_g[l], ln1_b[l],
                        w_router[l], b_router[l], w_gate_up[l], b_gate_up[l], w_down[l], b_down[l],
                        ln2_g[l], ln2_b[l], tables)
        outs.append(xb)
    return jnp.stack(outs)
```

```python
import functools

import numpy as np
import jax
import jax.numpy as jnp
from jax import lax
from jax.experimental import pallas as pl
from jax.experimental.pallas import tpu as pltpu

F32 = jnp.float32
BF16 = jnp.bfloat16
I32 = jnp.int32

D_MODEL = 2048
DEPTH = 1
CHUNK = 64
HEAD_DIM = 128
FOX_HEADS = 8
DSA_HEADS = 8
DSA_KV_HEADS = 2
DSA_GROUP = DSA_HEADS // DSA_KV_HEADS
IDX_HEADS = 16
IDX_DIM = 64
TOPK_MAX = 256
ROPE_THETA = 500000.0
ROT_FRACTION_DEN = 4
MIX_A = FOX_HEADS * HEAD_DIM
MIX_B = DSA_HEADS * HEAD_DIM
N_EXPERTS = 32
TOP_K = 4
D_FF = D_MODEL
SWIGLU_ALPHA = 1.702
SWIGLU_LIMIT = 7.0
LN_EPS = 1e-5

MIB = 1024 * 1024
NEG_BIG = -1e30
LOG2E = 1.4426950408889634
INT_MIN = -(2 ** 31)

COL_BQ = 0
COL_IQ = 1024
COL_AQ = 2048
COL_AK = 3072
COL_AV = 4096
COL_BK = 5120
COL_BV = 5376
COL_IK = 5632
N_PBF = 5888
PROJ_TN = 256
ROPE_HEAD_TILES = (0, 1, 2, 3, 20)
ROPE_IDX_TILES = (4, 5, 6, 7, 22)
COL_GA = 0
COL_GB = 2048
COL_SMALL = 4096
N_PG = 4224
PG_TN = 384

MOE_CHUNK = 256
MOE_ALIGN = 16
ROW_BURST = 8


def _cparams(dims, vmem_mib):
    return pltpu.CompilerParams(dimension_semantics=dims, vmem_limit_bytes=vmem_mib * MIB)


def _tile_in(j, tiles):
    cond = j == tiles[0]
    for t in tiles[1:]:
        cond = cond | (j == t)
    return cond


def _proj_rope_kernel(x_ref, w_ref, tab_ref, o_ref, xb_ref):
    j = pl.program_id(1)

    @pl.when(j == 0)
    def _():
        xb_ref[...] = x_ref[...].astype(BF16)

    acc = jnp.dot(xb_ref[...], w_ref[...], preferred_element_type=F32)
    tn = acc.shape[1]
    is_head = _tile_in(j, ROPE_HEAD_TILES)
    is_idx = _tile_in(j, ROPE_IDX_TILES)

    def rope(shift):
        c = tab_ref[0, 0]
        s_prev = tab_ref[0, 1]
        s_next = tab_ref[0, 2]
        out = acc * c + pltpu.roll(acc, shift, 1) * s_prev + pltpu.roll(acc, tn - shift, 1) * s_next
        o_ref[...] = out.astype(o_ref.dtype)

    @pl.when(is_head)
    def _():
        rope(HEAD_DIM // ROT_FRACTION_DEN // 2)

    @pl.when(is_idx)
    def _():
        rope(IDX_DIM // ROT_FRACTION_DEN // 2)

    @pl.when(jnp.logical_not(is_head | is_idx))
    def _():
        o_ref[...] = acc.astype(o_ref.dtype)


def _proj_plain_kernel(x_ref, w_ref, o_ref, xb_ref):
    j = pl.program_id(1)

    @pl.when(j == 0)
    def _():
        xb_ref[...] = x_ref[...].astype(BF16)

    o_ref[...] = jnp.dot(xb_ref[...], w_ref[...], preferred_element_type=F32).astype(o_ref.dtype)


def _rope_group(j):
    return jnp.where(_tile_in(j, ROPE_IDX_TILES), 1, 0)


def _project(x2d, w_bf, tables, w_g):
    s, d = x2d.shape
    tm = min(1024, s)
    pbf = pl.pallas_call(
        _proj_rope_kernel,
        grid=(s // tm, N_PBF // PROJ_TN),
        in_specs=[
            pl.BlockSpec((tm, d), lambda i, j: (i, 0)),
            pl.BlockSpec((d, PROJ_TN), lambda i, j: (0, j)),
            pl.BlockSpec((1, 3, tm, PROJ_TN), lambda i, j: (_rope_group(j), 0, i, 0)),
        ],
        out_specs=pl.BlockSpec((tm, PROJ_TN), lambda i, j: (i, j)),
        out_shape=jax.ShapeDtypeStruct((s, N_PBF), BF16),
        scratch_shapes=[pltpu.VMEM((tm, d), BF16)],
        compiler_params=_cparams(("arbitrary", "arbitrary"), 48),
        name="proj_bf16",
    )(x2d, w_bf, tables)
    pg = pl.pallas_call(
        _proj_plain_kernel,
        grid=(s // tm, N_PG // PG_TN),
        in_specs=[
            pl.BlockSpec((tm, d), lambda i, j: (i, 0)),
            pl.BlockSpec((d, PG_TN), lambda i, j: (0, j)),
        ],
        out_specs=pl.BlockSpec((tm, PG_TN), lambda i, j: (i, j)),
        out_shape=jax.ShapeDtypeStruct((s, N_PG), F32),
        scratch_shapes=[pltpu.VMEM((tm, d), BF16)],
        compiler_params=_cparams(("arbitrary", "arbitrary"), 48),
        name="proj_f32",
    )(x2d, w_g)
    return pbf, pg


def _rope_tables(s):
    pos = jnp.arange(s, dtype=F32)

    def one(period):
        rot = period // ROT_FRACTION_DEN
        half = rot // 2
        inv = jnp.power(ROPE_THETA, -jnp.arange(0, rot, 2, dtype=F32) / rot)
        ang = pos[:, None] * inv[None, :]
        cos, sin = jnp.cos(ang), jnp.sin(ang)
        zero = jnp.zeros((s, period - rot), F32)
        c = jnp.concatenate([cos, cos, jnp.ones((s, period - rot), F32)], axis=1)
        s_prev = jnp.concatenate([jnp.zeros((s, half), F32), sin, zero], axis=1)
        s_next = jnp.concatenate([-sin, jnp.zeros((s, half), F32), zero], axis=1)
        reps = PROJ_TN // period
        return jnp.stack([jnp.tile(c, (1, reps)), jnp.tile(s_prev, (1, reps)), jnp.tile(s_next, (1, reps))])

    return jnp.stack([one(HEAD_DIM), one(IDX_DIM)])


def _cumsum_kernel(af_ref, bf_ref, c_ref, carry_ref):
    i = pl.program_id(0)

    @pl.when(i == 0)
    def _():
        carry_ref[...] = jnp.zeros_like(carry_ref)

    z = af_ref[...] + bf_ref[...]
    logf = jnp.minimum(z, 0.0) - jnp.log1p(jnp.exp(-jnp.abs(z)))
    t = z.shape[1]
    row = lax.broadcasted_iota(I32, (t, t), 0)
    col = lax.broadcasted_iota(I32, (t, t), 1)
    upper = (row <= col).astype(F32)
    c = jnp.dot(logf, upper, preferred_element_type=F32, precision=lax.Precision.HIGHEST) + carry_ref[...]
    c_ref[...] = c * LOG2E
    carry_ref[...] = c[:, t - 1:t]


def _forget_cumsum(af_t, b_forget):
    h, s = af_t.shape
    t = min(512, s)
    return pl.pallas_call(
        _cumsum_kernel,
        grid=(s // t,),
        in_specs=[pl.BlockSpec((h, t), lambda i: (0, i)), pl.BlockSpec((h, 1), lambda i: (0, 0))],
        out_specs=pl.BlockSpec((h, t), lambda i: (0, i)),
        out_shape=jax.ShapeDtypeStruct((h, s), F32),
        scratch_shapes=[pltpu.VMEM((h, 1), F32)],
        compiler_params=_cparams(("arbitrary",), 32),
        name="forget_cumsum",
    )(af_t, b_forget.reshape(h, 1).astype(F32))


def _softmax_init(m_ref, l_ref, acc_ref):
    m_ref[...] = jnp.full(m_ref.shape, NEG_BIG, F32)
    l_ref[...] = jnp.zeros(l_ref.shape, F32)
    acc_ref[...] = jnp.zeros(acc_ref.shape, F32)


def _softmax_tile(s, v, m_ref, l_ref, acc_ref, i):
    reps = s.shape[1] // 128
    m_old = m_ref[i]
    m_new = jnp.maximum(m_old, jnp.max(s, axis=1, keepdims=True))
    p = jnp.exp2(s - jnp.tile(m_new, (1, reps)))
    alpha = jnp.exp2(m_old - m_new)
    psum = p[:, :128]
    for c in range(1, reps):
        psum = psum + p[:, c * 128:(c + 1) * 128]
    l_ref[i] = alpha * l_ref[i] + psum
    acc_ref[i] = alpha * acc_ref[i] + jnp.dot(p.astype(BF16), v, preferred_element_type=F32)
    m_ref[i] = m_new


def _softmax_result(l_ref, acc_ref, i):
    return acc_ref[i] / jnp.sum(l_ref[i], axis=1, keepdims=True)


def _fox_kernel(q_ref, k_ref, v_ref, c_ref, o_ref, m_ref, l_ref, acc_ref, *, tq, tk, nh):
    qi = pl.program_id(1)
    _softmax_init(m_ref, l_ref, acc_ref)

    def tile(kt, masked):
        start = pl.multiple_of(kt * tk, tk)
        if masked:
            row = qi * tq + lax.broadcasted_iota(I32, (tq, tk), 0)
            col = kt * tk + lax.broadcasted_iota(I32, (tq, tk), 1)
            causal = col <= row
        for h in range(nh):
            hs = slice(h * HEAD_DIM, (h + 1) * HEAD_DIM)
            k = k_ref[pl.ds(start, tk), hs]
            v = v_ref[pl.ds(start, tk), hs]
            s = lax.dot_general(q_ref[:, hs], k, (((1,), (1,)), ((), ())), preferred_element_type=F32)
            s = s - c_ref[h, kt]
            if masked:
                s = jnp.where(causal, s, NEG_BIG)
            _softmax_tile(s, v, m_ref, l_ref, acc_ref, h)

    n_full = (qi * tq) // tk

    def body(kt, _):
        tile(kt, False)
        return 0

    lax.fori_loop(0, n_full, body, 0)
    for t in range((tq + tk - 1) // tk):
        tile(n_full + t, True)
    for h in range(nh):
        o_ref[:, h * HEAD_DIM:(h + 1) * HEAD_DIM] = _softmax_result(l_ref, acc_ref, h).astype(o_ref.dtype)


def _fox_attention(pbf, c_tiles, s, tq, tk, nh):
    cb = nh * HEAD_DIM
    return pl.pallas_call(
        functools.partial(_fox_kernel, tq=tq, tk=tk, nh=nh),
        grid=(FOX_HEADS // nh, s // tq),
        in_specs=[
            pl.BlockSpec((tq, cb), lambda g, i: (i, COL_AQ // cb + g)),
            pl.BlockSpec((s, cb), lambda g, i: (0, COL_AK // cb + g)),
            pl.BlockSpec((s, cb), lambda g, i: (0, COL_AV // cb + g)),
            pl.BlockSpec((nh, s // tk, 1, tk), lambda g, i: (g, 0, 0, 0)),
        ],
        out_specs=pl.BlockSpec((tq, cb), lambda g, i: (i, g)),
        out_shape=jax.ShapeDtypeStruct((s, MIX_A), BF16),
        scratch_shapes=[
            pltpu.VMEM((nh, tq, 128), F32),
            pltpu.VMEM((nh, tq, 128), F32),
            pltpu.VMEM((nh, tq, HEAD_DIM), F32),
        ],
        compiler_params=_cparams(("arbitrary", "arbitrary"), 56),
        name="fox_attention",
    )(pbf, pbf, pbf, c_tiles)


def _dsa_kernel(bq_ref, iq_ref, iw_ref, ik_ref, bk_ref, bv_ref, o_ref,
                key_ref, qg_ref, m_ref, l_ref, acc_ref, *, tq, tk, n_sel):
    b = pl.program_id(0)
    n_tiles = (b * tq) // tk + 1
    row_g = b * tq + lax.broadcasted_iota(I32, (tq, tk), 0)
    adm_end = (row_g // CHUNK + 1) * CHUNK
    col_l = lax.broadcasted_iota(I32, (tq, tk), 1)

    idx_scale = (IDX_HEADS ** -0.5) * (IDX_DIM ** -0.5)
    iw = iw_ref[...][:, 8:8 + IDX_HEADS] * idx_scale
    iq = iq_ref[...]

    def score_tile(kt, _):
        start = pl.multiple_of(kt * tk, tk)
        ik = ik_ref[pl.ds(start, tk), :][:, :IDX_DIM]
        acc = jnp.zeros((tq, tk), F32)
        for h in range(IDX_HEADS):
            a = iq[:, h * IDX_DIM:(h + 1) * IDX_DIM]
            rel = lax.dot_general(a, ik, (((1,), (1,)), ((), ())), preferred_element_type=F32)
            acc = acc + jnp.maximum(rel, 0.0) * iw[:, h:h + 1]
        key_ref[kt] = jnp.where(kt * tk + col_l < adm_end, acc, -jnp.inf)
        return 0

    lax.fori_loop(0, n_tiles, score_tile, 0)

    def float_of(code_u):
        code = code_u ^ INT_MIN
        return pltpu.bitcast(jnp.where(code >= 0, code, code ^ 0x7FFFFFFF), F32)

    def count_ge(cand):
        def body(kt, part):
            ge = jnp.where(key_ref[kt] >= cand, 1, 0)
            for c in range(tk // 128):
                part = part + ge[:, c * 128:(c + 1) * 128]
            return part
        part = lax.fori_loop(0, n_tiles, body, jnp.zeros((tq, 128), I32))
        return jnp.sum(part, axis=1, keepdims=True)

    def search_cond(carry):
        i, _, _, pending = carry
        return (i < 32) & (pending > 0)

    def search_step(carry):
        i, t_u, hit, _ = carry
        cand_u = t_u | lax.shift_left(jnp.int32(1), 31 - i)
        cnt = count_ge(float_of(cand_u))
        t_u = jnp.where(cnt >= n_sel, cand_u, t_u)
        hit = jnp.where(cnt == n_sel, 1, hit)
        return i + 1, t_u, hit, jnp.sum(1 - hit)

    zeros = jnp.zeros((tq, 1), I32)
    _, t_u, _, _ = lax.while_loop(search_cond, search_step, (jnp.int32(0), zeros, zeros, jnp.int32(tq)))
    thr = jnp.where(t_u == 0, jnp.finfo(F32).min, float_of(t_u))

    for j in range(DSA_KV_HEADS):
        for g in range(DSA_GROUP):
            hd = j * DSA_GROUP + g
            qg_ref[j, g * tq:(g + 1) * tq, :] = bq_ref[:, hd * HEAD_DIM:(hd + 1) * HEAD_DIM]
    _softmax_init(m_ref, l_ref, acc_ref)

    def attn_tile(kt, _):
        start = pl.multiple_of(kt * tk, tk)
        sel = key_ref[kt] >= thr
        for j in range(DSA_KV_HEADS):
            hs = slice(j * HEAD_DIM, (j + 1) * HEAD_DIM)
            k = bk_ref[pl.ds(start, tk), hs]
            v = bv_ref[pl.ds(start, tk), hs]
            s = lax.dot_general(qg_ref[j], k, (((1,), (1,)), ((), ())), preferred_element_type=F32)
            s = jnp.where(sel[None], s.reshape(DSA_GROUP, tq, tk), NEG_BIG).reshape(DSA_GROUP * tq, tk)
            _softmax_tile(s, v, m_ref, l_ref, acc_ref, j)
        return 0

    lax.fori_loop(0, n_tiles, attn_tile, 0)

    for j in range(DSA_KV_HEADS):
        o = _softmax_result(l_ref, acc_ref, j)
        for g in range(DSA_GROUP):
            hd = j * DSA_GROUP + g
            o_ref[:, hd * HEAD_DIM:(hd + 1) * HEAD_DIM] = o[g * tq:(g + 1) * tq].astype(o_ref.dtype)


def _dsa_attention(pbf, pg, s, tq, tk):
    n_sel = min(TOPK_MAX, s // 4)
    kvw = DSA_KV_HEADS * HEAD_DIM
    rows = DSA_GROUP * tq
    return pl.pallas_call(
        functools.partial(_dsa_kernel, tq=tq, tk=tk, n_sel=n_sel),
        grid=(s // tq,),
        in_specs=[
            pl.BlockSpec((tq, MIX_B), lambda b: (b, COL_BQ // MIX_B)),
            pl.BlockSpec((tq, IDX_HEADS * IDX_DIM), lambda b: (b, COL_IQ // (IDX_HEADS * IDX_DIM))),
            pl.BlockSpec((tq, 128), lambda b: (b, COL_SMALL // 128)),
            pl.BlockSpec((s, 128), lambda b: (0, COL_IK // 128)),
            pl.BlockSpec((s, kvw), lambda b: (0, COL_BK // kvw)),
            pl.BlockSpec((s, kvw), lambda b: (0, COL_BV // kvw)),
        ],
        out_specs=pl.BlockSpec((tq, MIX_B), lambda b: (b, 0)),
        out_shape=jax.ShapeDtypeStruct((s, MIX_B), BF16),
        scratch_shapes=[
            pltpu.VMEM((s // tk, tq, tk), F32),
            pltpu.VMEM((DSA_KV_HEADS, rows, HEAD_DIM), BF16),
            pltpu.VMEM((DSA_KV_HEADS, rows, 128), F32),
            pltpu.VMEM((DSA_KV_HEADS, rows, 128), F32),
            pltpu.VMEM((DSA_KV_HEADS, rows, HEAD_DIM), F32),
        ],
        compiler_params=_cparams(("arbitrary",), 56),
        name="dsa_attention",
    )(pbf, pbf, pg, pbf, pbf, pbf)


def _layer_norm(z, g, b):
    mu = jnp.mean(z, axis=-1, keepdims=True)
    zc = z - mu
    var = jnp.mean(zc * zc, axis=-1, keepdims=True)
    return zc * lax.rsqrt(var + LN_EPS) * g + b


def _merge_kernel(a_ref, b_ref, ga_ref, gb_ref, x_ref, wa_ref, wb_ref, wo_ref, g_ref, beta_ref,
                  wr_ref, br_ref, h_ref, eidx_ref, gate_ref, pos_ref, cnt_ref, carry_ref, *, alpha):
    i = pl.program_id(0)

    @pl.when(i == 0)
    def _():
        carry_ref[...] = jnp.zeros_like(carry_ref)

    ma = jnp.dot(a_ref[...], wa_ref[...], preferred_element_type=F32)
    mb = jnp.dot(b_ref[...], wb_ref[...], preferred_element_type=F32)
    merged = jax.nn.sigmoid(ga_ref[...]) * ma + jax.nn.sigmoid(gb_ref[...]) * mb
    y = jnp.dot(merged.astype(BF16), wo_ref[...], preferred_element_type=F32)
    h = _layer_norm(alpha * x_ref[...] + y, g_ref[...], beta_ref[...])
    h_ref[...] = h

    logits = jnp.dot(h, wr_ref[...], preferred_element_type=F32, precision=lax.Precision.HIGHEST) + br_ref[...]
    tm, ne = logits.shape
    lane = lax.broadcasted_iota(I32, (tm, ne), 1)
    lane_k = lax.broadcasted_iota(I32, (tm, TOP_K), 1)
    work = logits
    vals, sels = [], []
    eidx = jnp.zeros((tm, TOP_K), I32)
    onehot = jnp.zeros((tm, ne), F32)
    for k in range(TOP_K):
        mv = jnp.max(work, axis=1, keepdims=True)
        idx = jnp.min(jnp.where(work == mv, lane, ne), axis=1, keepdims=True)
        sel = lane == idx
        vals.append(mv)
        sels.append(sel)
        eidx = jnp.where(lane_k == k, idx, eidx)
        onehot = onehot + jnp.where(sel, 1.0, 0.0)
        work = jnp.where(sel, -jnp.inf, work)
    exps = [jnp.exp(v - vals[0]) for v in vals]
    denom = exps[0] + exps[1] + exps[2] + exps[3]
    gates = jnp.zeros((tm, TOP_K), F32)
    for k in range(TOP_K):
        gates = jnp.where(lane_k == k, exps[k] / denom, gates)

    r_i = lax.broadcasted_iota(I32, (tm, tm), 0)
    c_i = lax.broadcasted_iota(I32, (tm, tm), 1)
    lower = jnp.where(c_i < r_i, 1.0, 0.0).astype(BF16)
    rank = carry_ref[...] + jnp.dot(lower, onehot.astype(BF16), preferred_element_type=F32)
    pos = jnp.zeros((tm, TOP_K), I32)
    for k in range(TOP_K):
        pk = jnp.sum(jnp.where(sels[k], rank, 0.0), axis=1, keepdims=True).astype(I32)
        pos = jnp.where(lane_k == k, pk, pos)
    carry_ref[...] = carry_ref[...] + jnp.sum(onehot, axis=0, keepdims=True)

    eidx_ref[...] = eidx
    gate_ref[...] = gates
    pos_ref[...] = pos
    cnt_ref[...] = carry_ref[...].astype(I32)


def _merge(a_out, b_out, pg, x2d, wa, wb, wo, ln_g, ln_b, w_router, b_router, alpha):
    s, d = x2d.shape
    tm = min(256, s)
    full = lambda shape: pl.BlockSpec(shape, lambda i: (0,) * len(shape))
    return pl.pallas_call(
        functools.partial(_merge_kernel, alpha=alpha),
        grid=(s // tm,),
        in_specs=[
            pl.BlockSpec((tm, MIX_A), lambda i: (i, 0)),
            pl.BlockSpec((tm, MIX_B), lambda i: (i, 0)),
            pl.BlockSpec((tm, d), lambda i: (i, COL_GA // d)),
            pl.BlockSpec((tm, d), lambda i: (i, COL_GB // d)),
            pl.BlockSpec((tm, d), lambda i: (i, 0)),
            full((MIX_A, d)), full((MIX_B, d)), full((d, d)),
            full((1, d)), full((1, d)), full((d, N_EXPERTS)), full((1, N_EXPERTS)),
        ],
        out_specs=[
            pl.BlockSpec((tm, d), lambda i: (i, 0)),
            pl.BlockSpec((tm, TOP_K), lambda i: (i, 0)),
            pl.BlockSpec((tm, TOP_K), lambda i: (i, 0)),
            pl.BlockSpec((tm, TOP_K), lambda i: (i, 0)),
            full((1, N_EXPERTS)),
        ],
        out_shape=[
            jax.ShapeDtypeStruct((s, d), F32),
            jax.ShapeDtypeStruct((s, TOP_K), I32),
            jax.ShapeDtypeStruct((s, TOP_K), F32),
            jax.ShapeDtypeStruct((s, TOP_K), I32),
            jax.ShapeDtypeStruct((1, N_EXPERTS), I32),
        ],
        scratch_shapes=[pltpu.VMEM((1, N_EXPERTS), F32)],
        compiler_params=_cparams(("arbitrary",), 56),
        name="merge_ln_router",
    )(a_out, b_out, pg, pg, x2d, wa, wb, wo, ln_g, ln_b, w_router, b_router)


def _dispatch_kernel(dest_ref, gap_start_ref, gap_len_ref, total_ref, h_ref, xs_ref, stage_ref, zero_ref,
                     sem, zsem, *, max_slack):
    i = pl.program_id(0)
    last = pl.num_programs(0) - 1
    slot = lax.rem(i, 2)
    n = dest_ref.shape[0]

    @pl.when(i == 0)
    def _():
        zero_ref[...] = jnp.zeros_like(zero_ref)

        def gap_copy(e, k):
            return pltpu.make_async_copy(zero_ref.at[pl.ds(0, 1)], xs_ref.at[pl.ds(gap_start_ref[e] + k, 1)], zsem)

        n_slots = xs_ref.shape[0]

        def slack_copy(k):
            row = pl.multiple_of(total_ref[0] + k * MOE_CHUNK, MOE_CHUNK)
            return pltpu.make_async_copy(zero_ref, xs_ref.at[pl.ds(row, MOE_CHUNK)], zsem)

        def slack(action):
            for k in range(max_slack):
                @pl.when(total_ref[0] + k * MOE_CHUNK < n_slots)
                def _():
                    action(slack_copy(k))

        def issue(e, _):
            lax.fori_loop(0, gap_len_ref[e], lambda k, c: (gap_copy(e, k).start(), c)[1], 0)
            return 0

        def settle(e, _):
            lax.fori_loop(0, gap_len_ref[e], lambda k, c: (gap_copy(e, k).wait(), c)[1], 0)
            return 0

        lax.fori_loop(0, N_EXPERTS, issue, 0)
        slack(lambda cp: cp.start())
        lax.fori_loop(0, N_EXPERTS, settle, 0)
        slack(lambda cp: cp.wait())

    def drain(sl):
        pltpu.make_async_copy(xs_ref.at[pl.ds(0, n)], xs_ref.at[pl.ds(0, n)], sem.at[sl]).wait()

    @pl.when(i >= 2)
    def _():
        drain(slot)

    stage_ref[slot] = h_ref[...]

    def start(t8, _):
        for u in range(ROW_BURST):
            t = t8 * ROW_BURST + u
            pltpu.make_async_copy(stage_ref.at[slot, pl.ds(t // TOP_K, 1)],
                                  xs_ref.at[pl.ds(dest_ref[t], 1)], sem.at[slot]).start(priority=u % 2)
        return 0

    lax.fori_loop(0, n // ROW_BURST, start, 0)

    @pl.when(i == last)
    def _():
        drain(slot)

        @pl.when(i >= 1)
        def _():
            drain(1 - slot)


def _dispatch(h, dest_flat, gap_start, gap_len, total, n_slots):
    s, d = h.shape
    tm = min(128, s)
    smem = pl.BlockSpec(memory_space=pltpu.SMEM)
    max_slack = (n_slots - s * TOP_K + MOE_CHUNK - 1) // MOE_CHUNK
    return pl.pallas_call(
        functools.partial(_dispatch_kernel, max_slack=max_slack),
        grid=(s // tm,),
        in_specs=[
            pl.BlockSpec((tm * TOP_K,), lambda i: (i,), memory_space=pltpu.SMEM),
            smem, smem, smem,
            pl.BlockSpec((tm, d), lambda i: (i, 0)),
        ],
        out_specs=pl.BlockSpec(memory_space=pl.ANY),
        out_shape=jax.ShapeDtypeStruct((n_slots, d), h.dtype),
        scratch_shapes=[pltpu.VMEM((2, tm, d), h.dtype), pltpu.VMEM((MOE_CHUNK, d), h.dtype),
                        pltpu.SemaphoreType.DMA((2,)), pltpu.SemaphoreType.DMA(())],
        compiler_params=_cparams(("arbitrary",), 32),
        name="moe_dispatch",
    )(dest_flat, gap_start, gap_len, total, h)


ROW_DMA_PRIORITY = 1


def _expert_rows_pipeline(n_chunks, prologue, in_copy, out_copy, compute):
    @pl.when(n_chunks > 0)
    def _():
        in_copy(0, 0).start(priority=ROW_DMA_PRIORITY)
        prologue()

        def body(c, _):
            slot = lax.rem(c, 2)
            in_copy(c, slot).wait()

            @pl.when(c + 1 < n_chunks)
            def _():
                in_copy(c + 1, 1 - slot).start(priority=ROW_DMA_PRIORITY)

            @pl.when(c >= 2)
            def _():
                out_copy(c - 2, slot).wait()

            compute(slot)
            out_copy(c, slot).start(priority=ROW_DMA_PRIORITY)
            return 0

        lax.fori_loop(0, n_chunks, body, 0)

        @pl.when(n_chunks >= 2)
        def _():
            out_copy(n_chunks - 2, lax.rem(n_chunks, 2)).wait()

        out_copy(n_chunks - 1, lax.rem(n_chunks - 1, 2)).wait()


def _chunk_rows(start_ref, e, c, n_slots):
    del n_slots
    return pl.ds(pl.multiple_of(start_ref[e] + c * MOE_CHUNK, MOE_ALIGN), MOE_CHUNK)


def _gate_up_kernel(start_ref, cnt_ref, x_hbm, wg_ref, wu_ref, bg_ref, bu_ref, act_hbm,
                    wg_sc, wu_sc, xbuf, obuf, sem_in, sem_out):
    j = pl.program_id(0)
    e = pl.program_id(1)
    n_slots = x_hbm.shape[0]
    n_chunks = (cnt_ref[e] + MOE_CHUNK - 1) // MOE_CHUNK

    def cast_weights():
        wg_sc[...] = wg_ref[...].astype(BF16)
        wu_sc[...] = wu_ref[...].astype(BF16)

    def in_copy(c, slot):
        return pltpu.make_async_copy(x_hbm.at[_chunk_rows(start_ref, e, c, n_slots)], xbuf.at[slot], sem_in.at[slot])

    def out_copy(c, slot):
        return pltpu.make_async_copy(obuf.at[slot], act_hbm.at[j, _chunk_rows(start_ref, e, c, n_slots)],
                                     sem_out.at[slot])

    def compute(slot):
        x = xbuf[slot].astype(BF16)
        cw = 256
        for c in range(obuf.shape[2] // cw):
            cs = slice(c * cw, (c + 1) * cw)
            g = jnp.dot(x, wg_sc[:, cs], preferred_element_type=F32) + bg_ref[:, cs]
            u = jnp.dot(x, wu_sc[:, cs], preferred_element_type=F32) + bu_ref[:, cs]
            g = jnp.minimum(g, SWIGLU_LIMIT)
            u = jnp.clip(u, -SWIGLU_LIMIT, SWIGLU_LIMIT)
            obuf[slot, :, cs] = (g * jax.nn.sigmoid(SWIGLU_ALPHA * g) * (u + 1.0)).astype(obuf.dtype)

    _expert_rows_pipeline(n_chunks, cast_weights, in_copy, out_copy, compute)


def _down_kernel(start_ref, cnt_ref, a_hbm, wd_ref, bd_ref, y_hbm, wd_sc, abuf, obuf, sem_in, sem_out):
    e = pl.program_id(0)
    n_slots = y_hbm.shape[0]
    n_chunks = (cnt_ref[e] + MOE_CHUNK - 1) // MOE_CHUNK
    nf, _, tf = abuf.shape[1:]

    def cast_weights():
        wd_sc[...] = wd_ref[...].astype(BF16)

    def in_copy(c, slot):
        return pltpu.make_async_copy(a_hbm.at[:, _chunk_rows(start_ref, e, c, n_slots)], abuf.at[slot],
                                     sem_in.at[slot])

    def out_copy(c, slot):
        return pltpu.make_async_copy(obuf.at[slot], y_hbm.at[_chunk_rows(start_ref, e, c, n_slots)],
                                     sem_out.at[slot])

    def compute(slot):
        cw = 512
        for c in range(obuf.shape[2] // cw):
            cs = slice(c * cw, (c + 1) * cw)
            y = bd_ref[:, cs]
            for k in range(nf):
                y = y + jnp.dot(abuf[slot, k], wd_sc[k * tf:(k + 1) * tf, cs], preferred_element_type=F32)
            obuf[slot, :, cs] = y

    _expert_rows_pipeline(n_chunks, cast_weights, in_copy, out_copy, compute)


def _experts(xs, starts, counts, w_gate_up, b_gate_up, w_down, b_down):
    n_slots, d = xs.shape
    tf = 1024
    nf = D_FF // tf
    any_spec = pl.BlockSpec(memory_space=pl.ANY)
    dma_sems = pltpu.SemaphoreType.DMA((2,))
    bgu = b_gate_up.reshape(N_EXPERTS, 1, 2 * D_FF)
    n_groups = starts.shape[0]
    wi = lambda e: jnp.minimum(e, N_EXPERTS - 1)
    act = pl.pallas_call(
        _gate_up_kernel,
        grid_spec=pltpu.PrefetchScalarGridSpec(
            num_scalar_prefetch=2,
            grid=(nf, n_groups),
            in_specs=[
                any_spec,
                pl.BlockSpec((None, d, tf), lambda j, e, st, ct: (wi(e), 0, j)),
                pl.BlockSpec((None, d, tf), lambda j, e, st, ct: (wi(e), 0, nf + j)),
                pl.BlockSpec((None, 1, tf), lambda j, e, st, ct: (wi(e), 0, j)),
                pl.BlockSpec((None, 1, tf), lambda j, e, st, ct: (wi(e), 0, nf + j)),
            ],
            out_specs=any_spec,
            scratch_shapes=[pltpu.VMEM((d, tf), BF16), pltpu.VMEM((d, tf), BF16),
                            pltpu.VMEM((2, MOE_CHUNK, d), xs.dtype), pltpu.VMEM((2, MOE_CHUNK, tf), BF16),
                            dma_sems, dma_sems],
        ),
        out_shape=jax.ShapeDtypeStruct((nf, n_slots, tf), BF16),
        compiler_params=_cparams(("arbitrary", "arbitrary"), 56),
        name="moe_gate_up",
    )(starts, counts, xs, w_gate_up, w_gate_up, bgu, bgu)
    bd = b_down.reshape(N_EXPERTS, 1, d)
    ys = pl.pallas_call(
        _down_kernel,
        grid_spec=pltpu.PrefetchScalarGridSpec(
            num_scalar_prefetch=2,
            grid=(n_groups,),
            in_specs=[
                any_spec,
                pl.BlockSpec((None, D_FF, d), lambda e, st, ct: (wi(e), 0, 0)),
                pl.BlockSpec((None, 1, d), lambda e, st, ct: (wi(e), 0, 0)),
            ],
            out_specs=any_spec,
            scratch_shapes=[pltpu.VMEM((D_FF, d), BF16),
                            pltpu.VMEM((2, nf, MOE_CHUNK, tf), BF16), pltpu.VMEM((2, MOE_CHUNK, d), F32),
                            dma_sems, dma_sems],
        ),
        out_shape=jax.ShapeDtypeStruct((n_slots, d), F32),
        compiler_params=_cparams(("arbitrary",), 56),
        name="moe_down",
    )(starts, counts, act, w_down, bd)
    return ys


def _combine_kernel(dest_ref, dest_next_ref, gate_ref, h_ref, g_ref, beta_ref, ys_ref, o_ref, buf_ref, sem,
                    *, alpha):
    i = pl.program_id(0)
    last = pl.num_programs(0) - 1
    slot = lax.rem(i, 2)
    n = dest_ref.shape[0]

    def gather(dref, sl):
        def start(t8, _):
            for u in range(ROW_BURST):
                t = t8 * ROW_BURST + u
                pltpu.make_async_copy(ys_ref.at[pl.ds(dref[t], 1)],
                                      buf_ref.at[sl, u % TOP_K, pl.ds(t // TOP_K, 1)],
                                      sem.at[sl]).start(priority=u % 2)
            return 0
        lax.fori_loop(0, n // ROW_BURST, start, 0)

    @pl.when(i == 0)
    def _():
        gather(dest_ref, slot)

    @pl.when(i < last)
    def _():
        gather(dest_next_ref, 1 - slot)

    pltpu.make_async_copy(buf_ref.at[slot], buf_ref.at[slot], sem.at[slot]).wait()
    gates = gate_ref[...]
    y = gates[:, 0:1] * buf_ref[slot, 0]
    for k in range(1, TOP_K):
        y = y + gates[:, k:k + 1] * buf_ref[slot, k]
    o_ref[...] = _layer_norm(alpha * h_ref[...] + y, g_ref[...], beta_ref[...])


def _combine(ys, dest_flat, gates, h, ln_g, ln_b, alpha):
    s, d = h.shape
    tm = min(128, s)
    n_steps = s // tm
    return pl.pallas_call(
        functools.partial(_combine_kernel, alpha=alpha),
        grid=(n_steps,),
        in_specs=[
            pl.BlockSpec((tm * TOP_K,), lambda i: (i,), memory_space=pltpu.SMEM),
            pl.BlockSpec((tm * TOP_K,), lambda i: (jnp.minimum(i + 1, n_steps - 1),), memory_space=pltpu.SMEM),
            pl.BlockSpec((tm, TOP_K), lambda i: (i, 0)),
            pl.BlockSpec((tm, d), lambda i: (i, 0)),
            pl.BlockSpec((1, d), lambda i: (0, 0)),
            pl.BlockSpec((1, d), lambda i: (0, 0)),
            pl.BlockSpec(memory_space=pl.ANY),
        ],
        out_specs=pl.BlockSpec((tm, d), lambda i: (i, 0)),
        out_shape=jax.ShapeDtypeStruct((s, d), F32),
        scratch_shapes=[pltpu.VMEM((2, TOP_K, tm, d), F32), pltpu.SemaphoreType.DMA((2,))],
        compiler_params=_cparams(("arbitrary",), 32),
        name="moe_combine",
    )(dest_flat, dest_flat, gates, h, ln_g, ln_b, ys)


def _layer(x2d, w_in, b_forget, w_branch_a, w_branch_b, w_out, ln1_g, ln1_b, w_router, b_router,
           w_gate_up, b_gate_up, w_down, b_down, ln2_g, ln2_b, tables):
    s, d = x2d.shape
    alpha = (2.0 * DEPTH) ** 0.25
    scale = HEAD_DIM ** -0.5 * LOG2E

    sizes = (MIX_A, MIX_A, MIX_A, FOX_HEADS, MIX_B, DSA_KV_HEADS * HEAD_DIM, DSA_KV_HEADS * HEAD_DIM,
             IDX_HEADS * IDX_DIM, IDX_DIM, IDX_HEADS, D_MODEL, D_MODEL)
    w_aq, w_ak, w_av, w_af, w_bq, w_bk, w_bv, w_iq, w_ik, w_iw, w_ga, w_gb = jnp.split(
        w_in, np.cumsum(sizes)[:-1].tolist(), axis=1)
    w_bf = jnp.concatenate(
        [w_bq * scale, w_iq, w_aq * scale, w_ak, w_av, w_bk, w_bv, w_ik,
         jnp.zeros((d, N_PBF - COL_IK - IDX_DIM), F32)], axis=1).astype(BF16)
    w_g = jnp.concatenate(
        [w_ga, w_gb, w_af, w_iw, jnp.zeros((d, N_PG - COL_SMALL - FOX_HEADS - IDX_HEADS), F32)],
        axis=1).astype(BF16)

    pbf, pg = _project(x2d, w_bf, tables, w_g)

    fox_tq = min(512, s)
    fox_tk = min(512, s)
    af_t = pg[:, COL_SMALL:COL_SMALL + FOX_HEADS].T
    c_t = _forget_cumsum(af_t, b_forget)
    c_tiles = c_t.reshape(FOX_HEADS, s // fox_tk, 1, fox_tk)
    a_out = _fox_attention(pbf, c_tiles, s, fox_tq, fox_tk, 4)

    b_out = _dsa_attention(pbf, pg, s, 128, min(512, s))

    h, eidx, gates, pos, counts = _merge(
        a_out, b_out, pg, x2d, w_branch_a.astype(BF16), w_branch_b.astype(BF16), w_out.astype(BF16),
        ln1_g.reshape(1, d), ln1_b.reshape(1, d), w_router, b_router.reshape(1, N_EXPERTS), alpha)

    counts = counts.reshape(N_EXPERTS).astype(I32)
    aligned = (counts + MOE_ALIGN - 1) // MOE_ALIGN * MOE_ALIGN
    ends = jnp.cumsum(aligned).astype(I32)
    starts = ends - aligned
    bound = s * TOP_K + N_EXPERTS * MOE_ALIGN + 2 * MOE_CHUNK
    n_slots = (bound + MOE_CHUNK - 1) // MOE_CHUNK * MOE_CHUNK
    experts = jnp.arange(N_EXPERTS, dtype=I32)
    start_of = jnp.sum(jnp.where(eidx[..., None] == experts, starts, 0), axis=-1)
    dest = (start_of + pos).reshape(s * TOP_K).astype(I32)
    total = (ends[-1:] + MOE_CHUNK - 1) // MOE_CHUNK * MOE_CHUNK
    gap_len = aligned - counts + jnp.where(experts == N_EXPERTS - 1, total[0] - ends[-1], 0)
    group_starts = jnp.concatenate([starts, total])
    group_counts = jnp.concatenate([counts + jnp.where(experts == N_EXPERTS - 1, gap_len, 0), n_slots - total])

    xs = _dispatch(h, dest, starts + counts, gap_len, total, n_slots)
    ys = _experts(xs, group_starts, group_counts, w_gate_up, b_gate_up, w_down, b_down)
    return _combine(ys, dest, gates, h, ln2_g.reshape(1, d), ln2_b.reshape(1, d), alpha)


def kernel(x, w_in, b_forget, w_branch_a, w_branch_b, w_out, ln1_g, ln1_b, w_router, b_router,
           w_gate_up, b_gate_up, w_down, b_down, ln2_g, ln2_b):
    bsz, s, d = x.shape
    tables = _rope_tables(s)
    outs = []
    for bi in range(bsz):
        xb = x[bi]
        for l in range(DEPTH):
            xb = _layer(xb, w_in[l], b_forget[l], w_branch_a[l], w_branch_b[l], w_out[l], ln1_g[l], ln1_b[l],
                        w_router[l], b_router[l], w_gate_up[l], b_gate_up[l], w_down[l], b_down[l],
                        ln2_g[l], ln2_b[l], tables)
        outs.append(xb)
    return jnp.stack(outs)
```

```python
import functools

import numpy as np
import jax
import jax.numpy as jnp
from jax import lax
from jax.experimental import pallas as pl
from jax.experimental.pallas import tpu as pltpu

F32 = jnp.float32
BF16 = jnp.bfloat16
I32 = jnp.int32

D_MODEL = 2048
DEPTH = 1
CHUNK = 64
HEAD_DIM = 128
FOX_HEADS = 8
DSA_HEADS = 8
DSA_KV_HEADS = 2
DSA_GROUP = DSA_HEADS // DSA_KV_HEADS
IDX_HEADS = 16
IDX_DIM = 64
TOPK_MAX = 256
ROPE_THETA = 500000.0
ROT_FRACTION_DEN = 4
MIX_A = FOX_HEADS * HEAD_DIM
MIX_B = DSA_HEADS * HEAD_DIM
N_EXPERTS = 32
TOP_K = 4
D_FF = D_MODEL
SWIGLU_ALPHA = 1.702
SWIGLU_LIMIT = 7.0
LN_EPS = 1e-5

MIB = 1024 * 1024
NEG_BIG = -1e30
LOG2E = 1.4426950408889634
INT_MIN = -(2 ** 31)

COL_BQ = 0
COL_IQ = 1024
COL_AQ = 2048
COL_AK = 3072
COL_AV = 4096
COL_BK = 5120
COL_BV = 5376
COL_IK = 5632
N_PBF = 5888
PROJ_TN = 256
ROPE_HEAD_TILES = (0, 1, 2, 3, 20)
ROPE_IDX_TILES = (4, 5, 6, 7, 22)
COL_GA = 0
COL_GB = 2048
COL_SMALL = 4096
N_PG = 4224
PG_TN = 384

MOE_CHUNK = 256
MOE_ALIGN = 16
ROW_TILE = 8


def _cparams(dims, vmem_mib):
    return pltpu.CompilerParams(dimension_semantics=dims, vmem_limit_bytes=vmem_mib * MIB)


def _tile_in(j, tiles):
    cond = j == tiles[0]
    for t in tiles[1:]:
        cond = cond | (j == t)
    return cond


def _proj_rope_kernel(x_ref, w_ref, tab_ref, o_ref, xb_ref):
    j = pl.program_id(1)

    @pl.when(j == 0)
    def _():
        xb_ref[...] = x_ref[...].astype(BF16)

    acc = jnp.dot(xb_ref[...], w_ref[...], preferred_element_type=F32)
    tn = acc.shape[1]
    is_head = _tile_in(j, ROPE_HEAD_TILES)
    is_idx = _tile_in(j, ROPE_IDX_TILES)

    def rope(shift):
        c = tab_ref[0, 0]
        s_prev = tab_ref[0, 1]
        s_next = tab_ref[0, 2]
        out = acc * c + pltpu.roll(acc, shift, 1) * s_prev + pltpu.roll(acc, tn - shift, 1) * s_next
        o_ref[...] = out.astype(o_ref.dtype)

    @pl.when(is_head)
    def _():
        rope(HEAD_DIM // ROT_FRACTION_DEN // 2)

    @pl.when(is_idx)
    def _():
        rope(IDX_DIM // ROT_FRACTION_DEN // 2)

    @pl.when(jnp.logical_not(is_head | is_idx))
    def _():
        o_ref[...] = acc.astype(o_ref.dtype)


def _proj_plain_kernel(x_ref, w_ref, o_ref, xb_ref):
    j = pl.program_id(1)

    @pl.when(j == 0)
    def _():
        xb_ref[...] = x_ref[...].astype(BF16)

    o_ref[...] = jnp.dot(xb_ref[...], w_ref[...], preferred_element_type=F32).astype(o_ref.dtype)


def _rope_group(j):
    return jnp.where(_tile_in(j, ROPE_IDX_TILES), 1, 0)


def _project(x2d, w_bf, tables, w_g):
    s, d = x2d.shape
    tm = min(1024, s)
    pbf = pl.pallas_call(
        _proj_rope_kernel,
        grid=(s // tm, N_PBF // PROJ_TN),
        in_specs=[
            pl.BlockSpec((tm, d), lambda i, j: (i, 0)),
            pl.BlockSpec((d, PROJ_TN), lambda i, j: (0, j)),
            pl.BlockSpec((1, 3, tm, PROJ_TN), lambda i, j: (_rope_group(j), 0, i, 0)),
        ],
        out_specs=pl.BlockSpec((tm, PROJ_TN), lambda i, j: (i, j)),
        out_shape=jax.ShapeDtypeStruct((s, N_PBF), BF16),
        scratch_shapes=[pltpu.VMEM((tm, d), BF16)],
        compiler_params=_cparams(("arbitrary", "arbitrary"), 48),
        name="proj_bf16",
    )(x2d, w_bf, tables)
    pg = pl.pallas_call(
        _proj_plain_kernel,
        grid=(s // tm, N_PG // PG_TN),
        in_specs=[
            pl.BlockSpec((tm, d), lambda i, j: (i, 0)),
            pl.BlockSpec((d, PG_TN), lambda i, j: (0, j)),
        ],
        out_specs=pl.BlockSpec((tm, PG_TN), lambda i, j: (i, j)),
        out_shape=jax.ShapeDtypeStruct((s, N_PG), F32),
        scratch_shapes=[pltpu.VMEM((tm, d), BF16)],
        compiler_params=_cparams(("arbitrary", "arbitrary"), 48),
        name="proj_f32",
    )(x2d, w_g)
    return pbf, pg


def _rope_tables(s):
    pos = jnp.arange(s, dtype=F32)

    def one(period):
        rot = period // ROT_FRACTION_DEN
        half = rot // 2
        inv = jnp.power(ROPE_THETA, -jnp.arange(0, rot, 2, dtype=F32) / rot)
        ang = pos[:, None] * inv[None, :]
        cos, sin = jnp.cos(ang), jnp.sin(ang)
        zero = jnp.zeros((s, period - rot), F32)
        c = jnp.concatenate([cos, cos, jnp.ones((s, period - rot), F32)], axis=1)
        s_prev = jnp.concatenate([jnp.zeros((s, half), F32), sin, zero], axis=1)
        s_next = jnp.concatenate([-sin, jnp.zeros((s, half), F32), zero], axis=1)
        reps = PROJ_TN // period
        return jnp.stack([jnp.tile(c, (1, reps)), jnp.tile(s_prev, (1, reps)), jnp.tile(s_next, (1, reps))])

    return jnp.stack([one(HEAD_DIM), one(IDX_DIM)])


def _cumsum_kernel(af_ref, bf_ref, c_ref, carry_ref):
    i = pl.program_id(0)

    @pl.when(i == 0)
    def _():
        carry_ref[...] = jnp.zeros_like(carry_ref)

    z = af_ref[...] + bf_ref[...]
    logf = jnp.minimum(z, 0.0) - jnp.log1p(jnp.exp(-jnp.abs(z)))
    t = z.shape[1]
    row = lax.broadcasted_iota(I32, (t, t), 0)
    col = lax.broadcasted_iota(I32, (t, t), 1)
    upper = (row <= col).astype(F32)
    c = jnp.dot(logf, upper, preferred_element_type=F32, precision=lax.Precision.HIGHEST) + carry_ref[...]
    c_ref[...] = c * LOG2E
    carry_ref[...] = c[:, t - 1:t]


def _forget_cumsum(af_t, b_forget):
    h, s = af_t.shape
    t = min(512, s)
    return pl.pallas_call(
        _cumsum_kernel,
        grid=(s // t,),
        in_specs=[pl.BlockSpec((h, t), lambda i: (0, i)), pl.BlockSpec((h, 1), lambda i: (0, 0))],
        out_specs=pl.BlockSpec((h, t), lambda i: (0, i)),
        out_shape=jax.ShapeDtypeStruct((h, s), F32),
        scratch_shapes=[pltpu.VMEM((h, 1), F32)],
        compiler_params=_cparams(("arbitrary",), 32),
        name="forget_cumsum",
    )(af_t, b_forget.reshape(h, 1).astype(F32))


def _softmax_init(m_ref, l_ref, acc_ref):
    m_ref[...] = jnp.full(m_ref.shape, NEG_BIG, F32)
    l_ref[...] = jnp.zeros(l_ref.shape, F32)
    acc_ref[...] = jnp.zeros(acc_ref.shape, F32)


def _softmax_tile(s, v, m_ref, l_ref, acc_ref, i):
    reps = s.shape[1] // 128
    m_old = m_ref[i]
    m_new = jnp.maximum(m_old, jnp.max(s, axis=1, keepdims=True))
    p = jnp.exp2(s - jnp.tile(m_new, (1, reps)))
    alpha = jnp.exp2(m_old - m_new)
    psum = p[:, :128]
    for c in range(1, reps):
        psum = psum + p[:, c * 128:(c + 1) * 128]
    l_ref[i] = alpha * l_ref[i] + psum
    acc_ref[i] = alpha * acc_ref[i] + jnp.dot(p.astype(BF16), v, preferred_element_type=F32)
    m_ref[i] = m_new


def _softmax_result(l_ref, acc_ref, i):
    return acc_ref[i] / jnp.sum(l_ref[i], axis=1, keepdims=True)


def _fox_kernel(q_ref, k_ref, v_ref, c_ref, o_ref, m_ref, l_ref, acc_ref, *, tq, tk, nh):
    qi = pl.program_id(1)
    _softmax_init(m_ref, l_ref, acc_ref)

    def tile(kt, masked):
        start = pl.multiple_of(kt * tk, tk)
        if masked:
            row = qi * tq + lax.broadcasted_iota(I32, (tq, tk), 0)
            col = kt * tk + lax.broadcasted_iota(I32, (tq, tk), 1)
            causal = col <= row
        for h in range(nh):
            hs = slice(h * HEAD_DIM, (h + 1) * HEAD_DIM)
            k = k_ref[pl.ds(start, tk), hs]
            v = v_ref[pl.ds(start, tk), hs]
            s = lax.dot_general(q_ref[:, hs], k, (((1,), (1,)), ((), ())), preferred_element_type=F32)
            s = s - c_ref[h, kt]
            if masked:
                s = jnp.where(causal, s, NEG_BIG)
            _softmax_tile(s, v, m_ref, l_ref, acc_ref, h)

    n_full = (qi * tq) // tk

    def body(kt, _):
        tile(kt, False)
        return 0

    lax.fori_loop(0, n_full, body, 0)
    for t in range((tq + tk - 1) // tk):
        tile(n_full + t, True)
    for h in range(nh):
        o_ref[:, h * HEAD_DIM:(h + 1) * HEAD_DIM] = _softmax_result(l_ref, acc_ref, h).astype(o_ref.dtype)


def _fox_attention(pbf, c_tiles, s, tq, tk, nh):
    cb = nh * HEAD_DIM
    return pl.pallas_call(
        functools.partial(_fox_kernel, tq=tq, tk=tk, nh=nh),
        grid=(FOX_HEADS // nh, s // tq),
        in_specs=[
            pl.BlockSpec((tq, cb), lambda g, i: (i, COL_AQ // cb + g)),
            pl.BlockSpec((s, cb), lambda g, i: (0, COL_AK // cb + g)),
            pl.BlockSpec((s, cb), lambda g, i: (0, COL_AV // cb + g)),
            pl.BlockSpec((nh, s // tk, 1, tk), lambda g, i: (g, 0, 0, 0)),
        ],
        out_specs=pl.BlockSpec((tq, cb), lambda g, i: (i, g)),
        out_shape=jax.ShapeDtypeStruct((s, MIX_A), BF16),
        scratch_shapes=[
            pltpu.VMEM((nh, tq, 128), F32),
            pltpu.VMEM((nh, tq, 128), F32),
            pltpu.VMEM((nh, tq, HEAD_DIM), F32),
        ],
        compiler_params=_cparams(("arbitrary", "arbitrary"), 56),
        name="fox_attention",
    )(pbf, pbf, pbf, c_tiles)


def _dsa_kernel(bq_ref, iq_ref, iw_ref, ik_ref, bk_ref, bv_ref, o_ref,
                key_ref, qg_ref, m_ref, l_ref, acc_ref, *, tq, tk, n_sel):
    b = pl.program_id(0)
    n_tiles = (b * tq) // tk + 1
    row_g = b * tq + lax.broadcasted_iota(I32, (tq, tk), 0)
    adm_end = (row_g // CHUNK + 1) * CHUNK
    col_l = lax.broadcasted_iota(I32, (tq, tk), 1)

    idx_scale = (IDX_HEADS ** -0.5) * (IDX_DIM ** -0.5)
    iw = iw_ref[...][:, 8:8 + IDX_HEADS] * idx_scale
    iq = iq_ref[...]

    def score_tile(kt, _):
        start = pl.multiple_of(kt * tk, tk)
        ik = ik_ref[pl.ds(start, tk), :][:, :IDX_DIM]
        acc = jnp.zeros((tq, tk), F32)
        for h in range(IDX_HEADS):
            a = iq[:, h * IDX_DIM:(h + 1) * IDX_DIM]
            rel = lax.dot_general(a, ik, (((1,), (1,)), ((), ())), preferred_element_type=F32)
            acc = acc + jnp.maximum(rel, 0.0) * iw[:, h:h + 1]
        key_ref[kt] = jnp.where(kt * tk + col_l < adm_end, acc, -jnp.inf)
        return 0

    lax.fori_loop(0, n_tiles, score_tile, 0)

    def float_of(code_u):
        code = code_u ^ INT_MIN
        return pltpu.bitcast(jnp.where(code >= 0, code, code ^ 0x7FFFFFFF), F32)

    def count_ge(cand):
        def body(kt, part):
            ge = jnp.where(key_ref[kt] >= cand, 1, 0)
            for c in range(tk // 128):
                part = part + ge[:, c * 128:(c + 1) * 128]
            return part
        part = lax.fori_loop(0, n_tiles, body, jnp.zeros((tq, 128), I32))
        return jnp.sum(part, axis=1, keepdims=True)

    def search_cond(carry):
        i, _, _, pending = carry
        return (i < 32) & (pending > 0)

    def refine(i, t_u, hit):
        cand_u = t_u | lax.shift_left(jnp.int32(1), 31 - i)
        cnt = count_ge(float_of(cand_u))
        return jnp.where(cnt >= n_sel, cand_u, t_u), jnp.where(cnt == n_sel, 1, hit)

    def search_step(carry):
        i, t_u, hit, _ = carry
        t_u, hit = refine(i, t_u, hit)
        return i + 1, t_u, hit, jnp.sum(1 - hit)

    zeros = jnp.zeros((tq, 1), I32)
    lead = 12
    t_u, hit = lax.fori_loop(0, lead, lambda i, c: refine(i, *c), (zeros, zeros))
    _, t_u, _, _ = lax.while_loop(search_cond, search_step, (jnp.int32(lead), t_u, hit, jnp.sum(1 - hit)))
    thr = jnp.where(t_u == 0, jnp.finfo(F32).min, float_of(t_u))

    for j in range(DSA_KV_HEADS):
        for g in range(DSA_GROUP):
            hd = j * DSA_GROUP + g
            qg_ref[j, g * tq:(g + 1) * tq, :] = bq_ref[:, hd * HEAD_DIM:(hd + 1) * HEAD_DIM]
    _softmax_init(m_ref, l_ref, acc_ref)

    def attn_tile(kt, _):
        start = pl.multiple_of(kt * tk, tk)
        sel = key_ref[kt] >= thr
        for j in range(DSA_KV_HEADS):
            hs = slice(j * HEAD_DIM, (j + 1) * HEAD_DIM)
            k = bk_ref[pl.ds(start, tk), hs]
            v = bv_ref[pl.ds(start, tk), hs]
            s = lax.dot_general(qg_ref[j], k, (((1,), (1,)), ((), ())), preferred_element_type=F32)
            s = jnp.where(sel[None], s.reshape(DSA_GROUP, tq, tk), NEG_BIG).reshape(DSA_GROUP * tq, tk)
            _softmax_tile(s, v, m_ref, l_ref, acc_ref, j)
        return 0

    lax.fori_loop(0, n_tiles, attn_tile, 0)

    for j in range(DSA_KV_HEADS):
        o = _softmax_result(l_ref, acc_ref, j)
        for g in range(DSA_GROUP):
            hd = j * DSA_GROUP + g
            o_ref[:, hd * HEAD_DIM:(hd + 1) * HEAD_DIM] = o[g * tq:(g + 1) * tq].astype(o_ref.dtype)


def _dsa_attention(pbf, pg, s, tq, tk):
    n_sel = min(TOPK_MAX, s // 4)
    kvw = DSA_KV_HEADS * HEAD_DIM
    rows = DSA_GROUP * tq
    return pl.pallas_call(
        functools.partial(_dsa_kernel, tq=tq, tk=tk, n_sel=n_sel),
        grid=(s // tq,),
        in_specs=[
            pl.BlockSpec((tq, MIX_B), lambda b: (b, COL_BQ // MIX_B)),
            pl.BlockSpec((tq, IDX_HEADS * IDX_DIM), lambda b: (b, COL_IQ // (IDX_HEADS * IDX_DIM))),
            pl.BlockSpec((tq, 128), lambda b: (b, COL_SMALL // 128)),
            pl.BlockSpec((s, 128), lambda b: (0, COL_IK // 128)),
            pl.BlockSpec((s, kvw), lambda b: (0, COL_BK // kvw)),
            pl.BlockSpec((s, kvw), lambda b: (0, COL_BV // kvw)),
        ],
        out_specs=pl.BlockSpec((tq, MIX_B), lambda b: (b, 0)),
        out_shape=jax.ShapeDtypeStruct((s, MIX_B), BF16),
        scratch_shapes=[
            pltpu.VMEM((s // tk, tq, tk), F32),
            pltpu.VMEM((DSA_KV_HEADS, rows, HEAD_DIM), BF16),
            pltpu.VMEM((DSA_KV_HEADS, rows, 128), F32),
            pltpu.VMEM((DSA_KV_HEADS, rows, 128), F32),
            pltpu.VMEM((DSA_KV_HEADS, rows, HEAD_DIM), F32),
        ],
        compiler_params=_cparams(("arbitrary",), 56),
        name="dsa_attention",
    )(pbf, pbf, pg, pbf, pbf, pbf)


def _layer_norm(z, g, b):
    mu = jnp.mean(z, axis=-1, keepdims=True)
    zc = z - mu
    var = jnp.mean(zc * zc, axis=-1, keepdims=True)
    return zc * lax.rsqrt(var + LN_EPS) * g + b


def _merge_kernel(a_ref, b_ref, ga_ref, gb_ref, x_ref, wa_ref, wb_ref, wo_ref, g_ref, beta_ref,
                  wr_ref, br_ref, h_ref, eidx_ref, gate_ref, pos_ref, cnt_ref, carry_ref, *, alpha):
    i = pl.program_id(0)

    @pl.when(i == 0)
    def _():
        carry_ref[...] = jnp.zeros_like(carry_ref)

    ma = jnp.dot(a_ref[...], wa_ref[...], preferred_element_type=F32)
    mb = jnp.dot(b_ref[...], wb_ref[...], preferred_element_type=F32)
    merged = jax.nn.sigmoid(ga_ref[...]) * ma + jax.nn.sigmoid(gb_ref[...]) * mb
    y = jnp.dot(merged.astype(BF16), wo_ref[...], preferred_element_type=F32)
    h = _layer_norm(alpha * x_ref[...] + y, g_ref[...], beta_ref[...])
    h_ref[...] = h

    logits = jnp.dot(h, wr_ref[...], preferred_element_type=F32, precision=lax.Precision.HIGHEST) + br_ref[...]
    tm, ne = logits.shape
    lane = lax.broadcasted_iota(I32, (tm, ne), 1)
    lane_k = lax.broadcasted_iota(I32, (tm, TOP_K), 1)
    work = logits
    vals, sels = [], []
    eidx = jnp.zeros((tm, TOP_K), I32)
    onehot = jnp.zeros((tm, ne), F32)
    for k in range(TOP_K):
        mv = jnp.max(work, axis=1, keepdims=True)
        idx = jnp.min(jnp.where(work == mv, lane, ne), axis=1, keepdims=True)
        sel = lane == idx
        vals.append(mv)
        sels.append(sel)
        eidx = jnp.where(lane_k == k, idx, eidx)
        onehot = onehot + jnp.where(sel, 1.0, 0.0)
        work = jnp.where(sel, -jnp.inf, work)
    exps = [jnp.exp(v - vals[0]) for v in vals]
    denom = exps[0] + exps[1] + exps[2] + exps[3]
    gates = jnp.zeros((tm, TOP_K), F32)
    for k in range(TOP_K):
        gates = jnp.where(lane_k == k, exps[k] / denom, gates)

    r_i = lax.broadcasted_iota(I32, (tm, tm), 0)
    c_i = lax.broadcasted_iota(I32, (tm, tm), 1)
    lower = jnp.where(c_i < r_i, 1.0, 0.0).astype(BF16)
    rank = carry_ref[...] + jnp.dot(lower, onehot.astype(BF16), preferred_element_type=F32)
    pos = jnp.zeros((tm, TOP_K), I32)
    for k in range(TOP_K):
        pk = jnp.sum(jnp.where(sels[k], rank, 0.0), axis=1, keepdims=True).astype(I32)
        pos = jnp.where(lane_k == k, pk, pos)
    carry_ref[...] = carry_ref[...] + jnp.sum(onehot, axis=0, keepdims=True)

    eidx_ref[...] = eidx
    gate_ref[...] = gates
    pos_ref[...] = pos
    cnt_ref[...] = carry_ref[...].astype(I32)


def _merge(a_out, b_out, pg, x2d, wa, wb, wo, ln_g, ln_b, w_router, b_router, alpha):
    s, d = x2d.shape
    tm = min(256, s)
    full = lambda shape: pl.BlockSpec(shape, lambda i: (0,) * len(shape))
    return pl.pallas_call(
        functools.partial(_merge_kernel, alpha=alpha),
        grid=(s // tm,),
        in_specs=[
            pl.BlockSpec((tm, MIX_A), lambda i: (i, 0)),
            pl.BlockSpec((tm, MIX_B), lambda i: (i, 0)),
            pl.BlockSpec((tm, d), lambda i: (i, COL_GA // d)),
            pl.BlockSpec((tm, d), lambda i: (i, COL_GB // d)),
            pl.BlockSpec((tm, d), lambda i: (i, 0)),
            full((MIX_A, d)), full((MIX_B, d)), full((d, d)),
            full((1, d)), full((1, d)), full((d, N_EXPERTS)), full((1, N_EXPERTS)),
        ],
        out_specs=[
            pl.BlockSpec((tm, d), lambda i: (i, 0)),
            pl.BlockSpec((tm, TOP_K), lambda i: (i, 0)),
            pl.BlockSpec((tm, TOP_K), lambda i: (i, 0)),
            pl.BlockSpec((tm, TOP_K), lambda i: (i, 0)),
            full((1, N_EXPERTS)),
        ],
        out_shape=[
            jax.ShapeDtypeStruct((s, d), F32),
            jax.ShapeDtypeStruct((s, TOP_K), I32),
            jax.ShapeDtypeStruct((s, TOP_K), F32),
            jax.ShapeDtypeStruct((s, TOP_K), I32),
            jax.ShapeDtypeStruct((1, N_EXPERTS), I32),
        ],
        scratch_shapes=[pltpu.VMEM((1, N_EXPERTS), F32)],
        compiler_params=_cparams(("arbitrary",), 56),
        name="merge_ln_router",
    )(a_out, b_out, pg, pg, x2d, wa, wb, wo, ln_g, ln_b, w_router, b_router)


def _dispatch_kernel(dest_ref, gap_start_ref, gap_len_ref, total_ref, h_ref, xs_ref, stage_ref, zero_ref,
                     sem, zsem, *, max_slack):
    i = pl.program_id(0)
    last = pl.num_programs(0) - 1
    slot = lax.rem(i, 2)
    n = dest_ref.shape[0]

    @pl.when(i == 0)
    def _():
        zero_ref[...] = jnp.zeros_like(zero_ref)

        def gap_copy(e, k):
            return pltpu.make_async_copy(zero_ref.at[pl.ds(0, 1)], xs_ref.at[pl.ds(gap_start_ref[e] + k, 1)], zsem)

        n_slots = xs_ref.shape[0]

        def slack_copy(k):
            row = pl.multiple_of(total_ref[0] + k * MOE_CHUNK, MOE_CHUNK)
            return pltpu.make_async_copy(zero_ref, xs_ref.at[pl.ds(row, MOE_CHUNK)], zsem)

        def slack(action):
            for k in range(max_slack):
                @pl.when(total_ref[0] + k * MOE_CHUNK < n_slots)
                def _():
                    action(slack_copy(k))

        def issue(e, _):
            lax.fori_loop(0, gap_len_ref[e], lambda k, c: (gap_copy(e, k).start(), c)[1], 0)
            return 0

        def settle(e, _):
            lax.fori_loop(0, gap_len_ref[e], lambda k, c: (gap_copy(e, k).wait(), c)[1], 0)
            return 0

        lax.fori_loop(0, N_EXPERTS, issue, 0)
        slack(lambda cp: cp.start())
        lax.fori_loop(0, N_EXPERTS, settle, 0)
        slack(lambda cp: cp.wait())

    def drain(sl):
        pltpu.make_async_copy(xs_ref.at[pl.ds(0, n)], xs_ref.at[pl.ds(0, n)], sem.at[sl]).wait()

    @pl.when(i >= 2)
    def _():
        drain(slot)

    stage_ref[slot] = h_ref[...]

    def scatter(sl):
        def start(g, _):
            base = pl.multiple_of(g * ROW_TILE, ROW_TILE)
            for u in range(ROW_TILE * TOP_K):
                r = u // TOP_K
                pltpu.make_async_copy(stage_ref.at[sl, pl.ds(base + r, 1)],
                                      xs_ref.at[pl.ds(dest_ref[g * (ROW_TILE * TOP_K) + u], 1)],
                                      sem.at[sl]).start(priority=u % 2)
            return 0
        lax.fori_loop(0, n // (ROW_TILE * TOP_K), start, 0)

    for parity in range(2):
        @pl.when(slot == parity)
        def _():
            scatter(parity)

    @pl.when(i == last)
    def _():
        drain(slot)

        @pl.when(i >= 1)
        def _():
            drain(1 - slot)


def _dispatch(h, dest_flat, gap_start, gap_len, total, n_slots):
    s, d = h.shape
    tm = min(128, s)
    smem = pl.BlockSpec(memory_space=pltpu.SMEM)
    max_slack = (n_slots - s * TOP_K + MOE_CHUNK - 1) // MOE_CHUNK
    return pl.pallas_call(
        functools.partial(_dispatch_kernel, max_slack=max_slack),
        grid=(s // tm,),
        in_specs=[
            pl.BlockSpec((tm * TOP_K,), lambda i: (i,), memory_space=pltpu.SMEM),
            smem, smem, smem,
            pl.BlockSpec((tm, d), lambda i: (i, 0)),
        ],
        out_specs=pl.BlockSpec(memory_space=pl.ANY),
        out_shape=jax.ShapeDtypeStruct((n_slots, d), h.dtype),
        scratch_shapes=[pltpu.VMEM((2, tm, d), h.dtype), pltpu.VMEM((MOE_CHUNK, d), h.dtype),
                        pltpu.SemaphoreType.DMA((2,)), pltpu.SemaphoreType.DMA(())],
        compiler_params=_cparams(("arbitrary",), 32),
        name="moe_dispatch",
    )(dest_flat, gap_start, gap_len, total, h)


ROW_DMA_PRIORITY = 1


def _expert_rows_pipeline(n_chunks, prologue, in_copy, out_copy, compute):
    @pl.when(n_chunks > 0)
    def _():
        in_copy(0, 0).start(priority=ROW_DMA_PRIORITY)
        prologue()

        def body(c, _):
            slot = lax.rem(c, 2)
            in_copy(c, slot).wait()

            @pl.when(c + 1 < n_chunks)
            def _():
                in_copy(c + 1, 1 - slot).start(priority=ROW_DMA_PRIORITY)

            @pl.when(c >= 2)
            def _():
                out_copy(c - 2, slot).wait()

            compute(slot)
            out_copy(c, slot).start(priority=ROW_DMA_PRIORITY)
            return 0

        lax.fori_loop(0, n_chunks, body, 0)

        @pl.when(n_chunks >= 2)
        def _():
            out_copy(n_chunks - 2, lax.rem(n_chunks, 2)).wait()

        out_copy(n_chunks - 1, lax.rem(n_chunks - 1, 2)).wait()


def _chunk_rows(start_ref, e, c, n_slots):
    del n_slots
    return pl.ds(pl.multiple_of(start_ref[e] + c * MOE_CHUNK, MOE_ALIGN), MOE_CHUNK)


def _gate_up_kernel(start_ref, cnt_ref, x_hbm, wg_ref, wu_ref, bg_ref, bu_ref, act_hbm,
                    wg_sc, wu_sc, xbuf, obuf, sem_in, sem_out):
    j = pl.program_id(0)
    e = pl.program_id(1)
    n_slots = x_hbm.shape[0]
    n_chunks = (cnt_ref[e] + MOE_CHUNK - 1) // MOE_CHUNK

    def cast_weights():
        wg_sc[...] = wg_ref[...].astype(BF16)
        wu_sc[...] = wu_ref[...].astype(BF16)

    def in_copy(c, slot):
        return pltpu.make_async_copy(x_hbm.at[_chunk_rows(start_ref, e, c, n_slots)], xbuf.at[slot], sem_in.at[slot])

    def out_copy(c, slot):
        return pltpu.make_async_copy(obuf.at[slot], act_hbm.at[j, _chunk_rows(start_ref, e, c, n_slots)],
                                     sem_out.at[slot])

    def compute(slot):
        x = xbuf[slot].astype(BF16)
        cw = 256
        for c in range(obuf.shape[2] // cw):
            cs = slice(c * cw, (c + 1) * cw)
            g = jnp.dot(x, wg_sc[:, cs], preferred_element_type=F32) + bg_ref[:, cs]
            u = jnp.dot(x, wu_sc[:, cs], preferred_element_type=F32) + bu_ref[:, cs]
            g = jnp.minimum(g, SWIGLU_LIMIT)
            u = jnp.clip(u, -SWIGLU_LIMIT, SWIGLU_LIMIT)
            obuf[slot, :, cs] = (g * jax.nn.sigmoid(SWIGLU_ALPHA * g) * (u + 1.0)).astype(obuf.dtype)

    _expert_rows_pipeline(n_chunks, cast_weights, in_copy, out_copy, compute)


def _down_kernel(start_ref, cnt_ref, a_hbm, wd_ref, bd_ref, y_hbm, wd_sc, abuf, obuf, sem_in, sem_out):
    e = pl.program_id(0)
    n_slots = y_hbm.shape[0]
    n_chunks = (cnt_ref[e] + MOE_CHUNK - 1) // MOE_CHUNK
    nf, _, tf = abuf.shape[1:]

    def cast_weights():
        wd_sc[...] = wd_ref[...].astype(BF16)

    def in_copy(c, slot):
        return pltpu.make_async_copy(a_hbm.at[:, _chunk_rows(start_ref, e, c, n_slots)], abuf.at[slot],
                                     sem_in.at[slot])

    def out_copy(c, slot):
        return pltpu.make_async_copy(obuf.at[slot], y_hbm.at[_chunk_rows(start_ref, e, c, n_slots)],
                                     sem_out.at[slot])

    def compute(slot):
        cw = 512
        for c in range(obuf.shape[2] // cw):
            cs = slice(c * cw, (c + 1) * cw)
            y = bd_ref[:, cs]
            for k in range(nf):
                y = y + jnp.dot(abuf[slot, k], wd_sc[k * tf:(k + 1) * tf, cs], preferred_element_type=F32)
            obuf[slot, :, cs] = y

    _expert_rows_pipeline(n_chunks, cast_weights, in_copy, out_copy, compute)


def _experts(xs, starts, counts, w_gate_up, b_gate_up, w_down, b_down):
    n_slots, d = xs.shape
    tf = 1024
    nf = D_FF // tf
    any_spec = pl.BlockSpec(memory_space=pl.ANY)
    dma_sems = pltpu.SemaphoreType.DMA((2,))
    bgu = b_gate_up.reshape(N_EXPERTS, 1, 2 * D_FF)
    n_groups = starts.shape[0]
    wi = lambda e: jnp.minimum(e, N_EXPERTS - 1)
    act = pl.pallas_call(
        _gate_up_kernel,
        grid_spec=pltpu.PrefetchScalarGridSpec(
            num_scalar_prefetch=2,
            grid=(nf, n_groups),
            in_specs=[
                any_spec,
                pl.BlockSpec((None, d, tf), lambda j, e, st, ct: (wi(e), 0, j)),
                pl.BlockSpec((None, d, tf), lambda j, e, st, ct: (wi(e), 0, nf + j)),
                pl.BlockSpec((None, 1, tf), lambda j, e, st, ct: (wi(e), 0, j)),
                pl.BlockSpec((None, 1, tf), lambda j, e, st, ct: (wi(e), 0, nf + j)),
            ],
            out_specs=any_spec,
            scratch_shapes=[pltpu.VMEM((d, tf), BF16), pltpu.VMEM((d, tf), BF16),
                            pltpu.VMEM((2, MOE_CHUNK, d), xs.dtype), pltpu.VMEM((2, MOE_CHUNK, tf), BF16),
                            dma_sems, dma_sems],
        ),
        out_shape=jax.ShapeDtypeStruct((nf, n_slots, tf), BF16),
        compiler_params=_cparams(("arbitrary", "arbitrary"), 56),
        name="moe_gate_up",
    )(starts, counts, xs, w_gate_up, w_gate_up, bgu, bgu)
    bd = b_down.reshape(N_EXPERTS, 1, d)
    ys = pl.pallas_call(
        _down_kernel,
        grid_spec=pltpu.PrefetchScalarGridSpec(
            num_scalar_prefetch=2,
            grid=(n_groups,),
            in_specs=[
                any_spec,
                pl.BlockSpec((None, D_FF, d), lambda e, st, ct: (wi(e), 0, 0)),
                pl.BlockSpec((None, 1, d), lambda e, st, ct: (wi(e), 0, 0)),
            ],
            out_specs=any_spec,
            scratch_shapes=[pltpu.VMEM((D_FF, d), BF16),
                            pltpu.VMEM((2, nf, MOE_CHUNK, tf), BF16), pltpu.VMEM((2, MOE_CHUNK, d), F32),
                            dma_sems, dma_sems],
        ),
        out_shape=jax.ShapeDtypeStruct((n_slots, d), F32),
        compiler_params=_cparams(("arbitrary",), 56),
        name="moe_down",
    )(starts, counts, act, w_down, bd)
    return ys


def _combine_kernel(dest_ref, dest_next_ref, gate_ref, h_ref, g_ref, beta_ref, ys_ref, o_ref, buf_ref, sem,
                    *, alpha):
    i = pl.program_id(0)
    last = pl.num_programs(0) - 1
    slot = lax.rem(i, 2)
    n = dest_ref.shape[0]

    def gather(dref, sl):
        def start(g, _):
            base = pl.multiple_of(g * ROW_TILE, ROW_TILE)
            for u in range(ROW_TILE * TOP_K):
                r, k = divmod(u, TOP_K)
                pltpu.make_async_copy(ys_ref.at[pl.ds(dref[g * (ROW_TILE * TOP_K) + u], 1)],
                                      buf_ref.at[sl, k, pl.ds(base + r, 1)],
                                      sem.at[sl]).start(priority=u % 2)
            return 0
        lax.fori_loop(0, n // (ROW_TILE * TOP_K), start, 0)

    for parity in range(2):
        @pl.when((i == 0) & (slot == parity))
        def _():
            gather(dest_ref, parity)

        @pl.when((i < last) & (slot == parity))
        def _():
            gather(dest_next_ref, 1 - parity)

    pltpu.make_async_copy(buf_ref.at[slot], buf_ref.at[slot], sem.at[slot]).wait()
    gates = gate_ref[...]
    y = gates[:, 0:1] * buf_ref[slot, 0]
    for k in range(1, TOP_K):
        y = y + gates[:, k:k + 1] * buf_ref[slot, k]
    o_ref[...] = _layer_norm(alpha * h_ref[...] + y, g_ref[...], beta_ref[...])


def _combine(ys, dest_flat, gates, h, ln_g, ln_b, alpha):
    s, d = h.shape
    tm = min(128, s)
    n_steps = s // tm
    return pl.pallas_call(
        functools.partial(_combine_kernel, alpha=alpha),
        grid=(n_steps,),
        in_specs=[
            pl.BlockSpec((tm * TOP_K,), lambda i: (i,), memory_space=pltpu.SMEM),
            pl.BlockSpec((tm * TOP_K,), lambda i: (jnp.minimum(i + 1, n_steps - 1),), memory_space=pltpu.SMEM),
            pl.BlockSpec((tm, TOP_K), lambda i: (i, 0)),
            pl.BlockSpec((tm, d), lambda i: (i, 0)),
            pl.BlockSpec((1, d), lambda i: (0, 0)),
            pl.BlockSpec((1, d), lambda i: (0, 0)),
            pl.BlockSpec(memory_space=pl.ANY),
        ],
        out_specs=pl.BlockSpec((tm, d), lambda i: (i, 0)),
        out_shape=jax.ShapeDtypeStruct((s, d), F32),
        scratch_shapes=[pltpu.VMEM((2, TOP_K, tm, d), F32), pltpu.SemaphoreType.DMA((2,))],
        compiler_params=_cparams(("arbitrary",), 32),
        name="moe_combine",
    )(dest_flat, dest_flat, gates, h, ln_g, ln_b, ys)


def _layer(x2d, w_in, b_forget, w_branch_a, w_branch_b, w_out, ln1_g, ln1_b, w_router, b_router,
           w_gate_up, b_gate_up, w_down, b_down, ln2_g, ln2_b, tables):
    s, d = x2d.shape
    alpha = (2.0 * DEPTH) ** 0.25
    scale = HEAD_DIM ** -0.5 * LOG2E

    sizes = (MIX_A, MIX_A, MIX_A, FOX_HEADS, MIX_B, DSA_KV_HEADS * HEAD_DIM, DSA_KV_HEADS * HEAD_DIM,
             IDX_HEADS * IDX_DIM, IDX_DIM, IDX_HEADS, D_MODEL, D_MODEL)
    w_aq, w_ak, w_av, w_af, w_bq, w_bk, w_bv, w_iq, w_ik, w_iw, w_ga, w_gb = jnp.split(
        w_in, np.cumsum(sizes)[:-1].tolist(), axis=1)
    w_bf = jnp.concatenate(
        [w_bq * scale, w_iq, w_aq * scale, w_ak, w_av, w_bk, w_bv, w_ik,
         jnp.zeros((d, N_PBF - COL_IK - IDX_DIM), F32)], axis=1).astype(BF16)
    w_g = jnp.concatenate(
        [w_ga, w_gb, w_af, w_iw, jnp.zeros((d, N_PG - COL_SMALL - FOX_HEADS - IDX_HEADS), F32)],
        axis=1).astype(BF16)

    pbf, pg = _project(x2d, w_bf, tables, w_g)

    fox_tq = min(512, s)
    fox_tk = min(512, s)
    af_t = pg[:, COL_SMALL:COL_SMALL + FOX_HEADS].T
    c_t = _forget_cumsum(af_t, b_forget)
    c_tiles = c_t.reshape(FOX_HEADS, s // fox_tk, 1, fox_tk)
    a_out = _fox_attention(pbf, c_tiles, s, fox_tq, fox_tk, 4)

    b_out = _dsa_attention(pbf, pg, s, 128, min(512, s))

    h, eidx, gates, pos, counts = _merge(
        a_out, b_out, pg, x2d, w_branch_a.astype(BF16), w_branch_b.astype(BF16), w_out.astype(BF16),
        ln1_g.reshape(1, d), ln1_b.reshape(1, d), w_router, b_router.reshape(1, N_EXPERTS), alpha)

    counts = counts.reshape(N_EXPERTS).astype(I32)
    aligned = (counts + MOE_ALIGN - 1) // MOE_ALIGN * MOE_ALIGN
    ends = jnp.cumsum(aligned).astype(I32)
    starts = ends - aligned
    bound = s * TOP_K + N_EXPERTS * MOE_ALIGN + 2 * MOE_CHUNK
    n_slots = (bound + MOE_CHUNK - 1) // MOE_CHUNK * MOE_CHUNK
    experts = jnp.arange(N_EXPERTS, dtype=I32)
    start_of = jnp.sum(jnp.where(eidx[..., None] == experts, starts, 0), axis=-1)
    dest = (start_of + pos).reshape(s * TOP_K).astype(I32)
    total = (ends[-1:] + MOE_CHUNK - 1) // MOE_CHUNK * MOE_CHUNK
    gap_len = aligned - counts + jnp.where(experts == N_EXPERTS - 1, total[0] - ends[-1], 0)
    group_starts = jnp.concatenate([starts, total])
    group_counts = jnp.concatenate([counts + jnp.where(experts == N_EXPERTS - 1, gap_len, 0), n_slots - total])

    xs = _dispatch(h, dest, starts + counts, gap_len, total, n_slots)
    ys = _experts(xs, group_starts, group_counts, w_gate_up, b_gate_up, w_down, b_down)
    return _combine(ys, dest, gates, h, ln2_g.reshape(1, d), ln2_b.reshape(1, d), alpha)


def kernel(x, w_in, b_forget, w_branch_a, w_branch_b, w_out, ln1_g, ln1_b, w_router, b_router,
           w_gate_up, b_gate_up, w_down, b_down, ln2_g, ln2_b):
    bsz, s, d = x.shape
    tables = _rope_tables(s)
    outs = []
    for bi in range(bsz):
        xb = x[bi]
        for l in range(DEPTH):
            xb = _layer(xb, w_in[l], b_forget[l], w_branch_a[l], w_branch_b[l], w_out[l], ln1_g[l], ln1_b[l],
                        w_router[l], b_router[l], w_gate_up[l], b_gate_up[l], w_down[l], b_down[l],
                        ln2_g[l], ln2_b[l], tables)
        outs.append(xb)
    return outs[0][None] if bsz == 1 else jnp.stack(outs)
```

```python
import functools

import numpy as np
import jax
import jax.numpy as jnp
from jax import lax
from jax.experimental import pallas as pl
from jax.experimental.pallas import tpu as pltpu

F32 = jnp.float32
BF16 = jnp.bfloat16
I32 = jnp.int32

D_MODEL = 2048
DEPTH = 1
CHUNK = 64
HEAD_DIM = 128
FOX_HEADS = 8
DSA_HEADS = 8
DSA_KV_HEADS = 2
DSA_GROUP = DSA_HEADS // DSA_KV_HEADS
IDX_HEADS = 16
IDX_DIM = 64
TOPK_MAX = 256
ROPE_THETA = 500000.0
ROT_FRACTION_DEN = 4
MIX_A = FOX_HEADS * HEAD_DIM
MIX_B = DSA_HEADS * HEAD_DIM
N_EXPERTS = 32
TOP_K = 4
D_FF = D_MODEL
SWIGLU_ALPHA = 1.702
SWIGLU_LIMIT = 7.0
LN_EPS = 1e-5

MIB = 1024 * 1024
NEG_BIG = -1e30
LOG2E = 1.4426950408889634
INT_MIN = -(2 ** 31)

COL_AQ = 0
COL_AK = 1024
COL_AV = 2048
COL_BQ = 3072
COL_BK = 4096
COL_BV = 4352
COL_IQ = 4608
COL_IK = 5632
N_PBF = 5888
PROJ_TN = 256
ROPE_HEAD_TILES = (12, 13, 14, 15, 16)
ROPE_IDX_TILES = (18, 19, 20, 21, 22)
COL_GA = 0
COL_GB = 2048
COL_SMALL = 4096
N_PG = 4224
PG_TN = 384

MOE_CHUNK = 256
MOE_ALIGN = 16
ROW_TILE = 8


def _cparams(dims, vmem_mib):
    return pltpu.CompilerParams(dimension_semantics=dims, vmem_limit_bytes=vmem_mib * MIB)


def _tile_in(j, tiles):
    cond = j == tiles[0]
    for t in tiles[1:]:
        cond = cond | (j == t)
    return cond


def _proj_rope_kernel(x_ref, w_ref, tab_ref, o_ref, xb_ref):
    j = pl.program_id(1)

    @pl.when(j == 0)
    def _():
        xb_ref[...] = x_ref[...].astype(BF16)

    acc = jnp.dot(xb_ref[...], w_ref[...], preferred_element_type=F32)
    tn = acc.shape[1]
    is_head = _tile_in(j, ROPE_HEAD_TILES)
    is_idx = _tile_in(j, ROPE_IDX_TILES)

    def rope(shift):
        c = tab_ref[0, 0]
        s_prev = tab_ref[0, 1]
        s_next = tab_ref[0, 2]
        out = acc * c + pltpu.roll(acc, shift, 1) * s_prev + pltpu.roll(acc, tn - shift, 1) * s_next
        o_ref[...] = out.astype(o_ref.dtype)

    @pl.when(is_head)
    def _():
        rope(HEAD_DIM // ROT_FRACTION_DEN // 2)

    @pl.when(is_idx)
    def _():
        rope(IDX_DIM // ROT_FRACTION_DEN // 2)

    @pl.when(jnp.logical_not(is_head | is_idx))
    def _():
        o_ref[...] = acc.astype(o_ref.dtype)


def _proj_plain_kernel(x_ref, w_ref, o_ref, xb_ref):
    j = pl.program_id(1)

    @pl.when(j == 0)
    def _():
        xb_ref[...] = x_ref[...].astype(BF16)

    o_ref[...] = jnp.dot(xb_ref[...], w_ref[...], preferred_element_type=F32).astype(o_ref.dtype)


def _rope_group(j):
    return jnp.where(_tile_in(j, ROPE_IDX_TILES), 1, 0)


def _project(x2d, w_bf, tables, w_g):
    s, d = x2d.shape
    tm = min(1024, s)
    pbf = pl.pallas_call(
        _proj_rope_kernel,
        grid=(s // tm, N_PBF // PROJ_TN),
        in_specs=[
            pl.BlockSpec((tm, d), lambda i, j: (i, 0)),
            pl.BlockSpec((d, PROJ_TN), lambda i, j: (0, j)),
            pl.BlockSpec((1, 3, tm, PROJ_TN), lambda i, j: (_rope_group(j), 0, i, 0)),
        ],
        out_specs=pl.BlockSpec((tm, PROJ_TN), lambda i, j: (i, j)),
        out_shape=jax.ShapeDtypeStruct((s, N_PBF), BF16),
        scratch_shapes=[pltpu.VMEM((tm, d), BF16)],
        compiler_params=_cparams(("arbitrary", "arbitrary"), 48),
        name="proj_bf16",
    )(x2d, w_bf, tables)
    pg = pl.pallas_call(
        _proj_plain_kernel,
        grid=(s // tm, N_PG // PG_TN),
        in_specs=[
            pl.BlockSpec((tm, d), lambda i, j: (i, 0)),
            pl.BlockSpec((d, PG_TN), lambda i, j: (0, j)),
        ],
        out_specs=pl.BlockSpec((tm, PG_TN), lambda i, j: (i, j)),
        out_shape=jax.ShapeDtypeStruct((s, N_PG), F32),
        scratch_shapes=[pltpu.VMEM((tm, d), BF16)],
        compiler_params=_cparams(("arbitrary", "arbitrary"), 48),
        name="proj_f32",
    )(x2d, w_g)
    return pbf, pg


def _rope_tables(s):
    pos = jnp.arange(s, dtype=F32)

    def one(period):
        rot = period // ROT_FRACTION_DEN
        half = rot // 2
        inv = jnp.power(ROPE_THETA, -jnp.arange(0, rot, 2, dtype=F32) / rot)
        ang = pos[:, None] * inv[None, :]
        cos, sin = jnp.cos(ang), jnp.sin(ang)
        zero = jnp.zeros((s, period - rot), F32)
        c = jnp.concatenate([cos, cos, jnp.ones((s, period - rot), F32)], axis=1)
        s_prev = jnp.concatenate([jnp.zeros((s, half), F32), sin, zero], axis=1)
        s_next = jnp.concatenate([-sin, jnp.zeros((s, half), F32), zero], axis=1)
        reps = PROJ_TN // period
        return jnp.stack([jnp.tile(c, (1, reps)), jnp.tile(s_prev, (1, reps)), jnp.tile(s_next, (1, reps))])

    return jnp.stack([one(HEAD_DIM), one(IDX_DIM)])


def _cumsum_kernel(af_ref, bf_ref, c_ref, carry_ref):
    i = pl.program_id(0)

    @pl.when(i == 0)
    def _():
        carry_ref[...] = jnp.zeros_like(carry_ref)

    z = af_ref[...] + bf_ref[...]
    logf = jnp.minimum(z, 0.0) - jnp.log1p(jnp.exp(-jnp.abs(z)))
    t = z.shape[1]
    row = lax.broadcasted_iota(I32, (t, t), 0)
    col = lax.broadcasted_iota(I32, (t, t), 1)
    upper = (row <= col).astype(F32)
    c = jnp.dot(logf, upper, preferred_element_type=F32, precision=lax.Precision.HIGHEST) + carry_ref[...]
    c_ref[...] = c * LOG2E
    carry_ref[...] = c[:, t - 1:t]


def _forget_cumsum(af_t, b_forget):
    h, s = af_t.shape
    t = min(512, s)
    return pl.pallas_call(
        _cumsum_kernel,
        grid=(s // t,),
        in_specs=[pl.BlockSpec((h, t), lambda i: (0, i)), pl.BlockSpec((h, 1), lambda i: (0, 0))],
        out_specs=pl.BlockSpec((h, t), lambda i: (0, i)),
        out_shape=jax.ShapeDtypeStruct((h, s), F32),
        scratch_shapes=[pltpu.VMEM((h, 1), F32)],
        compiler_params=_cparams(("arbitrary",), 32),
        name="forget_cumsum",
    )(af_t, b_forget.reshape(h, 1).astype(F32))


def _softmax_init(m_ref, l_ref, acc_ref):
    m_ref[...] = jnp.full(m_ref.shape, NEG_BIG, F32)
    l_ref[...] = jnp.zeros(l_ref.shape, F32)
    acc_ref[...] = jnp.zeros(acc_ref.shape, F32)


def _softmax_tile(s, v, m_ref, l_ref, acc_ref, i):
    reps = s.shape[1] // 128
    m_old = m_ref[i]
    m_new = jnp.maximum(m_old, jnp.max(s, axis=1, keepdims=True))
    p = jnp.exp2(s - jnp.tile(m_new, (1, reps)))
    alpha = jnp.exp2(m_old - m_new)
    psum = p[:, :128]
    for c in range(1, reps):
        psum = psum + p[:, c * 128:(c + 1) * 128]
    l_ref[i] = alpha * l_ref[i] + psum
    acc_ref[i] = alpha * acc_ref[i] + jnp.dot(p.astype(BF16), v, preferred_element_type=F32)
    m_ref[i] = m_new


def _softmax_result(l_ref, acc_ref, i):
    return acc_ref[i] / jnp.sum(l_ref[i], axis=1, keepdims=True)


def _fox_kernel(q_ref, k_ref, v_ref, c_ref, o_ref, m_ref, l_ref, acc_ref, *, tq, tk, nh):
    qi = pl.program_id(1)
    _softmax_init(m_ref, l_ref, acc_ref)

    def tile(kt, masked):
        start = pl.multiple_of(kt * tk, tk)
        if masked:
            row = qi * tq + lax.broadcasted_iota(I32, (tq, tk), 0)
            col = kt * tk + lax.broadcasted_iota(I32, (tq, tk), 1)
            causal = col <= row
        for h in range(nh):
            hs = slice(h * HEAD_DIM, (h + 1) * HEAD_DIM)
            k = k_ref[pl.ds(start, tk), hs]
            v = v_ref[pl.ds(start, tk), hs]
            s = lax.dot_general(q_ref[:, hs], k, (((1,), (1,)), ((), ())), preferred_element_type=F32)
            s = s - c_ref[h, kt]
            if masked:
                s = jnp.where(causal, s, NEG_BIG)
            _softmax_tile(s, v, m_ref, l_ref, acc_ref, h)

    n_full = (qi * tq) // tk

    def body(kt, _):
        tile(kt, False)
        return 0

    lax.fori_loop(0, n_full, body, 0)
    for t in range((tq + tk - 1) // tk):
        tile(n_full + t, True)
    for h in range(nh):
        o_ref[:, h * HEAD_DIM:(h + 1) * HEAD_DIM] = _softmax_result(l_ref, acc_ref, h).astype(o_ref.dtype)


def _fox_attention(pbf, c_tiles, s, tq, tk, nh):
    cb = nh * HEAD_DIM
    return pl.pallas_call(
        functools.partial(_fox_kernel, tq=tq, tk=tk, nh=nh),
        grid=(FOX_HEADS // nh, s // tq),
        in_specs=[
            pl.BlockSpec((tq, cb), lambda g, i: (i, COL_AQ // cb + g)),
            pl.BlockSpec((s, cb), lambda g, i: (0, COL_AK // cb + g)),
            pl.BlockSpec((s, cb), lambda g, i: (0, COL_AV // cb + g)),
            pl.BlockSpec((nh, s // tk, 1, tk), lambda g, i: (g, 0, 0, 0)),
        ],
        out_specs=pl.BlockSpec((tq, cb), lambda g, i: (i, g)),
        out_shape=jax.ShapeDtypeStruct((s, MIX_A), BF16),
        scratch_shapes=[
            pltpu.VMEM((nh, tq, 128), F32),
            pltpu.VMEM((nh, tq, 128), F32),
            pltpu.VMEM((nh, tq, HEAD_DIM), F32),
        ],
        compiler_params=_cparams(("arbitrary", "arbitrary"), 56),
        name="fox_attention",
    )(pbf, pbf, pbf, c_tiles)


def _dsa_kernel(bq_ref, iq_lo_ref, iq_hi_ref, iw_ref, ik_ref, bk_ref, bv_ref, o_ref,
                key_ref, qg_ref, m_ref, l_ref, acc_ref, *, tq, tk, n_sel):
    b = pl.program_id(0)
    n_tiles = (b * tq) // tk + 1
    row_g = b * tq + lax.broadcasted_iota(I32, (tq, tk), 0)
    adm_end = (row_g // CHUNK + 1) * CHUNK
    col_l = lax.broadcasted_iota(I32, (tq, tk), 1)

    idx_scale = (IDX_HEADS ** -0.5) * (IDX_DIM ** -0.5)
    iw = iw_ref[...][:, 8:8 + IDX_HEADS] * idx_scale
    iq = jnp.concatenate([iq_lo_ref[...], iq_hi_ref[...]], axis=1)

    def score_tile(kt, _):
        start = pl.multiple_of(kt * tk, tk)
        ik = ik_ref[pl.ds(start, tk), :][:, :IDX_DIM]
        acc = jnp.zeros((tq, tk), F32)
        for h in range(IDX_HEADS):
            a = iq[:, h * IDX_DIM:(h + 1) * IDX_DIM]
            rel = lax.dot_general(a, ik, (((1,), (1,)), ((), ())), preferred_element_type=F32)
            acc = acc + jnp.maximum(rel, 0.0) * iw[:, h:h + 1]
        key_ref[kt] = jnp.where(kt * tk + col_l < adm_end, acc, -jnp.inf)
        return 0

    lax.fori_loop(0, n_tiles, score_tile, 0)

    def float_of(code_u):
        code = code_u ^ INT_MIN
        return pltpu.bitcast(jnp.where(code >= 0, code, code ^ 0x7FFFFFFF), F32)

    def count_ge(cand):
        def body(kt, part):
            ge = jnp.where(key_ref[kt] >= cand, 1, 0)
            for c in range(tk // 128):
                part = part + ge[:, c * 128:(c + 1) * 128]
            return part
        part = lax.fori_loop(0, n_tiles, body, jnp.zeros((tq, 128), I32))
        return jnp.sum(part, axis=1, keepdims=True)

    def search_cond(carry):
        i, _, _, pending = carry
        return (i < 32) & (pending > 0)

    def refine(i, t_u, hit):
        cand_u = t_u | lax.shift_left(jnp.int32(1), 31 - i)
        cnt = count_ge(float_of(cand_u))
        return jnp.where(cnt >= n_sel, cand_u, t_u), jnp.where(cnt == n_sel, 1, hit)

    def search_step(carry):
        i, t_u, hit, _ = carry
        t_u, hit = refine(i, t_u, hit)
        return i + 1, t_u, hit, jnp.sum(1 - hit)

    zeros = jnp.zeros((tq, 1), I32)
    lead = 12
    t_u, hit = lax.fori_loop(0, lead, lambda i, c: refine(i, *c), (zeros, zeros))
    _, t_u, _, _ = lax.while_loop(search_cond, search_step, (jnp.int32(lead), t_u, hit, jnp.sum(1 - hit)))
    thr = jnp.where(t_u == 0, jnp.finfo(F32).min, float_of(t_u))

    for j in range(DSA_KV_HEADS):
        for g in range(DSA_GROUP):
            hd = j * DSA_GROUP + g
            qg_ref[j, g * tq:(g + 1) * tq, :] = bq_ref[:, hd * HEAD_DIM:(hd + 1) * HEAD_DIM]
    _softmax_init(m_ref, l_ref, acc_ref)

    def attn_tile(kt, _):
        start = pl.multiple_of(kt * tk, tk)
        sel = key_ref[kt] >= thr
        for j in range(DSA_KV_HEADS):
            hs = slice(j * HEAD_DIM, (j + 1) * HEAD_DIM)
            k = bk_ref[pl.ds(start, tk), hs]
            v = bv_ref[pl.ds(start, tk), hs]
            s = lax.dot_general(qg_ref[j], k, (((1,), (1,)), ((), ())), preferred_element_type=F32)
            s = jnp.where(sel[None], s.reshape(DSA_GROUP, tq, tk), NEG_BIG).reshape(DSA_GROUP * tq, tk)
            _softmax_tile(s, v, m_ref, l_ref, acc_ref, j)
        return 0

    lax.fori_loop(0, n_tiles, attn_tile, 0)

    for j in range(DSA_KV_HEADS):
        o = _softmax_result(l_ref, acc_ref, j)
        for g in range(DSA_GROUP):
            hd = j * DSA_GROUP + g
            o_ref[:, hd * HEAD_DIM:(hd + 1) * HEAD_DIM] = o[g * tq:(g + 1) * tq].astype(o_ref.dtype)


def _dsa_attention(pbf, pg, s, tq, tk):
    n_sel = min(TOPK_MAX, s // 4)
    kvw = DSA_KV_HEADS * HEAD_DIM
    rows = DSA_GROUP * tq
    iqw = IDX_HEADS * IDX_DIM // 2
    return pl.pallas_call(
        functools.partial(_dsa_kernel, tq=tq, tk=tk, n_sel=n_sel),
        grid=(s // tq,),
        in_specs=[
            pl.BlockSpec((tq, MIX_B), lambda b: (b, COL_BQ // MIX_B)),
            pl.BlockSpec((tq, iqw), lambda b: (b, COL_IQ // iqw)),
            pl.BlockSpec((tq, iqw), lambda b: (b, COL_IQ // iqw + 1)),
            pl.BlockSpec((tq, 128), lambda b: (b, COL_SMALL // 128)),
            pl.BlockSpec((s, 128), lambda b: (0, COL_IK // 128)),
            pl.BlockSpec((s, kvw), lambda b: (0, COL_BK // kvw)),
            pl.BlockSpec((s, kvw), lambda b: (0, COL_BV // kvw)),
        ],
        out_specs=pl.BlockSpec((tq, MIX_B), lambda b: (b, 0)),
        out_shape=jax.ShapeDtypeStruct((s, MIX_B), BF16),
        scratch_shapes=[
            pltpu.VMEM((s // tk, tq, tk), F32),
            pltpu.VMEM((DSA_KV_HEADS, rows, HEAD_DIM), BF16),
            pltpu.VMEM((DSA_KV_HEADS, rows, 128), F32),
            pltpu.VMEM((DSA_KV_HEADS, rows, 128), F32),
            pltpu.VMEM((DSA_KV_HEADS, rows, HEAD_DIM), F32),
        ],
        compiler_params=_cparams(("arbitrary",), 56),
        name="dsa_attention",
    )(pbf, pbf, pbf, pg, pbf, pbf, pbf)


def _layer_norm(z, g, b):
    mu = jnp.mean(z, axis=-1, keepdims=True)
    zc = z - mu
    var = jnp.mean(zc * zc, axis=-1, keepdims=True)
    return zc * lax.rsqrt(var + LN_EPS) * g + b


def _merge_kernel(a_ref, b_ref, ga_ref, gb_ref, x_ref, wa_ref, wb_ref, wo_ref, g_ref, beta_ref,
                  wr_ref, br_ref, h_ref, eidx_ref, gate_ref, pos_ref, cnt_ref, carry_ref, *, alpha):
    i = pl.program_id(0)

    @pl.when(i == 0)
    def _():
        carry_ref[...] = jnp.zeros_like(carry_ref)

    ma = jnp.dot(a_ref[...], wa_ref[...], preferred_element_type=F32)
    mb = jnp.dot(b_ref[...], wb_ref[...], preferred_element_type=F32)
    merged = jax.nn.sigmoid(ga_ref[...]) * ma + jax.nn.sigmoid(gb_ref[...]) * mb
    y = jnp.dot(merged.astype(BF16), wo_ref[...], preferred_element_type=F32)
    h = _layer_norm(alpha * x_ref[...] + y, g_ref[...], beta_ref[...])
    h_ref[...] = h

    logits = jnp.dot(h, wr_ref[...], preferred_element_type=F32, precision=lax.Precision.HIGHEST) + br_ref[...]
    tm, ne = logits.shape
    lane = lax.broadcasted_iota(I32, (tm, ne), 1)
    lane_k = lax.broadcasted_iota(I32, (tm, TOP_K), 1)
    work = logits
    vals, sels = [], []
    eidx = jnp.zeros((tm, TOP_K), I32)
    onehot = jnp.zeros((tm, ne), F32)
    for k in range(TOP_K):
        mv = jnp.max(work, axis=1, keepdims=True)
        idx = jnp.min(jnp.where(work == mv, lane, ne), axis=1, keepdims=True)
        sel = lane == idx
        vals.append(mv)
        sels.append(sel)
        eidx = jnp.where(lane_k == k, idx, eidx)
        onehot = onehot + jnp.where(sel, 1.0, 0.0)
        work = jnp.where(sel, -jnp.inf, work)
    exps = [jnp.exp(v - vals[0]) for v in vals]
    denom = exps[0] + exps[1] + exps[2] + exps[3]
    gates = jnp.zeros((tm, TOP_K), F32)
    for k in range(TOP_K):
        gates = jnp.where(lane_k == k, exps[k] / denom, gates)

    r_i = lax.broadcasted_iota(I32, (tm, tm), 0)
    c_i = lax.broadcasted_iota(I32, (tm, tm), 1)
    lower = jnp.where(c_i < r_i, 1.0, 0.0).astype(BF16)
    rank = carry_ref[...] + jnp.dot(lower, onehot.astype(BF16), preferred_element_type=F32)
    pos = jnp.zeros((tm, TOP_K), I32)
    for k in range(TOP_K):
        pk = jnp.sum(jnp.where(sels[k], rank, 0.0), axis=1, keepdims=True).astype(I32)
        pos = jnp.where(lane_k == k, pk, pos)
    carry_ref[...] = carry_ref[...] + jnp.sum(onehot, axis=0, keepdims=True)

    eidx_ref[...] = eidx
    gate_ref[...] = gates
    pos_ref[...] = pos
    cnt_ref[...] = carry_ref[...].astype(I32)


def _merge(a_out, b_out, pg, x2d, wa, wb, wo, ln_g, ln_b, w_router, b_router, alpha):
    s, d = x2d.shape
    tm = min(256, s)
    full = lambda shape: pl.BlockSpec(shape, lambda i: (0,) * len(shape))
    return pl.pallas_call(
        functools.partial(_merge_kernel, alpha=alpha),
        grid=(s // tm,),
        in_specs=[
            pl.BlockSpec((tm, MIX_A), lambda i: (i, 0)),
            pl.BlockSpec((tm, MIX_B), lambda i: (i, 0)),
            pl.BlockSpec((tm, d), lambda i: (i, COL_GA // d)),
            pl.BlockSpec((tm, d), lambda i: (i, COL_GB // d)),
            pl.BlockSpec((tm, d), lambda i: (i, 0)),
            full((MIX_A, d)), full((MIX_B, d)), full((d, d)),
            full((1, d)), full((1, d)), full((d, N_EXPERTS)), full((1, N_EXPERTS)),
        ],
        out_specs=[
            pl.BlockSpec((tm, d), lambda i: (i, 0)),
            pl.BlockSpec((tm, TOP_K), lambda i: (i, 0)),
            pl.BlockSpec((tm, TOP_K), lambda i: (i, 0)),
            pl.BlockSpec((tm, TOP_K), lambda i: (i, 0)),
            full((1, N_EXPERTS)),
        ],
        out_shape=[
            jax.ShapeDtypeStruct((s, d), F32),
            jax.ShapeDtypeStruct((s, TOP_K), I32),
            jax.ShapeDtypeStruct((s, TOP_K), F32),
            jax.ShapeDtypeStruct((s, TOP_K), I32),
            jax.ShapeDtypeStruct((1, N_EXPERTS), I32),
        ],
        scratch_shapes=[pltpu.VMEM((1, N_EXPERTS), F32)],
        compiler_params=_cparams(("arbitrary",), 56),
        name="merge_ln_router",
    )(a_out, b_out, pg, pg, x2d, wa, wb, wo, ln_g, ln_b, w_router, b_router)


def _dispatch_kernel(dest_ref, gap_start_ref, gap_len_ref, total_ref, h_ref, xs_ref, stage_ref, zero_ref,
                     sem, zsem, *, max_slack):
    i = pl.program_id(0)
    last = pl.num_programs(0) - 1
    slot = lax.rem(i, 2)
    n = dest_ref.shape[0]

    @pl.when(i == 0)
    def _():
        zero_ref[...] = jnp.zeros_like(zero_ref)

        def gap_copy(e, k):
            return pltpu.make_async_copy(zero_ref.at[pl.ds(0, 1)], xs_ref.at[pl.ds(gap_start_ref[e] + k, 1)], zsem)

        n_slots = xs_ref.shape[0]

        def slack_copy(k):
            row = pl.multiple_of(total_ref[0] + k * MOE_CHUNK, MOE_CHUNK)
            return pltpu.make_async_copy(zero_ref, xs_ref.at[pl.ds(row, MOE_CHUNK)], zsem)

        def slack(action):
            for k in range(max_slack):
                @pl.when(total_ref[0] + k * MOE_CHUNK < n_slots)
                def _():
                    action(slack_copy(k))

        def issue(e, _):
            lax.fori_loop(0, gap_len_ref[e], lambda k, c: (gap_copy(e, k).start(), c)[1], 0)
            return 0

        def settle(e, _):
            lax.fori_loop(0, gap_len_ref[e], lambda k, c: (gap_copy(e, k).wait(), c)[1], 0)
            return 0

        lax.fori_loop(0, N_EXPERTS, issue, 0)
        slack(lambda cp: cp.start())
        lax.fori_loop(0, N_EXPERTS, settle, 0)
        slack(lambda cp: cp.wait())

    def drain(sl):
        pltpu.make_async_copy(xs_ref.at[pl.ds(0, n)], xs_ref.at[pl.ds(0, n)], sem.at[sl]).wait()

    @pl.when(i >= 2)
    def _():
        drain(slot)

    stage_ref[slot] = h_ref[...]

    def scatter(sl):
        def start(g, _):
            base = pl.multiple_of(g * ROW_TILE, ROW_TILE)
            for u in range(ROW_TILE * TOP_K):
                r = u // TOP_K
                pltpu.make_async_copy(stage_ref.at[sl, pl.ds(base + r, 1)],
                                      xs_ref.at[pl.ds(dest_ref[g * (ROW_TILE * TOP_K) + u], 1)],
                                      sem.at[sl]).start(priority=u % 2)
            return 0
        lax.fori_loop(0, n // (ROW_TILE * TOP_K), start, 0)

    for parity in range(2):
        @pl.when(slot == parity)
        def _():
            scatter(parity)

    @pl.when(i == last)
    def _():
        drain(slot)

        @pl.when(i >= 1)
        def _():
            drain(1 - slot)


def _dispatch(h, dest_flat, gap_start, gap_len, total, n_slots):
    s, d = h.shape
    tm = min(128, s)
    smem = pl.BlockSpec(memory_space=pltpu.SMEM)
    max_slack = (n_slots - s * TOP_K + MOE_CHUNK - 1) // MOE_CHUNK
    return pl.pallas_call(
        functools.partial(_dispatch_kernel, max_slack=max_slack),
        grid=(s // tm,),
        in_specs=[
            pl.BlockSpec((tm * TOP_K,), lambda i: (i,), memory_space=pltpu.SMEM),
            smem, smem, smem,
            pl.BlockSpec((tm, d), lambda i: (i, 0)),
        ],
        out_specs=pl.BlockSpec(memory_space=pl.ANY),
        out_shape=jax.ShapeDtypeStruct((n_slots, d), h.dtype),
        scratch_shapes=[pltpu.VMEM((2, tm, d), h.dtype), pltpu.VMEM((MOE_CHUNK, d), h.dtype),
                        pltpu.SemaphoreType.DMA((2,)), pltpu.SemaphoreType.DMA(())],
        compiler_params=_cparams(("arbitrary",), 32),
        name="moe_dispatch",
    )(dest_flat, gap_start, gap_len, total, h)


ROW_DMA_PRIORITY = 1


def _expert_rows_pipeline(n_chunks, prologue, in_copy, out_copy, compute):
    @pl.when(n_chunks > 0)
    def _():
        in_copy(0, 0).start(priority=ROW_DMA_PRIORITY)
        prologue()

        def body(c, _):
            slot = lax.rem(c, 2)
            in_copy(c, slot).wait()

            @pl.when(c + 1 < n_chunks)
            def _():
                in_copy(c + 1, 1 - slot).start(priority=ROW_DMA_PRIORITY)

            @pl.when(c >= 2)
            def _():
                out_copy(c - 2, slot).wait()

            compute(slot)
            out_copy(c, slot).start(priority=ROW_DMA_PRIORITY)
            return 0

        lax.fori_loop(0, n_chunks, body, 0)

        @pl.when(n_chunks >= 2)
        def _():
            out_copy(n_chunks - 2, lax.rem(n_chunks, 2)).wait()

        out_copy(n_chunks - 1, lax.rem(n_chunks - 1, 2)).wait()


def _chunk_rows(start_ref, e, c, n_slots):
    del n_slots
    return pl.ds(pl.multiple_of(start_ref[e] + c * MOE_CHUNK, MOE_ALIGN), MOE_CHUNK)


def _gate_up_kernel(start_ref, cnt_ref, x_hbm, w_hbm, b_ref, act_hbm,
                    w_stage, w_sc, xbuf, obuf, sem_in, sem_out, sem_w):
    e = pl.program_id(0)
    n_slots = x_hbm.shape[0]
    n_chunks = (cnt_ref[e] + MOE_CHUNK - 1) // MOE_CHUNK
    d_ff = obuf.shape[2]

    def w_copy(idx):
        return pltpu.make_async_copy(w_hbm.at[idx], w_stage, sem_w)

    @pl.when(e == 0)
    def _():
        w_copy(0).start()

    def fetch_weights():
        @pl.when(e < N_EXPERTS)
        def _():
            w_copy(e).wait()
            w_sc[...] = w_stage[...].astype(BF16)

            @pl.when(e + 1 < N_EXPERTS)
            def _():
                w_copy(e + 1).start()

    def in_copy(c, slot):
        return pltpu.make_async_copy(x_hbm.at[_chunk_rows(start_ref, e, c, n_slots)], xbuf.at[slot], sem_in.at[slot])

    def out_copy(c, slot):
        return pltpu.make_async_copy(obuf.at[slot], act_hbm.at[_chunk_rows(start_ref, e, c, n_slots)],
                                     sem_out.at[slot])

    def compute(slot):
        x = xbuf[slot].astype(BF16)
        cw = 256
        for c in range(d_ff // cw):
            gs = slice(c * cw, (c + 1) * cw)
            us = slice(d_ff + c * cw, d_ff + (c + 1) * cw)
            g = jnp.dot(x, w_sc[:, gs], preferred_element_type=F32) + b_ref[:, gs]
            u = jnp.dot(x, w_sc[:, us], preferred_element_type=F32) + b_ref[:, us]
            g = jnp.minimum(g, SWIGLU_LIMIT)
            u = jnp.clip(u, -SWIGLU_LIMIT, SWIGLU_LIMIT)
            obuf[slot, :, gs] = (g * jax.nn.sigmoid(SWIGLU_ALPHA * g) * (u + 1.0)).astype(obuf.dtype)

    @pl.when(n_chunks == 0)
    def _():
        fetch_weights()

    _expert_rows_pipeline(n_chunks, fetch_weights, in_copy, out_copy, compute)


def _down_kernel(start_ref, cnt_ref, a_hbm, wd_ref, bd_ref, y_hbm, wd_sc, abuf, obuf, sem_in, sem_out):
    e = pl.program_id(0)
    n_slots = y_hbm.shape[0]
    n_chunks = (cnt_ref[e] + MOE_CHUNK - 1) // MOE_CHUNK

    def cast_weights():
        wd_sc[...] = wd_ref[...].astype(BF16)

    def in_copy(c, slot):
        return pltpu.make_async_copy(a_hbm.at[_chunk_rows(start_ref, e, c, n_slots)], abuf.at[slot],
                                     sem_in.at[slot])

    def out_copy(c, slot):
        return pltpu.make_async_copy(obuf.at[slot], y_hbm.at[_chunk_rows(start_ref, e, c, n_slots)],
                                     sem_out.at[slot])

    def compute(slot):
        cw = 512
        for c in range(obuf.shape[2] // cw):
            cs = slice(c * cw, (c + 1) * cw)
            obuf[slot, :, cs] = jnp.dot(abuf[slot], wd_sc[:, cs], preferred_element_type=F32) + bd_ref[:, cs]

    _expert_rows_pipeline(n_chunks, cast_weights, in_copy, out_copy, compute)


def _experts(xs, starts, counts, w_gate_up, b_gate_up, w_down, b_down):
    n_slots, d = xs.shape
    any_spec = pl.BlockSpec(memory_space=pl.ANY)
    dma_sems = pltpu.SemaphoreType.DMA((2,))
    bgu = b_gate_up.reshape(N_EXPERTS, 1, 2 * D_FF)
    n_groups = starts.shape[0]
    wi = lambda e: jnp.minimum(e, N_EXPERTS - 1)
    act = pl.pallas_call(
        _gate_up_kernel,
        grid_spec=pltpu.PrefetchScalarGridSpec(
            num_scalar_prefetch=2,
            grid=(n_groups,),
            in_specs=[
                any_spec,
                any_spec,
                pl.BlockSpec((None, 1, 2 * D_FF), lambda e, st, ct: (wi(e), 0, 0)),
            ],
            out_specs=any_spec,
            scratch_shapes=[pltpu.VMEM((d, 2 * D_FF), F32), pltpu.VMEM((d, 2 * D_FF), BF16),
                            pltpu.VMEM((2, MOE_CHUNK, d), xs.dtype), pltpu.VMEM((2, MOE_CHUNK, D_FF), BF16),
                            dma_sems, dma_sems, pltpu.SemaphoreType.DMA(())],
        ),
        out_shape=jax.ShapeDtypeStruct((n_slots, D_FF), BF16),
        compiler_params=_cparams(("arbitrary",), 60),
        name="moe_gate_up",
    )(starts, counts, xs, w_gate_up, bgu)
    bd = b_down.reshape(N_EXPERTS, 1, d)
    ys = pl.pallas_call(
        _down_kernel,
        grid_spec=pltpu.PrefetchScalarGridSpec(
            num_scalar_prefetch=2,
            grid=(n_groups,),
            in_specs=[
                any_spec,
                pl.BlockSpec((None, D_FF, d), lambda e, st, ct: (wi(e), 0, 0)),
                pl.BlockSpec((None, 1, d), lambda e, st, ct: (wi(e), 0, 0)),
            ],
            out_specs=any_spec,
            scratch_shapes=[pltpu.VMEM((D_FF, d), BF16),
                            pltpu.VMEM((2, MOE_CHUNK, D_FF), BF16), pltpu.VMEM((2, MOE_CHUNK, d), F32),
                            dma_sems, dma_sems],
        ),
        out_shape=jax.ShapeDtypeStruct((n_slots, d), F32),
        compiler_params=_cparams(("arbitrary",), 56),
        name="moe_down",
    )(starts, counts, act, w_down, bd)
    return ys


def _combine_kernel(dest_ref, dest_next_ref, gate_ref, h_ref, g_ref, beta_ref, ys_ref, o_ref, buf_ref, sem,
                    *, alpha):
    i = pl.program_id(0)
    last = pl.num_programs(0) - 1
    slot = lax.rem(i, 2)
    n = dest_ref.shape[0]

    def gather(dref, sl):
        def start(g, _):
            base = pl.multiple_of(g * ROW_TILE, ROW_TILE)
            for u in range(ROW_TILE * TOP_K):
                r, k = divmod(u, TOP_K)
                pltpu.make_async_copy(ys_ref.at[pl.ds(dref[g * (ROW_TILE * TOP_K) + u], 1)],
                                      buf_ref.at[sl, k, pl.ds(base + r, 1)],
                                      sem.at[sl]).start(priority=u % 2)
            return 0
        lax.fori_loop(0, n // (ROW_TILE * TOP_K), start, 0)

    for parity in range(2):
        @pl.when((i == 0) & (slot == parity))
        def _():
            gather(dest_ref, parity)

        @pl.when((i < last) & (slot == parity))
        def _():
            gather(dest_next_ref, 1 - parity)

    pltpu.make_async_copy(buf_ref.at[slot], buf_ref.at[slot], sem.at[slot]).wait()
    gates = gate_ref[...]
    y = gates[:, 0:1] * buf_ref[slot, 0]
    for k in range(1, TOP_K):
        y = y + gates[:, k:k + 1] * buf_ref[slot, k]
    o_ref[...] = _layer_norm(alpha * h_ref[...] + y, g_ref[...], beta_ref[...])


def _combine(ys, dest_flat, gates, h, ln_g, ln_b, alpha):
    s, d = h.shape
    tm = min(128, s)
    n_steps = s // tm
    return pl.pallas_call(
        functools.partial(_combine_kernel, alpha=alpha),
        grid=(n_steps,),
        in_specs=[
            pl.BlockSpec((tm * TOP_K,), lambda i: (i,), memory_space=pltpu.SMEM),
            pl.BlockSpec((tm * TOP_K,), lambda i: (jnp.minimum(i + 1, n_steps - 1),), memory_space=pltpu.SMEM),
            pl.BlockSpec((tm, TOP_K), lambda i: (i, 0)),
            pl.BlockSpec((tm, d), lambda i: (i, 0)),
            pl.BlockSpec((1, d), lambda i: (0, 0)),
            pl.BlockSpec((1, d), lambda i: (0, 0)),
            pl.BlockSpec(memory_space=pl.ANY),
        ],
        out_specs=pl.BlockSpec((tm, d), lambda i: (i, 0)),
        out_shape=jax.ShapeDtypeStruct((s, d), F32),
        scratch_shapes=[pltpu.VMEM((2, TOP_K, tm, d), F32), pltpu.SemaphoreType.DMA((2,))],
        compiler_params=_cparams(("arbitrary",), 32),
        name="moe_combine",
    )(dest_flat, dest_flat, gates, h, ln_g, ln_b, ys)


def _layer(x2d, w_in, b_forget, w_branch_a, w_branch_b, w_out, ln1_g, ln1_b, w_router, b_router,
           w_gate_up, b_gate_up, w_down, b_down, ln2_g, ln2_b, tables):
    s, d = x2d.shape
    alpha = (2.0 * DEPTH) ** 0.25
    scale = HEAD_DIM ** -0.5 * LOG2E

    sizes = (MIX_A, MIX_A, MIX_A, FOX_HEADS, MIX_B, DSA_KV_HEADS * HEAD_DIM, DSA_KV_HEADS * HEAD_DIM,
             IDX_HEADS * IDX_DIM, IDX_DIM, IDX_HEADS, D_MODEL, D_MODEL)
    off = dict(zip(("aq", "ak", "av", "af", "bq", "bk", "bv", "iq", "ik", "iw", "ga", "gb", "end"),
                   np.concatenate([[0], np.cumsum(sizes)]).tolist()))
    col_scale = np.ones((N_PBF,), np.float32)
    col_scale[COL_AQ:COL_AQ + MIX_A] = scale
    col_scale[COL_BQ:COL_BQ + MIX_B] = scale
    w_bf = (jnp.concatenate(
        [w_in[:, off["aq"]:off["af"]], w_in[:, off["bq"]:off["iw"]],
         jnp.zeros((d, N_PBF - COL_IK - IDX_DIM), F32)], axis=1) * col_scale).astype(BF16)
    w_g = jnp.concatenate(
        [w_in[:, off["ga"]:off["end"]], w_in[:, off["af"]:off["bq"]], w_in[:, off["iw"]:off["ga"]],
         jnp.zeros((d, N_PG - COL_SMALL - FOX_HEADS - IDX_HEADS), F32)], axis=1).astype(BF16)

    pbf, pg = _project(x2d, w_bf, tables, w_g)

    fox_tq = min(512, s)
    fox_tk = min(512, s)
    af_t = pg[:, COL_SMALL:COL_SMALL + FOX_HEADS].T
    c_t = _forget_cumsum(af_t, b_forget)
    c_tiles = c_t.reshape(FOX_HEADS, s // fox_tk, 1, fox_tk)
    a_out = _fox_attention(pbf, c_tiles, s, fox_tq, fox_tk, 4)

    b_out = _dsa_attention(pbf, pg, s, 128, min(512, s))

    h, eidx, gates, pos, counts = _merge(
        a_out, b_out, pg, x2d, w_branch_a.astype(BF16), w_branch_b.astype(BF16), w_out.astype(BF16),
        ln1_g.reshape(1, d), ln1_b.reshape(1, d), w_router, b_router.reshape(1, N_EXPERTS), alpha)

    counts = counts.reshape(N_EXPERTS).astype(I32)
    aligned = (counts + MOE_ALIGN - 1) // MOE_ALIGN * MOE_ALIGN
    ends = jnp.cumsum(aligned).astype(I32)
    starts = ends - aligned
    bound = s * TOP_K + N_EXPERTS * MOE_ALIGN + 2 * MOE_CHUNK
    n_slots = (bound + MOE_CHUNK - 1) // MOE_CHUNK * MOE_CHUNK
    experts = jnp.arange(N_EXPERTS, dtype=I32)
    start_of = jnp.sum(jnp.where(eidx[..., None] == experts, starts, 0), axis=-1)
    dest = (start_of + pos).reshape(s * TOP_K).astype(I32)
    total = (ends[-1:] + MOE_CHUNK - 1) // MOE_CHUNK * MOE_CHUNK
    gap_len = aligned - counts + jnp.where(experts == N_EXPERTS - 1, total[0] - ends[-1], 0)
    group_starts = jnp.concatenate([starts, total])
    group_counts = jnp.concatenate([counts + jnp.where(experts == N_EXPERTS - 1, gap_len, 0), n_slots - total])

    xs = _dispatch(h, dest, starts + counts, gap_len, total, n_slots)
    ys = _experts(xs, group_starts, group_counts, w_gate_up, b_gate_up, w_down, b_down)
    return _combine(ys, dest, gates, h, ln2_g.reshape(1, d), ln2_b.reshape(1, d), alpha)


def kernel(x, w_in, b_forget, w_branch_a, w_branch_b, w_out, ln1_g, ln1_b, w_router, b_router,
           w_gate_up, b_gate_up, w_down, b_down, ln2_g, ln2_b):
    bsz, s, d = x.shape
    tables = _rope_tables(s)
    outs = []
    for bi in range(bsz):
        xb = x[bi]
        for l in range(DEPTH):
            xb = _layer(xb, w_in[l], b_forget[l], w_branch_a[l], w_branch_b[l], w_out[l], ln1_g[l], ln1_b[l],
                        w_router[l], b_router[l], w_gate_up[l], b_gate_up[l], w_down[l], b_down[l],
                        ln2_g[l], ln2_b[l], tables)
        outs.append(xb)
    return outs[0][None] if bsz == 1 else jnp.stack(outs)
```

```python
import functools

import numpy as np
import jax
import jax.numpy as jnp
from jax import lax
from jax.experimental import pallas as pl
from jax.experimental.pallas import tpu as pltpu

F32 = jnp.float32
BF16 = jnp.bfloat16
I32 = jnp.int32

D_MODEL = 2048
DEPTH = 1
CHUNK = 64
HEAD_DIM = 128
FOX_HEADS = 8
DSA_HEADS = 8
DSA_KV_HEADS = 2
DSA_GROUP = DSA_HEADS // DSA_KV_HEADS
IDX_HEADS = 16
IDX_DIM = 64
TOPK_MAX = 256
ROPE_THETA = 500000.0
ROT_FRACTION_DEN = 4
MIX_A = FOX_HEADS * HEAD_DIM
MIX_B = DSA_HEADS * HEAD_DIM
N_EXPERTS = 32
TOP_K = 4
D_FF = D_MODEL
SWIGLU_ALPHA = 1.702
SWIGLU_LIMIT = 7.0
LN_EPS = 1e-5

MIB = 1024 * 1024
NEG_BIG = -1e30
LOG2E = 1.4426950408889634
INT_MIN = -(2 ** 31)

COL_AQ = 0
COL_AK = 1024
COL_AV = 2048
COL_BQ = 3072
COL_BK = 4096
COL_BV = 4352
COL_IQ = 4608
COL_IK = 5632
N_PBF = 5888
PROJ_TN = 256
ROPE_HEAD_TILES = (12, 13, 14, 15, 16)
ROPE_IDX_TILES = (18, 19, 20, 21, 22)
COL_GA = 0
COL_GB = 2048
COL_SMALL = 4096
N_PG = 4224
PG_TN = 384

MOE_CHUNK = 256
MOE_ALIGN = 16
ROW_TILE = 8


def _cparams(dims, vmem_mib):
    return pltpu.CompilerParams(dimension_semantics=dims, vmem_limit_bytes=vmem_mib * MIB)


IN_SIZES = (MIX_A, MIX_A, MIX_A, FOX_HEADS, MIX_B, DSA_KV_HEADS * HEAD_DIM, DSA_KV_HEADS * HEAD_DIM,
            IDX_HEADS * IDX_DIM, IDX_DIM, IDX_HEADS, D_MODEL, D_MODEL)
IN_OFF = dict(zip(("aq", "ak", "av", "af", "bq", "bk", "bv", "iq", "ik", "iw", "ga", "gb", "end"),
                  np.concatenate([[0], np.cumsum(IN_SIZES)]).tolist()))


def _prep_kernel(w_ref, wbf_ref, wg_ref, *, scale):
    o = IN_OFF
    rows = w_ref.shape[0]

    def cols(a, b, mult=None):
        v = w_ref[:, a:b]
        return (v if mult is None else v * mult).astype(BF16)

    wbf_ref[:, COL_AQ:COL_AK] = cols(o["aq"], o["ak"], scale)
    wbf_ref[:, COL_AK:COL_BQ] = cols(o["ak"], o["af"])
    wbf_ref[:, COL_BQ:COL_BK] = cols(o["bq"], o["bk"], scale)
    wbf_ref[:, COL_BK:COL_IK] = cols(o["bk"], o["ik"])
    wbf_ref[:, COL_IK:N_PBF] = jnp.concatenate(
        [cols(o["ik"], o["iw"]), jnp.zeros((rows, N_PBF - COL_IK - IDX_DIM), BF16)], axis=1)
    wg_ref[:, COL_GA:COL_SMALL] = cols(o["ga"], o["end"])
    wg_ref[:, COL_SMALL:N_PG] = jnp.concatenate(
        [cols(o["af"], o["bq"]), cols(o["iw"], o["ga"]),
         jnp.zeros((rows, N_PG - COL_SMALL - FOX_HEADS - IDX_HEADS), BF16)], axis=1)


def _prepare_w_in(w_in, scale):
    d, n_in = w_in.shape
    tr = 256
    return pl.pallas_call(
        functools.partial(_prep_kernel, scale=scale),
        grid=(d // tr,),
        in_specs=[pl.BlockSpec((tr, n_in), lambda i: (i, 0))],
        out_specs=[pl.BlockSpec((tr, N_PBF), lambda i: (i, 0)), pl.BlockSpec((tr, N_PG), lambda i: (i, 0))],
        out_shape=[jax.ShapeDtypeStruct((d, N_PBF), BF16), jax.ShapeDtypeStruct((d, N_PG), BF16)],
        compiler_params=_cparams(("arbitrary",), 48),
        name="w_in_prep",
    )(w_in)


def _tile_in(j, tiles):
    cond = j == tiles[0]
    for t in tiles[1:]:
        cond = cond | (j == t)
    return cond


def _proj_rope_kernel(x_ref, w_ref, tab_ref, o_ref, xb_ref):
    j = pl.program_id(1)

    @pl.when(j == 0)
    def _():
        xb_ref[...] = x_ref[...].astype(BF16)

    acc = jnp.dot(xb_ref[...], w_ref[...], preferred_element_type=F32)
    tn = acc.shape[1]
    is_head = _tile_in(j, ROPE_HEAD_TILES)
    is_idx = _tile_in(j, ROPE_IDX_TILES)

    def rope(shift):
        c = tab_ref[0, 0]
        s_prev = tab_ref[0, 1]
        s_next = tab_ref[0, 2]
        out = acc * c + pltpu.roll(acc, shift, 1) * s_prev + pltpu.roll(acc, tn - shift, 1) * s_next
        o_ref[...] = out.astype(o_ref.dtype)

    @pl.when(is_head)
    def _():
        rope(HEAD_DIM // ROT_FRACTION_DEN // 2)

    @pl.when(is_idx)
    def _():
        rope(IDX_DIM // ROT_FRACTION_DEN // 2)

    @pl.when(jnp.logical_not(is_head | is_idx))
    def _():
        o_ref[...] = acc.astype(o_ref.dtype)


def _proj_plain_kernel(x_ref, w_ref, o_ref, xb_ref):
    j = pl.program_id(1)

    @pl.when(j == 0)
    def _():
        xb_ref[...] = x_ref[...].astype(BF16)

    o_ref[...] = jnp.dot(xb_ref[...], w_ref[...], preferred_element_type=F32).astype(o_ref.dtype)


def _rope_group(j):
    return jnp.where(_tile_in(j, ROPE_IDX_TILES), 1, 0)


def _project(x2d, w_bf, tables, w_g):
    s, d = x2d.shape
    tm = min(1024, s)
    pbf = pl.pallas_call(
        _proj_rope_kernel,
        grid=(s // tm, N_PBF // PROJ_TN),
        in_specs=[
            pl.BlockSpec((tm, d), lambda i, j: (i, 0)),
            pl.BlockSpec((d, PROJ_TN), lambda i, j: (0, j)),
            pl.BlockSpec((1, 3, tm, PROJ_TN), lambda i, j: (_rope_group(j), 0, i, 0)),
        ],
        out_specs=pl.BlockSpec((tm, PROJ_TN), lambda i, j: (i, j)),
        out_shape=jax.ShapeDtypeStruct((s, N_PBF), BF16),
        scratch_shapes=[pltpu.VMEM((tm, d), BF16)],
        compiler_params=_cparams(("arbitrary", "arbitrary"), 48),
        name="proj_bf16",
    )(x2d, w_bf, tables)
    pg = pl.pallas_call(
        _proj_plain_kernel,
        grid=(s // tm, N_PG // PG_TN),
        in_specs=[
            pl.BlockSpec((tm, d), lambda i, j: (i, 0)),
            pl.BlockSpec((d, PG_TN), lambda i, j: (0, j)),
        ],
        out_specs=pl.BlockSpec((tm, PG_TN), lambda i, j: (i, j)),
        out_shape=jax.ShapeDtypeStruct((s, N_PG), F32),
        scratch_shapes=[pltpu.VMEM((tm, d), BF16)],
        compiler_params=_cparams(("arbitrary", "arbitrary"), 48),
        name="proj_f32",
    )(x2d, w_g)
    return pbf, pg


def _rope_tables(s):
    pos = jnp.arange(s, dtype=F32)

    def one(period):
        rot = period // ROT_FRACTION_DEN
        half = rot // 2
        inv = jnp.power(ROPE_THETA, -jnp.arange(0, rot, 2, dtype=F32) / rot)
        ang = pos[:, None] * inv[None, :]
        cos, sin = jnp.cos(ang), jnp.sin(ang)
        zero = jnp.zeros((s, period - rot), F32)
        c = jnp.concatenate([cos, cos, jnp.ones((s, period - rot), F32)], axis=1)
        s_prev = jnp.concatenate([jnp.zeros((s, half), F32), sin, zero], axis=1)
        s_next = jnp.concatenate([-sin, jnp.zeros((s, half), F32), zero], axis=1)
        reps = PROJ_TN // period
        return jnp.stack([jnp.tile(c, (1, reps)), jnp.tile(s_prev, (1, reps)), jnp.tile(s_next, (1, reps))])

    return jnp.stack([one(HEAD_DIM), one(IDX_DIM)])


def _cumsum_kernel(af_ref, bf_ref, c_ref, carry_ref):
    i = pl.program_id(0)

    @pl.when(i == 0)
    def _():
        carry_ref[...] = jnp.zeros_like(carry_ref)

    z = af_ref[...] + bf_ref[...]
    logf = jnp.minimum(z, 0.0) - jnp.log1p(jnp.exp(-jnp.abs(z)))
    t = z.shape[1]
    row = lax.broadcasted_iota(I32, (t, t), 0)
    col = lax.broadcasted_iota(I32, (t, t), 1)
    upper = (row <= col).astype(F32)
    c = jnp.dot(logf, upper, preferred_element_type=F32, precision=lax.Precision.HIGHEST) + carry_ref[...]
    c_ref[...] = c * LOG2E
    carry_ref[...] = c[:, t - 1:t]


def _forget_cumsum(af_t, b_forget):
    h, s = af_t.shape
    t = min(512, s)
    return pl.pallas_call(
        _cumsum_kernel,
        grid=(s // t,),
        in_specs=[pl.BlockSpec((h, t), lambda i: (0, i)), pl.BlockSpec((h, 1), lambda i: (0, 0))],
        out_specs=pl.BlockSpec((h, t), lambda i: (0, i)),
        out_shape=jax.ShapeDtypeStruct((h, s), F32),
        scratch_shapes=[pltpu.VMEM((h, 1), F32)],
        compiler_params=_cparams(("arbitrary",), 32),
        name="forget_cumsum",
    )(af_t, b_forget.reshape(h, 1).astype(F32))


def _softmax_init(m_ref, l_ref, acc_ref):
    m_ref[...] = jnp.full(m_ref.shape, NEG_BIG, F32)
    l_ref[...] = jnp.zeros(l_ref.shape, F32)
    acc_ref[...] = jnp.zeros(acc_ref.shape, F32)


def _softmax_tile(s, v, m_ref, l_ref, acc_ref, i):
    reps = s.shape[1] // 128
    m_old = m_ref[i]
    m_new = jnp.maximum(m_old, jnp.max(s, axis=1, keepdims=True))
    p = jnp.exp2(s - jnp.tile(m_new, (1, reps)))
    alpha = jnp.exp2(m_old - m_new)
    psum = p[:, :128]
    for c in range(1, reps):
        psum = psum + p[:, c * 128:(c + 1) * 128]
    l_ref[i] = alpha * l_ref[i] + psum
    acc_ref[i] = alpha * acc_ref[i] + jnp.dot(p.astype(BF16), v, preferred_element_type=F32)
    m_ref[i] = m_new


def _softmax_result(l_ref, acc_ref, i):
    return acc_ref[i] / jnp.sum(l_ref[i], axis=1, keepdims=True)


def _fox_kernel(q_ref, k_ref, v_ref, c_ref, o_ref, m_ref, l_ref, acc_ref, *, tq, tk, nh):
    qi = pl.program_id(1)
    _softmax_init(m_ref, l_ref, acc_ref)

    def tile(kt, masked):
        start = pl.multiple_of(kt * tk, tk)
        if masked:
            row = qi * tq + lax.broadcasted_iota(I32, (tq, tk), 0)
            col = kt * tk + lax.broadcasted_iota(I32, (tq, tk), 1)
            causal = col <= row
        for h in range(nh):
            hs = slice(h * HEAD_DIM, (h + 1) * HEAD_DIM)
            k = k_ref[pl.ds(start, tk), hs]
            v = v_ref[pl.ds(start, tk), hs]
            s = lax.dot_general(q_ref[:, hs], k, (((1,), (1,)), ((), ())), preferred_element_type=F32)
            s = s - c_ref[h, kt]
            if masked:
                s = jnp.where(causal, s, NEG_BIG)
            _softmax_tile(s, v, m_ref, l_ref, acc_ref, h)

    n_full = (qi * tq) // tk

    def body(kt, _):
        tile(kt, False)
        return 0

    lax.fori_loop(0, n_full, body, 0)
    for t in range((tq + tk - 1) // tk):
        tile(n_full + t, True)
    for h in range(nh):
        o_ref[:, h * HEAD_DIM:(h + 1) * HEAD_DIM] = _softmax_result(l_ref, acc_ref, h).astype(o_ref.dtype)


def _fox_attention(pbf, c_tiles, s, tq, tk, nh):
    cb = nh * HEAD_DIM
    return pl.pallas_call(
        functools.partial(_fox_kernel, tq=tq, tk=tk, nh=nh),
        grid=(FOX_HEADS // nh, s // tq),
        in_specs=[
            pl.BlockSpec((tq, cb), lambda g, i: (i, COL_AQ // cb + g)),
            pl.BlockSpec((s, cb), lambda g, i: (0, COL_AK // cb + g)),
            pl.BlockSpec((s, cb), lambda g, i: (0, COL_AV // cb + g)),
            pl.BlockSpec((nh, s // tk, 1, tk), lambda g, i: (g, 0, 0, 0)),
        ],
        out_specs=pl.BlockSpec((tq, cb), lambda g, i: (i, g)),
        out_shape=jax.ShapeDtypeStruct((s, MIX_A), BF16),
        scratch_shapes=[
            pltpu.VMEM((nh, tq, 128), F32),
            pltpu.VMEM((nh, tq, 128), F32),
            pltpu.VMEM((nh, tq, HEAD_DIM), F32),
        ],
        compiler_params=_cparams(("arbitrary", "arbitrary"), 56),
        name="fox_attention",
    )(pbf, pbf, pbf, c_tiles)


def _dsa_kernel(bq_ref, iq_lo_ref, iq_hi_ref, iw_ref, ik_ref, bk_ref, bv_ref, o_ref,
                key_ref, qg_ref, m_ref, l_ref, acc_ref, *, tq, tk, n_sel):
    b = pl.program_id(0)
    n_tiles = (b * tq) // tk + 1
    row_g = b * tq + lax.broadcasted_iota(I32, (tq, tk), 0)
    adm_end = (row_g // CHUNK + 1) * CHUNK
    col_l = lax.broadcasted_iota(I32, (tq, tk), 1)

    idx_scale = (IDX_HEADS ** -0.5) * (IDX_DIM ** -0.5)
    iw = iw_ref[...][:, 8:8 + IDX_HEADS] * idx_scale
    iq = jnp.concatenate([iq_lo_ref[...], iq_hi_ref[...]], axis=1)

    def score_tile(kt, _):
        start = pl.multiple_of(kt * tk, tk)
        ik = ik_ref[pl.ds(start, tk), :][:, :IDX_DIM]
        acc = jnp.zeros((tq, tk), F32)
        for h in range(IDX_HEADS):
            a = iq[:, h * IDX_DIM:(h + 1) * IDX_DIM]
            rel = lax.dot_general(a, ik, (((1,), (1,)), ((), ())), preferred_element_type=F32)
            acc = acc + jnp.maximum(rel, 0.0) * iw[:, h:h + 1]
        key_ref[kt] = jnp.where(kt * tk + col_l < adm_end, acc, -jnp.inf)
        return 0

    lax.fori_loop(0, n_tiles, score_tile, 0)

    def float_of(code_u):
        code = code_u ^ INT_MIN
        return pltpu.bitcast(jnp.where(code >= 0, code, code ^ 0x7FFFFFFF), F32)

    def count_ge(cand):
        def body(kt, part):
            ge = jnp.where(key_ref[kt] >= cand, 1, 0)
            for c in range(tk // 128):
                part = part + ge[:, c * 128:(c + 1) * 128]
            return part
        part = lax.fori_loop(0, n_tiles, body, jnp.zeros((tq, 128), I32))
        return jnp.sum(part, axis=1, keepdims=True)

    def search_cond(carry):
        i, _, _, pending = carry
        return (i < 32) & (pending > 0)

    def refine(i, t_u, hit):
        cand_u = t_u | lax.shift_left(jnp.int32(1), 31 - i)
        cnt = count_ge(float_of(cand_u))
        return jnp.where(cnt >= n_sel, cand_u, t_u), jnp.where(cnt == n_sel, 1, hit)

    def search_step(carry):
        i, t_u, hit, _ = carry
        t_u, hit = refine(i, t_u, hit)
        return i + 1, t_u, hit, jnp.sum(1 - hit)

    zeros = jnp.zeros((tq, 1), I32)
    lead = 12
    t_u, hit = lax.fori_loop(0, lead, lambda i, c: refine(i, *c), (zeros, zeros))
    _, t_u, _, _ = lax.while_loop(search_cond, search_step, (jnp.int32(lead), t_u, hit, jnp.sum(1 - hit)))
    thr = jnp.where(t_u == 0, jnp.finfo(F32).min, float_of(t_u))

    for j in range(DSA_KV_HEADS):
        for g in range(DSA_GROUP):
            hd = j * DSA_GROUP + g
            qg_ref[j, g * tq:(g + 1) * tq, :] = bq_ref[:, hd * HEAD_DIM:(hd + 1) * HEAD_DIM]
    _softmax_init(m_ref, l_ref, acc_ref)

    def attn_tile(kt, _):
        start = pl.multiple_of(kt * tk, tk)
        sel = key_ref[kt] >= thr
        for j in range(DSA_KV_HEADS):
            hs = slice(j * HEAD_DIM, (j + 1) * HEAD_DIM)
            k = bk_ref[pl.ds(start, tk), hs]
            v = bv_ref[pl.ds(start, tk), hs]
            s = lax.dot_general(qg_ref[j], k, (((1,), (1,)), ((), ())), preferred_element_type=F32)
            s = jnp.where(sel[None], s.reshape(DSA_GROUP, tq, tk), NEG_BIG).reshape(DSA_GROUP * tq, tk)
            _softmax_tile(s, v, m_ref, l_ref, acc_ref, j)
        return 0

    lax.fori_loop(0, n_tiles, attn_tile, 0)

    for j in range(DSA_KV_HEADS):
        o = _softmax_result(l_ref, acc_ref, j)
        for g in range(DSA_GROUP):
            hd = j * DSA_GROUP + g
            o_ref[:, hd * HEAD_DIM:(hd + 1) * HEAD_DIM] = o[g * tq:(g + 1) * tq].astype(o_ref.dtype)


def _dsa_attention(pbf, pg, s, tq, tk):
    n_sel = min(TOPK_MAX, s // 4)
    kvw = DSA_KV_HEADS * HEAD_DIM
    rows = DSA_GROUP * tq
    iqw = IDX_HEADS * IDX_DIM // 2
    return pl.pallas_call(
        functools.partial(_dsa_kernel, tq=tq, tk=tk, n_sel=n_sel),
        grid=(s // tq,),
        in_specs=[
            pl.BlockSpec((tq, MIX_B), lambda b: (b, COL_BQ // MIX_B)),
            pl.BlockSpec((tq, iqw), lambda b: (b, COL_IQ // iqw)),
            pl.BlockSpec((tq, iqw), lambda b: (b, COL_IQ // iqw + 1)),
            pl.BlockSpec((tq, 128), lambda b: (b, COL_SMALL // 128)),
            pl.BlockSpec((s, 128), lambda b: (0, COL_IK // 128)),
            pl.BlockSpec((s, kvw), lambda b: (0, COL_BK // kvw)),
            pl.BlockSpec((s, kvw), lambda b: (0, COL_BV // kvw)),
        ],
        out_specs=pl.BlockSpec((tq, MIX_B), lambda b: (b, 0)),
        out_shape=jax.ShapeDtypeStruct((s, MIX_B), BF16),
        scratch_shapes=[
            pltpu.VMEM((s // tk, tq, tk), F32),
            pltpu.VMEM((DSA_KV_HEADS, rows, HEAD_DIM), BF16),
            pltpu.VMEM((DSA_KV_HEADS, rows, 128), F32),
            pltpu.VMEM((DSA_KV_HEADS, rows, 128), F32),
            pltpu.VMEM((DSA_KV_HEADS, rows, HEAD_DIM), F32),
        ],
        compiler_params=_cparams(("arbitrary",), 56),
        name="dsa_attention",
    )(pbf, pbf, pbf, pg, pbf, pbf, pbf)


def _layer_norm(z, g, b):
    mu = jnp.mean(z, axis=-1, keepdims=True)
    zc = z - mu
    var = jnp.mean(zc * zc, axis=-1, keepdims=True)
    return zc * lax.rsqrt(var + LN_EPS) * g + b


def _merge_kernel(a_ref, b_ref, ga_ref, gb_ref, x_ref, wa_ref, wb_ref, wo_ref, g_ref, beta_ref,
                  wr_ref, br_ref, h_ref, eidx_ref, gate_ref, pos_ref, cnt_ref, carry_ref, *, alpha):
    i = pl.program_id(0)

    @pl.when(i == 0)
    def _():
        carry_ref[...] = jnp.zeros_like(carry_ref)

    ma = jnp.dot(a_ref[...], wa_ref[...], preferred_element_type=F32)
    mb = jnp.dot(b_ref[...], wb_ref[...], preferred_element_type=F32)
    merged = jax.nn.sigmoid(ga_ref[...]) * ma + jax.nn.sigmoid(gb_ref[...]) * mb
    y = jnp.dot(merged.astype(BF16), wo_ref[...], preferred_element_type=F32)
    h = _layer_norm(alpha * x_ref[...] + y, g_ref[...], beta_ref[...])
    h_ref[...] = h

    logits = jnp.dot(h, wr_ref[...], preferred_element_type=F32, precision=lax.Precision.HIGHEST) + br_ref[...]
    tm, ne = logits.shape
    lane = lax.broadcasted_iota(I32, (tm, ne), 1)
    lane_k = lax.broadcasted_iota(I32, (tm, TOP_K), 1)
    work = logits
    vals, sels = [], []
    eidx = jnp.zeros((tm, TOP_K), I32)
    onehot = jnp.zeros((tm, ne), F32)
    for k in range(TOP_K):
        mv = jnp.max(work, axis=1, keepdims=True)
        idx = jnp.min(jnp.where(work == mv, lane, ne), axis=1, keepdims=True)
        sel = lane == idx
        vals.append(mv)
        sels.append(sel)
        eidx = jnp.where(lane_k == k, idx, eidx)
        onehot = onehot + jnp.where(sel, 1.0, 0.0)
        work = jnp.where(sel, -jnp.inf, work)
    exps = [jnp.exp(v - vals[0]) for v in vals]
    denom = exps[0] + exps[1] + exps[2] + exps[3]
    gates = jnp.zeros((tm, TOP_K), F32)
    for k in range(TOP_K):
        gates = jnp.where(lane_k == k, exps[k] / denom, gates)

    r_i = lax.broadcasted_iota(I32, (tm, tm), 0)
    c_i = lax.broadcasted_iota(I32, (tm, tm), 1)
    lower = jnp.where(c_i < r_i, 1.0, 0.0).astype(BF16)
    rank = carry_ref[...] + jnp.dot(lower, onehot.astype(BF16), preferred_element_type=F32)
    pos = jnp.zeros((tm, TOP_K), I32)
    for k in range(TOP_K):
        pk = jnp.sum(jnp.where(sels[k], rank, 0.0), axis=1, keepdims=True).astype(I32)
        pos = jnp.where(lane_k == k, pk, pos)
    carry_ref[...] = carry_ref[...] + jnp.sum(onehot, axis=0, keepdims=True)

    eidx_ref[...] = eidx
    gate_ref[...] = gates
    pos_ref[...] = pos
    cnt_ref[...] = carry_ref[...].astype(I32)


def _merge(a_out, b_out, pg, x2d, wa, wb, wo, ln_g, ln_b, w_router, b_router, alpha):
    s, d = x2d.shape
    tm = min(256, s)
    full = lambda shape: pl.BlockSpec(shape, lambda i: (0,) * len(shape))
    return pl.pallas_call(
        functools.partial(_merge_kernel, alpha=alpha),
        grid=(s // tm,),
        in_specs=[
            pl.BlockSpec((tm, MIX_A), lambda i: (i, 0)),
            pl.BlockSpec((tm, MIX_B), lambda i: (i, 0)),
            pl.BlockSpec((tm, d), lambda i: (i, COL_GA // d)),
            pl.BlockSpec((tm, d), lambda i: (i, COL_GB // d)),
            pl.BlockSpec((tm, d), lambda i: (i, 0)),
            full((MIX_A, d)), full((MIX_B, d)), full((d, d)),
            full((1, d)), full((1, d)), full((d, N_EXPERTS)), full((1, N_EXPERTS)),
        ],
        out_specs=[
            pl.BlockSpec((tm, d), lambda i: (i, 0)),
            pl.BlockSpec((tm, TOP_K), lambda i: (i, 0)),
            pl.BlockSpec((tm, TOP_K), lambda i: (i, 0)),
            pl.BlockSpec((tm, TOP_K), lambda i: (i, 0)),
            full((1, N_EXPERTS)),
        ],
        out_shape=[
            jax.ShapeDtypeStruct((s, d), F32),
            jax.ShapeDtypeStruct((s, TOP_K), I32),
            jax.ShapeDtypeStruct((s, TOP_K), F32),
            jax.ShapeDtypeStruct((s, TOP_K), I32),
            jax.ShapeDtypeStruct((1, N_EXPERTS), I32),
        ],
        scratch_shapes=[pltpu.VMEM((1, N_EXPERTS), F32)],
        compiler_params=_cparams(("arbitrary",), 56),
        name="merge_ln_router",
    )(a_out, b_out, pg, pg, x2d, wa, wb, wo, ln_g, ln_b, w_router, b_router)


def _dispatch_kernel(dest_ref, gap_start_ref, gap_len_ref, total_ref, h_ref, xs_ref, stage_ref, zero_ref,
                     sem, zsem, *, max_slack):
    i = pl.program_id(0)
    last = pl.num_programs(0) - 1
    slot = lax.rem(i, 2)
    n = dest_ref.shape[0]

    @pl.when(i == 0)
    def _():
        zero_ref[...] = jnp.zeros_like(zero_ref)

        def gap_copy(e, k):
            return pltpu.make_async_copy(zero_ref.at[pl.ds(0, 1)], xs_ref.at[pl.ds(gap_start_ref[e] + k, 1)], zsem)

        n_slots = xs_ref.shape[0]

        def slack_copy(k):
            row = pl.multiple_of(total_ref[0] + k * MOE_CHUNK, MOE_CHUNK)
            return pltpu.make_async_copy(zero_ref, xs_ref.at[pl.ds(row, MOE_CHUNK)], zsem)

        def slack(action):
            for k in range(max_slack):
                @pl.when(total_ref[0] + k * MOE_CHUNK < n_slots)
                def _():
                    action(slack_copy(k))

        def issue(e, _):
            lax.fori_loop(0, gap_len_ref[e], lambda k, c: (gap_copy(e, k).start(), c)[1], 0)
            return 0

        def settle(e, _):
            lax.fori_loop(0, gap_len_ref[e], lambda k, c: (gap_copy(e, k).wait(), c)[1], 0)
            return 0

        lax.fori_loop(0, N_EXPERTS, issue, 0)
        slack(lambda cp: cp.start())
        lax.fori_loop(0, N_EXPERTS, settle, 0)
        slack(lambda cp: cp.wait())

    def drain(sl):
        pltpu.make_async_copy(xs_ref.at[pl.ds(0, n)], xs_ref.at[pl.ds(0, n)], sem.at[sl]).wait()

    @pl.when(i >= 2)
    def _():
        drain(slot)

    stage_ref[slot] = h_ref[...]

    def scatter(sl):
        def start(g, _):
            base = pl.multiple_of(g * ROW_TILE, ROW_TILE)
            for u in range(ROW_TILE * TOP_K):
                r = u // TOP_K
                pltpu.make_async_copy(stage_ref.at[sl, pl.ds(base + r, 1)],
                                      xs_ref.at[pl.ds(dest_ref[g * (ROW_TILE * TOP_K) + u], 1)],
                                      sem.at[sl]).start(priority=u % 2)
            return 0
        lax.fori_loop(0, n // (ROW_TILE * TOP_K), start, 0)

    for parity in range(2):
        @pl.when(slot == parity)
        def _():
            scatter(parity)

    @pl.when(i == last)
    def _():
        drain(slot)

        @pl.when(i >= 1)
        def _():
            drain(1 - slot)


def _dispatch(h, dest_flat, gap_start, gap_len, total, n_slots):
    s, d = h.shape
    tm = min(128, s)
    smem = pl.BlockSpec(memory_space=pltpu.SMEM)
    max_slack = (n_slots - s * TOP_K + MOE_CHUNK - 1) // MOE_CHUNK
    return pl.pallas_call(
        functools.partial(_dispatch_kernel, max_slack=max_slack),
        grid=(s // tm,),
        in_specs=[
            pl.BlockSpec((tm * TOP_K,), lambda i: (i,), memory_space=pltpu.SMEM),
            smem, smem, smem,
            pl.BlockSpec((tm, d), lambda i: (i, 0)),
        ],
        out_specs=pl.BlockSpec(memory_space=pl.ANY),
        out_shape=jax.ShapeDtypeStruct((n_slots, d), h.dtype),
        scratch_shapes=[pltpu.VMEM((2, tm, d), h.dtype), pltpu.VMEM((MOE_CHUNK, d), h.dtype),
                        pltpu.SemaphoreType.DMA((2,)), pltpu.SemaphoreType.DMA(())],
        compiler_params=_cparams(("arbitrary",), 32),
        name="moe_dispatch",
    )(dest_flat, gap_start, gap_len, total, h)


ROW_DMA_PRIORITY = 1


def _expert_rows_pipeline(n_chunks, prologue, in_copy, out_copy, compute):
    @pl.when(n_chunks > 0)
    def _():
        in_copy(0, 0).start(priority=ROW_DMA_PRIORITY)
        prologue()

        def body(c, _):
            slot = lax.rem(c, 2)
            in_copy(c, slot).wait()

            @pl.when(c + 1 < n_chunks)
            def _():
                in_copy(c + 1, 1 - slot).start(priority=ROW_DMA_PRIORITY)

            @pl.when(c >= 2)
            def _():
                out_copy(c - 2, slot).wait()

            compute(slot)
            out_copy(c, slot).start(priority=ROW_DMA_PRIORITY)
            return 0

        lax.fori_loop(0, n_chunks, body, 0)

        @pl.when(n_chunks >= 2)
        def _():
            out_copy(n_chunks - 2, lax.rem(n_chunks, 2)).wait()

        out_copy(n_chunks - 1, lax.rem(n_chunks - 1, 2)).wait()


def _chunk_rows(start_ref, e, c, n_slots):
    del n_slots
    return pl.ds(pl.multiple_of(start_ref[e] + c * MOE_CHUNK, MOE_ALIGN), MOE_CHUNK)


def _gate_up_kernel(start_ref, cnt_ref, x_hbm, w_hbm, b_ref, act_hbm,
                    w_stage, w_sc, xbuf, obuf, sem_in, sem_out, sem_w):
    e = pl.program_id(0)
    n_slots = x_hbm.shape[0]
    n_chunks = (cnt_ref[e] + MOE_CHUNK - 1) // MOE_CHUNK
    d_ff = obuf.shape[2]

    def w_copy(idx):
        return pltpu.make_async_copy(w_hbm.at[idx], w_stage, sem_w)

    @pl.when(e == 0)
    def _():
        w_copy(0).start()

    def fetch_weights():
        @pl.when(e < N_EXPERTS)
        def _():
            w_copy(e).wait()
            w_sc[...] = w_stage[...].astype(BF16)

            @pl.when(e + 1 < N_EXPERTS)
            def _():
                w_copy(e + 1).start()

    def in_copy(c, slot):
        return pltpu.make_async_copy(x_hbm.at[_chunk_rows(start_ref, e, c, n_slots)], xbuf.at[slot], sem_in.at[slot])

    def out_copy(c, slot):
        return pltpu.make_async_copy(obuf.at[slot], act_hbm.at[_chunk_rows(start_ref, e, c, n_slots)],
                                     sem_out.at[slot])

    def compute(slot):
        x = xbuf[slot].astype(BF16)
        cw = 256
        for c in range(d_ff // cw):
            gs = slice(c * cw, (c + 1) * cw)
            us = slice(d_ff + c * cw, d_ff + (c + 1) * cw)
            g = jnp.dot(x, w_sc[:, gs], preferred_element_type=F32) + b_ref[:, gs]
            u = jnp.dot(x, w_sc[:, us], preferred_element_type=F32) + b_ref[:, us]
            g = jnp.minimum(g, SWIGLU_LIMIT)
            u = jnp.clip(u, -SWIGLU_LIMIT, SWIGLU_LIMIT)
            obuf[slot, :, gs] = (g * jax.nn.sigmoid(SWIGLU_ALPHA * g) * (u + 1.0)).astype(obuf.dtype)

    @pl.when(n_chunks == 0)
    def _():
        fetch_weights()

    _expert_rows_pipeline(n_chunks, fetch_weights, in_copy, out_copy, compute)


def _down_kernel(start_ref, cnt_ref, a_hbm, wd_ref, bd_ref, y_hbm, wd_sc, abuf, obuf, sem_in, sem_out):
    e = pl.program_id(0)
    n_slots = y_hbm.shape[0]
    n_chunks = (cnt_ref[e] + MOE_CHUNK - 1) // MOE_CHUNK

    def cast_weights():
        wd_sc[...] = wd_ref[...].astype(BF16)

    def in_copy(c, slot):
        return pltpu.make_async_copy(a_hbm.at[_chunk_rows(start_ref, e, c, n_slots)], abuf.at[slot],
                                     sem_in.at[slot])

    def out_copy(c, slot):
        return pltpu.make_async_copy(obuf.at[slot], y_hbm.at[_chunk_rows(start_ref, e, c, n_slots)],
                                     sem_out.at[slot])

    def compute(slot):
        cw = 512
        for c in range(obuf.shape[2] // cw):
            cs = slice(c * cw, (c + 1) * cw)
            obuf[slot, :, cs] = jnp.dot(abuf[slot], wd_sc[:, cs], preferred_element_type=F32) + bd_ref[:, cs]

    _expert_rows_pipeline(n_chunks, cast_weights, in_copy, out_copy, compute)


def _experts(xs, starts, counts, w_gate_up, b_gate_up, w_down, b_down):
    n_slots, d = xs.shape
    any_spec = pl.BlockSpec(memory_space=pl.ANY)
    dma_sems = pltpu.SemaphoreType.DMA((2,))
    bgu = b_gate_up.reshape(N_EXPERTS, 1, 2 * D_FF)
    n_groups = starts.shape[0]
    wi = lambda e: jnp.minimum(e, N_EXPERTS - 1)
    act = pl.pallas_call(
        _gate_up_kernel,
        grid_spec=pltpu.PrefetchScalarGridSpec(
            num_scalar_prefetch=2,
            grid=(n_groups,),
            in_specs=[
                any_spec,
                any_spec,
                pl.BlockSpec((None, 1, 2 * D_FF), lambda e, st, ct: (wi(e), 0, 0)),
            ],
            out_specs=any_spec,
            scratch_shapes=[pltpu.VMEM((d, 2 * D_FF), F32), pltpu.VMEM((d, 2 * D_FF), BF16),
                            pltpu.VMEM((2, MOE_CHUNK, d), xs.dtype), pltpu.VMEM((2, MOE_CHUNK, D_FF), BF16),
                            dma_sems, dma_sems, pltpu.SemaphoreType.DMA(())],
        ),
        out_shape=jax.ShapeDtypeStruct((n_slots, D_FF), BF16),
        compiler_params=_cparams(("arbitrary",), 60),
        name="moe_gate_up",
    )(starts, counts, xs, w_gate_up, bgu)
    bd = b_down.reshape(N_EXPERTS, 1, d)
    ys = pl.pallas_call(
        _down_kernel,
        grid_spec=pltpu.PrefetchScalarGridSpec(
            num_scalar_prefetch=2,
            grid=(n_groups,),
            in_specs=[
                any_spec,
                pl.BlockSpec((None, D_FF, d), lambda e, st, ct: (wi(e), 0, 0)),
                pl.BlockSpec((None, 1, d), lambda e, st, ct: (wi(e), 0, 0)),
            ],
            out_specs=any_spec,
            scratch_shapes=[pltpu.VMEM((D_FF, d), BF16),
                            pltpu.VMEM((2, MOE_CHUNK, D_FF), BF16), pltpu.VMEM((2, MOE_CHUNK, d), F32),
                            dma_sems, dma_sems],
        ),
        out_shape=jax.ShapeDtypeStruct((n_slots, d), F32),
        compiler_params=_cparams(("arbitrary",), 56),
        name="moe_down",
    )(starts, counts, act, w_down, bd)
    return ys


def _combine_kernel(dest_ref, dest_next_ref, gate_ref, h_ref, g_ref, beta_ref, ys_ref, o_ref, buf_ref, sem,
                    *, alpha):
    i = pl.program_id(0)
    last = pl.num_programs(0) - 1
    slot = lax.rem(i, 2)
    n = dest_ref.shape[0]

    def gather(dref, sl):
        def start(g, _):
            base = pl.multiple_of(g * ROW_TILE, ROW_TILE)
            for u in range(ROW_TILE * TOP_K):
                r, k = divmod(u, TOP_K)
                pltpu.make_async_copy(ys_ref.at[pl.ds(dref[g * (ROW_TILE * TOP_K) + u], 1)],
                                      buf_ref.at[sl, k, pl.ds(base + r, 1)],
                                      sem.at[sl]).start(priority=u % 2)
            return 0
        lax.fori_loop(0, n // (ROW_TILE * TOP_K), start, 0)

    for parity in range(2):
        @pl.when((i == 0) & (slot == parity))
        def _():
            gather(dest_ref, parity)

        @pl.when((i < last) & (slot == parity))
        def _():
            gather(dest_next_ref, 1 - parity)

    pltpu.make_async_copy(buf_ref.at[slot], buf_ref.at[slot], sem.at[slot]).wait()
    gates = gate_ref[...]
    y = gates[:, 0:1] * buf_ref[slot, 0]
    for k in range(1, TOP_K):
        y = y + gates[:, k:k + 1] * buf_ref[slot, k]
    o_ref[...] = _layer_norm(alpha * h_ref[...] + y, g_ref[...], beta_ref[...])


def _combine(ys, dest_flat, gates, h, ln_g, ln_b, alpha):
    s, d = h.shape
    tm = min(128, s)
    n_steps = s // tm
    return pl.pallas_call(
        functools.partial(_combine_kernel, alpha=alpha),
        grid=(n_steps,),
        in_specs=[
            pl.BlockSpec((tm * TOP_K,), lambda i: (i,), memory_space=pltpu.SMEM),
            pl.BlockSpec((tm * TOP_K,), lambda i: (jnp.minimum(i + 1, n_steps - 1),), memory_space=pltpu.SMEM),
            pl.BlockSpec((tm, TOP_K), lambda i: (i, 0)),
            pl.BlockSpec((tm, d), lambda i: (i, 0)),
            pl.BlockSpec((1, d), lambda i: (0, 0)),
            pl.BlockSpec((1, d), lambda i: (0, 0)),
            pl.BlockSpec(memory_space=pl.ANY),
        ],
        out_specs=pl.BlockSpec((tm, d), lambda i: (i, 0)),
        out_shape=jax.ShapeDtypeStruct((s, d), F32),
        scratch_shapes=[pltpu.VMEM((2, TOP_K, tm, d), F32), pltpu.SemaphoreType.DMA((2,))],
        compiler_params=_cparams(("arbitrary",), 32),
        name="moe_combine",
    )(dest_flat, dest_flat, gates, h, ln_g, ln_b, ys)


def _layer(x2d, w_in, b_forget, w_branch_a, w_branch_b, w_out, ln1_g, ln1_b, w_router, b_router,
           w_gate_up, b_gate_up, w_down, b_down, ln2_g, ln2_b, tables):
    s, d = x2d.shape
    alpha = (2.0 * DEPTH) ** 0.25
    scale = HEAD_DIM ** -0.5 * LOG2E

    w_bf, w_g = _prepare_w_in(w_in, scale)
    pbf, pg = _project(x2d, w_bf, tables, w_g)

    fox_tq = min(512, s)
    fox_tk = min(512, s)
    af_t = pg[:, COL_SMALL:COL_SMALL + FOX_HEADS].T
    c_t = _forget_cumsum(af_t, b_forget)
    c_tiles = c_t.reshape(FOX_HEADS, s // fox_tk, 1, fox_tk)
    a_out = _fox_attention(pbf, c_tiles, s, fox_tq, fox_tk, 4)

    b_out = _dsa_attention(pbf, pg, s, min(256, s), min(512, s))

    h, eidx, gates, pos, counts = _merge(
        a_out, b_out, pg, x2d, w_branch_a.astype(BF16), w_branch_b.astype(BF16), w_out.astype(BF16),
        ln1_g.reshape(1, d), ln1_b.reshape(1, d), w_router, b_router.reshape(1, N_EXPERTS), alpha)

    counts = counts.reshape(N_EXPERTS).astype(I32)
    aligned = (counts + MOE_ALIGN - 1) // MOE_ALIGN * MOE_ALIGN
    ends = jnp.cumsum(aligned).astype(I32)
    starts = ends - aligned
    bound = s * TOP_K + N_EXPERTS * MOE_ALIGN + 2 * MOE_CHUNK
    n_slots = (bound + MOE_CHUNK - 1) // MOE_CHUNK * MOE_CHUNK
    experts = jnp.arange(N_EXPERTS, dtype=I32)
    start_of = jnp.sum(jnp.where(eidx[..., None] == experts, starts, 0), axis=-1)
    dest = (start_of + pos).reshape(s * TOP_K).astype(I32)
    total = (ends[-1:] + MOE_CHUNK - 1) // MOE_CHUNK * MOE_CHUNK
    gap_len = aligned - counts + jnp.where(experts == N_EXPERTS - 1, total[0] - ends[-1], 0)
    group_starts = jnp.concatenate([starts, total])
    group_counts = jnp.concatenate([counts + jnp.where(experts == N_EXPERTS - 1, gap_len, 0), n_slots - total])

    xs = _dispatch(h, dest, starts + counts, gap_len, total, n_slots)
    ys = _experts(xs, group_starts, group_counts, w_gate_up, b_gate_up, w_down, b_down)
    return _combine(ys, dest, gates, h, ln2_g.reshape(1, d), ln2_b.reshape(1, d), alpha)


def kernel(x, w_in, b_forget, w_branch_a, w_branch_b, w_out, ln1_g, ln1_b, w_router, b_router,
           w_gate_up, b_gate_up, w_down, b_down, ln2_g, ln2_b):
    bsz, s, d = x.shape
    tables = _rope_tables(s)
    outs = []
    for bi in range(bsz):
        xb = x[bi]
        for l in range(DEPTH):
            xb = _layer(xb, w_in[l], b_forget[l], w_branch_a[l], w_branch_b[l], w_out[l], ln1_g[l], ln1_b[l],
                        w_router[l], b_router[l], w_gate_up[l], b_gate_up[l], w_down[l], b_down[l],
                        ln2_g[l], ln2_b[l], tables)
        outs.append(xb)
    return outs[0][None] if bsz == 1 else jnp.stack(outs)
```

```python
import functools

import numpy as np
import jax
import jax.numpy as jnp
from jax import lax
from jax.experimental import pallas as pl
from jax.experimental.pallas import tpu as pltpu

F32 = jnp.float32
BF16 = jnp.bfloat16
I32 = jnp.int32

D_MODEL = 2048
DEPTH = 1
CHUNK = 64
HEAD_DIM = 128
FOX_HEADS = 8
DSA_HEADS = 8
DSA_KV_HEADS = 2
DSA_GROUP = DSA_HEADS // DSA_KV_HEADS
IDX_HEADS = 16
IDX_DIM = 64
TOPK_MAX = 256
ROPE_THETA = 500000.0
ROT_FRACTION_DEN = 4
MIX_A = FOX_HEADS * HEAD_DIM
MIX_B = DSA_HEADS * HEAD_DIM
N_EXPERTS = 32
TOP_K = 4
D_FF = D_MODEL
SWIGLU_ALPHA = 1.702
SWIGLU_LIMIT = 7.0
LN_EPS = 1e-5

MIB = 1024 * 1024
NEG_BIG = -1e30
LOG2E = 1.4426950408889634
NT_DIMS = (((1,), (1,)), ((), ()))
INT_MIN = -(2 ** 31)

COL_AQ = 0
COL_AK = 1024
COL_AV = 2048
COL_BQ = 3072
COL_BK = 4096
COL_BV = 4352
COL_IQ = 4608
COL_IK = 5632
N_PBF = 5888
PROJ_TN = 256
ROPE_HEAD_TILES = (12, 13, 14, 15, 16)
ROPE_IDX_TILES = (18, 19, 20, 21, 22)
COL_GA = 0
COL_GB = 2048
COL_SMALL = 4096
N_PG = 4224
PG_TN = 384

MOE_CHUNK = 256
MOE_ALIGN = 16
ROW_TILE = 8


def _cparams(dims, vmem_mib):
    return pltpu.CompilerParams(dimension_semantics=dims, vmem_limit_bytes=vmem_mib * MIB)


IN_SIZES = (MIX_A, MIX_A, MIX_A, FOX_HEADS, MIX_B, DSA_KV_HEADS * HEAD_DIM, DSA_KV_HEADS * HEAD_DIM,
            IDX_HEADS * IDX_DIM, IDX_DIM, IDX_HEADS, D_MODEL, D_MODEL)
IN_OFF = dict(zip(("aq", "ak", "av", "af", "bq", "bk", "bv", "iq", "ik", "iw", "ga", "gb", "end"),
                  np.concatenate([[0], np.cumsum(IN_SIZES)]).tolist()))


def _prep_kernel(w_ref, wbf_ref, wg_ref, *, scale):
    o = IN_OFF
    width = w_ref.shape[1]

    def rows(a, b, mult=None):
        v = w_ref[a:b, :]
        return v if mult is None else v * mult

    wbf_ref[COL_AQ:COL_AK, :] = rows(o["aq"], o["ak"], scale).astype(BF16)
    wbf_ref[COL_AK:COL_BQ, :] = rows(o["ak"], o["af"]).astype(BF16)
    wbf_ref[COL_BQ:COL_BK, :] = rows(o["bq"], o["bk"], scale).astype(BF16)
    wbf_ref[COL_BK:COL_IK, :] = rows(o["bk"], o["ik"]).astype(BF16)
    wbf_ref[COL_IK:N_PBF, :] = jnp.concatenate(
        [rows(o["ik"], o["iw"]), jnp.zeros((N_PBF - COL_IK - IDX_DIM, width), F32)], axis=0).astype(BF16)
    wg_ref[COL_GA:COL_SMALL, :] = rows(o["ga"], o["end"]).astype(BF16)
    wg_ref[COL_SMALL:N_PG, :] = jnp.concatenate(
        [rows(o["af"], o["bq"]), rows(o["iw"], o["ga"]),
         jnp.zeros((N_PG - COL_SMALL - FOX_HEADS - IDX_HEADS, width), F32)], axis=0).astype(BF16)


def _prepare_w_in(w_in_t, scale):
    n_in, d = w_in_t.shape
    tc = 256
    return pl.pallas_call(
        functools.partial(_prep_kernel, scale=scale),
        grid=(d // tc,),
        in_specs=[pl.BlockSpec((n_in, tc), lambda i: (0, i))],
        out_specs=[pl.BlockSpec((N_PBF, tc), lambda i: (0, i)), pl.BlockSpec((N_PG, tc), lambda i: (0, i))],
        out_shape=[jax.ShapeDtypeStruct((N_PBF, d), BF16), jax.ShapeDtypeStruct((N_PG, d), BF16)],
        compiler_params=_cparams(("arbitrary",), 48),
        name="w_in_prep",
    )(w_in_t)


def _tile_in(j, tiles):
    cond = j == tiles[0]
    for t in tiles[1:]:
        cond = cond | (j == t)
    return cond


def _proj_rope_kernel(x_ref, w_ref, tab_ref, o_ref, xb_ref):
    j = pl.program_id(1)

    @pl.when(j == 0)
    def _():
        xb_ref[...] = x_ref[...].astype(BF16)

    acc = lax.dot_general(xb_ref[...], w_ref[...], NT_DIMS, preferred_element_type=F32)
    tn = acc.shape[1]
    is_head = _tile_in(j, ROPE_HEAD_TILES)
    is_idx = _tile_in(j, ROPE_IDX_TILES)

    def rope(shift):
        c = tab_ref[0, 0]
        s_prev = tab_ref[0, 1]
        s_next = tab_ref[0, 2]
        out = acc * c + pltpu.roll(acc, shift, 1) * s_prev + pltpu.roll(acc, tn - shift, 1) * s_next
        o_ref[...] = out.astype(o_ref.dtype)

    @pl.when(is_head)
    def _():
        rope(HEAD_DIM // ROT_FRACTION_DEN // 2)

    @pl.when(is_idx)
    def _():
        rope(IDX_DIM // ROT_FRACTION_DEN // 2)

    @pl.when(jnp.logical_not(is_head | is_idx))
    def _():
        o_ref[...] = acc.astype(o_ref.dtype)


def _proj_plain_kernel(x_ref, w_ref, o_ref, xb_ref):
    j = pl.program_id(1)

    @pl.when(j == 0)
    def _():
        xb_ref[...] = x_ref[...].astype(BF16)

    o_ref[...] = lax.dot_general(xb_ref[...], w_ref[...], NT_DIMS, preferred_element_type=F32).astype(o_ref.dtype)


def _rope_group(j):
    return jnp.where(_tile_in(j, ROPE_IDX_TILES), 1, 0)


def _project(x2d, w_bf, tables, w_g):
    s, d = x2d.shape
    tm = min(1024, s)
    pbf = pl.pallas_call(
        _proj_rope_kernel,
        grid=(s // tm, N_PBF // PROJ_TN),
        in_specs=[
            pl.BlockSpec((tm, d), lambda i, j: (i, 0)),
            pl.BlockSpec((PROJ_TN, d), lambda i, j: (j, 0)),
            pl.BlockSpec((1, 3, tm, PROJ_TN), lambda i, j: (_rope_group(j), 0, i, 0)),
        ],
        out_specs=pl.BlockSpec((tm, PROJ_TN), lambda i, j: (i, j)),
        out_shape=jax.ShapeDtypeStruct((s, N_PBF), BF16),
        scratch_shapes=[pltpu.VMEM((tm, d), BF16)],
        compiler_params=_cparams(("arbitrary", "arbitrary"), 48),
        name="proj_bf16",
    )(x2d, w_bf, tables)
    pg = pl.pallas_call(
        _proj_plain_kernel,
        grid=(s // tm, N_PG // PG_TN),
        in_specs=[
            pl.BlockSpec((tm, d), lambda i, j: (i, 0)),
            pl.BlockSpec((PG_TN, d), lambda i, j: (j, 0)),
        ],
        out_specs=pl.BlockSpec((tm, PG_TN), lambda i, j: (i, j)),
        out_shape=jax.ShapeDtypeStruct((s, N_PG), F32),
        scratch_shapes=[pltpu.VMEM((tm, d), BF16)],
        compiler_params=_cparams(("arbitrary", "arbitrary"), 48),
        name="proj_f32",
    )(x2d, w_g)
    return pbf, pg


def _rope_tables(s):
    pos = jnp.arange(s, dtype=F32)

    def one(period):
        rot = period // ROT_FRACTION_DEN
        half = rot // 2
        inv = jnp.power(ROPE_THETA, -jnp.arange(0, rot, 2, dtype=F32) / rot)
        ang = pos[:, None] * inv[None, :]
        cos, sin = jnp.cos(ang), jnp.sin(ang)
        zero = jnp.zeros((s, period - rot), F32)
        c = jnp.concatenate([cos, cos, jnp.ones((s, period - rot), F32)], axis=1)
        s_prev = jnp.concatenate([jnp.zeros((s, half), F32), sin, zero], axis=1)
        s_next = jnp.concatenate([-sin, jnp.zeros((s, half), F32), zero], axis=1)
        reps = PROJ_TN // period
        return jnp.stack([jnp.tile(c, (1, reps)), jnp.tile(s_prev, (1, reps)), jnp.tile(s_next, (1, reps))])

    return jnp.stack([one(HEAD_DIM), one(IDX_DIM)])


def _cumsum_kernel(af_ref, bf_ref, c_ref, carry_ref):
    i = pl.program_id(0)

    @pl.when(i == 0)
    def _():
        carry_ref[...] = jnp.zeros_like(carry_ref)

    z = af_ref[...] + bf_ref[...]
    logf = jnp.minimum(z, 0.0) - jnp.log1p(jnp.exp(-jnp.abs(z)))
    t = z.shape[1]
    row = lax.broadcasted_iota(I32, (t, t), 0)
    col = lax.broadcasted_iota(I32, (t, t), 1)
    upper = (row <= col).astype(F32)
    c = jnp.dot(logf, upper, preferred_element_type=F32, precision=lax.Precision.HIGHEST) + carry_ref[...]
    c_ref[...] = c * LOG2E
    carry_ref[...] = c[:, t - 1:t]


def _forget_cumsum(af_t, b_forget):
    h, s = af_t.shape
    t = min(512, s)
    return pl.pallas_call(
        _cumsum_kernel,
        grid=(s // t,),
        in_specs=[pl.BlockSpec((h, t), lambda i: (0, i)), pl.BlockSpec((h, 1), lambda i: (0, 0))],
        out_specs=pl.BlockSpec((h, t), lambda i: (0, i)),
        out_shape=jax.ShapeDtypeStruct((h, s), F32),
        scratch_shapes=[pltpu.VMEM((h, 1), F32)],
        compiler_params=_cparams(("arbitrary",), 32),
        name="forget_cumsum",
    )(af_t, b_forget.reshape(h, 1).astype(F32))


def _softmax_init(m_ref, l_ref, acc_ref):
    m_ref[...] = jnp.full(m_ref.shape, NEG_BIG, F32)
    l_ref[...] = jnp.zeros(l_ref.shape, F32)
    acc_ref[...] = jnp.zeros(acc_ref.shape, F32)


def _softmax_tile(s, v, m_ref, l_ref, acc_ref, i):
    reps = s.shape[1] // 128
    m_old = m_ref[i]
    m_new = jnp.maximum(m_old, jnp.max(s, axis=1, keepdims=True))
    p = jnp.exp2(s - jnp.tile(m_new, (1, reps)))
    alpha = jnp.exp2(m_old - m_new)
    psum = p[:, :128]
    for c in range(1, reps):
        psum = psum + p[:, c * 128:(c + 1) * 128]
    l_ref[i] = alpha * l_ref[i] + psum
    acc_ref[i] = alpha * acc_ref[i] + jnp.dot(p.astype(BF16), v, preferred_element_type=F32)
    m_ref[i] = m_new


def _softmax_result(l_ref, acc_ref, i):
    return acc_ref[i] / jnp.sum(l_ref[i], axis=1, keepdims=True)


def _fox_kernel(q_ref, k_ref, v_ref, c_ref, o_ref, m_ref, l_ref, acc_ref, *, tq, tk, nh):
    qi = pl.program_id(1)
    _softmax_init(m_ref, l_ref, acc_ref)

    def tile(kt, masked):
        start = pl.multiple_of(kt * tk, tk)
        if masked:
            row = qi * tq + lax.broadcasted_iota(I32, (tq, tk), 0)
            col = kt * tk + lax.broadcasted_iota(I32, (tq, tk), 1)
            causal = col <= row
        for h in range(nh):
            hs = slice(h * HEAD_DIM, (h + 1) * HEAD_DIM)
            k = k_ref[pl.ds(start, tk), hs]
            v = v_ref[pl.ds(start, tk), hs]
            s = lax.dot_general(q_ref[:, hs], k, (((1,), (1,)), ((), ())), preferred_element_type=F32)
            s = s - c_ref[h, kt]
            if masked:
                s = jnp.where(causal, s, NEG_BIG)
            _softmax_tile(s, v, m_ref, l_ref, acc_ref, h)

    n_full = (qi * tq) // tk

    def body(kt, _):
        tile(kt, False)
        return 0

    lax.fori_loop(0, n_full, body, 0)
    for t in range((tq + tk - 1) // tk):
        tile(n_full + t, True)
    for h in range(nh):
        o_ref[:, h * HEAD_DIM:(h + 1) * HEAD_DIM] = _softmax_result(l_ref, acc_ref, h).astype(o_ref.dtype)


def _fox_attention(pbf, c_tiles, s, tq, tk, nh):
    cb = nh * HEAD_DIM
    return pl.pallas_call(
        functools.partial(_fox_kernel, tq=tq, tk=tk, nh=nh),
        grid=(FOX_HEADS // nh, s // tq),
        in_specs=[
            pl.BlockSpec((tq, cb), lambda g, i: (i, COL_AQ // cb + g)),
            pl.BlockSpec((s, cb), lambda g, i: (0, COL_AK // cb + g)),
            pl.BlockSpec((s, cb), lambda g, i: (0, COL_AV // cb + g)),
            pl.BlockSpec((nh, s // tk, 1, tk), lambda g, i: (g, 0, 0, 0)),
        ],
        out_specs=pl.BlockSpec((tq, cb), lambda g, i: (i, g)),
        out_shape=jax.ShapeDtypeStruct((s, MIX_A), BF16),
        scratch_shapes=[
            pltpu.VMEM((nh, tq, 128), F32),
            pltpu.VMEM((nh, tq, 128), F32),
            pltpu.VMEM((nh, tq, HEAD_DIM), F32),
        ],
        compiler_params=_cparams(("arbitrary", "arbitrary"), 56),
        name="fox_attention",
    )(pbf, pbf, pbf, c_tiles)


def _dsa_kernel(bq_ref, iq_lo_ref, iq_hi_ref, iw_ref, ik_ref, bk_ref, bv_ref, o_ref,
                key_ref, qg_ref, m_ref, l_ref, acc_ref, *, tq, tk, n_sel):
    b = pl.program_id(0)
    n_tiles = (b * tq) // tk + 1
    row_g = b * tq + lax.broadcasted_iota(I32, (tq, tk), 0)
    adm_end = (row_g // CHUNK + 1) * CHUNK
    col_l = lax.broadcasted_iota(I32, (tq, tk), 1)

    idx_scale = (IDX_HEADS ** -0.5) * (IDX_DIM ** -0.5)
    iw = iw_ref[...][:, 8:8 + IDX_HEADS] * idx_scale
    iq = jnp.concatenate([iq_lo_ref[...], iq_hi_ref[...]], axis=1)

    def score_tile(kt, _):
        start = pl.multiple_of(kt * tk, tk)
        ik = ik_ref[pl.ds(start, tk), :][:, :IDX_DIM]
        acc = jnp.zeros((tq, tk), F32)
        for h in range(IDX_HEADS):
            a = iq[:, h * IDX_DIM:(h + 1) * IDX_DIM]
            rel = lax.dot_general(a, ik, (((1,), (1,)), ((), ())), preferred_element_type=F32)
            acc = acc + jnp.maximum(rel, 0.0) * iw[:, h:h + 1]
        key_ref[kt] = jnp.where(kt * tk + col_l < adm_end, acc, -jnp.inf)
        return 0

    lax.fori_loop(0, n_tiles, score_tile, 0)

    def float_of(code_u):
        code = code_u ^ INT_MIN
        return pltpu.bitcast(jnp.where(code >= 0, code, code ^ 0x7FFFFFFF), F32)

    def count_ge(cand):
        def body(kt, part):
            ge = jnp.where(key_ref[kt] >= cand, 1, 0)
            for c in range(tk // 128):
                part = part + ge[:, c * 128:(c + 1) * 128]
            return part
        part = lax.fori_loop(0, n_tiles, body, jnp.zeros((tq, 128), I32))
        return jnp.sum(part, axis=1, keepdims=True)

    def search_cond(carry):
        i, _, _, pending = carry
        return (i < 32) & (pending > 0)

    def refine(i, t_u, hit):
        cand_u = t_u | lax.shift_left(jnp.int32(1), 31 - i)
        cnt = count_ge(float_of(cand_u))
        return jnp.where(cnt >= n_sel, cand_u, t_u), jnp.where(cnt == n_sel, 1, hit)

    def search_step(carry):
        i, t_u, hit, _ = carry
        t_u, hit = refine(i, t_u, hit)
        return i + 1, t_u, hit, jnp.sum(1 - hit)

    zeros = jnp.zeros((tq, 1), I32)
    lead = 12
    t_u, hit = lax.fori_loop(0, lead, lambda i, c: refine(i, *c), (zeros, zeros))
    _, t_u, _, _ = lax.while_loop(search_cond, search_step, (jnp.int32(lead), t_u, hit, jnp.sum(1 - hit)))
    thr = jnp.where(t_u == 0, jnp.finfo(F32).min, float_of(t_u))

    for j in range(DSA_KV_HEADS):
        for g in range(DSA_GROUP):
            hd = j * DSA_GROUP + g
            qg_ref[j, g * tq:(g + 1) * tq, :] = bq_ref[:, hd * HEAD_DIM:(hd + 1) * HEAD_DIM]
    _softmax_init(m_ref, l_ref, acc_ref)

    def attn_tile(kt, _):
        start = pl.multiple_of(kt * tk, tk)
        sel = key_ref[kt] >= thr
        for j in range(DSA_KV_HEADS):
            hs = slice(j * HEAD_DIM, (j + 1) * HEAD_DIM)
            k = bk_ref[pl.ds(start, tk), hs]
            v = bv_ref[pl.ds(start, tk), hs]
            s = lax.dot_general(qg_ref[j], k, (((1,), (1,)), ((), ())), preferred_element_type=F32)
            s = jnp.where(sel[None], s.reshape(DSA_GROUP, tq, tk), NEG_BIG).reshape(DSA_GROUP * tq, tk)
            _softmax_tile(s, v, m_ref, l_ref, acc_ref, j)
        return 0

    lax.fori_loop(0, n_tiles, attn_tile, 0)

    for j in range(DSA_KV_HEADS):
        o = _softmax_result(l_ref, acc_ref, j)
        for g in range(DSA_GROUP):
            hd = j * DSA_GROUP + g
            o_ref[:, hd * HEAD_DIM:(hd + 1) * HEAD_DIM] = o[g * tq:(g + 1) * tq].astype(o_ref.dtype)


def _dsa_attention(pbf, pg, s, tq, tk):
    n_sel = min(TOPK_MAX, s // 4)
    kvw = DSA_KV_HEADS * HEAD_DIM
    rows = DSA_GROUP * tq
    iqw = IDX_HEADS * IDX_DIM // 2
    return pl.pallas_call(
        functools.partial(_dsa_kernel, tq=tq, tk=tk, n_sel=n_sel),
        grid=(s // tq,),
        in_specs=[
            pl.BlockSpec((tq, MIX_B), lambda b: (b, COL_BQ // MIX_B)),
            pl.BlockSpec((tq, iqw), lambda b: (b, COL_IQ // iqw)),
            pl.BlockSpec((tq, iqw), lambda b: (b, COL_IQ // iqw + 1)),
            pl.BlockSpec((tq, 128), lambda b: (b, COL_SMALL // 128)),
            pl.BlockSpec((s, 128), lambda b: (0, COL_IK // 128)),
            pl.BlockSpec((s, kvw), lambda b: (0, COL_BK // kvw)),
            pl.BlockSpec((s, kvw), lambda b: (0, COL_BV // kvw)),
        ],
        out_specs=pl.BlockSpec((tq, MIX_B), lambda b: (b, 0)),
        out_shape=jax.ShapeDtypeStruct((s, MIX_B), BF16),
        scratch_shapes=[
            pltpu.VMEM((s // tk, tq, tk), F32),
            pltpu.VMEM((DSA_KV_HEADS, rows, HEAD_DIM), BF16),
            pltpu.VMEM((DSA_KV_HEADS, rows, 128), F32),
            pltpu.VMEM((DSA_KV_HEADS, rows, 128), F32),
            pltpu.VMEM((DSA_KV_HEADS, rows, HEAD_DIM), F32),
        ],
        compiler_params=_cparams(("arbitrary",), 56),
        name="dsa_attention",
    )(pbf, pbf, pbf, pg, pbf, pbf, pbf)


def _layer_norm(z, g, b):
    mu = jnp.mean(z, axis=-1, keepdims=True)
    zc = z - mu
    var = jnp.mean(zc * zc, axis=-1, keepdims=True)
    return zc * lax.rsqrt(var + LN_EPS) * g + b


def _merge_kernel(a_ref, b_ref, ga_ref, gb_ref, x_ref, wa_ref, wb_ref, wo_ref, g_ref, beta_ref,
                  wr_ref, br_ref, h_ref, eidx_ref, gate_ref, pos_ref, cnt_ref, carry_ref, *, alpha):
    i = pl.program_id(0)

    @pl.when(i == 0)
    def _():
        carry_ref[...] = jnp.zeros_like(carry_ref)

    ma = jnp.dot(a_ref[...], wa_ref[...], preferred_element_type=F32)
    mb = jnp.dot(b_ref[...], wb_ref[...], preferred_element_type=F32)
    merged = jax.nn.sigmoid(ga_ref[...]) * ma + jax.nn.sigmoid(gb_ref[...]) * mb
    y = jnp.dot(merged.astype(BF16), wo_ref[...], preferred_element_type=F32)
    h = _layer_norm(alpha * x_ref[...] + y, g_ref[...], beta_ref[...])
    h_ref[...] = h

    wr = wr_ref[...]
    h_hi = h.astype(BF16)
    h_lo = (h - h_hi.astype(F32)).astype(BF16)
    wr_hi = wr.astype(BF16)
    wr_lo = (wr - wr_hi.astype(F32)).astype(BF16)
    logits = (jnp.dot(h_hi, wr_hi, preferred_element_type=F32) + jnp.dot(h_lo, wr_hi, preferred_element_type=F32)
              + jnp.dot(h_hi, wr_lo, preferred_element_type=F32) + br_ref[...])
    tm, ne = logits.shape
    lane = lax.broadcasted_iota(I32, (tm, ne), 1)
    lane_k = lax.broadcasted_iota(I32, (tm, TOP_K), 1)
    work = logits
    vals, sels = [], []
    eidx = jnp.zeros((tm, TOP_K), I32)
    onehot = jnp.zeros((tm, ne), F32)
    for k in range(TOP_K):
        mv = jnp.max(work, axis=1, keepdims=True)
        idx = jnp.min(jnp.where(work == mv, lane, ne), axis=1, keepdims=True)
        sel = lane == idx
        vals.append(mv)
        sels.append(sel)
        eidx = jnp.where(lane_k == k, idx, eidx)
        onehot = onehot + jnp.where(sel, 1.0, 0.0)
        work = jnp.where(sel, -jnp.inf, work)
    exps = [jnp.exp(v - vals[0]) for v in vals]
    denom = exps[0] + exps[1] + exps[2] + exps[3]
    gates = jnp.zeros((tm, TOP_K), F32)
    for k in range(TOP_K):
        gates = jnp.where(lane_k == k, exps[k] / denom, gates)

    r_i = lax.broadcasted_iota(I32, (tm, tm), 0)
    c_i = lax.broadcasted_iota(I32, (tm, tm), 1)
    lower = jnp.where(c_i < r_i, 1.0, 0.0).astype(BF16)
    rank = carry_ref[...] + jnp.dot(lower, onehot.astype(BF16), preferred_element_type=F32)
    pos = jnp.zeros((tm, TOP_K), I32)
    for k in range(TOP_K):
        pk = jnp.sum(jnp.where(sels[k], rank, 0.0), axis=1, keepdims=True).astype(I32)
        pos = jnp.where(lane_k == k, pk, pos)
    carry_ref[...] = carry_ref[...] + jnp.sum(onehot, axis=0, keepdims=True)

    eidx_ref[...] = eidx
    gate_ref[...] = gates
    pos_ref[...] = pos
    cnt_ref[...] = carry_ref[...].astype(I32)


def _merge(a_out, b_out, pg, x2d, wa, wb, wo, ln_g, ln_b, w_router, b_router, alpha):
    s, d = x2d.shape
    tm = min(256, s)
    full = lambda shape: pl.BlockSpec(shape, lambda i: (0,) * len(shape))
    return pl.pallas_call(
        functools.partial(_merge_kernel, alpha=alpha),
        grid=(s // tm,),
        in_specs=[
            pl.BlockSpec((tm, MIX_A), lambda i: (i, 0)),
            pl.BlockSpec((tm, MIX_B), lambda i: (i, 0)),
            pl.BlockSpec((tm, d), lambda i: (i, COL_GA // d)),
            pl.BlockSpec((tm, d), lambda i: (i, COL_GB // d)),
            pl.BlockSpec((tm, d), lambda i: (i, 0)),
            full((MIX_A, d)), full((MIX_B, d)), full((d, d)),
            full((1, d)), full((1, d)), full((d, N_EXPERTS)), full((1, N_EXPERTS)),
        ],
        out_specs=[
            pl.BlockSpec((tm, d), lambda i: (i, 0)),
            pl.BlockSpec((tm, TOP_K), lambda i: (i, 0)),
            pl.BlockSpec((tm, TOP_K), lambda i: (i, 0)),
            pl.BlockSpec((tm, TOP_K), lambda i: (i, 0)),
            full((1, N_EXPERTS)),
        ],
        out_shape=[
            jax.ShapeDtypeStruct((s, d), F32),
            jax.ShapeDtypeStruct((s, TOP_K), I32),
            jax.ShapeDtypeStruct((s, TOP_K), F32),
            jax.ShapeDtypeStruct((s, TOP_K), I32),
            jax.ShapeDtypeStruct((1, N_EXPERTS), I32),
        ],
        scratch_shapes=[pltpu.VMEM((1, N_EXPERTS), F32)],
        compiler_params=_cparams(("arbitrary",), 56),
        name="merge_ln_router",
    )(a_out, b_out, pg, pg, x2d, wa, wb, wo, ln_g, ln_b, w_router, b_router)


def _dispatch_kernel(dest_ref, gap_start_ref, gap_len_ref, total_ref, h_ref, xs_ref, stage_ref, zero_ref,
                     sem, zsem, *, max_slack):
    i = pl.program_id(0)
    last = pl.num_programs(0) - 1
    slot = lax.rem(i, 2)
    n = dest_ref.shape[0]

    @pl.when(i == 0)
    def _():
        zero_ref[...] = jnp.zeros_like(zero_ref)

        def gap_copy(e, k):
            return pltpu.make_async_copy(zero_ref.at[pl.ds(0, 1)], xs_ref.at[pl.ds(gap_start_ref[e] + k, 1)], zsem)

        n_slots = xs_ref.shape[0]

        def slack_copy(k):
            row = pl.multiple_of(total_ref[0] + k * MOE_CHUNK, MOE_CHUNK)
            return pltpu.make_async_copy(zero_ref, xs_ref.at[pl.ds(row, MOE_CHUNK)], zsem)

        def slack(action):
            for k in range(max_slack):
                @pl.when(total_ref[0] + k * MOE_CHUNK < n_slots)
                def _():
                    action(slack_copy(k))

        def issue(e, _):
            lax.fori_loop(0, gap_len_ref[e], lambda k, c: (gap_copy(e, k).start(), c)[1], 0)
            return 0

        def settle(e, _):
            lax.fori_loop(0, gap_len_ref[e], lambda k, c: (gap_copy(e, k).wait(), c)[1], 0)
            return 0

        lax.fori_loop(0, N_EXPERTS, issue, 0)
        slack(lambda cp: cp.start())
        lax.fori_loop(0, N_EXPERTS, settle, 0)
        slack(lambda cp: cp.wait())

    def drain(sl):
        pltpu.make_async_copy(xs_ref.at[pl.ds(0, n)], xs_ref.at[pl.ds(0, n)], sem.at[sl]).wait()

    @pl.when(i >= 2)
    def _():
        drain(slot)

    stage_ref[slot] = h_ref[...]

    def scatter(sl):
        def start(g, _):
            base = pl.multiple_of(g * ROW_TILE, ROW_TILE)
            for u in range(ROW_TILE * TOP_K):
                r = u // TOP_K
                pltpu.make_async_copy(stage_ref.at[sl, pl.ds(base + r, 1)],
                                      xs_ref.at[pl.ds(dest_ref[g * (ROW_TILE * TOP_K) + u], 1)],
                                      sem.at[sl]).start(priority=u % 2)
            return 0
        lax.fori_loop(0, n // (ROW_TILE * TOP_K), start, 0)

    for parity in range(2):
        @pl.when(slot == parity)
        def _():
            scatter(parity)

    @pl.when(i == last)
    def _():
        drain(slot)

        @pl.when(i >= 1)
        def _():
            drain(1 - slot)


def _dispatch(h, dest_flat, gap_start, gap_len, total, n_slots):
    s, d = h.shape
    tm = min(128, s)
    smem = pl.BlockSpec(memory_space=pltpu.SMEM)
    max_slack = (n_slots - s * TOP_K + MOE_CHUNK - 1) // MOE_CHUNK
    return pl.pallas_call(
        functools.partial(_dispatch_kernel, max_slack=max_slack),
        grid=(s // tm,),
        in_specs=[
            pl.BlockSpec((tm * TOP_K,), lambda i: (i,), memory_space=pltpu.SMEM),
            smem, smem, smem,
            pl.BlockSpec((tm, d), lambda i: (i, 0)),
        ],
        out_specs=pl.BlockSpec(memory_space=pl.ANY),
        out_shape=jax.ShapeDtypeStruct((n_slots, d), h.dtype),
        scratch_shapes=[pltpu.VMEM((2, tm, d), h.dtype), pltpu.VMEM((MOE_CHUNK, d), h.dtype),
                        pltpu.SemaphoreType.DMA((2,)), pltpu.SemaphoreType.DMA(())],
        compiler_params=_cparams(("arbitrary",), 32),
        name="moe_dispatch",
    )(dest_flat, gap_start, gap_len, total, h)


ROW_DMA_PRIORITY = 1


def _expert_rows_pipeline(n_chunks, prologue, in_copy, out_copy, compute):
    @pl.when(n_chunks > 0)
    def _():
        in_copy(0, 0).start(priority=ROW_DMA_PRIORITY)
        prologue()

        def body(c, _):
            slot = lax.rem(c, 2)
            in_copy(c, slot).wait()

            @pl.when(c + 1 < n_chunks)
            def _():
                in_copy(c + 1, 1 - slot).start(priority=ROW_DMA_PRIORITY)

            @pl.when(c >= 2)
            def _():
                out_copy(c - 2, slot).wait()

            compute(slot)
            out_copy(c, slot).start(priority=ROW_DMA_PRIORITY)
            return 0

        lax.fori_loop(0, n_chunks, body, 0)

        @pl.when(n_chunks >= 2)
        def _():
            out_copy(n_chunks - 2, lax.rem(n_chunks, 2)).wait()

        out_copy(n_chunks - 1, lax.rem(n_chunks - 1, 2)).wait()


def _chunk_rows(start_ref, e, c, n_slots):
    del n_slots
    return pl.ds(pl.multiple_of(start_ref[e] + c * MOE_CHUNK, MOE_ALIGN), MOE_CHUNK)


def _gate_up_kernel(start_ref, cnt_ref, x_hbm, w_hbm, b_ref, act_hbm,
                    w_stage, w_sc, xbuf, obuf, sem_in, sem_out, sem_w):
    e = pl.program_id(0)
    n_slots = x_hbm.shape[0]
    n_chunks = (cnt_ref[e] + MOE_CHUNK - 1) // MOE_CHUNK
    d_ff = obuf.shape[2]

    def w_copy(idx):
        return pltpu.make_async_copy(w_hbm.at[idx], w_stage, sem_w)

    @pl.when(e == 0)
    def _():
        w_copy(0).start()

    def fetch_weights():
        @pl.when(e < N_EXPERTS)
        def _():
            w_copy(e).wait()
            w_sc[...] = w_stage[...].astype(BF16)

            @pl.when(e + 1 < N_EXPERTS)
            def _():
                w_copy(e + 1).start()

    def in_copy(c, slot):
        return pltpu.make_async_copy(x_hbm.at[_chunk_rows(start_ref, e, c, n_slots)], xbuf.at[slot], sem_in.at[slot])

    def out_copy(c, slot):
        return pltpu.make_async_copy(obuf.at[slot], act_hbm.at[_chunk_rows(start_ref, e, c, n_slots)],
                                     sem_out.at[slot])

    def compute(slot):
        x = xbuf[slot].astype(BF16)
        cw = 256
        for c in range(d_ff // cw):
            gs = slice(c * cw, (c + 1) * cw)
            us = slice(d_ff + c * cw, d_ff + (c + 1) * cw)
            g = jnp.dot(x, w_sc[:, gs], preferred_element_type=F32) + b_ref[:, gs]
            u = jnp.dot(x, w_sc[:, us], preferred_element_type=F32) + b_ref[:, us]
            g = jnp.minimum(g, SWIGLU_LIMIT)
            u = jnp.clip(u, -SWIGLU_LIMIT, SWIGLU_LIMIT)
            obuf[slot, :, gs] = (g * jax.nn.sigmoid(SWIGLU_ALPHA * g) * (u + 1.0)).astype(obuf.dtype)

    @pl.when(n_chunks == 0)
    def _():
        fetch_weights()

    _expert_rows_pipeline(n_chunks, fetch_weights, in_copy, out_copy, compute)


def _down_kernel(start_ref, cnt_ref, a_hbm, wd_ref, bd_ref, y_hbm, wd_sc, abuf, obuf, sem_in, sem_out):
    e = pl.program_id(0)
    n_slots = y_hbm.shape[0]
    n_chunks = (cnt_ref[e] + MOE_CHUNK - 1) // MOE_CHUNK

    def cast_weights():
        wd_sc[...] = wd_ref[...].astype(BF16)

    def in_copy(c, slot):
        return pltpu.make_async_copy(a_hbm.at[_chunk_rows(start_ref, e, c, n_slots)], abuf.at[slot],
                                     sem_in.at[slot])

    def out_copy(c, slot):
        return pltpu.make_async_copy(obuf.at[slot], y_hbm.at[_chunk_rows(start_ref, e, c, n_slots)],
                                     sem_out.at[slot])

    def compute(slot):
        cw = 512
        for c in range(obuf.shape[2] // cw):
            cs = slice(c * cw, (c + 1) * cw)
            obuf[slot, :, cs] = jnp.dot(abuf[slot], wd_sc[:, cs], preferred_element_type=F32) + bd_ref[:, cs]

    _expert_rows_pipeline(n_chunks, cast_weights, in_copy, out_copy, compute)


def _experts(xs, starts, counts, w_gate_up, b_gate_up, w_down, b_down):
    n_slots, d = xs.shape
    any_spec = pl.BlockSpec(memory_space=pl.ANY)
    dma_sems = pltpu.SemaphoreType.DMA((2,))
    bgu = b_gate_up.reshape(N_EXPERTS, 1, 2 * D_FF)
    n_groups = starts.shape[0]
    wi = lambda e: jnp.minimum(e, N_EXPERTS - 1)
    act = pl.pallas_call(
        _gate_up_kernel,
        grid_spec=pltpu.PrefetchScalarGridSpec(
            num_scalar_prefetch=2,
            grid=(n_groups,),
            in_specs=[
                any_spec,
                any_spec,
                pl.BlockSpec((None, 1, 2 * D_FF), lambda e, st, ct: (wi(e), 0, 0)),
            ],
            out_specs=any_spec,
            scratch_shapes=[pltpu.VMEM((d, 2 * D_FF), F32), pltpu.VMEM((d, 2 * D_FF), BF16),
                            pltpu.VMEM((2, MOE_CHUNK, d), xs.dtype), pltpu.VMEM((2, MOE_CHUNK, D_FF), BF16),
                            dma_sems, dma_sems, pltpu.SemaphoreType.DMA(())],
        ),
        out_shape=jax.ShapeDtypeStruct((n_slots, D_FF), BF16),
        compiler_params=_cparams(("arbitrary",), 60),
        name="moe_gate_up",
    )(starts, counts, xs, w_gate_up, bgu)
    bd = b_down.reshape(N_EXPERTS, 1, d)
    ys = pl.pallas_call(
        _down_kernel,
        grid_spec=pltpu.PrefetchScalarGridSpec(
            num_scalar_prefetch=2,
            grid=(n_groups,),
            in_specs=[
                any_spec,
                pl.BlockSpec((None, D_FF, d), lambda e, st, ct: (wi(e), 0, 0)),
                pl.BlockSpec((None, 1, d), lambda e, st, ct: (wi(e), 0, 0)),
            ],
            out_specs=any_spec,
            scratch_shapes=[pltpu.VMEM((D_FF, d), BF16),
                            pltpu.VMEM((2, MOE_CHUNK, D_FF), BF16), pltpu.VMEM((2, MOE_CHUNK, d), F32),
                            dma_sems, dma_sems],
        ),
        out_shape=jax.ShapeDtypeStruct((n_slots, d), F32),
        compiler_params=_cparams(("arbitrary",), 56),
        name="moe_down",
    )(starts, counts, act, w_down, bd)
    return ys


def _combine_kernel(dest_ref, dest_next_ref, gate_ref, h_ref, g_ref, beta_ref, ys_ref, o_ref, buf_ref, sem,
                    *, alpha):
    i = pl.program_id(0)
    last = pl.num_programs(0) - 1
    slot = lax.rem(i, 2)
    n = dest_ref.shape[0]

    def gather(dref, sl):
        def start(g, _):
            base = pl.multiple_of(g * ROW_TILE, ROW_TILE)
            for u in range(ROW_TILE * TOP_K):
                r, k = divmod(u, TOP_K)
                pltpu.make_async_copy(ys_ref.at[pl.ds(dref[g * (ROW_TILE * TOP_K) + u], 1)],
                                      buf_ref.at[sl, k, pl.ds(base + r, 1)],
                                      sem.at[sl]).start(priority=u % 2)
            return 0
        lax.fori_loop(0, n // (ROW_TILE * TOP_K), start, 0)

    for parity in range(2):
        @pl.when((i == 0) & (slot == parity))
        def _():
            gather(dest_ref, parity)

        @pl.when((i < last) & (slot == parity))
        def _():
            gather(dest_next_ref, 1 - parity)

    pltpu.make_async_copy(buf_ref.at[slot], buf_ref.at[slot], sem.at[slot]).wait()
    gates = gate_ref[...]
    y = gates[:, 0:1] * buf_ref[slot, 0]
    for k in range(1, TOP_K):
        y = y + gates[:, k:k + 1] * buf_ref[slot, k]
    o_ref[...] = _layer_norm(alpha * h_ref[...] + y, g_ref[...], beta_ref[...])


def _combine(ys, dest_flat, gates, h, ln_g, ln_b, alpha):
    s, d = h.shape
    tm = min(128, s)
    n_steps = s // tm
    return pl.pallas_call(
        functools.partial(_combine_kernel, alpha=alpha),
        grid=(n_steps,),
        in_specs=[
            pl.BlockSpec((tm * TOP_K,), lambda i: (i,), memory_space=pltpu.SMEM),
            pl.BlockSpec((tm * TOP_K,), lambda i: (jnp.minimum(i + 1, n_steps - 1),), memory_space=pltpu.SMEM),
            pl.BlockSpec((tm, TOP_K), lambda i: (i, 0)),
            pl.BlockSpec((tm, d), lambda i: (i, 0)),
            pl.BlockSpec((1, d), lambda i: (0, 0)),
            pl.BlockSpec((1, d), lambda i: (0, 0)),
            pl.BlockSpec(memory_space=pl.ANY),
        ],
        out_specs=pl.BlockSpec((tm, d), lambda i: (i, 0)),
        out_shape=jax.ShapeDtypeStruct((s, d), F32),
        scratch_shapes=[pltpu.VMEM((2, TOP_K, tm, d), F32), pltpu.SemaphoreType.DMA((2,))],
        compiler_params=_cparams(("arbitrary",), 32),
        name="moe_combine",
    )(dest_flat, dest_flat, gates, h, ln_g, ln_b, ys)


def _layer(x2d, w_in, b_forget, w_branch_a, w_branch_b, w_out, ln1_g, ln1_b, w_router, b_router,
           w_gate_up, b_gate_up, w_down, b_down, ln2_g, ln2_b, tables):
    s, d = x2d.shape
    alpha = (2.0 * DEPTH) ** 0.25
    scale = HEAD_DIM ** -0.5 * LOG2E

    w_bf, w_g = _prepare_w_in(w_in.T, scale)
    pbf, pg = _project(x2d, w_bf, tables, w_g)

    fox_tq = min(512, s)
    fox_tk = min(512, s)
    af_t = pg[:, COL_SMALL:COL_SMALL + FOX_HEADS].T
    c_t = _forget_cumsum(af_t, b_forget)
    c_tiles = c_t.reshape(FOX_HEADS, s // fox_tk, 1, fox_tk)
    a_out = _fox_attention(pbf, c_tiles, s, fox_tq, fox_tk, 4)

    b_out = _dsa_attention(pbf, pg, s, min(256, s), min(512, s))

    h, eidx, gates, pos, counts = _merge(
        a_out, b_out, pg, x2d, w_branch_a.astype(BF16), w_branch_b.astype(BF16), w_out.astype(BF16),
        ln1_g.reshape(1, d), ln1_b.reshape(1, d), w_router, b_router.reshape(1, N_EXPERTS), alpha)

    counts = counts.reshape(N_EXPERTS).astype(I32)
    aligned = (counts + MOE_ALIGN - 1) // MOE_ALIGN * MOE_ALIGN
    ends = jnp.cumsum(aligned).astype(I32)
    starts = ends - aligned
    bound = s * TOP_K + N_EXPERTS * MOE_ALIGN + 2 * MOE_CHUNK
    n_slots = (bound + MOE_CHUNK - 1) // MOE_CHUNK * MOE_CHUNK
    experts = jnp.arange(N_EXPERTS, dtype=I32)
    start_of = jnp.sum(jnp.where(eidx[..., None] == experts, starts, 0), axis=-1)
    dest = (start_of + pos).reshape(s * TOP_K).astype(I32)
    total = (ends[-1:] + MOE_CHUNK - 1) // MOE_CHUNK * MOE_CHUNK
    gap_len = aligned - counts + jnp.where(experts == N_EXPERTS - 1, total[0] - ends[-1], 0)
    group_starts = jnp.concatenate([starts, total])
    group_counts = jnp.concatenate([counts + jnp.where(experts == N_EXPERTS - 1, gap_len, 0), n_slots - total])

    xs = _dispatch(h, dest, starts + counts, gap_len, total, n_slots)
    ys = _experts(xs, group_starts, group_counts, w_gate_up, b_gate_up, w_down, b_down)
    return _combine(ys, dest, gates, h, ln2_g.reshape(1, d), ln2_b.reshape(1, d), alpha)


def kernel(x, w_in, b_forget, w_branch_a, w_branch_b, w_out, ln1_g, ln1_b, w_router, b_router,
           w_gate_up, b_gate_up, w_down, b_down, ln2_g, ln2_b):
    bsz, s, d = x.shape
    tables = _rope_tables(s)
    outs = []
    for bi in range(bsz):
        xb = x[bi]
        for l in range(DEPTH):
            xb = _layer(xb, w_in[l], b_forget[l], w_branch_a[l], w_branch_b[l], w_out[l], ln1_g[l], ln1_b[l],
                        w_router[l], b_router[l], w_gate_up[l], b_gate_up[l], w_down[l], b_down[l],
                        ln2_g[l], ln2_b[l], tables)
        outs.append(xb)
    return outs[0][None] if bsz == 1 else jnp.stack(outs)
```

```python
import functools

import numpy as np
import jax
import jax.numpy as jnp
from jax import lax
from jax.experimental import pallas as pl
from jax.experimental.pallas import tpu as pltpu

F32 = jnp.float32
BF16 = jnp.bfloat16
I32 = jnp.int32

D_MODEL = 2048
DEPTH = 1
CHUNK = 64
HEAD_DIM = 128
FOX_HEADS = 8
DSA_HEADS = 8
DSA_KV_HEADS = 2
DSA_GROUP = DSA_HEADS // DSA_KV_HEADS
IDX_HEADS = 16
IDX_DIM = 64
TOPK_MAX = 256
ROPE_THETA = 500000.0
ROT_FRACTION_DEN = 4
MIX_A = FOX_HEADS * HEAD_DIM
MIX_B = DSA_HEADS * HEAD_DIM
N_EXPERTS = 32
TOP_K = 4
D_FF = D_MODEL
SWIGLU_ALPHA = 1.702
SWIGLU_LIMIT = 7.0
LN_EPS = 1e-5

MIB = 1024 * 1024
NEG_BIG = -1e30
LOG2E = 1.4426950408889634
NT_DIMS = (((1,), (1,)), ((), ()))
INT_MIN = -(2 ** 31)

COL_AQ = 0
COL_AK = 1024
COL_AV = 2048
COL_BQ = 3072
COL_BK = 4096
COL_BV = 4352
COL_IQ = 4608
COL_IK = 5632
N_PBF = 5888
PROJ_TN = 256
ROPE_HEAD_TILES = (12, 13, 14, 15, 16)
ROPE_IDX_TILES = (18, 19, 20, 21, 22)
COL_GA = 0
COL_GB = 2048
COL_SMALL = 4096
N_PG = 4224
PG_TN = 384

MOE_CHUNK = 256
MOE_ALIGN = 16
ROW_TILE = 8


def _cparams(dims, vmem_mib):
    return pltpu.CompilerParams(dimension_semantics=dims, vmem_limit_bytes=vmem_mib * MIB)


IN_SIZES = (MIX_A, MIX_A, MIX_A, FOX_HEADS, MIX_B, DSA_KV_HEADS * HEAD_DIM, DSA_KV_HEADS * HEAD_DIM,
            IDX_HEADS * IDX_DIM, IDX_DIM, IDX_HEADS, D_MODEL, D_MODEL)
IN_OFF = dict(zip(("aq", "ak", "av", "af", "bq", "bk", "bv", "iq", "ik", "iw", "ga", "gb", "end"),
                  np.concatenate([[0], np.cumsum(IN_SIZES)]).tolist()))


def _prep_kernel(w_ref, wbf_ref, wg_ref, *, scale):
    o = IN_OFF
    width = w_ref.shape[1]

    def rows(a, b, mult=None):
        v = w_ref[a:b, :]
        return v if mult is None else v * mult

    wbf_ref[COL_AQ:COL_AK, :] = rows(o["aq"], o["ak"], scale).astype(BF16)
    wbf_ref[COL_AK:COL_BQ, :] = rows(o["ak"], o["af"]).astype(BF16)
    wbf_ref[COL_BQ:COL_BK, :] = rows(o["bq"], o["bk"], scale).astype(BF16)
    wbf_ref[COL_BK:COL_IK, :] = rows(o["bk"], o["ik"]).astype(BF16)
    wbf_ref[COL_IK:N_PBF, :] = jnp.concatenate(
        [rows(o["ik"], o["iw"]), jnp.zeros((N_PBF - COL_IK - IDX_DIM, width), F32)], axis=0).astype(BF16)
    wg_ref[COL_GA:COL_SMALL, :] = rows(o["ga"], o["end"]).astype(BF16)
    wg_ref[COL_SMALL:N_PG, :] = jnp.concatenate(
        [rows(o["af"], o["bq"]), rows(o["iw"], o["ga"]),
         jnp.zeros((N_PG - COL_SMALL - FOX_HEADS - IDX_HEADS, width), F32)], axis=0).astype(BF16)


def _prepare_w_in(w_in_t, scale):
    n_in, d = w_in_t.shape
    tc = 256
    return pl.pallas_call(
        functools.partial(_prep_kernel, scale=scale),
        grid=(d // tc,),
        in_specs=[pl.BlockSpec((n_in, tc), lambda i: (0, i))],
        out_specs=[pl.BlockSpec((N_PBF, tc), lambda i: (0, i)), pl.BlockSpec((N_PG, tc), lambda i: (0, i))],
        out_shape=[jax.ShapeDtypeStruct((N_PBF, d), BF16), jax.ShapeDtypeStruct((N_PG, d), BF16)],
        compiler_params=_cparams(("arbitrary",), 48),
        name="w_in_prep",
    )(w_in_t)


def _tile_in(j, tiles):
    cond = j == tiles[0]
    for t in tiles[1:]:
        cond = cond | (j == t)
    return cond


def _proj_rope_kernel(x_ref, w_ref, tab_ref, o_ref, xb_ref):
    j = pl.program_id(1)

    @pl.when(j == 0)
    def _():
        xb_ref[...] = x_ref[...].astype(BF16)

    acc = lax.dot_general(xb_ref[...], w_ref[...], NT_DIMS, preferred_element_type=F32)
    tn = acc.shape[1]
    is_head = _tile_in(j, ROPE_HEAD_TILES)
    is_idx = _tile_in(j, ROPE_IDX_TILES)

    def rope(shift):
        c = tab_ref[0, 0]
        s_prev = tab_ref[0, 1]
        s_next = tab_ref[0, 2]
        out = acc * c + pltpu.roll(acc, shift, 1) * s_prev + pltpu.roll(acc, tn - shift, 1) * s_next
        o_ref[...] = out.astype(o_ref.dtype)

    @pl.when(is_head)
    def _():
        rope(HEAD_DIM // ROT_FRACTION_DEN // 2)

    @pl.when(is_idx)
    def _():
        rope(IDX_DIM // ROT_FRACTION_DEN // 2)

    @pl.when(jnp.logical_not(is_head | is_idx))
    def _():
        o_ref[...] = acc.astype(o_ref.dtype)


def _proj_plain_kernel(x_ref, w_ref, o_ref, xb_ref):
    j = pl.program_id(1)

    @pl.when(j == 0)
    def _():
        xb_ref[...] = x_ref[...].astype(BF16)

    o_ref[...] = lax.dot_general(xb_ref[...], w_ref[...], NT_DIMS, preferred_element_type=F32).astype(o_ref.dtype)


def _rope_group(j):
    return jnp.where(_tile_in(j, ROPE_IDX_TILES), 1, 0)


def _project(x2d, w_bf, tables, w_g):
    s, d = x2d.shape
    tm = min(1024, s)
    pbf = pl.pallas_call(
        _proj_rope_kernel,
        grid=(s // tm, N_PBF // PROJ_TN),
        in_specs=[
            pl.BlockSpec((tm, d), lambda i, j: (i, 0)),
            pl.BlockSpec((PROJ_TN, d), lambda i, j: (j, 0)),
            pl.BlockSpec((1, 3, tm, PROJ_TN), lambda i, j: (_rope_group(j), 0, i, 0)),
        ],
        out_specs=pl.BlockSpec((tm, PROJ_TN), lambda i, j: (i, j)),
        out_shape=jax.ShapeDtypeStruct((s, N_PBF), BF16),
        scratch_shapes=[pltpu.VMEM((tm, d), BF16)],
        compiler_params=_cparams(("arbitrary", "arbitrary"), 48),
        name="proj_bf16",
    )(x2d, w_bf, tables)
    pg = pl.pallas_call(
        _proj_plain_kernel,
        grid=(s // tm, N_PG // PG_TN),
        in_specs=[
            pl.BlockSpec((tm, d), lambda i, j: (i, 0)),
            pl.BlockSpec((PG_TN, d), lambda i, j: (j, 0)),
        ],
        out_specs=pl.BlockSpec((tm, PG_TN), lambda i, j: (i, j)),
        out_shape=jax.ShapeDtypeStruct((s, N_PG), F32),
        scratch_shapes=[pltpu.VMEM((tm, d), BF16)],
        compiler_params=_cparams(("arbitrary", "arbitrary"), 48),
        name="proj_f32",
    )(x2d, w_g)
    return pbf, pg


def _rope_tables(s):
    pos = jnp.arange(s, dtype=F32)

    def one(period):
        rot = period // ROT_FRACTION_DEN
        half = rot // 2
        inv = jnp.power(ROPE_THETA, -jnp.arange(0, rot, 2, dtype=F32) / rot)
        ang = pos[:, None] * inv[None, :]
        cos, sin = jnp.cos(ang), jnp.sin(ang)
        zero = jnp.zeros((s, period - rot), F32)
        c = jnp.concatenate([cos, cos, jnp.ones((s, period - rot), F32)], axis=1)
        s_prev = jnp.concatenate([jnp.zeros((s, half), F32), sin, zero], axis=1)
        s_next = jnp.concatenate([-sin, jnp.zeros((s, half), F32), zero], axis=1)
        reps = PROJ_TN // period
        return jnp.stack([jnp.tile(c, (1, reps)), jnp.tile(s_prev, (1, reps)), jnp.tile(s_next, (1, reps))])

    return jnp.stack([one(HEAD_DIM), one(IDX_DIM)])


def _cumsum_kernel(af_ref, bf_ref, c_ref, carry_ref):
    i = pl.program_id(0)

    @pl.when(i == 0)
    def _():
        carry_ref[...] = jnp.zeros_like(carry_ref)

    z = af_ref[...] + bf_ref[...]
    logf = jnp.minimum(z, 0.0) - jnp.log1p(jnp.exp(-jnp.abs(z)))
    t = z.shape[1]
    row = lax.broadcasted_iota(I32, (t, t), 0)
    col = lax.broadcasted_iota(I32, (t, t), 1)
    upper = (row <= col).astype(F32)
    c = jnp.dot(logf, upper, preferred_element_type=F32, precision=lax.Precision.HIGHEST) + carry_ref[...]
    c_ref[...] = c * LOG2E
    carry_ref[...] = c[:, t - 1:t]


def _forget_cumsum(af_t, b_forget):
    h, s = af_t.shape
    t = min(512, s)
    return pl.pallas_call(
        _cumsum_kernel,
        grid=(s // t,),
        in_specs=[pl.BlockSpec((h, t), lambda i: (0, i)), pl.BlockSpec((h, 1), lambda i: (0, 0))],
        out_specs=pl.BlockSpec((h, t), lambda i: (0, i)),
        out_shape=jax.ShapeDtypeStruct((h, s), F32),
        scratch_shapes=[pltpu.VMEM((h, 1), F32)],
        compiler_params=_cparams(("arbitrary",), 32),
        name="forget_cumsum",
    )(af_t, b_forget.reshape(h, 1).astype(F32))


def _softmax_init(m_ref, l_ref, acc_ref):
    m_ref[...] = jnp.full(m_ref.shape, NEG_BIG, F32)
    l_ref[...] = jnp.zeros(l_ref.shape, F32)
    acc_ref[...] = jnp.zeros(acc_ref.shape, F32)


def _softmax_tile(s, v, m_ref, l_ref, acc_ref, i):
    reps = s.shape[1] // 128
    m_old = m_ref[i]
    m_new = jnp.maximum(m_old, jnp.max(s, axis=1, keepdims=True))
    p = jnp.exp2(s - jnp.tile(m_new, (1, reps)))
    alpha = jnp.exp2(m_old - m_new)
    psum = p[:, :128]
    for c in range(1, reps):
        psum = psum + p[:, c * 128:(c + 1) * 128]
    l_ref[i] = alpha * l_ref[i] + psum
    acc_ref[i] = alpha * acc_ref[i] + jnp.dot(p.astype(BF16), v, preferred_element_type=F32)
    m_ref[i] = m_new


def _softmax_result(l_ref, acc_ref, i):
    return acc_ref[i] / jnp.sum(l_ref[i], axis=1, keepdims=True)


def _fox_kernel(q_ref, k_ref, v_ref, c_ref, o_ref, m_ref, l_ref, acc_ref, *, tq, tk, nh):
    qi = pl.program_id(1)
    _softmax_init(m_ref, l_ref, acc_ref)

    def tile(kt, masked):
        start = pl.multiple_of(kt * tk, tk)
        if masked:
            row = qi * tq + lax.broadcasted_iota(I32, (tq, tk), 0)
            col = kt * tk + lax.broadcasted_iota(I32, (tq, tk), 1)
            causal = col <= row
        for h in range(nh):
            hs = slice(h * HEAD_DIM, (h + 1) * HEAD_DIM)
            k = k_ref[pl.ds(start, tk), hs]
            v = v_ref[pl.ds(start, tk), hs]
            s = lax.dot_general(q_ref[:, hs], k, (((1,), (1,)), ((), ())), preferred_element_type=F32)
            s = s - c_ref[h, kt]
            if masked:
                s = jnp.where(causal, s, NEG_BIG)
            _softmax_tile(s, v, m_ref, l_ref, acc_ref, h)

    n_full = (qi * tq) // tk

    def body(kt, _):
        tile(kt, False)
        return 0

    lax.fori_loop(0, n_full, body, 0)
    for t in range((tq + tk - 1) // tk):
        tile(n_full + t, True)
    for h in range(nh):
        o_ref[:, h * HEAD_DIM:(h + 1) * HEAD_DIM] = _softmax_result(l_ref, acc_ref, h).astype(o_ref.dtype)


def _fox_attention(pbf, c_tiles, s, tq, tk, nh):
    cb = nh * HEAD_DIM
    return pl.pallas_call(
        functools.partial(_fox_kernel, tq=tq, tk=tk, nh=nh),
        grid=(FOX_HEADS // nh, s // tq),
        in_specs=[
            pl.BlockSpec((tq, cb), lambda g, i: (i, COL_AQ // cb + g)),
            pl.BlockSpec((s, cb), lambda g, i: (0, COL_AK // cb + g)),
            pl.BlockSpec((s, cb), lambda g, i: (0, COL_AV // cb + g)),
            pl.BlockSpec((nh, s // tk, 1, tk), lambda g, i: (g, 0, 0, 0)),
        ],
        out_specs=pl.BlockSpec((tq, cb), lambda g, i: (i, g)),
        out_shape=jax.ShapeDtypeStruct((s, MIX_A), BF16),
        scratch_shapes=[
            pltpu.VMEM((nh, tq, 128), F32),
            pltpu.VMEM((nh, tq, 128), F32),
            pltpu.VMEM((nh, tq, HEAD_DIM), F32),
        ],
        compiler_params=_cparams(("arbitrary", "arbitrary"), 56),
        name="fox_attention",
    )(pbf, pbf, pbf, c_tiles)


def _dsa_kernel(bq_ref, iq_lo_ref, iq_hi_ref, iw_ref, ik_ref, bk_ref, bv_ref, o_ref,
                key_ref, qg_ref, m_ref, l_ref, acc_ref, *, tq, tk, n_sel):
    b = pl.program_id(0)
    n_tiles = (b * tq) // tk + 1
    row_g = b * tq + lax.broadcasted_iota(I32, (tq, tk), 0)
    adm_end = (row_g // CHUNK + 1) * CHUNK
    col_l = lax.broadcasted_iota(I32, (tq, tk), 1)

    idx_scale = (IDX_HEADS ** -0.5) * (IDX_DIM ** -0.5)
    iw = iw_ref[...][:, 8:8 + IDX_HEADS] * idx_scale
    iq = jnp.concatenate([iq_lo_ref[...], iq_hi_ref[...]], axis=1)

    def score_tile(kt, _):
        start = pl.multiple_of(kt * tk, tk)
        ik = ik_ref[pl.ds(start, tk), :][:, :IDX_DIM]
        acc = jnp.zeros((tq, tk), F32)
        for h in range(IDX_HEADS):
            a = iq[:, h * IDX_DIM:(h + 1) * IDX_DIM]
            rel = lax.dot_general(a, ik, (((1,), (1,)), ((), ())), preferred_element_type=F32)
            acc = acc + jnp.maximum(rel, 0.0) * iw[:, h:h + 1]
        key_ref[kt] = jnp.where(kt * tk + col_l < adm_end, acc, -jnp.inf)
        return 0

    lax.fori_loop(0, n_tiles, score_tile, 0)

    def float_of(code_u):
        code = code_u ^ INT_MIN
        return pltpu.bitcast(jnp.where(code >= 0, code, code ^ 0x7FFFFFFF), F32)

    def count_ge(cand):
        def body(kt, part):
            ge = jnp.where(key_ref[kt] >= cand, 1, 0)
            for c in range(tk // 128):
                part = part + ge[:, c * 128:(c + 1) * 128]
            return part
        part = lax.fori_loop(0, n_tiles, body, jnp.zeros((tq, 128), I32))
        return jnp.sum(part, axis=1, keepdims=True)

    def search_cond(carry):
        i, _, _, pending = carry
        return (i < 32) & (pending > 0)

    def refine(i, t_u, hit):
        cand_u = t_u | lax.shift_left(jnp.int32(1), 31 - i)
        cnt = count_ge(float_of(cand_u))
        return jnp.where(cnt >= n_sel, cand_u, t_u), jnp.where(cnt == n_sel, 1, hit)

    def search_step(carry):
        i, t_u, hit, _ = carry
        t_u, hit = refine(i, t_u, hit)
        return i + 1, t_u, hit, jnp.sum(1 - hit)

    zeros = jnp.zeros((tq, 1), I32)
    lead = 12
    t_u, hit = lax.fori_loop(0, lead, lambda i, c: refine(i, *c), (zeros, zeros))
    _, t_u, _, _ = lax.while_loop(search_cond, search_step, (jnp.int32(lead), t_u, hit, jnp.sum(1 - hit)))
    thr = jnp.where(t_u == 0, jnp.finfo(F32).min, float_of(t_u))

    for j in range(DSA_KV_HEADS):
        for g in range(DSA_GROUP):
            hd = j * DSA_GROUP + g
            qg_ref[j, g * tq:(g + 1) * tq, :] = bq_ref[:, hd * HEAD_DIM:(hd + 1) * HEAD_DIM]
    _softmax_init(m_ref, l_ref, acc_ref)

    def attn_tile(kt, _):
        start = pl.multiple_of(kt * tk, tk)
        sel = key_ref[kt] >= thr
        for j in range(DSA_KV_HEADS):
            hs = slice(j * HEAD_DIM, (j + 1) * HEAD_DIM)
            k = bk_ref[pl.ds(start, tk), hs]
            v = bv_ref[pl.ds(start, tk), hs]
            s = lax.dot_general(qg_ref[j], k, (((1,), (1,)), ((), ())), preferred_element_type=F32)
            s = jnp.where(sel[None], s.reshape(DSA_GROUP, tq, tk), NEG_BIG).reshape(DSA_GROUP * tq, tk)
            _softmax_tile(s, v, m_ref, l_ref, acc_ref, j)
        return 0

    lax.fori_loop(0, n_tiles, attn_tile, 0)

    for j in range(DSA_KV_HEADS):
        o = _softmax_result(l_ref, acc_ref, j)
        for g in range(DSA_GROUP):
            hd = j * DSA_GROUP + g
            o_ref[:, hd * HEAD_DIM:(hd + 1) * HEAD_DIM] = o[g * tq:(g + 1) * tq].astype(o_ref.dtype)


def _dsa_attention(pbf, pg, s, tq, tk):
    n_sel = min(TOPK_MAX, s // 4)
    kvw = DSA_KV_HEADS * HEAD_DIM
    rows = DSA_GROUP * tq
    iqw = IDX_HEADS * IDX_DIM // 2
    return pl.pallas_call(
        functools.partial(_dsa_kernel, tq=tq, tk=tk, n_sel=n_sel),
        grid=(s // tq,),
        in_specs=[
            pl.BlockSpec((tq, MIX_B), lambda b: (b, COL_BQ // MIX_B)),
            pl.BlockSpec((tq, iqw), lambda b: (b, COL_IQ // iqw)),
            pl.BlockSpec((tq, iqw), lambda b: (b, COL_IQ // iqw + 1)),
            pl.BlockSpec((tq, 128), lambda b: (b, COL_SMALL // 128)),
            pl.BlockSpec((s, 128), lambda b: (0, COL_IK // 128)),
            pl.BlockSpec((s, kvw), lambda b: (0, COL_BK // kvw)),
            pl.BlockSpec((s, kvw), lambda b: (0, COL_BV // kvw)),
        ],
        out_specs=pl.BlockSpec((tq, MIX_B), lambda b: (b, 0)),
        out_shape=jax.ShapeDtypeStruct((s, MIX_B), BF16),
        scratch_shapes=[
            pltpu.VMEM((s // tk, tq, tk), F32),
            pltpu.VMEM((DSA_KV_HEADS, rows, HEAD_DIM), BF16),
            pltpu.VMEM((DSA_KV_HEADS, rows, 128), F32),
            pltpu.VMEM((DSA_KV_HEADS, rows, 128), F32),
            pltpu.VMEM((DSA_KV_HEADS, rows, HEAD_DIM), F32),
        ],
        compiler_params=_cparams(("arbitrary",), 56),
        name="dsa_attention",
    )(pbf, pbf, pbf, pg, pbf, pbf, pbf)


def _layer_norm(z, g, b):
    mu = jnp.mean(z, axis=-1, keepdims=True)
    zc = z - mu
    var = jnp.mean(zc * zc, axis=-1, keepdims=True)
    return zc * lax.rsqrt(var + LN_EPS) * g + b


def _merge_kernel(a_ref, b_ref, ga_ref, gb_ref, x_ref, wa_ref, wb_ref, wo_ref, g_ref, beta_ref,
                  wr_ref, br_ref, h_ref, eidx_ref, gate_ref, pos_ref, cnt_ref, carry_ref, *, alpha):
    i = pl.program_id(0)

    @pl.when(i == 0)
    def _():
        carry_ref[...] = jnp.zeros_like(carry_ref)

    ma = jnp.dot(a_ref[...], wa_ref[...], preferred_element_type=F32)
    mb = jnp.dot(b_ref[...], wb_ref[...], preferred_element_type=F32)
    merged = jax.nn.sigmoid(ga_ref[...]) * ma + jax.nn.sigmoid(gb_ref[...]) * mb
    y = jnp.dot(merged.astype(BF16), wo_ref[...], preferred_element_type=F32)
    h = _layer_norm(alpha * x_ref[...] + y, g_ref[...], beta_ref[...])
    h_ref[...] = h

    wr = wr_ref[...]
    h_hi = h.astype(BF16)
    h_lo = (h - h_hi.astype(F32)).astype(BF16)
    wr_hi = wr.astype(BF16)
    wr_lo = (wr - wr_hi.astype(F32)).astype(BF16)
    logits = (jnp.dot(h_hi, wr_hi, preferred_element_type=F32) + jnp.dot(h_lo, wr_hi, preferred_element_type=F32)
              + jnp.dot(h_hi, wr_lo, preferred_element_type=F32) + br_ref[...])
    tm, ne = logits.shape
    lane = lax.broadcasted_iota(I32, (tm, ne), 1)
    lane_k = lax.broadcasted_iota(I32, (tm, TOP_K), 1)
    work = logits
    vals, sels = [], []
    eidx = jnp.zeros((tm, TOP_K), I32)
    onehot = jnp.zeros((tm, ne), F32)
    for k in range(TOP_K):
        mv = jnp.max(work, axis=1, keepdims=True)
        idx = jnp.min(jnp.where(work == mv, lane, ne), axis=1, keepdims=True)
        sel = lane == idx
        vals.append(mv)
        sels.append(sel)
        eidx = jnp.where(lane_k == k, idx, eidx)
        onehot = onehot + jnp.where(sel, 1.0, 0.0)
        work = jnp.where(sel, -jnp.inf, work)
    exps = [jnp.exp(v - vals[0]) for v in vals]
    denom = exps[0] + exps[1] + exps[2] + exps[3]
    gates = jnp.zeros((tm, TOP_K), F32)
    for k in range(TOP_K):
        gates = jnp.where(lane_k == k, exps[k] / denom, gates)

    r_i = lax.broadcasted_iota(I32, (tm, tm), 0)
    c_i = lax.broadcasted_iota(I32, (tm, tm), 1)
    lower = jnp.where(c_i < r_i, 1.0, 0.0).astype(BF16)
    rank = carry_ref[...] + jnp.dot(lower, onehot.astype(BF16), preferred_element_type=F32)
    pos = jnp.zeros((tm, TOP_K), I32)
    for k in range(TOP_K):
        pk = jnp.sum(jnp.where(sels[k], rank, 0.0), axis=1, keepdims=True).astype(I32)
        pos = jnp.where(lane_k == k, pk, pos)
    carry_ref[...] = carry_ref[...] + jnp.sum(onehot, axis=0, keepdims=True)

    eidx_ref[...] = eidx
    gate_ref[...] = gates
    pos_ref[...] = pos
    cnt_ref[...] = carry_ref[...].astype(I32)


def _merge(a_out, b_out, pg, x2d, wa, wb, wo, ln_g, ln_b, w_router, b_router, alpha):
    s, d = x2d.shape
    tm = min(256, s)
    full = lambda shape: pl.BlockSpec(shape, lambda i: (0,) * len(shape))
    return pl.pallas_call(
        functools.partial(_merge_kernel, alpha=alpha),
        grid=(s // tm,),
        in_specs=[
            pl.BlockSpec((tm, MIX_A), lambda i: (i, 0)),
            pl.BlockSpec((tm, MIX_B), lambda i: (i, 0)),
            pl.BlockSpec((tm, d), lambda i: (i, COL_GA // d)),
            pl.BlockSpec((tm, d), lambda i: (i, COL_GB // d)),
            pl.BlockSpec((tm, d), lambda i: (i, 0)),
            full((MIX_A, d)), full((MIX_B, d)), full((d, d)),
            full((1, d)), full((1, d)), full((d, N_EXPERTS)), full((1, N_EXPERTS)),
        ],
        out_specs=[
            pl.BlockSpec((tm, d), lambda i: (i, 0)),
            pl.BlockSpec((tm, TOP_K), lambda i: (i, 0)),
            pl.BlockSpec((tm, TOP_K), lambda i: (i, 0)),
            pl.BlockSpec((tm, TOP_K), lambda i: (i, 0)),
            full((1, N_EXPERTS)),
        ],
        out_shape=[
            jax.ShapeDtypeStruct((s, d), F32),
            jax.ShapeDtypeStruct((s, TOP_K), I32),
            jax.ShapeDtypeStruct((s, TOP_K), F32),
            jax.ShapeDtypeStruct((s, TOP_K), I32),
            jax.ShapeDtypeStruct((1, N_EXPERTS), I32),
        ],
        scratch_shapes=[pltpu.VMEM((1, N_EXPERTS), F32)],
        compiler_params=_cparams(("arbitrary",), 56),
        name="merge_ln_router",
    )(a_out, b_out, pg, pg, x2d, wa, wb, wo, ln_g, ln_b, w_router, b_router)


def _dispatch_kernel(dest_ref, gap_start_ref, gap_len_ref, total_ref, h_ref, xs_ref, stage_ref, zero_ref,
                     sem, zsem, *, max_slack):
    i = pl.program_id(0)
    last = pl.num_programs(0) - 1
    slot = lax.rem(i, 2)
    n = dest_ref.shape[0]

    @pl.when(i == 0)
    def _():
        zero_ref[...] = jnp.zeros_like(zero_ref)

        def gap_copy(e, k):
            return pltpu.make_async_copy(zero_ref.at[pl.ds(0, 1)], xs_ref.at[pl.ds(gap_start_ref[e] + k, 1)], zsem)

        n_slots = xs_ref.shape[0]

        def slack_copy(k):
            row = pl.multiple_of(total_ref[0] + k * MOE_CHUNK, MOE_CHUNK)
            return pltpu.make_async_copy(zero_ref, xs_ref.at[pl.ds(row, MOE_CHUNK)], zsem)

        def slack(action):
            for k in range(max_slack):
                @pl.when(total_ref[0] + k * MOE_CHUNK < n_slots)
                def _():
                    action(slack_copy(k))

        def issue(e, _):
            lax.fori_loop(0, gap_len_ref[e], lambda k, c: (gap_copy(e, k).start(), c)[1], 0)
            return 0

        def settle(e, _):
            lax.fori_loop(0, gap_len_ref[e], lambda k, c: (gap_copy(e, k).wait(), c)[1], 0)
            return 0

        lax.fori_loop(0, N_EXPERTS, issue, 0)
        slack(lambda cp: cp.start())
        lax.fori_loop(0, N_EXPERTS, settle, 0)
        slack(lambda cp: cp.wait())

    def drain(sl):
        pltpu.make_async_copy(xs_ref.at[pl.ds(0, n)], xs_ref.at[pl.ds(0, n)], sem.at[sl]).wait()

    @pl.when(i >= 2)
    def _():
        drain(slot)

    stage_ref[slot] = h_ref[...]

    def scatter(sl):
        def start(g, _):
            base = pl.multiple_of(g * ROW_TILE, ROW_TILE)
            for u in range(ROW_TILE * TOP_K):
                r = u // TOP_K
                pltpu.make_async_copy(stage_ref.at[sl, pl.ds(base + r, 1)],
                                      xs_ref.at[pl.ds(dest_ref[g * (ROW_TILE * TOP_K) + u], 1)],
                                      sem.at[sl]).start(priority=u % 2)
            return 0
        lax.fori_loop(0, n // (ROW_TILE * TOP_K), start, 0)

    for parity in range(2):
        @pl.when(slot == parity)
        def _():
            scatter(parity)

    @pl.when(i == last)
    def _():
        drain(slot)

        @pl.when(i >= 1)
        def _():
            drain(1 - slot)


def _dispatch(h, dest_flat, gap_start, gap_len, total, n_slots):
    s, d = h.shape
    tm = min(128, s)
    smem = pl.BlockSpec(memory_space=pltpu.SMEM)
    max_slack = (n_slots - s * TOP_K + MOE_CHUNK - 1) // MOE_CHUNK
    return pl.pallas_call(
        functools.partial(_dispatch_kernel, max_slack=max_slack),
        grid=(s // tm,),
        in_specs=[
            pl.BlockSpec((tm * TOP_K,), lambda i: (i,), memory_space=pltpu.SMEM),
            smem, smem, smem,
            pl.BlockSpec((tm, d), lambda i: (i, 0)),
        ],
        out_specs=pl.BlockSpec(memory_space=pl.ANY),
        out_shape=jax.ShapeDtypeStruct((n_slots, d), h.dtype),
        scratch_shapes=[pltpu.VMEM((2, tm, d), h.dtype), pltpu.VMEM((MOE_CHUNK, d), h.dtype),
                        pltpu.SemaphoreType.DMA((2,)), pltpu.SemaphoreType.DMA(())],
        compiler_params=_cparams(("arbitrary",), 32),
        name="moe_dispatch",
    )(dest_flat, gap_start, gap_len, total, h)


ROW_DMA_PRIORITY = 1


def _expert_rows_pipeline(n_chunks, prologue, in_copy, out_copy, compute):
    @pl.when(n_chunks > 0)
    def _():
        in_copy(0, 0).start(priority=ROW_DMA_PRIORITY)
        prologue()

        def body(c, _):
            slot = lax.rem(c, 2)
            in_copy(c, slot).wait()

            @pl.when(c + 1 < n_chunks)
            def _():
                in_copy(c + 1, 1 - slot).start(priority=ROW_DMA_PRIORITY)

            @pl.when(c >= 2)
            def _():
                out_copy(c - 2, slot).wait()

            compute(slot, c)
            out_copy(c, slot).start(priority=ROW_DMA_PRIORITY)
            return 0

        lax.fori_loop(0, n_chunks, body, 0)

        @pl.when(n_chunks >= 2)
        def _():
            out_copy(n_chunks - 2, lax.rem(n_chunks, 2)).wait()

        out_copy(n_chunks - 1, lax.rem(n_chunks - 1, 2)).wait()


def _chunk_rows(start_ref, e, c, n_slots):
    del n_slots
    return pl.ds(pl.multiple_of(start_ref[e] + c * MOE_CHUNK, MOE_ALIGN), MOE_CHUNK)


def _gate_up_kernel(start_ref, cnt_ref, x_hbm, w_hbm, b_ref, act_hbm,
                    w_stage, w_sc, xbuf, obuf, sem_in, sem_out, sem_w):
    e = pl.program_id(0)
    n_slots = x_hbm.shape[0]
    n_chunks = (cnt_ref[e] + MOE_CHUNK - 1) // MOE_CHUNK
    d_ff = obuf.shape[2]

    def w_copy(idx):
        return pltpu.make_async_copy(w_hbm.at[idx], w_stage, sem_w)

    @pl.when(e == 0)
    def _():
        w_copy(0).start()

    has_weights = e < N_EXPERTS

    def await_weights():
        @pl.when(has_weights)
        def _():
            w_copy(e).wait()

    def refill():
        @pl.when(e + 1 < N_EXPERTS)
        def _():
            w_copy(e + 1).start()

    def in_copy(c, slot):
        return pltpu.make_async_copy(x_hbm.at[_chunk_rows(start_ref, e, c, n_slots)], xbuf.at[slot], sem_in.at[slot])

    def out_copy(c, slot):
        return pltpu.make_async_copy(obuf.at[slot], act_hbm.at[_chunk_rows(start_ref, e, c, n_slots)],
                                     sem_out.at[slot])

    def rows_times_weights(slot, cast_first):
        x = xbuf[slot].astype(BF16)
        cw = 256
        for c in range(d_ff // cw):
            gs = slice(c * cw, (c + 1) * cw)
            us = slice(d_ff + c * cw, d_ff + (c + 1) * cw)
            if cast_first:
                w_sc[:, gs] = w_stage[:, gs].astype(BF16)
                w_sc[:, us] = w_stage[:, us].astype(BF16)
            g = jnp.dot(x, w_sc[:, gs], preferred_element_type=F32) + b_ref[:, gs]
            u = jnp.dot(x, w_sc[:, us], preferred_element_type=F32) + b_ref[:, us]
            g = jnp.minimum(g, SWIGLU_LIMIT)
            u = jnp.clip(u, -SWIGLU_LIMIT, SWIGLU_LIMIT)
            obuf[slot, :, gs] = (g * jax.nn.sigmoid(SWIGLU_ALPHA * g) * (u + 1.0)).astype(obuf.dtype)

    def compute(slot, c):
        fresh = (c == 0) & has_weights

        @pl.when(fresh)
        def _():
            rows_times_weights(slot, True)
            refill()

        @pl.when(jnp.logical_not(fresh))
        def _():
            rows_times_weights(slot, False)

    @pl.when(n_chunks == 0)
    def _():
        await_weights()
        refill()

    _expert_rows_pipeline(n_chunks, await_weights, in_copy, out_copy, compute)


def _down_kernel(start_ref, cnt_ref, a_hbm, wd_ref, bd_ref, y_hbm, wd_sc, abuf, obuf, sem_in, sem_out):
    e = pl.program_id(0)
    n_slots = y_hbm.shape[0]
    n_chunks = (cnt_ref[e] + MOE_CHUNK - 1) // MOE_CHUNK

    def cast_weights():
        wd_sc[...] = wd_ref[...].astype(BF16)

    def in_copy(c, slot):
        return pltpu.make_async_copy(a_hbm.at[_chunk_rows(start_ref, e, c, n_slots)], abuf.at[slot],
                                     sem_in.at[slot])

    def out_copy(c, slot):
        return pltpu.make_async_copy(obuf.at[slot], y_hbm.at[_chunk_rows(start_ref, e, c, n_slots)],
                                     sem_out.at[slot])

    def compute(slot, c):
        del c
        cw = 512
        for c in range(obuf.shape[2] // cw):
            cs = slice(c * cw, (c + 1) * cw)
            obuf[slot, :, cs] = jnp.dot(abuf[slot], wd_sc[:, cs], preferred_element_type=F32) + bd_ref[:, cs]

    _expert_rows_pipeline(n_chunks, cast_weights, in_copy, out_copy, compute)


def _experts(xs, starts, counts, w_gate_up, b_gate_up, w_down, b_down):
    n_slots, d = xs.shape
    any_spec = pl.BlockSpec(memory_space=pl.ANY)
    dma_sems = pltpu.SemaphoreType.DMA((2,))
    bgu = b_gate_up.reshape(N_EXPERTS, 1, 2 * D_FF)
    n_groups = starts.shape[0]
    wi = lambda e: jnp.minimum(e, N_EXPERTS - 1)
    act = pl.pallas_call(
        _gate_up_kernel,
        grid_spec=pltpu.PrefetchScalarGridSpec(
            num_scalar_prefetch=2,
            grid=(n_groups,),
            in_specs=[
                any_spec,
                any_spec,
                pl.BlockSpec((None, 1, 2 * D_FF), lambda e, st, ct: (wi(e), 0, 0)),
            ],
            out_specs=any_spec,
            scratch_shapes=[pltpu.VMEM((d, 2 * D_FF), F32), pltpu.VMEM((d, 2 * D_FF), BF16),
                            pltpu.VMEM((2, MOE_CHUNK, d), xs.dtype), pltpu.VMEM((2, MOE_CHUNK, D_FF), BF16),
                            dma_sems, dma_sems, pltpu.SemaphoreType.DMA(())],
        ),
        out_shape=jax.ShapeDtypeStruct((n_slots, D_FF), BF16),
        compiler_params=_cparams(("arbitrary",), 60),
        name="moe_gate_up",
    )(starts, counts, xs, w_gate_up, bgu)
    bd = b_down.reshape(N_EXPERTS, 1, d)
    ys = pl.pallas_call(
        _down_kernel,
        grid_spec=pltpu.PrefetchScalarGridSpec(
            num_scalar_prefetch=2,
            grid=(n_groups,),
            in_specs=[
                any_spec,
                pl.BlockSpec((None, D_FF, d), lambda e, st, ct: (wi(e), 0, 0)),
                pl.BlockSpec((None, 1, d), lambda e, st, ct: (wi(e), 0, 0)),
            ],
            out_specs=any_spec,
            scratch_shapes=[pltpu.VMEM((D_FF, d), BF16),
                            pltpu.VMEM((2, MOE_CHUNK, D_FF), BF16), pltpu.VMEM((2, MOE_CHUNK, d), F32),
                            dma_sems, dma_sems],
        ),
        out_shape=jax.ShapeDtypeStruct((n_slots, d), F32),
        compiler_params=_cparams(("arbitrary",), 56),
        name="moe_down",
    )(starts, counts, act, w_down, bd)
    return ys


def _combine_kernel(dest_ref, dest_next_ref, gate_ref, h_ref, g_ref, beta_ref, ys_ref, o_ref, buf_ref, sem,
                    *, alpha):
    i = pl.program_id(0)
    last = pl.num_programs(0) - 1
    slot = lax.rem(i, 2)
    n = dest_ref.shape[0]

    def gather(dref, sl):
        def start(g, _):
            base = pl.multiple_of(g * ROW_TILE, ROW_TILE)
            for u in range(ROW_TILE * TOP_K):
                r, k = divmod(u, TOP_K)
                pltpu.make_async_copy(ys_ref.at[pl.ds(dref[g * (ROW_TILE * TOP_K) + u], 1)],
                                      buf_ref.at[sl, k, pl.ds(base + r, 1)],
                                      sem.at[sl]).start(priority=u % 2)
            return 0
        lax.fori_loop(0, n // (ROW_TILE * TOP_K), start, 0)

    for parity in range(2):
        @pl.when((i == 0) & (slot == parity))
        def _():
            gather(dest_ref, parity)

        @pl.when((i < last) & (slot == parity))
        def _():
            gather(dest_next_ref, 1 - parity)

    pltpu.make_async_copy(buf_ref.at[slot], buf_ref.at[slot], sem.at[slot]).wait()
    gates = gate_ref[...]
    y = gates[:, 0:1] * buf_ref[slot, 0]
    for k in range(1, TOP_K):
        y = y + gates[:, k:k + 1] * buf_ref[slot, k]
    o_ref[...] = _layer_norm(alpha * h_ref[...] + y, g_ref[...], beta_ref[...])


def _combine(ys, dest_flat, gates, h, ln_g, ln_b, alpha):
    s, d = h.shape
    tm = min(128, s)
    n_steps = s // tm
    return pl.pallas_call(
        functools.partial(_combine_kernel, alpha=alpha),
        grid=(n_steps,),
        in_specs=[
            pl.BlockSpec((tm * TOP_K,), lambda i: (i,), memory_space=pltpu.SMEM),
            pl.BlockSpec((tm * TOP_K,), lambda i: (jnp.minimum(i + 1, n_steps - 1),), memory_space=pltpu.SMEM),
            pl.BlockSpec((tm, TOP_K), lambda i: (i, 0)),
            pl.BlockSpec((tm, d), lambda i: (i, 0)),
            pl.BlockSpec((1, d), lambda i: (0, 0)),
            pl.BlockSpec((1, d), lambda i: (0, 0)),
            pl.BlockSpec(memory_space=pl.ANY),
        ],
        out_specs=pl.BlockSpec((tm, d), lambda i: (i, 0)),
        out_shape=jax.ShapeDtypeStruct((s, d), F32),
        scratch_shapes=[pltpu.VMEM((2, TOP_K, tm, d), F32), pltpu.SemaphoreType.DMA((2,))],
        compiler_params=_cparams(("arbitrary",), 32),
        name="moe_combine",
    )(dest_flat, dest_flat, gates, h, ln_g, ln_b, ys)


def _layer(x2d, w_in, b_forget, w_branch_a, w_branch_b, w_out, ln1_g, ln1_b, w_router, b_router,
           w_gate_up, b_gate_up, w_down, b_down, ln2_g, ln2_b, tables):
    s, d = x2d.shape
    alpha = (2.0 * DEPTH) ** 0.25
    scale = HEAD_DIM ** -0.5 * LOG2E

    w_bf, w_g = _prepare_w_in(w_in.T, scale)
    pbf, pg = _project(x2d, w_bf, tables, w_g)

    fox_tq = min(512, s)
    fox_tk = min(512, s)
    af_t = pg[:, COL_SMALL:COL_SMALL + FOX_HEADS].T
    c_t = _forget_cumsum(af_t, b_forget)
    c_tiles = c_t.reshape(FOX_HEADS, s // fox_tk, 1, fox_tk)
    a_out = _fox_attention(pbf, c_tiles, s, fox_tq, fox_tk, 4)

    b_out = _dsa_attention(pbf, pg, s, min(256, s), min(512, s))

    h, eidx, gates, pos, counts = _merge(
        a_out, b_out, pg, x2d, w_branch_a.astype(BF16), w_branch_b.astype(BF16), w_out.astype(BF16),
        ln1_g.reshape(1, d), ln1_b.reshape(1, d), w_router, b_router.reshape(1, N_EXPERTS), alpha)

    counts = counts.reshape(N_EXPERTS).astype(I32)
    aligned = (counts + MOE_ALIGN - 1) // MOE_ALIGN * MOE_ALIGN
    ends = jnp.cumsum(aligned).astype(I32)
    starts = ends - aligned
    bound = s * TOP_K + N_EXPERTS * MOE_ALIGN + 2 * MOE_CHUNK
    n_slots = (bound + MOE_CHUNK - 1) // MOE_CHUNK * MOE_CHUNK
    experts = jnp.arange(N_EXPERTS, dtype=I32)
    start_of = jnp.sum(jnp.where(eidx[..., None] == experts, starts, 0), axis=-1)
    dest = (start_of + pos).reshape(s * TOP_K).astype(I32)
    total = (ends[-1:] + MOE_CHUNK - 1) // MOE_CHUNK * MOE_CHUNK
    gap_len = aligned - counts + jnp.where(experts == N_EXPERTS - 1, total[0] - ends[-1], 0)
    group_starts = jnp.concatenate([starts, total])
    group_counts = jnp.concatenate([counts + jnp.where(experts == N_EXPERTS - 1, gap_len, 0), n_slots - total])

    xs = _dispatch(h, dest, starts + counts, gap_len, total, n_slots)
    ys = _experts(xs, group_starts, group_counts, w_gate_up, b_gate_up, w_down, b_down)
    return _combine(ys, dest, gates, h, ln2_g.reshape(1, d), ln2_b.reshape(1, d), alpha)


def kernel(x, w_in, b_forget, w_branch_a, w_branch_b, w_out, ln1_g, ln1_b, w_router, b_router,
           w_gate_up, b_gate_up, w_down, b_down, ln2_g, ln2_b):
    bsz, s, d = x.shape
    tables = _rope_tables(s)
    outs = []
    for bi in range(bsz):
        xb = x[bi]
        for l in range(DEPTH):
            xb = _layer(xb, w_in[l], b_forget[l], w_branch_a[l], w_branch_b[l], w_out[l], ln1_g[l], ln1_b[l],
                        w_router[l], b_router[l], w_gate_up[l], b_gate_up[l], w_down[l], b_down[l],
                        ln2_g[l], ln2_b[l], tables)
        outs.append(xb)
    return outs[0][None] if bsz == 1 else jnp.stack(outs)
```

```python
import functools

import numpy as np
import jax
import jax.numpy as jnp
from jax import lax
from jax.experimental import pallas as pl
from jax.experimental.pallas import tpu as pltpu

F32 = jnp.float32
BF16 = jnp.bfloat16
I32 = jnp.int32

D_MODEL = 2048
DEPTH = 1
CHUNK = 64
HEAD_DIM = 128
FOX_HEADS = 8
DSA_HEADS = 8
DSA_KV_HEADS = 2
DSA_GROUP = DSA_HEADS // DSA_KV_HEADS
IDX_HEADS = 16
IDX_DIM = 64
TOPK_MAX = 256
ROPE_THETA = 500000.0
ROT_FRACTION_DEN = 4
MIX_A = FOX_HEADS * HEAD_DIM
MIX_B = DSA_HEADS * HEAD_DIM
N_EXPERTS = 32
TOP_K = 4
D_FF = D_MODEL
SWIGLU_ALPHA = 1.702
SWIGLU_LIMIT = 7.0
LN_EPS = 1e-5

MIB = 1024 * 1024
NEG_BIG = -1e30
LOG2E = 1.4426950408889634
NT_DIMS = (((1,), (1,)), ((), ()))
INT_MIN = -(2 ** 31)

COL_AQ = 0
COL_AK = 1024
COL_AV = 2048
COL_BQ = 3072
COL_BK = 4096
COL_BV = 4352
COL_IQ = 4608
COL_IK = 5632
N_PBF = 5888
PROJ_TN = 256
ROPE_HEAD_TILES = (12, 13, 14, 15, 16)
ROPE_IDX_TILES = (18, 19, 20, 21, 22)
COL_GA = 0
COL_GB = 2048
COL_SMALL = 4096
N_PG = 4224
PG_TN = 384

MOE_CHUNK = 256
MOE_ALIGN = 16
ROW_TILE = 8


def _cparams(dims, vmem_mib):
    return pltpu.CompilerParams(dimension_semantics=dims, vmem_limit_bytes=vmem_mib * MIB)


IN_SIZES = (MIX_A, MIX_A, MIX_A, FOX_HEADS, MIX_B, DSA_KV_HEADS * HEAD_DIM, DSA_KV_HEADS * HEAD_DIM,
            IDX_HEADS * IDX_DIM, IDX_DIM, IDX_HEADS, D_MODEL, D_MODEL)
IN_OFF = dict(zip(("aq", "ak", "av", "af", "bq", "bk", "bv", "iq", "ik", "iw", "ga", "gb", "end"),
                  np.concatenate([[0], np.cumsum(IN_SIZES)]).tolist()))


def _prep_kernel(w_ref, wbf_ref, wg_ref, *, scale):
    o = IN_OFF
    width = w_ref.shape[1]

    def rows(a, b, mult=None):
        v = w_ref[a:b, :]
        return v if mult is None else v * mult

    wbf_ref[COL_AQ:COL_AK, :] = rows(o["aq"], o["ak"], scale).astype(BF16)
    wbf_ref[COL_AK:COL_BQ, :] = rows(o["ak"], o["af"]).astype(BF16)
    wbf_ref[COL_BQ:COL_BK, :] = rows(o["bq"], o["bk"], scale).astype(BF16)
    wbf_ref[COL_BK:COL_IK, :] = rows(o["bk"], o["ik"]).astype(BF16)
    wbf_ref[COL_IK:N_PBF, :] = jnp.concatenate(
        [rows(o["ik"], o["iw"]), jnp.zeros((N_PBF - COL_IK - IDX_DIM, width), F32)], axis=0).astype(BF16)
    wg_ref[COL_GA:COL_SMALL, :] = rows(o["ga"], o["end"]).astype(BF16)
    wg_ref[COL_SMALL:N_PG, :] = jnp.concatenate(
        [rows(o["af"], o["bq"]), rows(o["iw"], o["ga"]),
         jnp.zeros((N_PG - COL_SMALL - FOX_HEADS - IDX_HEADS, width), F32)], axis=0).astype(BF16)


def _prepare_w_in(w_in_t, scale):
    n_in, d = w_in_t.shape
    tc = 256
    return pl.pallas_call(
        functools.partial(_prep_kernel, scale=scale),
        grid=(d // tc,),
        in_specs=[pl.BlockSpec((n_in, tc), lambda i: (0, i))],
        out_specs=[pl.BlockSpec((N_PBF, tc), lambda i: (0, i)), pl.BlockSpec((N_PG, tc), lambda i: (0, i))],
        out_shape=[jax.ShapeDtypeStruct((N_PBF, d), BF16), jax.ShapeDtypeStruct((N_PG, d), BF16)],
        compiler_params=_cparams(("arbitrary",), 48),
        name="w_in_prep",
    )(w_in_t)


def _tile_in(j, tiles):
    cond = j == tiles[0]
    for t in tiles[1:]:
        cond = cond | (j == t)
    return cond


def _proj_rope_kernel(x_ref, w_ref, tab_ref, o_ref, xb_ref):
    j = pl.program_id(1)

    @pl.when(j == 0)
    def _():
        xb_ref[...] = x_ref[...].astype(BF16)

    acc = lax.dot_general(xb_ref[...], w_ref[...], NT_DIMS, preferred_element_type=F32)
    tn = acc.shape[1]
    is_head = _tile_in(j, ROPE_HEAD_TILES)
    is_idx = _tile_in(j, ROPE_IDX_TILES)

    def rope(shift):
        c = tab_ref[0, 0]
        s_prev = tab_ref[0, 1]
        s_next = tab_ref[0, 2]
        out = acc * c + pltpu.roll(acc, shift, 1) * s_prev + pltpu.roll(acc, tn - shift, 1) * s_next
        o_ref[...] = out.astype(o_ref.dtype)

    @pl.when(is_head)
    def _():
        rope(HEAD_DIM // ROT_FRACTION_DEN // 2)

    @pl.when(is_idx)
    def _():
        rope(IDX_DIM // ROT_FRACTION_DEN // 2)

    @pl.when(jnp.logical_not(is_head | is_idx))
    def _():
        o_ref[...] = acc.astype(o_ref.dtype)


def _proj_plain_kernel(x_ref, w_ref, o_ref, xb_ref):
    j = pl.program_id(1)

    @pl.when(j == 0)
    def _():
        xb_ref[...] = x_ref[...].astype(BF16)

    o_ref[...] = lax.dot_general(xb_ref[...], w_ref[...], NT_DIMS, preferred_element_type=F32).astype(o_ref.dtype)


def _rope_group(j):
    return jnp.where(_tile_in(j, ROPE_IDX_TILES), 1, 0)


def _project(x2d, w_bf, tables, w_g):
    s, d = x2d.shape
    tm = min(1024, s)
    pbf = pl.pallas_call(
        _proj_rope_kernel,
        grid=(s // tm, N_PBF // PROJ_TN),
        in_specs=[
            pl.BlockSpec((tm, d), lambda i, j: (i, 0)),
            pl.BlockSpec((PROJ_TN, d), lambda i, j: (j, 0)),
            pl.BlockSpec((1, 3, tm, PROJ_TN), lambda i, j: (_rope_group(j), 0, i, 0)),
        ],
        out_specs=pl.BlockSpec((tm, PROJ_TN), lambda i, j: (i, j)),
        out_shape=jax.ShapeDtypeStruct((s, N_PBF), BF16),
        scratch_shapes=[pltpu.VMEM((tm, d), BF16)],
        compiler_params=_cparams(("arbitrary", "arbitrary"), 48),
        name="proj_bf16",
    )(x2d, w_bf, tables)
    pg = pl.pallas_call(
        _proj_plain_kernel,
        grid=(s // tm, N_PG // PG_TN),
        in_specs=[
            pl.BlockSpec((tm, d), lambda i, j: (i, 0)),
            pl.BlockSpec((PG_TN, d), lambda i, j: (j, 0)),
        ],
        out_specs=pl.BlockSpec((tm, PG_TN), lambda i, j: (i, j)),
        out_shape=jax.ShapeDtypeStruct((s, N_PG), F32),
        scratch_shapes=[pltpu.VMEM((tm, d), BF16)],
        compiler_params=_cparams(("arbitrary", "arbitrary"), 48),
        name="proj_f32",
    )(x2d, w_g)
    return pbf, pg


def _rope_tables(s):
    pos = jnp.arange(s, dtype=F32)

    def one(period):
        rot = period // ROT_FRACTION_DEN
        half = rot // 2
        inv = jnp.power(ROPE_THETA, -jnp.arange(0, rot, 2, dtype=F32) / rot)
        ang = pos[:, None] * inv[None, :]
        cos, sin = jnp.cos(ang), jnp.sin(ang)
        zero = jnp.zeros((s, period - rot), F32)
        c = jnp.concatenate([cos, cos, jnp.ones((s, period - rot), F32)], axis=1)
        s_prev = jnp.concatenate([jnp.zeros((s, half), F32), sin, zero], axis=1)
        s_next = jnp.concatenate([-sin, jnp.zeros((s, half), F32), zero], axis=1)
        reps = PROJ_TN // period
        return jnp.stack([jnp.tile(c, (1, reps)), jnp.tile(s_prev, (1, reps)), jnp.tile(s_next, (1, reps))])

    return jnp.stack([one(HEAD_DIM), one(IDX_DIM)])


def _cumsum_kernel(af_ref, bf_ref, c_ref, carry_ref):
    i = pl.program_id(0)

    @pl.when(i == 0)
    def _():
        carry_ref[...] = jnp.zeros_like(carry_ref)

    z = af_ref[...] + bf_ref[...]
    logf = jnp.minimum(z, 0.0) - jnp.log1p(jnp.exp(-jnp.abs(z)))
    t = z.shape[1]
    row = lax.broadcasted_iota(I32, (t, t), 0)
    col = lax.broadcasted_iota(I32, (t, t), 1)
    upper = (row <= col).astype(F32)
    c = jnp.dot(logf, upper, preferred_element_type=F32, precision=lax.Precision.HIGHEST) + carry_ref[...]
    c_ref[...] = c * LOG2E
    carry_ref[...] = c[:, t - 1:t]


def _forget_cumsum(af_t, b_forget):
    h, s = af_t.shape
    t = min(512, s)
    return pl.pallas_call(
        _cumsum_kernel,
        grid=(s // t,),
        in_specs=[pl.BlockSpec((h, t), lambda i: (0, i)), pl.BlockSpec((h, 1), lambda i: (0, 0))],
        out_specs=pl.BlockSpec((h, t), lambda i: (0, i)),
        out_shape=jax.ShapeDtypeStruct((h, s), F32),
        scratch_shapes=[pltpu.VMEM((h, 1), F32)],
        compiler_params=_cparams(("arbitrary",), 32),
        name="forget_cumsum",
    )(af_t, b_forget.reshape(h, 1).astype(F32))


def _softmax_init(m_ref, l_ref, acc_ref):
    m_ref[...] = jnp.full(m_ref.shape, NEG_BIG, F32)
    l_ref[...] = jnp.zeros(l_ref.shape, F32)
    acc_ref[...] = jnp.zeros(acc_ref.shape, F32)


def _softmax_tile(s, v, m_ref, l_ref, acc_ref, i):
    reps = s.shape[1] // 128
    m_old = m_ref[i]
    m_new = jnp.maximum(m_old, jnp.max(s, axis=1, keepdims=True))
    p = jnp.exp2(s - jnp.tile(m_new, (1, reps)))
    alpha = jnp.exp2(m_old - m_new)
    psum = p[:, :128]
    for c in range(1, reps):
        psum = psum + p[:, c * 128:(c + 1) * 128]
    l_ref[i] = alpha * l_ref[i] + psum
    acc_ref[i] = alpha * acc_ref[i] + jnp.dot(p.astype(BF16), v, preferred_element_type=F32)
    m_ref[i] = m_new


def _softmax_result(l_ref, acc_ref, i):
    return acc_ref[i] / jnp.sum(l_ref[i], axis=1, keepdims=True)


def _fox_kernel(q_ref, k_ref, v_ref, c_ref, o_ref, m_ref, l_ref, acc_ref, *, tq, tk, nh):
    qi = pl.program_id(1)
    _softmax_init(m_ref, l_ref, acc_ref)

    def tile(kt, masked):
        start = pl.multiple_of(kt * tk, tk)
        if masked:
            row = qi * tq + lax.broadcasted_iota(I32, (tq, tk), 0)
            col = kt * tk + lax.broadcasted_iota(I32, (tq, tk), 1)
            causal = col <= row
        for h in range(nh):
            hs = slice(h * HEAD_DIM, (h + 1) * HEAD_DIM)
            k = k_ref[pl.ds(start, tk), hs]
            v = v_ref[pl.ds(start, tk), hs]
            s = lax.dot_general(q_ref[:, hs], k, (((1,), (1,)), ((), ())), preferred_element_type=F32)
            s = s - c_ref[h, kt]
            if masked:
                s = jnp.where(causal, s, NEG_BIG)
            _softmax_tile(s, v, m_ref, l_ref, acc_ref, h)

    n_full = (qi * tq) // tk

    def body(kt, _):
        tile(kt, False)
        return 0

    lax.fori_loop(0, n_full, body, 0)
    for t in range((tq + tk - 1) // tk):
        tile(n_full + t, True)
    for h in range(nh):
        o_ref[:, h * HEAD_DIM:(h + 1) * HEAD_DIM] = _softmax_result(l_ref, acc_ref, h).astype(o_ref.dtype)


def _fox_attention(pbf, c_tiles, s, tq, tk, nh):
    cb = nh * HEAD_DIM
    return pl.pallas_call(
        functools.partial(_fox_kernel, tq=tq, tk=tk, nh=nh),
        grid=(FOX_HEADS // nh, s // tq),
        in_specs=[
            pl.BlockSpec((tq, cb), lambda g, i: (i, COL_AQ // cb + g)),
            pl.BlockSpec((s, cb), lambda g, i: (0, COL_AK // cb + g)),
            pl.BlockSpec((s, cb), lambda g, i: (0, COL_AV // cb + g)),
            pl.BlockSpec((nh, s // tk, 1, tk), lambda g, i: (g, 0, 0, 0)),
        ],
        out_specs=pl.BlockSpec((tq, cb), lambda g, i: (i, g)),
        out_shape=jax.ShapeDtypeStruct((s, MIX_A), BF16),
        scratch_shapes=[
            pltpu.VMEM((nh, tq, 128), F32),
            pltpu.VMEM((nh, tq, 128), F32),
            pltpu.VMEM((nh, tq, HEAD_DIM), F32),
        ],
        compiler_params=_cparams(("arbitrary", "arbitrary"), 56),
        name="fox_attention",
    )(pbf, pbf, pbf, c_tiles)


def _dsa_kernel(bq_ref, iq_lo_ref, iq_hi_ref, iw_ref, ik_ref, bk_ref, bv_ref, o_ref,
                key_ref, qg_ref, m_ref, l_ref, acc_ref, *, tq, tk, n_sel):
    b = pl.program_id(0)
    n_tiles = (b * tq) // tk + 1
    row_g = b * tq + lax.broadcasted_iota(I32, (tq, tk), 0)
    adm_end = (row_g // CHUNK + 1) * CHUNK
    col_l = lax.broadcasted_iota(I32, (tq, tk), 1)

    idx_scale = (IDX_HEADS ** -0.5) * (IDX_DIM ** -0.5)
    iw = iw_ref[...][:, 8:8 + IDX_HEADS] * idx_scale
    iq = jnp.concatenate([iq_lo_ref[...], iq_hi_ref[...]], axis=1)

    def score_tile(kt, _):
        start = pl.multiple_of(kt * tk, tk)
        ik = ik_ref[pl.ds(start, tk), :][:, :IDX_DIM]
        acc = jnp.zeros((tq, tk), F32)
        for h in range(IDX_HEADS):
            a = iq[:, h * IDX_DIM:(h + 1) * IDX_DIM]
            rel = lax.dot_general(a, ik, (((1,), (1,)), ((), ())), preferred_element_type=F32)
            acc = acc + jnp.maximum(rel, 0.0) * iw[:, h:h + 1]
        key_ref[kt] = jnp.where(kt * tk + col_l < adm_end, acc, -jnp.inf)
        return 0

    lax.fori_loop(0, n_tiles, score_tile, 0)

    def float_of(code_u):
        code = code_u ^ INT_MIN
        return pltpu.bitcast(jnp.where(code >= 0, code, code ^ 0x7FFFFFFF), F32)

    def count_ge(cand):
        def body(kt, part):
            ge = jnp.where(key_ref[kt] >= cand, 1, 0)
            for c in range(tk // 128):
                part = part + ge[:, c * 128:(c + 1) * 128]
            return part
        part = lax.fori_loop(0, n_tiles, body, jnp.zeros((tq, 128), I32))
        return jnp.sum(part, axis=1, keepdims=True)

    def search_cond(carry):
        i, _, _, pending = carry
        return (i < 32) & (pending > 0)

    def refine(i, t_u, hit):
        cand_u = t_u | lax.shift_left(jnp.int32(1), 31 - i)
        cnt = count_ge(float_of(cand_u))
        return jnp.where(cnt >= n_sel, cand_u, t_u), jnp.where(cnt == n_sel, 1, hit)

    def search_step(carry):
        i, t_u, hit, _ = carry
        t_u, hit = refine(i, t_u, hit)
        return i + 1, t_u, hit, jnp.sum(1 - hit)

    zeros = jnp.zeros((tq, 1), I32)
    lead = 12
    t_u, hit = lax.fori_loop(0, lead, lambda i, c: refine(i, *c), (zeros, zeros))
    _, t_u, _, _ = lax.while_loop(search_cond, search_step, (jnp.int32(lead), t_u, hit, jnp.sum(1 - hit)))
    thr = jnp.where(t_u == 0, jnp.finfo(F32).min, float_of(t_u))

    for j in range(DSA_KV_HEADS):
        for g in range(DSA_GROUP):
            hd = j * DSA_GROUP + g
            qg_ref[j, g * tq:(g + 1) * tq, :] = bq_ref[:, hd * HEAD_DIM:(hd + 1) * HEAD_DIM]
    _softmax_init(m_ref, l_ref, acc_ref)

    def attn_tile(kt, _):
        start = pl.multiple_of(kt * tk, tk)
        sel = key_ref[kt] >= thr
        for j in range(DSA_KV_HEADS):
            hs = slice(j * HEAD_DIM, (j + 1) * HEAD_DIM)
            k = bk_ref[pl.ds(start, tk), hs]
            v = bv_ref[pl.ds(start, tk), hs]
            s = lax.dot_general(qg_ref[j], k, (((1,), (1,)), ((), ())), preferred_element_type=F32)
            s = jnp.where(sel[None], s.reshape(DSA_GROUP, tq, tk), NEG_BIG).reshape(DSA_GROUP * tq, tk)
            _softmax_tile(s, v, m_ref, l_ref, acc_ref, j)
        return 0

    lax.fori_loop(0, n_tiles, attn_tile, 0)

    for j in range(DSA_KV_HEADS):
        o = _softmax_result(l_ref, acc_ref, j)
        for g in range(DSA_GROUP):
            hd = j * DSA_GROUP + g
            o_ref[:, hd * HEAD_DIM:(hd + 1) * HEAD_DIM] = o[g * tq:(g + 1) * tq].astype(o_ref.dtype)


def _dsa_attention(pbf, pg, s, tq, tk):
    n_sel = min(TOPK_MAX, s // 4)
    kvw = DSA_KV_HEADS * HEAD_DIM
    rows = DSA_GROUP * tq
    iqw = IDX_HEADS * IDX_DIM // 2
    return pl.pallas_call(
        functools.partial(_dsa_kernel, tq=tq, tk=tk, n_sel=n_sel),
        grid=(s // tq,),
        in_specs=[
            pl.BlockSpec((tq, MIX_B), lambda b: (b, COL_BQ // MIX_B)),
            pl.BlockSpec((tq, iqw), lambda b: (b, COL_IQ // iqw)),
            pl.BlockSpec((tq, iqw), lambda b: (b, COL_IQ // iqw + 1)),
            pl.BlockSpec((tq, 128), lambda b: (b, COL_SMALL // 128)),
            pl.BlockSpec((s, 128), lambda b: (0, COL_IK // 128)),
            pl.BlockSpec((s, kvw), lambda b: (0, COL_BK // kvw)),
            pl.BlockSpec((s, kvw), lambda b: (0, COL_BV // kvw)),
        ],
        out_specs=pl.BlockSpec((tq, MIX_B), lambda b: (b, 0)),
        out_shape=jax.ShapeDtypeStruct((s, MIX_B), BF16),
        scratch_shapes=[
            pltpu.VMEM((s // tk, tq, tk), F32),
            pltpu.VMEM((DSA_KV_HEADS, rows, HEAD_DIM), BF16),
            pltpu.VMEM((DSA_KV_HEADS, rows, 128), F32),
            pltpu.VMEM((DSA_KV_HEADS, rows, 128), F32),
            pltpu.VMEM((DSA_KV_HEADS, rows, HEAD_DIM), F32),
        ],
        compiler_params=_cparams(("arbitrary",), 56),
        name="dsa_attention",
    )(pbf, pbf, pbf, pg, pbf, pbf, pbf)


def _layer_norm(z, g, b):
    mu = jnp.mean(z, axis=-1, keepdims=True)
    zc = z - mu
    var = jnp.mean(zc * zc, axis=-1, keepdims=True)
    return zc * lax.rsqrt(var + LN_EPS) * g + b


def _merge_kernel(a_ref, b_ref, ga_ref, gb_ref, x_ref, wa_ref, wb_ref, wo_ref, g_ref, beta_ref,
                  wr_ref, br_ref, h_ref, eidx_ref, gate_ref, pos_ref, cnt_ref, carry_ref, *, alpha):
    i = pl.program_id(0)

    @pl.when(i == 0)
    def _():
        carry_ref[...] = jnp.zeros_like(carry_ref)

    ma = jnp.dot(a_ref[...], wa_ref[...], preferred_element_type=F32)
    mb = jnp.dot(b_ref[...], wb_ref[...], preferred_element_type=F32)
    merged = jax.nn.sigmoid(ga_ref[...]) * ma + jax.nn.sigmoid(gb_ref[...]) * mb
    y = jnp.dot(merged.astype(BF16), wo_ref[...], preferred_element_type=F32)
    h = _layer_norm(alpha * x_ref[...] + y, g_ref[...], beta_ref[...])
    h_ref[...] = h

    wr = wr_ref[...]
    h_hi = h.astype(BF16)
    h_lo = (h - h_hi.astype(F32)).astype(BF16)
    wr_hi = wr.astype(BF16)
    wr_lo = (wr - wr_hi.astype(F32)).astype(BF16)
    logits = (jnp.dot(h_hi, wr_hi, preferred_element_type=F32) + jnp.dot(h_lo, wr_hi, preferred_element_type=F32)
              + jnp.dot(h_hi, wr_lo, preferred_element_type=F32) + br_ref[...])
    tm, ne = logits.shape
    lane = lax.broadcasted_iota(I32, (tm, ne), 1)
    lane_k = lax.broadcasted_iota(I32, (tm, TOP_K), 1)
    work = logits
    vals, sels = [], []
    eidx = jnp.zeros((tm, TOP_K), I32)
    onehot = jnp.zeros((tm, ne), F32)
    for k in range(TOP_K):
        mv = jnp.max(work, axis=1, keepdims=True)
        idx = jnp.min(jnp.where(work == mv, lane, ne), axis=1, keepdims=True)
        sel = lane == idx
        vals.append(mv)
        sels.append(sel)
        eidx = jnp.where(lane_k == k, idx, eidx)
        onehot = onehot + jnp.where(sel, 1.0, 0.0)
        work = jnp.where(sel, -jnp.inf, work)
    exps = [jnp.exp(v - vals[0]) for v in vals]
    denom = exps[0] + exps[1] + exps[2] + exps[3]
    gates = jnp.zeros((tm, TOP_K), F32)
    for k in range(TOP_K):
        gates = jnp.where(lane_k == k, exps[k] / denom, gates)

    r_i = lax.broadcasted_iota(I32, (tm, tm), 0)
    c_i = lax.broadcasted_iota(I32, (tm, tm), 1)
    lower = jnp.where(c_i < r_i, 1.0, 0.0).astype(BF16)
    rank = carry_ref[...] + jnp.dot(lower, onehot.astype(BF16), preferred_element_type=F32)
    pos = jnp.zeros((tm, TOP_K), I32)
    for k in range(TOP_K):
        pk = jnp.sum(jnp.where(sels[k], rank, 0.0), axis=1, keepdims=True).astype(I32)
        pos = jnp.where(lane_k == k, pk, pos)
    carry_ref[...] = carry_ref[...] + jnp.sum(onehot, axis=0, keepdims=True)

    eidx_ref[...] = eidx
    gate_ref[...] = gates
    pos_ref[...] = pos
    cnt_ref[...] = carry_ref[...].astype(I32)


def _merge(a_out, b_out, pg, x2d, wa, wb, wo, ln_g, ln_b, w_router, b_router, alpha):
    s, d = x2d.shape
    tm = min(256, s)
    full = lambda shape: pl.BlockSpec(shape, lambda i: (0,) * len(shape))
    return pl.pallas_call(
        functools.partial(_merge_kernel, alpha=alpha),
        grid=(s // tm,),
        in_specs=[
            pl.BlockSpec((tm, MIX_A), lambda i: (i, 0)),
            pl.BlockSpec((tm, MIX_B), lambda i: (i, 0)),
            pl.BlockSpec((tm, d), lambda i: (i, COL_GA // d)),
            pl.BlockSpec((tm, d), lambda i: (i, COL_GB // d)),
            pl.BlockSpec((tm, d), lambda i: (i, 0)),
            full((MIX_A, d)), full((MIX_B, d)), full((d, d)),
            full((1, d)), full((1, d)), full((d, N_EXPERTS)), full((1, N_EXPERTS)),
        ],
        out_specs=[
            pl.BlockSpec((tm, d), lambda i: (i, 0)),
            pl.BlockSpec((tm, TOP_K), lambda i: (i, 0)),
            pl.BlockSpec((tm, TOP_K), lambda i: (i, 0)),
            pl.BlockSpec((tm, TOP_K), lambda i: (i, 0)),
            full((1, N_EXPERTS)),
        ],
        out_shape=[
            jax.ShapeDtypeStruct((s, d), F32),
            jax.ShapeDtypeStruct((s, TOP_K), I32),
            jax.ShapeDtypeStruct((s, TOP_K), F32),
            jax.ShapeDtypeStruct((s, TOP_K), I32),
            jax.ShapeDtypeStruct((1, N_EXPERTS), I32),
        ],
        scratch_shapes=[pltpu.VMEM((1, N_EXPERTS), F32)],
        compiler_params=_cparams(("arbitrary",), 56),
        name="merge_ln_router",
    )(a_out, b_out, pg, pg, x2d, wa, wb, wo, ln_g, ln_b, w_router, b_router)


def _dispatch_kernel(dest_ref, gap_start_ref, gap_len_ref, total_ref, h_ref, xs_ref, stage_ref, zero_ref,
                     sem, zsem, *, max_slack):
    i = pl.program_id(0)
    last = pl.num_programs(0) - 1
    slot = lax.rem(i, 2)
    n = dest_ref.shape[0]

    @pl.when(i == 0)
    def _():
        zero_ref[...] = jnp.zeros_like(zero_ref)

        def gap_copy(e, k):
            return pltpu.make_async_copy(zero_ref.at[pl.ds(0, 1)], xs_ref.at[pl.ds(gap_start_ref[e] + k, 1)], zsem)

        n_slots = xs_ref.shape[0]

        def slack_copy(k):
            row = pl.multiple_of(total_ref[0] + k * MOE_CHUNK, MOE_CHUNK)
            return pltpu.make_async_copy(zero_ref, xs_ref.at[pl.ds(row, MOE_CHUNK)], zsem)

        def slack(action):
            for k in range(max_slack):
                @pl.when(total_ref[0] + k * MOE_CHUNK < n_slots)
                def _():
                    action(slack_copy(k))

        def issue(e, _):
            lax.fori_loop(0, gap_len_ref[e], lambda k, c: (gap_copy(e, k).start(), c)[1], 0)
            return 0

        def settle(e, _):
            lax.fori_loop(0, gap_len_ref[e], lambda k, c: (gap_copy(e, k).wait(), c)[1], 0)
            return 0

        lax.fori_loop(0, N_EXPERTS, issue, 0)
        slack(lambda cp: cp.start())
        lax.fori_loop(0, N_EXPERTS, settle, 0)
        slack(lambda cp: cp.wait())

    def drain(sl):
        pltpu.make_async_copy(xs_ref.at[pl.ds(0, n)], xs_ref.at[pl.ds(0, n)], sem.at[sl]).wait()

    @pl.when(i >= 2)
    def _():
        drain(slot)

    stage_ref[slot] = h_ref[...]

    def scatter(sl):
        def start(g, _):
            base = pl.multiple_of(g * ROW_TILE, ROW_TILE)
            for u in range(ROW_TILE * TOP_K):
                r = u // TOP_K
                pltpu.make_async_copy(stage_ref.at[sl, pl.ds(base + r, 1)],
                                      xs_ref.at[pl.ds(dest_ref[g * (ROW_TILE * TOP_K) + u], 1)],
                                      sem.at[sl]).start(priority=u % 2)
            return 0
        lax.fori_loop(0, n // (ROW_TILE * TOP_K), start, 0)

    for parity in range(2):
        @pl.when(slot == parity)
        def _():
            scatter(parity)

    @pl.when(i == last)
    def _():
        drain(slot)

        @pl.when(i >= 1)
        def _():
            drain(1 - slot)


def _dispatch(h, dest_flat, gap_start, gap_len, total, n_slots):
    s, d = h.shape
    tm = min(128, s)
    smem = pl.BlockSpec(memory_space=pltpu.SMEM)
    max_slack = (n_slots - s * TOP_K + MOE_CHUNK - 1) // MOE_CHUNK
    return pl.pallas_call(
        functools.partial(_dispatch_kernel, max_slack=max_slack),
        grid=(s // tm,),
        in_specs=[
            pl.BlockSpec((tm * TOP_K,), lambda i: (i,), memory_space=pltpu.SMEM),
            smem, smem, smem,
            pl.BlockSpec((tm, d), lambda i: (i, 0)),
        ],
        out_specs=pl.BlockSpec(memory_space=pl.ANY),
        out_shape=jax.ShapeDtypeStruct((n_slots, d), h.dtype),
        scratch_shapes=[pltpu.VMEM((2, tm, d), h.dtype), pltpu.VMEM((MOE_CHUNK, d), h.dtype),
                        pltpu.SemaphoreType.DMA((2,)), pltpu.SemaphoreType.DMA(())],
        compiler_params=_cparams(("arbitrary",), 32),
        name="moe_dispatch",
    )(dest_flat, gap_start, gap_len, total, h)


ROW_DMA_PRIORITY = 1


def _expert_rows_pipeline(n_chunks, prologue, in_copy, out_copy, compute):
    @pl.when(n_chunks > 0)
    def _():
        in_copy(0, 0).start(priority=ROW_DMA_PRIORITY)
        prologue()

        def body(c, _):
            slot = lax.rem(c, 2)
            in_copy(c, slot).wait()

            @pl.when(c + 1 < n_chunks)
            def _():
                in_copy(c + 1, 1 - slot).start(priority=ROW_DMA_PRIORITY)

            @pl.when(c >= 2)
            def _():
                out_copy(c - 2, slot).wait()

            compute(slot, c)
            out_copy(c, slot).start(priority=ROW_DMA_PRIORITY)
            return 0

        lax.fori_loop(0, n_chunks, body, 0)

        @pl.when(n_chunks >= 2)
        def _():
            out_copy(n_chunks - 2, lax.rem(n_chunks, 2)).wait()

        out_copy(n_chunks - 1, lax.rem(n_chunks - 1, 2)).wait()


WEIGHT_COPY_PARTS = 8


class _SplitCopy:
    def __init__(self, src_hbm, idx, dst, sem):
        rows = dst.shape[0] // WEIGHT_COPY_PARTS
        self.parts = [
            pltpu.make_async_copy(src_hbm.at[idx, pl.ds(p * rows, rows)], dst.at[pl.ds(p * rows, rows)], sem)
            for p in range(WEIGHT_COPY_PARTS)]

    def start(self):
        for p, part in enumerate(self.parts):
            part.start(priority=p % 2)

    def wait(self):
        for part in self.parts:
            part.wait()


def _chunk_rows(start_ref, e, c, n_slots):
    del n_slots
    return pl.ds(pl.multiple_of(start_ref[e] + c * MOE_CHUNK, MOE_ALIGN), MOE_CHUNK)


def _gate_up_kernel(start_ref, cnt_ref, x_hbm, w_hbm, b_ref, act_hbm,
                    w_stage, w_sc, xbuf, obuf, sem_in, sem_out, sem_w):
    e = pl.program_id(0)
    n_slots = x_hbm.shape[0]
    n_chunks = (cnt_ref[e] + MOE_CHUNK - 1) // MOE_CHUNK
    d_ff = obuf.shape[2]

    def w_copy(idx):
        return _SplitCopy(w_hbm, idx, w_stage, sem_w)

    @pl.when(e == 0)
    def _():
        w_copy(0).start()

    has_weights = e < N_EXPERTS

    def await_weights():
        @pl.when(has_weights)
        def _():
            w_copy(e).wait()

    def refill():
        @pl.when(e + 1 < N_EXPERTS)
        def _():
            w_copy(e + 1).start()

    def in_copy(c, slot):
        return pltpu.make_async_copy(x_hbm.at[_chunk_rows(start_ref, e, c, n_slots)], xbuf.at[slot], sem_in.at[slot])

    def out_copy(c, slot):
        return pltpu.make_async_copy(obuf.at[slot], act_hbm.at[_chunk_rows(start_ref, e, c, n_slots)],
                                     sem_out.at[slot])

    def rows_times_weights(slot, cast_first):
        x = xbuf[slot].astype(BF16)
        cw = 256
        for c in range(d_ff // cw):
            gs = slice(c * cw, (c + 1) * cw)
            us = slice(d_ff + c * cw, d_ff + (c + 1) * cw)
            if cast_first:
                w_sc[:, gs] = w_stage[:, gs].astype(BF16)
                w_sc[:, us] = w_stage[:, us].astype(BF16)
            g = jnp.dot(x, w_sc[:, gs], preferred_element_type=F32) + b_ref[:, gs]
            u = jnp.dot(x, w_sc[:, us], preferred_element_type=F32) + b_ref[:, us]
            g = jnp.minimum(g, SWIGLU_LIMIT)
            u = jnp.clip(u, -SWIGLU_LIMIT, SWIGLU_LIMIT)
            obuf[slot, :, gs] = (g * jax.nn.sigmoid(SWIGLU_ALPHA * g) * (u + 1.0)).astype(obuf.dtype)

    def compute(slot, c):
        fresh = (c == 0) & has_weights

        @pl.when(fresh)
        def _():
            rows_times_weights(slot, True)
            refill()

        @pl.when(jnp.logical_not(fresh))
        def _():
            rows_times_weights(slot, False)

    @pl.when(n_chunks == 0)
    def _():
        await_weights()
        refill()

    _expert_rows_pipeline(n_chunks, await_weights, in_copy, out_copy, compute)


def _down_kernel(start_ref, cnt_ref, a_hbm, w_hbm, bd_ref, y_hbm, w_stage, wd_sc, abuf, obuf, sem_in, sem_out,
                 sem_w):
    e = pl.program_id(0)
    n_slots = y_hbm.shape[0]
    n_chunks = (cnt_ref[e] + MOE_CHUNK - 1) // MOE_CHUNK

    def w_copy(idx):
        return _SplitCopy(w_hbm, idx, w_stage, sem_w)

    @pl.when(e == 0)
    def _():
        w_copy(0).start()

    def cast_weights():
        @pl.when(e < N_EXPERTS)
        def _():
            w_copy(e).wait()
            wd_sc[...] = w_stage[...].astype(BF16)

            @pl.when(e + 1 < N_EXPERTS)
            def _():
                w_copy(e + 1).start()

    @pl.when(n_chunks == 0)
    def _():
        cast_weights()

    def in_copy(c, slot):
        return pltpu.make_async_copy(a_hbm.at[_chunk_rows(start_ref, e, c, n_slots)], abuf.at[slot],
                                     sem_in.at[slot])

    def out_copy(c, slot):
        return pltpu.make_async_copy(obuf.at[slot], y_hbm.at[_chunk_rows(start_ref, e, c, n_slots)],
                                     sem_out.at[slot])

    def compute(slot, c):
        del c
        cw = 512
        for c in range(obuf.shape[2] // cw):
            cs = slice(c * cw, (c + 1) * cw)
            obuf[slot, :, cs] = jnp.dot(abuf[slot], wd_sc[:, cs], preferred_element_type=F32) + bd_ref[:, cs]

    _expert_rows_pipeline(n_chunks, cast_weights, in_copy, out_copy, compute)


def _experts(xs, starts, counts, w_gate_up, b_gate_up, w_down, b_down):
    n_slots, d = xs.shape
    any_spec = pl.BlockSpec(memory_space=pl.ANY)
    dma_sems = pltpu.SemaphoreType.DMA((2,))
    bgu = b_gate_up.reshape(N_EXPERTS, 1, 2 * D_FF)
    n_groups = starts.shape[0]
    wi = lambda e: jnp.minimum(e, N_EXPERTS - 1)
    act = pl.pallas_call(
        _gate_up_kernel,
        grid_spec=pltpu.PrefetchScalarGridSpec(
            num_scalar_prefetch=2,
            grid=(n_groups,),
            in_specs=[
                any_spec,
                any_spec,
                pl.BlockSpec((None, 1, 2 * D_FF), lambda e, st, ct: (wi(e), 0, 0)),
            ],
            out_specs=any_spec,
            scratch_shapes=[pltpu.VMEM((d, 2 * D_FF), F32), pltpu.VMEM((d, 2 * D_FF), BF16),
                            pltpu.VMEM((2, MOE_CHUNK, d), xs.dtype), pltpu.VMEM((2, MOE_CHUNK, D_FF), BF16),
                            dma_sems, dma_sems, pltpu.SemaphoreType.DMA(())],
        ),
        out_shape=jax.ShapeDtypeStruct((n_slots, D_FF), BF16),
        compiler_params=_cparams(("arbitrary",), 60),
        name="moe_gate_up",
    )(starts, counts, xs, w_gate_up, bgu)
    bd = b_down.reshape(N_EXPERTS, 1, d)
    ys = pl.pallas_call(
        _down_kernel,
        grid_spec=pltpu.PrefetchScalarGridSpec(
            num_scalar_prefetch=2,
            grid=(n_groups,),
            in_specs=[
                any_spec,
                any_spec,
                pl.BlockSpec((None, 1, d), lambda e, st, ct: (wi(e), 0, 0)),
            ],
            out_specs=any_spec,
            scratch_shapes=[pltpu.VMEM((D_FF, d), F32), pltpu.VMEM((D_FF, d), BF16),
                            pltpu.VMEM((2, MOE_CHUNK, D_FF), BF16), pltpu.VMEM((2, MOE_CHUNK, d), F32),
                            dma_sems, dma_sems, pltpu.SemaphoreType.DMA(())],
        ),
        out_shape=jax.ShapeDtypeStruct((n_slots, d), F32),
        compiler_params=_cparams(("arbitrary",), 56),
        name="moe_down",
    )(starts, counts, act, w_down, bd)
    return ys


def _combine_kernel(dest_ref, dest_next_ref, gate_ref, h_ref, g_ref, beta_ref, ys_ref, o_ref, buf_ref, sem,
                    *, alpha):
    i = pl.program_id(0)
    last = pl.num_programs(0) - 1
    slot = lax.rem(i, 2)
    n = dest_ref.shape[0]

    def gather(dref, sl):
        def start(g, _):
            base = pl.multiple_of(g * ROW_TILE, ROW_TILE)
            for u in range(ROW_TILE * TOP_K):
                r, k = divmod(u, TOP_K)
                pltpu.make_async_copy(ys_ref.at[pl.ds(dref[g * (ROW_TILE * TOP_K) + u], 1)],
                                      buf_ref.at[sl, k, pl.ds(base + r, 1)],
                                      sem.at[sl]).start(priority=u % 2)
            return 0
        lax.fori_loop(0, n // (ROW_TILE * TOP_K), start, 0)

    for parity in range(2):
        @pl.when((i == 0) & (slot == parity))
        def _():
            gather(dest_ref, parity)

        @pl.when((i < last) & (slot == parity))
        def _():
            gather(dest_next_ref, 1 - parity)

    pltpu.make_async_copy(buf_ref.at[slot], buf_ref.at[slot], sem.at[slot]).wait()
    gates = gate_ref[...]
    y = gates[:, 0:1] * buf_ref[slot, 0]
    for k in range(1, TOP_K):
        y = y + gates[:, k:k + 1] * buf_ref[slot, k]
    o_ref[...] = _layer_norm(alpha * h_ref[...] + y, g_ref[...], beta_ref[...])


def _combine(ys, dest_flat, gates, h, ln_g, ln_b, alpha):
    s, d = h.shape
    tm = min(128, s)
    n_steps = s // tm
    return pl.pallas_call(
        functools.partial(_combine_kernel, alpha=alpha),
        grid=(n_steps,),
        in_specs=[
            pl.BlockSpec((tm * TOP_K,), lambda i: (i,), memory_space=pltpu.SMEM),
            pl.BlockSpec((tm * TOP_K,), lambda i: (jnp.minimum(i + 1, n_steps - 1),), memory_space=pltpu.SMEM),
            pl.BlockSpec((tm, TOP_K), lambda i: (i, 0)),
            pl.BlockSpec((tm, d), lambda i: (i, 0)),
            pl.BlockSpec((1, d), lambda i: (0, 0)),
            pl.BlockSpec((1, d), lambda i: (0, 0)),
            pl.BlockSpec(memory_space=pl.ANY),
        ],
        out_specs=pl.BlockSpec((tm, d), lambda i: (i, 0)),
        out_shape=jax.ShapeDtypeStruct((s, d), F32),
        scratch_shapes=[pltpu.VMEM((2, TOP_K, tm, d), F32), pltpu.SemaphoreType.DMA((2,))],
        compiler_params=_cparams(("arbitrary",), 32),
        name="moe_combine",
    )(dest_flat, dest_flat, gates, h, ln_g, ln_b, ys)


def _layer(x2d, w_in, b_forget, w_branch_a, w_branch_b, w_out, ln1_g, ln1_b, w_router, b_router,
           w_gate_up, b_gate_up, w_down, b_down, ln2_g, ln2_b, tables):
    s, d = x2d.shape
    alpha = (2.0 * DEPTH) ** 0.25
    scale = HEAD_DIM ** -0.5 * LOG2E

    w_bf, w_g = _prepare_w_in(w_in.T, scale)
    pbf, pg = _project(x2d, w_bf, tables, w_g)

    fox_tq = min(512, s)
    fox_tk = min(512, s)
    af_t = pg[:, COL_SMALL:COL_SMALL + FOX_HEADS].T
    c_t = _forget_cumsum(af_t, b_forget)
    c_tiles = c_t.reshape(FOX_HEADS, s // fox_tk, 1, fox_tk)
    a_out = _fox_attention(pbf, c_tiles, s, fox_tq, fox_tk, 4)

    b_out = _dsa_attention(pbf, pg, s, min(256, s), min(512, s))

    h, eidx, gates, pos, counts = _merge(
        a_out, b_out, pg, x2d, w_branch_a.astype(BF16), w_branch_b.astype(BF16), w_out.astype(BF16),
        ln1_g.reshape(1, d), ln1_b.reshape(1, d), w_router, b_router.reshape(1, N_EXPERTS), alpha)

    counts = counts.reshape(N_EXPERTS).astype(I32)
    aligned = (counts + MOE_ALIGN - 1) // MOE_ALIGN * MOE_ALIGN
    ends = jnp.cumsum(aligned).astype(I32)
    starts = ends - aligned
    bound = s * TOP_K + N_EXPERTS * MOE_ALIGN + 2 * MOE_CHUNK
    n_slots = (bound + MOE_CHUNK - 1) // MOE_CHUNK * MOE_CHUNK
    experts = jnp.arange(N_EXPERTS, dtype=I32)
    start_of = jnp.sum(jnp.where(eidx[..., None] == experts, starts, 0), axis=-1)
    dest = (start_of + pos).reshape(s * TOP_K).astype(I32)
    total = (ends[-1:] + MOE_CHUNK - 1) // MOE_CHUNK * MOE_CHUNK
    gap_len = aligned - counts + jnp.where(experts == N_EXPERTS - 1, total[0] - ends[-1], 0)
    group_starts = jnp.concatenate([starts, total])
    group_counts = jnp.concatenate([counts + jnp.where(experts == N_EXPERTS - 1, gap_len, 0), n_slots - total])

    xs = _dispatch(h, dest, starts + counts, gap_len, total, n_slots)
    ys = _experts(xs, group_starts, group_counts, w_gate_up, b_gate_up, w_down, b_down)
    return _combine(ys, dest, gates, h, ln2_g.reshape(1, d), ln2_b.reshape(1, d), alpha)


def kernel(x, w_in, b_forget, w_branch_a, w_branch_b, w_out, ln1_g, ln1_b, w_router, b_router,
           w_gate_up, b_gate_up, w_down, b_down, ln2_g, ln2_b):
    bsz, s, d = x.shape
    tables = _rope_tables(s)
    outs = []
    for bi in range(bsz):
        xb = x[bi]
        for l in range(DEPTH):
            xb = _layer(xb, w_in[l], b_forget[l], w_branch_a[l], w_branch_b[l], w_out[l], ln1_g[l], ln1_b[l],
                        w_router[l], b_router[l], w_gate_up[l], b_gate_up[l], w_down[l], b_down[l],
                        ln2_g[l], ln2_b[l], tables)
        outs.append(xb)
    return outs[0][None] if bsz == 1 else jnp.stack(outs)
```

```python
import functools

import numpy as np
import jax
import jax.numpy as jnp
from jax import lax
from jax.experimental import pallas as pl
from jax.experimental.pallas import tpu as pltpu

F32 = jnp.float32
BF16 = jnp.bfloat16
I32 = jnp.int32

D_MODEL = 2048
DEPTH = 1
CHUNK = 64
HEAD_DIM = 128
FOX_HEADS = 8
DSA_HEADS = 8
DSA_KV_HEADS = 2
DSA_GROUP = DSA_HEADS // DSA_KV_HEADS
IDX_HEADS = 16
IDX_DIM = 64
TOPK_MAX = 256
ROPE_THETA = 500000.0
ROT_FRACTION_DEN = 4
MIX_A = FOX_HEADS * HEAD_DIM
MIX_B = DSA_HEADS * HEAD_DIM
N_EXPERTS = 32
TOP_K = 4
D_FF = D_MODEL
SWIGLU_ALPHA = 1.702
SWIGLU_LIMIT = 7.0
LN_EPS = 1e-5

MIB = 1024 * 1024
NEG_BIG = -1e30
LOG2E = 1.4426950408889634
NT_DIMS = (((1,), (1,)), ((), ()))
INT_MIN = -(2 ** 31)

COL_AQ = 0
COL_AK = 1024
COL_AV = 2048
COL_BQ = 3072
COL_BK = 4096
COL_BV = 4352
COL_IQ = 4608
COL_IK = 5632
N_PBF = 5888
PROJ_TN = 256
ROPE_TABLE_W = 128
ROPE_HEAD_TILES = (12, 13, 14, 15, 16)
ROPE_IDX_TILES = (18, 19, 20, 21, 22)
COL_GA = 0
COL_GB = 2048
COL_SMALL = 4096
N_PG = 4224
PG_TN = 384

MOE_CHUNK = 128
MOE_ALIGN = 16
ROW_TILE = 8


def _cparams(dims, vmem_mib):
    return pltpu.CompilerParams(dimension_semantics=dims, vmem_limit_bytes=vmem_mib * MIB)


IN_SIZES = (MIX_A, MIX_A, MIX_A, FOX_HEADS, MIX_B, DSA_KV_HEADS * HEAD_DIM, DSA_KV_HEADS * HEAD_DIM,
            IDX_HEADS * IDX_DIM, IDX_DIM, IDX_HEADS, D_MODEL, D_MODEL)
IN_OFF = dict(zip(("aq", "ak", "av", "af", "bq", "bk", "bv", "iq", "ik", "iw", "ga", "gb", "end"),
                  np.concatenate([[0], np.cumsum(IN_SIZES)]).tolist()))


def _prep_kernel(w_ref, wbf_ref, wg_ref, *, scale):
    o = IN_OFF
    width = w_ref.shape[1]

    def rows(a, b, mult=None):
        v = w_ref[a:b, :]
        return v if mult is None else v * mult

    wbf_ref[COL_AQ:COL_AK, :] = rows(o["aq"], o["ak"], scale).astype(BF16)
    wbf_ref[COL_AK:COL_BQ, :] = rows(o["ak"], o["af"]).astype(BF16)
    wbf_ref[COL_BQ:COL_BK, :] = rows(o["bq"], o["bk"], scale).astype(BF16)
    wbf_ref[COL_BK:COL_IK, :] = rows(o["bk"], o["ik"]).astype(BF16)
    wbf_ref[COL_IK:N_PBF, :] = jnp.concatenate(
        [rows(o["ik"], o["iw"]), jnp.zeros((N_PBF - COL_IK - IDX_DIM, width), F32)], axis=0).astype(BF16)
    wg_ref[COL_GA:COL_SMALL, :] = rows(o["ga"], o["end"]).astype(BF16)
    wg_ref[COL_SMALL:N_PG, :] = jnp.concatenate(
        [rows(o["af"], o["bq"]), rows(o["iw"], o["ga"]),
         jnp.zeros((N_PG - COL_SMALL - FOX_HEADS - IDX_HEADS, width), F32)], axis=0).astype(BF16)


def _prepare_w_in(w_in_t, scale):
    n_in, d = w_in_t.shape
    tc = 256
    return pl.pallas_call(
        functools.partial(_prep_kernel, scale=scale),
        grid=(d // tc,),
        in_specs=[pl.BlockSpec((n_in, tc), lambda i: (0, i))],
        out_specs=[pl.BlockSpec((N_PBF, tc), lambda i: (0, i)), pl.BlockSpec((N_PG, tc), lambda i: (0, i))],
        out_shape=[jax.ShapeDtypeStruct((N_PBF, d), BF16), jax.ShapeDtypeStruct((N_PG, d), BF16)],
        compiler_params=_cparams(("arbitrary",), 48),
        name="w_in_prep",
    )(w_in_t)


def _tile_in(j, tiles):
    cond = j == tiles[0]
    for t in tiles[1:]:
        cond = cond | (j == t)
    return cond


def _proj_rope_kernel(x_ref, w_ref, tab_ref, o_ref, xb_ref):
    j = pl.program_id(1)

    @pl.when(j == 0)
    def _():
        xb_ref[...] = x_ref[...].astype(BF16)

    acc = lax.dot_general(xb_ref[...], w_ref[...], NT_DIMS, preferred_element_type=F32)
    tn = acc.shape[1]
    is_head = _tile_in(j, ROPE_HEAD_TILES)
    is_idx = _tile_in(j, ROPE_IDX_TILES)

    def rope(shift):
        reps = tn // ROPE_TABLE_W
        c = jnp.tile(tab_ref[0, 0], (1, reps))
        s_prev = jnp.tile(tab_ref[0, 1], (1, reps))
        s_next = jnp.tile(tab_ref[0, 2], (1, reps))
        out = acc * c + pltpu.roll(acc, shift, 1) * s_prev + pltpu.roll(acc, tn - shift, 1) * s_next
        o_ref[...] = out.astype(o_ref.dtype)

    @pl.when(is_head)
    def _():
        rope(HEAD_DIM // ROT_FRACTION_DEN // 2)

    @pl.when(is_idx)
    def _():
        rope(IDX_DIM // ROT_FRACTION_DEN // 2)

    @pl.when(jnp.logical_not(is_head | is_idx))
    def _():
        o_ref[...] = acc.astype(o_ref.dtype)


def _proj_plain_kernel(x_ref, w_ref, o_ref, xb_ref):
    j = pl.program_id(1)

    @pl.when(j == 0)
    def _():
        xb_ref[...] = x_ref[...].astype(BF16)

    o_ref[...] = lax.dot_general(xb_ref[...], w_ref[...], NT_DIMS, preferred_element_type=F32).astype(o_ref.dtype)


def _rope_group(j):
    return jnp.where(_tile_in(j, ROPE_IDX_TILES), 1, 0)


def _project(x2d, w_bf, tables, w_g):
    s, d = x2d.shape
    tm = min(1024, s)
    pbf = pl.pallas_call(
        _proj_rope_kernel,
        grid=(s // tm, N_PBF // PROJ_TN),
        in_specs=[
            pl.BlockSpec((tm, d), lambda i, j: (i, 0)),
            pl.BlockSpec((PROJ_TN, d), lambda i, j: (j, 0)),
            pl.BlockSpec((1, 3, tm, ROPE_TABLE_W), lambda i, j: (_rope_group(j), 0, i, 0)),
        ],
        out_specs=pl.BlockSpec((tm, PROJ_TN), lambda i, j: (i, j)),
        out_shape=jax.ShapeDtypeStruct((s, N_PBF), BF16),
        scratch_shapes=[pltpu.VMEM((tm, d), BF16)],
        compiler_params=_cparams(("arbitrary", "arbitrary"), 48),
        name="proj_bf16",
    )(x2d, w_bf, tables)
    pg = pl.pallas_call(
        _proj_plain_kernel,
        grid=(s // tm, N_PG // PG_TN),
        in_specs=[
            pl.BlockSpec((tm, d), lambda i, j: (i, 0)),
            pl.BlockSpec((PG_TN, d), lambda i, j: (j, 0)),
        ],
        out_specs=pl.BlockSpec((tm, PG_TN), lambda i, j: (i, j)),
        out_shape=jax.ShapeDtypeStruct((s, N_PG), F32),
        scratch_shapes=[pltpu.VMEM((tm, d), BF16)],
        compiler_params=_cparams(("arbitrary", "arbitrary"), 48),
        name="proj_f32",
    )(x2d, w_g)
    return pbf, pg


def _rope_tables(s):
    pos = jnp.arange(s, dtype=F32)

    def one(period):
        rot = period // ROT_FRACTION_DEN
        half = rot // 2
        inv = jnp.power(ROPE_THETA, -jnp.arange(0, rot, 2, dtype=F32) / rot)
        ang = pos[:, None] * inv[None, :]
        cos, sin = jnp.cos(ang), jnp.sin(ang)
        zero = jnp.zeros((s, period - rot), F32)
        c = jnp.concatenate([cos, cos, jnp.ones((s, period - rot), F32)], axis=1)
        s_prev = jnp.concatenate([jnp.zeros((s, half), F32), sin, zero], axis=1)
        s_next = jnp.concatenate([-sin, jnp.zeros((s, half), F32), zero], axis=1)
        reps = ROPE_TABLE_W // period
        return jnp.stack([jnp.tile(c, (1, reps)), jnp.tile(s_prev, (1, reps)), jnp.tile(s_next, (1, reps))])

    return jnp.stack([one(HEAD_DIM), one(IDX_DIM)])


def _cumsum_kernel(af_ref, bf_ref, c_ref, carry_ref):
    i = pl.program_id(0)

    @pl.when(i == 0)
    def _():
        carry_ref[...] = jnp.zeros_like(carry_ref)

    z = af_ref[...] + bf_ref[...]
    logf = jnp.minimum(z, 0.0) - jnp.log1p(jnp.exp(-jnp.abs(z)))
    t = z.shape[1]
    row = lax.broadcasted_iota(I32, (t, t), 0)
    col = lax.broadcasted_iota(I32, (t, t), 1)
    upper = (row <= col).astype(F32)
    c = jnp.dot(logf, upper, preferred_element_type=F32, precision=lax.Precision.HIGHEST) + carry_ref[...]
    c_ref[...] = c * LOG2E
    carry_ref[...] = c[:, t - 1:t]


def _forget_cumsum(af_t, b_forget):
    h, s = af_t.shape
    t = min(512, s)
    return pl.pallas_call(
        _cumsum_kernel,
        grid=(s // t,),
        in_specs=[pl.BlockSpec((h, t), lambda i: (0, i)), pl.BlockSpec((h, 1), lambda i: (0, 0))],
        out_specs=pl.BlockSpec((h, t), lambda i: (0, i)),
        out_shape=jax.ShapeDtypeStruct((h, s), F32),
        scratch_shapes=[pltpu.VMEM((h, 1), F32)],
        compiler_params=_cparams(("arbitrary",), 32),
        name="forget_cumsum",
    )(af_t, b_forget.reshape(h, 1).astype(F32))


def _softmax_init(m_ref, l_ref, acc_ref):
    m_ref[...] = jnp.full(m_ref.shape, NEG_BIG, F32)
    l_ref[...] = jnp.zeros(l_ref.shape, F32)
    acc_ref[...] = jnp.zeros(acc_ref.shape, F32)


def _softmax_tile(s, v, m_ref, l_ref, acc_ref, i):
    reps = s.shape[1] // 128
    m_old = m_ref[i]
    m_new = jnp.maximum(m_old, jnp.max(s, axis=1, keepdims=True))
    p = jnp.exp2(s - jnp.tile(m_new, (1, reps)))
    alpha = jnp.exp2(m_old - m_new)
    psum = p[:, :128]
    for c in range(1, reps):
        psum = psum + p[:, c * 128:(c + 1) * 128]
    l_ref[i] = alpha * l_ref[i] + psum
    acc_ref[i] = alpha * acc_ref[i] + jnp.dot(p.astype(BF16), v, preferred_element_type=F32)
    m_ref[i] = m_new


def _softmax_result(l_ref, acc_ref, i):
    return acc_ref[i] / jnp.sum(l_ref[i], axis=1, keepdims=True)


def _fox_kernel(q_ref, k_ref, v_ref, c_ref, o_ref, m_ref, l_ref, acc_ref, *, tq, tk, nh):
    qi = pl.program_id(1)
    _softmax_init(m_ref, l_ref, acc_ref)

    def tile(kt, masked):
        start = pl.multiple_of(kt * tk, tk)
        if masked:
            row = qi * tq + lax.broadcasted_iota(I32, (tq, tk), 0)
            col = kt * tk + lax.broadcasted_iota(I32, (tq, tk), 1)
            causal = col <= row
        for h in range(nh):
            hs = slice(h * HEAD_DIM, (h + 1) * HEAD_DIM)
            k = k_ref[pl.ds(start, tk), hs]
            v = v_ref[pl.ds(start, tk), hs]
            s = lax.dot_general(q_ref[:, hs], k, (((1,), (1,)), ((), ())), preferred_element_type=F32)
            s = s - c_ref[h, kt]
            if masked:
                s = jnp.where(causal, s, NEG_BIG)
            _softmax_tile(s, v, m_ref, l_ref, acc_ref, h)

    n_full = (qi * tq) // tk

    def body(kt, _):
        tile(kt, False)
        return 0

    lax.fori_loop(0, n_full, body, 0)
    for t in range((tq + tk - 1) // tk):
        tile(n_full + t, True)
    for h in range(nh):
        o_ref[:, h * HEAD_DIM:(h + 1) * HEAD_DIM] = _softmax_result(l_ref, acc_ref, h).astype(o_ref.dtype)


def _fox_attention(pbf, c_tiles, s, tq, tk, nh):
    cb = nh * HEAD_DIM
    return pl.pallas_call(
        functools.partial(_fox_kernel, tq=tq, tk=tk, nh=nh),
        grid=(FOX_HEADS // nh, s // tq),
        in_specs=[
            pl.BlockSpec((tq, cb), lambda g, i: (i, COL_AQ // cb + g)),
            pl.BlockSpec((s, cb), lambda g, i: (0, COL_AK // cb + g)),
            pl.BlockSpec((s, cb), lambda g, i: (0, COL_AV // cb + g)),
            pl.BlockSpec((nh, s // tk, 1, tk), lambda g, i: (g, 0, 0, 0)),
        ],
        out_specs=pl.BlockSpec((tq, cb), lambda g, i: (i, g)),
        out_shape=jax.ShapeDtypeStruct((s, MIX_A), BF16),
        scratch_shapes=[
            pltpu.VMEM((nh, tq, 128), F32),
            pltpu.VMEM((nh, tq, 128), F32),
            pltpu.VMEM((nh, tq, HEAD_DIM), F32),
        ],
        compiler_params=_cparams(("arbitrary", "arbitrary"), 56),
        name="fox_attention",
    )(pbf, pbf, pbf, c_tiles)


def _dsa_kernel(bq_ref, iq_lo_ref, iq_hi_ref, iw_ref, ik_ref, bk_ref, bv_ref, o_ref,
                key_ref, qg_ref, m_ref, l_ref, acc_ref, *, tq, tk, n_sel):
    b = pl.program_id(0)
    n_tiles = (b * tq) // tk + 1
    row_g = b * tq + lax.broadcasted_iota(I32, (tq, tk), 0)
    adm_end = (row_g // CHUNK + 1) * CHUNK
    col_l = lax.broadcasted_iota(I32, (tq, tk), 1)

    idx_scale = (IDX_HEADS ** -0.5) * (IDX_DIM ** -0.5)
    iw = iw_ref[...][:, 8:8 + IDX_HEADS] * idx_scale
    iq = jnp.concatenate([iq_lo_ref[...], iq_hi_ref[...]], axis=1)

    def score_tile(kt, _):
        start = pl.multiple_of(kt * tk, tk)
        ik = ik_ref[pl.ds(start, tk), :][:, :IDX_DIM]
        acc = jnp.zeros((tq, tk), F32)
        for h in range(IDX_HEADS):
            a = iq[:, h * IDX_DIM:(h + 1) * IDX_DIM]
            rel = lax.dot_general(a, ik, (((1,), (1,)), ((), ())), preferred_element_type=F32)
            acc = acc + jnp.maximum(rel, 0.0) * iw[:, h:h + 1]
        key_ref[kt] = jnp.where(kt * tk + col_l < adm_end, acc, -jnp.inf)
        return 0

    lax.fori_loop(0, n_tiles, score_tile, 0)

    def float_of(code_u):
        code = code_u ^ INT_MIN
        return pltpu.bitcast(jnp.where(code >= 0, code, code ^ 0x7FFFFFFF), F32)

    def count_ge(cand):
        def body(kt, part):
            ge = jnp.where(key_ref[kt] >= cand, 1, 0)
            for c in range(tk // 128):
                part = part + ge[:, c * 128:(c + 1) * 128]
            return part
        part = lax.fori_loop(0, n_tiles, body, jnp.zeros((tq, 128), I32))
        return jnp.sum(part, axis=1, keepdims=True)

    def search_cond(carry):
        i, _, _, pending = carry
        return (i < 32) & (pending > 0)

    def refine(i, t_u, hit):
        cand_u = t_u | lax.shift_left(jnp.int32(1), 31 - i)
        cnt = count_ge(float_of(cand_u))
        return jnp.where(cnt >= n_sel, cand_u, t_u), jnp.where(cnt == n_sel, 1, hit)

    def search_step(carry):
        i, t_u, hit, _ = carry
        t_u, hit = refine(i, t_u, hit)
        return i + 1, t_u, hit, jnp.sum(1 - hit)

    zeros = jnp.zeros((tq, 1), I32)
    lead = 12
    t_u, hit = lax.fori_loop(0, lead, lambda i, c: refine(i, *c), (zeros, zeros))
    _, t_u, _, _ = lax.while_loop(search_cond, search_step, (jnp.int32(lead), t_u, hit, jnp.sum(1 - hit)))
    thr = jnp.where(t_u == 0, jnp.finfo(F32).min, float_of(t_u))

    for j in range(DSA_KV_HEADS):
        for g in range(DSA_GROUP):
            hd = j * DSA_GROUP + g
            qg_ref[j, g * tq:(g + 1) * tq, :] = bq_ref[:, hd * HEAD_DIM:(hd + 1) * HEAD_DIM]
    _softmax_init(m_ref, l_ref, acc_ref)

    def attn_tile(kt, _):
        start = pl.multiple_of(kt * tk, tk)
        sel = key_ref[kt] >= thr
        for j in range(DSA_KV_HEADS):
            hs = slice(j * HEAD_DIM, (j + 1) * HEAD_DIM)
            k = bk_ref[pl.ds(start, tk), hs]
            v = bv_ref[pl.ds(start, tk), hs]
            s = lax.dot_general(qg_ref[j], k, (((1,), (1,)), ((), ())), preferred_element_type=F32)
            s = jnp.where(sel[None], s.reshape(DSA_GROUP, tq, tk), NEG_BIG).reshape(DSA_GROUP * tq, tk)
            _softmax_tile(s, v, m_ref, l_ref, acc_ref, j)
        return 0

    lax.fori_loop(0, n_tiles, attn_tile, 0)

    for j in range(DSA_KV_HEADS):
        o = _softmax_result(l_ref, acc_ref, j)
        for g in range(DSA_GROUP):
            hd = j * DSA_GROUP + g
            o_ref[:, hd * HEAD_DIM:(hd + 1) * HEAD_DIM] = o[g * tq:(g + 1) * tq].astype(o_ref.dtype)


def _dsa_attention(pbf, pg, s, tq, tk):
    n_sel = min(TOPK_MAX, s // 4)
    kvw = DSA_KV_HEADS * HEAD_DIM
    rows = DSA_GROUP * tq
    iqw = IDX_HEADS * IDX_DIM // 2
    return pl.pallas_call(
        functools.partial(_dsa_kernel, tq=tq, tk=tk, n_sel=n_sel),
        grid=(s // tq,),
        in_specs=[
            pl.BlockSpec((tq, MIX_B), lambda b: (b, COL_BQ // MIX_B)),
            pl.BlockSpec((tq, iqw), lambda b: (b, COL_IQ // iqw)),
            pl.BlockSpec((tq, iqw), lambda b: (b, COL_IQ // iqw + 1)),
            pl.BlockSpec((tq, 128), lambda b: (b, COL_SMALL // 128)),
            pl.BlockSpec((s, 128), lambda b: (0, COL_IK // 128)),
            pl.BlockSpec((s, kvw), lambda b: (0, COL_BK // kvw)),
            pl.BlockSpec((s, kvw), lambda b: (0, COL_BV // kvw)),
        ],
        out_specs=pl.BlockSpec((tq, MIX_B), lambda b: (b, 0)),
        out_shape=jax.ShapeDtypeStruct((s, MIX_B), BF16),
        scratch_shapes=[
            pltpu.VMEM((s // tk, tq, tk), F32),
            pltpu.VMEM((DSA_KV_HEADS, rows, HEAD_DIM), BF16),
            pltpu.VMEM((DSA_KV_HEADS, rows, 128), F32),
            pltpu.VMEM((DSA_KV_HEADS, rows, 128), F32),
            pltpu.VMEM((DSA_KV_HEADS, rows, HEAD_DIM), F32),
        ],
        compiler_params=_cparams(("arbitrary",), 56),
        name="dsa_attention",
    )(pbf, pbf, pbf, pg, pbf, pbf, pbf)


def _layer_norm(z, g, b):
    mu = jnp.mean(z, axis=-1, keepdims=True)
    zc = z - mu
    var = jnp.mean(zc * zc, axis=-1, keepdims=True)
    return zc * lax.rsqrt(var + LN_EPS) * g + b


def _merge_kernel(a_ref, b_ref, ga_ref, gb_ref, x_ref, wa_ref, wb_ref, wo_ref, g_ref, beta_ref,
                  wr_ref, br_ref, h_ref, eidx_ref, gate_ref, pos_ref, cnt_ref, carry_ref, *, alpha):
    i = pl.program_id(0)

    @pl.when(i == 0)
    def _():
        carry_ref[...] = jnp.zeros_like(carry_ref)

    ma = jnp.dot(a_ref[...], wa_ref[...], preferred_element_type=F32)
    mb = jnp.dot(b_ref[...], wb_ref[...], preferred_element_type=F32)
    merged = jax.nn.sigmoid(ga_ref[...]) * ma + jax.nn.sigmoid(gb_ref[...]) * mb
    y = jnp.dot(merged.astype(BF16), wo_ref[...], preferred_element_type=F32)
    h = _layer_norm(alpha * x_ref[...] + y, g_ref[...], beta_ref[...])
    h_ref[...] = h

    wr = wr_ref[...]
    h_hi = h.astype(BF16)
    h_lo = (h - h_hi.astype(F32)).astype(BF16)
    wr_hi = wr.astype(BF16)
    wr_lo = (wr - wr_hi.astype(F32)).astype(BF16)
    logits = (jnp.dot(h_hi, wr_hi, preferred_element_type=F32) + jnp.dot(h_lo, wr_hi, preferred_element_type=F32)
              + jnp.dot(h_hi, wr_lo, preferred_element_type=F32) + br_ref[...])
    tm, ne = logits.shape
    lane = lax.broadcasted_iota(I32, (tm, ne), 1)
    lane_k = lax.broadcasted_iota(I32, (tm, TOP_K), 1)
    work = logits
    vals, sels = [], []
    eidx = jnp.zeros((tm, TOP_K), I32)
    onehot = jnp.zeros((tm, ne), F32)
    for k in range(TOP_K):
        mv = jnp.max(work, axis=1, keepdims=True)
        idx = jnp.min(jnp.where(work == mv, lane, ne), axis=1, keepdims=True)
        sel = lane == idx
        vals.append(mv)
        sels.append(sel)
        eidx = jnp.where(lane_k == k, idx, eidx)
        onehot = onehot + jnp.where(sel, 1.0, 0.0)
        work = jnp.where(sel, -jnp.inf, work)
    exps = [jnp.exp(v - vals[0]) for v in vals]
    denom = exps[0] + exps[1] + exps[2] + exps[3]
    gates = jnp.zeros((tm, TOP_K), F32)
    for k in range(TOP_K):
        gates = jnp.where(lane_k == k, exps[k] / denom, gates)

    r_i = lax.broadcasted_iota(I32, (tm, tm), 0)
    c_i = lax.broadcasted_iota(I32, (tm, tm), 1)
    lower = jnp.where(c_i < r_i, 1.0, 0.0).astype(BF16)
    rank = carry_ref[...] + jnp.dot(lower, onehot.astype(BF16), preferred_element_type=F32)
    pos = jnp.zeros((tm, TOP_K), I32)
    for k in range(TOP_K):
        pk = jnp.sum(jnp.where(sels[k], rank, 0.0), axis=1, keepdims=True).astype(I32)
        pos = jnp.where(lane_k == k, pk, pos)
    carry_ref[...] = carry_ref[...] + jnp.sum(onehot, axis=0, keepdims=True)

    eidx_ref[...] = eidx
    gate_ref[...] = gates
    pos_ref[...] = pos
    cnt_ref[...] = carry_ref[...].astype(I32)


def _merge(a_out, b_out, pg, x2d, wa, wb, wo, ln_g, ln_b, w_router, b_router, alpha):
    s, d = x2d.shape
    tm = min(256, s)
    full = lambda shape: pl.BlockSpec(shape, lambda i: (0,) * len(shape))
    return pl.pallas_call(
        functools.partial(_merge_kernel, alpha=alpha),
        grid=(s // tm,),
        in_specs=[
            pl.BlockSpec((tm, MIX_A), lambda i: (i, 0)),
            pl.BlockSpec((tm, MIX_B), lambda i: (i, 0)),
            pl.BlockSpec((tm, d), lambda i: (i, COL_GA // d)),
            pl.BlockSpec((tm, d), lambda i: (i, COL_GB // d)),
            pl.BlockSpec((tm, d), lambda i: (i, 0)),
            full((MIX_A, d)), full((MIX_B, d)), full((d, d)),
            full((1, d)), full((1, d)), full((d, N_EXPERTS)), full((1, N_EXPERTS)),
        ],
        out_specs=[
            pl.BlockSpec((tm, d), lambda i: (i, 0)),
            pl.BlockSpec((tm, TOP_K), lambda i: (i, 0)),
            pl.BlockSpec((tm, TOP_K), lambda i: (i, 0)),
            pl.BlockSpec((tm, TOP_K), lambda i: (i, 0)),
            full((1, N_EXPERTS)),
        ],
        out_shape=[
            jax.ShapeDtypeStruct((s, d), F32),
            jax.ShapeDtypeStruct((s, TOP_K), I32),
            jax.ShapeDtypeStruct((s, TOP_K), F32),
            jax.ShapeDtypeStruct((s, TOP_K), I32),
            jax.ShapeDtypeStruct((1, N_EXPERTS), I32),
        ],
        scratch_shapes=[pltpu.VMEM((1, N_EXPERTS), F32)],
        compiler_params=_cparams(("arbitrary",), 56),
        name="merge_ln_router",
    )(a_out, b_out, pg, pg, x2d, wa, wb, wo, ln_g, ln_b, w_router, b_router)


def _dispatch_kernel(dest_ref, gap_start_ref, gap_len_ref, total_ref, h_ref, xs_ref, stage_ref, zero_ref,
                     sem, zsem, *, max_slack):
    i = pl.program_id(0)
    last = pl.num_programs(0) - 1
    slot = lax.rem(i, 2)
    n = dest_ref.shape[0]

    @pl.when(i == 0)
    def _():
        zero_ref[...] = jnp.zeros_like(zero_ref)

        def gap_copy(e, k):
            return pltpu.make_async_copy(zero_ref.at[pl.ds(0, 1)], xs_ref.at[pl.ds(gap_start_ref[e] + k, 1)], zsem)

        n_slots = xs_ref.shape[0]

        def slack_copy(k):
            row = pl.multiple_of(total_ref[0] + k * MOE_CHUNK, MOE_CHUNK)
            return pltpu.make_async_copy(zero_ref, xs_ref.at[pl.ds(row, MOE_CHUNK)], zsem)

        def slack(action):
            for k in range(max_slack):
                @pl.when(total_ref[0] + k * MOE_CHUNK < n_slots)
                def _():
                    action(slack_copy(k))

        def issue(e, _):
            lax.fori_loop(0, gap_len_ref[e], lambda k, c: (gap_copy(e, k).start(), c)[1], 0)
            return 0

        def settle(e, _):
            lax.fori_loop(0, gap_len_ref[e], lambda k, c: (gap_copy(e, k).wait(), c)[1], 0)
            return 0

        lax.fori_loop(0, N_EXPERTS, issue, 0)
        slack(lambda cp: cp.start())
        lax.fori_loop(0, N_EXPERTS, settle, 0)
        slack(lambda cp: cp.wait())

    def drain(sl):
        pltpu.make_async_copy(xs_ref.at[pl.ds(0, n)], xs_ref.at[pl.ds(0, n)], sem.at[sl]).wait()

    @pl.when(i >= 2)
    def _():
        drain(slot)

    stage_ref[slot] = h_ref[...]

    def scatter(sl):
        def start(g, _):
            base = pl.multiple_of(g * ROW_TILE, ROW_TILE)
            for u in range(ROW_TILE * TOP_K):
                r = u // TOP_K
                pltpu.make_async_copy(stage_ref.at[sl, pl.ds(base + r, 1)],
                                      xs_ref.at[pl.ds(dest_ref[g * (ROW_TILE * TOP_K) + u], 1)],
                                      sem.at[sl]).start(priority=u % 2)
            return 0
        lax.fori_loop(0, n // (ROW_TILE * TOP_K), start, 0)

    for parity in range(2):
        @pl.when(slot == parity)
        def _():
            scatter(parity)

    @pl.when(i == last)
    def _():
        drain(slot)

        @pl.when(i >= 1)
        def _():
            drain(1 - slot)


def _dispatch(h, dest_flat, gap_start, gap_len, total, n_slots):
    s, d = h.shape
    tm = min(128, s)
    smem = pl.BlockSpec(memory_space=pltpu.SMEM)
    max_slack = (n_slots - s * TOP_K + MOE_CHUNK - 1) // MOE_CHUNK
    return pl.pallas_call(
        functools.partial(_dispatch_kernel, max_slack=max_slack),
        grid=(s // tm,),
        in_specs=[
            pl.BlockSpec((tm * TOP_K,), lambda i: (i,), memory_space=pltpu.SMEM),
            smem, smem, smem,
            pl.BlockSpec((tm, d), lambda i: (i, 0)),
        ],
        out_specs=pl.BlockSpec(memory_space=pl.ANY),
        out_shape=jax.ShapeDtypeStruct((n_slots, d), h.dtype),
        scratch_shapes=[pltpu.VMEM((2, tm, d), h.dtype), pltpu.VMEM((MOE_CHUNK, d), h.dtype),
                        pltpu.SemaphoreType.DMA((2,)), pltpu.SemaphoreType.DMA(())],
        compiler_params=_cparams(("arbitrary",), 32),
        name="moe_dispatch",
    )(dest_flat, gap_start, gap_len, total, h)


ROW_DMA_PRIORITY = 1


def _expert_rows_pipeline(n_chunks, prologue, in_copy, out_copy, compute):
    @pl.when(n_chunks > 0)
    def _():
        in_copy(0, 0).start(priority=ROW_DMA_PRIORITY)
        prologue()

        def body(c, _):
            slot = lax.rem(c, 2)
            in_copy(c, slot).wait()

            @pl.when(c + 1 < n_chunks)
            def _():
                in_copy(c + 1, 1 - slot).start(priority=ROW_DMA_PRIORITY)

            @pl.when(c >= 2)
            def _():
                out_copy(c - 2, slot).wait()

            compute(slot, c)
            out_copy(c, slot).start(priority=ROW_DMA_PRIORITY)
            return 0

        lax.fori_loop(0, n_chunks, body, 0)

        @pl.when(n_chunks >= 2)
        def _():
            out_copy(n_chunks - 2, lax.rem(n_chunks, 2)).wait()

        out_copy(n_chunks - 1, lax.rem(n_chunks - 1, 2)).wait()


WEIGHT_COPY_PARTS = 8


class _SplitCopy:
    def __init__(self, src_hbm, idx, dst, sem):
        rows = dst.shape[0] // WEIGHT_COPY_PARTS
        self.parts = [
            pltpu.make_async_copy(src_hbm.at[idx, pl.ds(p * rows, rows)], dst.at[pl.ds(p * rows, rows)], sem)
            for p in range(WEIGHT_COPY_PARTS)]

    def start(self):
        for p, part in enumerate(self.parts):
            part.start(priority=p % 2)

    def wait(self):
        for part in self.parts:
            part.wait()


def _chunk_rows(start_ref, e, c, n_slots):
    del n_slots
    return pl.ds(pl.multiple_of(start_ref[e] + c * MOE_CHUNK, MOE_ALIGN), MOE_CHUNK)


def _gate_up_kernel(start_ref, cnt_ref, x_hbm, w_hbm, b_ref, act_hbm,
                    w_stage, w_sc, xbuf, obuf, sem_in, sem_out, sem_w):
    e = pl.program_id(0)
    n_slots = x_hbm.shape[0]
    n_chunks = (cnt_ref[e] + MOE_CHUNK - 1) // MOE_CHUNK
    d_ff = obuf.shape[2]

    def w_copy(idx):
        return _SplitCopy(w_hbm, idx, w_stage, sem_w)

    @pl.when(e == 0)
    def _():
        w_copy(0).start()

    has_weights = e < N_EXPERTS

    def await_weights():
        @pl.when(has_weights)
        def _():
            w_copy(e).wait()

    def refill():
        @pl.when(e + 1 < N_EXPERTS)
        def _():
            w_copy(e + 1).start()

    def in_copy(c, slot):
        return pltpu.make_async_copy(x_hbm.at[_chunk_rows(start_ref, e, c, n_slots)], xbuf.at[slot], sem_in.at[slot])

    def out_copy(c, slot):
        return pltpu.make_async_copy(obuf.at[slot], act_hbm.at[_chunk_rows(start_ref, e, c, n_slots)],
                                     sem_out.at[slot])

    def rows_times_weights(slot, cast_first):
        x = xbuf[slot].astype(BF16)
        cw = 256
        for c in range(d_ff // cw):
            gs = slice(c * cw, (c + 1) * cw)
            us = slice(d_ff + c * cw, d_ff + (c + 1) * cw)
            if cast_first:
                w_sc[:, gs] = w_stage[:, gs].astype(BF16)
                w_sc[:, us] = w_stage[:, us].astype(BF16)
            g = jnp.dot(x, w_sc[:, gs], preferred_element_type=F32) + b_ref[:, gs]
            u = jnp.dot(x, w_sc[:, us], preferred_element_type=F32) + b_ref[:, us]
            g = jnp.minimum(g, SWIGLU_LIMIT)
            u = jnp.clip(u, -SWIGLU_LIMIT, SWIGLU_LIMIT)
            obuf[slot, :, gs] = (g * jax.nn.sigmoid(SWIGLU_ALPHA * g) * (u + 1.0)).astype(obuf.dtype)

    def compute(slot, c):
        fresh = (c == 0) & has_weights

        @pl.when(fresh)
        def _():
            rows_times_weights(slot, True)
            refill()

        @pl.when(jnp.logical_not(fresh))
        def _():
            rows_times_weights(slot, False)

    @pl.when(n_chunks == 0)
    def _():
        await_weights()
        refill()

    _expert_rows_pipeline(n_chunks, await_weights, in_copy, out_copy, compute)


def _down_kernel(start_ref, cnt_ref, a_hbm, w_hbm, bd_ref, y_hbm, w_stage, wd_sc, abuf, obuf, sem_in, sem_out,
                 sem_w):
    e = pl.program_id(0)
    n_slots = y_hbm.shape[0]
    n_chunks = (cnt_ref[e] + MOE_CHUNK - 1) // MOE_CHUNK

    def w_copy(idx):
        return _SplitCopy(w_hbm, idx, w_stage, sem_w)

    @pl.when(e == 0)
    def _():
        w_copy(0).start()

    def cast_weights():
        @pl.when(e < N_EXPERTS)
        def _():
            w_copy(e).wait()
            wd_sc[...] = w_stage[...].astype(BF16)

            @pl.when(e + 1 < N_EXPERTS)
            def _():
                w_copy(e + 1).start()

    @pl.when(n_chunks == 0)
    def _():
        cast_weights()

    def in_copy(c, slot):
        return pltpu.make_async_copy(a_hbm.at[_chunk_rows(start_ref, e, c, n_slots)], abuf.at[slot],
                                     sem_in.at[slot])

    def out_copy(c, slot):
        return pltpu.make_async_copy(obuf.at[slot], y_hbm.at[_chunk_rows(start_ref, e, c, n_slots)],
                                     sem_out.at[slot])

    def compute(slot, c):
        del c
        cw = 512
        for c in range(obuf.shape[2] // cw):
            cs = slice(c * cw, (c + 1) * cw)
            obuf[slot, :, cs] = jnp.dot(abuf[slot], wd_sc[:, cs], preferred_element_type=F32) + bd_ref[:, cs]

    _expert_rows_pipeline(n_chunks, cast_weights, in_copy, out_copy, compute)


def _experts(xs, starts, counts, w_gate_up, b_gate_up, w_down, b_down):
    n_slots, d = xs.shape
    any_spec = pl.BlockSpec(memory_space=pl.ANY)
    dma_sems = pltpu.SemaphoreType.DMA((2,))
    bgu = b_gate_up.reshape(N_EXPERTS, 1, 2 * D_FF)
    n_groups = starts.shape[0]
    wi = lambda e: jnp.minimum(e, N_EXPERTS - 1)
    act = pl.pallas_call(
        _gate_up_kernel,
        grid_spec=pltpu.PrefetchScalarGridSpec(
            num_scalar_prefetch=2,
            grid=(n_groups,),
            in_specs=[
                any_spec,
                any_spec,
                pl.BlockSpec((None, 1, 2 * D_FF), lambda e, st, ct: (wi(e), 0, 0)),
            ],
            out_specs=any_spec,
            scratch_shapes=[pltpu.VMEM((d, 2 * D_FF), F32), pltpu.VMEM((d, 2 * D_FF), BF16),
                            pltpu.VMEM((2, MOE_CHUNK, d), xs.dtype), pltpu.VMEM((2, MOE_CHUNK, D_FF), BF16),
                            dma_sems, dma_sems, pltpu.SemaphoreType.DMA(())],
        ),
        out_shape=jax.ShapeDtypeStruct((n_slots, D_FF), BF16),
        compiler_params=_cparams(("arbitrary",), 60),
        name="moe_gate_up",
    )(starts, counts, xs, w_gate_up, bgu)
    bd = b_down.reshape(N_EXPERTS, 1, d)
    ys = pl.pallas_call(
        _down_kernel,
        grid_spec=pltpu.PrefetchScalarGridSpec(
            num_scalar_prefetch=2,
            grid=(n_groups,),
            in_specs=[
                any_spec,
                any_spec,
                pl.BlockSpec((None, 1, d), lambda e, st, ct: (wi(e), 0, 0)),
            ],
            out_specs=any_spec,
            scratch_shapes=[pltpu.VMEM((D_FF, d), F32), pltpu.VMEM((D_FF, d), BF16),
                            pltpu.VMEM((2, MOE_CHUNK, D_FF), BF16), pltpu.VMEM((2, MOE_CHUNK, d), F32),
                            dma_sems, dma_sems, pltpu.SemaphoreType.DMA(())],
        ),
        out_shape=jax.ShapeDtypeStruct((n_slots, d), F32),
        compiler_params=_cparams(("arbitrary",), 56),
        name="moe_down",
    )(starts, counts, act, w_down, bd)
    return ys


def _combine_kernel(dest_ref, dest_next_ref, gate_ref, h_ref, g_ref, beta_ref, ys_ref, o_ref, buf_ref, sem,
                    *, alpha):
    i = pl.program_id(0)
    last = pl.num_programs(0) - 1
    slot = lax.rem(i, 2)
    n = dest_ref.shape[0]

    def gather(dref, sl):
        def start(g, _):
            base = pl.multiple_of(g * ROW_TILE, ROW_TILE)
            for u in range(ROW_TILE * TOP_K):
                r, k = divmod(u, TOP_K)
                pltpu.make_async_copy(ys_ref.at[pl.ds(dref[g * (ROW_TILE * TOP_K) + u], 1)],
                                      buf_ref.at[sl, k, pl.ds(base + r, 1)],
                                      sem.at[sl]).start(priority=u % 2)
            return 0
        lax.fori_loop(0, n // (ROW_TILE * TOP_K), start, 0)

    for parity in range(2):
        @pl.when((i == 0) & (slot == parity))
        def _():
            gather(dest_ref, parity)

        @pl.when((i < last) & (slot == parity))
        def _():
            gather(dest_next_ref, 1 - parity)

    pltpu.make_async_copy(buf_ref.at[slot], buf_ref.at[slot], sem.at[slot]).wait()
    gates = gate_ref[...]
    y = gates[:, 0:1] * buf_ref[slot, 0]
    for k in range(1, TOP_K):
        y = y + gates[:, k:k + 1] * buf_ref[slot, k]
    o_ref[...] = _layer_norm(alpha * h_ref[...] + y, g_ref[...], beta_ref[...])


def _combine(ys, dest_flat, gates, h, ln_g, ln_b, alpha):
    s, d = h.shape
    tm = min(128, s)
    n_steps = s // tm
    return pl.pallas_call(
        functools.partial(_combine_kernel, alpha=alpha),
        grid=(n_steps,),
        in_specs=[
            pl.BlockSpec((tm * TOP_K,), lambda i: (i,), memory_space=pltpu.SMEM),
            pl.BlockSpec((tm * TOP_K,), lambda i: (jnp.minimum(i + 1, n_steps - 1),), memory_space=pltpu.SMEM),
            pl.BlockSpec((tm, TOP_K), lambda i: (i, 0)),
            pl.BlockSpec((tm, d), lambda i: (i, 0)),
            pl.BlockSpec((1, d), lambda i: (0, 0)),
            pl.BlockSpec((1, d), lambda i: (0, 0)),
            pl.BlockSpec(memory_space=pl.ANY),
        ],
        out_specs=pl.BlockSpec((tm, d), lambda i: (i, 0)),
        out_shape=jax.ShapeDtypeStruct((s, d), F32),
        scratch_shapes=[pltpu.VMEM((2, TOP_K, tm, d), F32), pltpu.SemaphoreType.DMA((2,))],
        compiler_params=_cparams(("arbitrary",), 32),
        name="moe_combine",
    )(dest_flat, dest_flat, gates, h, ln_g, ln_b, ys)


def _layer(x2d, w_in, b_forget, w_branch_a, w_branch_b, w_out, ln1_g, ln1_b, w_router, b_router,
           w_gate_up, b_gate_up, w_down, b_down, ln2_g, ln2_b, tables):
    s, d = x2d.shape
    alpha = (2.0 * DEPTH) ** 0.25
    scale = HEAD_DIM ** -0.5 * LOG2E

    w_bf, w_g = _prepare_w_in(w_in.T, scale)
    pbf, pg = _project(x2d, w_bf, tables, w_g)

    fox_tq = min(512, s)
    fox_tk = min(512, s)
    af_t = pg[:, COL_SMALL:COL_SMALL + FOX_HEADS].T
    c_t = _forget_cumsum(af_t, b_forget)
    c_tiles = c_t.reshape(FOX_HEADS, s // fox_tk, 1, fox_tk)
    a_out = _fox_attention(pbf, c_tiles, s, fox_tq, fox_tk, 4)

    b_out = _dsa_attention(pbf, pg, s, min(256, s), min(512, s))

    h, eidx, gates, pos, counts = _merge(
        a_out, b_out, pg, x2d, w_branch_a.astype(BF16), w_branch_b.astype(BF16), w_out.astype(BF16),
        ln1_g.reshape(1, d), ln1_b.reshape(1, d), w_router, b_router.reshape(1, N_EXPERTS), alpha)

    counts = counts.reshape(N_EXPERTS).astype(I32)
    aligned = (counts + MOE_ALIGN - 1) // MOE_ALIGN * MOE_ALIGN
    ends = jnp.cumsum(aligned).astype(I32)
    starts = ends - aligned
    bound = s * TOP_K + N_EXPERTS * MOE_ALIGN + 2 * MOE_CHUNK
    n_slots = (bound + MOE_CHUNK - 1) // MOE_CHUNK * MOE_CHUNK
    experts = jnp.arange(N_EXPERTS, dtype=I32)
    start_of = jnp.sum(jnp.where(eidx[..., None] == experts, starts, 0), axis=-1)
    dest = (start_of + pos).reshape(s * TOP_K).astype(I32)
    total = (ends[-1:] + MOE_CHUNK - 1) // MOE_CHUNK * MOE_CHUNK
    gap_len = aligned - counts + jnp.where(experts == N_EXPERTS - 1, total[0] - ends[-1], 0)
    group_starts = jnp.concatenate([starts, total])
    group_counts = jnp.concatenate([counts + jnp.where(experts == N_EXPERTS - 1, gap_len, 0), n_slots - total])

    xs = _dispatch(h, dest, starts + counts, gap_len, total, n_slots)
    ys = _experts(xs, group_starts, group_counts, w_gate_up, b_gate_up, w_down, b_down)
    return _combine(ys, dest, gates, h, ln2_g.reshape(1, d), ln2_b.reshape(1, d), alpha)


def kernel(x, w_in, b_forget, w_branch_a, w_branch_b, w_out, ln1_g, ln1_b, w_router, b_router,
           w_gate_up, b_gate_up, w_down, b_down, ln2_g, ln2_b):
    bsz, s, d = x.shape
    tables = _rope_tables(s)
    outs = []
    for bi in range(bsz):
        xb = x[bi]
        for l in range(DEPTH):
            xb = _layer(xb, w_in[l], b_forget[l], w_branch_a[l], w_branch_b[l], w_out[l], ln1_g[l], ln1_b[l],
                        w_router[l], b_router[l], w_gate_up[l], b_gate_up[l], w_down[l], b_down[l],
                        ln2_g[l], ln2_b[l], tables)
        outs.append(xb)
    return outs[0][None] if bsz == 1 else jnp.stack(outs)
```

```python
import functools

import numpy as np
import jax
import jax.numpy as jnp
from jax import lax
from jax.experimental import pallas as pl
from jax.experimental.pallas import tpu as pltpu

F32 = jnp.float32
BF16 = jnp.bfloat16
I32 = jnp.int32

D_MODEL = 2048
DEPTH = 1
CHUNK = 64
HEAD_DIM = 128
FOX_HEADS = 8
DSA_HEADS = 8
DSA_KV_HEADS = 2
DSA_GROUP = DSA_HEADS // DSA_KV_HEADS
IDX_HEADS = 16
IDX_DIM = 64
TOPK_MAX = 256
ROPE_THETA = 500000.0
ROT_FRACTION_DEN = 4
MIX_A = FOX_HEADS * HEAD_DIM
MIX_B = DSA_HEADS * HEAD_DIM
N_EXPERTS = 32
TOP_K = 4
D_FF = D_MODEL
SWIGLU_ALPHA = 1.702
SWIGLU_LIMIT = 7.0
LN_EPS = 1e-5

MIB = 1024 * 1024
NEG_BIG = -1e30
LOG2E = 1.4426950408889634
NT_DIMS = (((1,), (1,)), ((), ()))
INT_MIN = -(2 ** 31)

COL_AQ = 0
COL_AK = 1024
COL_AV = 2048
COL_BQ = 3072
COL_BK = 4096
COL_BV = 4352
COL_IQ = 4608
COL_IK = 5632
N_PBF = 5888
PROJ_TN = 256
ROPE_TABLE_W = 128
ROPE_HEAD_TILES = (12, 13, 14, 15, 16)
ROPE_IDX_TILES = (18, 19, 20, 21, 22)
COL_GA = 0
COL_GB = 2048
COL_SMALL = 4096
N_PG = 4224
PG_TN = 384

MOE_CHUNK = 256
MOE_ALIGN = 16
ROW_TILE = 8


def _cparams(dims, vmem_mib):
    return pltpu.CompilerParams(dimension_semantics=dims, vmem_limit_bytes=vmem_mib * MIB)


IN_SIZES = (MIX_A, MIX_A, MIX_A, FOX_HEADS, MIX_B, DSA_KV_HEADS * HEAD_DIM, DSA_KV_HEADS * HEAD_DIM,
            IDX_HEADS * IDX_DIM, IDX_DIM, IDX_HEADS, D_MODEL, D_MODEL)
IN_OFF = dict(zip(("aq", "ak", "av", "af", "bq", "bk", "bv", "iq", "ik", "iw", "ga", "gb", "end"),
                  np.concatenate([[0], np.cumsum(IN_SIZES)]).tolist()))


def _prep_kernel(w_ref, wbf_ref, wg_ref, *, scale):
    o = IN_OFF
    width = w_ref.shape[1]

    def rows(a, b, mult=None):
        v = w_ref[a:b, :]
        return v if mult is None else v * mult

    wbf_ref[COL_AQ:COL_AK, :] = rows(o["aq"], o["ak"], scale).astype(BF16)
    wbf_ref[COL_AK:COL_BQ, :] = rows(o["ak"], o["af"]).astype(BF16)
    wbf_ref[COL_BQ:COL_BK, :] = rows(o["bq"], o["bk"], scale).astype(BF16)
    wbf_ref[COL_BK:COL_IK, :] = rows(o["bk"], o["ik"]).astype(BF16)
    wbf_ref[COL_IK:N_PBF, :] = jnp.concatenate(
        [rows(o["ik"], o["iw"]), jnp.zeros((N_PBF - COL_IK - IDX_DIM, width), F32)], axis=0).astype(BF16)
    wg_ref[COL_GA:COL_SMALL, :] = rows(o["ga"], o["end"]).astype(BF16)
    wg_ref[COL_SMALL:N_PG, :] = jnp.concatenate(
        [rows(o["af"], o["bq"]), rows(o["iw"], o["ga"]),
         jnp.zeros((N_PG - COL_SMALL - FOX_HEADS - IDX_HEADS, width), F32)], axis=0).astype(BF16)


def _prepare_w_in(w_in_t, scale):
    n_in, d = w_in_t.shape
    tc = 256
    return pl.pallas_call(
        functools.partial(_prep_kernel, scale=scale),
        grid=(d // tc,),
        in_specs=[pl.BlockSpec((n_in, tc), lambda i: (0, i))],
        out_specs=[pl.BlockSpec((N_PBF, tc), lambda i: (0, i)), pl.BlockSpec((N_PG, tc), lambda i: (0, i))],
        out_shape=[jax.ShapeDtypeStruct((N_PBF, d), BF16), jax.ShapeDtypeStruct((N_PG, d), BF16)],
        compiler_params=_cparams(("arbitrary",), 48),
        name="w_in_prep",
    )(w_in_t)


def _tile_in(j, tiles):
    cond = j == tiles[0]
    for t in tiles[1:]:
        cond = cond | (j == t)
    return cond


def _proj_rope_kernel(x_ref, w_ref, tab_ref, o_ref, xb_ref):
    j = pl.program_id(1)

    @pl.when(j == 0)
    def _():
        xb_ref[...] = x_ref[...].astype(BF16)

    acc = lax.dot_general(xb_ref[...], w_ref[...], NT_DIMS, preferred_element_type=F32)
    tn = acc.shape[1]
    is_head = _tile_in(j, ROPE_HEAD_TILES)
    is_idx = _tile_in(j, ROPE_IDX_TILES)

    def rope(shift):
        reps = tn // ROPE_TABLE_W
        c = jnp.tile(tab_ref[0, 0], (1, reps))
        s_prev = jnp.tile(tab_ref[0, 1], (1, reps))
        s_next = jnp.tile(tab_ref[0, 2], (1, reps))
        out = acc * c + pltpu.roll(acc, shift, 1) * s_prev + pltpu.roll(acc, tn - shift, 1) * s_next
        o_ref[...] = out.astype(o_ref.dtype)

    @pl.when(is_head)
    def _():
        rope(HEAD_DIM // ROT_FRACTION_DEN // 2)

    @pl.when(is_idx)
    def _():
        rope(IDX_DIM // ROT_FRACTION_DEN // 2)

    @pl.when(jnp.logical_not(is_head | is_idx))
    def _():
        o_ref[...] = acc.astype(o_ref.dtype)


def _proj_plain_kernel(x_ref, w_ref, o_ref, xb_ref):
    j = pl.program_id(1)

    @pl.when(j == 0)
    def _():
        xb_ref[...] = x_ref[...].astype(BF16)

    o_ref[...] = lax.dot_general(xb_ref[...], w_ref[...], NT_DIMS, preferred_element_type=F32).astype(o_ref.dtype)


def _rope_group(j):
    return jnp.where(_tile_in(j, ROPE_IDX_TILES), 1, 0)


def _project(x2d, w_bf, tables, w_g):
    s, d = x2d.shape
    tm = min(1024, s)
    pbf = pl.pallas_call(
        _proj_rope_kernel,
        grid=(s // tm, N_PBF // PROJ_TN),
        in_specs=[
            pl.BlockSpec((tm, d), lambda i, j: (i, 0)),
            pl.BlockSpec((PROJ_TN, d), lambda i, j: (j, 0)),
            pl.BlockSpec((1, 3, tm, ROPE_TABLE_W), lambda i, j: (_rope_group(j), 0, i, 0)),
        ],
        out_specs=pl.BlockSpec((tm, PROJ_TN), lambda i, j: (i, j)),
        out_shape=jax.ShapeDtypeStruct((s, N_PBF), BF16),
        scratch_shapes=[pltpu.VMEM((tm, d), BF16)],
        compiler_params=_cparams(("arbitrary", "arbitrary"), 48),
        name="proj_bf16",
    )(x2d, w_bf, tables)
    pg = pl.pallas_call(
        _proj_plain_kernel,
        grid=(s // tm, N_PG // PG_TN),
        in_specs=[
            pl.BlockSpec((tm, d), lambda i, j: (i, 0)),
            pl.BlockSpec((PG_TN, d), lambda i, j: (j, 0)),
        ],
        out_specs=pl.BlockSpec((tm, PG_TN), lambda i, j: (i, j)),
        out_shape=jax.ShapeDtypeStruct((s, N_PG), F32),
        scratch_shapes=[pltpu.VMEM((tm, d), BF16)],
        compiler_params=_cparams(("arbitrary", "arbitrary"), 48),
        name="proj_f32",
    )(x2d, w_g)
    return pbf, pg


def _rope_tables(s):
    pos = jnp.arange(s, dtype=F32)

    def one(period):
        rot = period // ROT_FRACTION_DEN
        half = rot // 2
        inv = jnp.power(ROPE_THETA, -jnp.arange(0, rot, 2, dtype=F32) / rot)
        ang = pos[:, None] * inv[None, :]
        cos, sin = jnp.cos(ang), jnp.sin(ang)
        zero = jnp.zeros((s, period - rot), F32)
        c = jnp.concatenate([cos, cos, jnp.ones((s, period - rot), F32)], axis=1)
        s_prev = jnp.concatenate([jnp.zeros((s, half), F32), sin, zero], axis=1)
        s_next = jnp.concatenate([-sin, jnp.zeros((s, half), F32), zero], axis=1)
        reps = ROPE_TABLE_W // period
        return jnp.stack([jnp.tile(c, (1, reps)), jnp.tile(s_prev, (1, reps)), jnp.tile(s_next, (1, reps))])

    return jnp.stack([one(HEAD_DIM), one(IDX_DIM)])


def _cumsum_kernel(af_ref, bf_ref, c_ref, carry_ref):
    i = pl.program_id(0)

    @pl.when(i == 0)
    def _():
        carry_ref[...] = jnp.zeros_like(carry_ref)

    z = af_ref[...] + bf_ref[...]
    logf = jnp.minimum(z, 0.0) - jnp.log1p(jnp.exp(-jnp.abs(z)))
    t = z.shape[1]
    row = lax.broadcasted_iota(I32, (t, t), 0)
    col = lax.broadcasted_iota(I32, (t, t), 1)
    upper = (row <= col).astype(F32)
    c = jnp.dot(logf, upper, preferred_element_type=F32, precision=lax.Precision.HIGHEST) + carry_ref[...]
    c_ref[...] = c * LOG2E
    carry_ref[...] = c[:, t - 1:t]


def _forget_cumsum(af_t, b_forget):
    h, s = af_t.shape
    t = min(512, s)
    return pl.pallas_call(
        _cumsum_kernel,
        grid=(s // t,),
        in_specs=[pl.BlockSpec((h, t), lambda i: (0, i)), pl.BlockSpec((h, 1), lambda i: (0, 0))],
        out_specs=pl.BlockSpec((h, t), lambda i: (0, i)),
        out_shape=jax.ShapeDtypeStruct((h, s), F32),
        scratch_shapes=[pltpu.VMEM((h, 1), F32)],
        compiler_params=_cparams(("arbitrary",), 32),
        name="forget_cumsum",
    )(af_t, b_forget.reshape(h, 1).astype(F32))


def _softmax_init(m_ref, l_ref, acc_ref):
    m_ref[...] = jnp.full(m_ref.shape, NEG_BIG, F32)
    l_ref[...] = jnp.zeros(l_ref.shape, F32)
    acc_ref[...] = jnp.zeros(acc_ref.shape, F32)


def _softmax_tile(s, v, m_ref, l_ref, acc_ref, i):
    reps = s.shape[1] // 128
    m_old = m_ref[i]
    m_new = jnp.maximum(m_old, jnp.max(s, axis=1, keepdims=True))
    p = jnp.exp2(s - jnp.tile(m_new, (1, reps)))
    alpha = jnp.exp2(m_old - m_new)
    psum = p[:, :128]
    for c in range(1, reps):
        psum = psum + p[:, c * 128:(c + 1) * 128]
    l_ref[i] = alpha * l_ref[i] + psum
    acc_ref[i] = alpha * acc_ref[i] + jnp.dot(p.astype(BF16), v, preferred_element_type=F32)
    m_ref[i] = m_new


def _softmax_result(l_ref, acc_ref, i):
    return acc_ref[i] / jnp.sum(l_ref[i], axis=1, keepdims=True)


def _fox_kernel(q_ref, k_ref, v_ref, c_ref, o_ref, m_ref, l_ref, acc_ref, *, tq, tk, nh):
    qi = pl.program_id(1)
    _softmax_init(m_ref, l_ref, acc_ref)

    def tile(kt, masked):
        start = pl.multiple_of(kt * tk, tk)
        if masked:
            row = qi * tq + lax.broadcasted_iota(I32, (tq, tk), 0)
            col = kt * tk + lax.broadcasted_iota(I32, (tq, tk), 1)
            causal = col <= row
        for h in range(nh):
            hs = slice(h * HEAD_DIM, (h + 1) * HEAD_DIM)
            k = k_ref[pl.ds(start, tk), hs]
            v = v_ref[pl.ds(start, tk), hs]
            s = lax.dot_general(q_ref[:, hs], k, (((1,), (1,)), ((), ())), preferred_element_type=F32)
            s = s - c_ref[h, kt]
            if masked:
                s = jnp.where(causal, s, NEG_BIG)
            _softmax_tile(s, v, m_ref, l_ref, acc_ref, h)

    n_full = (qi * tq) // tk

    def body(kt, _):
        tile(kt, False)
        return 0

    lax.fori_loop(0, n_full, body, 0)
    for t in range((tq + tk - 1) // tk):
        tile(n_full + t, True)
    for h in range(nh):
        o_ref[:, h * HEAD_DIM:(h + 1) * HEAD_DIM] = _softmax_result(l_ref, acc_ref, h).astype(o_ref.dtype)


def _fox_attention(pbf, c_tiles, s, tq, tk, nh):
    cb = nh * HEAD_DIM
    return pl.pallas_call(
        functools.partial(_fox_kernel, tq=tq, tk=tk, nh=nh),
        grid=(FOX_HEADS // nh, s // tq),
        in_specs=[
            pl.BlockSpec((tq, cb), lambda g, i: (i, COL_AQ // cb + g)),
            pl.BlockSpec((s, cb), lambda g, i: (0, COL_AK // cb + g)),
            pl.BlockSpec((s, cb), lambda g, i: (0, COL_AV // cb + g)),
            pl.BlockSpec((nh, s // tk, 1, tk), lambda g, i: (g, 0, 0, 0)),
        ],
        out_specs=pl.BlockSpec((tq, cb), lambda g, i: (i, g)),
        out_shape=jax.ShapeDtypeStruct((s, MIX_A), BF16),
        scratch_shapes=[
            pltpu.VMEM((nh, tq, 128), F32),
            pltpu.VMEM((nh, tq, 128), F32),
            pltpu.VMEM((nh, tq, HEAD_DIM), F32),
        ],
        compiler_params=_cparams(("arbitrary", "arbitrary"), 56),
        name="fox_attention",
    )(pbf, pbf, pbf, c_tiles)


def _dsa_kernel(bq_ref, iq_lo_ref, iq_hi_ref, iw_ref, ik_ref, bk_ref, bv_ref, o_ref,
                key_ref, qg_ref, m_ref, l_ref, acc_ref, *, tq, tk, n_sel):
    b = pl.program_id(0)
    n_tiles = (b * tq) // tk + 1
    row_g = b * tq + lax.broadcasted_iota(I32, (tq, tk), 0)
    adm_end = (row_g // CHUNK + 1) * CHUNK
    col_l = lax.broadcasted_iota(I32, (tq, tk), 1)

    idx_scale = (IDX_HEADS ** -0.5) * (IDX_DIM ** -0.5)
    iw = iw_ref[...][:, 8:8 + IDX_HEADS] * idx_scale
    iq = jnp.concatenate([iq_lo_ref[...], iq_hi_ref[...]], axis=1)

    def score_tile(kt, _):
        start = pl.multiple_of(kt * tk, tk)
        ik = ik_ref[pl.ds(start, tk), :][:, :IDX_DIM]
        acc = jnp.zeros((tq, tk), F32)
        for h in range(IDX_HEADS):
            a = iq[:, h * IDX_DIM:(h + 1) * IDX_DIM]
            rel = lax.dot_general(a, ik, (((1,), (1,)), ((), ())), preferred_element_type=F32)
            acc = acc + jnp.maximum(rel, 0.0) * iw[:, h:h + 1]
        key_ref[kt] = jnp.where(kt * tk + col_l < adm_end, acc, -jnp.inf)
        return 0

    lax.fori_loop(0, n_tiles, score_tile, 0)

    def float_of(code_u):
        code = code_u ^ INT_MIN
        return pltpu.bitcast(jnp.where(code >= 0, code, code ^ 0x7FFFFFFF), F32)

    def count_ge(cand):
        def body(kt, part):
            ge = jnp.where(key_ref[kt] >= cand, 1, 0)
            for c in range(tk // 128):
                part = part + ge[:, c * 128:(c + 1) * 128]
            return part
        part = lax.fori_loop(0, n_tiles, body, jnp.zeros((tq, 128), I32))
        return jnp.sum(part, axis=1, keepdims=True)

    def search_cond(carry):
        i, _, _, pending = carry
        return (i < 32) & (pending > 0)

    def refine(i, t_u, hit):
        cand_u = t_u | lax.shift_left(jnp.int32(1), 31 - i)
        cnt = count_ge(float_of(cand_u))
        return jnp.where(cnt >= n_sel, cand_u, t_u), jnp.where(cnt == n_sel, 1, hit)

    def search_step(carry):
        i, t_u, hit, _ = carry
        t_u, hit = refine(i, t_u, hit)
        return i + 1, t_u, hit, jnp.sum(1 - hit)

    zeros = jnp.zeros((tq, 1), I32)
    lead = 12
    t_u, hit = lax.fori_loop(0, lead, lambda i, c: refine(i, *c), (zeros, zeros))
    _, t_u, _, _ = lax.while_loop(search_cond, search_step, (jnp.int32(lead), t_u, hit, jnp.sum(1 - hit)))
    thr = jnp.where(t_u == 0, jnp.finfo(F32).min, float_of(t_u))

    for j in range(DSA_KV_HEADS):
        for g in range(DSA_GROUP):
            hd = j * DSA_GROUP + g
            qg_ref[j, g * tq:(g + 1) * tq, :] = bq_ref[:, hd * HEAD_DIM:(hd + 1) * HEAD_DIM]
    _softmax_init(m_ref, l_ref, acc_ref)

    def attn_tile(kt, _):
        start = pl.multiple_of(kt * tk, tk)
        sel = key_ref[kt] >= thr
        for j in range(DSA_KV_HEADS):
            hs = slice(j * HEAD_DIM, (j + 1) * HEAD_DIM)
            k = bk_ref[pl.ds(start, tk), hs]
            v = bv_ref[pl.ds(start, tk), hs]
            s = lax.dot_general(qg_ref[j], k, (((1,), (1,)), ((), ())), preferred_element_type=F32)
            s = jnp.where(sel[None], s.reshape(DSA_GROUP, tq, tk), NEG_BIG).reshape(DSA_GROUP * tq, tk)
            _softmax_tile(s, v, m_ref, l_ref, acc_ref, j)
        return 0

    lax.fori_loop(0, n_tiles, attn_tile, 0)

    for j in range(DSA_KV_HEADS):
        o = _softmax_result(l_ref, acc_ref, j)
        for g in range(DSA_GROUP):
            hd = j * DSA_GROUP + g
            o_ref[:, hd * HEAD_DIM:(hd + 1) * HEAD_DIM] = o[g * tq:(g + 1) * tq].astype(o_ref.dtype)


def _dsa_attention(pbf, pg, s, tq, tk):
    n_sel = min(TOPK_MAX, s // 4)
    kvw = DSA_KV_HEADS * HEAD_DIM
    rows = DSA_GROUP * tq
    iqw = IDX_HEADS * IDX_DIM // 2
    return pl.pallas_call(
        functools.partial(_dsa_kernel, tq=tq, tk=tk, n_sel=n_sel),
        grid=(s // tq,),
        in_specs=[
            pl.BlockSpec((tq, MIX_B), lambda b: (b, COL_BQ // MIX_B)),
            pl.BlockSpec((tq, iqw), lambda b: (b, COL_IQ // iqw)),
            pl.BlockSpec((tq, iqw), lambda b: (b, COL_IQ // iqw + 1)),
            pl.BlockSpec((tq, 128), lambda b: (b, COL_SMALL // 128)),
            pl.BlockSpec((s, 128), lambda b: (0, COL_IK // 128)),
            pl.BlockSpec((s, kvw), lambda b: (0, COL_BK // kvw)),
            pl.BlockSpec((s, kvw), lambda b: (0, COL_BV // kvw)),
        ],
        out_specs=pl.BlockSpec((tq, MIX_B), lambda b: (b, 0)),
        out_shape=jax.ShapeDtypeStruct((s, MIX_B), BF16),
        scratch_shapes=[
            pltpu.VMEM((s // tk, tq, tk), F32),
            pltpu.VMEM((DSA_KV_HEADS, rows, HEAD_DIM), BF16),
            pltpu.VMEM((DSA_KV_HEADS, rows, 128), F32),
            pltpu.VMEM((DSA_KV_HEADS, rows, 128), F32),
            pltpu.VMEM((DSA_KV_HEADS, rows, HEAD_DIM), F32),
        ],
        compiler_params=_cparams(("arbitrary",), 56),
        name="dsa_attention",
    )(pbf, pbf, pbf, pg, pbf, pbf, pbf)


def _layer_norm(z, g, b):
    mu = jnp.mean(z, axis=-1, keepdims=True)
    zc = z - mu
    var = jnp.mean(zc * zc, axis=-1, keepdims=True)
    return zc * lax.rsqrt(var + LN_EPS) * g + b


def _merge_kernel(a_ref, b_ref, ga_ref, gb_ref, x_ref, wa_ref, wb_ref, wo_ref, g_ref, beta_ref,
                  wr_ref, br_ref, h_ref, eidx_ref, gate_ref, pos_ref, cnt_ref, carry_ref, *, alpha):
    i = pl.program_id(0)

    @pl.when(i == 0)
    def _():
        carry_ref[...] = jnp.zeros_like(carry_ref)

    ma = jnp.dot(a_ref[...], wa_ref[...], preferred_element_type=F32)
    mb = jnp.dot(b_ref[...], wb_ref[...], preferred_element_type=F32)
    merged = jax.nn.sigmoid(ga_ref[...]) * ma + jax.nn.sigmoid(gb_ref[...]) * mb
    y = jnp.dot(merged.astype(BF16), wo_ref[...], preferred_element_type=F32)
    h = _layer_norm(alpha * x_ref[...] + y, g_ref[...], beta_ref[...])
    h_ref[...] = h

    wr = wr_ref[...]
    h_hi = h.astype(BF16)
    h_lo = (h - h_hi.astype(F32)).astype(BF16)
    wr_hi = wr.astype(BF16)
    wr_lo = (wr - wr_hi.astype(F32)).astype(BF16)
    logits = (jnp.dot(h_hi, wr_hi, preferred_element_type=F32) + jnp.dot(h_lo, wr_hi, preferred_element_type=F32)
              + jnp.dot(h_hi, wr_lo, preferred_element_type=F32) + br_ref[...])
    tm, ne = logits.shape
    lane = lax.broadcasted_iota(I32, (tm, ne), 1)
    lane_k = lax.broadcasted_iota(I32, (tm, TOP_K), 1)
    work = logits
    vals, sels = [], []
    eidx = jnp.zeros((tm, TOP_K), I32)
    onehot = jnp.zeros((tm, ne), F32)
    for k in range(TOP_K):
        mv = jnp.max(work, axis=1, keepdims=True)
        idx = jnp.min(jnp.where(work == mv, lane, ne), axis=1, keepdims=True)
        sel = lane == idx
        vals.append(mv)
        sels.append(sel)
        eidx = jnp.where(lane_k == k, idx, eidx)
        onehot = onehot + jnp.where(sel, 1.0, 0.0)
        work = jnp.where(sel, -jnp.inf, work)
    exps = [jnp.exp(v - vals[0]) for v in vals]
    denom = exps[0] + exps[1] + exps[2] + exps[3]
    gates = jnp.zeros((tm, TOP_K), F32)
    for k in range(TOP_K):
        gates = jnp.where(lane_k == k, exps[k] / denom, gates)

    r_i = lax.broadcasted_iota(I32, (tm, tm), 0)
    c_i = lax.broadcasted_iota(I32, (tm, tm), 1)
    lower = jnp.where(c_i < r_i, 1.0, 0.0).astype(BF16)
    rank = carry_ref[...] + jnp.dot(lower, onehot.astype(BF16), preferred_element_type=F32)
    pos = jnp.zeros((tm, TOP_K), I32)
    for k in range(TOP_K):
        pk = jnp.sum(jnp.where(sels[k], rank, 0.0), axis=1, keepdims=True).astype(I32)
        pos = jnp.where(lane_k == k, pk, pos)
    carry_ref[...] = carry_ref[...] + jnp.sum(onehot, axis=0, keepdims=True)

    eidx_ref[...] = eidx
    gate_ref[...] = gates
    pos_ref[...] = pos
    cnt_ref[...] = carry_ref[...].astype(I32)


def _merge(a_out, b_out, pg, x2d, wa, wb, wo, ln_g, ln_b, w_router, b_router, alpha):
    s, d = x2d.shape
    tm = min(256, s)
    full = lambda shape: pl.BlockSpec(shape, lambda i: (0,) * len(shape))
    return pl.pallas_call(
        functools.partial(_merge_kernel, alpha=alpha),
        grid=(s // tm,),
        in_specs=[
            pl.BlockSpec((tm, MIX_A), lambda i: (i, 0)),
            pl.BlockSpec((tm, MIX_B), lambda i: (i, 0)),
            pl.BlockSpec((tm, d), lambda i: (i, COL_GA // d)),
            pl.BlockSpec((tm, d), lambda i: (i, COL_GB // d)),
            pl.BlockSpec((tm, d), lambda i: (i, 0)),
            full((MIX_A, d)), full((MIX_B, d)), full((d, d)),
            full((1, d)), full((1, d)), full((d, N_EXPERTS)), full((1, N_EXPERTS)),
        ],
        out_specs=[
            pl.BlockSpec((tm, d), lambda i: (i, 0)),
            pl.BlockSpec((tm, TOP_K), lambda i: (i, 0)),
            pl.BlockSpec((tm, TOP_K), lambda i: (i, 0)),
            pl.BlockSpec((tm, TOP_K), lambda i: (i, 0)),
            full((1, N_EXPERTS)),
        ],
        out_shape=[
            jax.ShapeDtypeStruct((s, d), F32),
            jax.ShapeDtypeStruct((s, TOP_K), I32),
            jax.ShapeDtypeStruct((s, TOP_K), F32),
            jax.ShapeDtypeStruct((s, TOP_K), I32),
            jax.ShapeDtypeStruct((1, N_EXPERTS), I32),
        ],
        scratch_shapes=[pltpu.VMEM((1, N_EXPERTS), F32)],
        compiler_params=_cparams(("arbitrary",), 56),
        name="merge_ln_router",
    )(a_out, b_out, pg, pg, x2d, wa, wb, wo, ln_g, ln_b, w_router, b_router)


def _dispatch_kernel(dest_ref, gap_start_ref, gap_len_ref, total_ref, h_ref, xs_ref, stage_ref, zero_ref,
                     sem, zsem, *, max_slack):
    i = pl.program_id(0)
    last = pl.num_programs(0) - 1
    slot = lax.rem(i, 2)
    n = dest_ref.shape[0]

    @pl.when(i == 0)
    def _():
        zero_ref[...] = jnp.zeros_like(zero_ref)

        def gap_copy(e, k):
            return pltpu.make_async_copy(zero_ref.at[pl.ds(0, 1)], xs_ref.at[pl.ds(gap_start_ref[e] + k, 1)], zsem)

        n_slots = xs_ref.shape[0]

        def slack_copy(k):
            row = pl.multiple_of(total_ref[0] + k * MOE_CHUNK, MOE_CHUNK)
            return pltpu.make_async_copy(zero_ref, xs_ref.at[pl.ds(row, MOE_CHUNK)], zsem)

        def slack(action):
            for k in range(max_slack):
                @pl.when(total_ref[0] + k * MOE_CHUNK < n_slots)
                def _():
                    action(slack_copy(k))

        def issue(e, _):
            lax.fori_loop(0, gap_len_ref[e], lambda k, c: (gap_copy(e, k).start(), c)[1], 0)
            return 0

        def settle(e, _):
            lax.fori_loop(0, gap_len_ref[e], lambda k, c: (gap_copy(e, k).wait(), c)[1], 0)
            return 0

        lax.fori_loop(0, N_EXPERTS, issue, 0)
        slack(lambda cp: cp.start())
        lax.fori_loop(0, N_EXPERTS, settle, 0)
        slack(lambda cp: cp.wait())

    def drain(sl):
        pltpu.make_async_copy(xs_ref.at[pl.ds(0, n)], xs_ref.at[pl.ds(0, n)], sem.at[sl]).wait()

    @pl.when(i >= 2)
    def _():
        drain(slot)

    stage_ref[slot] = h_ref[...]

    def scatter(sl):
        def start(g, _):
            base = pl.multiple_of(g * ROW_TILE, ROW_TILE)
            for u in range(ROW_TILE * TOP_K):
                r = u // TOP_K
                pltpu.make_async_copy(stage_ref.at[sl, pl.ds(base + r, 1)],
                                      xs_ref.at[pl.ds(dest_ref[g * (ROW_TILE * TOP_K) + u], 1)],
                                      sem.at[sl]).start(priority=u % 2)
            return 0
        lax.fori_loop(0, n // (ROW_TILE * TOP_K), start, 0)

    for parity in range(2):
        @pl.when(slot == parity)
        def _():
            scatter(parity)

    @pl.when(i == last)
    def _():
        drain(slot)

        @pl.when(i >= 1)
        def _():
            drain(1 - slot)


def _dispatch(h, dest_flat, gap_start, gap_len, total, n_slots):
    s, d = h.shape
    tm = min(128, s)
    smem = pl.BlockSpec(memory_space=pltpu.SMEM)
    max_slack = (n_slots - s * TOP_K + MOE_CHUNK - 1) // MOE_CHUNK
    return pl.pallas_call(
        functools.partial(_dispatch_kernel, max_slack=max_slack),
        grid=(s // tm,),
        in_specs=[
            pl.BlockSpec((tm * TOP_K,), lambda i: (i,), memory_space=pltpu.SMEM),
            smem, smem, smem,
            pl.BlockSpec((tm, d), lambda i: (i, 0)),
        ],
        out_specs=pl.BlockSpec(memory_space=pl.ANY),
        out_shape=jax.ShapeDtypeStruct((n_slots, d), h.dtype),
        scratch_shapes=[pltpu.VMEM((2, tm, d), h.dtype), pltpu.VMEM((MOE_CHUNK, d), h.dtype),
                        pltpu.SemaphoreType.DMA((2,)), pltpu.SemaphoreType.DMA(())],
        compiler_params=_cparams(("arbitrary",), 32),
        name="moe_dispatch",
    )(dest_flat, gap_start, gap_len, total, h)


ROW_DMA_PRIORITY = 1


def _expert_rows_pipeline(n_chunks, prologue, in_copy, out_copy, compute):
    @pl.when(n_chunks > 0)
    def _():
        in_copy(0, 0).start(priority=ROW_DMA_PRIORITY)
        prologue()

        def body(c, _):
            slot = lax.rem(c, 2)
            in_copy(c, slot).wait()

            @pl.when(c + 1 < n_chunks)
            def _():
                in_copy(c + 1, 1 - slot).start(priority=ROW_DMA_PRIORITY)

            @pl.when(c >= 2)
            def _():
                out_copy(c - 2, slot).wait()

            compute(slot, c)
            out_copy(c, slot).start(priority=ROW_DMA_PRIORITY)
            return 0

        lax.fori_loop(0, n_chunks, body, 0)

        @pl.when(n_chunks >= 2)
        def _():
            out_copy(n_chunks - 2, lax.rem(n_chunks, 2)).wait()

        out_copy(n_chunks - 1, lax.rem(n_chunks - 1, 2)).wait()


WEIGHT_COPY_PARTS = 8
GATE_UP_BAND = 256


class _SplitCopy:
    def __init__(self, src_hbm, idx, dst, sem):
        rows = dst.shape[0] // WEIGHT_COPY_PARTS
        self.parts = [
            pltpu.make_async_copy(src_hbm.at[idx, pl.ds(p * rows, rows)], dst.at[pl.ds(p * rows, rows)], sem)
            for p in range(WEIGHT_COPY_PARTS)]

    def start(self):
        for p, part in enumerate(self.parts):
            part.start(priority=p % 2)

    def wait(self):
        for part in self.parts:
            part.wait()


def _chunk_rows(start_ref, e, c, n_slots):
    del n_slots
    return pl.ds(pl.multiple_of(start_ref[e] + c * MOE_CHUNK, MOE_ALIGN), MOE_CHUNK)


def _gate_up_kernel(start_ref, cnt_ref, x_hbm, w_hbm, b_ref, act_hbm,
                    w_stage, w_sc, xbuf, obuf, sem_in, sem_out, sem_w):
    e = pl.program_id(0)
    n_slots = x_hbm.shape[0]
    n_chunks = (cnt_ref[e] + MOE_CHUNK - 1) // MOE_CHUNK
    d_ff = obuf.shape[2]

    cw = GATE_UP_BAND
    n_bands = 2 * d_ff // cw
    has_weights = e < N_EXPERTS

    def band_copy(idx, b):
        cols = pl.ds(b * cw, cw)
        return pltpu.make_async_copy(w_hbm.at[idx, :, cols], w_stage.at[:, cols], sem_w.at[b])

    def start_bands(idx, bands):
        for b in bands:
            band_copy(idx, b).start(priority=b % 2)

    def refill(bands):
        @pl.when(e + 1 < N_EXPERTS)
        def _():
            start_bands(e + 1, bands)

    @pl.when(e == 0)
    def _():
        start_bands(0, range(n_bands))

    def no_prologue():
        pass

    def in_copy(c, slot):
        return pltpu.make_async_copy(x_hbm.at[_chunk_rows(start_ref, e, c, n_slots)], xbuf.at[slot], sem_in.at[slot])

    def out_copy(c, slot):
        return pltpu.make_async_copy(obuf.at[slot], act_hbm.at[_chunk_rows(start_ref, e, c, n_slots)],
                                     sem_out.at[slot])

    def rows_times_weights(slot, cast_first):
        x = xbuf[slot].astype(BF16)
        for c in range(d_ff // cw):
            gs = slice(c * cw, (c + 1) * cw)
            us = slice(d_ff + c * cw, d_ff + (c + 1) * cw)
            if cast_first:
                bands = (c, d_ff // cw + c)
                for b in bands:
                    band_copy(e, b).wait()
                w_sc[:, gs] = w_stage[:, gs].astype(BF16)
                w_sc[:, us] = w_stage[:, us].astype(BF16)
                refill(bands)
            g = jnp.dot(x, w_sc[:, gs], preferred_element_type=F32) + b_ref[:, gs]
            u = jnp.dot(x, w_sc[:, us], preferred_element_type=F32) + b_ref[:, us]
            g = jnp.minimum(g, SWIGLU_LIMIT)
            u = jnp.clip(u, -SWIGLU_LIMIT, SWIGLU_LIMIT)
            obuf[slot, :, gs] = (g * jax.nn.sigmoid(SWIGLU_ALPHA * g) * (u + 1.0)).astype(obuf.dtype)

    def compute(slot, c):
        fresh = (c == 0) & has_weights

        @pl.when(fresh)
        def _():
            rows_times_weights(slot, True)

        @pl.when(jnp.logical_not(fresh))
        def _():
            rows_times_weights(slot, False)

    @pl.when((n_chunks == 0) & has_weights)
    def _():
        for b in range(n_bands):
            band_copy(e, b).wait()
        refill(range(n_bands))

    _expert_rows_pipeline(n_chunks, no_prologue, in_copy, out_copy, compute)


def _down_kernel(start_ref, cnt_ref, a_hbm, w_hbm, bd_ref, y_hbm, w_stage, wd_sc, abuf, obuf, sem_in, sem_out,
                 sem_w):
    e = pl.program_id(0)
    n_slots = y_hbm.shape[0]
    n_chunks = (cnt_ref[e] + MOE_CHUNK - 1) // MOE_CHUNK

    def w_copy(idx):
        return _SplitCopy(w_hbm, idx, w_stage, sem_w)

    @pl.when(e == 0)
    def _():
        w_copy(0).start()

    def cast_weights():
        @pl.when(e < N_EXPERTS)
        def _():
            w_copy(e).wait()
            wd_sc[...] = w_stage[...].astype(BF16)

            @pl.when(e + 1 < N_EXPERTS)
            def _():
                w_copy(e + 1).start()

    @pl.when(n_chunks == 0)
    def _():
        cast_weights()

    def in_copy(c, slot):
        return pltpu.make_async_copy(a_hbm.at[_chunk_rows(start_ref, e, c, n_slots)], abuf.at[slot],
                                     sem_in.at[slot])

    def out_copy(c, slot):
        return pltpu.make_async_copy(obuf.at[slot], y_hbm.at[_chunk_rows(start_ref, e, c, n_slots)],
                                     sem_out.at[slot])

    def compute(slot, c):
        del c
        cw = 512
        for c in range(obuf.shape[2] // cw):
            cs = slice(c * cw, (c + 1) * cw)
            obuf[slot, :, cs] = jnp.dot(abuf[slot], wd_sc[:, cs], preferred_element_type=F32) + bd_ref[:, cs]

    _expert_rows_pipeline(n_chunks, cast_weights, in_copy, out_copy, compute)


def _experts(xs, starts, counts, w_gate_up, b_gate_up, w_down, b_down):
    n_slots, d = xs.shape
    any_spec = pl.BlockSpec(memory_space=pl.ANY)
    dma_sems = pltpu.SemaphoreType.DMA((2,))
    bgu = b_gate_up.reshape(N_EXPERTS, 1, 2 * D_FF)
    n_groups = starts.shape[0]
    wi = lambda e: jnp.minimum(e, N_EXPERTS - 1)
    act = pl.pallas_call(
        _gate_up_kernel,
        grid_spec=pltpu.PrefetchScalarGridSpec(
            num_scalar_prefetch=2,
            grid=(n_groups,),
            in_specs=[
                any_spec,
                any_spec,
                pl.BlockSpec((None, 1, 2 * D_FF), lambda e, st, ct: (wi(e), 0, 0)),
            ],
            out_specs=any_spec,
            scratch_shapes=[pltpu.VMEM((d, 2 * D_FF), F32), pltpu.VMEM((d, 2 * D_FF), BF16),
                            pltpu.VMEM((2, MOE_CHUNK, d), xs.dtype), pltpu.VMEM((2, MOE_CHUNK, D_FF), BF16),
                            dma_sems, dma_sems, pltpu.SemaphoreType.DMA((2 * D_FF // GATE_UP_BAND,))],
        ),
        out_shape=jax.ShapeDtypeStruct((n_slots, D_FF), BF16),
        compiler_params=_cparams(("arbitrary",), 60),
        name="moe_gate_up",
    )(starts, counts, xs, w_gate_up, bgu)
    bd = b_down.reshape(N_EXPERTS, 1, d)
    ys = pl.pallas_call(
        _down_kernel,
        grid_spec=pltpu.PrefetchScalarGridSpec(
            num_scalar_prefetch=2,
            grid=(n_groups,),
            in_specs=[
                any_spec,
                any_spec,
                pl.BlockSpec((None, 1, d), lambda e, st, ct: (wi(e), 0, 0)),
            ],
            out_specs=any_spec,
            scratch_shapes=[pltpu.VMEM((D_FF, d), F32), pltpu.VMEM((D_FF, d), BF16),
                            pltpu.VMEM((2, MOE_CHUNK, D_FF), BF16), pltpu.VMEM((2, MOE_CHUNK, d), F32),
                            dma_sems, dma_sems, pltpu.SemaphoreType.DMA(())],
        ),
        out_shape=jax.ShapeDtypeStruct((n_slots, d), F32),
        compiler_params=_cparams(("arbitrary",), 56),
        name="moe_down",
    )(starts, counts, act, w_down, bd)
    return ys


def _combine_kernel(dest_ref, dest_next_ref, gate_ref, h_ref, g_ref, beta_ref, ys_ref, o_ref, buf_ref, sem,
                    *, alpha):
    i = pl.program_id(0)
    last = pl.num_programs(0) - 1
    slot = lax.rem(i, 2)
    n = dest_ref.shape[0]

    def gather(dref, sl):
        def start(g, _):
            base = pl.multiple_of(g * ROW_TILE, ROW_TILE)
            for u in range(ROW_TILE * TOP_K):
                r, k = divmod(u, TOP_K)
                pltpu.make_async_copy(ys_ref.at[pl.ds(dref[g * (ROW_TILE * TOP_K) + u], 1)],
                                      buf_ref.at[sl, k, pl.ds(base + r, 1)],
                                      sem.at[sl]).start(priority=u % 2)
            return 0
        lax.fori_loop(0, n // (ROW_TILE * TOP_K), start, 0)

    for parity in range(2):
        @pl.when((i == 0) & (slot == parity))
        def _():
            gather(dest_ref, parity)

        @pl.when((i < last) & (slot == parity))
        def _():
            gather(dest_next_ref, 1 - parity)

    pltpu.make_async_copy(buf_ref.at[slot], buf_ref.at[slot], sem.at[slot]).wait()
    gates = gate_ref[...]
    y = gates[:, 0:1] * buf_ref[slot, 0]
    for k in range(1, TOP_K):
        y = y + gates[:, k:k + 1] * buf_ref[slot, k]
    o_ref[...] = _layer_norm(alpha * h_ref[...] + y, g_ref[...], beta_ref[...])


def _combine(ys, dest_flat, gates, h, ln_g, ln_b, alpha):
    s, d = h.shape
    tm = min(128, s)
    n_steps = s // tm
    return pl.pallas_call(
        functools.partial(_combine_kernel, alpha=alpha),
        grid=(n_steps,),
        in_specs=[
            pl.BlockSpec((tm * TOP_K,), lambda i: (i,), memory_space=pltpu.SMEM),
            pl.BlockSpec((tm * TOP_K,), lambda i: (jnp.minimum(i + 1, n_steps - 1),), memory_space=pltpu.SMEM),
            pl.BlockSpec((tm, TOP_K), lambda i: (i, 0)),
            pl.BlockSpec((tm, d), lambda i: (i, 0)),
            pl.BlockSpec((1, d), lambda i: (0, 0)),
            pl.BlockSpec((1, d), lambda i: (0, 0)),
            pl.BlockSpec(memory_space=pl.ANY),
        ],
        out_specs=pl.BlockSpec((tm, d), lambda i: (i, 0)),
        out_shape=jax.ShapeDtypeStruct((s, d), F32),
        scratch_shapes=[pltpu.VMEM((2, TOP_K, tm, d), F32), pltpu.SemaphoreType.DMA((2,))],
        compiler_params=_cparams(("arbitrary",), 32),
        name="moe_combine",
    )(dest_flat, dest_flat, gates, h, ln_g, ln_b, ys)


def _layer(x2d, w_in, b_forget, w_branch_a, w_branch_b, w_out, ln1_g, ln1_b, w_router, b_router,
           w_gate_up, b_gate_up, w_down, b_down, ln2_g, ln2_b, tables):
    s, d = x2d.shape
    alpha = (2.0 * DEPTH) ** 0.25
    scale = HEAD_DIM ** -0.5 * LOG2E

    w_bf, w_g = _prepare_w_in(w_in.T, scale)
    pbf, pg = _project(x2d, w_bf, tables, w_g)

    fox_tq = min(512, s)
    fox_tk = min(512, s)
    af_t = pg[:, COL_SMALL:COL_SMALL + FOX_HEADS].T
    c_t = _forget_cumsum(af_t, b_forget)
    c_tiles = c_t.reshape(FOX_HEADS, s // fox_tk, 1, fox_tk)
    a_out = _fox_attention(pbf, c_tiles, s, fox_tq, fox_tk, 4)

    b_out = _dsa_attention(pbf, pg, s, min(256, s), min(512, s))

    h, eidx, gates, pos, counts = _merge(
        a_out, b_out, pg, x2d, w_branch_a.astype(BF16), w_branch_b.astype(BF16), w_out.astype(BF16),
        ln1_g.reshape(1, d), ln1_b.reshape(1, d), w_router, b_router.reshape(1, N_EXPERTS), alpha)

    counts = counts.reshape(N_EXPERTS).astype(I32)
    aligned = (counts + MOE_ALIGN - 1) // MOE_ALIGN * MOE_ALIGN
    ends = jnp.cumsum(aligned).astype(I32)
    starts = ends - aligned
    bound = s * TOP_K + N_EXPERTS * MOE_ALIGN + 2 * MOE_CHUNK
    n_slots = (bound + MOE_CHUNK - 1) // MOE_CHUNK * MOE_CHUNK
    experts = jnp.arange(N_EXPERTS, dtype=I32)
    start_of = jnp.sum(jnp.where(eidx[..., None] == experts, starts, 0), axis=-1)
    dest = (start_of + pos).reshape(s * TOP_K).astype(I32)
    total = (ends[-1:] + MOE_CHUNK - 1) // MOE_CHUNK * MOE_CHUNK
    gap_len = aligned - counts + jnp.where(experts == N_EXPERTS - 1, total[0] - ends[-1], 0)
    group_starts = jnp.concatenate([starts, total])
    group_counts = jnp.concatenate([counts + jnp.where(experts == N_EXPERTS - 1, gap_len, 0), n_slots - total])

    xs = _dispatch(h, dest, starts + counts, gap_len, total, n_slots)
    ys = _experts(xs, group_starts, group_counts, w_gate_up, b_gate_up, w_down, b_down)
    return _combine(ys, dest, gates, h, ln2_g.reshape(1, d), ln2_b.reshape(1, d), alpha)


def kernel(x, w_in, b_forget, w_branch_a, w_branch_b, w_out, ln1_g, ln1_b, w_router, b_router,
           w_gate_up, b_gate_up, w_down, b_down, ln2_g, ln2_b):
    bsz, s, d = x.shape
    tables = _rope_tables(s)
    outs = []
    for bi in range(bsz):
        xb = x[bi]
        for l in range(DEPTH):
            xb = _layer(xb, w_in[l], b_forget[l], w_branch_a[l], w_branch_b[l], w_out[l], ln1_g[l], ln1_b[l],
                        w_router[l], b_router[l], w_gate_up[l], b_gate_up[l], w_down[l], b_down[l],
                        ln2_g[l], ln2_b[l], tables)
        outs.append(xb)
    return outs[0][None] if bsz == 1 else jnp.stack(outs)
```

```python
import functools

import numpy as np
import jax
import jax.numpy as jnp
from jax import lax
from jax.experimental import pallas as pl
from jax.experimental.pallas import tpu as pltpu

F32 = jnp.float32
BF16 = jnp.bfloat16
I32 = jnp.int32

D_MODEL = 2048
DEPTH = 1
CHUNK = 64
HEAD_DIM = 128
FOX_HEADS = 8
DSA_HEADS = 8
DSA_KV_HEADS = 2
DSA_GROUP = DSA_HEADS // DSA_KV_HEADS
IDX_HEADS = 16
IDX_DIM = 64
TOPK_MAX = 256
ROPE_THETA = 500000.0
ROT_FRACTION_DEN = 4
MIX_A = FOX_HEADS * HEAD_DIM
MIX_B = DSA_HEADS * HEAD_DIM
N_EXPERTS = 32
TOP_K = 4
D_FF = D_MODEL
SWIGLU_ALPHA = 1.702
SWIGLU_LIMIT = 7.0
LN_EPS = 1e-5

MIB = 1024 * 1024
NEG_BIG = -1e30
LOG2E = 1.4426950408889634
NT_DIMS = (((1,), (1,)), ((), ()))
INT_MIN = -(2 ** 31)

COL_AQ = 0
COL_AK = 1024
COL_AV = 2048
COL_BQ = 3072
COL_BK = 4096
COL_BV = 4352
COL_IQ = 4608
COL_IK = 5632
N_PBF = 5888
PROJ_TN = 256
ROPE_TABLE_W = 128
ROPE_HEAD_TILES = (12, 13, 14, 15, 16)
ROPE_IDX_TILES = (18, 19, 20, 21, 22)
COL_GA = 0
COL_GB = 2048
COL_SMALL = 4096
N_PG = 4224
PG_TN = 384

MOE_CHUNK = 256
MOE_ALIGN = 16
ROW_TILE = 8


def _cparams(dims, vmem_mib):
    return pltpu.CompilerParams(dimension_semantics=dims, vmem_limit_bytes=vmem_mib * MIB)


IN_SIZES = (MIX_A, MIX_A, MIX_A, FOX_HEADS, MIX_B, DSA_KV_HEADS * HEAD_DIM, DSA_KV_HEADS * HEAD_DIM,
            IDX_HEADS * IDX_DIM, IDX_DIM, IDX_HEADS, D_MODEL, D_MODEL)
IN_OFF = dict(zip(("aq", "ak", "av", "af", "bq", "bk", "bv", "iq", "ik", "iw", "ga", "gb", "end"),
                  np.concatenate([[0], np.cumsum(IN_SIZES)]).tolist()))


def _prep_kernel(w_ref, wbf_ref, wg_ref, *, scale):
    o = IN_OFF
    width = w_ref.shape[1]

    def rows(a, b, mult=None):
        v = w_ref[a:b, :]
        return v if mult is None else v * mult

    wbf_ref[COL_AQ:COL_AK, :] = rows(o["aq"], o["ak"], scale).astype(BF16)
    wbf_ref[COL_AK:COL_BQ, :] = rows(o["ak"], o["af"]).astype(BF16)
    wbf_ref[COL_BQ:COL_BK, :] = rows(o["bq"], o["bk"], scale).astype(BF16)
    wbf_ref[COL_BK:COL_IK, :] = rows(o["bk"], o["ik"]).astype(BF16)
    wbf_ref[COL_IK:N_PBF, :] = jnp.concatenate(
        [rows(o["ik"], o["iw"]), jnp.zeros((N_PBF - COL_IK - IDX_DIM, width), F32)], axis=0).astype(BF16)
    wg_ref[COL_GA:COL_SMALL, :] = rows(o["ga"], o["end"]).astype(BF16)
    wg_ref[COL_SMALL:N_PG, :] = jnp.concatenate(
        [rows(o["af"], o["bq"]), rows(o["iw"], o["ga"]),
         jnp.zeros((N_PG - COL_SMALL - FOX_HEADS - IDX_HEADS, width), F32)], axis=0).astype(BF16)


def _prepare_w_in(w_in_t, scale):
    n_in, d = w_in_t.shape
    tc = 256
    return pl.pallas_call(
        functools.partial(_prep_kernel, scale=scale),
        grid=(d // tc,),
        in_specs=[pl.BlockSpec((n_in, tc), lambda i: (0, i))],
        out_specs=[pl.BlockSpec((N_PBF, tc), lambda i: (0, i)), pl.BlockSpec((N_PG, tc), lambda i: (0, i))],
        out_shape=[jax.ShapeDtypeStruct((N_PBF, d), BF16), jax.ShapeDtypeStruct((N_PG, d), BF16)],
        compiler_params=_cparams(("arbitrary",), 48),
        name="w_in_prep",
    )(w_in_t)


def _tile_in(j, tiles):
    cond = j == tiles[0]
    for t in tiles[1:]:
        cond = cond | (j == t)
    return cond


def _proj_rope_kernel(x_ref, w_ref, tab_ref, o_ref, xb_ref):
    j = pl.program_id(1)

    @pl.when(j == 0)
    def _():
        xb_ref[...] = x_ref[...].astype(BF16)

    acc = lax.dot_general(xb_ref[...], w_ref[...], NT_DIMS, preferred_element_type=F32)
    tn = acc.shape[1]
    is_head = _tile_in(j, ROPE_HEAD_TILES)
    is_idx = _tile_in(j, ROPE_IDX_TILES)

    def rope(shift):
        reps = tn // ROPE_TABLE_W
        c = jnp.tile(tab_ref[0, 0], (1, reps))
        s_prev = jnp.tile(tab_ref[0, 1], (1, reps))
        s_next = jnp.tile(tab_ref[0, 2], (1, reps))
        out = acc * c + pltpu.roll(acc, shift, 1) * s_prev + pltpu.roll(acc, tn - shift, 1) * s_next
        o_ref[...] = out.astype(o_ref.dtype)

    @pl.when(is_head)
    def _():
        rope(HEAD_DIM // ROT_FRACTION_DEN // 2)

    @pl.when(is_idx)
    def _():
        rope(IDX_DIM // ROT_FRACTION_DEN // 2)

    @pl.when(jnp.logical_not(is_head | is_idx))
    def _():
        o_ref[...] = acc.astype(o_ref.dtype)


def _proj_plain_kernel(x_ref, w_ref, o_ref, xb_ref):
    j = pl.program_id(1)

    @pl.when(j == 0)
    def _():
        xb_ref[...] = x_ref[...].astype(BF16)

    o_ref[...] = lax.dot_general(xb_ref[...], w_ref[...], NT_DIMS, preferred_element_type=F32).astype(o_ref.dtype)


def _rope_group(j):
    return jnp.where(_tile_in(j, ROPE_IDX_TILES), 1, 0)


def _project(x2d, w_bf, tables, w_g):
    s, d = x2d.shape
    tm = min(1024, s)
    pbf = pl.pallas_call(
        _proj_rope_kernel,
        grid=(s // tm, N_PBF // PROJ_TN),
        in_specs=[
            pl.BlockSpec((tm, d), lambda i, j: (i, 0)),
            pl.BlockSpec((PROJ_TN, d), lambda i, j: (j, 0)),
            pl.BlockSpec((1, 3, tm, ROPE_TABLE_W), lambda i, j: (_rope_group(j), 0, i, 0)),
        ],
        out_specs=pl.BlockSpec((tm, PROJ_TN), lambda i, j: (i, j)),
        out_shape=jax.ShapeDtypeStruct((s, N_PBF), BF16),
        scratch_shapes=[pltpu.VMEM((tm, d), BF16)],
        compiler_params=_cparams(("arbitrary", "arbitrary"), 48),
        name="proj_bf16",
    )(x2d, w_bf, tables)
    pg = pl.pallas_call(
        _proj_plain_kernel,
        grid=(s // tm, N_PG // PG_TN),
        in_specs=[
            pl.BlockSpec((tm, d), lambda i, j: (i, 0)),
            pl.BlockSpec((PG_TN, d), lambda i, j: (j, 0)),
        ],
        out_specs=pl.BlockSpec((tm, PG_TN), lambda i, j: (i, j)),
        out_shape=jax.ShapeDtypeStruct((s, N_PG), F32),
        scratch_shapes=[pltpu.VMEM((tm, d), BF16)],
        compiler_params=_cparams(("arbitrary", "arbitrary"), 48),
        name="proj_f32",
    )(x2d, w_g)
    return pbf, pg


def _rope_tables(s):
    pos = jnp.arange(s, dtype=F32)

    def one(period):
        rot = period // ROT_FRACTION_DEN
        half = rot // 2
        inv = jnp.power(ROPE_THETA, -jnp.arange(0, rot, 2, dtype=F32) / rot)
        ang = pos[:, None] * inv[None, :]
        cos, sin = jnp.cos(ang), jnp.sin(ang)
        zero = jnp.zeros((s, period - rot), F32)
        c = jnp.concatenate([cos, cos, jnp.ones((s, period - rot), F32)], axis=1)
        s_prev = jnp.concatenate([jnp.zeros((s, half), F32), sin, zero], axis=1)
        s_next = jnp.concatenate([-sin, jnp.zeros((s, half), F32), zero], axis=1)
        reps = ROPE_TABLE_W // period
        return jnp.stack([jnp.tile(c, (1, reps)), jnp.tile(s_prev, (1, reps)), jnp.tile(s_next, (1, reps))])

    return jnp.stack([one(HEAD_DIM), one(IDX_DIM)])


def _cumsum_kernel(af_ref, bf_ref, c_ref, carry_ref):
    i = pl.program_id(0)

    @pl.when(i == 0)
    def _():
        carry_ref[...] = jnp.zeros_like(carry_ref)

    z = af_ref[...] + bf_ref[...]
    logf = jnp.minimum(z, 0.0) - jnp.log1p(jnp.exp(-jnp.abs(z)))
    t = z.shape[1]
    row = lax.broadcasted_iota(I32, (t, t), 0)
    col = lax.broadcasted_iota(I32, (t, t), 1)
    upper = (row <= col).astype(F32)
    c = jnp.dot(logf, upper, preferred_element_type=F32, precision=lax.Precision.HIGHEST) + carry_ref[...]
    c_ref[...] = c * LOG2E
    carry_ref[...] = c[:, t - 1:t]


def _forget_cumsum(af_t, b_forget):
    h, s = af_t.shape
    t = min(512, s)
    return pl.pallas_call(
        _cumsum_kernel,
        grid=(s // t,),
        in_specs=[pl.BlockSpec((h, t), lambda i: (0, i)), pl.BlockSpec((h, 1), lambda i: (0, 0))],
        out_specs=pl.BlockSpec((h, t), lambda i: (0, i)),
        out_shape=jax.ShapeDtypeStruct((h, s), F32),
        scratch_shapes=[pltpu.VMEM((h, 1), F32)],
        compiler_params=_cparams(("arbitrary",), 32),
        name="forget_cumsum",
    )(af_t, b_forget.reshape(h, 1).astype(F32))


def _softmax_init(m_ref, l_ref, acc_ref):
    m_ref[...] = jnp.full(m_ref.shape, NEG_BIG, F32)
    l_ref[...] = jnp.zeros(l_ref.shape, F32)
    acc_ref[...] = jnp.zeros(acc_ref.shape, F32)


def _softmax_tile(s, v, m_ref, l_ref, acc_ref, i):
    reps = s.shape[1] // 128
    m_old = m_ref[i]
    m_new = jnp.maximum(m_old, jnp.max(s, axis=1, keepdims=True))
    p = jnp.exp2(s - jnp.tile(m_new, (1, reps)))
    alpha = jnp.exp2(m_old - m_new)
    psum = p[:, :128]
    for c in range(1, reps):
        psum = psum + p[:, c * 128:(c + 1) * 128]
    l_ref[i] = alpha * l_ref[i] + psum
    acc_ref[i] = alpha * acc_ref[i] + jnp.dot(p.astype(BF16), v, preferred_element_type=F32)
    m_ref[i] = m_new


def _softmax_result(l_ref, acc_ref, i):
    return acc_ref[i] / jnp.sum(l_ref[i], axis=1, keepdims=True)


def _fox_kernel(q_ref, k_ref, v_ref, c_ref, o_ref, m_ref, l_ref, acc_ref, *, tq, tk, nh):
    qi = pl.program_id(1)
    _softmax_init(m_ref, l_ref, acc_ref)

    def tile(kt, masked):
        start = pl.multiple_of(kt * tk, tk)
        if masked:
            row = qi * tq + lax.broadcasted_iota(I32, (tq, tk), 0)
            col = kt * tk + lax.broadcasted_iota(I32, (tq, tk), 1)
            causal = col <= row
        for h in range(nh):
            hs = slice(h * HEAD_DIM, (h + 1) * HEAD_DIM)
            k = k_ref[pl.ds(start, tk), hs]
            v = v_ref[pl.ds(start, tk), hs]
            s = lax.dot_general(q_ref[:, hs], k, (((1,), (1,)), ((), ())), preferred_element_type=F32)
            s = s - c_ref[h, kt]
            if masked:
                s = jnp.where(causal, s, NEG_BIG)
            _softmax_tile(s, v, m_ref, l_ref, acc_ref, h)

    n_full = (qi * tq) // tk

    def body(kt, _):
        tile(kt, False)
        return 0

    lax.fori_loop(0, n_full, body, 0)
    for t in range((tq + tk - 1) // tk):
        tile(n_full + t, True)
    for h in range(nh):
        o_ref[:, h * HEAD_DIM:(h + 1) * HEAD_DIM] = _softmax_result(l_ref, acc_ref, h).astype(o_ref.dtype)


def _fox_attention(pbf, c_tiles, s, tq, tk, nh):
    cb = nh * HEAD_DIM
    return pl.pallas_call(
        functools.partial(_fox_kernel, tq=tq, tk=tk, nh=nh),
        grid=(FOX_HEADS // nh, s // tq),
        in_specs=[
            pl.BlockSpec((tq, cb), lambda g, i: (i, COL_AQ // cb + g)),
            pl.BlockSpec((s, cb), lambda g, i: (0, COL_AK // cb + g)),
            pl.BlockSpec((s, cb), lambda g, i: (0, COL_AV // cb + g)),
            pl.BlockSpec((nh, s // tk, 1, tk), lambda g, i: (g, 0, 0, 0)),
        ],
        out_specs=pl.BlockSpec((tq, cb), lambda g, i: (i, g)),
        out_shape=jax.ShapeDtypeStruct((s, MIX_A), BF16),
        scratch_shapes=[
            pltpu.VMEM((nh, tq, 128), F32),
            pltpu.VMEM((nh, tq, 128), F32),
            pltpu.VMEM((nh, tq, HEAD_DIM), F32),
        ],
        compiler_params=_cparams(("arbitrary", "arbitrary"), 56),
        name="fox_attention",
    )(pbf, pbf, pbf, c_tiles)


def _dsa_kernel(bq_ref, iq_lo_ref, iq_hi_ref, iw_ref, ik_ref, bk_ref, bv_ref, o_ref,
                key_ref, qg_ref, m_ref, l_ref, acc_ref, *, tq, tk, n_sel):
    b = pl.program_id(0)
    n_tiles = (b * tq) // tk + 1
    row_g = b * tq + lax.broadcasted_iota(I32, (tq, tk), 0)
    adm_end = (row_g // CHUNK + 1) * CHUNK
    col_l = lax.broadcasted_iota(I32, (tq, tk), 1)

    idx_scale = (IDX_HEADS ** -0.5) * (IDX_DIM ** -0.5)
    iw = iw_ref[...][:, 8:8 + IDX_HEADS] * idx_scale
    iq = jnp.concatenate([iq_lo_ref[...], iq_hi_ref[...]], axis=1)

    def score_tile(kt, _):
        start = pl.multiple_of(kt * tk, tk)
        ik = ik_ref[pl.ds(start, tk), :][:, :IDX_DIM]
        acc = jnp.zeros((tq, tk), F32)
        for h in range(IDX_HEADS):
            a = iq[:, h * IDX_DIM:(h + 1) * IDX_DIM]
            rel = lax.dot_general(a, ik, (((1,), (1,)), ((), ())), preferred_element_type=F32)
            acc = acc + jnp.maximum(rel, 0.0) * iw[:, h:h + 1]
        key_ref[kt] = jnp.where(kt * tk + col_l < adm_end, acc, -jnp.inf)
        return 0

    lax.fori_loop(0, n_tiles, score_tile, 0)

    def float_of(code_u):
        code = code_u ^ INT_MIN
        return pltpu.bitcast(jnp.where(code >= 0, code, code ^ 0x7FFFFFFF), F32)

    def count_ge(cand):
        def body(kt, part):
            ge = jnp.where(key_ref[kt] >= cand, 1, 0)
            for c in range(tk // 128):
                part = part + ge[:, c * 128:(c + 1) * 128]
            return part
        part = lax.fori_loop(0, n_tiles, body, jnp.zeros((tq, 128), I32))
        return jnp.sum(part, axis=1, keepdims=True)

    def search_cond(carry):
        i, _, _, pending = carry
        return (i < 32) & (pending > 0)

    def refine(i, t_u, hit):
        cand_u = t_u | lax.shift_left(jnp.int32(1), 31 - i)
        cnt = count_ge(float_of(cand_u))
        return jnp.where(cnt >= n_sel, cand_u, t_u), jnp.where(cnt == n_sel, 1, hit)

    def search_step(carry):
        i, t_u, hit, _ = carry
        t_u, hit = refine(i, t_u, hit)
        return i + 1, t_u, hit, jnp.sum(1 - hit)

    zeros = jnp.zeros((tq, 1), I32)
    lead = 12
    t_u, hit = lax.fori_loop(0, lead, lambda i, c: refine(i, *c), (zeros, zeros))
    _, t_u, _, _ = lax.while_loop(search_cond, search_step, (jnp.int32(lead), t_u, hit, jnp.sum(1 - hit)))
    thr = jnp.where(t_u == 0, jnp.finfo(F32).min, float_of(t_u))

    for j in range(DSA_KV_HEADS):
        for g in range(DSA_GROUP):
            hd = j * DSA_GROUP + g
            qg_ref[j, g * tq:(g + 1) * tq, :] = bq_ref[:, hd * HEAD_DIM:(hd + 1) * HEAD_DIM]
    _softmax_init(m_ref, l_ref, acc_ref)

    def attn_tile(kt, _):
        start = pl.multiple_of(kt * tk, tk)
        sel = key_ref[kt] >= thr
        for j in range(DSA_KV_HEADS):
            hs = slice(j * HEAD_DIM, (j + 1) * HEAD_DIM)
            k = bk_ref[pl.ds(start, tk), hs]
            v = bv_ref[pl.ds(start, tk), hs]
            s = lax.dot_general(qg_ref[j], k, (((1,), (1,)), ((), ())), preferred_element_type=F32)
            s = jnp.where(sel[None], s.reshape(DSA_GROUP, tq, tk), NEG_BIG).reshape(DSA_GROUP * tq, tk)
            _softmax_tile(s, v, m_ref, l_ref, acc_ref, j)
        return 0

    lax.fori_loop(0, n_tiles, attn_tile, 0)

    for j in range(DSA_KV_HEADS):
        o = _softmax_result(l_ref, acc_ref, j)
        for g in range(DSA_GROUP):
            hd = j * DSA_GROUP + g
            o_ref[:, hd * HEAD_DIM:(hd + 1) * HEAD_DIM] = o[g * tq:(g + 1) * tq].astype(o_ref.dtype)


def _dsa_attention(pbf, pg, s, tq, tk):
    n_sel = min(TOPK_MAX, s // 4)
    kvw = DSA_KV_HEADS * HEAD_DIM
    rows = DSA_GROUP * tq
    iqw = IDX_HEADS * IDX_DIM // 2
    return pl.pallas_call(
        functools.partial(_dsa_kernel, tq=tq, tk=tk, n_sel=n_sel),
        grid=(s // tq,),
        in_specs=[
            pl.BlockSpec((tq, MIX_B), lambda b: (b, COL_BQ // MIX_B)),
            pl.BlockSpec((tq, iqw), lambda b: (b, COL_IQ // iqw)),
            pl.BlockSpec((tq, iqw), lambda b: (b, COL_IQ // iqw + 1)),
            pl.BlockSpec((tq, 128), lambda b: (b, COL_SMALL // 128)),
            pl.BlockSpec((s, 128), lambda b: (0, COL_IK // 128)),
            pl.BlockSpec((s, kvw), lambda b: (0, COL_BK // kvw)),
            pl.BlockSpec((s, kvw), lambda b: (0, COL_BV // kvw)),
        ],
        out_specs=pl.BlockSpec((tq, MIX_B), lambda b: (b, 0)),
        out_shape=jax.ShapeDtypeStruct((s, MIX_B), BF16),
        scratch_shapes=[
            pltpu.VMEM((s // tk, tq, tk), F32),
            pltpu.VMEM((DSA_KV_HEADS, rows, HEAD_DIM), BF16),
            pltpu.VMEM((DSA_KV_HEADS, rows, 128), F32),
            pltpu.VMEM((DSA_KV_HEADS, rows, 128), F32),
            pltpu.VMEM((DSA_KV_HEADS, rows, HEAD_DIM), F32),
        ],
        compiler_params=_cparams(("arbitrary",), 56),
        name="dsa_attention",
    )(pbf, pbf, pbf, pg, pbf, pbf, pbf)


def _layer_norm(z, g, b):
    mu = jnp.mean(z, axis=-1, keepdims=True)
    zc = z - mu
    var = jnp.mean(zc * zc, axis=-1, keepdims=True)
    return zc * lax.rsqrt(var + LN_EPS) * g + b


def _merge_kernel(a_ref, b_ref, ga_ref, gb_ref, x_ref, wa_ref, wb_ref, wo_ref, g_ref, beta_ref,
                  wr_ref, br_ref, h_ref, eidx_ref, gate_ref, pos_ref, cnt_ref, carry_ref, *, alpha):
    i = pl.program_id(0)

    @pl.when(i == 0)
    def _():
        carry_ref[...] = jnp.zeros_like(carry_ref)

    ma = jnp.dot(a_ref[...], wa_ref[...], preferred_element_type=F32)
    mb = jnp.dot(b_ref[...], wb_ref[...], preferred_element_type=F32)
    merged = jax.nn.sigmoid(ga_ref[...]) * ma + jax.nn.sigmoid(gb_ref[...]) * mb
    y = jnp.dot(merged.astype(BF16), wo_ref[...], preferred_element_type=F32)
    h = _layer_norm(alpha * x_ref[...] + y, g_ref[...], beta_ref[...])
    h_ref[...] = h

    wr = wr_ref[...]
    h_hi = h.astype(BF16)
    h_lo = (h - h_hi.astype(F32)).astype(BF16)
    wr_hi = wr.astype(BF16)
    wr_lo = (wr - wr_hi.astype(F32)).astype(BF16)
    logits = (jnp.dot(h_hi, wr_hi, preferred_element_type=F32) + jnp.dot(h_lo, wr_hi, preferred_element_type=F32)
              + jnp.dot(h_hi, wr_lo, preferred_element_type=F32) + br_ref[...])
    tm, ne = logits.shape
    lane = lax.broadcasted_iota(I32, (tm, ne), 1)
    lane_k = lax.broadcasted_iota(I32, (tm, TOP_K), 1)
    work = logits
    vals, sels = [], []
    eidx = jnp.zeros((tm, TOP_K), I32)
    onehot = jnp.zeros((tm, ne), F32)
    for k in range(TOP_K):
        mv = jnp.max(work, axis=1, keepdims=True)
        idx = jnp.min(jnp.where(work == mv, lane, ne), axis=1, keepdims=True)
        sel = lane == idx
        vals.append(mv)
        sels.append(sel)
        eidx = jnp.where(lane_k == k, idx, eidx)
        onehot = onehot + jnp.where(sel, 1.0, 0.0)
        work = jnp.where(sel, -jnp.inf, work)
    exps = [jnp.exp(v - vals[0]) for v in vals]
    denom = exps[0] + exps[1] + exps[2] + exps[3]
    gates = jnp.zeros((tm, TOP_K), F32)
    for k in range(TOP_K):
        gates = jnp.where(lane_k == k, exps[k] / denom, gates)

    r_i = lax.broadcasted_iota(I32, (tm, tm), 0)
    c_i = lax.broadcasted_iota(I32, (tm, tm), 1)
    lower = jnp.where(c_i < r_i, 1.0, 0.0).astype(BF16)
    rank = carry_ref[...] + jnp.dot(lower, onehot.astype(BF16), preferred_element_type=F32)
    pos = jnp.zeros((tm, TOP_K), I32)
    for k in range(TOP_K):
        pk = jnp.sum(jnp.where(sels[k], rank, 0.0), axis=1, keepdims=True).astype(I32)
        pos = jnp.where(lane_k == k, pk, pos)
    carry_ref[...] = carry_ref[...] + jnp.sum(onehot, axis=0, keepdims=True)

    eidx_ref[...] = eidx
    gate_ref[...] = gates
    pos_ref[...] = pos
    cnt_ref[...] = carry_ref[...].astype(I32)


def _merge(a_out, b_out, pg, x2d, wa, wb, wo, ln_g, ln_b, w_router, b_router, alpha):
    s, d = x2d.shape
    tm = min(256, s)
    full = lambda shape: pl.BlockSpec(shape, lambda i: (0,) * len(shape))
    return pl.pallas_call(
        functools.partial(_merge_kernel, alpha=alpha),
        grid=(s // tm,),
        in_specs=[
            pl.BlockSpec((tm, MIX_A), lambda i: (i, 0)),
            pl.BlockSpec((tm, MIX_B), lambda i: (i, 0)),
            pl.BlockSpec((tm, d), lambda i: (i, COL_GA // d)),
            pl.BlockSpec((tm, d), lambda i: (i, COL_GB // d)),
            pl.BlockSpec((tm, d), lambda i: (i, 0)),
            full((MIX_A, d)), full((MIX_B, d)), full((d, d)),
            full((1, d)), full((1, d)), full((d, N_EXPERTS)), full((1, N_EXPERTS)),
        ],
        out_specs=[
            pl.BlockSpec((tm, d), lambda i: (i, 0)),
            pl.BlockSpec((tm, TOP_K), lambda i: (i, 0)),
            pl.BlockSpec((tm, TOP_K), lambda i: (i, 0)),
            pl.BlockSpec((tm, TOP_K), lambda i: (i, 0)),
            full((1, N_EXPERTS)),
        ],
        out_shape=[
            jax.ShapeDtypeStruct((s, d), F32),
            jax.ShapeDtypeStruct((s, TOP_K), I32),
            jax.ShapeDtypeStruct((s, TOP_K), F32),
            jax.ShapeDtypeStruct((s, TOP_K), I32),
            jax.ShapeDtypeStruct((1, N_EXPERTS), I32),
        ],
        scratch_shapes=[pltpu.VMEM((1, N_EXPERTS), F32)],
        compiler_params=_cparams(("arbitrary",), 56),
        name="merge_ln_router",
    )(a_out, b_out, pg, pg, x2d, wa, wb, wo, ln_g, ln_b, w_router, b_router)


def _dispatch_kernel(dest_ref, gap_start_ref, gap_len_ref, total_ref, h_ref, xs_ref, stage_ref, zero_ref,
                     sem, zsem, *, max_slack):
    i = pl.program_id(0)
    last = pl.num_programs(0) - 1
    slot = lax.rem(i, 2)
    n = dest_ref.shape[0]

    @pl.when(i == 0)
    def _():
        zero_ref[...] = jnp.zeros_like(zero_ref)

        def gap_copy(e, k):
            return pltpu.make_async_copy(zero_ref.at[pl.ds(0, 1)], xs_ref.at[pl.ds(gap_start_ref[e] + k, 1)], zsem)

        n_slots = xs_ref.shape[0]

        def slack_copy(k):
            row = pl.multiple_of(total_ref[0] + k * MOE_CHUNK, MOE_CHUNK)
            return pltpu.make_async_copy(zero_ref, xs_ref.at[pl.ds(row, MOE_CHUNK)], zsem)

        def slack(action):
            for k in range(max_slack):
                @pl.when(total_ref[0] + k * MOE_CHUNK < n_slots)
                def _():
                    action(slack_copy(k))

        def issue(e, _):
            lax.fori_loop(0, gap_len_ref[e], lambda k, c: (gap_copy(e, k).start(), c)[1], 0)
            return 0

        def settle(e, _):
            lax.fori_loop(0, gap_len_ref[e], lambda k, c: (gap_copy(e, k).wait(), c)[1], 0)
            return 0

        lax.fori_loop(0, N_EXPERTS, issue, 0)
        slack(lambda cp: cp.start())
        lax.fori_loop(0, N_EXPERTS, settle, 0)
        slack(lambda cp: cp.wait())

    def drain(sl):
        pltpu.make_async_copy(xs_ref.at[pl.ds(0, n)], xs_ref.at[pl.ds(0, n)], sem.at[sl]).wait()

    @pl.when(i >= 2)
    def _():
        drain(slot)

    stage_ref[slot] = h_ref[...]

    def scatter(sl):
        def start(g, _):
            base = pl.multiple_of(g * ROW_TILE, ROW_TILE)
            for u in range(ROW_TILE * TOP_K):
                r = u // TOP_K
                pltpu.make_async_copy(stage_ref.at[sl, pl.ds(base + r, 1)],
                                      xs_ref.at[pl.ds(dest_ref[g * (ROW_TILE * TOP_K) + u], 1)],
                                      sem.at[sl]).start(priority=u % 2)
            return 0
        lax.fori_loop(0, n // (ROW_TILE * TOP_K), start, 0)

    for parity in range(2):
        @pl.when(slot == parity)
        def _():
            scatter(parity)

    @pl.when(i == last)
    def _():
        drain(slot)

        @pl.when(i >= 1)
        def _():
            drain(1 - slot)


def _dispatch(h, dest_flat, gap_start, gap_len, total, n_slots):
    s, d = h.shape
    tm = min(128, s)
    smem = pl.BlockSpec(memory_space=pltpu.SMEM)
    max_slack = (n_slots - s * TOP_K + MOE_CHUNK - 1) // MOE_CHUNK
    return pl.pallas_call(
        functools.partial(_dispatch_kernel, max_slack=max_slack),
        grid=(s // tm,),
        in_specs=[
            pl.BlockSpec((tm * TOP_K,), lambda i: (i,), memory_space=pltpu.SMEM),
            smem, smem, smem,
            pl.BlockSpec((tm, d), lambda i: (i, 0)),
        ],
        out_specs=pl.BlockSpec(memory_space=pl.ANY),
        out_shape=jax.ShapeDtypeStruct((n_slots, d), h.dtype),
        scratch_shapes=[pltpu.VMEM((2, tm, d), h.dtype), pltpu.VMEM((MOE_CHUNK, d), h.dtype),
                        pltpu.SemaphoreType.DMA((2,)), pltpu.SemaphoreType.DMA(())],
        compiler_params=_cparams(("arbitrary",), 32),
        name="moe_dispatch",
    )(dest_flat, gap_start, gap_len, total, h)


ROW_DMA_PRIORITY = 1


def _expert_rows_pipeline(n_chunks, in_copy, out_copy, compute):
    @pl.when(n_chunks > 0)
    def _():
        in_copy(0, 0).start(priority=ROW_DMA_PRIORITY)

        def body(c, _):
            slot = lax.rem(c, 2)
            in_copy(c, slot).wait()

            @pl.when(c + 1 < n_chunks)
            def _():
                in_copy(c + 1, 1 - slot).start(priority=ROW_DMA_PRIORITY)

            @pl.when(c >= 2)
            def _():
                out_copy(c - 2, slot).wait()

            compute(slot, c)
            out_copy(c, slot).start(priority=ROW_DMA_PRIORITY)
            return 0

        lax.fori_loop(0, n_chunks, body, 0)

        @pl.when(n_chunks >= 2)
        def _():
            out_copy(n_chunks - 2, lax.rem(n_chunks, 2)).wait()

        out_copy(n_chunks - 1, lax.rem(n_chunks - 1, 2)).wait()


GATE_UP_BAND = 256
DOWN_BAND = 512


def _weight_band_stream(w_hbm, w_stage, sem_w, e, band, n_bands):
    def band_copy(idx, b):
        cols = pl.ds(b * band, band)
        return pltpu.make_async_copy(w_hbm.at[idx, :, cols], w_stage.at[:, cols], sem_w.at[b])

    def start_bands(idx, bands):
        for b in bands:
            band_copy(idx, b).start(priority=b % 2)

    def await_bands(bands):
        for b in bands:
            band_copy(e, b).wait()

    def refill_bands(bands):
        @pl.when(e + 1 < N_EXPERTS)
        def _():
            start_bands(e + 1, bands)

    @pl.when(e == 0)
    def _():
        start_bands(0, range(n_bands))

    return await_bands, refill_bands


def _chunk_rows(start_ref, e, c, n_slots):
    del n_slots
    return pl.ds(pl.multiple_of(start_ref[e] + c * MOE_CHUNK, MOE_ALIGN), MOE_CHUNK)


def _gate_up_kernel(start_ref, cnt_ref, x_hbm, w_hbm, b_ref, act_hbm,
                    w_stage, w_sc, xbuf, obuf, sem_in, sem_out, sem_w):
    e = pl.program_id(0)
    n_slots = x_hbm.shape[0]
    n_chunks = (cnt_ref[e] + MOE_CHUNK - 1) // MOE_CHUNK
    d_ff = obuf.shape[2]

    cw = GATE_UP_BAND
    n_bands = 2 * d_ff // cw
    has_weights = e < N_EXPERTS
    await_bands, refill = _weight_band_stream(w_hbm, w_stage, sem_w, e, cw, n_bands)

    def in_copy(c, slot):
        return pltpu.make_async_copy(x_hbm.at[_chunk_rows(start_ref, e, c, n_slots)], xbuf.at[slot], sem_in.at[slot])

    def out_copy(c, slot):
        return pltpu.make_async_copy(obuf.at[slot], act_hbm.at[_chunk_rows(start_ref, e, c, n_slots)],
                                     sem_out.at[slot])

    def rows_times_weights(slot, cast_first):
        x = xbuf[slot].astype(BF16)
        for c in range(d_ff // cw):
            gs = slice(c * cw, (c + 1) * cw)
            us = slice(d_ff + c * cw, d_ff + (c + 1) * cw)
            if cast_first:
                bands = (c, d_ff // cw + c)
                await_bands(bands)
                w_sc[:, gs] = w_stage[:, gs].astype(BF16)
                w_sc[:, us] = w_stage[:, us].astype(BF16)
                refill(bands)
            g = jnp.dot(x, w_sc[:, gs], preferred_element_type=F32) + b_ref[:, gs]
            u = jnp.dot(x, w_sc[:, us], preferred_element_type=F32) + b_ref[:, us]
            g = jnp.minimum(g, SWIGLU_LIMIT)
            u = jnp.clip(u, -SWIGLU_LIMIT, SWIGLU_LIMIT)
            obuf[slot, :, gs] = (g * jax.nn.sigmoid(SWIGLU_ALPHA * g) * (u + 1.0)).astype(obuf.dtype)

    def compute(slot, c):
        fresh = (c == 0) & has_weights

        @pl.when(fresh)
        def _():
            rows_times_weights(slot, True)

        @pl.when(jnp.logical_not(fresh))
        def _():
            rows_times_weights(slot, False)

    @pl.when((n_chunks == 0) & has_weights)
    def _():
        await_bands(range(n_bands))
        refill(range(n_bands))

    _expert_rows_pipeline(n_chunks, in_copy, out_copy, compute)


def _down_kernel(start_ref, cnt_ref, a_hbm, w_hbm, bd_ref, y_hbm, w_stage, wd_sc, abuf, obuf, sem_in, sem_out,
                 sem_w):
    e = pl.program_id(0)
    n_slots = y_hbm.shape[0]
    n_chunks = (cnt_ref[e] + MOE_CHUNK - 1) // MOE_CHUNK

    cw = DOWN_BAND
    n_bands = obuf.shape[2] // cw
    has_weights = e < N_EXPERTS
    await_bands, refill = _weight_band_stream(w_hbm, w_stage, sem_w, e, cw, n_bands)

    @pl.when((n_chunks == 0) & has_weights)
    def _():
        await_bands(range(n_bands))
        refill(range(n_bands))

    def in_copy(c, slot):
        return pltpu.make_async_copy(a_hbm.at[_chunk_rows(start_ref, e, c, n_slots)], abuf.at[slot],
                                     sem_in.at[slot])

    def out_copy(c, slot):
        return pltpu.make_async_copy(obuf.at[slot], y_hbm.at[_chunk_rows(start_ref, e, c, n_slots)],
                                     sem_out.at[slot])

    def rows_times_weights(slot, cast_first):
        for b in range(n_bands):
            cs = slice(b * cw, (b + 1) * cw)
            if cast_first:
                await_bands((b,))
                wd_sc[:, cs] = w_stage[:, cs].astype(BF16)
                refill((b,))
            obuf[slot, :, cs] = jnp.dot(abuf[slot], wd_sc[:, cs], preferred_element_type=F32) + bd_ref[:, cs]

    def compute(slot, c):
        fresh = (c == 0) & has_weights

        @pl.when(fresh)
        def _():
            rows_times_weights(slot, True)

        @pl.when(jnp.logical_not(fresh))
        def _():
            rows_times_weights(slot, False)

    _expert_rows_pipeline(n_chunks, in_copy, out_copy, compute)


def _experts(xs, starts, counts, w_gate_up, b_gate_up, w_down, b_down):
    n_slots, d = xs.shape
    any_spec = pl.BlockSpec(memory_space=pl.ANY)
    dma_sems = pltpu.SemaphoreType.DMA((2,))
    bgu = b_gate_up.reshape(N_EXPERTS, 1, 2 * D_FF)
    n_groups = starts.shape[0]
    wi = lambda e: jnp.minimum(e, N_EXPERTS - 1)
    act = pl.pallas_call(
        _gate_up_kernel,
        grid_spec=pltpu.PrefetchScalarGridSpec(
            num_scalar_prefetch=2,
            grid=(n_groups,),
            in_specs=[
                any_spec,
                any_spec,
                pl.BlockSpec((None, 1, 2 * D_FF), lambda e, st, ct: (wi(e), 0, 0)),
            ],
            out_specs=any_spec,
            scratch_shapes=[pltpu.VMEM((d, 2 * D_FF), F32), pltpu.VMEM((d, 2 * D_FF), BF16),
                            pltpu.VMEM((2, MOE_CHUNK, d), xs.dtype), pltpu.VMEM((2, MOE_CHUNK, D_FF), BF16),
                            dma_sems, dma_sems, pltpu.SemaphoreType.DMA((2 * D_FF // GATE_UP_BAND,))],
        ),
        out_shape=jax.ShapeDtypeStruct((n_slots, D_FF), BF16),
        compiler_params=_cparams(("arbitrary",), 60),
        name="moe_gate_up",
    )(starts, counts, xs, w_gate_up, bgu)
    bd = b_down.reshape(N_EXPERTS, 1, d)
    ys = pl.pallas_call(
        _down_kernel,
        grid_spec=pltpu.PrefetchScalarGridSpec(
            num_scalar_prefetch=2,
            grid=(n_groups,),
            in_specs=[
                any_spec,
                any_spec,
                pl.BlockSpec((None, 1, d), lambda e, st, ct: (wi(e), 0, 0)),
            ],
            out_specs=any_spec,
            scratch_shapes=[pltpu.VMEM((D_FF, d), F32), pltpu.VMEM((D_FF, d), BF16),
                            pltpu.VMEM((2, MOE_CHUNK, D_FF), BF16), pltpu.VMEM((2, MOE_CHUNK, d), F32),
                            dma_sems, dma_sems, pltpu.SemaphoreType.DMA((d // DOWN_BAND,))],
        ),
        out_shape=jax.ShapeDtypeStruct((n_slots, d), F32),
        compiler_params=_cparams(("arbitrary",), 56),
        name="moe_down",
    )(starts, counts, act, w_down, bd)
    return ys


def _combine_kernel(dest_ref, dest_next_ref, gate_ref, h_ref, g_ref, beta_ref, ys_ref, o_ref, buf_ref, sem,
                    *, alpha):
    i = pl.program_id(0)
    last = pl.num_programs(0) - 1
    slot = lax.rem(i, 2)
    n = dest_ref.shape[0]

    def gather(dref, sl):
        def start(g, _):
            base = pl.multiple_of(g * ROW_TILE, ROW_TILE)
            for u in range(ROW_TILE * TOP_K):
                r, k = divmod(u, TOP_K)
                pltpu.make_async_copy(ys_ref.at[pl.ds(dref[g * (ROW_TILE * TOP_K) + u], 1)],
                                      buf_ref.at[sl, k, pl.ds(base + r, 1)],
                                      sem.at[sl]).start(priority=u % 2)
            return 0
        lax.fori_loop(0, n // (ROW_TILE * TOP_K), start, 0)

    for parity in range(2):
        @pl.when((i == 0) & (slot == parity))
        def _():
            gather(dest_ref, parity)

        @pl.when((i < last) & (slot == parity))
        def _():
            gather(dest_next_ref, 1 - parity)

    pltpu.make_async_copy(buf_ref.at[slot], buf_ref.at[slot], sem.at[slot]).wait()
    gates = gate_ref[...]
    y = gates[:, 0:1] * buf_ref[slot, 0]
    for k in range(1, TOP_K):
        y = y + gates[:, k:k + 1] * buf_ref[slot, k]
    o_ref[...] = _layer_norm(alpha * h_ref[...] + y, g_ref[...], beta_ref[...])


def _combine(ys, dest_flat, gates, h, ln_g, ln_b, alpha):
    s, d = h.shape
    tm = min(128, s)
    n_steps = s // tm
    return pl.pallas_call(
        functools.partial(_combine_kernel, alpha=alpha),
        grid=(n_steps,),
        in_specs=[
            pl.BlockSpec((tm * TOP_K,), lambda i: (i,), memory_space=pltpu.SMEM),
            pl.BlockSpec((tm * TOP_K,), lambda i: (jnp.minimum(i + 1, n_steps - 1),), memory_space=pltpu.SMEM),
            pl.BlockSpec((tm, TOP_K), lambda i: (i, 0)),
            pl.BlockSpec((tm, d), lambda i: (i, 0)),
            pl.BlockSpec((1, d), lambda i: (0, 0)),
            pl.BlockSpec((1, d), lambda i: (0, 0)),
            pl.BlockSpec(memory_space=pl.ANY),
        ],
        out_specs=pl.BlockSpec((tm, d), lambda i: (i, 0)),
        out_shape=jax.ShapeDtypeStruct((s, d), F32),
        scratch_shapes=[pltpu.VMEM((2, TOP_K, tm, d), F32), pltpu.SemaphoreType.DMA((2,))],
        compiler_params=_cparams(("arbitrary",), 32),
        name="moe_combine",
    )(dest_flat, dest_flat, gates, h, ln_g, ln_b, ys)


def _layer(x2d, w_in, b_forget, w_branch_a, w_branch_b, w_out, ln1_g, ln1_b, w_router, b_router,
           w_gate_up, b_gate_up, w_down, b_down, ln2_g, ln2_b, tables):
    s, d = x2d.shape
    alpha = (2.0 * DEPTH) ** 0.25
    scale = HEAD_DIM ** -0.5 * LOG2E

    w_bf, w_g = _prepare_w_in(w_in.T, scale)
    pbf, pg = _project(x2d, w_bf, tables, w_g)

    fox_tq = min(512, s)
    fox_tk = min(512, s)
    af_t = pg[:, COL_SMALL:COL_SMALL + FOX_HEADS].T
    c_t = _forget_cumsum(af_t, b_forget)
    c_tiles = c_t.reshape(FOX_HEADS, s // fox_tk, 1, fox_tk)
    a_out = _fox_attention(pbf, c_tiles, s, fox_tq, fox_tk, 4)

    b_out = _dsa_attention(pbf, pg, s, min(256, s), min(512, s))

    h, eidx, gates, pos, counts = _merge(
        a_out, b_out, pg, x2d, w_branch_a.astype(BF16), w_branch_b.astype(BF16), w_out.astype(BF16),
        ln1_g.reshape(1, d), ln1_b.reshape(1, d), w_router, b_router.reshape(1, N_EXPERTS), alpha)

    counts = counts.reshape(N_EXPERTS).astype(I32)
    aligned = (counts + MOE_ALIGN - 1) // MOE_ALIGN * MOE_ALIGN
    ends = jnp.cumsum(aligned).astype(I32)
    starts = ends - aligned
    bound = s * TOP_K + N_EXPERTS * MOE_ALIGN + 2 * MOE_CHUNK
    n_slots = (bound + MOE_CHUNK - 1) // MOE_CHUNK * MOE_CHUNK
    experts = jnp.arange(N_EXPERTS, dtype=I32)
    start_of = jnp.sum(jnp.where(eidx[..., None] == experts, starts, 0), axis=-1)
    dest = (start_of + pos).reshape(s * TOP_K).astype(I32)
    total = (ends[-1:] + MOE_CHUNK - 1) // MOE_CHUNK * MOE_CHUNK
    gap_len = aligned - counts + jnp.where(experts == N_EXPERTS - 1, total[0] - ends[-1], 0)
    group_starts = jnp.concatenate([starts, total])
    group_counts = jnp.concatenate([counts + jnp.where(experts == N_EXPERTS - 1, gap_len, 0), n_slots - total])

    xs = _dispatch(h, dest, starts + counts, gap_len, total, n_slots)
    ys = _experts(xs, group_starts, group_counts, w_gate_up, b_gate_up, w_down, b_down)
    return _combine(ys, dest, gates, h, ln2_g.reshape(1, d), ln2_b.reshape(1, d), alpha)


def kernel(x, w_in, b_forget, w_branch_a, w_branch_b, w_out, ln1_g, ln1_b, w_router, b_router,
           w_gate_up, b_gate_up, w_down, b_down, ln2_g, ln2_b):
    bsz, s, d = x.shape
    tables = _rope_tables(s)
    outs = []
    for bi in range(bsz):
        xb = x[bi]
        for l in range(DEPTH):
            xb = _layer(xb, w_in[l], b_forget[l], w_branch_a[l], w_branch_b[l], w_out[l], ln1_g[l], ln1_b[l],
                        w_router[l], b_router[l], w_gate_up[l], b_gate_up[l], w_down[l], b_down[l],
                        ln2_g[l], ln2_b[l], tables)
        outs.append(xb)
    return outs[0][None] if bsz == 1 else jnp.stack(outs)
```

```python
import functools

import numpy as np
import jax
import jax.numpy as jnp
from jax import lax
from jax.experimental import pallas as pl
from jax.experimental.pallas import tpu as pltpu

F32 = jnp.float32
BF16 = jnp.bfloat16
I32 = jnp.int32

D_MODEL = 2048
DEPTH = 1
CHUNK = 64
HEAD_DIM = 128
FOX_HEADS = 8
DSA_HEADS = 8
DSA_KV_HEADS = 2
DSA_GROUP = DSA_HEADS // DSA_KV_HEADS
IDX_HEADS = 16
IDX_DIM = 64
TOPK_MAX = 256
ROPE_THETA = 500000.0
ROT_FRACTION_DEN = 4
MIX_A = FOX_HEADS * HEAD_DIM
MIX_B = DSA_HEADS * HEAD_DIM
N_EXPERTS = 32
TOP_K = 4
D_FF = D_MODEL
SWIGLU_ALPHA = 1.702
SWIGLU_LIMIT = 7.0
LN_EPS = 1e-5

MIB = 1024 * 1024
NEG_BIG = -1e30
LOG2E = 1.4426950408889634
NT_DIMS = (((1,), (1,)), ((), ()))
INT_MIN = -(2 ** 31)

COL_AQ = 0
COL_AK = 1024
COL_AV = 2048
COL_BQ = 3072
COL_BK = 4096
COL_BV = 4352
COL_IQ = 4608
COL_IK = 5632
N_PBF = 5888
PROJ_TN = 256
ROPE_TABLE_W = 128
ROPE_HEAD_TILES = (12, 13, 14, 15, 16)
ROPE_IDX_TILES = (18, 19, 20, 21, 22)
COL_GA = 0
COL_GB = 2048
COL_SMALL = 4096
N_PG = 4224
PG_TN = 384

MOE_CHUNK = 256
MOE_ALIGN = 16
ROW_TILE = 8


def _cparams(dims, vmem_mib):
    return pltpu.CompilerParams(dimension_semantics=dims, vmem_limit_bytes=vmem_mib * MIB)


IN_SIZES = (MIX_A, MIX_A, MIX_A, FOX_HEADS, MIX_B, DSA_KV_HEADS * HEAD_DIM, DSA_KV_HEADS * HEAD_DIM,
            IDX_HEADS * IDX_DIM, IDX_DIM, IDX_HEADS, D_MODEL, D_MODEL)
IN_OFF = dict(zip(("aq", "ak", "av", "af", "bq", "bk", "bv", "iq", "ik", "iw", "ga", "gb", "end"),
                  np.concatenate([[0], np.cumsum(IN_SIZES)]).tolist()))


def _prep_kernel(w_ref, wbf_ref, wg_ref, *, scale):
    o = IN_OFF
    width = w_ref.shape[1]

    def rows(a, b, mult=None):
        v = w_ref[a:b, :]
        return v if mult is None else v * mult

    wbf_ref[COL_AQ:COL_AK, :] = rows(o["aq"], o["ak"], scale).astype(BF16)
    wbf_ref[COL_AK:COL_BQ, :] = rows(o["ak"], o["af"]).astype(BF16)
    wbf_ref[COL_BQ:COL_BK, :] = rows(o["bq"], o["bk"], scale).astype(BF16)
    wbf_ref[COL_BK:COL_IK, :] = rows(o["bk"], o["ik"]).astype(BF16)
    wbf_ref[COL_IK:N_PBF, :] = jnp.concatenate(
        [rows(o["ik"], o["iw"]), jnp.zeros((N_PBF - COL_IK - IDX_DIM, width), F32)], axis=0).astype(BF16)
    wg_ref[COL_GA:COL_SMALL, :] = rows(o["ga"], o["end"]).astype(BF16)
    wg_ref[COL_SMALL:N_PG, :] = jnp.concatenate(
        [rows(o["af"], o["bq"]), rows(o["iw"], o["ga"]),
         jnp.zeros((N_PG - COL_SMALL - FOX_HEADS - IDX_HEADS, width), F32)], axis=0).astype(BF16)


def _prepare_w_in(w_in_t, scale):
    n_in, d = w_in_t.shape
    tc = 256
    return pl.pallas_call(
        functools.partial(_prep_kernel, scale=scale),
        grid=(d // tc,),
        in_specs=[pl.BlockSpec((n_in, tc), lambda i: (0, i))],
        out_specs=[pl.BlockSpec((N_PBF, tc), lambda i: (0, i)), pl.BlockSpec((N_PG, tc), lambda i: (0, i))],
        out_shape=[jax.ShapeDtypeStruct((N_PBF, d), BF16), jax.ShapeDtypeStruct((N_PG, d), BF16)],
        compiler_params=_cparams(("arbitrary",), 48),
        name="w_in_prep",
    )(w_in_t)


def _tile_in(j, tiles):
    cond = j == tiles[0]
    for t in tiles[1:]:
        cond = cond | (j == t)
    return cond


def _proj_rope_kernel(x_ref, w_ref, tab_ref, o_ref, xb_ref):
    j = pl.program_id(1)

    @pl.when(j == 0)
    def _():
        xb_ref[...] = x_ref[...].astype(BF16)

    acc = lax.dot_general(xb_ref[...], w_ref[...], NT_DIMS, preferred_element_type=F32)
    tn = acc.shape[1]
    is_head = _tile_in(j, ROPE_HEAD_TILES)
    is_idx = _tile_in(j, ROPE_IDX_TILES)

    def rope(shift):
        reps = tn // ROPE_TABLE_W
        c = jnp.tile(tab_ref[0, 0], (1, reps))
        s_prev = jnp.tile(tab_ref[0, 1], (1, reps))
        s_next = jnp.tile(tab_ref[0, 2], (1, reps))
        out = acc * c + pltpu.roll(acc, shift, 1) * s_prev + pltpu.roll(acc, tn - shift, 1) * s_next
        o_ref[...] = out.astype(o_ref.dtype)

    @pl.when(is_head)
    def _():
        rope(HEAD_DIM // ROT_FRACTION_DEN // 2)

    @pl.when(is_idx)
    def _():
        rope(IDX_DIM // ROT_FRACTION_DEN // 2)

    @pl.when(jnp.logical_not(is_head | is_idx))
    def _():
        o_ref[...] = acc.astype(o_ref.dtype)


def _proj_plain_kernel(x_ref, w_ref, o_ref, xb_ref):
    j = pl.program_id(1)

    @pl.when(j == 0)
    def _():
        xb_ref[...] = x_ref[...].astype(BF16)

    o_ref[...] = lax.dot_general(xb_ref[...], w_ref[...], NT_DIMS, preferred_element_type=F32).astype(o_ref.dtype)


def _rope_group(j):
    return jnp.where(_tile_in(j, ROPE_IDX_TILES), 1, 0)


def _project(x2d, w_bf, tables, w_g):
    s, d = x2d.shape
    tm = min(1024, s)
    pbf = pl.pallas_call(
        _proj_rope_kernel,
        grid=(s // tm, N_PBF // PROJ_TN),
        in_specs=[
            pl.BlockSpec((tm, d), lambda i, j: (i, 0)),
            pl.BlockSpec((PROJ_TN, d), lambda i, j: (j, 0)),
            pl.BlockSpec((1, 3, tm, ROPE_TABLE_W), lambda i, j: (_rope_group(j), 0, i, 0)),
        ],
        out_specs=pl.BlockSpec((tm, PROJ_TN), lambda i, j: (i, j)),
        out_shape=jax.ShapeDtypeStruct((s, N_PBF), BF16),
        scratch_shapes=[pltpu.VMEM((tm, d), BF16)],
        compiler_params=_cparams(("arbitrary", "arbitrary"), 48),
        name="proj_bf16",
    )(x2d, w_bf, tables)
    pg = pl.pallas_call(
        _proj_plain_kernel,
        grid=(s // tm, N_PG // PG_TN),
        in_specs=[
            pl.BlockSpec((tm, d), lambda i, j: (i, 0)),
            pl.BlockSpec((PG_TN, d), lambda i, j: (j, 0)),
        ],
        out_specs=pl.BlockSpec((tm, PG_TN), lambda i, j: (i, j)),
        out_shape=jax.ShapeDtypeStruct((s, N_PG), F32),
        scratch_shapes=[pltpu.VMEM((tm, d), BF16)],
        compiler_params=_cparams(("arbitrary", "arbitrary"), 48),
        name="proj_f32",
    )(x2d, w_g)
    return pbf, pg


def _rope_tables(s):
    pos = jnp.arange(s, dtype=F32)

    def one(period):
        rot = period // ROT_FRACTION_DEN
        half = rot // 2
        inv = jnp.power(ROPE_THETA, -jnp.arange(0, rot, 2, dtype=F32) / rot)
        ang = pos[:, None] * inv[None, :]
        cos, sin = jnp.cos(ang), jnp.sin(ang)
        zero = jnp.zeros((s, period - rot), F32)
        c = jnp.concatenate([cos, cos, jnp.ones((s, period - rot), F32)], axis=1)
        s_prev = jnp.concatenate([jnp.zeros((s, half), F32), sin, zero], axis=1)
        s_next = jnp.concatenate([-sin, jnp.zeros((s, half), F32), zero], axis=1)
        reps = ROPE_TABLE_W // period
        return jnp.stack([jnp.tile(c, (1, reps)), jnp.tile(s_prev, (1, reps)), jnp.tile(s_next, (1, reps))])

    return jnp.stack([one(HEAD_DIM), one(IDX_DIM)])


def _cumsum_kernel(af_ref, bf_ref, c_ref, carry_ref):
    i = pl.program_id(0)

    @pl.when(i == 0)
    def _():
        carry_ref[...] = jnp.zeros_like(carry_ref)

    z = af_ref[...] + bf_ref[...]
    logf = jnp.minimum(z, 0.0) - jnp.log1p(jnp.exp(-jnp.abs(z)))
    t = z.shape[1]
    row = lax.broadcasted_iota(I32, (t, t), 0)
    col = lax.broadcasted_iota(I32, (t, t), 1)
    upper = (row <= col).astype(F32)
    c = jnp.dot(logf, upper, preferred_element_type=F32, precision=lax.Precision.HIGHEST) + carry_ref[...]
    c_ref[...] = c * LOG2E
    carry_ref[...] = c[:, t - 1:t]


def _forget_cumsum(af_t, b_forget):
    h, s = af_t.shape
    t = min(512, s)
    return pl.pallas_call(
        _cumsum_kernel,
        grid=(s // t,),
        in_specs=[pl.BlockSpec((h, t), lambda i: (0, i)), pl.BlockSpec((h, 1), lambda i: (0, 0))],
        out_specs=pl.BlockSpec((h, t), lambda i: (0, i)),
        out_shape=jax.ShapeDtypeStruct((h, s), F32),
        scratch_shapes=[pltpu.VMEM((h, 1), F32)],
        compiler_params=_cparams(("arbitrary",), 32),
        name="forget_cumsum",
    )(af_t, b_forget.reshape(h, 1).astype(F32))


def _softmax_init(m_ref, l_ref, acc_ref):
    m_ref[...] = jnp.full(m_ref.shape, NEG_BIG, F32)
    l_ref[...] = jnp.zeros(l_ref.shape, F32)
    acc_ref[...] = jnp.zeros(acc_ref.shape, F32)


def _softmax_tile(s, v, m_ref, l_ref, acc_ref, i):
    reps = s.shape[1] // 128
    m_old = m_ref[i]
    m_new = jnp.maximum(m_old, jnp.max(s, axis=1, keepdims=True))
    p = jnp.exp2(s - jnp.tile(m_new, (1, reps)))
    alpha = jnp.exp2(m_old - m_new)
    psum = p[:, :128]
    for c in range(1, reps):
        psum = psum + p[:, c * 128:(c + 1) * 128]
    l_ref[i] = alpha * l_ref[i] + psum
    acc_ref[i] = alpha * acc_ref[i] + jnp.dot(p.astype(BF16), v, preferred_element_type=F32)
    m_ref[i] = m_new


def _softmax_result(l_ref, acc_ref, i):
    return acc_ref[i] / jnp.sum(l_ref[i], axis=1, keepdims=True)


def _fox_kernel(q_ref, k_ref, v_ref, c_ref, o_ref, m_ref, l_ref, acc_ref, *, tq, tk, nh):
    qi = pl.program_id(1)
    _softmax_init(m_ref, l_ref, acc_ref)

    def tile(kt, masked):
        start = pl.multiple_of(kt * tk, tk)
        if masked:
            row = qi * tq + lax.broadcasted_iota(I32, (tq, tk), 0)
            col = kt * tk + lax.broadcasted_iota(I32, (tq, tk), 1)
            causal = col <= row
        for h in range(nh):
            hs = slice(h * HEAD_DIM, (h + 1) * HEAD_DIM)
            k = k_ref[pl.ds(start, tk), hs]
            v = v_ref[pl.ds(start, tk), hs]
            s = lax.dot_general(q_ref[:, hs], k, (((1,), (1,)), ((), ())), preferred_element_type=F32)
            s = s - c_ref[h, kt]
            if masked:
                s = jnp.where(causal, s, NEG_BIG)
            _softmax_tile(s, v, m_ref, l_ref, acc_ref, h)

    n_full = (qi * tq) // tk

    def body(kt, _):
        tile(kt, False)
        return 0

    lax.fori_loop(0, n_full, body, 0)
    for t in range((tq + tk - 1) // tk):
        tile(n_full + t, True)
    for h in range(nh):
        o_ref[:, h * HEAD_DIM:(h + 1) * HEAD_DIM] = _softmax_result(l_ref, acc_ref, h).astype(o_ref.dtype)


def _fox_attention(pbf, c_tiles, s, tq, tk, nh):
    cb = nh * HEAD_DIM
    return pl.pallas_call(
        functools.partial(_fox_kernel, tq=tq, tk=tk, nh=nh),
        grid=(FOX_HEADS // nh, s // tq),
        in_specs=[
            pl.BlockSpec((tq, cb), lambda g, i: (i, COL_AQ // cb + g)),
            pl.BlockSpec((s, cb), lambda g, i: (0, COL_AK // cb + g)),
            pl.BlockSpec((s, cb), lambda g, i: (0, COL_AV // cb + g)),
            pl.BlockSpec((nh, s // tk, 1, tk), lambda g, i: (g, 0, 0, 0)),
        ],
        out_specs=pl.BlockSpec((tq, cb), lambda g, i: (i, g)),
        out_shape=jax.ShapeDtypeStruct((s, MIX_A), BF16),
        scratch_shapes=[
            pltpu.VMEM((nh, tq, 128), F32),
            pltpu.VMEM((nh, tq, 128), F32),
            pltpu.VMEM((nh, tq, HEAD_DIM), F32),
        ],
        compiler_params=_cparams(("arbitrary", "arbitrary"), 56),
        name="fox_attention",
    )(pbf, pbf, pbf, c_tiles)


def _dsa_kernel(bq_ref, iq_lo_ref, iq_hi_ref, iw_ref, ik_ref, bk_ref, bv_ref, o_ref,
                key_ref, cnt_ref, qg_ref, m_ref, l_ref, acc_ref, *, tq, tk, n_sel):
    b = pl.program_id(0)
    n_tiles = (b * tq) // tk + 1
    row_g = b * tq + lax.broadcasted_iota(I32, (tq, tk), 0)
    adm_end = (row_g // CHUNK + 1) * CHUNK
    col_l = lax.broadcasted_iota(I32, (tq, tk), 1)

    idx_scale = (IDX_HEADS ** -0.5) * (IDX_DIM ** -0.5)
    iw = iw_ref[...][:, 8:8 + IDX_HEADS] * idx_scale
    iq = jnp.concatenate([iq_lo_ref[...], iq_hi_ref[...]], axis=1)

    def score_tile(kt, _):
        start = pl.multiple_of(kt * tk, tk)
        ik = ik_ref[pl.ds(start, tk), :][:, :IDX_DIM]
        acc = jnp.zeros((tq, tk), F32)
        for h in range(IDX_HEADS):
            a = iq[:, h * IDX_DIM:(h + 1) * IDX_DIM]
            rel = lax.dot_general(a, ik, (((1,), (1,)), ((), ())), preferred_element_type=F32)
            acc = acc + jnp.maximum(rel, 0.0) * iw[:, h:h + 1]
        key_ref[kt] = jnp.where(kt * tk + col_l < adm_end, acc, -jnp.inf)
        return 0

    lax.fori_loop(0, n_tiles, score_tile, 0)

    def float_of(code_u):
        code = code_u ^ INT_MIN
        return pltpu.bitcast(jnp.where(code >= 0, code, code ^ 0x7FFFFFFF), F32)

    def count_ge(cand):
        cnt_ref[...] = jnp.zeros(cnt_ref.shape, I32)

        def body(kt, _):
            for r in range(tq // 128):
                rs = slice(r * 128, (r + 1) * 128)
                part = cnt_ref[rs, :]
                for c in range(tk // 128):
                    part = part + jnp.where(key_ref[kt, rs, c * 128:(c + 1) * 128] >= cand[rs], 1, 0)
                cnt_ref[rs, :] = part
            return 0
        lax.fori_loop(0, n_tiles, body, 0)
        return jnp.sum(cnt_ref[...], axis=1, keepdims=True)

    def search_cond(carry):
        i, _, _, pending = carry
        return (i < 32) & (pending > 0)

    def refine(i, t_u, hit):
        cand_u = t_u | lax.shift_left(jnp.int32(1), 31 - i)
        cnt = count_ge(float_of(cand_u))
        return jnp.where(cnt >= n_sel, cand_u, t_u), jnp.where(cnt == n_sel, 1, hit)

    def search_step(carry):
        i, t_u, hit, _ = carry
        t_u, hit = refine(i, t_u, hit)
        return i + 1, t_u, hit, jnp.sum(1 - hit)

    zeros = jnp.zeros((tq, 1), I32)
    lead = 12
    t_u, hit = lax.fori_loop(0, lead, lambda i, c: refine(i, *c), (zeros, zeros))
    _, t_u, _, _ = lax.while_loop(search_cond, search_step, (jnp.int32(lead), t_u, hit, jnp.sum(1 - hit)))
    thr = jnp.where(t_u == 0, jnp.finfo(F32).min, float_of(t_u))

    for j in range(DSA_KV_HEADS):
        for g in range(DSA_GROUP):
            hd = j * DSA_GROUP + g
            qg_ref[j, g * tq:(g + 1) * tq, :] = bq_ref[:, hd * HEAD_DIM:(hd + 1) * HEAD_DIM]
    _softmax_init(m_ref, l_ref, acc_ref)

    def attn_tile(kt, _):
        start = pl.multiple_of(kt * tk, tk)
        sel = key_ref[kt] >= thr
        for j in range(DSA_KV_HEADS):
            hs = slice(j * HEAD_DIM, (j + 1) * HEAD_DIM)
            k = bk_ref[pl.ds(start, tk), hs]
            v = bv_ref[pl.ds(start, tk), hs]
            s = lax.dot_general(qg_ref[j], k, (((1,), (1,)), ((), ())), preferred_element_type=F32)
            s = jnp.where(sel[None], s.reshape(DSA_GROUP, tq, tk), NEG_BIG).reshape(DSA_GROUP * tq, tk)
            _softmax_tile(s, v, m_ref, l_ref, acc_ref, j)
        return 0

    lax.fori_loop(0, n_tiles, attn_tile, 0)

    for j in range(DSA_KV_HEADS):
        o = _softmax_result(l_ref, acc_ref, j)
        for g in range(DSA_GROUP):
            hd = j * DSA_GROUP + g
            o_ref[:, hd * HEAD_DIM:(hd + 1) * HEAD_DIM] = o[g * tq:(g + 1) * tq].astype(o_ref.dtype)


def _dsa_attention(pbf, pg, s, tq, tk):
    n_sel = min(TOPK_MAX, s // 4)
    kvw = DSA_KV_HEADS * HEAD_DIM
    rows = DSA_GROUP * tq
    iqw = IDX_HEADS * IDX_DIM // 2
    return pl.pallas_call(
        functools.partial(_dsa_kernel, tq=tq, tk=tk, n_sel=n_sel),
        grid=(s // tq,),
        in_specs=[
            pl.BlockSpec((tq, MIX_B), lambda b: (b, COL_BQ // MIX_B)),
            pl.BlockSpec((tq, iqw), lambda b: (b, COL_IQ // iqw)),
            pl.BlockSpec((tq, iqw), lambda b: (b, COL_IQ // iqw + 1)),
            pl.BlockSpec((tq, 128), lambda b: (b, COL_SMALL // 128)),
            pl.BlockSpec((s, 128), lambda b: (0, COL_IK // 128)),
            pl.BlockSpec((s, kvw), lambda b: (0, COL_BK // kvw)),
            pl.BlockSpec((s, kvw), lambda b: (0, COL_BV // kvw)),
        ],
        out_specs=pl.BlockSpec((tq, MIX_B), lambda b: (b, 0)),
        out_shape=jax.ShapeDtypeStruct((s, MIX_B), BF16),
        scratch_shapes=[
            pltpu.VMEM((s // tk, tq, tk), F32),
            pltpu.VMEM((tq, 128), I32),
            pltpu.VMEM((DSA_KV_HEADS, rows, HEAD_DIM), BF16),
            pltpu.VMEM((DSA_KV_HEADS, rows, 128), F32),
            pltpu.VMEM((DSA_KV_HEADS, rows, 128), F32),
            pltpu.VMEM((DSA_KV_HEADS, rows, HEAD_DIM), F32),
        ],
        compiler_params=_cparams(("arbitrary",), 56),
        name="dsa_attention",
    )(pbf, pbf, pbf, pg, pbf, pbf, pbf)


def _layer_norm(z, g, b):
    mu = jnp.mean(z, axis=-1, keepdims=True)
    zc = z - mu
    var = jnp.mean(zc * zc, axis=-1, keepdims=True)
    return zc * lax.rsqrt(var + LN_EPS) * g + b


def _merge_kernel(a_ref, b_ref, ga_ref, gb_ref, x_ref, wa_ref, wb_ref, wo_ref, g_ref, beta_ref,
                  wr_ref, br_ref, h_ref, eidx_ref, gate_ref, pos_ref, cnt_ref, carry_ref, *, alpha):
    i = pl.program_id(0)

    @pl.when(i == 0)
    def _():
        carry_ref[...] = jnp.zeros_like(carry_ref)

    ma = jnp.dot(a_ref[...], wa_ref[...], preferred_element_type=F32)
    mb = jnp.dot(b_ref[...], wb_ref[...], preferred_element_type=F32)
    merged = jax.nn.sigmoid(ga_ref[...]) * ma + jax.nn.sigmoid(gb_ref[...]) * mb
    y = jnp.dot(merged.astype(BF16), wo_ref[...], preferred_element_type=F32)
    h = _layer_norm(alpha * x_ref[...] + y, g_ref[...], beta_ref[...])
    h_ref[...] = h

    wr = wr_ref[...]
    h_hi = h.astype(BF16)
    h_lo = (h - h_hi.astype(F32)).astype(BF16)
    wr_hi = wr.astype(BF16)
    wr_lo = (wr - wr_hi.astype(F32)).astype(BF16)
    logits = (jnp.dot(h_hi, wr_hi, preferred_element_type=F32) + jnp.dot(h_lo, wr_hi, preferred_element_type=F32)
              + jnp.dot(h_hi, wr_lo, preferred_element_type=F32) + br_ref[...])
    tm, ne = logits.shape
    lane = lax.broadcasted_iota(I32, (tm, ne), 1)
    lane_k = lax.broadcasted_iota(I32, (tm, TOP_K), 1)
    work = logits
    vals, sels = [], []
    eidx = jnp.zeros((tm, TOP_K), I32)
    onehot = jnp.zeros((tm, ne), F32)
    for k in range(TOP_K):
        mv = jnp.max(work, axis=1, keepdims=True)
        idx = jnp.min(jnp.where(work == mv, lane, ne), axis=1, keepdims=True)
        sel = lane == idx
        vals.append(mv)
        sels.append(sel)
        eidx = jnp.where(lane_k == k, idx, eidx)
        onehot = onehot + jnp.where(sel, 1.0, 0.0)
        work = jnp.where(sel, -jnp.inf, work)
    exps = [jnp.exp(v - vals[0]) for v in vals]
    denom = exps[0] + exps[1] + exps[2] + exps[3]
    gates = jnp.zeros((tm, TOP_K), F32)
    for k in range(TOP_K):
        gates = jnp.where(lane_k == k, exps[k] / denom, gates)

    r_i = lax.broadcasted_iota(I32, (tm, tm), 0)
    c_i = lax.broadcasted_iota(I32, (tm, tm), 1)
    lower = jnp.where(c_i < r_i, 1.0, 0.0).astype(BF16)
    rank = carry_ref[...] + jnp.dot(lower, onehot.astype(BF16), preferred_element_type=F32)
    pos = jnp.zeros((tm, TOP_K), I32)
    for k in range(TOP_K):
        pk = jnp.sum(jnp.where(sels[k], rank, 0.0), axis=1, keepdims=True).astype(I32)
        pos = jnp.where(lane_k == k, pk, pos)
    carry_ref[...] = carry_ref[...] + jnp.sum(onehot, axis=0, keepdims=True)

    eidx_ref[...] = eidx
    gate_ref[...] = gates
    pos_ref[...] = pos
    cnt_ref[...] = carry_ref[...].astype(I32)


def _merge(a_out, b_out, pg, x2d, wa, wb, wo, ln_g, ln_b, w_router, b_router, alpha):
    s, d = x2d.shape
    tm = min(256, s)
    full = lambda shape: pl.BlockSpec(shape, lambda i: (0,) * len(shape))
    return pl.pallas_call(
        functools.partial(_merge_kernel, alpha=alpha),
        grid=(s // tm,),
        in_specs=[
            pl.BlockSpec((tm, MIX_A), lambda i: (i, 0)),
            pl.BlockSpec((tm, MIX_B), lambda i: (i, 0)),
            pl.BlockSpec((tm, d), lambda i: (i, COL_GA // d)),
            pl.BlockSpec((tm, d), lambda i: (i, COL_GB // d)),
            pl.BlockSpec((tm, d), lambda i: (i, 0)),
            full((MIX_A, d)), full((MIX_B, d)), full((d, d)),
            full((1, d)), full((1, d)), full((d, N_EXPERTS)), full((1, N_EXPERTS)),
        ],
        out_specs=[
            pl.BlockSpec((tm, d), lambda i: (i, 0)),
            pl.BlockSpec((tm, TOP_K), lambda i: (i, 0)),
            pl.BlockSpec((tm, TOP_K), lambda i: (i, 0)),
            pl.BlockSpec((tm, TOP_K), lambda i: (i, 0)),
            full((1, N_EXPERTS)),
        ],
        out_shape=[
            jax.ShapeDtypeStruct((s, d), F32),
            jax.ShapeDtypeStruct((s, TOP_K), I32),
            jax.ShapeDtypeStruct((s, TOP_K), F32),
            jax.ShapeDtypeStruct((s, TOP_K), I32),
            jax.ShapeDtypeStruct((1, N_EXPERTS), I32),
        ],
        scratch_shapes=[pltpu.VMEM((1, N_EXPERTS), F32)],
        compiler_params=_cparams(("arbitrary",), 56),
        name="merge_ln_router",
    )(a_out, b_out, pg, pg, x2d, wa, wb, wo, ln_g, ln_b, w_router, b_router)


def _dispatch_kernel(dest_ref, gap_start_ref, gap_len_ref, total_ref, h_ref, xs_ref, stage_ref, zero_ref,
                     sem, zsem, *, max_slack):
    i = pl.program_id(0)
    last = pl.num_programs(0) - 1
    slot = lax.rem(i, 2)
    n = dest_ref.shape[0]

    @pl.when(i == 0)
    def _():
        zero_ref[...] = jnp.zeros_like(zero_ref)

        def gap_copy(e, k):
            return pltpu.make_async_copy(zero_ref.at[pl.ds(0, 1)], xs_ref.at[pl.ds(gap_start_ref[e] + k, 1)], zsem)

        n_slots = xs_ref.shape[0]

        def slack_copy(k):
            row = pl.multiple_of(total_ref[0] + k * MOE_CHUNK, MOE_CHUNK)
            return pltpu.make_async_copy(zero_ref, xs_ref.at[pl.ds(row, MOE_CHUNK)], zsem)

        def slack(action):
            for k in range(max_slack):
                @pl.when(total_ref[0] + k * MOE_CHUNK < n_slots)
                def _():
                    action(slack_copy(k))

        def issue(e, _):
            lax.fori_loop(0, gap_len_ref[e], lambda k, c: (gap_copy(e, k).start(), c)[1], 0)
            return 0

        def settle(e, _):
            lax.fori_loop(0, gap_len_ref[e], lambda k, c: (gap_copy(e, k).wait(), c)[1], 0)
            return 0

        lax.fori_loop(0, N_EXPERTS, issue, 0)
        slack(lambda cp: cp.start())
        lax.fori_loop(0, N_EXPERTS, settle, 0)
        slack(lambda cp: cp.wait())

    def drain(sl):
        pltpu.make_async_copy(xs_ref.at[pl.ds(0, n)], xs_ref.at[pl.ds(0, n)], sem.at[sl]).wait()

    @pl.when(i >= 2)
    def _():
        drain(slot)

    stage_ref[slot] = h_ref[...]

    def scatter(sl):
        def start(g, _):
            base = pl.multiple_of(g * ROW_TILE, ROW_TILE)
            for u in range(ROW_TILE * TOP_K):
                r = u // TOP_K
                pltpu.make_async_copy(stage_ref.at[sl, pl.ds(base + r, 1)],
                                      xs_ref.at[pl.ds(dest_ref[g * (ROW_TILE * TOP_K) + u], 1)],
                                      sem.at[sl]).start(priority=u % 2)
            return 0
        lax.fori_loop(0, n // (ROW_TILE * TOP_K), start, 0)

    for parity in range(2):
        @pl.when(slot == parity)
        def _():
            scatter(parity)

    @pl.when(i == last)
    def _():
        drain(slot)

        @pl.when(i >= 1)
        def _():
            drain(1 - slot)


def _dispatch(h, dest_flat, gap_start, gap_len, total, n_slots):
    s, d = h.shape
    tm = min(128, s)
    smem = pl.BlockSpec(memory_space=pltpu.SMEM)
    max_slack = (n_slots - s * TOP_K + MOE_CHUNK - 1) // MOE_CHUNK
    return pl.pallas_call(
        functools.partial(_dispatch_kernel, max_slack=max_slack),
        grid=(s // tm,),
        in_specs=[
            pl.BlockSpec((tm * TOP_K,), lambda i: (i,), memory_space=pltpu.SMEM),
            smem, smem, smem,
            pl.BlockSpec((tm, d), lambda i: (i, 0)),
        ],
        out_specs=pl.BlockSpec(memory_space=pl.ANY),
        out_shape=jax.ShapeDtypeStruct((n_slots, d), h.dtype),
        scratch_shapes=[pltpu.VMEM((2, tm, d), h.dtype), pltpu.VMEM((MOE_CHUNK, d), h.dtype),
                        pltpu.SemaphoreType.DMA((2,)), pltpu.SemaphoreType.DMA(())],
        compiler_params=_cparams(("arbitrary",), 32),
        name="moe_dispatch",
    )(dest_flat, gap_start, gap_len, total, h)


ROW_DMA_PRIORITY = 1


def _expert_rows_pipeline(n_chunks, in_copy, out_copy, compute):
    @pl.when(n_chunks > 0)
    def _():
        in_copy(0, 0).start(priority=ROW_DMA_PRIORITY)

        def body(c, _):
            slot = lax.rem(c, 2)
            in_copy(c, slot).wait()

            @pl.when(c + 1 < n_chunks)
            def _():
                in_copy(c + 1, 1 - slot).start(priority=ROW_DMA_PRIORITY)

            @pl.when(c >= 2)
            def _():
                out_copy(c - 2, slot).wait()

            compute(slot, c)
            out_copy(c, slot).start(priority=ROW_DMA_PRIORITY)
            return 0

        lax.fori_loop(0, n_chunks, body, 0)

        @pl.when(n_chunks >= 2)
        def _():
            out_copy(n_chunks - 2, lax.rem(n_chunks, 2)).wait()

        out_copy(n_chunks - 1, lax.rem(n_chunks - 1, 2)).wait()


GATE_UP_BAND = 256
DOWN_BAND = 512


def _weight_band_stream(w_hbm, w_stage, sem_w, e, band, n_bands):
    def band_copy(idx, b):
        cols = pl.ds(b * band, band)
        return pltpu.make_async_copy(w_hbm.at[idx, :, cols], w_stage.at[:, cols], sem_w.at[b])

    def start_bands(idx, bands):
        for b in bands:
            band_copy(idx, b).start(priority=b % 2)

    def await_bands(bands):
        for b in bands:
            band_copy(e, b).wait()

    def refill_bands(bands):
        @pl.when(e + 1 < N_EXPERTS)
        def _():
            start_bands(e + 1, bands)

    @pl.when(e == 0)
    def _():
        start_bands(0, range(n_bands))

    return await_bands, refill_bands


def _chunk_rows(start_ref, e, c, n_slots):
    del n_slots
    return pl.ds(pl.multiple_of(start_ref[e] + c * MOE_CHUNK, MOE_ALIGN), MOE_CHUNK)


def _gate_up_kernel(start_ref, cnt_ref, x_hbm, w_hbm, b_ref, act_hbm,
                    w_stage, w_sc, xbuf, obuf, sem_in, sem_out, sem_w):
    e = pl.program_id(0)
    n_slots = x_hbm.shape[0]
    n_chunks = (cnt_ref[e] + MOE_CHUNK - 1) // MOE_CHUNK
    d_ff = obuf.shape[2]

    cw = GATE_UP_BAND
    n_bands = 2 * d_ff // cw
    has_weights = e < N_EXPERTS
    await_bands, refill = _weight_band_stream(w_hbm, w_stage, sem_w, e, cw, n_bands)

    def in_copy(c, slot):
        return pltpu.make_async_copy(x_hbm.at[_chunk_rows(start_ref, e, c, n_slots)], xbuf.at[slot], sem_in.at[slot])

    def out_copy(c, slot):
        return pltpu.make_async_copy(obuf.at[slot], act_hbm.at[_chunk_rows(start_ref, e, c, n_slots)],
                                     sem_out.at[slot])

    def rows_times_weights(slot, cast_first):
        x = xbuf[slot].astype(BF16)
        for c in range(d_ff // cw):
            gs = slice(c * cw, (c + 1) * cw)
            us = slice(d_ff + c * cw, d_ff + (c + 1) * cw)
            if cast_first:
                bands = (c, d_ff // cw + c)
                await_bands(bands)
                w_sc[:, gs] = w_stage[:, gs].astype(BF16)
                w_sc[:, us] = w_stage[:, us].astype(BF16)
                refill(bands)
            g = jnp.dot(x, w_sc[:, gs], preferred_element_type=F32) + b_ref[:, gs]
            u = jnp.dot(x, w_sc[:, us], preferred_element_type=F32) + b_ref[:, us]
            g = jnp.minimum(g, SWIGLU_LIMIT)
            u = jnp.clip(u, -SWIGLU_LIMIT, SWIGLU_LIMIT)
            obuf[slot, :, gs] = (g * jax.nn.sigmoid(SWIGLU_ALPHA * g) * (u + 1.0)).astype(obuf.dtype)

    def compute(slot, c):
        fresh = (c == 0) & has_weights

        @pl.when(fresh)
        def _():
            rows_times_weights(slot, True)

        @pl.when(jnp.logical_not(fresh))
        def _():
            rows_times_weights(slot, False)

    @pl.when((n_chunks == 0) & has_weights)
    def _():
        await_bands(range(n_bands))
        refill(range(n_bands))

    _expert_rows_pipeline(n_chunks, in_copy, out_copy, compute)


def _down_kernel(start_ref, cnt_ref, a_hbm, w_hbm, bd_ref, y_hbm, w_stage, wd_sc, abuf, obuf, sem_in, sem_out,
                 sem_w):
    e = pl.program_id(0)
    n_slots = y_hbm.shape[0]
    n_chunks = (cnt_ref[e] + MOE_CHUNK - 1) // MOE_CHUNK

    cw = DOWN_BAND
    n_bands = obuf.shape[2] // cw
    has_weights = e < N_EXPERTS
    await_bands, refill = _weight_band_stream(w_hbm, w_stage, sem_w, e, cw, n_bands)

    @pl.when((n_chunks == 0) & has_weights)
    def _():
        await_bands(range(n_bands))
        refill(range(n_bands))

    def in_copy(c, slot):
        return pltpu.make_async_copy(a_hbm.at[_chunk_rows(start_ref, e, c, n_slots)], abuf.at[slot],
                                     sem_in.at[slot])

    def out_copy(c, slot):
        return pltpu.make_async_copy(obuf.at[slot], y_hbm.at[_chunk_rows(start_ref, e, c, n_slots)],
                                     sem_out.at[slot])

    def rows_times_weights(slot, cast_first):
        for b in range(n_bands):
            cs = slice(b * cw, (b + 1) * cw)
            if cast_first:
                await_bands((b,))
                wd_sc[:, cs] = w_stage[:, cs].astype(BF16)
                refill((b,))
            obuf[slot, :, cs] = jnp.dot(abuf[slot], wd_sc[:, cs], preferred_element_type=F32) + bd_ref[:, cs]

    def compute(slot, c):
        fresh = (c == 0) & has_weights

        @pl.when(fresh)
        def _():
            rows_times_weights(slot, True)

        @pl.when(jnp.logical_not(fresh))
        def _():
            rows_times_weights(slot, False)

    _expert_rows_pipeline(n_chunks, in_copy, out_copy, compute)


def _experts(xs, starts, counts, w_gate_up, b_gate_up, w_down, b_down):
    n_slots, d = xs.shape
    any_spec = pl.BlockSpec(memory_space=pl.ANY)
    dma_sems = pltpu.SemaphoreType.DMA((2,))
    bgu = b_gate_up.reshape(N_EXPERTS, 1, 2 * D_FF)
    n_groups = starts.shape[0]
    wi = lambda e: jnp.minimum(e, N_EXPERTS - 1)
    act = pl.pallas_call(
        _gate_up_kernel,
        grid_spec=pltpu.PrefetchScalarGridSpec(
            num_scalar_prefetch=2,
            grid=(n_groups,),
            in_specs=[
                any_spec,
                any_spec,
                pl.BlockSpec((None, 1, 2 * D_FF), lambda e, st, ct: (wi(e), 0, 0)),
            ],
            out_specs=any_spec,
            scratch_shapes=[pltpu.VMEM((d, 2 * D_FF), F32), pltpu.VMEM((d, 2 * D_FF), BF16),
                            pltpu.VMEM((2, MOE_CHUNK, d), xs.dtype), pltpu.VMEM((2, MOE_CHUNK, D_FF), BF16),
                            dma_sems, dma_sems, pltpu.SemaphoreType.DMA((2 * D_FF // GATE_UP_BAND,))],
        ),
        out_shape=jax.ShapeDtypeStruct((n_slots, D_FF), BF16),
        compiler_params=_cparams(("arbitrary",), 60),
        name="moe_gate_up",
    )(starts, counts, xs, w_gate_up, bgu)
    bd = b_down.reshape(N_EXPERTS, 1, d)
    ys = pl.pallas_call(
        _down_kernel,
        grid_spec=pltpu.PrefetchScalarGridSpec(
            num_scalar_prefetch=2,
            grid=(n_groups,),
            in_specs=[
                any_spec,
                any_spec,
                pl.BlockSpec((None, 1, d), lambda e, st, ct: (wi(e), 0, 0)),
            ],
            out_specs=any_spec,
            scratch_shapes=[pltpu.VMEM((D_FF, d), F32), pltpu.VMEM((D_FF, d), BF16),
                            pltpu.VMEM((2, MOE_CHUNK, D_FF), BF16), pltpu.VMEM((2, MOE_CHUNK, d), F32),
                            dma_sems, dma_sems, pltpu.SemaphoreType.DMA((d // DOWN_BAND,))],
        ),
        out_shape=jax.ShapeDtypeStruct((n_slots, d), F32),
        compiler_params=_cparams(("arbitrary",), 56),
        name="moe_down",
    )(starts, counts, act, w_down, bd)
    return ys


def _combine_kernel(dest_ref, dest_next_ref, gate_ref, h_ref, g_ref, beta_ref, ys_ref, o_ref, buf_ref, sem,
                    *, alpha):
    i = pl.program_id(0)
    last = pl.num_programs(0) - 1
    slot = lax.rem(i, 2)
    n = dest_ref.shape[0]

    def gather(dref, sl):
        def start(g, _):
            base = pl.multiple_of(g * ROW_TILE, ROW_TILE)
            for u in range(ROW_TILE * TOP_K):
                r, k = divmod(u, TOP_K)
                pltpu.make_async_copy(ys_ref.at[pl.ds(dref[g * (ROW_TILE * TOP_K) + u], 1)],
                                      buf_ref.at[sl, k, pl.ds(base + r, 1)],
                                      sem.at[sl]).start(priority=u % 2)
            return 0
        lax.fori_loop(0, n // (ROW_TILE * TOP_K), start, 0)

    for parity in range(2):
        @pl.when((i == 0) & (slot == parity))
        def _():
            gather(dest_ref, parity)

        @pl.when((i < last) & (slot == parity))
        def _():
            gather(dest_next_ref, 1 - parity)

    pltpu.make_async_copy(buf_ref.at[slot], buf_ref.at[slot], sem.at[slot]).wait()
    gates = gate_ref[...]
    y = gates[:, 0:1] * buf_ref[slot, 0]
    for k in range(1, TOP_K):
        y = y + gates[:, k:k + 1] * buf_ref[slot, k]
    o_ref[...] = _layer_norm(alpha * h_ref[...] + y, g_ref[...], beta_ref[...])


def _combine(ys, dest_flat, gates, h, ln_g, ln_b, alpha):
    s, d = h.shape
    tm = min(128, s)
    n_steps = s // tm
    return pl.pallas_call(
        functools.partial(_combine_kernel, alpha=alpha),
        grid=(n_steps,),
        in_specs=[
            pl.BlockSpec((tm * TOP_K,), lambda i: (i,), memory_space=pltpu.SMEM),
            pl.BlockSpec((tm * TOP_K,), lambda i: (jnp.minimum(i + 1, n_steps - 1),), memory_space=pltpu.SMEM),
            pl.BlockSpec((tm, TOP_K), lambda i: (i, 0)),
            pl.BlockSpec((tm, d), lambda i: (i, 0)),
            pl.BlockSpec((1, d), lambda i: (0, 0)),
            pl.BlockSpec((1, d), lambda i: (0, 0)),
            pl.BlockSpec(memory_space=pl.ANY),
        ],
        out_specs=pl.BlockSpec((tm, d), lambda i: (i, 0)),
        out_shape=jax.ShapeDtypeStruct((s, d), F32),
        scratch_shapes=[pltpu.VMEM((2, TOP_K, tm, d), F32), pltpu.SemaphoreType.DMA((2,))],
        compiler_params=_cparams(("arbitrary",), 32),
        name="moe_combine",
    )(dest_flat, dest_flat, gates, h, ln_g, ln_b, ys)


def _layer(x2d, w_in, b_forget, w_branch_a, w_branch_b, w_out, ln1_g, ln1_b, w_router, b_router,
           w_gate_up, b_gate_up, w_down, b_down, ln2_g, ln2_b, tables):
    s, d = x2d.shape
    alpha = (2.0 * DEPTH) ** 0.25
    scale = HEAD_DIM ** -0.5 * LOG2E

    w_bf, w_g = _prepare_w_in(w_in.T, scale)
    pbf, pg = _project(x2d, w_bf, tables, w_g)

    fox_tq = min(512, s)
    fox_tk = min(512, s)
    af_t = pg[:, COL_SMALL:COL_SMALL + FOX_HEADS].T
    c_t = _forget_cumsum(af_t, b_forget)
    c_tiles = c_t.reshape(FOX_HEADS, s // fox_tk, 1, fox_tk)
    a_out = _fox_attention(pbf, c_tiles, s, fox_tq, fox_tk, 4)

    b_out = _dsa_attention(pbf, pg, s, min(256, s), min(512, s))

    h, eidx, gates, pos, counts = _merge(
        a_out, b_out, pg, x2d, w_branch_a.astype(BF16), w_branch_b.astype(BF16), w_out.astype(BF16),
        ln1_g.reshape(1, d), ln1_b.reshape(1, d), w_router, b_router.reshape(1, N_EXPERTS), alpha)

    counts = counts.reshape(N_EXPERTS).astype(I32)
    aligned = (counts + MOE_ALIGN - 1) // MOE_ALIGN * MOE_ALIGN
    ends = jnp.cumsum(aligned).astype(I32)
    starts = ends - aligned
    bound = s * TOP_K + N_EXPERTS * MOE_ALIGN + 2 * MOE_CHUNK
    n_slots = (bound + MOE_CHUNK - 1) // MOE_CHUNK * MOE_CHUNK
    experts = jnp.arange(N_EXPERTS, dtype=I32)
    start_of = jnp.sum(jnp.where(eidx[..., None] == experts, starts, 0), axis=-1)
    dest = (start_of + pos).reshape(s * TOP_K).astype(I32)
    total = (ends[-1:] + MOE_CHUNK - 1) // MOE_CHUNK * MOE_CHUNK
    gap_len = aligned - counts + jnp.where(experts == N_EXPERTS - 1, total[0] - ends[-1], 0)
    group_starts = jnp.concatenate([starts, total])
    group_counts = jnp.concatenate([counts + jnp.where(experts == N_EXPERTS - 1, gap_len, 0), n_slots - total])

    xs = _dispatch(h, dest, starts + counts, gap_len, total, n_slots)
    ys = _experts(xs, group_starts, group_counts, w_gate_up, b_gate_up, w_down, b_down)
    return _combine(ys, dest, gates, h, ln2_g.reshape(1, d), ln2_b.reshape(1, d), alpha)


def kernel(x, w_in, b_forget, w_branch_a, w_branch_b, w_out, ln1_g, ln1_b, w_router, b_router,
           w_gate_up, b_gate_up, w_down, b_down, ln2_g, ln2_b):
    bsz, s, d = x.shape
    tables = _rope_tables(s)
    outs = []
    for bi in range(bsz):
        xb = x[bi]
        for l in range(DEPTH):
            xb = _layer(xb, w_in[l], b_forget[l], w_branch_a[l], w_branch_b[l], w_out[l], ln1_g[l], ln1_b[l],
                        w_router[l], b_router[l], w_gate_up[l], b_gate_up[l], w_down[l], b_down[l],
                        ln2_g[l], ln2_b[l], tables)
        outs.append(xb)
    return outs[0][None] if bsz == 1 else jnp.stack(outs)
```

```python
import functools

import numpy as np
import jax
import jax.numpy as jnp
from jax import lax
from jax.experimental import pallas as pl
from jax.experimental.pallas import tpu as pltpu

F32 = jnp.float32
BF16 = jnp.bfloat16
I32 = jnp.int32

D_MODEL = 2048
DEPTH = 1
CHUNK = 64
HEAD_DIM = 128
FOX_HEADS = 8
DSA_HEADS = 8
DSA_KV_HEADS = 2
DSA_GROUP = DSA_HEADS // DSA_KV_HEADS
IDX_HEADS = 16
IDX_DIM = 64
TOPK_MAX = 256
ROPE_THETA = 500000.0
ROT_FRACTION_DEN = 4
MIX_A = FOX_HEADS * HEAD_DIM
MIX_B = DSA_HEADS * HEAD_DIM
N_EXPERTS = 32
TOP_K = 4
D_FF = D_MODEL
SWIGLU_ALPHA = 1.702
SWIGLU_LIMIT = 7.0
LN_EPS = 1e-5

MIB = 1024 * 1024
NEG_BIG = -1e30
LOG2E = 1.4426950408889634
NT_DIMS = (((1,), (1,)), ((), ()))
INT_MIN = -(2 ** 31)

COL_AQ = 0
COL_AK = 1024
COL_AV = 2048
COL_BQ = 3072
COL_BK = 4096
COL_BV = 4352
COL_IQ = 4608
COL_IK = 5632
N_PBF = 5888
PROJ_TN = 256
ROPE_TABLE_W = 128
ROPE_HEAD_TILES = (12, 13, 14, 15, 16)
ROPE_IDX_TILES = (18, 19, 20, 21, 22)
COL_GA = 0
COL_GB = 2048
COL_SMALL = 4096
N_PG = 4224
PG_TN = 384

MOE_CHUNK = 256
MOE_ALIGN = 16
ROW_TILE = 8


def _cparams(dims, vmem_mib):
    return pltpu.CompilerParams(dimension_semantics=dims, vmem_limit_bytes=vmem_mib * MIB)


IN_SIZES = (MIX_A, MIX_A, MIX_A, FOX_HEADS, MIX_B, DSA_KV_HEADS * HEAD_DIM, DSA_KV_HEADS * HEAD_DIM,
            IDX_HEADS * IDX_DIM, IDX_DIM, IDX_HEADS, D_MODEL, D_MODEL)
IN_OFF = dict(zip(("aq", "ak", "av", "af", "bq", "bk", "bv", "iq", "ik", "iw", "ga", "gb", "end"),
                  np.concatenate([[0], np.cumsum(IN_SIZES)]).tolist()))


def _prep_kernel(w_ref, wbf_ref, wg_ref, *, scale):
    o = IN_OFF
    width = w_ref.shape[1]

    def rows(a, b, mult=None):
        v = w_ref[a:b, :]
        return v if mult is None else v * mult

    wbf_ref[COL_AQ:COL_AK, :] = rows(o["aq"], o["ak"], scale).astype(BF16)
    wbf_ref[COL_AK:COL_BQ, :] = rows(o["ak"], o["af"]).astype(BF16)
    wbf_ref[COL_BQ:COL_BK, :] = rows(o["bq"], o["bk"], scale).astype(BF16)
    wbf_ref[COL_BK:COL_IK, :] = rows(o["bk"], o["ik"]).astype(BF16)
    wbf_ref[COL_IK:N_PBF, :] = jnp.concatenate(
        [rows(o["ik"], o["iw"]), jnp.zeros((N_PBF - COL_IK - IDX_DIM, width), F32)], axis=0).astype(BF16)
    wg_ref[COL_GA:COL_SMALL, :] = rows(o["ga"], o["end"]).astype(BF16)
    wg_ref[COL_SMALL:N_PG, :] = jnp.concatenate(
        [rows(o["af"], o["bq"]), rows(o["iw"], o["ga"]),
         jnp.zeros((N_PG - COL_SMALL - FOX_HEADS - IDX_HEADS, width), F32)], axis=0).astype(BF16)


def _prepare_w_in(w_in_t, scale):
    n_in, d = w_in_t.shape
    tc = 256
    return pl.pallas_call(
        functools.partial(_prep_kernel, scale=scale),
        grid=(d // tc,),
        in_specs=[pl.BlockSpec((n_in, tc), lambda i: (0, i))],
        out_specs=[pl.BlockSpec((N_PBF, tc), lambda i: (0, i)), pl.BlockSpec((N_PG, tc), lambda i: (0, i))],
        out_shape=[jax.ShapeDtypeStruct((N_PBF, d), BF16), jax.ShapeDtypeStruct((N_PG, d), BF16)],
        compiler_params=_cparams(("arbitrary",), 48),
        name="w_in_prep",
    )(w_in_t)


def _tile_in(j, tiles):
    cond = j == tiles[0]
    for t in tiles[1:]:
        cond = cond | (j == t)
    return cond


def _proj_rope_kernel(x_ref, w_ref, tab_ref, o_ref, xb_ref):
    j = pl.program_id(1)

    @pl.when(j == 0)
    def _():
        xb_ref[...] = x_ref[...].astype(BF16)

    acc = lax.dot_general(xb_ref[...], w_ref[...], NT_DIMS, preferred_element_type=F32)
    tn = acc.shape[1]
    is_head = _tile_in(j, ROPE_HEAD_TILES)
    is_idx = _tile_in(j, ROPE_IDX_TILES)

    def rope(shift):
        reps = tn // ROPE_TABLE_W
        c = jnp.tile(tab_ref[0, 0], (1, reps))
        s_prev = jnp.tile(tab_ref[0, 1], (1, reps))
        s_next = jnp.tile(tab_ref[0, 2], (1, reps))
        out = acc * c + pltpu.roll(acc, shift, 1) * s_prev + pltpu.roll(acc, tn - shift, 1) * s_next
        o_ref[...] = out.astype(o_ref.dtype)

    @pl.when(is_head)
    def _():
        rope(HEAD_DIM // ROT_FRACTION_DEN // 2)

    @pl.when(is_idx)
    def _():
        rope(IDX_DIM // ROT_FRACTION_DEN // 2)

    @pl.when(jnp.logical_not(is_head | is_idx))
    def _():
        o_ref[...] = acc.astype(o_ref.dtype)


def _proj_plain_kernel(x_ref, w_ref, o_ref, xb_ref):
    j = pl.program_id(1)

    @pl.when(j == 0)
    def _():
        xb_ref[...] = x_ref[...].astype(BF16)

    o_ref[...] = lax.dot_general(xb_ref[...], w_ref[...], NT_DIMS, preferred_element_type=F32).astype(o_ref.dtype)


def _rope_group(j):
    return jnp.where(_tile_in(j, ROPE_IDX_TILES), 1, 0)


def _project(x2d, w_bf, tables, w_g):
    s, d = x2d.shape
    tm = min(1024, s)
    pbf = pl.pallas_call(
        _proj_rope_kernel,
        grid=(s // tm, N_PBF // PROJ_TN),
        in_specs=[
            pl.BlockSpec((tm, d), lambda i, j: (i, 0)),
            pl.BlockSpec((PROJ_TN, d), lambda i, j: (j, 0)),
            pl.BlockSpec((1, 3, tm, ROPE_TABLE_W), lambda i, j: (_rope_group(j), 0, i, 0)),
        ],
        out_specs=pl.BlockSpec((tm, PROJ_TN), lambda i, j: (i, j)),
        out_shape=jax.ShapeDtypeStruct((s, N_PBF), BF16),
        scratch_shapes=[pltpu.VMEM((tm, d), BF16)],
        compiler_params=_cparams(("arbitrary", "arbitrary"), 48),
        name="proj_bf16",
    )(x2d, w_bf, tables)
    pg = pl.pallas_call(
        _proj_plain_kernel,
        grid=(s // tm, N_PG // PG_TN),
        in_specs=[
            pl.BlockSpec((tm, d), lambda i, j: (i, 0)),
            pl.BlockSpec((PG_TN, d), lambda i, j: (j, 0)),
        ],
        out_specs=pl.BlockSpec((tm, PG_TN), lambda i, j: (i, j)),
        out_shape=jax.ShapeDtypeStruct((s, N_PG), F32),
        scratch_shapes=[pltpu.VMEM((tm, d), BF16)],
        compiler_params=_cparams(("arbitrary", "arbitrary"), 48),
        name="proj_f32",
    )(x2d, w_g)
    return pbf, pg


def _rope_tables(s):
    pos = jnp.arange(s, dtype=F32)

    def one(period):
        rot = period // ROT_FRACTION_DEN
        half = rot // 2
        inv = jnp.power(ROPE_THETA, -jnp.arange(0, rot, 2, dtype=F32) / rot)
        ang = pos[:, None] * inv[None, :]
        cos, sin = jnp.cos(ang), jnp.sin(ang)
        zero = jnp.zeros((s, period - rot), F32)
        c = jnp.concatenate([cos, cos, jnp.ones((s, period - rot), F32)], axis=1)
        s_prev = jnp.concatenate([jnp.zeros((s, half), F32), sin, zero], axis=1)
        s_next = jnp.concatenate([-sin, jnp.zeros((s, half), F32), zero], axis=1)
        reps = ROPE_TABLE_W // period
        return jnp.stack([jnp.tile(c, (1, reps)), jnp.tile(s_prev, (1, reps)), jnp.tile(s_next, (1, reps))])

    return jnp.stack([one(HEAD_DIM), one(IDX_DIM)])


def _cumsum_kernel(af_ref, bf_ref, c_ref, carry_ref):
    i = pl.program_id(0)

    @pl.when(i == 0)
    def _():
        carry_ref[...] = jnp.zeros_like(carry_ref)

    z = af_ref[...] + bf_ref[...]
    logf = jnp.minimum(z, 0.0) - jnp.log1p(jnp.exp(-jnp.abs(z)))
    t = z.shape[1]
    row = lax.broadcasted_iota(I32, (t, t), 0)
    col = lax.broadcasted_iota(I32, (t, t), 1)
    upper = (row <= col).astype(F32)
    c = jnp.dot(logf, upper, preferred_element_type=F32, precision=lax.Precision.HIGHEST) + carry_ref[...]
    c_ref[...] = c * LOG2E
    carry_ref[...] = c[:, t - 1:t]


def _forget_cumsum(af_t, b_forget):
    h, s = af_t.shape
    t = min(512, s)
    return pl.pallas_call(
        _cumsum_kernel,
        grid=(s // t,),
        in_specs=[pl.BlockSpec((h, t), lambda i: (0, i)), pl.BlockSpec((h, 1), lambda i: (0, 0))],
        out_specs=pl.BlockSpec((h, t), lambda i: (0, i)),
        out_shape=jax.ShapeDtypeStruct((h, s), F32),
        scratch_shapes=[pltpu.VMEM((h, 1), F32)],
        compiler_params=_cparams(("arbitrary",), 32),
        name="forget_cumsum",
    )(af_t, b_forget.reshape(h, 1).astype(F32))


def _softmax_init(m_ref, l_ref, acc_ref):
    m_ref[...] = jnp.full(m_ref.shape, NEG_BIG, F32)
    l_ref[...] = jnp.zeros(l_ref.shape, F32)
    acc_ref[...] = jnp.zeros(acc_ref.shape, F32)


def _softmax_tile(s, v, m_ref, l_ref, acc_ref, i):
    reps = s.shape[1] // 128
    m_old = m_ref[i]
    m_new = jnp.maximum(m_old, jnp.max(s, axis=1, keepdims=True))
    p = jnp.exp2(s - jnp.tile(m_new, (1, reps)))
    alpha = jnp.exp2(m_old - m_new)
    psum = p[:, :128]
    for c in range(1, reps):
        psum = psum + p[:, c * 128:(c + 1) * 128]
    l_ref[i] = alpha * l_ref[i] + psum
    acc_ref[i] = alpha * acc_ref[i] + jnp.dot(p.astype(BF16), v, preferred_element_type=F32)
    m_ref[i] = m_new


def _softmax_result(l_ref, acc_ref, i):
    return acc_ref[i] / jnp.sum(l_ref[i], axis=1, keepdims=True)


def _fox_kernel(q_ref, k_ref, v_ref, c_ref, o_ref, m_ref, l_ref, acc_ref, *, tq, tk, nh):
    qi = pl.program_id(1)
    _softmax_init(m_ref, l_ref, acc_ref)

    def tile(kt, masked):
        start = pl.multiple_of(kt * tk, tk)
        if masked:
            row = qi * tq + lax.broadcasted_iota(I32, (tq, tk), 0)
            col = kt * tk + lax.broadcasted_iota(I32, (tq, tk), 1)
            causal = col <= row
        for h in range(nh):
            hs = slice(h * HEAD_DIM, (h + 1) * HEAD_DIM)
            k = k_ref[pl.ds(start, tk), hs]
            v = v_ref[pl.ds(start, tk), hs]
            s = lax.dot_general(q_ref[:, hs], k, (((1,), (1,)), ((), ())), preferred_element_type=F32)
            s = s - c_ref[h, kt]
            if masked:
                s = jnp.where(causal, s, NEG_BIG)
            _softmax_tile(s, v, m_ref, l_ref, acc_ref, h)

    n_full = (qi * tq) // tk

    def body(kt, _):
        tile(kt, False)
        return 0

    lax.fori_loop(0, n_full, body, 0)
    for t in range((tq + tk - 1) // tk):
        tile(n_full + t, True)
    for h in range(nh):
        o_ref[:, h * HEAD_DIM:(h + 1) * HEAD_DIM] = _softmax_result(l_ref, acc_ref, h).astype(o_ref.dtype)


def _fox_attention(pbf, c_tiles, s, tq, tk, nh):
    cb = nh * HEAD_DIM
    return pl.pallas_call(
        functools.partial(_fox_kernel, tq=tq, tk=tk, nh=nh),
        grid=(FOX_HEADS // nh, s // tq),
        in_specs=[
            pl.BlockSpec((tq, cb), lambda g, i: (i, COL_AQ // cb + g)),
            pl.BlockSpec((s, cb), lambda g, i: (0, COL_AK // cb + g)),
            pl.BlockSpec((s, cb), lambda g, i: (0, COL_AV // cb + g)),
            pl.BlockSpec((nh, s // tk, 1, tk), lambda g, i: (g, 0, 0, 0)),
        ],
        out_specs=pl.BlockSpec((tq, cb), lambda g, i: (i, g)),
        out_shape=jax.ShapeDtypeStruct((s, MIX_A), BF16),
        scratch_shapes=[
            pltpu.VMEM((nh, tq, 128), F32),
            pltpu.VMEM((nh, tq, 128), F32),
            pltpu.VMEM((nh, tq, HEAD_DIM), F32),
        ],
        compiler_params=_cparams(("arbitrary", "arbitrary"), 56),
        name="fox_attention",
    )(pbf, pbf, pbf, c_tiles)


def _dsa_kernel(bq_ref, iq_lo_ref, iq_hi_ref, iw_ref, ik_ref, bk_ref, bv_ref, o_ref,
                key_ref, cnt_ref, qg_ref, m_ref, l_ref, acc_ref, *, tq, tk, n_sel):
    b = pl.program_id(0)
    n_tiles = (b * tq) // tk + 1
    row_g = b * tq + lax.broadcasted_iota(I32, (tq, tk), 0)
    adm_end = (row_g // CHUNK + 1) * CHUNK
    col_l = lax.broadcasted_iota(I32, (tq, tk), 1)

    idx_scale = (IDX_HEADS ** -0.5) * (IDX_DIM ** -0.5)
    iw = iw_ref[...][:, 8:8 + IDX_HEADS] * idx_scale
    iq = jnp.concatenate([iq_lo_ref[...], iq_hi_ref[...]], axis=1)

    def score_tile(kt, _):
        start = pl.multiple_of(kt * tk, tk)
        ik = ik_ref[pl.ds(start, tk), :][:, :IDX_DIM]
        acc = jnp.zeros((tq, tk), F32)
        for h in range(IDX_HEADS):
            a = iq[:, h * IDX_DIM:(h + 1) * IDX_DIM]
            rel = lax.dot_general(a, ik, (((1,), (1,)), ((), ())), preferred_element_type=F32)
            acc = acc + jnp.maximum(rel, 0.0) * iw[:, h:h + 1]
        key_ref[kt] = jnp.where(kt * tk + col_l < adm_end, acc, -jnp.inf)
        return 0

    lax.fori_loop(0, n_tiles, score_tile, 0)

    def float_of(code_u):
        code = code_u ^ INT_MIN
        return pltpu.bitcast(jnp.where(code >= 0, code, code ^ 0x7FFFFFFF), F32)

    def count(above, cand):
        cnt_ref[...] = jnp.zeros(cnt_ref.shape, I32)

        def body(kt, _):
            for r in range(tq // 128):
                rs = slice(r * 128, (r + 1) * 128)
                part = cnt_ref[rs, :]
                for c in range(tk // 128):
                    part = part + jnp.where(above(key_ref[kt, rs, c * 128:(c + 1) * 128], cand[rs]), 1, 0)
                cnt_ref[rs, :] = part
            return 0
        lax.fori_loop(0, n_tiles, body, 0)
        return jnp.sum(cnt_ref[...], axis=1, keepdims=True)

    def count_ge(cand):
        return count(lambda a, b: a >= b, cand)

    def search_cond(carry):
        i, _, _, pending = carry
        return (i < 32) & (pending > 0)

    def refine(i, t_u, hit):
        cand_u = t_u | lax.shift_left(jnp.int32(1), 31 - i)
        cnt = count_ge(float_of(cand_u))
        return jnp.where(cnt >= n_sel, cand_u, t_u), jnp.where(cnt == n_sel, 1, hit)

    def search_step(carry):
        i, t_u, hit, _ = carry
        t_u, hit = refine(i, t_u, hit)
        return i + 1, t_u, hit, jnp.sum(1 - hit)

    zeros = jnp.zeros((tq, 1), I32)
    lead = 12
    t_u, hit = lax.fori_loop(0, lead, lambda i, c: refine(i, *c), (zeros, zeros))
    _, t_u, _, _ = lax.while_loop(search_cond, search_step, (jnp.int32(lead), t_u, hit, jnp.sum(1 - hit)))
    thr = jnp.where(t_u == 0, jnp.finfo(F32).min, float_of(t_u))
    ties_wanted = (n_sel - count(lambda a, b: a > b, thr)).astype(F32)

    for j in range(DSA_KV_HEADS):
        for g in range(DSA_GROUP):
            hd = j * DSA_GROUP + g
            qg_ref[j, g * tq:(g + 1) * tq, :] = bq_ref[:, hd * HEAD_DIM:(hd + 1) * HEAD_DIM]
    _softmax_init(m_ref, l_ref, acc_ref)

    r_i = lax.broadcasted_iota(I32, (tk, tk), 0)
    c_i = lax.broadcasted_iota(I32, (tk, tk), 1)
    tri = jnp.where(r_i <= c_i, 1.0, 0.0).astype(BF16)

    def attn_tile(kt, ties_seen):
        start = pl.multiple_of(kt * tk, tk)
        key = key_ref[kt]
        tie = key == thr
        rank = ties_seen + jnp.dot(jnp.where(tie, 1.0, 0.0).astype(BF16), tri, preferred_element_type=F32)
        sel = (key > thr) | (tie & (rank <= ties_wanted))
        ties_seen = rank[:, tk - 1:tk]
        for j in range(DSA_KV_HEADS):
            hs = slice(j * HEAD_DIM, (j + 1) * HEAD_DIM)
            k = bk_ref[pl.ds(start, tk), hs]
            v = bv_ref[pl.ds(start, tk), hs]
            s = lax.dot_general(qg_ref[j], k, (((1,), (1,)), ((), ())), preferred_element_type=F32)
            s = jnp.where(sel[None], s.reshape(DSA_GROUP, tq, tk), NEG_BIG).reshape(DSA_GROUP * tq, tk)
            _softmax_tile(s, v, m_ref, l_ref, acc_ref, j)
        return ties_seen

    lax.fori_loop(0, n_tiles, attn_tile, jnp.zeros((tq, 1), F32))

    for j in range(DSA_KV_HEADS):
        o = _softmax_result(l_ref, acc_ref, j)
        for g in range(DSA_GROUP):
            hd = j * DSA_GROUP + g
            o_ref[:, hd * HEAD_DIM:(hd + 1) * HEAD_DIM] = o[g * tq:(g + 1) * tq].astype(o_ref.dtype)


def _dsa_attention(pbf, pg, s, tq, tk):
    n_sel = min(TOPK_MAX, s // 4)
    kvw = DSA_KV_HEADS * HEAD_DIM
    rows = DSA_GROUP * tq
    iqw = IDX_HEADS * IDX_DIM // 2
    return pl.pallas_call(
        functools.partial(_dsa_kernel, tq=tq, tk=tk, n_sel=n_sel),
        grid=(s // tq,),
        in_specs=[
            pl.BlockSpec((tq, MIX_B), lambda b: (b, COL_BQ // MIX_B)),
            pl.BlockSpec((tq, iqw), lambda b: (b, COL_IQ // iqw)),
            pl.BlockSpec((tq, iqw), lambda b: (b, COL_IQ // iqw + 1)),
            pl.BlockSpec((tq, 128), lambda b: (b, COL_SMALL // 128)),
            pl.BlockSpec((s, 128), lambda b: (0, COL_IK // 128)),
            pl.BlockSpec((s, kvw), lambda b: (0, COL_BK // kvw)),
            pl.BlockSpec((s, kvw), lambda b: (0, COL_BV // kvw)),
        ],
        out_specs=pl.BlockSpec((tq, MIX_B), lambda b: (b, 0)),
        out_shape=jax.ShapeDtypeStruct((s, MIX_B), BF16),
        scratch_shapes=[
            pltpu.VMEM((s // tk, tq, tk), F32),
            pltpu.VMEM((tq, 128), I32),
            pltpu.VMEM((DSA_KV_HEADS, rows, HEAD_DIM), BF16),
            pltpu.VMEM((DSA_KV_HEADS, rows, 128), F32),
            pltpu.VMEM((DSA_KV_HEADS, rows, 128), F32),
            pltpu.VMEM((DSA_KV_HEADS, rows, HEAD_DIM), F32),
        ],
        compiler_params=_cparams(("arbitrary",), 56),
        name="dsa_attention",
    )(pbf, pbf, pbf, pg, pbf, pbf, pbf)


def _layer_norm(z, g, b):
    mu = jnp.mean(z, axis=-1, keepdims=True)
    zc = z - mu
    var = jnp.mean(zc * zc, axis=-1, keepdims=True)
    return zc * lax.rsqrt(var + LN_EPS) * g + b


def _merge_kernel(a_ref, b_ref, ga_ref, gb_ref, x_ref, wa_ref, wb_ref, wo_ref, g_ref, beta_ref,
                  wr_ref, br_ref, h_ref, eidx_ref, gate_ref, pos_ref, cnt_ref, carry_ref, *, alpha):
    i = pl.program_id(0)

    @pl.when(i == 0)
    def _():
        carry_ref[...] = jnp.zeros_like(carry_ref)

    ma = jnp.dot(a_ref[...], wa_ref[...], preferred_element_type=F32)
    mb = jnp.dot(b_ref[...], wb_ref[...], preferred_element_type=F32)
    merged = jax.nn.sigmoid(ga_ref[...]) * ma + jax.nn.sigmoid(gb_ref[...]) * mb
    y = jnp.dot(merged.astype(BF16), wo_ref[...], preferred_element_type=F32)
    h = _layer_norm(alpha * x_ref[...] + y, g_ref[...], beta_ref[...])
    h_ref[...] = h

    wr = wr_ref[...]
    h_hi = h.astype(BF16)
    h_lo = (h - h_hi.astype(F32)).astype(BF16)
    wr_hi = wr.astype(BF16)
    wr_lo = (wr - wr_hi.astype(F32)).astype(BF16)
    logits = (jnp.dot(h_hi, wr_hi, preferred_element_type=F32) + jnp.dot(h_lo, wr_hi, preferred_element_type=F32)
              + jnp.dot(h_hi, wr_lo, preferred_element_type=F32) + br_ref[...])
    tm, ne = logits.shape
    lane = lax.broadcasted_iota(I32, (tm, ne), 1)
    lane_k = lax.broadcasted_iota(I32, (tm, TOP_K), 1)
    work = logits
    vals, sels = [], []
    eidx = jnp.zeros((tm, TOP_K), I32)
    onehot = jnp.zeros((tm, ne), F32)
    for k in range(TOP_K):
        mv = jnp.max(work, axis=1, keepdims=True)
        idx = jnp.min(jnp.where(work == mv, lane, ne), axis=1, keepdims=True)
        sel = lane == idx
        vals.append(mv)
        sels.append(sel)
        eidx = jnp.where(lane_k == k, idx, eidx)
        onehot = onehot + jnp.where(sel, 1.0, 0.0)
        work = jnp.where(sel, -jnp.inf, work)
    exps = [jnp.exp(v - vals[0]) for v in vals]
    denom = exps[0] + exps[1] + exps[2] + exps[3]
    gates = jnp.zeros((tm, TOP_K), F32)
    for k in range(TOP_K):
        gates = jnp.where(lane_k == k, exps[k] / denom, gates)

    r_i = lax.broadcasted_iota(I32, (tm, tm), 0)
    c_i = lax.broadcasted_iota(I32, (tm, tm), 1)
    lower = jnp.where(c_i < r_i, 1.0, 0.0).astype(BF16)
    rank = carry_ref[...] + jnp.dot(lower, onehot.astype(BF16), preferred_element_type=F32)
    pos = jnp.zeros((tm, TOP_K), I32)
    for k in range(TOP_K):
        pk = jnp.sum(jnp.where(sels[k], rank, 0.0), axis=1, keepdims=True).astype(I32)
        pos = jnp.where(lane_k == k, pk, pos)
    carry_ref[...] = carry_ref[...] + jnp.sum(onehot, axis=0, keepdims=True)

    eidx_ref[...] = eidx
    gate_ref[...] = gates
    pos_ref[...] = pos
    cnt_ref[...] = carry_ref[...].astype(I32)


def _merge(a_out, b_out, pg, x2d, wa, wb, wo, ln_g, ln_b, w_router, b_router, alpha):
    s, d = x2d.shape
    tm = min(256, s)
    full = lambda shape: pl.BlockSpec(shape, lambda i: (0,) * len(shape))
    return pl.pallas_call(
        functools.partial(_merge_kernel, alpha=alpha),
        grid=(s // tm,),
        in_specs=[
            pl.BlockSpec((tm, MIX_A), lambda i: (i, 0)),
            pl.BlockSpec((tm, MIX_B), lambda i: (i, 0)),
            pl.BlockSpec((tm, d), lambda i: (i, COL_GA // d)),
            pl.BlockSpec((tm, d), lambda i: (i, COL_GB // d)),
            pl.BlockSpec((tm, d), lambda i: (i, 0)),
            full((MIX_A, d)), full((MIX_B, d)), full((d, d)),
            full((1, d)), full((1, d)), full((d, N_EXPERTS)), full((1, N_EXPERTS)),
        ],
        out_specs=[
            pl.BlockSpec((tm, d), lambda i: (i, 0)),
            pl.BlockSpec((tm, TOP_K), lambda i: (i, 0)),
            pl.BlockSpec((tm, TOP_K), lambda i: (i, 0)),
            pl.BlockSpec((tm, TOP_K), lambda i: (i, 0)),
            full((1, N_EXPERTS)),
        ],
        out_shape=[
            jax.ShapeDtypeStruct((s, d), F32),
            jax.ShapeDtypeStruct((s, TOP_K), I32),
            jax.ShapeDtypeStruct((s, TOP_K), F32),
            jax.ShapeDtypeStruct((s, TOP_K), I32),
            jax.ShapeDtypeStruct((1, N_EXPERTS), I32),
        ],
        scratch_shapes=[pltpu.VMEM((1, N_EXPERTS), F32)],
        compiler_params=_cparams(("arbitrary",), 56),
        name="merge_ln_router",
    )(a_out, b_out, pg, pg, x2d, wa, wb, wo, ln_g, ln_b, w_router, b_router)


def _dispatch_kernel(dest_ref, gap_start_ref, gap_len_ref, total_ref, h_ref, xs_ref, stage_ref, zero_ref,
                     sem, zsem, *, max_slack):
    i = pl.program_id(0)
    last = pl.num_programs(0) - 1
    slot = lax.rem(i, 2)
    n = dest_ref.shape[0]

    @pl.when(i == 0)
    def _():
        zero_ref[...] = jnp.zeros_like(zero_ref)

        def gap_copy(e, k):
            return pltpu.make_async_copy(zero_ref.at[pl.ds(0, 1)], xs_ref.at[pl.ds(gap_start_ref[e] + k, 1)], zsem)

        n_slots = xs_ref.shape[0]

        def slack_copy(k):
            row = pl.multiple_of(total_ref[0] + k * MOE_CHUNK, MOE_CHUNK)
            return pltpu.make_async_copy(zero_ref, xs_ref.at[pl.ds(row, MOE_CHUNK)], zsem)

        def slack(action):
            for k in range(max_slack):
                @pl.when(total_ref[0] + k * MOE_CHUNK < n_slots)
                def _():
                    action(slack_copy(k))

        def issue(e, _):
            lax.fori_loop(0, gap_len_ref[e], lambda k, c: (gap_copy(e, k).start(), c)[1], 0)
            return 0

        def settle(e, _):
            lax.fori_loop(0, gap_len_ref[e], lambda k, c: (gap_copy(e, k).wait(), c)[1], 0)
            return 0

        lax.fori_loop(0, N_EXPERTS, issue, 0)
        slack(lambda cp: cp.start())
        lax.fori_loop(0, N_EXPERTS, settle, 0)
        slack(lambda cp: cp.wait())

    def drain(sl):
        pltpu.make_async_copy(xs_ref.at[pl.ds(0, n)], xs_ref.at[pl.ds(0, n)], sem.at[sl]).wait()

    @pl.when(i >= 2)
    def _():
        drain(slot)

    stage_ref[slot] = h_ref[...]

    def scatter(sl):
        def start(g, _):
            base = pl.multiple_of(g * ROW_TILE, ROW_TILE)
            for u in range(ROW_TILE * TOP_K):
                r = u // TOP_K
                pltpu.make_async_copy(stage_ref.at[sl, pl.ds(base + r, 1)],
                                      xs_ref.at[pl.ds(dest_ref[g * (ROW_TILE * TOP_K) + u], 1)],
                                      sem.at[sl]).start(priority=u % 2)
            return 0
        lax.fori_loop(0, n // (ROW_TILE * TOP_K), start, 0)

    for parity in range(2):
        @pl.when(slot == parity)
        def _():
            scatter(parity)

    @pl.when(i == last)
    def _():
        drain(slot)

        @pl.when(i >= 1)
        def _():
            drain(1 - slot)


def _dispatch(h, dest_flat, gap_start, gap_len, total, n_slots):
    s, d = h.shape
    tm = min(128, s)
    smem = pl.BlockSpec(memory_space=pltpu.SMEM)
    max_slack = (n_slots - s * TOP_K + MOE_CHUNK - 1) // MOE_CHUNK
    return pl.pallas_call(
        functools.partial(_dispatch_kernel, max_slack=max_slack),
        grid=(s // tm,),
        in_specs=[
            pl.BlockSpec((tm * TOP_K,), lambda i: (i,), memory_space=pltpu.SMEM),
            smem, smem, smem,
            pl.BlockSpec((tm, d), lambda i: (i, 0)),
        ],
        out_specs=pl.BlockSpec(memory_space=pl.ANY),
        out_shape=jax.ShapeDtypeStruct((n_slots, d), h.dtype),
        scratch_shapes=[pltpu.VMEM((2, tm, d), h.dtype), pltpu.VMEM((MOE_CHUNK, d), h.dtype),
                        pltpu.SemaphoreType.DMA((2,)), pltpu.SemaphoreType.DMA(())],
        compiler_params=_cparams(("arbitrary",), 32),
        name="moe_dispatch",
    )(dest_flat, gap_start, gap_len, total, h)


ROW_DMA_PRIORITY = 1


def _expert_rows_pipeline(n_chunks, in_copy, out_copy, compute):
    @pl.when(n_chunks > 0)
    def _():
        in_copy(0, 0).start(priority=ROW_DMA_PRIORITY)

        def body(c, _):
            slot = lax.rem(c, 2)
            in_copy(c, slot).wait()

            @pl.when(c + 1 < n_chunks)
            def _():
                in_copy(c + 1, 1 - slot).start(priority=ROW_DMA_PRIORITY)

            @pl.when(c >= 2)
            def _():
                out_copy(c - 2, slot).wait()

            compute(slot, c)
            out_copy(c, slot).start(priority=ROW_DMA_PRIORITY)
            return 0

        lax.fori_loop(0, n_chunks, body, 0)

        @pl.when(n_chunks >= 2)
        def _():
            out_copy(n_chunks - 2, lax.rem(n_chunks, 2)).wait()

        out_copy(n_chunks - 1, lax.rem(n_chunks - 1, 2)).wait()


GATE_UP_BAND = 256
DOWN_BAND = 512


def _weight_band_stream(w_hbm, w_stage, sem_w, e, band, n_bands):
    def band_copy(idx, b):
        cols = pl.ds(b * band, band)
        return pltpu.make_async_copy(w_hbm.at[idx, :, cols], w_stage.at[:, cols], sem_w.at[b])

    def start_bands(idx, bands):
        for b in bands:
            band_copy(idx, b).start(priority=b % 2)

    def await_bands(bands):
        for b in bands:
            band_copy(e, b).wait()

    def refill_bands(bands):
        @pl.when(e + 1 < N_EXPERTS)
        def _():
            start_bands(e + 1, bands)

    @pl.when(e == 0)
    def _():
        start_bands(0, range(n_bands))

    return await_bands, refill_bands


def _chunk_rows(start_ref, e, c, n_slots):
    del n_slots
    return pl.ds(pl.multiple_of(start_ref[e] + c * MOE_CHUNK, MOE_ALIGN), MOE_CHUNK)


def _gate_up_kernel(start_ref, cnt_ref, x_hbm, w_hbm, b_ref, act_hbm,
                    w_stage, w_sc, xbuf, obuf, sem_in, sem_out, sem_w):
    e = pl.program_id(0)
    n_slots = x_hbm.shape[0]
    n_chunks = (cnt_ref[e] + MOE_CHUNK - 1) // MOE_CHUNK
    d_ff = obuf.shape[2]

    cw = GATE_UP_BAND
    n_bands = 2 * d_ff // cw
    has_weights = e < N_EXPERTS
    await_bands, refill = _weight_band_stream(w_hbm, w_stage, sem_w, e, cw, n_bands)

    def in_copy(c, slot):
        return pltpu.make_async_copy(x_hbm.at[_chunk_rows(start_ref, e, c, n_slots)], xbuf.at[slot], sem_in.at[slot])

    def out_copy(c, slot):
        return pltpu.make_async_copy(obuf.at[slot], act_hbm.at[_chunk_rows(start_ref, e, c, n_slots)],
                                     sem_out.at[slot])

    def rows_times_weights(slot, cast_first):
        x = xbuf[slot].astype(BF16)
        for c in range(d_ff // cw):
            gs = slice(c * cw, (c + 1) * cw)
            us = slice(d_ff + c * cw, d_ff + (c + 1) * cw)
            if cast_first:
                bands = (c, d_ff // cw + c)
                await_bands(bands)
                w_sc[:, gs] = w_stage[:, gs].astype(BF16)
                w_sc[:, us] = w_stage[:, us].astype(BF16)
                refill(bands)
            g = jnp.dot(x, w_sc[:, gs], preferred_element_type=F32) + b_ref[:, gs]
            u = jnp.dot(x, w_sc[:, us], preferred_element_type=F32) + b_ref[:, us]
            g = jnp.minimum(g, SWIGLU_LIMIT)
            u = jnp.clip(u, -SWIGLU_LIMIT, SWIGLU_LIMIT)
            obuf[slot, :, gs] = (g * jax.nn.sigmoid(SWIGLU_ALPHA * g) * (u + 1.0)).astype(obuf.dtype)

    def compute(slot, c):
        fresh = (c == 0) & has_weights

        @pl.when(fresh)
        def _():
            rows_times_weights(slot, True)

        @pl.when(jnp.logical_not(fresh))
        def _():
            rows_times_weights(slot, False)

    @pl.when((n_chunks == 0) & has_weights)
    def _():
        await_bands(range(n_bands))
        refill(range(n_bands))

    _expert_rows_pipeline(n_chunks, in_copy, out_copy, compute)


def _down_kernel(start_ref, cnt_ref, a_hbm, w_hbm, bd_ref, y_hbm, w_stage, wd_sc, abuf, obuf, sem_in, sem_out,
                 sem_w):
    e = pl.program_id(0)
    n_slots = y_hbm.shape[0]
    n_chunks = (cnt_ref[e] + MOE_CHUNK - 1) // MOE_CHUNK

    cw = DOWN_BAND
    n_bands = obuf.shape[2] // cw
    has_weights = e < N_EXPERTS
    await_bands, refill = _weight_band_stream(w_hbm, w_stage, sem_w, e, cw, n_bands)

    @pl.when((n_chunks == 0) & has_weights)
    def _():
        await_bands(range(n_bands))
        refill(range(n_bands))

    def in_copy(c, slot):
        return pltpu.make_async_copy(a_hbm.at[_chunk_rows(start_ref, e, c, n_slots)], abuf.at[slot],
                                     sem_in.at[slot])

    def out_copy(c, slot):
        return pltpu.make_async_copy(obuf.at[slot], y_hbm.at[_chunk_rows(start_ref, e, c, n_slots)],
                                     sem_out.at[slot])

    def rows_times_weights(slot, cast_first):
        for b in range(n_bands):
            cs = slice(b * cw, (b + 1) * cw)
            if cast_first:
                await_bands((b,))
                wd_sc[:, cs] = w_stage[:, cs].astype(BF16)
                refill((b,))
            obuf[slot, :, cs] = jnp.dot(abuf[slot], wd_sc[:, cs], preferred_element_type=F32) + bd_ref[:, cs]

    def compute(slot, c):
        fresh = (c == 0) & has_weights

        @pl.when(fresh)
        def _():
            rows_times_weights(slot, True)

        @pl.when(jnp.logical_not(fresh))
        def _():
            rows_times_weights(slot, False)

    _expert_rows_pipeline(n_chunks, in_copy, out_copy, compute)


def _experts(xs, starts, counts, w_gate_up, b_gate_up, w_down, b_down):
    n_slots, d = xs.shape
    any_spec = pl.BlockSpec(memory_space=pl.ANY)
    dma_sems = pltpu.SemaphoreType.DMA((2,))
    bgu = b_gate_up.reshape(N_EXPERTS, 1, 2 * D_FF)
    n_groups = starts.shape[0]
    wi = lambda e: jnp.minimum(e, N_EXPERTS - 1)
    act = pl.pallas_call(
        _gate_up_kernel,
        grid_spec=pltpu.PrefetchScalarGridSpec(
            num_scalar_prefetch=2,
            grid=(n_groups,),
            in_specs=[
                any_spec,
                any_spec,
                pl.BlockSpec((None, 1, 2 * D_FF), lambda e, st, ct: (wi(e), 0, 0)),
            ],
            out_specs=any_spec,
            scratch_shapes=[pltpu.VMEM((d, 2 * D_FF), F32), pltpu.VMEM((d, 2 * D_FF), BF16),
                            pltpu.VMEM((2, MOE_CHUNK, d), xs.dtype), pltpu.VMEM((2, MOE_CHUNK, D_FF), BF16),
                            dma_sems, dma_sems, pltpu.SemaphoreType.DMA((2 * D_FF // GATE_UP_BAND,))],
        ),
        out_shape=jax.ShapeDtypeStruct((n_slots, D_FF), BF16),
        compiler_params=_cparams(("arbitrary",), 60),
        name="moe_gate_up",
    )(starts, counts, xs, w_gate_up, bgu)
    bd = b_down.reshape(N_EXPERTS, 1, d)
    ys = pl.pallas_call(
        _down_kernel,
        grid_spec=pltpu.PrefetchScalarGridSpec(
            num_scalar_prefetch=2,
            grid=(n_groups,),
            in_specs=[
                any_spec,
                any_spec,
                pl.BlockSpec((None, 1, d), lambda e, st, ct: (wi(e), 0, 0)),
            ],
            out_specs=any_spec,
            scratch_shapes=[pltpu.VMEM((D_FF, d), F32), pltpu.VMEM((D_FF, d), BF16),
                            pltpu.VMEM((2, MOE_CHUNK, D_FF), BF16), pltpu.VMEM((2, MOE_CHUNK, d), F32),
                            dma_sems, dma_sems, pltpu.SemaphoreType.DMA((d // DOWN_BAND,))],
        ),
        out_shape=jax.ShapeDtypeStruct((n_slots, d), F32),
        compiler_params=_cparams(("arbitrary",), 56),
        name="moe_down",
    )(starts, counts, act, w_down, bd)
    return ys


def _combine_kernel(dest_ref, dest_next_ref, gate_ref, h_ref, g_ref, beta_ref, ys_ref, o_ref, buf_ref, sem,
                    *, alpha):
    i = pl.program_id(0)
    last = pl.num_programs(0) - 1
    slot = lax.rem(i, 2)
    n = dest_ref.shape[0]

    def gather(dref, sl):
        def start(g, _):
            base = pl.multiple_of(g * ROW_TILE, ROW_TILE)
            for u in range(ROW_TILE * TOP_K):
                r, k = divmod(u, TOP_K)
                pltpu.make_async_copy(ys_ref.at[pl.ds(dref[g * (ROW_TILE * TOP_K) + u], 1)],
                                      buf_ref.at[sl, k, pl.ds(base + r, 1)],
                                      sem.at[sl]).start(priority=u % 2)
            return 0
        lax.fori_loop(0, n // (ROW_TILE * TOP_K), start, 0)

    for parity in range(2):
        @pl.when((i == 0) & (slot == parity))
        def _():
            gather(dest_ref, parity)

        @pl.when((i < last) & (slot == parity))
        def _():
            gather(dest_next_ref, 1 - parity)

    pltpu.make_async_copy(buf_ref.at[slot], buf_ref.at[slot], sem.at[slot]).wait()
    gates = gate_ref[...]
    y = gates[:, 0:1] * buf_ref[slot, 0]
    for k in range(1, TOP_K):
        y = y + gates[:, k:k + 1] * buf_ref[slot, k]
    o_ref[...] = _layer_norm(alpha * h_ref[...] + y, g_ref[...], beta_ref[...])


def _combine(ys, dest_flat, gates, h, ln_g, ln_b, alpha):
    s, d = h.shape
    tm = min(128, s)
    n_steps = s // tm
    return pl.pallas_call(
        functools.partial(_combine_kernel, alpha=alpha),
        grid=(n_steps,),
        in_specs=[
            pl.BlockSpec((tm * TOP_K,), lambda i: (i,), memory_space=pltpu.SMEM),
            pl.BlockSpec((tm * TOP_K,), lambda i: (jnp.minimum(i + 1, n_steps - 1),), memory_space=pltpu.SMEM),
            pl.BlockSpec((tm, TOP_K), lambda i: (i, 0)),
            pl.BlockSpec((tm, d), lambda i: (i, 0)),
            pl.BlockSpec((1, d), lambda i: (0, 0)),
            pl.BlockSpec((1, d), lambda i: (0, 0)),
            pl.BlockSpec(memory_space=pl.ANY),
        ],
        out_specs=pl.BlockSpec((tm, d), lambda i: (i, 0)),
        out_shape=jax.ShapeDtypeStruct((s, d), F32),
        scratch_shapes=[pltpu.VMEM((2, TOP_K, tm, d), F32), pltpu.SemaphoreType.DMA((2,))],
        compiler_params=_cparams(("arbitrary",), 32),
        name="moe_combine",
    )(dest_flat, dest_flat, gates, h, ln_g, ln_b, ys)


def _layer(x2d, w_in, b_forget, w_branch_a, w_branch_b, w_out, ln1_g, ln1_b, w_router, b_router,
           w_gate_up, b_gate_up, w_down, b_down, ln2_g, ln2_b, tables):
    s, d = x2d.shape
    alpha = (2.0 * DEPTH) ** 0.25
    scale = HEAD_DIM ** -0.5 * LOG2E

    w_bf, w_g = _prepare_w_in(w_in.T, scale)
    pbf, pg = _project(x2d, w_bf, tables, w_g)

    fox_tq = min(512, s)
    fox_tk = min(512, s)
    af_t = pg[:, COL_SMALL:COL_SMALL + FOX_HEADS].T
    c_t = _forget_cumsum(af_t, b_forget)
    c_tiles = c_t.reshape(FOX_HEADS, s // fox_tk, 1, fox_tk)
    a_out = _fox_attention(pbf, c_tiles, s, fox_tq, fox_tk, 4)

    b_out = _dsa_attention(pbf, pg, s, min(256, s), min(512, s))

    h, eidx, gates, pos, counts = _merge(
        a_out, b_out, pg, x2d, w_branch_a.astype(BF16), w_branch_b.astype(BF16), w_out.astype(BF16),
        ln1_g.reshape(1, d), ln1_b.reshape(1, d), w_router, b_router.reshape(1, N_EXPERTS), alpha)

    counts = counts.reshape(N_EXPERTS).astype(I32)
    aligned = (counts + MOE_ALIGN - 1) // MOE_ALIGN * MOE_ALIGN
    ends = jnp.cumsum(aligned).astype(I32)
    starts = ends - aligned
    bound = s * TOP_K + N_EXPERTS * MOE_ALIGN + 2 * MOE_CHUNK
    n_slots = (bound + MOE_CHUNK - 1) // MOE_CHUNK * MOE_CHUNK
    experts = jnp.arange(N_EXPERTS, dtype=I32)
    start_of = jnp.sum(jnp.where(eidx[..., None] == experts, starts, 0), axis=-1)
    dest = (start_of + pos).reshape(s * TOP_K).astype(I32)
    total = (ends[-1:] + MOE_CHUNK - 1) // MOE_CHUNK * MOE_CHUNK
    gap_len = aligned - counts + jnp.where(experts == N_EXPERTS - 1, total[0] - ends[-1], 0)
    group_starts = jnp.concatenate([starts, total])
    group_counts = jnp.concatenate([counts + jnp.where(experts == N_EXPERTS - 1, gap_len, 0), n_slots - total])

    xs = _dispatch(h, dest, starts + counts, gap_len, total, n_slots)
    ys = _experts(xs, group_starts, group_counts, w_gate_up, b_gate_up, w_down, b_down)
    return _combine(ys, dest, gates, h, ln2_g.reshape(1, d), ln2_b.reshape(1, d), alpha)


def kernel(x, w_in, b_forget, w_branch_a, w_branch_b, w_out, ln1_g, ln1_b, w_router, b_router,
           w_gate_up, b_gate_up, w_down, b_down, ln2_g, ln2_b):
    bsz, s, d = x.shape
    tables = _rope_tables(s)
    outs = []
    for bi in range(bsz):
        xb = x[bi]
        for l in range(DEPTH):
            xb = _layer(xb, w_in[l], b_forget[l], w_branch_a[l], w_branch_b[l], w_out[l], ln1_g[l], ln1_b[l],
                        w_router[l], b_router[l], w_gate_up[l], b_gate_up[l], w_down[l], b_down[l],
                        ln2_g[l], ln2_b[l], tables)
        outs.append(xb)
    return outs[0][None] if bsz == 1 else jnp.stack(outs)
```

```python
import functools

import numpy as np
import jax
import jax.numpy as jnp
from jax import lax
from jax.experimental import pallas as pl
from jax.experimental.pallas import tpu as pltpu

F32 = jnp.float32
BF16 = jnp.bfloat16
I32 = jnp.int32

D_MODEL = 2048
DEPTH = 1
CHUNK = 64
HEAD_DIM = 128
FOX_HEADS = 8
DSA_HEADS = 8
DSA_KV_HEADS = 2
DSA_GROUP = DSA_HEADS // DSA_KV_HEADS
IDX_HEADS = 16
IDX_DIM = 64
TOPK_MAX = 256
ROPE_THETA = 500000.0
ROT_FRACTION_DEN = 4
MIX_A = FOX_HEADS * HEAD_DIM
MIX_B = DSA_HEADS * HEAD_DIM
N_EXPERTS = 32
TOP_K = 4
D_FF = D_MODEL
SWIGLU_ALPHA = 1.702
SWIGLU_LIMIT = 7.0
LN_EPS = 1e-5

MIB = 1024 * 1024
NEG_BIG = -1e30
LOG2E = 1.4426950408889634
NT_DIMS = (((1,), (1,)), ((), ()))
INT_MIN = -(2 ** 31)

COL_AQ = 0
COL_AK = 1024
COL_AV = 2048
COL_BQ = 3072
COL_BK = 4096
COL_BV = 4352
COL_IQ = 4608
COL_IK = 5632
N_PBF = 5888
PROJ_TN = 256
ROPE_TABLE_W = 128
ROPE_HEAD_TILES = (12, 13, 14, 15, 16)
ROPE_IDX_TILES = (18, 19, 20, 21, 22)
COL_GA = 0
COL_GB = 2048
COL_SMALL = 4096
N_PG = 4224
PG_TN = 384

MOE_CHUNK = 256
MOE_ALIGN = 16
ROW_TILE = 8


def _cparams(dims, vmem_mib):
    return pltpu.CompilerParams(dimension_semantics=dims, vmem_limit_bytes=vmem_mib * MIB)


IN_SIZES = (MIX_A, MIX_A, MIX_A, FOX_HEADS, MIX_B, DSA_KV_HEADS * HEAD_DIM, DSA_KV_HEADS * HEAD_DIM,
            IDX_HEADS * IDX_DIM, IDX_DIM, IDX_HEADS, D_MODEL, D_MODEL)
IN_OFF = dict(zip(("aq", "ak", "av", "af", "bq", "bk", "bv", "iq", "ik", "iw", "ga", "gb", "end"),
                  np.concatenate([[0], np.cumsum(IN_SIZES)]).tolist()))


def _prep_kernel(w_ref, wbf_ref, wg_ref, *, scale):
    o = IN_OFF
    width = w_ref.shape[1]

    def rows(a, b, mult=None):
        v = w_ref[a:b, :]
        return v if mult is None else v * mult

    wbf_ref[COL_AQ:COL_AK, :] = rows(o["aq"], o["ak"], scale).astype(BF16)
    wbf_ref[COL_AK:COL_BQ, :] = rows(o["ak"], o["af"]).astype(BF16)
    wbf_ref[COL_BQ:COL_BK, :] = rows(o["bq"], o["bk"], scale).astype(BF16)
    wbf_ref[COL_BK:COL_IK, :] = rows(o["bk"], o["ik"]).astype(BF16)
    wbf_ref[COL_IK:N_PBF, :] = jnp.concatenate(
        [rows(o["ik"], o["iw"]), jnp.zeros((N_PBF - COL_IK - IDX_DIM, width), F32)], axis=0).astype(BF16)
    wg_ref[COL_GA:COL_SMALL, :] = rows(o["ga"], o["end"]).astype(BF16)
    wg_ref[COL_SMALL:N_PG, :] = jnp.concatenate(
        [rows(o["af"], o["bq"]), rows(o["iw"], o["ga"]),
         jnp.zeros((N_PG - COL_SMALL - FOX_HEADS - IDX_HEADS, width), F32)], axis=0).astype(BF16)


def _prepare_w_in(w_in_t, scale):
    n_in, d = w_in_t.shape
    tc = 256
    return pl.pallas_call(
        functools.partial(_prep_kernel, scale=scale),
        grid=(d // tc,),
        in_specs=[pl.BlockSpec((n_in, tc), lambda i: (0, i))],
        out_specs=[pl.BlockSpec((N_PBF, tc), lambda i: (0, i)), pl.BlockSpec((N_PG, tc), lambda i: (0, i))],
        out_shape=[jax.ShapeDtypeStruct((N_PBF, d), BF16), jax.ShapeDtypeStruct((N_PG, d), BF16)],
        compiler_params=_cparams(("arbitrary",), 48),
        name="w_in_prep",
    )(w_in_t)


def _tile_in(j, tiles):
    cond = j == tiles[0]
    for t in tiles[1:]:
        cond = cond | (j == t)
    return cond


def _proj_rope_kernel(x_ref, w_ref, tab_ref, o_ref, xb_ref):
    j = pl.program_id(1)

    @pl.when(j == 0)
    def _():
        xb_ref[...] = x_ref[...].astype(BF16)

    acc = lax.dot_general(xb_ref[...], w_ref[...], NT_DIMS, preferred_element_type=F32)
    tn = acc.shape[1]
    is_head = _tile_in(j, ROPE_HEAD_TILES)
    is_idx = _tile_in(j, ROPE_IDX_TILES)

    def rope(shift):
        reps = tn // ROPE_TABLE_W
        c = jnp.tile(tab_ref[0, 0], (1, reps))
        s_prev = jnp.tile(tab_ref[0, 1], (1, reps))
        s_next = jnp.tile(tab_ref[0, 2], (1, reps))
        out = acc * c + pltpu.roll(acc, shift, 1) * s_prev + pltpu.roll(acc, tn - shift, 1) * s_next
        o_ref[...] = out.astype(o_ref.dtype)

    @pl.when(is_head)
    def _():
        rope(HEAD_DIM // ROT_FRACTION_DEN // 2)

    @pl.when(is_idx)
    def _():
        rope(IDX_DIM // ROT_FRACTION_DEN // 2)

    @pl.when(jnp.logical_not(is_head | is_idx))
    def _():
        o_ref[...] = acc.astype(o_ref.dtype)


def _proj_plain_kernel(x_ref, w_ref, o_ref, xb_ref):
    j = pl.program_id(1)

    @pl.when(j == 0)
    def _():
        xb_ref[...] = x_ref[...].astype(BF16)

    o_ref[...] = lax.dot_general(xb_ref[...], w_ref[...], NT_DIMS, preferred_element_type=F32).astype(o_ref.dtype)


def _rope_group(j):
    return jnp.where(_tile_in(j, ROPE_IDX_TILES), 1, 0)


def _project(x2d, w_bf, tables, w_g):
    s, d = x2d.shape
    tm = min(1024, s)
    pbf = pl.pallas_call(
        _proj_rope_kernel,
        grid=(s // tm, N_PBF // PROJ_TN),
        in_specs=[
            pl.BlockSpec((tm, d), lambda i, j: (i, 0)),
            pl.BlockSpec((PROJ_TN, d), lambda i, j: (j, 0)),
            pl.BlockSpec((1, 3, tm, ROPE_TABLE_W), lambda i, j: (_rope_group(j), 0, i, 0)),
        ],
        out_specs=pl.BlockSpec((tm, PROJ_TN), lambda i, j: (i, j)),
        out_shape=jax.ShapeDtypeStruct((s, N_PBF), BF16),
        scratch_shapes=[pltpu.VMEM((tm, d), BF16)],
        compiler_params=_cparams(("arbitrary", "arbitrary"), 48),
        name="proj_bf16",
    )(x2d, w_bf, tables)
    pg = pl.pallas_call(
        _proj_plain_kernel,
        grid=(s // tm, N_PG // PG_TN),
        in_specs=[
            pl.BlockSpec((tm, d), lambda i, j: (i, 0)),
            pl.BlockSpec((PG_TN, d), lambda i, j: (j, 0)),
        ],
        out_specs=pl.BlockSpec((tm, PG_TN), lambda i, j: (i, j)),
        out_shape=jax.ShapeDtypeStruct((s, N_PG), F32),
        scratch_shapes=[pltpu.VMEM((tm, d), BF16)],
        compiler_params=_cparams(("arbitrary", "arbitrary"), 48),
        name="proj_f32",
    )(x2d, w_g)
    return pbf, pg


def _rope_tables(s):
    pos = jnp.arange(s, dtype=F32)

    def one(period):
        rot = period // ROT_FRACTION_DEN
        half = rot // 2
        inv = jnp.power(ROPE_THETA, -jnp.arange(0, rot, 2, dtype=F32) / rot)
        ang = pos[:, None] * inv[None, :]
        cos, sin = jnp.cos(ang), jnp.sin(ang)
        zero = jnp.zeros((s, period - rot), F32)
        c = jnp.concatenate([cos, cos, jnp.ones((s, period - rot), F32)], axis=1)
        s_prev = jnp.concatenate([jnp.zeros((s, half), F32), sin, zero], axis=1)
        s_next = jnp.concatenate([-sin, jnp.zeros((s, half), F32), zero], axis=1)
        reps = ROPE_TABLE_W // period
        return jnp.stack([jnp.tile(c, (1, reps)), jnp.tile(s_prev, (1, reps)), jnp.tile(s_next, (1, reps))])

    return jnp.stack([one(HEAD_DIM), one(IDX_DIM)])


def _cumsum_kernel(af_ref, bf_ref, c_ref, carry_ref):
    i = pl.program_id(0)

    @pl.when(i == 0)
    def _():
        carry_ref[...] = jnp.zeros_like(carry_ref)

    z = af_ref[...] + bf_ref[...]
    logf = jnp.minimum(z, 0.0) - jnp.log1p(jnp.exp(-jnp.abs(z)))
    t = z.shape[1]
    row = lax.broadcasted_iota(I32, (t, t), 0)
    col = lax.broadcasted_iota(I32, (t, t), 1)
    upper = (row <= col).astype(F32)
    c = jnp.dot(logf, upper, preferred_element_type=F32, precision=lax.Precision.HIGHEST) + carry_ref[...]
    c_ref[...] = c * LOG2E
    carry_ref[...] = c[:, t - 1:t]


def _forget_cumsum(af_t, b_forget):
    h, s = af_t.shape
    t = min(512, s)
    return pl.pallas_call(
        _cumsum_kernel,
        grid=(s // t,),
        in_specs=[pl.BlockSpec((h, t), lambda i: (0, i)), pl.BlockSpec((h, 1), lambda i: (0, 0))],
        out_specs=pl.BlockSpec((h, t), lambda i: (0, i)),
        out_shape=jax.ShapeDtypeStruct((h, s), F32),
        scratch_shapes=[pltpu.VMEM((h, 1), F32)],
        compiler_params=_cparams(("arbitrary",), 32),
        name="forget_cumsum",
    )(af_t, b_forget.reshape(h, 1).astype(F32))


def _softmax_init(m_ref, l_ref, acc_ref):
    m_ref[...] = jnp.full(m_ref.shape, NEG_BIG, F32)
    l_ref[...] = jnp.zeros(l_ref.shape, F32)
    acc_ref[...] = jnp.zeros(acc_ref.shape, F32)


def _softmax_tile(s, v, m_ref, l_ref, acc_ref, i):
    reps = s.shape[1] // 128
    m_old = m_ref[i]
    m_new = jnp.maximum(m_old, jnp.max(s, axis=1, keepdims=True))
    p = jnp.exp2(s - jnp.tile(m_new, (1, reps)))
    alpha = jnp.exp2(m_old - m_new)
    psum = p[:, :128]
    for c in range(1, reps):
        psum = psum + p[:, c * 128:(c + 1) * 128]
    l_ref[i] = alpha * l_ref[i] + psum
    acc_ref[i] = alpha * acc_ref[i] + jnp.dot(p.astype(BF16), v, preferred_element_type=F32)
    m_ref[i] = m_new


def _softmax_result(l_ref, acc_ref, i):
    return acc_ref[i] / jnp.sum(l_ref[i], axis=1, keepdims=True)


def _fox_kernel(q_ref, k_ref, v_ref, c_ref, o_ref, m_ref, l_ref, acc_ref, *, tq, tk, nh):
    qi = pl.program_id(1)
    _softmax_init(m_ref, l_ref, acc_ref)

    def tile(kt, masked):
        start = pl.multiple_of(kt * tk, tk)
        if masked:
            row = qi * tq + lax.broadcasted_iota(I32, (tq, tk), 0)
            col = kt * tk + lax.broadcasted_iota(I32, (tq, tk), 1)
            causal = col <= row
        for h in range(nh):
            hs = slice(h * HEAD_DIM, (h + 1) * HEAD_DIM)
            k = k_ref[pl.ds(start, tk), hs]
            v = v_ref[pl.ds(start, tk), hs]
            s = lax.dot_general(q_ref[:, hs], k, (((1,), (1,)), ((), ())), preferred_element_type=F32)
            s = s - c_ref[h, kt]
            if masked:
                s = jnp.where(causal, s, NEG_BIG)
            _softmax_tile(s, v, m_ref, l_ref, acc_ref, h)

    n_full = (qi * tq) // tk

    def body(kt, _):
        tile(kt, False)
        return 0

    lax.fori_loop(0, n_full, body, 0)
    for t in range((tq + tk - 1) // tk):
        tile(n_full + t, True)
    for h in range(nh):
        o_ref[:, h * HEAD_DIM:(h + 1) * HEAD_DIM] = _softmax_result(l_ref, acc_ref, h).astype(o_ref.dtype)


def _fox_attention(pbf, c_tiles, s, tq, tk, nh):
    cb = nh * HEAD_DIM
    return pl.pallas_call(
        functools.partial(_fox_kernel, tq=tq, tk=tk, nh=nh),
        grid=(FOX_HEADS // nh, s // tq),
        in_specs=[
            pl.BlockSpec((tq, cb), lambda g, i: (i, COL_AQ // cb + g)),
            pl.BlockSpec((s, cb), lambda g, i: (0, COL_AK // cb + g)),
            pl.BlockSpec((s, cb), lambda g, i: (0, COL_AV // cb + g)),
            pl.BlockSpec((nh, s // tk, 1, tk), lambda g, i: (g, 0, 0, 0)),
        ],
        out_specs=pl.BlockSpec((tq, cb), lambda g, i: (i, g)),
        out_shape=jax.ShapeDtypeStruct((s, MIX_A), BF16),
        scratch_shapes=[
            pltpu.VMEM((nh, tq, 128), F32),
            pltpu.VMEM((nh, tq, 128), F32),
            pltpu.VMEM((nh, tq, HEAD_DIM), F32),
        ],
        compiler_params=_cparams(("arbitrary", "arbitrary"), 56),
        name="fox_attention",
    )(pbf, pbf, pbf, c_tiles)


def _dsa_kernel(bq_ref, iq_lo_ref, iq_hi_ref, iw_ref, ik_ref, bk_ref, bv_ref, o_ref,
                key_ref, cnt_ref, qg_ref, m_ref, l_ref, acc_ref, *, tq, tk, n_sel):
    b = pl.program_id(0)
    n_tiles = (b * tq) // tk + 1
    row_g = b * tq + lax.broadcasted_iota(I32, (tq, tk), 0)
    adm_end = (row_g // CHUNK + 1) * CHUNK
    col_l = lax.broadcasted_iota(I32, (tq, tk), 1)

    idx_scale = (IDX_HEADS ** -0.5) * (IDX_DIM ** -0.5)
    iw = iw_ref[...][:, 8:8 + IDX_HEADS] * idx_scale
    iq = jnp.concatenate([iq_lo_ref[...], iq_hi_ref[...]], axis=1)

    def score_tile(kt, _):
        start = pl.multiple_of(kt * tk, tk)
        ik = ik_ref[pl.ds(start, tk), :][:, :IDX_DIM]
        acc = jnp.zeros((tq, tk), F32)
        for h in range(IDX_HEADS):
            a = iq[:, h * IDX_DIM:(h + 1) * IDX_DIM]
            rel = lax.dot_general(a, ik, (((1,), (1,)), ((), ())), preferred_element_type=F32)
            acc = acc + jnp.maximum(rel, 0.0) * iw[:, h:h + 1]
        key_ref[kt] = jnp.where(kt * tk + col_l < adm_end, acc, -jnp.inf)
        return 0

    lax.fori_loop(0, n_tiles, score_tile, 0)

    def float_of(code_u):
        code = code_u ^ INT_MIN
        return pltpu.bitcast(jnp.where(code >= 0, code, code ^ 0x7FFFFFFF), F32)

    def count(above, cand):
        cnt_ref[...] = jnp.zeros(cnt_ref.shape, I32)

        def body(kt, _):
            for r in range(tq // 128):
                rs = slice(r * 128, (r + 1) * 128)
                part = cnt_ref[rs, :]
                for c in range(tk // 128):
                    part = part + jnp.where(above(key_ref[kt, rs, c * 128:(c + 1) * 128], cand[rs]), 1, 0)
                cnt_ref[rs, :] = part
            return 0
        lax.fori_loop(0, n_tiles, body, 0)
        return jnp.sum(cnt_ref[...], axis=1, keepdims=True)

    def count_ge(cand):
        return count(lambda a, b: a >= b, cand)

    def search_cond(carry):
        i, _, _, pending = carry
        return (i < 32) & (pending > 0)

    def refine(i, t_u, hit):
        cand_u = t_u | lax.shift_left(jnp.int32(1), 31 - i)
        cnt = count_ge(float_of(cand_u))
        return jnp.where(cnt >= n_sel, cand_u, t_u), jnp.where(cnt == n_sel, 1, hit)

    def search_step(carry):
        i, t_u, hit, _ = carry
        t_u, hit = refine(i, t_u, hit)
        return i + 1, t_u, hit, jnp.sum(1 - hit)

    zeros = jnp.zeros((tq, 1), I32)
    lead = 12
    t_u, hit = lax.fori_loop(0, lead, lambda i, c: refine(i, *c), (zeros, zeros))
    _, t_u, hit, _ = lax.while_loop(search_cond, search_step, (jnp.int32(lead), t_u, hit, jnp.sum(1 - hit)))
    thr = jnp.where(t_u == 0, jnp.finfo(F32).min, float_of(t_u))
    tied_rows = jnp.sum(jnp.where((hit == 0) & (t_u != 0), 1, 0))

    for j in range(DSA_KV_HEADS):
        for g in range(DSA_GROUP):
            hd = j * DSA_GROUP + g
            qg_ref[j, g * tq:(g + 1) * tq, :] = bq_ref[:, hd * HEAD_DIM:(hd + 1) * HEAD_DIM]
    _softmax_init(m_ref, l_ref, acc_ref)

    def attend(kt, sel):
        start = pl.multiple_of(kt * tk, tk)
        for j in range(DSA_KV_HEADS):
            hs = slice(j * HEAD_DIM, (j + 1) * HEAD_DIM)
            k = bk_ref[pl.ds(start, tk), hs]
            v = bv_ref[pl.ds(start, tk), hs]
            s = lax.dot_general(qg_ref[j], k, (((1,), (1,)), ((), ())), preferred_element_type=F32)
            s = jnp.where(sel[None], s.reshape(DSA_GROUP, tq, tk), NEG_BIG).reshape(DSA_GROUP * tq, tk)
            _softmax_tile(s, v, m_ref, l_ref, acc_ref, j)

    @pl.when(tied_rows == 0)
    def _():
        def attn_tile(kt, _):
            attend(kt, key_ref[kt] >= thr)
            return 0
        lax.fori_loop(0, n_tiles, attn_tile, 0)

    @pl.when(tied_rows > 0)
    def _():
        ties_wanted = (n_sel - count(lambda a, b: a > b, thr)).astype(F32)
        r_i = lax.broadcasted_iota(I32, (tk, tk), 0)
        c_i = lax.broadcasted_iota(I32, (tk, tk), 1)
        tri = jnp.where(r_i <= c_i, 1.0, 0.0).astype(BF16)

        def attn_tile(kt, ties_seen):
            key = key_ref[kt]
            tie = key == thr
            rank = ties_seen + jnp.dot(jnp.where(tie, 1.0, 0.0).astype(BF16), tri, preferred_element_type=F32)
            attend(kt, (key > thr) | (tie & (rank <= ties_wanted)))
            return rank[:, tk - 1:tk]
        lax.fori_loop(0, n_tiles, attn_tile, jnp.zeros((tq, 1), F32))

    for j in range(DSA_KV_HEADS):
        o = _softmax_result(l_ref, acc_ref, j)
        for g in range(DSA_GROUP):
            hd = j * DSA_GROUP + g
            o_ref[:, hd * HEAD_DIM:(hd + 1) * HEAD_DIM] = o[g * tq:(g + 1) * tq].astype(o_ref.dtype)


def _dsa_attention(pbf, pg, s, tq, tk):
    n_sel = min(TOPK_MAX, s // 4)
    kvw = DSA_KV_HEADS * HEAD_DIM
    rows = DSA_GROUP * tq
    iqw = IDX_HEADS * IDX_DIM // 2
    return pl.pallas_call(
        functools.partial(_dsa_kernel, tq=tq, tk=tk, n_sel=n_sel),
        grid=(s // tq,),
        in_specs=[
            pl.BlockSpec((tq, MIX_B), lambda b: (b, COL_BQ // MIX_B)),
            pl.BlockSpec((tq, iqw), lambda b: (b, COL_IQ // iqw)),
            pl.BlockSpec((tq, iqw), lambda b: (b, COL_IQ // iqw + 1)),
            pl.BlockSpec((tq, 128), lambda b: (b, COL_SMALL // 128)),
            pl.BlockSpec((s, 128), lambda b: (0, COL_IK // 128)),
            pl.BlockSpec((s, kvw), lambda b: (0, COL_BK // kvw)),
            pl.BlockSpec((s, kvw), lambda b: (0, COL_BV // kvw)),
        ],
        out_specs=pl.BlockSpec((tq, MIX_B), lambda b: (b, 0)),
        out_shape=jax.ShapeDtypeStruct((s, MIX_B), BF16),
        scratch_shapes=[
            pltpu.VMEM((s // tk, tq, tk), F32),
            pltpu.VMEM((tq, 128), I32),
            pltpu.VMEM((DSA_KV_HEADS, rows, HEAD_DIM), BF16),
            pltpu.VMEM((DSA_KV_HEADS, rows, 128), F32),
            pltpu.VMEM((DSA_KV_HEADS, rows, 128), F32),
            pltpu.VMEM((DSA_KV_HEADS, rows, HEAD_DIM), F32),
        ],
        compiler_params=_cparams(("arbitrary",), 56),
        name="dsa_attention",
    )(pbf, pbf, pbf, pg, pbf, pbf, pbf)


def _layer_norm(z, g, b):
    mu = jnp.mean(z, axis=-1, keepdims=True)
    zc = z - mu
    var = jnp.mean(zc * zc, axis=-1, keepdims=True)
    return zc * lax.rsqrt(var + LN_EPS) * g + b


def _merge_kernel(a_ref, b_ref, ga_ref, gb_ref, x_ref, wa_ref, wb_ref, wo_ref, g_ref, beta_ref,
                  wr_ref, br_ref, h_ref, eidx_ref, gate_ref, pos_ref, cnt_ref, carry_ref, *, alpha):
    i = pl.program_id(0)

    @pl.when(i == 0)
    def _():
        carry_ref[...] = jnp.zeros_like(carry_ref)

    ma = jnp.dot(a_ref[...], wa_ref[...], preferred_element_type=F32)
    mb = jnp.dot(b_ref[...], wb_ref[...], preferred_element_type=F32)
    merged = jax.nn.sigmoid(ga_ref[...]) * ma + jax.nn.sigmoid(gb_ref[...]) * mb
    y = jnp.dot(merged.astype(BF16), wo_ref[...], preferred_element_type=F32)
    h = _layer_norm(alpha * x_ref[...] + y, g_ref[...], beta_ref[...])
    h_ref[...] = h

    wr = wr_ref[...]
    h_hi = h.astype(BF16)
    h_lo = (h - h_hi.astype(F32)).astype(BF16)
    wr_hi = wr.astype(BF16)
    wr_lo = (wr - wr_hi.astype(F32)).astype(BF16)
    logits = (jnp.dot(h_hi, wr_hi, preferred_element_type=F32) + jnp.dot(h_lo, wr_hi, preferred_element_type=F32)
              + jnp.dot(h_hi, wr_lo, preferred_element_type=F32) + br_ref[...])
    tm, ne = logits.shape
    lane = lax.broadcasted_iota(I32, (tm, ne), 1)
    lane_k = lax.broadcasted_iota(I32, (tm, TOP_K), 1)
    work = logits
    vals, sels = [], []
    eidx = jnp.zeros((tm, TOP_K), I32)
    onehot = jnp.zeros((tm, ne), F32)
    for k in range(TOP_K):
        mv = jnp.max(work, axis=1, keepdims=True)
        idx = jnp.min(jnp.where(work == mv, lane, ne), axis=1, keepdims=True)
        sel = lane == idx
        vals.append(mv)
        sels.append(sel)
        eidx = jnp.where(lane_k == k, idx, eidx)
        onehot = onehot + jnp.where(sel, 1.0, 0.0)
        work = jnp.where(sel, -jnp.inf, work)
    exps = [jnp.exp(v - vals[0]) for v in vals]
    denom = exps[0] + exps[1] + exps[2] + exps[3]
    gates = jnp.zeros((tm, TOP_K), F32)
    for k in range(TOP_K):
        gates = jnp.where(lane_k == k, exps[k] / denom, gates)

    r_i = lax.broadcasted_iota(I32, (tm, tm), 0)
    c_i = lax.broadcasted_iota(I32, (tm, tm), 1)
    lower = jnp.where(c_i < r_i, 1.0, 0.0).astype(BF16)
    rank = carry_ref[...] + jnp.dot(lower, onehot.astype(BF16), preferred_element_type=F32)
    pos = jnp.zeros((tm, TOP_K), I32)
    for k in range(TOP_K):
        pk = jnp.sum(jnp.where(sels[k], rank, 0.0), axis=1, keepdims=True).astype(I32)
        pos = jnp.where(lane_k == k, pk, pos)
    carry_ref[...] = carry_ref[...] + jnp.sum(onehot, axis=0, keepdims=True)

    eidx_ref[...] = eidx
    gate_ref[...] = gates
    pos_ref[...] = pos
    cnt_ref[...] = carry_ref[...].astype(I32)


def _merge(a_out, b_out, pg, x2d, wa, wb, wo, ln_g, ln_b, w_router, b_router, alpha):
    s, d = x2d.shape
    tm = min(256, s)
    full = lambda shape: pl.BlockSpec(shape, lambda i: (0,) * len(shape))
    return pl.pallas_call(
        functools.partial(_merge_kernel, alpha=alpha),
        grid=(s // tm,),
        in_specs=[
            pl.BlockSpec((tm, MIX_A), lambda i: (i, 0)),
            pl.BlockSpec((tm, MIX_B), lambda i: (i, 0)),
            pl.BlockSpec((tm, d), lambda i: (i, COL_GA // d)),
            pl.BlockSpec((tm, d), lambda i: (i, COL_GB // d)),
            pl.BlockSpec((tm, d), lambda i: (i, 0)),
            full((MIX_A, d)), full((MIX_B, d)), full((d, d)),
            full((1, d)), full((1, d)), full((d, N_EXPERTS)), full((1, N_EXPERTS)),
        ],
        out_specs=[
            pl.BlockSpec((tm, d), lambda i: (i, 0)),
            pl.BlockSpec((tm, TOP_K), lambda i: (i, 0)),
            pl.BlockSpec((tm, TOP_K), lambda i: (i, 0)),
            pl.BlockSpec((tm, TOP_K), lambda i: (i, 0)),
            full((1, N_EXPERTS)),
        ],
        out_shape=[
            jax.ShapeDtypeStruct((s, d), F32),
            jax.ShapeDtypeStruct((s, TOP_K), I32),
            jax.ShapeDtypeStruct((s, TOP_K), F32),
            jax.ShapeDtypeStruct((s, TOP_K), I32),
            jax.ShapeDtypeStruct((1, N_EXPERTS), I32),
        ],
        scratch_shapes=[pltpu.VMEM((1, N_EXPERTS), F32)],
        compiler_params=_cparams(("arbitrary",), 56),
        name="merge_ln_router",
    )(a_out, b_out, pg, pg, x2d, wa, wb, wo, ln_g, ln_b, w_router, b_router)


def _dispatch_kernel(dest_ref, gap_start_ref, gap_len_ref, total_ref, h_ref, xs_ref, stage_ref, zero_ref,
                     sem, zsem, *, max_slack):
    i = pl.program_id(0)
    last = pl.num_programs(0) - 1
    slot = lax.rem(i, 2)
    n = dest_ref.shape[0]

    @pl.when(i == 0)
    def _():
        zero_ref[...] = jnp.zeros_like(zero_ref)

        def gap_copy(e, k):
            return pltpu.make_async_copy(zero_ref.at[pl.ds(0, 1)], xs_ref.at[pl.ds(gap_start_ref[e] + k, 1)], zsem)

        n_slots = xs_ref.shape[0]

        def slack_copy(k):
            row = pl.multiple_of(total_ref[0] + k * MOE_CHUNK, MOE_CHUNK)
            return pltpu.make_async_copy(zero_ref, xs_ref.at[pl.ds(row, MOE_CHUNK)], zsem)

        def slack(action):
            for k in range(max_slack):
                @pl.when(total_ref[0] + k * MOE_CHUNK < n_slots)
                def _():
                    action(slack_copy(k))

        def issue(e, _):
            lax.fori_loop(0, gap_len_ref[e], lambda k, c: (gap_copy(e, k).start(), c)[1], 0)
            return 0

        def settle(e, _):
            lax.fori_loop(0, gap_len_ref[e], lambda k, c: (gap_copy(e, k).wait(), c)[1], 0)
            return 0

        lax.fori_loop(0, N_EXPERTS, issue, 0)
        slack(lambda cp: cp.start())
        lax.fori_loop(0, N_EXPERTS, settle, 0)
        slack(lambda cp: cp.wait())

    def drain(sl):
        pltpu.make_async_copy(xs_ref.at[pl.ds(0, n)], xs_ref.at[pl.ds(0, n)], sem.at[sl]).wait()

    @pl.when(i >= 2)
    def _():
        drain(slot)

    stage_ref[slot] = h_ref[...]

    def scatter(sl):
        def start(g, _):
            base = pl.multiple_of(g * ROW_TILE, ROW_TILE)
            for u in range(ROW_TILE * TOP_K):
                r = u // TOP_K
                pltpu.make_async_copy(stage_ref.at[sl, pl.ds(base + r, 1)],
                                      xs_ref.at[pl.ds(dest_ref[g * (ROW_TILE * TOP_K) + u], 1)],
                                      sem.at[sl]).start(priority=u % 2)
            return 0
        lax.fori_loop(0, n // (ROW_TILE * TOP_K), start, 0)

    for parity in range(2):
        @pl.when(slot == parity)
        def _():
            scatter(parity)

    @pl.when(i == last)
    def _():
        drain(slot)

        @pl.when(i >= 1)
        def _():
            drain(1 - slot)


def _dispatch(h, dest_flat, gap_start, gap_len, total, n_slots):
    s, d = h.shape
    tm = min(128, s)
    smem = pl.BlockSpec(memory_space=pltpu.SMEM)
    max_slack = (n_slots - s * TOP_K + MOE_CHUNK - 1) // MOE_CHUNK
    return pl.pallas_call(
        functools.partial(_dispatch_kernel, max_slack=max_slack),
        grid=(s // tm,),
        in_specs=[
            pl.BlockSpec((tm * TOP_K,), lambda i: (i,), memory_space=pltpu.SMEM),
            smem, smem, smem,
            pl.BlockSpec((tm, d), lambda i: (i, 0)),
        ],
        out_specs=pl.BlockSpec(memory_space=pl.ANY),
        out_shape=jax.ShapeDtypeStruct((n_slots, d), h.dtype),
        scratch_shapes=[pltpu.VMEM((2, tm, d), h.dtype), pltpu.VMEM((MOE_CHUNK, d), h.dtype),
                        pltpu.SemaphoreType.DMA((2,)), pltpu.SemaphoreType.DMA(())],
        compiler_params=_cparams(("arbitrary",), 32),
        name="moe_dispatch",
    )(dest_flat, gap_start, gap_len, total, h)


ROW_DMA_PRIORITY = 1


def _expert_rows_pipeline(n_chunks, in_copy, out_copy, compute):
    @pl.when(n_chunks > 0)
    def _():
        in_copy(0, 0).start(priority=ROW_DMA_PRIORITY)

        def body(c, _):
            slot = lax.rem(c, 2)
            in_copy(c, slot).wait()

            @pl.when(c + 1 < n_chunks)
            def _():
                in_copy(c + 1, 1 - slot).start(priority=ROW_DMA_PRIORITY)

            @pl.when(c >= 2)
            def _():
                out_copy(c - 2, slot).wait()

            compute(slot, c)
            out_copy(c, slot).start(priority=ROW_DMA_PRIORITY)
            return 0

        lax.fori_loop(0, n_chunks, body, 0)

        @pl.when(n_chunks >= 2)
        def _():
            out_copy(n_chunks - 2, lax.rem(n_chunks, 2)).wait()

        out_copy(n_chunks - 1, lax.rem(n_chunks - 1, 2)).wait()


GATE_UP_BAND = 256
DOWN_BAND = 512


def _weight_band_stream(w_hbm, w_stage, sem_w, e, band, n_bands):
    def band_copy(idx, b):
        cols = pl.ds(b * band, band)
        return pltpu.make_async_copy(w_hbm.at[idx, :, cols], w_stage.at[:, cols], sem_w.at[b])

    def start_bands(idx, bands):
        for b in bands:
            band_copy(idx, b).start(priority=b % 2)

    def await_bands(bands):
        for b in bands:
            band_copy(e, b).wait()

    def refill_bands(bands):
        @pl.when(e + 1 < N_EXPERTS)
        def _():
            start_bands(e + 1, bands)

    @pl.when(e == 0)
    def _():
        start_bands(0, range(n_bands))

    return await_bands, refill_bands


def _chunk_rows(start_ref, e, c, n_slots):
    del n_slots
    return pl.ds(pl.multiple_of(start_ref[e] + c * MOE_CHUNK, MOE_ALIGN), MOE_CHUNK)


def _gate_up_kernel(start_ref, cnt_ref, x_hbm, w_hbm, b_ref, act_hbm,
                    w_stage, w_sc, xbuf, obuf, sem_in, sem_out, sem_w):
    e = pl.program_id(0)
    n_slots = x_hbm.shape[0]
    n_chunks = (cnt_ref[e] + MOE_CHUNK - 1) // MOE_CHUNK
    d_ff = obuf.shape[2]

    cw = GATE_UP_BAND
    n_bands = 2 * d_ff // cw
    has_weights = e < N_EXPERTS
    await_bands, refill = _weight_band_stream(w_hbm, w_stage, sem_w, e, cw, n_bands)

    def in_copy(c, slot):
        return pltpu.make_async_copy(x_hbm.at[_chunk_rows(start_ref, e, c, n_slots)], xbuf.at[slot], sem_in.at[slot])

    def out_copy(c, slot):
        return pltpu.make_async_copy(obuf.at[slot], act_hbm.at[_chunk_rows(start_ref, e, c, n_slots)],
                                     sem_out.at[slot])

    def rows_times_weights(slot, cast_first):
        x = xbuf[slot].astype(BF16)
        for c in range(d_ff // cw):
            gs = slice(c * cw, (c + 1) * cw)
            us = slice(d_ff + c * cw, d_ff + (c + 1) * cw)
            if cast_first:
                bands = (c, d_ff // cw + c)
                await_bands(bands)
                w_sc[:, gs] = w_stage[:, gs].astype(BF16)
                w_sc[:, us] = w_stage[:, us].astype(BF16)
                refill(bands)
            g = jnp.dot(x, w_sc[:, gs], preferred_element_type=F32) + b_ref[:, gs]
            u = jnp.dot(x, w_sc[:, us], preferred_element_type=F32) + b_ref[:, us]
            g = jnp.minimum(g, SWIGLU_LIMIT)
            u = jnp.clip(u, -SWIGLU_LIMIT, SWIGLU_LIMIT)
            obuf[slot, :, gs] = (g * jax.nn.sigmoid(SWIGLU_ALPHA * g) * (u + 1.0)).astype(obuf.dtype)

    def compute(slot, c):
        fresh = (c == 0) & has_weights

        @pl.when(fresh)
        def _():
            rows_times_weights(slot, True)

        @pl.when(jnp.logical_not(fresh))
        def _():
            rows_times_weights(slot, False)

    @pl.when((n_chunks == 0) & has_weights)
    def _():
        await_bands(range(n_bands))
        refill(range(n_bands))

    _expert_rows_pipeline(n_chunks, in_copy, out_copy, compute)


def _down_kernel(start_ref, cnt_ref, a_hbm, w_hbm, bd_ref, y_hbm, w_stage, wd_sc, abuf, obuf, sem_in, sem_out,
                 sem_w):
    e = pl.program_id(0)
    n_slots = y_hbm.shape[0]
    n_chunks = (cnt_ref[e] + MOE_CHUNK - 1) // MOE_CHUNK

    cw = DOWN_BAND
    n_bands = obuf.shape[2] // cw
    has_weights = e < N_EXPERTS
    await_bands, refill = _weight_band_stream(w_hbm, w_stage, sem_w, e, cw, n_bands)

    @pl.when((n_chunks == 0) & has_weights)
    def _():
        await_bands(range(n_bands))
        refill(range(n_bands))

    def in_copy(c, slot):
        return pltpu.make_async_copy(a_hbm.at[_chunk_rows(start_ref, e, c, n_slots)], abuf.at[slot],
                                     sem_in.at[slot])

    def out_copy(c, slot):
        return pltpu.make_async_copy(obuf.at[slot], y_hbm.at[_chunk_rows(start_ref, e, c, n_slots)],
                                     sem_out.at[slot])

    def rows_times_weights(slot, cast_first):
        for b in range(n_bands):
            cs = slice(b * cw, (b + 1) * cw)
            if cast_first:
                await_bands((b,))
                wd_sc[:, cs] = w_stage[:, cs].astype(BF16)
                refill((b,))
            obuf[slot, :, cs] = jnp.dot(abuf[slot], wd_sc[:, cs], preferred_element_type=F32) + bd_ref[:, cs]

    def compute(slot, c):
        fresh = (c == 0) & has_weights

        @pl.when(fresh)
        def _():
            rows_times_weights(slot, True)

        @pl.when(jnp.logical_not(fresh))
        def _():
            rows_times_weights(slot, False)

    _expert_rows_pipeline(n_chunks, in_copy, out_copy, compute)


def _experts(xs, starts, counts, w_gate_up, b_gate_up, w_down, b_down):
    n_slots, d = xs.shape
    any_spec = pl.BlockSpec(memory_space=pl.ANY)
    dma_sems = pltpu.SemaphoreType.DMA((2,))
    bgu = b_gate_up.reshape(N_EXPERTS, 1, 2 * D_FF)
    n_groups = starts.shape[0]
    wi = lambda e: jnp.minimum(e, N_EXPERTS - 1)
    act = pl.pallas_call(
        _gate_up_kernel,
        grid_spec=pltpu.PrefetchScalarGridSpec(
            num_scalar_prefetch=2,
            grid=(n_groups,),
            in_specs=[
                any_spec,
                any_spec,
                pl.BlockSpec((None, 1, 2 * D_FF), lambda e, st, ct: (wi(e), 0, 0)),
            ],
            out_specs=any_spec,
            scratch_shapes=[pltpu.VMEM((d, 2 * D_FF), F32), pltpu.VMEM((d, 2 * D_FF), BF16),
                            pltpu.VMEM((2, MOE_CHUNK, d), xs.dtype), pltpu.VMEM((2, MOE_CHUNK, D_FF), BF16),
                            dma_sems, dma_sems, pltpu.SemaphoreType.DMA((2 * D_FF // GATE_UP_BAND,))],
        ),
        out_shape=jax.ShapeDtypeStruct((n_slots, D_FF), BF16),
        compiler_params=_cparams(("arbitrary",), 60),
        name="moe_gate_up",
    )(starts, counts, xs, w_gate_up, bgu)
    bd = b_down.reshape(N_EXPERTS, 1, d)
    ys = pl.pallas_call(
        _down_kernel,
        grid_spec=pltpu.PrefetchScalarGridSpec(
            num_scalar_prefetch=2,
            grid=(n_groups,),
            in_specs=[
                any_spec,
                any_spec,
                pl.BlockSpec((None, 1, d), lambda e, st, ct: (wi(e), 0, 0)),
            ],
            out_specs=any_spec,
            scratch_shapes=[pltpu.VMEM((D_FF, d), F32), pltpu.VMEM((D_FF, d), BF16),
                            pltpu.VMEM((2, MOE_CHUNK, D_FF), BF16), pltpu.VMEM((2, MOE_CHUNK, d), F32),
                            dma_sems, dma_sems, pltpu.SemaphoreType.DMA((d // DOWN_BAND,))],
        ),
        out_shape=jax.ShapeDtypeStruct((n_slots, d), F32),
        compiler_params=_cparams(("arbitrary",), 56),
        name="moe_down",
    )(starts, counts, act, w_down, bd)
    return ys


def _combine_kernel(dest_ref, dest_next_ref, gate_ref, h_ref, g_ref, beta_ref, ys_ref, o_ref, buf_ref, sem,
                    *, alpha):
    i = pl.program_id(0)
    last = pl.num_programs(0) - 1
    slot = lax.rem(i, 2)
    n = dest_ref.shape[0]

    def gather(dref, sl):
        def start(g, _):
            base = pl.multiple_of(g * ROW_TILE, ROW_TILE)
            for u in range(ROW_TILE * TOP_K):
                r, k = divmod(u, TOP_K)
                pltpu.make_async_copy(ys_ref.at[pl.ds(dref[g * (ROW_TILE * TOP_K) + u], 1)],
                                      buf_ref.at[sl, k, pl.ds(base + r, 1)],
                                      sem.at[sl]).start(priority=u % 2)
            return 0
        lax.fori_loop(0, n // (ROW_TILE * TOP_K), start, 0)

    for parity in range(2):
        @pl.when((i == 0) & (slot == parity))
        def _():
            gather(dest_ref, parity)

        @pl.when((i < last) & (slot == parity))
        def _():
            gather(dest_next_ref, 1 - parity)

    pltpu.make_async_copy(buf_ref.at[slot], buf_ref.at[slot], sem.at[slot]).wait()
    gates = gate_ref[...]
    y = gates[:, 0:1] * buf_ref[slot, 0]
    for k in range(1, TOP_K):
        y = y + gates[:, k:k + 1] * buf_ref[slot, k]
    o_ref[...] = _layer_norm(alpha * h_ref[...] + y, g_ref[...], beta_ref[...])


def _combine(ys, dest_flat, gates, h, ln_g, ln_b, alpha):
    s, d = h.shape
    tm = min(128, s)
    n_steps = s // tm
    return pl.pallas_call(
        functools.partial(_combine_kernel, alpha=alpha),
        grid=(n_steps,),
        in_specs=[
            pl.BlockSpec((tm * TOP_K,), lambda i: (i,), memory_space=pltpu.SMEM),
            pl.BlockSpec((tm * TOP_K,), lambda i: (jnp.minimum(i + 1, n_steps - 1),), memory_space=pltpu.SMEM),
            pl.BlockSpec((tm, TOP_K), lambda i: (i, 0)),
            pl.BlockSpec((tm, d), lambda i: (i, 0)),
            pl.BlockSpec((1, d), lambda i: (0, 0)),
            pl.BlockSpec((1, d), lambda i: (0, 0)),
            pl.BlockSpec(memory_space=pl.ANY),
        ],
        out_specs=pl.BlockSpec((tm, d), lambda i: (i, 0)),
        out_shape=jax.ShapeDtypeStruct((s, d), F32),
        scratch_shapes=[pltpu.VMEM((2, TOP_K, tm, d), F32), pltpu.SemaphoreType.DMA((2,))],
        compiler_params=_cparams(("arbitrary",), 32),
        name="moe_combine",
    )(dest_flat, dest_flat, gates, h, ln_g, ln_b, ys)


def _layer(x2d, w_in, b_forget, w_branch_a, w_branch_b, w_out, ln1_g, ln1_b, w_router, b_router,
           w_gate_up, b_gate_up, w_down, b_down, ln2_g, ln2_b, tables):
    s, d = x2d.shape
    alpha = (2.0 * DEPTH) ** 0.25
    scale = HEAD_DIM ** -0.5 * LOG2E

    w_bf, w_g = _prepare_w_in(w_in.T, scale)
    pbf, pg = _project(x2d, w_bf, tables, w_g)

    fox_tq = min(512, s)
    fox_tk = min(512, s)
    af_t = pg[:, COL_SMALL:COL_SMALL + FOX_HEADS].T
    c_t = _forget_cumsum(af_t, b_forget)
    c_tiles = c_t.reshape(FOX_HEADS, s // fox_tk, 1, fox_tk)
    a_out = _fox_attention(pbf, c_tiles, s, fox_tq, fox_tk, 4)

    b_out = _dsa_attention(pbf, pg, s, min(256, s), min(512, s))

    h, eidx, gates, pos, counts = _merge(
        a_out, b_out, pg, x2d, w_branch_a.astype(BF16), w_branch_b.astype(BF16), w_out.astype(BF16),
        ln1_g.reshape(1, d), ln1_b.reshape(1, d), w_router, b_router.reshape(1, N_EXPERTS), alpha)

    counts = counts.reshape(N_EXPERTS).astype(I32)
    aligned = (counts + MOE_ALIGN - 1) // MOE_ALIGN * MOE_ALIGN
    ends = jnp.cumsum(aligned).astype(I32)
    starts = ends - aligned
    bound = s * TOP_K + N_EXPERTS * MOE_ALIGN + 2 * MOE_CHUNK
    n_slots = (bound + MOE_CHUNK - 1) // MOE_CHUNK * MOE_CHUNK
    experts = jnp.arange(N_EXPERTS, dtype=I32)
    start_of = jnp.sum(jnp.where(eidx[..., None] == experts, starts, 0), axis=-1)
    dest = (start_of + pos).reshape(s * TOP_K).astype(I32)
    total = (ends[-1:] + MOE_CHUNK - 1) // MOE_CHUNK * MOE_CHUNK
    gap_len = aligned - counts + jnp.where(experts == N_EXPERTS - 1, total[0] - ends[-1], 0)
    group_starts = jnp.concatenate([starts, total])
    group_counts = jnp.concatenate([counts + jnp.where(experts == N_EXPERTS - 1, gap_len, 0), n_slots - total])

    xs = _dispatch(h, dest, starts + counts, gap_len, total, n_slots)
    ys = _experts(xs, group_starts, group_counts, w_gate_up, b_gate_up, w_down, b_down)
    return _combine(ys, dest, gates, h, ln2_g.reshape(1, d), ln2_b.reshape(1, d), alpha)


def kernel(x, w_in, b_forget, w_branch_a, w_branch_b, w_out, ln1_g, ln1_b, w_router, b_router,
           w_gate_up, b_gate_up, w_down, b_down, ln2_g, ln2_b):
    bsz, s, d = x.shape
    tables = _rope_tables(s)
    outs = []
    for bi in range(bsz):
        xb = x[bi]
        for l in range(DEPTH):
            xb = _layer(xb, w_in[l], b_forget[l], w_branch_a[l], w_branch_b[l], w_out[l], ln1_g[l], ln1_b[l],
                        w_router[l], b_router[l], w_gate_up[l], b_gate_up[l], w_down[l], b_down[l],
                        ln2_g[l], ln2_b[l], tables)
        outs.append(xb)
    return outs[0][None] if bsz == 1 else jnp.stack(outs)
```

```python
import functools

import numpy as np
import jax
import jax.numpy as jnp
from jax import lax
from jax.experimental import pallas as pl
from jax.experimental.pallas import tpu as pltpu

F32 = jnp.float32
BF16 = jnp.bfloat16
I32 = jnp.int32

D_MODEL = 2048
DEPTH = 1
CHUNK = 64
HEAD_DIM = 128
FOX_HEADS = 8
DSA_HEADS = 8
DSA_KV_HEADS = 2
DSA_GROUP = DSA_HEADS // DSA_KV_HEADS
IDX_HEADS = 16
IDX_DIM = 64
TOPK_MAX = 256
ROPE_THETA = 500000.0
ROT_FRACTION_DEN = 4
MIX_A = FOX_HEADS * HEAD_DIM
MIX_B = DSA_HEADS * HEAD_DIM
N_EXPERTS = 32
TOP_K = 4
D_FF = D_MODEL
SWIGLU_ALPHA = 1.702
SWIGLU_LIMIT = 7.0
LN_EPS = 1e-5

MIB = 1024 * 1024
NEG_BIG = -1e30
LOG2E = 1.4426950408889634
NT_DIMS = (((1,), (1,)), ((), ()))
INT_MIN = -(2 ** 31)

COL_AQ = 0
COL_AK = 1024
COL_AV = 2048
COL_BQ = 3072
COL_BK = 4096
COL_BV = 4352
COL_IQ = 4608
COL_IK = 5632
N_PBF = 5888
PROJ_TN = 256
ROPE_TABLE_W = 128
ROPE_HEAD_TILES = (12, 13, 14, 15, 16)
ROPE_IDX_TILES = (18, 19, 20, 21, 22)
COL_GA = 0
COL_GB = 2048
COL_SMALL = 4096
N_PG = 4224
PG_TN = 384

MOE_CHUNK = 256
MOE_ALIGN = 16
ROW_TILE = 8


V7X_VMEM_MIB = 64
VMEM_LIMIT_MIB = dict(
    w_in_prep=48, proj_bf16=48, proj_f32=48, forget_cumsum=32, fox_attention=56, dsa_attention=56,
    merge_ln_router=56, moe_dispatch=32, moe_gate_up=60, moe_down=56, moe_combine=32)
assert max(VMEM_LIMIT_MIB.values()) < V7X_VMEM_MIB


def _cparams(dims, name):
    return pltpu.CompilerParams(dimension_semantics=dims, vmem_limit_bytes=VMEM_LIMIT_MIB[name] * MIB)


IN_SIZES = (MIX_A, MIX_A, MIX_A, FOX_HEADS, MIX_B, DSA_KV_HEADS * HEAD_DIM, DSA_KV_HEADS * HEAD_DIM,
            IDX_HEADS * IDX_DIM, IDX_DIM, IDX_HEADS, D_MODEL, D_MODEL)
IN_OFF = dict(zip(("aq", "ak", "av", "af", "bq", "bk", "bv", "iq", "ik", "iw", "ga", "gb", "end"),
                  np.concatenate([[0], np.cumsum(IN_SIZES)]).tolist()))


def _prep_kernel(w_ref, wbf_ref, wg_ref, *, scale):
    o = IN_OFF
    width = w_ref.shape[1]

    def rows(a, b, mult=None):
        v = w_ref[a:b, :]
        return v if mult is None else v * mult

    wbf_ref[COL_AQ:COL_AK, :] = rows(o["aq"], o["ak"], scale).astype(BF16)
    wbf_ref[COL_AK:COL_BQ, :] = rows(o["ak"], o["af"]).astype(BF16)
    wbf_ref[COL_BQ:COL_BK, :] = rows(o["bq"], o["bk"], scale).astype(BF16)
    wbf_ref[COL_BK:COL_IK, :] = rows(o["bk"], o["ik"]).astype(BF16)
    wbf_ref[COL_IK:N_PBF, :] = jnp.concatenate(
        [rows(o["ik"], o["iw"]), jnp.zeros((N_PBF - COL_IK - IDX_DIM, width), F32)], axis=0).astype(BF16)
    wg_ref[COL_GA:COL_SMALL, :] = rows(o["ga"], o["end"]).astype(BF16)
    wg_ref[COL_SMALL:N_PG, :] = jnp.concatenate(
        [rows(o["af"], o["bq"]), rows(o["iw"], o["ga"]),
         jnp.zeros((N_PG - COL_SMALL - FOX_HEADS - IDX_HEADS, width), F32)], axis=0).astype(BF16)


def _prepare_w_in(w_in_t, scale):
    n_in, d = w_in_t.shape
    tc = 256
    return pl.pallas_call(
        functools.partial(_prep_kernel, scale=scale),
        grid=(d // tc,),
        in_specs=[pl.BlockSpec((n_in, tc), lambda i: (0, i))],
        out_specs=[pl.BlockSpec((N_PBF, tc), lambda i: (0, i)), pl.BlockSpec((N_PG, tc), lambda i: (0, i))],
        out_shape=[jax.ShapeDtypeStruct((N_PBF, d), BF16), jax.ShapeDtypeStruct((N_PG, d), BF16)],
        compiler_params=_cparams(("arbitrary",), "w_in_prep"),
        name="w_in_prep",
    )(w_in_t)


def _tile_in(j, tiles):
    cond = j == tiles[0]
    for t in tiles[1:]:
        cond = cond | (j == t)
    return cond


def _proj_rope_kernel(x_ref, w_ref, tab_ref, o_ref, xb_ref):
    j = pl.program_id(1)

    @pl.when(j == 0)
    def _():
        xb_ref[...] = x_ref[...].astype(BF16)

    acc = lax.dot_general(xb_ref[...], w_ref[...], NT_DIMS, preferred_element_type=F32)
    tn = acc.shape[1]
    is_head = _tile_in(j, ROPE_HEAD_TILES)
    is_idx = _tile_in(j, ROPE_IDX_TILES)

    def rope(shift):
        reps = tn // ROPE_TABLE_W
        c = jnp.tile(tab_ref[0, 0], (1, reps))
        s_prev = jnp.tile(tab_ref[0, 1], (1, reps))
        s_next = jnp.tile(tab_ref[0, 2], (1, reps))
        out = acc * c + pltpu.roll(acc, shift, 1) * s_prev + pltpu.roll(acc, tn - shift, 1) * s_next
        o_ref[...] = out.astype(o_ref.dtype)

    @pl.when(is_head)
    def _():
        rope(HEAD_DIM // ROT_FRACTION_DEN // 2)

    @pl.when(is_idx)
    def _():
        rope(IDX_DIM // ROT_FRACTION_DEN // 2)

    @pl.when(jnp.logical_not(is_head | is_idx))
    def _():
        o_ref[...] = acc.astype(o_ref.dtype)


def _proj_plain_kernel(x_ref, w_ref, o_ref, xb_ref):
    j = pl.program_id(1)

    @pl.when(j == 0)
    def _():
        xb_ref[...] = x_ref[...].astype(BF16)

    o_ref[...] = lax.dot_general(xb_ref[...], w_ref[...], NT_DIMS, preferred_element_type=F32).astype(o_ref.dtype)


def _rope_group(j):
    return jnp.where(_tile_in(j, ROPE_IDX_TILES), 1, 0)


def _project(x2d, w_bf, tables, w_g):
    s, d = x2d.shape
    tm = min(1024, s)
    pbf = pl.pallas_call(
        _proj_rope_kernel,
        grid=(s // tm, N_PBF // PROJ_TN),
        in_specs=[
            pl.BlockSpec((tm, d), lambda i, j: (i, 0)),
            pl.BlockSpec((PROJ_TN, d), lambda i, j: (j, 0)),
            pl.BlockSpec((1, 3, tm, ROPE_TABLE_W), lambda i, j: (_rope_group(j), 0, i, 0)),
        ],
        out_specs=pl.BlockSpec((tm, PROJ_TN), lambda i, j: (i, j)),
        out_shape=jax.ShapeDtypeStruct((s, N_PBF), BF16),
        scratch_shapes=[pltpu.VMEM((tm, d), BF16)],
        compiler_params=_cparams(("arbitrary", "arbitrary"), "proj_bf16"),
        name="proj_bf16",
    )(x2d, w_bf, tables)
    pg = pl.pallas_call(
        _proj_plain_kernel,
        grid=(s // tm, N_PG // PG_TN),
        in_specs=[
            pl.BlockSpec((tm, d), lambda i, j: (i, 0)),
            pl.BlockSpec((PG_TN, d), lambda i, j: (j, 0)),
        ],
        out_specs=pl.BlockSpec((tm, PG_TN), lambda i, j: (i, j)),
        out_shape=jax.ShapeDtypeStruct((s, N_PG), F32),
        scratch_shapes=[pltpu.VMEM((tm, d), BF16)],
        compiler_params=_cparams(("arbitrary", "arbitrary"), "proj_f32"),
        name="proj_f32",
    )(x2d, w_g)
    return pbf, pg


def _rope_tables(s):
    pos = jnp.arange(s, dtype=F32)

    def one(period):
        rot = period // ROT_FRACTION_DEN
        half = rot // 2
        inv = jnp.power(ROPE_THETA, -jnp.arange(0, rot, 2, dtype=F32) / rot)
        ang = pos[:, None] * inv[None, :]
        cos, sin = jnp.cos(ang), jnp.sin(ang)
        zero = jnp.zeros((s, period - rot), F32)
        c = jnp.concatenate([cos, cos, jnp.ones((s, period - rot), F32)], axis=1)
        s_prev = jnp.concatenate([jnp.zeros((s, half), F32), sin, zero], axis=1)
        s_next = jnp.concatenate([-sin, jnp.zeros((s, half), F32), zero], axis=1)
        reps = ROPE_TABLE_W // period
        return jnp.stack([jnp.tile(c, (1, reps)), jnp.tile(s_prev, (1, reps)), jnp.tile(s_next, (1, reps))])

    return jnp.stack([one(HEAD_DIM), one(IDX_DIM)])


def _cumsum_kernel(af_ref, bf_ref, c_ref, carry_ref):
    i = pl.program_id(0)

    @pl.when(i == 0)
    def _():
        carry_ref[...] = jnp.zeros_like(carry_ref)

    z = af_ref[...] + bf_ref[...]
    logf = jnp.minimum(z, 0.0) - jnp.log1p(jnp.exp(-jnp.abs(z)))
    t = z.shape[1]
    row = lax.broadcasted_iota(I32, (t, t), 0)
    col = lax.broadcasted_iota(I32, (t, t), 1)
    upper = (row <= col).astype(F32)
    c = jnp.dot(logf, upper, preferred_element_type=F32, precision=lax.Precision.HIGHEST) + carry_ref[...]
    c_ref[...] = c * LOG2E
    carry_ref[...] = c[:, t - 1:t]


def _forget_cumsum(af_t, b_forget):
    h, s = af_t.shape
    t = min(512, s)
    return pl.pallas_call(
        _cumsum_kernel,
        grid=(s // t,),
        in_specs=[pl.BlockSpec((h, t), lambda i: (0, i)), pl.BlockSpec((h, 1), lambda i: (0, 0))],
        out_specs=pl.BlockSpec((h, t), lambda i: (0, i)),
        out_shape=jax.ShapeDtypeStruct((h, s), F32),
        scratch_shapes=[pltpu.VMEM((h, 1), F32)],
        compiler_params=_cparams(("arbitrary",), "forget_cumsum"),
        name="forget_cumsum",
    )(af_t, b_forget.reshape(h, 1).astype(F32))


def _softmax_init(m_ref, l_ref, acc_ref):
    m_ref[...] = jnp.full(m_ref.shape, NEG_BIG, F32)
    l_ref[...] = jnp.zeros(l_ref.shape, F32)
    acc_ref[...] = jnp.zeros(acc_ref.shape, F32)


def _softmax_tile(s, v, m_ref, l_ref, acc_ref, i):
    reps = s.shape[1] // 128
    m_old = m_ref[i]
    m_new = jnp.maximum(m_old, jnp.max(s, axis=1, keepdims=True))
    p = jnp.exp2(s - jnp.tile(m_new, (1, reps)))
    alpha = jnp.exp2(m_old - m_new)
    psum = p[:, :128]
    for c in range(1, reps):
        psum = psum + p[:, c * 128:(c + 1) * 128]
    l_ref[i] = alpha * l_ref[i] + psum
    acc_ref[i] = alpha * acc_ref[i] + jnp.dot(p.astype(BF16), v, preferred_element_type=F32)
    m_ref[i] = m_new


def _softmax_result(l_ref, acc_ref, i):
    return acc_ref[i] / jnp.sum(l_ref[i], axis=1, keepdims=True)


def _fox_kernel(q_ref, k_ref, v_ref, c_ref, o_ref, m_ref, l_ref, acc_ref, *, tq, tk, nh):
    qi = pl.program_id(1)
    _softmax_init(m_ref, l_ref, acc_ref)

    def tile(kt, masked):
        start = pl.multiple_of(kt * tk, tk)
        if masked:
            row = qi * tq + lax.broadcasted_iota(I32, (tq, tk), 0)
            col = kt * tk + lax.broadcasted_iota(I32, (tq, tk), 1)
            causal = col <= row
        for h in range(nh):
            hs = slice(h * HEAD_DIM, (h + 1) * HEAD_DIM)
            k = k_ref[pl.ds(start, tk), hs]
            v = v_ref[pl.ds(start, tk), hs]
            s = lax.dot_general(q_ref[:, hs], k, (((1,), (1,)), ((), ())), preferred_element_type=F32)
            s = s - c_ref[h, kt]
            if masked:
                s = jnp.where(causal, s, NEG_BIG)
            _softmax_tile(s, v, m_ref, l_ref, acc_ref, h)

    n_full = (qi * tq) // tk

    def body(kt, _):
        tile(kt, False)
        return 0

    lax.fori_loop(0, n_full, body, 0)
    for t in range((tq + tk - 1) // tk):
        tile(n_full + t, True)
    for h in range(nh):
        o_ref[:, h * HEAD_DIM:(h + 1) * HEAD_DIM] = _softmax_result(l_ref, acc_ref, h).astype(o_ref.dtype)


def _fox_attention(pbf, c_tiles, s, tq, tk, nh):
    cb = nh * HEAD_DIM
    return pl.pallas_call(
        functools.partial(_fox_kernel, tq=tq, tk=tk, nh=nh),
        grid=(FOX_HEADS // nh, s // tq),
        in_specs=[
            pl.BlockSpec((tq, cb), lambda g, i: (i, COL_AQ // cb + g)),
            pl.BlockSpec((s, cb), lambda g, i: (0, COL_AK // cb + g)),
            pl.BlockSpec((s, cb), lambda g, i: (0, COL_AV // cb + g)),
            pl.BlockSpec((nh, s // tk, 1, tk), lambda g, i: (g, 0, 0, 0)),
        ],
        out_specs=pl.BlockSpec((tq, cb), lambda g, i: (i, g)),
        out_shape=jax.ShapeDtypeStruct((s, MIX_A), BF16),
        scratch_shapes=[
            pltpu.VMEM((nh, tq, 128), F32),
            pltpu.VMEM((nh, tq, 128), F32),
            pltpu.VMEM((nh, tq, HEAD_DIM), F32),
        ],
        compiler_params=_cparams(("arbitrary", "arbitrary"), "fox_attention"),
        name="fox_attention",
    )(pbf, pbf, pbf, c_tiles)


def _dsa_kernel(bq_ref, iq_lo_ref, iq_hi_ref, iw_ref, ik_ref, bk_ref, bv_ref, o_ref,
                key_ref, cnt_ref, qg_ref, m_ref, l_ref, acc_ref, *, tq, tk, n_sel):
    b = pl.program_id(0)
    n_tiles = (b * tq) // tk + 1
    row_g = b * tq + lax.broadcasted_iota(I32, (tq, tk), 0)
    adm_end = (row_g // CHUNK + 1) * CHUNK
    col_l = lax.broadcasted_iota(I32, (tq, tk), 1)

    idx_scale = (IDX_HEADS ** -0.5) * (IDX_DIM ** -0.5)
    iw = iw_ref[...][:, 8:8 + IDX_HEADS] * idx_scale
    iq = jnp.concatenate([iq_lo_ref[...], iq_hi_ref[...]], axis=1)

    def score_tile(kt, _):
        start = pl.multiple_of(kt * tk, tk)
        ik = ik_ref[pl.ds(start, tk), :][:, :IDX_DIM]
        acc = jnp.zeros((tq, tk), F32)
        for h in range(IDX_HEADS):
            a = iq[:, h * IDX_DIM:(h + 1) * IDX_DIM]
            rel = lax.dot_general(a, ik, (((1,), (1,)), ((), ())), preferred_element_type=F32)
            acc = acc + jnp.maximum(rel, 0.0) * iw[:, h:h + 1]
        key_ref[kt] = jnp.where(kt * tk + col_l < adm_end, acc, -jnp.inf)
        return 0

    lax.fori_loop(0, n_tiles, score_tile, 0)

    def float_of(code_u):
        code = code_u ^ INT_MIN
        return pltpu.bitcast(jnp.where(code >= 0, code, code ^ 0x7FFFFFFF), F32)

    def count(above, cand):
        cnt_ref[...] = jnp.zeros(cnt_ref.shape, I32)

        def body(kt, _):
            for r in range(tq // 128):
                rs = slice(r * 128, (r + 1) * 128)
                part = cnt_ref[rs, :]
                for c in range(tk // 128):
                    part = part + jnp.where(above(key_ref[kt, rs, c * 128:(c + 1) * 128], cand[rs]), 1, 0)
                cnt_ref[rs, :] = part
            return 0
        lax.fori_loop(0, n_tiles, body, 0)
        return jnp.sum(cnt_ref[...], axis=1, keepdims=True)

    def count_ge(cand):
        return count(lambda a, b: a >= b, cand)

    def search_cond(carry):
        i, _, _, pending = carry
        return (i < 32) & (pending > 0)

    def refine(i, t_u, hit):
        cand_u = t_u | lax.shift_left(jnp.int32(1), 31 - i)
        cnt = count_ge(float_of(cand_u))
        return jnp.where(cnt >= n_sel, cand_u, t_u), jnp.where(cnt == n_sel, 1, hit)

    def search_step(carry):
        i, t_u, hit, _ = carry
        t_u, hit = refine(i, t_u, hit)
        return i + 1, t_u, hit, jnp.sum(1 - hit)

    zeros = jnp.zeros((tq, 1), I32)
    lead = 12
    t_u, hit = lax.fori_loop(0, lead, lambda i, c: refine(i, *c), (zeros, zeros))
    _, t_u, hit, _ = lax.while_loop(search_cond, search_step, (jnp.int32(lead), t_u, hit, jnp.sum(1 - hit)))
    thr = jnp.where(t_u == 0, jnp.finfo(F32).min, float_of(t_u))
    tied_rows = jnp.sum(jnp.where((hit == 0) & (t_u != 0), 1, 0))

    for j in range(DSA_KV_HEADS):
        for g in range(DSA_GROUP):
            hd = j * DSA_GROUP + g
            qg_ref[j, g * tq:(g + 1) * tq, :] = bq_ref[:, hd * HEAD_DIM:(hd + 1) * HEAD_DIM]
    _softmax_init(m_ref, l_ref, acc_ref)

    def attend(kt, sel):
        start = pl.multiple_of(kt * tk, tk)
        for j in range(DSA_KV_HEADS):
            hs = slice(j * HEAD_DIM, (j + 1) * HEAD_DIM)
            k = bk_ref[pl.ds(start, tk), hs]
            v = bv_ref[pl.ds(start, tk), hs]
            s = lax.dot_general(qg_ref[j], k, (((1,), (1,)), ((), ())), preferred_element_type=F32)
            s = jnp.where(sel[None], s.reshape(DSA_GROUP, tq, tk), NEG_BIG).reshape(DSA_GROUP * tq, tk)
            _softmax_tile(s, v, m_ref, l_ref, acc_ref, j)

    @pl.when(tied_rows == 0)
    def _():
        def attn_tile(kt, _):
            attend(kt, key_ref[kt] >= thr)
            return 0
        lax.fori_loop(0, n_tiles, attn_tile, 0)

    @pl.when(tied_rows > 0)
    def _():
        ties_wanted = (n_sel - count(lambda a, b: a > b, thr)).astype(F32)
        r_i = lax.broadcasted_iota(I32, (tk, tk), 0)
        c_i = lax.broadcasted_iota(I32, (tk, tk), 1)
        tri = jnp.where(r_i <= c_i, 1.0, 0.0).astype(BF16)

        def attn_tile(kt, ties_seen):
            key = key_ref[kt]
            tie = key == thr
            rank = ties_seen + jnp.dot(jnp.where(tie, 1.0, 0.0).astype(BF16), tri, preferred_element_type=F32)
            attend(kt, (key > thr) | (tie & (rank <= ties_wanted)))
            return rank[:, tk - 1:tk]
        lax.fori_loop(0, n_tiles, attn_tile, jnp.zeros((tq, 1), F32))

    for j in range(DSA_KV_HEADS):
        o = _softmax_result(l_ref, acc_ref, j)
        for g in range(DSA_GROUP):
            hd = j * DSA_GROUP + g
            o_ref[:, hd * HEAD_DIM:(hd + 1) * HEAD_DIM] = o[g * tq:(g + 1) * tq].astype(o_ref.dtype)


def _dsa_attention(pbf, pg, s, tq, tk):
    n_sel = min(TOPK_MAX, s // 4)
    kvw = DSA_KV_HEADS * HEAD_DIM
    rows = DSA_GROUP * tq
    iqw = IDX_HEADS * IDX_DIM // 2
    return pl.pallas_call(
        functools.partial(_dsa_kernel, tq=tq, tk=tk, n_sel=n_sel),
        grid=(s // tq,),
        in_specs=[
            pl.BlockSpec((tq, MIX_B), lambda b: (b, COL_BQ // MIX_B)),
            pl.BlockSpec((tq, iqw), lambda b: (b, COL_IQ // iqw)),
            pl.BlockSpec((tq, iqw), lambda b: (b, COL_IQ // iqw + 1)),
            pl.BlockSpec((tq, 128), lambda b: (b, COL_SMALL // 128)),
            pl.BlockSpec((s, 128), lambda b: (0, COL_IK // 128)),
            pl.BlockSpec((s, kvw), lambda b: (0, COL_BK // kvw)),
            pl.BlockSpec((s, kvw), lambda b: (0, COL_BV // kvw)),
        ],
        out_specs=pl.BlockSpec((tq, MIX_B), lambda b: (b, 0)),
        out_shape=jax.ShapeDtypeStruct((s, MIX_B), BF16),
        scratch_shapes=[
            pltpu.VMEM((s // tk, tq, tk), F32),
            pltpu.VMEM((tq, 128), I32),
            pltpu.VMEM((DSA_KV_HEADS, rows, HEAD_DIM), BF16),
            pltpu.VMEM((DSA_KV_HEADS, rows, 128), F32),
            pltpu.VMEM((DSA_KV_HEADS, rows, 128), F32),
            pltpu.VMEM((DSA_KV_HEADS, rows, HEAD_DIM), F32),
        ],
        compiler_params=_cparams(("arbitrary",), "dsa_attention"),
        name="dsa_attention",
    )(pbf, pbf, pbf, pg, pbf, pbf, pbf)


def _layer_norm(z, g, b):
    mu = jnp.mean(z, axis=-1, keepdims=True)
    zc = z - mu
    var = jnp.mean(zc * zc, axis=-1, keepdims=True)
    return zc * lax.rsqrt(var + LN_EPS) * g + b


def _merge_kernel(a_ref, b_ref, ga_ref, gb_ref, x_ref, wa_ref, wb_ref, wo_ref, g_ref, beta_ref,
                  wr_ref, br_ref, h_ref, eidx_ref, gate_ref, pos_ref, cnt_ref, carry_ref, *, alpha):
    i = pl.program_id(0)

    @pl.when(i == 0)
    def _():
        carry_ref[...] = jnp.zeros_like(carry_ref)

    ma = jnp.dot(a_ref[...], wa_ref[...], preferred_element_type=F32)
    mb = jnp.dot(b_ref[...], wb_ref[...], preferred_element_type=F32)
    merged = jax.nn.sigmoid(ga_ref[...]) * ma + jax.nn.sigmoid(gb_ref[...]) * mb
    y = jnp.dot(merged.astype(BF16), wo_ref[...], preferred_element_type=F32)
    h = _layer_norm(alpha * x_ref[...] + y, g_ref[...], beta_ref[...])
    h_ref[...] = h

    wr = wr_ref[...]
    h_hi = h.astype(BF16)
    h_lo = (h - h_hi.astype(F32)).astype(BF16)
    wr_hi = wr.astype(BF16)
    wr_lo = (wr - wr_hi.astype(F32)).astype(BF16)
    logits = (jnp.dot(h_hi, wr_hi, preferred_element_type=F32) + jnp.dot(h_lo, wr_hi, preferred_element_type=F32)
              + jnp.dot(h_hi, wr_lo, preferred_element_type=F32) + br_ref[...])
    tm, ne = logits.shape
    lane = lax.broadcasted_iota(I32, (tm, ne), 1)
    lane_k = lax.broadcasted_iota(I32, (tm, TOP_K), 1)
    work = logits
    vals, sels = [], []
    eidx = jnp.zeros((tm, TOP_K), I32)
    onehot = jnp.zeros((tm, ne), F32)
    for k in range(TOP_K):
        mv = jnp.max(work, axis=1, keepdims=True)
        idx = jnp.min(jnp.where(work == mv, lane, ne), axis=1, keepdims=True)
        sel = lane == idx
        vals.append(mv)
        sels.append(sel)
        eidx = jnp.where(lane_k == k, idx, eidx)
        onehot = onehot + jnp.where(sel, 1.0, 0.0)
        work = jnp.where(sel, -jnp.inf, work)
    exps = [jnp.exp(v - vals[0]) for v in vals]
    denom = exps[0] + exps[1] + exps[2] + exps[3]
    gates = jnp.zeros((tm, TOP_K), F32)
    for k in range(TOP_K):
        gates = jnp.where(lane_k == k, exps[k] / denom, gates)

    r_i = lax.broadcasted_iota(I32, (tm, tm), 0)
    c_i = lax.broadcasted_iota(I32, (tm, tm), 1)
    lower = jnp.where(c_i < r_i, 1.0, 0.0).astype(BF16)
    rank = carry_ref[...] + jnp.dot(lower, onehot.astype(BF16), preferred_element_type=F32)
    pos = jnp.zeros((tm, TOP_K), I32)
    for k in range(TOP_K):
        pk = jnp.sum(jnp.where(sels[k], rank, 0.0), axis=1, keepdims=True).astype(I32)
        pos = jnp.where(lane_k == k, pk, pos)
    carry_ref[...] = carry_ref[...] + jnp.sum(onehot, axis=0, keepdims=True)

    eidx_ref[...] = eidx
    gate_ref[...] = gates
    pos_ref[...] = pos
    cnt_ref[...] = carry_ref[...].astype(I32)


def _merge(a_out, b_out, pg, x2d, wa, wb, wo, ln_g, ln_b, w_router, b_router, alpha):
    s, d = x2d.shape
    tm = min(256, s)
    full = lambda shape: pl.BlockSpec(shape, lambda i: (0,) * len(shape))
    return pl.pallas_call(
        functools.partial(_merge_kernel, alpha=alpha),
        grid=(s // tm,),
        in_specs=[
            pl.BlockSpec((tm, MIX_A), lambda i: (i, 0)),
            pl.BlockSpec((tm, MIX_B), lambda i: (i, 0)),
            pl.BlockSpec((tm, d), lambda i: (i, COL_GA // d)),
            pl.BlockSpec((tm, d), lambda i: (i, COL_GB // d)),
            pl.BlockSpec((tm, d), lambda i: (i, 0)),
            full((MIX_A, d)), full((MIX_B, d)), full((d, d)),
            full((1, d)), full((1, d)), full((d, N_EXPERTS)), full((1, N_EXPERTS)),
        ],
        out_specs=[
            pl.BlockSpec((tm, d), lambda i: (i, 0)),
            pl.BlockSpec((tm, TOP_K), lambda i: (i, 0)),
            pl.BlockSpec((tm, TOP_K), lambda i: (i, 0)),
            pl.BlockSpec((tm, TOP_K), lambda i: (i, 0)),
            full((1, N_EXPERTS)),
        ],
        out_shape=[
            jax.ShapeDtypeStruct((s, d), F32),
            jax.ShapeDtypeStruct((s, TOP_K), I32),
            jax.ShapeDtypeStruct((s, TOP_K), F32),
            jax.ShapeDtypeStruct((s, TOP_K), I32),
            jax.ShapeDtypeStruct((1, N_EXPERTS), I32),
        ],
        scratch_shapes=[pltpu.VMEM((1, N_EXPERTS), F32)],
        compiler_params=_cparams(("arbitrary",), "merge_ln_router"),
        name="merge_ln_router",
    )(a_out, b_out, pg, pg, x2d, wa, wb, wo, ln_g, ln_b, w_router, b_router)


def _dispatch_kernel(dest_ref, gap_start_ref, gap_len_ref, total_ref, h_ref, xs_ref, stage_ref, zero_ref,
                     sem, zsem, *, max_slack):
    i = pl.program_id(0)
    last = pl.num_programs(0) - 1
    slot = lax.rem(i, 2)
    n = dest_ref.shape[0]

    @pl.when(i == 0)
    def _():
        zero_ref[...] = jnp.zeros_like(zero_ref)

        def gap_copy(e, k):
            return pltpu.make_async_copy(zero_ref.at[pl.ds(0, 1)], xs_ref.at[pl.ds(gap_start_ref[e] + k, 1)], zsem)

        n_slots = xs_ref.shape[0]

        def slack_copy(k):
            row = pl.multiple_of(total_ref[0] + k * MOE_CHUNK, MOE_CHUNK)
            return pltpu.make_async_copy(zero_ref, xs_ref.at[pl.ds(row, MOE_CHUNK)], zsem)

        def slack(action):
            for k in range(max_slack):
                @pl.when(total_ref[0] + k * MOE_CHUNK < n_slots)
                def _():
                    action(slack_copy(k))

        def issue(e, _):
            lax.fori_loop(0, gap_len_ref[e], lambda k, c: (gap_copy(e, k).start(), c)[1], 0)
            return 0

        def settle(e, _):
            lax.fori_loop(0, gap_len_ref[e], lambda k, c: (gap_copy(e, k).wait(), c)[1], 0)
            return 0

        lax.fori_loop(0, N_EXPERTS, issue, 0)
        slack(lambda cp: cp.start())
        lax.fori_loop(0, N_EXPERTS, settle, 0)
        slack(lambda cp: cp.wait())

    def drain(sl):
        pltpu.make_async_copy(xs_ref.at[pl.ds(0, n)], xs_ref.at[pl.ds(0, n)], sem.at[sl]).wait()

    @pl.when(i >= 2)
    def _():
        drain(slot)

    stage_ref[slot] = h_ref[...]

    def scatter(sl):
        def start(g, _):
            base = pl.multiple_of(g * ROW_TILE, ROW_TILE)
            for u in range(ROW_TILE * TOP_K):
                r = u // TOP_K
                pltpu.make_async_copy(stage_ref.at[sl, pl.ds(base + r, 1)],
                                      xs_ref.at[pl.ds(dest_ref[g * (ROW_TILE * TOP_K) + u], 1)],
                                      sem.at[sl]).start(priority=u % 2)
            return 0
        lax.fori_loop(0, n // (ROW_TILE * TOP_K), start, 0)

    for parity in range(2):
        @pl.when(slot == parity)
        def _():
            scatter(parity)

    @pl.when(i == last)
    def _():
        drain(slot)

        @pl.when(i >= 1)
        def _():
            drain(1 - slot)


def _dispatch(h, dest_flat, gap_start, gap_len, total, n_slots):
    s, d = h.shape
    tm = min(128, s)
    smem = pl.BlockSpec(memory_space=pltpu.SMEM)
    max_slack = (n_slots - s * TOP_K + MOE_CHUNK - 1) // MOE_CHUNK
    return pl.pallas_call(
        functools.partial(_dispatch_kernel, max_slack=max_slack),
        grid=(s // tm,),
        in_specs=[
            pl.BlockSpec((tm * TOP_K,), lambda i: (i,), memory_space=pltpu.SMEM),
            smem, smem, smem,
            pl.BlockSpec((tm, d), lambda i: (i, 0)),
        ],
        out_specs=pl.BlockSpec(memory_space=pl.ANY),
        out_shape=jax.ShapeDtypeStruct((n_slots, d), h.dtype),
        scratch_shapes=[pltpu.VMEM((2, tm, d), h.dtype), pltpu.VMEM((MOE_CHUNK, d), h.dtype),
                        pltpu.SemaphoreType.DMA((2,)), pltpu.SemaphoreType.DMA(())],
        compiler_params=_cparams(("arbitrary",), "moe_dispatch"),
        name="moe_dispatch",
    )(dest_flat, gap_start, gap_len, total, h)


ROW_DMA_PRIORITY = 1


def _expert_rows_pipeline(n_chunks, in_copy, out_copy, compute):
    @pl.when(n_chunks > 0)
    def _():
        in_copy(0, 0).start(priority=ROW_DMA_PRIORITY)

        def body(c, _):
            slot = lax.rem(c, 2)
            in_copy(c, slot).wait()

            @pl.when(c + 1 < n_chunks)
            def _():
                in_copy(c + 1, 1 - slot).start(priority=ROW_DMA_PRIORITY)

            @pl.when(c >= 2)
            def _():
                out_copy(c - 2, slot).wait()

            compute(slot, c)
            out_copy(c, slot).start(priority=ROW_DMA_PRIORITY)
            return 0

        lax.fori_loop(0, n_chunks, body, 0)

        @pl.when(n_chunks >= 2)
        def _():
            out_copy(n_chunks - 2, lax.rem(n_chunks, 2)).wait()

        out_copy(n_chunks - 1, lax.rem(n_chunks - 1, 2)).wait()


GATE_UP_BAND = 256
DOWN_BAND = 512


def _weight_band_stream(w_hbm, w_stage, sem_w, e, band, n_bands):
    def band_copy(idx, b):
        cols = pl.ds(b * band, band)
        return pltpu.make_async_copy(w_hbm.at[idx, :, cols], w_stage.at[:, cols], sem_w.at[b])

    def start_bands(idx, bands):
        for b in bands:
            band_copy(idx, b).start(priority=b % 2)

    def await_bands(bands):
        for b in bands:
            band_copy(e, b).wait()

    def refill_bands(bands):
        @pl.when(e + 1 < N_EXPERTS)
        def _():
            start_bands(e + 1, bands)

    @pl.when(e == 0)
    def _():
        start_bands(0, range(n_bands))

    return await_bands, refill_bands


def _chunk_rows(start_ref, e, c):
    return pl.ds(pl.multiple_of(start_ref[e] + c * MOE_CHUNK, MOE_ALIGN), MOE_CHUNK)


def _gate_up_kernel(start_ref, cnt_ref, x_hbm, w_hbm, b_ref, act_hbm,
                    w_stage, w_sc, xbuf, obuf, sem_in, sem_out, sem_w):
    e = pl.program_id(0)
    n_chunks = (cnt_ref[e] + MOE_CHUNK - 1) // MOE_CHUNK
    d_ff = obuf.shape[2]

    cw = GATE_UP_BAND
    n_bands = 2 * d_ff // cw
    has_weights = e < N_EXPERTS
    await_bands, refill = _weight_band_stream(w_hbm, w_stage, sem_w, e, cw, n_bands)

    def in_copy(c, slot):
        return pltpu.make_async_copy(x_hbm.at[_chunk_rows(start_ref, e, c)], xbuf.at[slot], sem_in.at[slot])

    def out_copy(c, slot):
        return pltpu.make_async_copy(obuf.at[slot], act_hbm.at[_chunk_rows(start_ref, e, c)],
                                     sem_out.at[slot])

    def rows_times_weights(slot, cast_first):
        x = xbuf[slot].astype(BF16)
        for c in range(d_ff // cw):
            gs = slice(c * cw, (c + 1) * cw)
            us = slice(d_ff + c * cw, d_ff + (c + 1) * cw)
            if cast_first:
                bands = (c, d_ff // cw + c)
                await_bands(bands)
                w_sc[:, gs] = w_stage[:, gs].astype(BF16)
                w_sc[:, us] = w_stage[:, us].astype(BF16)
                refill(bands)
            g = jnp.dot(x, w_sc[:, gs], preferred_element_type=F32) + b_ref[:, gs]
            u = jnp.dot(x, w_sc[:, us], preferred_element_type=F32) + b_ref[:, us]
            g = jnp.minimum(g, SWIGLU_LIMIT)
            u = jnp.clip(u, -SWIGLU_LIMIT, SWIGLU_LIMIT)
            obuf[slot, :, gs] = (g * jax.nn.sigmoid(SWIGLU_ALPHA * g) * (u + 1.0)).astype(obuf.dtype)

    def compute(slot, c):
        fresh = (c == 0) & has_weights

        @pl.when(fresh)
        def _():
            rows_times_weights(slot, True)

        @pl.when(jnp.logical_not(fresh))
        def _():
            rows_times_weights(slot, False)

    @pl.when((n_chunks == 0) & has_weights)
    def _():
        await_bands(range(n_bands))
        refill(range(n_bands))

    _expert_rows_pipeline(n_chunks, in_copy, out_copy, compute)


def _down_kernel(start_ref, cnt_ref, a_hbm, w_hbm, bd_ref, y_hbm, w_stage, wd_sc, abuf, obuf, sem_in, sem_out,
                 sem_w):
    e = pl.program_id(0)
    n_chunks = (cnt_ref[e] + MOE_CHUNK - 1) // MOE_CHUNK

    cw = DOWN_BAND
    n_bands = obuf.shape[2] // cw
    has_weights = e < N_EXPERTS
    await_bands, refill = _weight_band_stream(w_hbm, w_stage, sem_w, e, cw, n_bands)

    @pl.when((n_chunks == 0) & has_weights)
    def _():
        await_bands(range(n_bands))
        refill(range(n_bands))

    def in_copy(c, slot):
        return pltpu.make_async_copy(a_hbm.at[_chunk_rows(start_ref, e, c)], abuf.at[slot],
                                     sem_in.at[slot])

    def out_copy(c, slot):
        return pltpu.make_async_copy(obuf.at[slot], y_hbm.at[_chunk_rows(start_ref, e, c)],
                                     sem_out.at[slot])

    def rows_times_weights(slot, cast_first):
        for b in range(n_bands):
            cs = slice(b * cw, (b + 1) * cw)
            if cast_first:
                await_bands((b,))
                wd_sc[:, cs] = w_stage[:, cs].astype(BF16)
                refill((b,))
            obuf[slot, :, cs] = jnp.dot(abuf[slot], wd_sc[:, cs], preferred_element_type=F32) + bd_ref[:, cs]

    def compute(slot, c):
        fresh = (c == 0) & has_weights

        @pl.when(fresh)
        def _():
            rows_times_weights(slot, True)

        @pl.when(jnp.logical_not(fresh))
        def _():
            rows_times_weights(slot, False)

    _expert_rows_pipeline(n_chunks, in_copy, out_copy, compute)


def _experts(xs, starts, counts, w_gate_up, b_gate_up, w_down, b_down):
    n_slots, d = xs.shape
    any_spec = pl.BlockSpec(memory_space=pl.ANY)
    dma_sems = pltpu.SemaphoreType.DMA((2,))
    bgu = b_gate_up.reshape(N_EXPERTS, 1, 2 * D_FF)
    n_groups = starts.shape[0]
    wi = lambda e: jnp.minimum(e, N_EXPERTS - 1)
    act = pl.pallas_call(
        _gate_up_kernel,
        grid_spec=pltpu.PrefetchScalarGridSpec(
            num_scalar_prefetch=2,
            grid=(n_groups,),
            in_specs=[
                any_spec,
                any_spec,
                pl.BlockSpec((None, 1, 2 * D_FF), lambda e, st, ct: (wi(e), 0, 0)),
            ],
            out_specs=any_spec,
            scratch_shapes=[pltpu.VMEM((d, 2 * D_FF), F32), pltpu.VMEM((d, 2 * D_FF), BF16),
                            pltpu.VMEM((2, MOE_CHUNK, d), xs.dtype), pltpu.VMEM((2, MOE_CHUNK, D_FF), BF16),
                            dma_sems, dma_sems, pltpu.SemaphoreType.DMA((2 * D_FF // GATE_UP_BAND,))],
        ),
        out_shape=jax.ShapeDtypeStruct((n_slots, D_FF), BF16),
        compiler_params=_cparams(("arbitrary",), "moe_gate_up"),
        name="moe_gate_up",
    )(starts, counts, xs, w_gate_up, bgu)
    bd = b_down.reshape(N_EXPERTS, 1, d)
    ys = pl.pallas_call(
        _down_kernel,
        grid_spec=pltpu.PrefetchScalarGridSpec(
            num_scalar_prefetch=2,
            grid=(n_groups,),
            in_specs=[
                any_spec,
                any_spec,
                pl.BlockSpec((None, 1, d), lambda e, st, ct: (wi(e), 0, 0)),
            ],
            out_specs=any_spec,
            scratch_shapes=[pltpu.VMEM((D_FF, d), F32), pltpu.VMEM((D_FF, d), BF16),
                            pltpu.VMEM((2, MOE_CHUNK, D_FF), BF16), pltpu.VMEM((2, MOE_CHUNK, d), F32),
                            dma_sems, dma_sems, pltpu.SemaphoreType.DMA((d // DOWN_BAND,))],
        ),
        out_shape=jax.ShapeDtypeStruct((n_slots, d), F32),
        compiler_params=_cparams(("arbitrary",), "moe_down"),
        name="moe_down",
    )(starts, counts, act, w_down, bd)
    return ys


def _combine_kernel(dest_ref, dest_next_ref, gate_ref, h_ref, g_ref, beta_ref, ys_ref, o_ref, buf_ref, sem,
                    *, alpha):
    i = pl.program_id(0)
    last = pl.num_programs(0) - 1
    slot = lax.rem(i, 2)
    n = dest_ref.shape[0]

    def gather(dref, sl):
        def start(g, _):
            base = pl.multiple_of(g * ROW_TILE, ROW_TILE)
            for u in range(ROW_TILE * TOP_K):
                r, k = divmod(u, TOP_K)
                pltpu.make_async_copy(ys_ref.at[pl.ds(dref[g * (ROW_TILE * TOP_K) + u], 1)],
                                      buf_ref.at[sl, k, pl.ds(base + r, 1)],
                                      sem.at[sl]).start(priority=u % 2)
            return 0
        lax.fori_loop(0, n // (ROW_TILE * TOP_K), start, 0)

    for parity in range(2):
        @pl.when((i == 0) & (slot == parity))
        def _():
            gather(dest_ref, parity)

        @pl.when((i < last) & (slot == parity))
        def _():
            gather(dest_next_ref, 1 - parity)

    pltpu.make_async_copy(buf_ref.at[slot], buf_ref.at[slot], sem.at[slot]).wait()
    gates = gate_ref[...]
    y = gates[:, 0:1] * buf_ref[slot, 0]
    for k in range(1, TOP_K):
        y = y + gates[:, k:k + 1] * buf_ref[slot, k]
    o_ref[...] = _layer_norm(alpha * h_ref[...] + y, g_ref[...], beta_ref[...])


def _combine(ys, dest_flat, gates, h, ln_g, ln_b, alpha):
    s, d = h.shape
    tm = min(128, s)
    n_steps = s // tm
    return pl.pallas_call(
        functools.partial(_combine_kernel, alpha=alpha),
        grid=(n_steps,),
        in_specs=[
            pl.BlockSpec((tm * TOP_K,), lambda i: (i,), memory_space=pltpu.SMEM),
            pl.BlockSpec((tm * TOP_K,), lambda i: (jnp.minimum(i + 1, n_steps - 1),), memory_space=pltpu.SMEM),
            pl.BlockSpec((tm, TOP_K), lambda i: (i, 0)),
            pl.BlockSpec((tm, d), lambda i: (i, 0)),
            pl.BlockSpec((1, d), lambda i: (0, 0)),
            pl.BlockSpec((1, d), lambda i: (0, 0)),
            pl.BlockSpec(memory_space=pl.ANY),
        ],
        out_specs=pl.BlockSpec((tm, d), lambda i: (i, 0)),
        out_shape=jax.ShapeDtypeStruct((s, d), F32),
        scratch_shapes=[pltpu.VMEM((2, TOP_K, tm, d), F32), pltpu.SemaphoreType.DMA((2,))],
        compiler_params=_cparams(("arbitrary",), "moe_combine"),
        name="moe_combine",
    )(dest_flat, dest_flat, gates, h, ln_g, ln_b, ys)


def _layer(x2d, w_in, b_forget, w_branch_a, w_branch_b, w_out, ln1_g, ln1_b, w_router, b_router,
           w_gate_up, b_gate_up, w_down, b_down, ln2_g, ln2_b, tables):
    s, d = x2d.shape
    alpha = (2.0 * DEPTH) ** 0.25
    scale = HEAD_DIM ** -0.5 * LOG2E

    w_bf, w_g = _prepare_w_in(w_in.T, scale)
    pbf, pg = _project(x2d, w_bf, tables, w_g)

    fox_tq = min(512, s)
    fox_tk = min(512, s)
    af_t = pg[:, COL_SMALL:COL_SMALL + FOX_HEADS].T
    c_t = _forget_cumsum(af_t, b_forget)
    c_tiles = c_t.reshape(FOX_HEADS, s // fox_tk, 1, fox_tk)
    a_out = _fox_attention(pbf, c_tiles, s, fox_tq, fox_tk, 4)

    b_out = _dsa_attention(pbf, pg, s, min(256, s), min(512, s))

    h, eidx, gates, pos, counts = _merge(
        a_out, b_out, pg, x2d, w_branch_a.astype(BF16), w_branch_b.astype(BF16), w_out.astype(BF16),
        ln1_g.reshape(1, d), ln1_b.reshape(1, d), w_router, b_router.reshape(1, N_EXPERTS), alpha)

    counts = counts.reshape(N_EXPERTS).astype(I32)
    aligned = (counts + MOE_ALIGN - 1) // MOE_ALIGN * MOE_ALIGN
    ends = jnp.cumsum(aligned).astype(I32)
    starts = ends - aligned
    bound = s * TOP_K + N_EXPERTS * MOE_ALIGN + 2 * MOE_CHUNK
    n_slots = (bound + MOE_CHUNK - 1) // MOE_CHUNK * MOE_CHUNK
    experts = jnp.arange(N_EXPERTS, dtype=I32)
    start_of = jnp.sum(jnp.where(eidx[..., None] == experts, starts, 0), axis=-1)
    dest = (start_of + pos).reshape(s * TOP_K).astype(I32)
    total = (ends[-1:] + MOE_CHUNK - 1) // MOE_CHUNK * MOE_CHUNK
    gap_len = aligned - counts + jnp.where(experts == N_EXPERTS - 1, total[0] - ends[-1], 0)
    group_starts = jnp.concatenate([starts, total])
    group_counts = jnp.concatenate([counts + jnp.where(experts == N_EXPERTS - 1, gap_len, 0), n_slots - total])

    xs = _dispatch(h, dest, starts + counts, gap_len, total, n_slots)
    ys = _experts(xs, group_starts, group_counts, w_gate_up, b_gate_up, w_down, b_down)
    return _combine(ys, dest, gates, h, ln2_g.reshape(1, d), ln2_b.reshape(1, d), alpha)


def kernel(x, w_in, b_forget, w_branch_a, w_branch_b, w_out, ln1_g, ln1_b, w_router, b_router,
           w_gate_up, b_gate_up, w_down, b_down, ln2_g, ln2_b):
    bsz, s, d = x.shape
    tables = _rope_tables(s)
    outs = []
    for bi in range(bsz):
        xb = x[bi]
        for l in range(DEPTH):
            xb = _layer(xb, w_in[l], b_forget[l], w_branch_a[l], w_branch_b[l], w_out[l], ln1_g[l], ln1_b[l],
                        w_router[l], b_router[l], w_gate_up[l], b_gate_up[l], w_down[l], b_down[l],
                        ln2_g[l], ln2_b[l], tables)
        outs.append(xb)
    return outs[0][None] if bsz == 1 else jnp.stack(outs)
```

```python
import functools

import numpy as np
import jax
import jax.numpy as jnp
from jax import lax
from jax.experimental import pallas as pl
from jax.experimental.pallas import tpu as pltpu

F32 = jnp.float32
BF16 = jnp.bfloat16
I32 = jnp.int32

D_MODEL = 2048
DEPTH = 1
CHUNK = 64
HEAD_DIM = 128
FOX_HEADS = 8
DSA_HEADS = 8
DSA_KV_HEADS = 2
DSA_GROUP = DSA_HEADS // DSA_KV_HEADS
IDX_HEADS = 16
IDX_DIM = 64
TOPK_MAX = 256
ROPE_THETA = 500000.0
ROT_FRACTION_DEN = 4
MIX_A = FOX_HEADS * HEAD_DIM
MIX_B = DSA_HEADS * HEAD_DIM
N_EXPERTS = 32
TOP_K = 4
D_FF = D_MODEL
SWIGLU_ALPHA = 1.702
SWIGLU_LIMIT = 7.0
LN_EPS = 1e-5

MIB = 1024 * 1024
NEG_BIG = -1e30
LOG2E = 1.4426950408889634
NT_DIMS = (((1,), (1,)), ((), ()))
INT_MIN = -(2 ** 31)

COL_AQ = 0
COL_AK = 1024
COL_AV = 2048
COL_BQ = 3072
COL_BK = 4096
COL_BV = 4352
COL_IQ = 4608
COL_IK = 5632
N_PBF = 5888
PROJ_TN = 256
ROPE_TABLE_W = 128
ROPE_HEAD_TILES = (12, 13, 14, 15, 16)
ROPE_IDX_TILES = (18, 19, 20, 21, 22)
COL_GA = 0
COL_GB = 2048
COL_SMALL = 4096
N_PG = 4224
PG_TN = 384

MOE_CHUNK = 256
MOE_ALIGN = 16
ROW_TILE = 8


V7X_VMEM_MIB = 64
VMEM_LIMIT_MIB = dict(
    w_in_prep=48, proj_bf16=60, proj_f32=60, forget_cumsum=32, fox_attention=56, dsa_attention=56,
    merge_ln_router=56, moe_dispatch=32, moe_gate_up=60, moe_down=56, moe_combine=32)
assert max(VMEM_LIMIT_MIB.values()) < V7X_VMEM_MIB


def _cparams(dims, name):
    return pltpu.CompilerParams(dimension_semantics=dims, vmem_limit_bytes=VMEM_LIMIT_MIB[name] * MIB)


IN_SIZES = (MIX_A, MIX_A, MIX_A, FOX_HEADS, MIX_B, DSA_KV_HEADS * HEAD_DIM, DSA_KV_HEADS * HEAD_DIM,
            IDX_HEADS * IDX_DIM, IDX_DIM, IDX_HEADS, D_MODEL, D_MODEL)
IN_OFF = dict(zip(("aq", "ak", "av", "af", "bq", "bk", "bv", "iq", "ik", "iw", "ga", "gb", "end"),
                  np.concatenate([[0], np.cumsum(IN_SIZES)]).tolist()))


def _prep_kernel(w_ref, wbf_ref, wg_ref, *, scale):
    o = IN_OFF
    width = w_ref.shape[1]

    def rows(a, b, mult=None):
        v = w_ref[a:b, :]
        return v if mult is None else v * mult

    wbf_ref[COL_AQ:COL_AK, :] = rows(o["aq"], o["ak"], scale).astype(BF16)
    wbf_ref[COL_AK:COL_BQ, :] = rows(o["ak"], o["af"]).astype(BF16)
    wbf_ref[COL_BQ:COL_BK, :] = rows(o["bq"], o["bk"], scale).astype(BF16)
    wbf_ref[COL_BK:COL_IK, :] = rows(o["bk"], o["ik"]).astype(BF16)
    wbf_ref[COL_IK:N_PBF, :] = jnp.concatenate(
        [rows(o["ik"], o["iw"]), jnp.zeros((N_PBF - COL_IK - IDX_DIM, width), F32)], axis=0).astype(BF16)
    wg_ref[COL_GA:COL_SMALL, :] = rows(o["ga"], o["end"]).astype(BF16)
    wg_ref[COL_SMALL:N_PG, :] = jnp.concatenate(
        [rows(o["af"], o["bq"]), rows(o["iw"], o["ga"]),
         jnp.zeros((N_PG - COL_SMALL - FOX_HEADS - IDX_HEADS, width), F32)], axis=0).astype(BF16)


def _prepare_w_in(w_in_t, scale):
    n_in, d = w_in_t.shape
    tc = 256
    return pl.pallas_call(
        functools.partial(_prep_kernel, scale=scale),
        grid=(d // tc,),
        in_specs=[pl.BlockSpec((n_in, tc), lambda i: (0, i))],
        out_specs=[pl.BlockSpec((N_PBF, tc), lambda i: (0, i)), pl.BlockSpec((N_PG, tc), lambda i: (0, i))],
        out_shape=[jax.ShapeDtypeStruct((N_PBF, d), BF16), jax.ShapeDtypeStruct((N_PG, d), BF16)],
        compiler_params=_cparams(("arbitrary",), "w_in_prep"),
        name="w_in_prep",
    )(w_in_t)


def _tile_in(j, tiles):
    cond = j == tiles[0]
    for t in tiles[1:]:
        cond = cond | (j == t)
    return cond


def _proj_rope_kernel(x_ref, w_ref, tab_ref, o_ref, xb_ref):
    j = pl.program_id(1)

    @pl.when(j == 0)
    def _():
        xb_ref[...] = x_ref[...].astype(BF16)

    acc = lax.dot_general(xb_ref[...], w_ref[...], NT_DIMS, preferred_element_type=F32)
    tn = acc.shape[1]
    is_head = _tile_in(j, ROPE_HEAD_TILES)
    is_idx = _tile_in(j, ROPE_IDX_TILES)

    def rope(shift):
        reps = tn // ROPE_TABLE_W
        c = jnp.tile(tab_ref[0, 0], (1, reps))
        s_prev = jnp.tile(tab_ref[0, 1], (1, reps))
        s_next = jnp.tile(tab_ref[0, 2], (1, reps))
        out = acc * c + pltpu.roll(acc, shift, 1) * s_prev + pltpu.roll(acc, tn - shift, 1) * s_next
        o_ref[...] = out.astype(o_ref.dtype)

    @pl.when(is_head)
    def _():
        rope(HEAD_DIM // ROT_FRACTION_DEN // 2)

    @pl.when(is_idx)
    def _():
        rope(IDX_DIM // ROT_FRACTION_DEN // 2)

    @pl.when(jnp.logical_not(is_head | is_idx))
    def _():
        o_ref[...] = acc.astype(o_ref.dtype)


def _proj_plain_kernel(x_ref, w_ref, o_ref, xb_ref):
    j = pl.program_id(1)

    @pl.when(j == 0)
    def _():
        xb_ref[...] = x_ref[...].astype(BF16)

    o_ref[...] = lax.dot_general(xb_ref[...], w_ref[...], NT_DIMS, preferred_element_type=F32).astype(o_ref.dtype)


def _rope_group(j):
    return jnp.where(_tile_in(j, ROPE_IDX_TILES), 1, 0)


def _project(x2d, w_bf, tables, w_g):
    s, d = x2d.shape
    tm = min(2048, s)
    pbf = pl.pallas_call(
        _proj_rope_kernel,
        grid=(s // tm, N_PBF // PROJ_TN),
        in_specs=[
            pl.BlockSpec((tm, d), lambda i, j: (i, 0)),
            pl.BlockSpec((PROJ_TN, d), lambda i, j: (j, 0)),
            pl.BlockSpec((1, 3, tm, ROPE_TABLE_W), lambda i, j: (_rope_group(j), 0, i, 0)),
        ],
        out_specs=pl.BlockSpec((tm, PROJ_TN), lambda i, j: (i, j)),
        out_shape=jax.ShapeDtypeStruct((s, N_PBF), BF16),
        scratch_shapes=[pltpu.VMEM((tm, d), BF16)],
        compiler_params=_cparams(("arbitrary", "arbitrary"), "proj_bf16"),
        name="proj_bf16",
    )(x2d, w_bf, tables)
    pg = pl.pallas_call(
        _proj_plain_kernel,
        grid=(s // tm, N_PG // PG_TN),
        in_specs=[
            pl.BlockSpec((tm, d), lambda i, j: (i, 0)),
            pl.BlockSpec((PG_TN, d), lambda i, j: (j, 0)),
        ],
        out_specs=pl.BlockSpec((tm, PG_TN), lambda i, j: (i, j)),
        out_shape=jax.ShapeDtypeStruct((s, N_PG), F32),
        scratch_shapes=[pltpu.VMEM((tm, d), BF16)],
        compiler_params=_cparams(("arbitrary", "arbitrary"), "proj_f32"),
        name="proj_f32",
    )(x2d, w_g)
    return pbf, pg


def _rope_tables(s):
    pos = jnp.arange(s, dtype=F32)

    def one(period):
        rot = period // ROT_FRACTION_DEN
        half = rot // 2
        inv = jnp.power(ROPE_THETA, -jnp.arange(0, rot, 2, dtype=F32) / rot)
        ang = pos[:, None] * inv[None, :]
        cos, sin = jnp.cos(ang), jnp.sin(ang)
        zero = jnp.zeros((s, period - rot), F32)
        c = jnp.concatenate([cos, cos, jnp.ones((s, period - rot), F32)], axis=1)
        s_prev = jnp.concatenate([jnp.zeros((s, half), F32), sin, zero], axis=1)
        s_next = jnp.concatenate([-sin, jnp.zeros((s, half), F32), zero], axis=1)
        reps = ROPE_TABLE_W // period
        return jnp.stack([jnp.tile(c, (1, reps)), jnp.tile(s_prev, (1, reps)), jnp.tile(s_next, (1, reps))])

    return jnp.stack([one(HEAD_DIM), one(IDX_DIM)])


def _cumsum_kernel(af_ref, bf_ref, c_ref, carry_ref):
    i = pl.program_id(0)

    @pl.when(i == 0)
    def _():
        carry_ref[...] = jnp.zeros_like(carry_ref)

    z = af_ref[...] + bf_ref[...]
    logf = jnp.minimum(z, 0.0) - jnp.log1p(jnp.exp(-jnp.abs(z)))
    t = z.shape[1]
    row = lax.broadcasted_iota(I32, (t, t), 0)
    col = lax.broadcasted_iota(I32, (t, t), 1)
    upper = (row <= col).astype(F32)
    c = jnp.dot(logf, upper, preferred_element_type=F32, precision=lax.Precision.HIGHEST) + carry_ref[...]
    c_ref[...] = c * LOG2E
    carry_ref[...] = c[:, t - 1:t]


def _forget_cumsum(af_t, b_forget):
    h, s = af_t.shape
    t = min(512, s)
    return pl.pallas_call(
        _cumsum_kernel,
        grid=(s // t,),
        in_specs=[pl.BlockSpec((h, t), lambda i: (0, i)), pl.BlockSpec((h, 1), lambda i: (0, 0))],
        out_specs=pl.BlockSpec((h, t), lambda i: (0, i)),
        out_shape=jax.ShapeDtypeStruct((h, s), F32),
        scratch_shapes=[pltpu.VMEM((h, 1), F32)],
        compiler_params=_cparams(("arbitrary",), "forget_cumsum"),
        name="forget_cumsum",
    )(af_t, b_forget.reshape(h, 1).astype(F32))


def _softmax_init(m_ref, l_ref, acc_ref):
    m_ref[...] = jnp.full(m_ref.shape, NEG_BIG, F32)
    l_ref[...] = jnp.zeros(l_ref.shape, F32)
    acc_ref[...] = jnp.zeros(acc_ref.shape, F32)


def _softmax_tile(s, v, m_ref, l_ref, acc_ref, i):
    reps = s.shape[1] // 128
    m_old = m_ref[i]
    m_new = jnp.maximum(m_old, jnp.max(s, axis=1, keepdims=True))
    p = jnp.exp2(s - jnp.tile(m_new, (1, reps)))
    alpha = jnp.exp2(m_old - m_new)
    psum = p[:, :128]
    for c in range(1, reps):
        psum = psum + p[:, c * 128:(c + 1) * 128]
    l_ref[i] = alpha * l_ref[i] + psum
    acc_ref[i] = alpha * acc_ref[i] + jnp.dot(p.astype(BF16), v, preferred_element_type=F32)
    m_ref[i] = m_new


def _softmax_result(l_ref, acc_ref, i):
    return acc_ref[i] / jnp.sum(l_ref[i], axis=1, keepdims=True)


def _fox_kernel(q_ref, k_ref, v_ref, c_ref, o_ref, m_ref, l_ref, acc_ref, *, tq, tk, nh):
    qi = pl.program_id(1)
    _softmax_init(m_ref, l_ref, acc_ref)

    def tile(kt, masked):
        start = pl.multiple_of(kt * tk, tk)
        if masked:
            row = qi * tq + lax.broadcasted_iota(I32, (tq, tk), 0)
            col = kt * tk + lax.broadcasted_iota(I32, (tq, tk), 1)
            causal = col <= row
        for h in range(nh):
            hs = slice(h * HEAD_DIM, (h + 1) * HEAD_DIM)
            k = k_ref[pl.ds(start, tk), hs]
            v = v_ref[pl.ds(start, tk), hs]
            s = lax.dot_general(q_ref[:, hs], k, (((1,), (1,)), ((), ())), preferred_element_type=F32)
            s = s - c_ref[h, kt]
            if masked:
                s = jnp.where(causal, s, NEG_BIG)
            _softmax_tile(s, v, m_ref, l_ref, acc_ref, h)

    n_full = (qi * tq) // tk

    def body(kt, _):
        tile(kt, False)
        return 0

    lax.fori_loop(0, n_full, body, 0)
    for t in range((tq + tk - 1) // tk):
        tile(n_full + t, True)
    for h in range(nh):
        o_ref[:, h * HEAD_DIM:(h + 1) * HEAD_DIM] = _softmax_result(l_ref, acc_ref, h).astype(o_ref.dtype)


def _fox_attention(pbf, c_tiles, s, tq, tk, nh):
    cb = nh * HEAD_DIM
    return pl.pallas_call(
        functools.partial(_fox_kernel, tq=tq, tk=tk, nh=nh),
        grid=(FOX_HEADS // nh, s // tq),
        in_specs=[
            pl.BlockSpec((tq, cb), lambda g, i: (i, COL_AQ // cb + g)),
            pl.BlockSpec((s, cb), lambda g, i: (0, COL_AK // cb + g)),
            pl.BlockSpec((s, cb), lambda g, i: (0, COL_AV // cb + g)),
            pl.BlockSpec((nh, s // tk, 1, tk), lambda g, i: (g, 0, 0, 0)),
        ],
        out_specs=pl.BlockSpec((tq, cb), lambda g, i: (i, g)),
        out_shape=jax.ShapeDtypeStruct((s, MIX_A), BF16),
        scratch_shapes=[
            pltpu.VMEM((nh, tq, 128), F32),
            pltpu.VMEM((nh, tq, 128), F32),
            pltpu.VMEM((nh, tq, HEAD_DIM), F32),
        ],
        compiler_params=_cparams(("arbitrary", "arbitrary"), "fox_attention"),
        name="fox_attention",
    )(pbf, pbf, pbf, c_tiles)


def _dsa_kernel(bq_ref, iq_lo_ref, iq_hi_ref, iw_ref, ik_ref, bk_ref, bv_ref, o_ref,
                key_ref, cnt_ref, qg_ref, m_ref, l_ref, acc_ref, *, tq, tk, n_sel):
    b = pl.program_id(0)
    n_tiles = (b * tq) // tk + 1
    row_g = b * tq + lax.broadcasted_iota(I32, (tq, tk), 0)
    adm_end = (row_g // CHUNK + 1) * CHUNK
    col_l = lax.broadcasted_iota(I32, (tq, tk), 1)

    idx_scale = (IDX_HEADS ** -0.5) * (IDX_DIM ** -0.5)
    iw = iw_ref[...][:, 8:8 + IDX_HEADS] * idx_scale
    iq = jnp.concatenate([iq_lo_ref[...], iq_hi_ref[...]], axis=1)

    def score_tile(kt, _):
        start = pl.multiple_of(kt * tk, tk)
        ik = ik_ref[pl.ds(start, tk), :][:, :IDX_DIM]
        acc = jnp.zeros((tq, tk), F32)
        for h in range(IDX_HEADS):
            a = iq[:, h * IDX_DIM:(h + 1) * IDX_DIM]
            rel = lax.dot_general(a, ik, (((1,), (1,)), ((), ())), preferred_element_type=F32)
            acc = acc + jnp.maximum(rel, 0.0) * iw[:, h:h + 1]
        key_ref[kt] = jnp.where(kt * tk + col_l < adm_end, acc, -jnp.inf)
        return 0

    lax.fori_loop(0, n_tiles, score_tile, 0)

    def float_of(code_u):
        code = code_u ^ INT_MIN
        return pltpu.bitcast(jnp.where(code >= 0, code, code ^ 0x7FFFFFFF), F32)

    def count(above, cand):
        cnt_ref[...] = jnp.zeros(cnt_ref.shape, I32)

        def body(kt, _):
            for r in range(tq // 128):
                rs = slice(r * 128, (r + 1) * 128)
                part = cnt_ref[rs, :]
                for c in range(tk // 128):
                    part = part + jnp.where(above(key_ref[kt, rs, c * 128:(c + 1) * 128], cand[rs]), 1, 0)
                cnt_ref[rs, :] = part
            return 0
        lax.fori_loop(0, n_tiles, body, 0)
        return jnp.sum(cnt_ref[...], axis=1, keepdims=True)

    def count_ge(cand):
        return count(lambda a, b: a >= b, cand)

    def search_cond(carry):
        i, _, _, pending = carry
        return (i < 32) & (pending > 0)

    def refine(i, t_u, hit):
        cand_u = t_u | lax.shift_left(jnp.int32(1), 31 - i)
        cnt = count_ge(float_of(cand_u))
        return jnp.where(cnt >= n_sel, cand_u, t_u), jnp.where(cnt == n_sel, 1, hit)

    def search_step(carry):
        i, t_u, hit, _ = carry
        t_u, hit = refine(i, t_u, hit)
        return i + 1, t_u, hit, jnp.sum(1 - hit)

    zeros = jnp.zeros((tq, 1), I32)
    lead = 12
    t_u, hit = lax.fori_loop(0, lead, lambda i, c: refine(i, *c), (zeros, zeros))
    _, t_u, hit, _ = lax.while_loop(search_cond, search_step, (jnp.int32(lead), t_u, hit, jnp.sum(1 - hit)))
    thr = jnp.where(t_u == 0, jnp.finfo(F32).min, float_of(t_u))
    tied_rows = jnp.sum(jnp.where((hit == 0) & (t_u != 0), 1, 0))

    for j in range(DSA_KV_HEADS):
        for g in range(DSA_GROUP):
            hd = j * DSA_GROUP + g
            qg_ref[j, g * tq:(g + 1) * tq, :] = bq_ref[:, hd * HEAD_DIM:(hd + 1) * HEAD_DIM]
    _softmax_init(m_ref, l_ref, acc_ref)

    def attend(kt, sel):
        start = pl.multiple_of(kt * tk, tk)
        for j in range(DSA_KV_HEADS):
            hs = slice(j * HEAD_DIM, (j + 1) * HEAD_DIM)
            k = bk_ref[pl.ds(start, tk), hs]
            v = bv_ref[pl.ds(start, tk), hs]
            s = lax.dot_general(qg_ref[j], k, (((1,), (1,)), ((), ())), preferred_element_type=F32)
            s = jnp.where(sel[None], s.reshape(DSA_GROUP, tq, tk), NEG_BIG).reshape(DSA_GROUP * tq, tk)
            _softmax_tile(s, v, m_ref, l_ref, acc_ref, j)

    @pl.when(tied_rows == 0)
    def _():
        def attn_tile(kt, _):
            attend(kt, key_ref[kt] >= thr)
            return 0
        lax.fori_loop(0, n_tiles, attn_tile, 0)

    @pl.when(tied_rows > 0)
    def _():
        ties_wanted = (n_sel - count(lambda a, b: a > b, thr)).astype(F32)
        r_i = lax.broadcasted_iota(I32, (tk, tk), 0)
        c_i = lax.broadcasted_iota(I32, (tk, tk), 1)
        tri = jnp.where(r_i <= c_i, 1.0, 0.0).astype(BF16)

        def attn_tile(kt, ties_seen):
            key = key_ref[kt]
            tie = key == thr
            rank = ties_seen + jnp.dot(jnp.where(tie, 1.0, 0.0).astype(BF16), tri, preferred_element_type=F32)
            attend(kt, (key > thr) | (tie & (rank <= ties_wanted)))
            return rank[:, tk - 1:tk]
        lax.fori_loop(0, n_tiles, attn_tile, jnp.zeros((tq, 1), F32))

    for j in range(DSA_KV_HEADS):
        o = _softmax_result(l_ref, acc_ref, j)
        for g in range(DSA_GROUP):
            hd = j * DSA_GROUP + g
            o_ref[:, hd * HEAD_DIM:(hd + 1) * HEAD_DIM] = o[g * tq:(g + 1) * tq].astype(o_ref.dtype)


def _dsa_attention(pbf, pg, s, tq, tk):
    n_sel = min(TOPK_MAX, s // 4)
    kvw = DSA_KV_HEADS * HEAD_DIM
    rows = DSA_GROUP * tq
    iqw = IDX_HEADS * IDX_DIM // 2
    return pl.pallas_call(
        functools.partial(_dsa_kernel, tq=tq, tk=tk, n_sel=n_sel),
        grid=(s // tq,),
        in_specs=[
            pl.BlockSpec((tq, MIX_B), lambda b: (b, COL_BQ // MIX_B)),
            pl.BlockSpec((tq, iqw), lambda b: (b, COL_IQ // iqw)),
            pl.BlockSpec((tq, iqw), lambda b: (b, COL_IQ // iqw + 1)),
            pl.BlockSpec((tq, 128), lambda b: (b, COL_SMALL // 128)),
            pl.BlockSpec((s, 128), lambda b: (0, COL_IK // 128)),
            pl.BlockSpec((s, kvw), lambda b: (0, COL_BK // kvw)),
            pl.BlockSpec((s, kvw), lambda b: (0, COL_BV // kvw)),
        ],
        out_specs=pl.BlockSpec((tq, MIX_B), lambda b: (b, 0)),
        out_shape=jax.ShapeDtypeStruct((s, MIX_B), BF16),
        scratch_shapes=[
            pltpu.VMEM((s // tk, tq, tk), F32),
            pltpu.VMEM((tq, 128), I32),
            pltpu.VMEM((DSA_KV_HEADS, rows, HEAD_DIM), BF16),
            pltpu.VMEM((DSA_KV_HEADS, rows, 128), F32),
            pltpu.VMEM((DSA_KV_HEADS, rows, 128), F32),
            pltpu.VMEM((DSA_KV_HEADS, rows, HEAD_DIM), F32),
        ],
        compiler_params=_cparams(("arbitrary",), "dsa_attention"),
        name="dsa_attention",
    )(pbf, pbf, pbf, pg, pbf, pbf, pbf)


def _layer_norm(z, g, b):
    mu = jnp.mean(z, axis=-1, keepdims=True)
    zc = z - mu
    var = jnp.mean(zc * zc, axis=-1, keepdims=True)
    return zc * lax.rsqrt(var + LN_EPS) * g + b


def _merge_kernel(a_ref, b_ref, ga_ref, gb_ref, x_ref, wa_ref, wb_ref, wo_ref, g_ref, beta_ref,
                  wr_ref, br_ref, h_ref, eidx_ref, gate_ref, pos_ref, cnt_ref, carry_ref, *, alpha):
    i = pl.program_id(0)

    @pl.when(i == 0)
    def _():
        carry_ref[...] = jnp.zeros_like(carry_ref)

    ma = jnp.dot(a_ref[...], wa_ref[...], preferred_element_type=F32)
    mb = jnp.dot(b_ref[...], wb_ref[...], preferred_element_type=F32)
    merged = jax.nn.sigmoid(ga_ref[...]) * ma + jax.nn.sigmoid(gb_ref[...]) * mb
    y = jnp.dot(merged.astype(BF16), wo_ref[...], preferred_element_type=F32)
    h = _layer_norm(alpha * x_ref[...] + y, g_ref[...], beta_ref[...])
    h_ref[...] = h

    wr = wr_ref[...]
    h_hi = h.astype(BF16)
    h_lo = (h - h_hi.astype(F32)).astype(BF16)
    wr_hi = wr.astype(BF16)
    wr_lo = (wr - wr_hi.astype(F32)).astype(BF16)
    logits = (jnp.dot(h_hi, wr_hi, preferred_element_type=F32) + jnp.dot(h_lo, wr_hi, preferred_element_type=F32)
              + jnp.dot(h_hi, wr_lo, preferred_element_type=F32) + br_ref[...])
    tm, ne = logits.shape
    lane = lax.broadcasted_iota(I32, (tm, ne), 1)
    lane_k = lax.broadcasted_iota(I32, (tm, TOP_K), 1)
    work = logits
    vals, sels = [], []
    eidx = jnp.zeros((tm, TOP_K), I32)
    onehot = jnp.zeros((tm, ne), F32)
    for k in range(TOP_K):
        mv = jnp.max(work, axis=1, keepdims=True)
        idx = jnp.min(jnp.where(work == mv, lane, ne), axis=1, keepdims=True)
        sel = lane == idx
        vals.append(mv)
        sels.append(sel)
        eidx = jnp.where(lane_k == k, idx, eidx)
        onehot = onehot + jnp.where(sel, 1.0, 0.0)
        work = jnp.where(sel, -jnp.inf, work)
    exps = [jnp.exp(v - vals[0]) for v in vals]
    denom = exps[0] + exps[1] + exps[2] + exps[3]
    gates = jnp.zeros((tm, TOP_K), F32)
    for k in range(TOP_K):
        gates = jnp.where(lane_k == k, exps[k] / denom, gates)

    r_i = lax.broadcasted_iota(I32, (tm, tm), 0)
    c_i = lax.broadcasted_iota(I32, (tm, tm), 1)
    lower = jnp.where(c_i < r_i, 1.0, 0.0).astype(BF16)
    rank = carry_ref[...] + jnp.dot(lower, onehot.astype(BF16), preferred_element_type=F32)
    pos = jnp.zeros((tm, TOP_K), I32)
    for k in range(TOP_K):
        pk = jnp.sum(jnp.where(sels[k], rank, 0.0), axis=1, keepdims=True).astype(I32)
        pos = jnp.where(lane_k == k, pk, pos)
    carry_ref[...] = carry_ref[...] + jnp.sum(onehot, axis=0, keepdims=True)

    eidx_ref[...] = eidx
    gate_ref[...] = gates
    pos_ref[...] = pos
    cnt_ref[...] = carry_ref[...].astype(I32)


def _merge(a_out, b_out, pg, x2d, wa, wb, wo, ln_g, ln_b, w_router, b_router, alpha):
    s, d = x2d.shape
    tm = min(256, s)
    full = lambda shape: pl.BlockSpec(shape, lambda i: (0,) * len(shape))
    return pl.pallas_call(
        functools.partial(_merge_kernel, alpha=alpha),
        grid=(s // tm,),
        in_specs=[
            pl.BlockSpec((tm, MIX_A), lambda i: (i, 0)),
            pl.BlockSpec((tm, MIX_B), lambda i: (i, 0)),
            pl.BlockSpec((tm, d), lambda i: (i, COL_GA // d)),
            pl.BlockSpec((tm, d), lambda i: (i, COL_GB // d)),
            pl.BlockSpec((tm, d), lambda i: (i, 0)),
            full((MIX_A, d)), full((MIX_B, d)), full((d, d)),
            full((1, d)), full((1, d)), full((d, N_EXPERTS)), full((1, N_EXPERTS)),
        ],
        out_specs=[
            pl.BlockSpec((tm, d), lambda i: (i, 0)),
            pl.BlockSpec((tm, TOP_K), lambda i: (i, 0)),
            pl.BlockSpec((tm, TOP_K), lambda i: (i, 0)),
            pl.BlockSpec((tm, TOP_K), lambda i: (i, 0)),
            full((1, N_EXPERTS)),
        ],
        out_shape=[
            jax.ShapeDtypeStruct((s, d), F32),
            jax.ShapeDtypeStruct((s, TOP_K), I32),
            jax.ShapeDtypeStruct((s, TOP_K), F32),
            jax.ShapeDtypeStruct((s, TOP_K), I32),
            jax.ShapeDtypeStruct((1, N_EXPERTS), I32),
        ],
        scratch_shapes=[pltpu.VMEM((1, N_EXPERTS), F32)],
        compiler_params=_cparams(("arbitrary",), "merge_ln_router"),
        name="merge_ln_router",
    )(a_out, b_out, pg, pg, x2d, wa, wb, wo, ln_g, ln_b, w_router, b_router)


def _dispatch_kernel(dest_ref, gap_start_ref, gap_len_ref, total_ref, h_ref, xs_ref, stage_ref, zero_ref,
                     sem, zsem, *, max_slack):
    i = pl.program_id(0)
    last = pl.num_programs(0) - 1
    slot = lax.rem(i, 2)
    n = dest_ref.shape[0]

    @pl.when(i == 0)
    def _():
        zero_ref[...] = jnp.zeros_like(zero_ref)

        def gap_copy(e, k):
            return pltpu.make_async_copy(zero_ref.at[pl.ds(0, 1)], xs_ref.at[pl.ds(gap_start_ref[e] + k, 1)], zsem)

        n_slots = xs_ref.shape[0]

        def slack_copy(k):
            row = pl.multiple_of(total_ref[0] + k * MOE_CHUNK, MOE_CHUNK)
            return pltpu.make_async_copy(zero_ref, xs_ref.at[pl.ds(row, MOE_CHUNK)], zsem)

        def slack(action):
            for k in range(max_slack):
                @pl.when(total_ref[0] + k * MOE_CHUNK < n_slots)
                def _():
                    action(slack_copy(k))

        def issue(e, _):
            lax.fori_loop(0, gap_len_ref[e], lambda k, c: (gap_copy(e, k).start(), c)[1], 0)
            return 0

        def settle(e, _):
            lax.fori_loop(0, gap_len_ref[e], lambda k, c: (gap_copy(e, k).wait(), c)[1], 0)
            return 0

        lax.fori_loop(0, N_EXPERTS, issue, 0)
        slack(lambda cp: cp.start())
        lax.fori_loop(0, N_EXPERTS, settle, 0)
        slack(lambda cp: cp.wait())

    def drain(sl):
        pltpu.make_async_copy(xs_ref.at[pl.ds(0, n)], xs_ref.at[pl.ds(0, n)], sem.at[sl]).wait()

    @pl.when(i >= 2)
    def _():
        drain(slot)

    stage_ref[slot] = h_ref[...]

    def scatter(sl):
        def start(g, _):
            base = pl.multiple_of(g * ROW_TILE, ROW_TILE)
            for u in range(ROW_TILE * TOP_K):
                r = u // TOP_K
                pltpu.make_async_copy(stage_ref.at[sl, pl.ds(base + r, 1)],
                                      xs_ref.at[pl.ds(dest_ref[g * (ROW_TILE * TOP_K) + u], 1)],
                                      sem.at[sl]).start(priority=u % 2)
            return 0
        lax.fori_loop(0, n // (ROW_TILE * TOP_K), start, 0)

    for parity in range(2):
        @pl.when(slot == parity)
        def _():
            scatter(parity)

    @pl.when(i == last)
    def _():
        drain(slot)

        @pl.when(i >= 1)
        def _():
            drain(1 - slot)


def _dispatch(h, dest_flat, gap_start, gap_len, total, n_slots):
    s, d = h.shape
    tm = min(128, s)
    smem = pl.BlockSpec(memory_space=pltpu.SMEM)
    max_slack = (n_slots - s * TOP_K + MOE_CHUNK - 1) // MOE_CHUNK
    return pl.pallas_call(
        functools.partial(_dispatch_kernel, max_slack=max_slack),
        grid=(s // tm,),
        in_specs=[
            pl.BlockSpec((tm * TOP_K,), lambda i: (i,), memory_space=pltpu.SMEM),
            smem, smem, smem,
            pl.BlockSpec((tm, d), lambda i: (i, 0)),
        ],
        out_specs=pl.BlockSpec(memory_space=pl.ANY),
        out_shape=jax.ShapeDtypeStruct((n_slots, d), h.dtype),
        scratch_shapes=[pltpu.VMEM((2, tm, d), h.dtype), pltpu.VMEM((MOE_CHUNK, d), h.dtype),
                        pltpu.SemaphoreType.DMA((2,)), pltpu.SemaphoreType.DMA(())],
        compiler_params=_cparams(("arbitrary",), "moe_dispatch"),
        name="moe_dispatch",
    )(dest_flat, gap_start, gap_len, total, h)


ROW_DMA_PRIORITY = 1


def _expert_rows_pipeline(n_chunks, in_copy, out_copy, compute):
    @pl.when(n_chunks > 0)
    def _():
        in_copy(0, 0).start(priority=ROW_DMA_PRIORITY)

        def body(c, _):
            slot = lax.rem(c, 2)
            in_copy(c, slot).wait()

            @pl.when(c + 1 < n_chunks)
            def _():
                in_copy(c + 1, 1 - slot).start(priority=ROW_DMA_PRIORITY)

            @pl.when(c >= 2)
            def _():
                out_copy(c - 2, slot).wait()

            compute(slot, c)
            out_copy(c, slot).start(priority=ROW_DMA_PRIORITY)
            return 0

        lax.fori_loop(0, n_chunks, body, 0)

        @pl.when(n_chunks >= 2)
        def _():
            out_copy(n_chunks - 2, lax.rem(n_chunks, 2)).wait()

        out_copy(n_chunks - 1, lax.rem(n_chunks - 1, 2)).wait()


GATE_UP_BAND = 256
DOWN_BAND = 512


def _weight_band_stream(w_hbm, w_stage, sem_w, e, band, n_bands):
    def band_copy(idx, b):
        cols = pl.ds(b * band, band)
        return pltpu.make_async_copy(w_hbm.at[idx, :, cols], w_stage.at[:, cols], sem_w.at[b])

    def start_bands(idx, bands):
        for b in bands:
            band_copy(idx, b).start(priority=b % 2)

    def await_bands(bands):
        for b in bands:
            band_copy(e, b).wait()

    def refill_bands(bands):
        @pl.when(e + 1 < N_EXPERTS)
        def _():
            start_bands(e + 1, bands)

    @pl.when(e == 0)
    def _():
        start_bands(0, range(n_bands))

    return await_bands, refill_bands


def _chunk_rows(start_ref, e, c):
    return pl.ds(pl.multiple_of(start_ref[e] + c * MOE_CHUNK, MOE_ALIGN), MOE_CHUNK)


def _gate_up_kernel(start_ref, cnt_ref, x_hbm, w_hbm, b_ref, act_hbm,
                    w_stage, w_sc, xbuf, obuf, sem_in, sem_out, sem_w):
    e = pl.program_id(0)
    n_chunks = (cnt_ref[e] + MOE_CHUNK - 1) // MOE_CHUNK
    d_ff = obuf.shape[2]

    cw = GATE_UP_BAND
    n_bands = 2 * d_ff // cw
    has_weights = e < N_EXPERTS
    await_bands, refill = _weight_band_stream(w_hbm, w_stage, sem_w, e, cw, n_bands)

    def in_copy(c, slot):
        return pltpu.make_async_copy(x_hbm.at[_chunk_rows(start_ref, e, c)], xbuf.at[slot], sem_in.at[slot])

    def out_copy(c, slot):
        return pltpu.make_async_copy(obuf.at[slot], act_hbm.at[_chunk_rows(start_ref, e, c)],
                                     sem_out.at[slot])

    def rows_times_weights(slot, cast_first):
        x = xbuf[slot].astype(BF16)
        for c in range(d_ff // cw):
            gs = slice(c * cw, (c + 1) * cw)
            us = slice(d_ff + c * cw, d_ff + (c + 1) * cw)
            if cast_first:
                bands = (c, d_ff // cw + c)
                await_bands(bands)
                w_sc[:, gs] = w_stage[:, gs].astype(BF16)
                w_sc[:, us] = w_stage[:, us].astype(BF16)
                refill(bands)
            g = jnp.dot(x, w_sc[:, gs], preferred_element_type=F32) + b_ref[:, gs]
            u = jnp.dot(x, w_sc[:, us], preferred_element_type=F32) + b_ref[:, us]
            g = jnp.minimum(g, SWIGLU_LIMIT)
            u = jnp.clip(u, -SWIGLU_LIMIT, SWIGLU_LIMIT)
            obuf[slot, :, gs] = (g * jax.nn.sigmoid(SWIGLU_ALPHA * g) * (u + 1.0)).astype(obuf.dtype)

    def compute(slot, c):
        fresh = (c == 0) & has_weights

        @pl.when(fresh)
        def _():
            rows_times_weights(slot, True)

        @pl.when(jnp.logical_not(fresh))
        def _():
            rows_times_weights(slot, False)

    @pl.when((n_chunks == 0) & has_weights)
    def _():
        await_bands(range(n_bands))
        refill(range(n_bands))

    _expert_rows_pipeline(n_chunks, in_copy, out_copy, compute)


def _down_kernel(start_ref, cnt_ref, a_hbm, w_hbm, bd_ref, y_hbm, w_stage, wd_sc, abuf, obuf, sem_in, sem_out,
                 sem_w):
    e = pl.program_id(0)
    n_chunks = (cnt_ref[e] + MOE_CHUNK - 1) // MOE_CHUNK

    cw = DOWN_BAND
    n_bands = obuf.shape[2] // cw
    has_weights = e < N_EXPERTS
    await_bands, refill = _weight_band_stream(w_hbm, w_stage, sem_w, e, cw, n_bands)

    @pl.when((n_chunks == 0) & has_weights)
    def _():
        await_bands(range(n_bands))
        refill(range(n_bands))

    def in_copy(c, slot):
        return pltpu.make_async_copy(a_hbm.at[_chunk_rows(start_ref, e, c)], abuf.at[slot],
                                     sem_in.at[slot])

    def out_copy(c, slot):
        return pltpu.make_async_copy(obuf.at[slot], y_hbm.at[_chunk_rows(start_ref, e, c)],
                                     sem_out.at[slot])

    def rows_times_weights(slot, cast_first):
        for b in range(n_bands):
            cs = slice(b * cw, (b + 1) * cw)
            if cast_first:
                await_bands((b,))
                wd_sc[:, cs] = w_stage[:, cs].astype(BF16)
                refill((b,))
            obuf[slot, :, cs] = jnp.dot(abuf[slot], wd_sc[:, cs], preferred_element_type=F32) + bd_ref[:, cs]

    def compute(slot, c):
        fresh = (c == 0) & has_weights

        @pl.when(fresh)
        def _():
            rows_times_weights(slot, True)

        @pl.when(jnp.logical_not(fresh))
        def _():
            rows_times_weights(slot, False)

    _expert_rows_pipeline(n_chunks, in_copy, out_copy, compute)


def _experts(xs, starts, counts, w_gate_up, b_gate_up, w_down, b_down):
    n_slots, d = xs.shape
    any_spec = pl.BlockSpec(memory_space=pl.ANY)
    dma_sems = pltpu.SemaphoreType.DMA((2,))
    bgu = b_gate_up.reshape(N_EXPERTS, 1, 2 * D_FF)
    n_groups = starts.shape[0]
    wi = lambda e: jnp.minimum(e, N_EXPERTS - 1)
    act = pl.pallas_call(
        _gate_up_kernel,
        grid_spec=pltpu.PrefetchScalarGridSpec(
            num_scalar_prefetch=2,
            grid=(n_groups,),
            in_specs=[
                any_spec,
                any_spec,
                pl.BlockSpec((None, 1, 2 * D_FF), lambda e, st, ct: (wi(e), 0, 0)),
            ],
            out_specs=any_spec,
            scratch_shapes=[pltpu.VMEM((d, 2 * D_FF), F32), pltpu.VMEM((d, 2 * D_FF), BF16),
                            pltpu.VMEM((2, MOE_CHUNK, d), xs.dtype), pltpu.VMEM((2, MOE_CHUNK, D_FF), BF16),
                            dma_sems, dma_sems, pltpu.SemaphoreType.DMA((2 * D_FF // GATE_UP_BAND,))],
        ),
        out_shape=jax.ShapeDtypeStruct((n_slots, D_FF), BF16),
        compiler_params=_cparams(("arbitrary",), "moe_gate_up"),
        name="moe_gate_up",
    )(starts, counts, xs, w_gate_up, bgu)
    bd = b_down.reshape(N_EXPERTS, 1, d)
    ys = pl.pallas_call(
        _down_kernel,
        grid_spec=pltpu.PrefetchScalarGridSpec(
            num_scalar_prefetch=2,
            grid=(n_groups,),
            in_specs=[
                any_spec,
                any_spec,
                pl.BlockSpec((None, 1, d), lambda e, st, ct: (wi(e), 0, 0)),
            ],
            out_specs=any_spec,
            scratch_shapes=[pltpu.VMEM((D_FF, d), F32), pltpu.VMEM((D_FF, d), BF16),
                            pltpu.VMEM((2, MOE_CHUNK, D_FF), BF16), pltpu.VMEM((2, MOE_CHUNK, d), F32),
                            dma_sems, dma_sems, pltpu.SemaphoreType.DMA((d // DOWN_BAND,))],
        ),
        out_shape=jax.ShapeDtypeStruct((n_slots, d), F32),
        compiler_params=_cparams(("arbitrary",), "moe_down"),
        name="moe_down",
    )(starts, counts, act, w_down, bd)
    return ys


def _combine_kernel(dest_ref, dest_next_ref, gate_ref, h_ref, g_ref, beta_ref, ys_ref, o_ref, buf_ref, sem,
                    *, alpha):
    i = pl.program_id(0)
    last = pl.num_programs(0) - 1
    slot = lax.rem(i, 2)
    n = dest_ref.shape[0]

    def gather(dref, sl):
        def start(g, _):
            base = pl.multiple_of(g * ROW_TILE, ROW_TILE)
            for u in range(ROW_TILE * TOP_K):
                r, k = divmod(u, TOP_K)
                pltpu.make_async_copy(ys_ref.at[pl.ds(dref[g * (ROW_TILE * TOP_K) + u], 1)],
                                      buf_ref.at[sl, k, pl.ds(base + r, 1)],
                                      sem.at[sl]).start(priority=u % 2)
            return 0
        lax.fori_loop(0, n // (ROW_TILE * TOP_K), start, 0)

    for parity in range(2):
        @pl.when((i == 0) & (slot == parity))
        def _():
            gather(dest_ref, parity)

        @pl.when((i < last) & (slot == parity))
        def _():
            gather(dest_next_ref, 1 - parity)

    pltpu.make_async_copy(buf_ref.at[slot], buf_ref.at[slot], sem.at[slot]).wait()
    gates = gate_ref[...]
    y = gates[:, 0:1] * buf_ref[slot, 0]
    for k in range(1, TOP_K):
        y = y + gates[:, k:k + 1] * buf_ref[slot, k]
    o_ref[...] = _layer_norm(alpha * h_ref[...] + y, g_ref[...], beta_ref[...])


def _combine(ys, dest_flat, gates, h, ln_g, ln_b, alpha):
    s, d = h.shape
    tm = min(128, s)
    n_steps = s // tm
    return pl.pallas_call(
        functools.partial(_combine_kernel, alpha=alpha),
        grid=(n_steps,),
        in_specs=[
            pl.BlockSpec((tm * TOP_K,), lambda i: (i,), memory_space=pltpu.SMEM),
            pl.BlockSpec((tm * TOP_K,), lambda i: (jnp.minimum(i + 1, n_steps - 1),), memory_space=pltpu.SMEM),
            pl.BlockSpec((tm, TOP_K), lambda i: (i, 0)),
            pl.BlockSpec((tm, d), lambda i: (i, 0)),
            pl.BlockSpec((1, d), lambda i: (0, 0)),
            pl.BlockSpec((1, d), lambda i: (0, 0)),
            pl.BlockSpec(memory_space=pl.ANY),
        ],
        out_specs=pl.BlockSpec((tm, d), lambda i: (i, 0)),
        out_shape=jax.ShapeDtypeStruct((s, d), F32),
        scratch_shapes=[pltpu.VMEM((2, TOP_K, tm, d), F32), pltpu.SemaphoreType.DMA((2,))],
        compiler_params=_cparams(("arbitrary",), "moe_combine"),
        name="moe_combine",
    )(dest_flat, dest_flat, gates, h, ln_g, ln_b, ys)


def _layer(x2d, w_in, b_forget, w_branch_a, w_branch_b, w_out, ln1_g, ln1_b, w_router, b_router,
           w_gate_up, b_gate_up, w_down, b_down, ln2_g, ln2_b, tables):
    s, d = x2d.shape
    alpha = (2.0 * DEPTH) ** 0.25
    scale = HEAD_DIM ** -0.5 * LOG2E

    w_bf, w_g = _prepare_w_in(w_in.T, scale)
    pbf, pg = _project(x2d, w_bf, tables, w_g)

    fox_tq = min(512, s)
    fox_tk = min(512, s)
    af_t = pg[:, COL_SMALL:COL_SMALL + FOX_HEADS].T
    c_t = _forget_cumsum(af_t, b_forget)
    c_tiles = c_t.reshape(FOX_HEADS, s // fox_tk, 1, fox_tk)
    a_out = _fox_attention(pbf, c_tiles, s, fox_tq, fox_tk, 4)

    b_out = _dsa_attention(pbf, pg, s, min(256, s), min(512, s))

    h, eidx, gates, pos, counts = _merge(
        a_out, b_out, pg, x2d, w_branch_a.astype(BF16), w_branch_b.astype(BF16), w_out.astype(BF16),
        ln1_g.reshape(1, d), ln1_b.reshape(1, d), w_router, b_router.reshape(1, N_EXPERTS), alpha)

    counts = counts.reshape(N_EXPERTS).astype(I32)
    aligned = (counts + MOE_ALIGN - 1) // MOE_ALIGN * MOE_ALIGN
    ends = jnp.cumsum(aligned).astype(I32)
    starts = ends - aligned
    bound = s * TOP_K + N_EXPERTS * MOE_ALIGN + 2 * MOE_CHUNK
    n_slots = (bound + MOE_CHUNK - 1) // MOE_CHUNK * MOE_CHUNK
    experts = jnp.arange(N_EXPERTS, dtype=I32)
    start_of = jnp.sum(jnp.where(eidx[..., None] == experts, starts, 0), axis=-1)
    dest = (start_of + pos).reshape(s * TOP_K).astype(I32)
    total = (ends[-1:] + MOE_CHUNK - 1) // MOE_CHUNK * MOE_CHUNK
    gap_len = aligned - counts + jnp.where(experts == N_EXPERTS - 1, total[0] - ends[-1], 0)
    group_starts = jnp.concatenate([starts, total])
    group_counts = jnp.concatenate([counts + jnp.where(experts == N_EXPERTS - 1, gap_len, 0), n_slots - total])

    xs = _dispatch(h, dest, starts + counts, gap_len, total, n_slots)
    ys = _experts(xs, group_starts, group_counts, w_gate_up, b_gate_up, w_down, b_down)
    return _combine(ys, dest, gates, h, ln2_g.reshape(1, d), ln2_b.reshape(1, d), alpha)


def kernel(x, w_in, b_forget, w_branch_a, w_branch_b, w_out, ln1_g, ln1_b, w_router, b_router,
           w_gate_up, b_gate_up, w_down, b_down, ln2_g, ln2_b):
    bsz, s, d = x.shape
    tables = _rope_tables(s)
    outs = []
    for bi in range(bsz):
        xb = x[bi]
        for l in range(DEPTH):
            xb = _layer(xb, w_in[l], b_forget[l], w_branch_a[l], w_branch_b[l], w_out[l], ln1_g[l], ln1_b[l],
                        w_router[l], b_router[l], w_gate_up[l], b_gate_up[l], w_down[l], b_down[l],
                        ln2_g[l], ln2_b[l], tables)
        outs.append(xb)
    return outs[0][None] if bsz == 1 else jnp.stack(outs)
```

```python
import functools

import numpy as np
import jax
import jax.numpy as jnp
from jax import lax
from jax.experimental import pallas as pl
from jax.experimental.pallas import tpu as pltpu

F32 = jnp.float32
BF16 = jnp.bfloat16
I32 = jnp.int32

D_MODEL = 2048
DEPTH = 1
CHUNK = 64
HEAD_DIM = 128
FOX_HEADS = 8
DSA_HEADS = 8
DSA_KV_HEADS = 2
DSA_GROUP = DSA_HEADS // DSA_KV_HEADS
IDX_HEADS = 16
IDX_DIM = 64
IDX_STACK = 4
TOPK_MAX = 256
ROPE_THETA = 500000.0
ROT_FRACTION_DEN = 4
MIX_A = FOX_HEADS * HEAD_DIM
MIX_B = DSA_HEADS * HEAD_DIM
N_EXPERTS = 32
TOP_K = 4
D_FF = D_MODEL
SWIGLU_ALPHA = 1.702
SWIGLU_LIMIT = 7.0
LN_EPS = 1e-5

MIB = 1024 * 1024
NEG_BIG = -1e30
LOG2E = 1.4426950408889634
NT_DIMS = (((1,), (1,)), ((), ()))
INT_MIN = -(2 ** 31)

COL_AQ = 0
COL_AK = 1024
COL_AV = 2048
COL_BQ = 3072
COL_BK = 4096
COL_BV = 4352
COL_IQ = 4608
COL_IK = 5632
N_PBF = 5888
PROJ_TN = 256
ROPE_TABLE_W = 128
ROPE_HEAD_TILES = (12, 13, 14, 15, 16)
ROPE_IDX_TILES = (18, 19, 20, 21, 22)
COL_GA = 0
COL_GB = 2048
COL_SMALL = 4096
N_PG = 4224
PG_TN = 384

MOE_CHUNK = 256
MOE_ALIGN = 16
ROW_TILE = 8


V7X_VMEM_MIB = 64
VMEM_LIMIT_MIB = dict(
    w_in_prep=48, proj_bf16=60, proj_f32=60, forget_cumsum=32, fox_attention=56, dsa_attention=56,
    merge_ln_router=56, moe_dispatch=32, moe_gate_up=60, moe_down=56, moe_combine=32)
assert max(VMEM_LIMIT_MIB.values()) < V7X_VMEM_MIB


def _cparams(dims, name):
    return pltpu.CompilerParams(dimension_semantics=dims, vmem_limit_bytes=VMEM_LIMIT_MIB[name] * MIB)


IN_SIZES = (MIX_A, MIX_A, MIX_A, FOX_HEADS, MIX_B, DSA_KV_HEADS * HEAD_DIM, DSA_KV_HEADS * HEAD_DIM,
            IDX_HEADS * IDX_DIM, IDX_DIM, IDX_HEADS, D_MODEL, D_MODEL)
IN_OFF = dict(zip(("aq", "ak", "av", "af", "bq", "bk", "bv", "iq", "ik", "iw", "ga", "gb", "end"),
                  np.concatenate([[0], np.cumsum(IN_SIZES)]).tolist()))


def _prep_kernel(w_ref, wbf_ref, wg_ref, *, scale):
    o = IN_OFF
    width = w_ref.shape[1]

    def rows(a, b, mult=None):
        v = w_ref[a:b, :]
        return v if mult is None else v * mult

    wbf_ref[COL_AQ:COL_AK, :] = rows(o["aq"], o["ak"], scale).astype(BF16)
    wbf_ref[COL_AK:COL_BQ, :] = rows(o["ak"], o["af"]).astype(BF16)
    wbf_ref[COL_BQ:COL_BK, :] = rows(o["bq"], o["bk"], scale).astype(BF16)
    wbf_ref[COL_BK:COL_IK, :] = rows(o["bk"], o["ik"]).astype(BF16)
    wbf_ref[COL_IK:N_PBF, :] = jnp.concatenate(
        [rows(o["ik"], o["iw"]), jnp.zeros((N_PBF - COL_IK - IDX_DIM, width), F32)], axis=0).astype(BF16)
    wg_ref[COL_GA:COL_SMALL, :] = rows(o["ga"], o["end"]).astype(BF16)
    wg_ref[COL_SMALL:N_PG, :] = jnp.concatenate(
        [rows(o["af"], o["bq"]), rows(o["iw"], o["ga"]),
         jnp.zeros((N_PG - COL_SMALL - FOX_HEADS - IDX_HEADS, width), F32)], axis=0).astype(BF16)


def _prepare_w_in(w_in_t, scale):
    n_in, d = w_in_t.shape
    tc = 256
    return pl.pallas_call(
        functools.partial(_prep_kernel, scale=scale),
        grid=(d // tc,),
        in_specs=[pl.BlockSpec((n_in, tc), lambda i: (0, i))],
        out_specs=[pl.BlockSpec((N_PBF, tc), lambda i: (0, i)), pl.BlockSpec((N_PG, tc), lambda i: (0, i))],
        out_shape=[jax.ShapeDtypeStruct((N_PBF, d), BF16), jax.ShapeDtypeStruct((N_PG, d), BF16)],
        compiler_params=_cparams(("arbitrary",), "w_in_prep"),
        name="w_in_prep",
    )(w_in_t)


def _tile_in(j, tiles):
    cond = j == tiles[0]
    for t in tiles[1:]:
        cond = cond | (j == t)
    return cond


def _proj_rope_kernel(x_ref, w_ref, tab_ref, o_ref, xb_ref):
    j = pl.program_id(1)

    @pl.when(j == 0)
    def _():
        xb_ref[...] = x_ref[...].astype(BF16)

    acc = lax.dot_general(xb_ref[...], w_ref[...], NT_DIMS, preferred_element_type=F32)
    tn = acc.shape[1]
    is_head = _tile_in(j, ROPE_HEAD_TILES)
    is_idx = _tile_in(j, ROPE_IDX_TILES)

    def rope(shift):
        reps = tn // ROPE_TABLE_W
        c = jnp.tile(tab_ref[0, 0], (1, reps))
        s_prev = jnp.tile(tab_ref[0, 1], (1, reps))
        s_next = jnp.tile(tab_ref[0, 2], (1, reps))
        out = acc * c + pltpu.roll(acc, shift, 1) * s_prev + pltpu.roll(acc, tn - shift, 1) * s_next
        o_ref[...] = out.astype(o_ref.dtype)

    @pl.when(is_head)
    def _():
        rope(HEAD_DIM // ROT_FRACTION_DEN // 2)

    @pl.when(is_idx)
    def _():
        rope(IDX_DIM // ROT_FRACTION_DEN // 2)

    @pl.when(jnp.logical_not(is_head | is_idx))
    def _():
        o_ref[...] = acc.astype(o_ref.dtype)


def _proj_plain_kernel(x_ref, w_ref, o_ref, xb_ref):
    j = pl.program_id(1)

    @pl.when(j == 0)
    def _():
        xb_ref[...] = x_ref[...].astype(BF16)

    o_ref[...] = lax.dot_general(xb_ref[...], w_ref[...], NT_DIMS, preferred_element_type=F32).astype(o_ref.dtype)


def _rope_group(j):
    return jnp.where(_tile_in(j, ROPE_IDX_TILES), 1, 0)


def _project(x2d, w_bf, tables, w_g):
    s, d = x2d.shape
    tm = min(2048, s)
    pbf = pl.pallas_call(
        _proj_rope_kernel,
        grid=(s // tm, N_PBF // PROJ_TN),
        in_specs=[
            pl.BlockSpec((tm, d), lambda i, j: (i, 0)),
            pl.BlockSpec((PROJ_TN, d), lambda i, j: (j, 0)),
            pl.BlockSpec((1, 3, tm, ROPE_TABLE_W), lambda i, j: (_rope_group(j), 0, i, 0)),
        ],
        out_specs=pl.BlockSpec((tm, PROJ_TN), lambda i, j: (i, j)),
        out_shape=jax.ShapeDtypeStruct((s, N_PBF), BF16),
        scratch_shapes=[pltpu.VMEM((tm, d), BF16)],
        compiler_params=_cparams(("arbitrary", "arbitrary"), "proj_bf16"),
        name="proj_bf16",
    )(x2d, w_bf, tables)
    pg = pl.pallas_call(
        _proj_plain_kernel,
        grid=(s // tm, N_PG // PG_TN),
        in_specs=[
            pl.BlockSpec((tm, d), lambda i, j: (i, 0)),
            pl.BlockSpec((PG_TN, d), lambda i, j: (j, 0)),
        ],
        out_specs=pl.BlockSpec((tm, PG_TN), lambda i, j: (i, j)),
        out_shape=jax.ShapeDtypeStruct((s, N_PG), F32),
        scratch_shapes=[pltpu.VMEM((tm, d), BF16)],
        compiler_params=_cparams(("arbitrary", "arbitrary"), "proj_f32"),
        name="proj_f32",
    )(x2d, w_g)
    return pbf, pg


def _rope_tables(s):
    pos = jnp.arange(s, dtype=F32)

    def one(period):
        rot = period // ROT_FRACTION_DEN
        half = rot // 2
        inv = jnp.power(ROPE_THETA, -jnp.arange(0, rot, 2, dtype=F32) / rot)
        ang = pos[:, None] * inv[None, :]
        cos, sin = jnp.cos(ang), jnp.sin(ang)
        zero = jnp.zeros((s, period - rot), F32)
        c = jnp.concatenate([cos, cos, jnp.ones((s, period - rot), F32)], axis=1)
        s_prev = jnp.concatenate([jnp.zeros((s, half), F32), sin, zero], axis=1)
        s_next = jnp.concatenate([-sin, jnp.zeros((s, half), F32), zero], axis=1)
        reps = ROPE_TABLE_W // period
        return jnp.stack([jnp.tile(c, (1, reps)), jnp.tile(s_prev, (1, reps)), jnp.tile(s_next, (1, reps))])

    return jnp.stack([one(HEAD_DIM), one(IDX_DIM)])


def _cumsum_kernel(af_ref, bf_ref, c_ref, carry_ref):
    i = pl.program_id(0)

    @pl.when(i == 0)
    def _():
        carry_ref[...] = jnp.zeros_like(carry_ref)

    z = af_ref[...] + bf_ref[...]
    logf = jnp.minimum(z, 0.0) - jnp.log1p(jnp.exp(-jnp.abs(z)))
    t = z.shape[1]
    row = lax.broadcasted_iota(I32, (t, t), 0)
    col = lax.broadcasted_iota(I32, (t, t), 1)
    upper = (row <= col).astype(F32)
    c = jnp.dot(logf, upper, preferred_element_type=F32, precision=lax.Precision.HIGHEST) + carry_ref[...]
    c_ref[...] = c * LOG2E
    carry_ref[...] = c[:, t - 1:t]


def _forget_cumsum(af_t, b_forget):
    h, s = af_t.shape
    t = min(512, s)
    return pl.pallas_call(
        _cumsum_kernel,
        grid=(s // t,),
        in_specs=[pl.BlockSpec((h, t), lambda i: (0, i)), pl.BlockSpec((h, 1), lambda i: (0, 0))],
        out_specs=pl.BlockSpec((h, t), lambda i: (0, i)),
        out_shape=jax.ShapeDtypeStruct((h, s), F32),
        scratch_shapes=[pltpu.VMEM((h, 1), F32)],
        compiler_params=_cparams(("arbitrary",), "forget_cumsum"),
        name="forget_cumsum",
    )(af_t, b_forget.reshape(h, 1).astype(F32))


def _softmax_init(m_ref, l_ref, acc_ref):
    m_ref[...] = jnp.full(m_ref.shape, NEG_BIG, F32)
    l_ref[...] = jnp.zeros(l_ref.shape, F32)
    acc_ref[...] = jnp.zeros(acc_ref.shape, F32)


def _softmax_tile(s, v, m_ref, l_ref, acc_ref, i):
    reps = s.shape[1] // 128
    m_old = m_ref[i]
    m_new = jnp.maximum(m_old, jnp.max(s, axis=1, keepdims=True))
    p = jnp.exp2(s - jnp.tile(m_new, (1, reps)))
    alpha = jnp.exp2(m_old - m_new)
    psum = p[:, :128]
    for c in range(1, reps):
        psum = psum + p[:, c * 128:(c + 1) * 128]
    l_ref[i] = alpha * l_ref[i] + psum
    acc_ref[i] = alpha * acc_ref[i] + jnp.dot(p.astype(BF16), v, preferred_element_type=F32)
    m_ref[i] = m_new


def _softmax_result(l_ref, acc_ref, i):
    return acc_ref[i] / jnp.sum(l_ref[i], axis=1, keepdims=True)


def _fox_kernel(q_ref, k_ref, v_ref, c_ref, o_ref, m_ref, l_ref, acc_ref, *, tq, tk, nh):
    qi = pl.program_id(1)
    _softmax_init(m_ref, l_ref, acc_ref)

    def tile(kt, masked):
        start = pl.multiple_of(kt * tk, tk)
        if masked:
            row = qi * tq + lax.broadcasted_iota(I32, (tq, tk), 0)
            col = kt * tk + lax.broadcasted_iota(I32, (tq, tk), 1)
            causal = col <= row
        for h in range(nh):
            hs = slice(h * HEAD_DIM, (h + 1) * HEAD_DIM)
            k = k_ref[pl.ds(start, tk), hs]
            v = v_ref[pl.ds(start, tk), hs]
            s = lax.dot_general(q_ref[:, hs], k, (((1,), (1,)), ((), ())), preferred_element_type=F32)
            s = s - c_ref[h, kt]
            if masked:
                s = jnp.where(causal, s, NEG_BIG)
            _softmax_tile(s, v, m_ref, l_ref, acc_ref, h)

    n_full = (qi * tq) // tk

    def body(kt, _):
        tile(kt, False)
        return 0

    lax.fori_loop(0, n_full, body, 0)
    for t in range((tq + tk - 1) // tk):
        tile(n_full + t, True)
    for h in range(nh):
        o_ref[:, h * HEAD_DIM:(h + 1) * HEAD_DIM] = _softmax_result(l_ref, acc_ref, h).astype(o_ref.dtype)


def _fox_attention(pbf, c_tiles, s, tq, tk, nh):
    cb = nh * HEAD_DIM
    return pl.pallas_call(
        functools.partial(_fox_kernel, tq=tq, tk=tk, nh=nh),
        grid=(FOX_HEADS // nh, s // tq),
        in_specs=[
            pl.BlockSpec((tq, cb), lambda g, i: (i, COL_AQ // cb + g)),
            pl.BlockSpec((s, cb), lambda g, i: (0, COL_AK // cb + g)),
            pl.BlockSpec((s, cb), lambda g, i: (0, COL_AV // cb + g)),
            pl.BlockSpec((nh, s // tk, 1, tk), lambda g, i: (g, 0, 0, 0)),
        ],
        out_specs=pl.BlockSpec((tq, cb), lambda g, i: (i, g)),
        out_shape=jax.ShapeDtypeStruct((s, MIX_A), BF16),
        scratch_shapes=[
            pltpu.VMEM((nh, tq, 128), F32),
            pltpu.VMEM((nh, tq, 128), F32),
            pltpu.VMEM((nh, tq, HEAD_DIM), F32),
        ],
        compiler_params=_cparams(("arbitrary", "arbitrary"), "fox_attention"),
        name="fox_attention",
    )(pbf, pbf, pbf, c_tiles)


def _dsa_kernel(bq_ref, iq_lo_ref, iq_hi_ref, iw_ref, ik_ref, bk_ref, bv_ref, o_ref,
                key_ref, cnt_ref, iqs_ref, qg_ref, m_ref, l_ref, acc_ref, *, tq, tk, n_sel):
    b = pl.program_id(0)
    n_tiles = (b * tq) // tk + 1
    row_g = b * tq + lax.broadcasted_iota(I32, (tq, tk), 0)
    adm_end = (row_g // CHUNK + 1) * CHUNK
    col_l = lax.broadcasted_iota(I32, (tq, tk), 1)

    idx_scale = (IDX_HEADS ** -0.5) * (IDX_DIM ** -0.5)
    iw = iw_ref[...][:, 8:8 + IDX_HEADS] * idx_scale
    iq = jnp.concatenate([iq_lo_ref[...], iq_hi_ref[...]], axis=1)
    for h in range(IDX_HEADS):
        iqs_ref[h * tq:(h + 1) * tq, :] = iq[:, h * IDX_DIM:(h + 1) * IDX_DIM]

    def score_tile(kt, _):
        start = pl.multiple_of(kt * tk, tk)
        ik = ik_ref[pl.ds(start, tk), :][:, :IDX_DIM]
        acc = jnp.zeros((tq, tk), F32)
        for g in range(IDX_HEADS // IDX_STACK):
            a = iqs_ref[g * IDX_STACK * tq:(g + 1) * IDX_STACK * tq, :]
            rel = lax.dot_general(a, ik, (((1,), (1,)), ((), ())), preferred_element_type=F32)
            for hh in range(IDX_STACK):
                h = g * IDX_STACK + hh
                acc = acc + jnp.maximum(rel[hh * tq:(hh + 1) * tq], 0.0) * iw[:, h:h + 1]
        key_ref[kt] = jnp.where(kt * tk + col_l < adm_end, acc, -jnp.inf)
        return 0

    lax.fori_loop(0, n_tiles, score_tile, 0)

    def float_of(code_u):
        code = code_u ^ INT_MIN
        return pltpu.bitcast(jnp.where(code >= 0, code, code ^ 0x7FFFFFFF), F32)

    def count(above, cand):
        cnt_ref[...] = jnp.zeros(cnt_ref.shape, I32)

        def body(kt, _):
            for r in range(tq // 128):
                rs = slice(r * 128, (r + 1) * 128)
                part = cnt_ref[rs, :]
                for c in range(tk // 128):
                    part = part + jnp.where(above(key_ref[kt, rs, c * 128:(c + 1) * 128], cand[rs]), 1, 0)
                cnt_ref[rs, :] = part
            return 0
        lax.fori_loop(0, n_tiles, body, 0)
        return jnp.sum(cnt_ref[...], axis=1, keepdims=True)

    def count_ge(cand):
        return count(lambda a, b: a >= b, cand)

    def search_cond(carry):
        i, _, _, pending = carry
        return (i < 32) & (pending > 0)

    def refine(i, t_u, hit):
        cand_u = t_u | lax.shift_left(jnp.int32(1), 31 - i)
        cnt = count_ge(float_of(cand_u))
        return jnp.where(cnt >= n_sel, cand_u, t_u), jnp.where(cnt == n_sel, 1, hit)

    def search_step(carry):
        i, t_u, hit, _ = carry
        t_u, hit = refine(i, t_u, hit)
        return i + 1, t_u, hit, jnp.sum(1 - hit)

    zeros = jnp.zeros((tq, 1), I32)
    lead = 12
    t_u, hit = lax.fori_loop(0, lead, lambda i, c: refine(i, *c), (zeros, zeros))
    _, t_u, hit, _ = lax.while_loop(search_cond, search_step, (jnp.int32(lead), t_u, hit, jnp.sum(1 - hit)))
    thr = jnp.where(t_u == 0, jnp.finfo(F32).min, float_of(t_u))
    tied_rows = jnp.sum(jnp.where((hit == 0) & (t_u != 0), 1, 0))

    for j in range(DSA_KV_HEADS):
        for g in range(DSA_GROUP):
            hd = j * DSA_GROUP + g
            qg_ref[j, g * tq:(g + 1) * tq, :] = bq_ref[:, hd * HEAD_DIM:(hd + 1) * HEAD_DIM]
    _softmax_init(m_ref, l_ref, acc_ref)

    def attend(kt, sel):
        start = pl.multiple_of(kt * tk, tk)
        for j in range(DSA_KV_HEADS):
            hs = slice(j * HEAD_DIM, (j + 1) * HEAD_DIM)
            k = bk_ref[pl.ds(start, tk), hs]
            v = bv_ref[pl.ds(start, tk), hs]
            s = lax.dot_general(qg_ref[j], k, (((1,), (1,)), ((), ())), preferred_element_type=F32)
            s = jnp.where(sel[None], s.reshape(DSA_GROUP, tq, tk), NEG_BIG).reshape(DSA_GROUP * tq, tk)
            _softmax_tile(s, v, m_ref, l_ref, acc_ref, j)

    @pl.when(tied_rows == 0)
    def _():
        def attn_tile(kt, _):
            attend(kt, key_ref[kt] >= thr)
            return 0
        lax.fori_loop(0, n_tiles, attn_tile, 0)

    @pl.when(tied_rows > 0)
    def _():
        ties_wanted = (n_sel - count(lambda a, b: a > b, thr)).astype(F32)
        r_i = lax.broadcasted_iota(I32, (tk, tk), 0)
        c_i = lax.broadcasted_iota(I32, (tk, tk), 1)
        tri = jnp.where(r_i <= c_i, 1.0, 0.0).astype(BF16)

        def attn_tile(kt, ties_seen):
            key = key_ref[kt]
            tie = key == thr
            rank = ties_seen + jnp.dot(jnp.where(tie, 1.0, 0.0).astype(BF16), tri, preferred_element_type=F32)
            attend(kt, (key > thr) | (tie & (rank <= ties_wanted)))
            return rank[:, tk - 1:tk]
        lax.fori_loop(0, n_tiles, attn_tile, jnp.zeros((tq, 1), F32))

    for j in range(DSA_KV_HEADS):
        o = _softmax_result(l_ref, acc_ref, j)
        for g in range(DSA_GROUP):
            hd = j * DSA_GROUP + g
            o_ref[:, hd * HEAD_DIM:(hd + 1) * HEAD_DIM] = o[g * tq:(g + 1) * tq].astype(o_ref.dtype)


def _dsa_attention(pbf, pg, s, tq, tk):
    n_sel = min(TOPK_MAX, s // 4)
    kvw = DSA_KV_HEADS * HEAD_DIM
    rows = DSA_GROUP * tq
    iqw = IDX_HEADS * IDX_DIM // 2
    return pl.pallas_call(
        functools.partial(_dsa_kernel, tq=tq, tk=tk, n_sel=n_sel),
        grid=(s // tq,),
        in_specs=[
            pl.BlockSpec((tq, MIX_B), lambda b: (b, COL_BQ // MIX_B)),
            pl.BlockSpec((tq, iqw), lambda b: (b, COL_IQ // iqw)),
            pl.BlockSpec((tq, iqw), lambda b: (b, COL_IQ // iqw + 1)),
            pl.BlockSpec((tq, 128), lambda b: (b, COL_SMALL // 128)),
            pl.BlockSpec((s, 128), lambda b: (0, COL_IK // 128)),
            pl.BlockSpec((s, kvw), lambda b: (0, COL_BK // kvw)),
            pl.BlockSpec((s, kvw), lambda b: (0, COL_BV // kvw)),
        ],
        out_specs=pl.BlockSpec((tq, MIX_B), lambda b: (b, 0)),
        out_shape=jax.ShapeDtypeStruct((s, MIX_B), BF16),
        scratch_shapes=[
            pltpu.VMEM((s // tk, tq, tk), F32),
            pltpu.VMEM((tq, 128), I32),
            pltpu.VMEM((IDX_HEADS * tq, IDX_DIM), BF16),
            pltpu.VMEM((DSA_KV_HEADS, rows, HEAD_DIM), BF16),
            pltpu.VMEM((DSA_KV_HEADS, rows, 128), F32),
            pltpu.VMEM((DSA_KV_HEADS, rows, 128), F32),
            pltpu.VMEM((DSA_KV_HEADS, rows, HEAD_DIM), F32),
        ],
        compiler_params=_cparams(("arbitrary",), "dsa_attention"),
        name="dsa_attention",
    )(pbf, pbf, pbf, pg, pbf, pbf, pbf)


def _layer_norm(z, g, b):
    mu = jnp.mean(z, axis=-1, keepdims=True)
    zc = z - mu
    var = jnp.mean(zc * zc, axis=-1, keepdims=True)
    return zc * lax.rsqrt(var + LN_EPS) * g + b


def _merge_kernel(a_ref, b_ref, ga_ref, gb_ref, x_ref, wa_ref, wb_ref, wo_ref, g_ref, beta_ref,
                  wr_ref, br_ref, h_ref, eidx_ref, gate_ref, pos_ref, cnt_ref, carry_ref, *, alpha):
    i = pl.program_id(0)

    @pl.when(i == 0)
    def _():
        carry_ref[...] = jnp.zeros_like(carry_ref)

    ma = jnp.dot(a_ref[...], wa_ref[...], preferred_element_type=F32)
    mb = jnp.dot(b_ref[...], wb_ref[...], preferred_element_type=F32)
    merged = jax.nn.sigmoid(ga_ref[...]) * ma + jax.nn.sigmoid(gb_ref[...]) * mb
    y = jnp.dot(merged.astype(BF16), wo_ref[...], preferred_element_type=F32)
    h = _layer_norm(alpha * x_ref[...] + y, g_ref[...], beta_ref[...])
    h_ref[...] = h

    wr = wr_ref[...]
    h_hi = h.astype(BF16)
    h_lo = (h - h_hi.astype(F32)).astype(BF16)
    wr_hi = wr.astype(BF16)
    wr_lo = (wr - wr_hi.astype(F32)).astype(BF16)
    logits = (jnp.dot(h_hi, wr_hi, preferred_element_type=F32) + jnp.dot(h_lo, wr_hi, preferred_element_type=F32)
              + jnp.dot(h_hi, wr_lo, preferred_element_type=F32) + br_ref[...])
    tm, ne = logits.shape
    lane = lax.broadcasted_iota(I32, (tm, ne), 1)
    lane_k = lax.broadcasted_iota(I32, (tm, TOP_K), 1)
    work = logits
    vals, sels = [], []
    eidx = jnp.zeros((tm, TOP_K), I32)
    onehot = jnp.zeros((tm, ne), F32)
    for k in range(TOP_K):
        mv = jnp.max(work, axis=1, keepdims=True)
        idx = jnp.min(jnp.where(work == mv, lane, ne), axis=1, keepdims=True)
        sel = lane == idx
        vals.append(mv)
        sels.append(sel)
        eidx = jnp.where(lane_k == k, idx, eidx)
        onehot = onehot + jnp.where(sel, 1.0, 0.0)
        work = jnp.where(sel, -jnp.inf, work)
    exps = [jnp.exp(v - vals[0]) for v in vals]
    denom = exps[0] + exps[1] + exps[2] + exps[3]
    gates = jnp.zeros((tm, TOP_K), F32)
    for k in range(TOP_K):
        gates = jnp.where(lane_k == k, exps[k] / denom, gates)

    r_i = lax.broadcasted_iota(I32, (tm, tm), 0)
    c_i = lax.broadcasted_iota(I32, (tm, tm), 1)
    lower = jnp.where(c_i < r_i, 1.0, 0.0).astype(BF16)
    rank = carry_ref[...] + jnp.dot(lower, onehot.astype(BF16), preferred_element_type=F32)
    pos = jnp.zeros((tm, TOP_K), I32)
    for k in range(TOP_K):
        pk = jnp.sum(jnp.where(sels[k], rank, 0.0), axis=1, keepdims=True).astype(I32)
        pos = jnp.where(lane_k == k, pk, pos)
    carry_ref[...] = carry_ref[...] + jnp.sum(onehot, axis=0, keepdims=True)

    eidx_ref[...] = eidx
    gate_ref[...] = gates
    pos_ref[...] = pos
    cnt_ref[...] = carry_ref[...].astype(I32)


def _merge(a_out, b_out, pg, x2d, wa, wb, wo, ln_g, ln_b, w_router, b_router, alpha):
    s, d = x2d.shape
    tm = min(256, s)
    full = lambda shape: pl.BlockSpec(shape, lambda i: (0,) * len(shape))
    return pl.pallas_call(
        functools.partial(_merge_kernel, alpha=alpha),
        grid=(s // tm,),
        in_specs=[
            pl.BlockSpec((tm, MIX_A), lambda i: (i, 0)),
            pl.BlockSpec((tm, MIX_B), lambda i: (i, 0)),
            pl.BlockSpec((tm, d), lambda i: (i, COL_GA // d)),
            pl.BlockSpec((tm, d), lambda i: (i, COL_GB // d)),
            pl.BlockSpec((tm, d), lambda i: (i, 0)),
            full((MIX_A, d)), full((MIX_B, d)), full((d, d)),
            full((1, d)), full((1, d)), full((d, N_EXPERTS)), full((1, N_EXPERTS)),
        ],
        out_specs=[
            pl.BlockSpec((tm, d), lambda i: (i, 0)),
            pl.BlockSpec((tm, TOP_K), lambda i: (i, 0)),
            pl.BlockSpec((tm, TOP_K), lambda i: (i, 0)),
            pl.BlockSpec((tm, TOP_K), lambda i: (i, 0)),
            full((1, N_EXPERTS)),
        ],
        out_shape=[
            jax.ShapeDtypeStruct((s, d), F32),
            jax.ShapeDtypeStruct((s, TOP_K), I32),
            jax.ShapeDtypeStruct((s, TOP_K), F32),
            jax.ShapeDtypeStruct((s, TOP_K), I32),
            jax.ShapeDtypeStruct((1, N_EXPERTS), I32),
        ],
        scratch_shapes=[pltpu.VMEM((1, N_EXPERTS), F32)],
        compiler_params=_cparams(("arbitrary",), "merge_ln_router"),
        name="merge_ln_router",
    )(a_out, b_out, pg, pg, x2d, wa, wb, wo, ln_g, ln_b, w_router, b_router)


def _dispatch_kernel(dest_ref, gap_start_ref, gap_len_ref, total_ref, h_ref, xs_ref, stage_ref, zero_ref,
                     sem, zsem, *, max_slack):
    i = pl.program_id(0)
    last = pl.num_programs(0) - 1
    slot = lax.rem(i, 2)
    n = dest_ref.shape[0]

    @pl.when(i == 0)
    def _():
        zero_ref[...] = jnp.zeros_like(zero_ref)

        def gap_copy(e, k):
            return pltpu.make_async_copy(zero_ref.at[pl.ds(0, 1)], xs_ref.at[pl.ds(gap_start_ref[e] + k, 1)], zsem)

        n_slots = xs_ref.shape[0]

        def slack_copy(k):
            row = pl.multiple_of(total_ref[0] + k * MOE_CHUNK, MOE_CHUNK)
            return pltpu.make_async_copy(zero_ref, xs_ref.at[pl.ds(row, MOE_CHUNK)], zsem)

        def slack(action):
            for k in range(max_slack):
                @pl.when(total_ref[0] + k * MOE_CHUNK < n_slots)
                def _():
                    action(slack_copy(k))

        def issue(e, _):
            lax.fori_loop(0, gap_len_ref[e], lambda k, c: (gap_copy(e, k).start(), c)[1], 0)
            return 0

        def settle(e, _):
            lax.fori_loop(0, gap_len_ref[e], lambda k, c: (gap_copy(e, k).wait(), c)[1], 0)
            return 0

        lax.fori_loop(0, N_EXPERTS, issue, 0)
        slack(lambda cp: cp.start())
        lax.fori_loop(0, N_EXPERTS, settle, 0)
        slack(lambda cp: cp.wait())

    def drain(sl):
        pltpu.make_async_copy(xs_ref.at[pl.ds(0, n)], xs_ref.at[pl.ds(0, n)], sem.at[sl]).wait()

    @pl.when(i >= 2)
    def _():
        drain(slot)

    stage_ref[slot] = h_ref[...]

    def scatter(sl):
        def start(g, _):
            base = pl.multiple_of(g * ROW_TILE, ROW_TILE)
            for u in range(ROW_TILE * TOP_K):
                r = u // TOP_K
                pltpu.make_async_copy(stage_ref.at[sl, pl.ds(base + r, 1)],
                                      xs_ref.at[pl.ds(dest_ref[g * (ROW_TILE * TOP_K) + u], 1)],
                                      sem.at[sl]).start(priority=u % 2)
            return 0
        lax.fori_loop(0, n // (ROW_TILE * TOP_K), start, 0)

    for parity in range(2):
        @pl.when(slot == parity)
        def _():
            scatter(parity)

    @pl.when(i == last)
    def _():
        drain(slot)

        @pl.when(i >= 1)
        def _():
            drain(1 - slot)


def _dispatch(h, dest_flat, gap_start, gap_len, total, n_slots):
    s, d = h.shape
    tm = min(128, s)
    smem = pl.BlockSpec(memory_space=pltpu.SMEM)
    max_slack = (n_slots - s * TOP_K + MOE_CHUNK - 1) // MOE_CHUNK
    return pl.pallas_call(
        functools.partial(_dispatch_kernel, max_slack=max_slack),
        grid=(s // tm,),
        in_specs=[
            pl.BlockSpec((tm * TOP_K,), lambda i: (i,), memory_space=pltpu.SMEM),
            smem, smem, smem,
            pl.BlockSpec((tm, d), lambda i: (i, 0)),
        ],
        out_specs=pl.BlockSpec(memory_space=pl.ANY),
        out_shape=jax.ShapeDtypeStruct((n_slots, d), h.dtype),
        scratch_shapes=[pltpu.VMEM((2, tm, d), h.dtype), pltpu.VMEM((MOE_CHUNK, d), h.dtype),
                        pltpu.SemaphoreType.DMA((2,)), pltpu.SemaphoreType.DMA(())],
        compiler_params=_cparams(("arbitrary",), "moe_dispatch"),
        name="moe_dispatch",
    )(dest_flat, gap_start, gap_len, total, h)


ROW_DMA_PRIORITY = 1


def _expert_rows_pipeline(n_chunks, in_copy, out_copy, compute):
    @pl.when(n_chunks > 0)
    def _():
        in_copy(0, 0).start(priority=ROW_DMA_PRIORITY)

        def body(c, _):
            slot = lax.rem(c, 2)
            in_copy(c, slot).wait()

            @pl.when(c + 1 < n_chunks)
            def _():
                in_copy(c + 1, 1 - slot).start(priority=ROW_DMA_PRIORITY)

            @pl.when(c >= 2)
            def _():
                out_copy(c - 2, slot).wait()

            compute(slot, c)
            out_copy(c, slot).start(priority=ROW_DMA_PRIORITY)
            return 0

        lax.fori_loop(0, n_chunks, body, 0)

        @pl.when(n_chunks >= 2)
        def _():
            out_copy(n_chunks - 2, lax.rem(n_chunks, 2)).wait()

        out_copy(n_chunks - 1, lax.rem(n_chunks - 1, 2)).wait()


GATE_UP_BAND = 256
DOWN_BAND = 512


def _weight_band_stream(w_hbm, w_stage, sem_w, e, band, n_bands):
    def band_copy(idx, b):
        cols = pl.ds(b * band, band)
        return pltpu.make_async_copy(w_hbm.at[idx, :, cols], w_stage.at[:, cols], sem_w.at[b])

    def start_bands(idx, bands):
        for b in bands:
            band_copy(idx, b).start(priority=b % 2)

    def await_bands(bands):
        for b in bands:
            band_copy(e, b).wait()

    def refill_bands(bands):
        @pl.when(e + 1 < N_EXPERTS)
        def _():
            start_bands(e + 1, bands)

    @pl.when(e == 0)
    def _():
        start_bands(0, range(n_bands))

    return await_bands, refill_bands


def _chunk_rows(start_ref, e, c):
    return pl.ds(pl.multiple_of(start_ref[e] + c * MOE_CHUNK, MOE_ALIGN), MOE_CHUNK)


def _gate_up_kernel(start_ref, cnt_ref, x_hbm, w_hbm, b_ref, act_hbm,
                    w_stage, w_sc, xbuf, obuf, sem_in, sem_out, sem_w):
    e = pl.program_id(0)
    n_chunks = (cnt_ref[e] + MOE_CHUNK - 1) // MOE_CHUNK
    d_ff = obuf.shape[2]

    cw = GATE_UP_BAND
    n_bands = 2 * d_ff // cw
    has_weights = e < N_EXPERTS
    await_bands, refill = _weight_band_stream(w_hbm, w_stage, sem_w, e, cw, n_bands)

    def in_copy(c, slot):
        return pltpu.make_async_copy(x_hbm.at[_chunk_rows(start_ref, e, c)], xbuf.at[slot], sem_in.at[slot])

    def out_copy(c, slot):
        return pltpu.make_async_copy(obuf.at[slot], act_hbm.at[_chunk_rows(start_ref, e, c)],
                                     sem_out.at[slot])

    def rows_times_weights(slot, cast_first):
        x = xbuf[slot].astype(BF16)
        for c in range(d_ff // cw):
            gs = slice(c * cw, (c + 1) * cw)
            us = slice(d_ff + c * cw, d_ff + (c + 1) * cw)
            if cast_first:
                bands = (c, d_ff // cw + c)
                await_bands(bands)
                w_sc[:, gs] = w_stage[:, gs].astype(BF16)
                w_sc[:, us] = w_stage[:, us].astype(BF16)
                refill(bands)
            g = jnp.dot(x, w_sc[:, gs], preferred_element_type=F32) + b_ref[:, gs]
            u = jnp.dot(x, w_sc[:, us], preferred_element_type=F32) + b_ref[:, us]
            g = jnp.minimum(g, SWIGLU_LIMIT)
            u = jnp.clip(u, -SWIGLU_LIMIT, SWIGLU_LIMIT)
            obuf[slot, :, gs] = (g * jax.nn.sigmoid(SWIGLU_ALPHA * g) * (u + 1.0)).astype(obuf.dtype)

    def compute(slot, c):
        fresh = (c == 0) & has_weights

        @pl.when(fresh)
        def _():
            rows_times_weights(slot, True)

        @pl.when(jnp.logical_not(fresh))
        def _():
            rows_times_weights(slot, False)

    @pl.when((n_chunks == 0) & has_weights)
    def _():
        await_bands(range(n_bands))
        refill(range(n_bands))

    _expert_rows_pipeline(n_chunks, in_copy, out_copy, compute)


def _down_kernel(start_ref, cnt_ref, a_hbm, w_hbm, bd_ref, y_hbm, w_stage, wd_sc, abuf, obuf, sem_in, sem_out,
                 sem_w):
    e = pl.program_id(0)
    n_chunks = (cnt_ref[e] + MOE_CHUNK - 1) // MOE_CHUNK

    cw = DOWN_BAND
    n_bands = obuf.shape[2] // cw
    has_weights = e < N_EXPERTS
    await_bands, refill = _weight_band_stream(w_hbm, w_stage, sem_w, e, cw, n_bands)

    @pl.when((n_chunks == 0) & has_weights)
    def _():
        await_bands(range(n_bands))
        refill(range(n_bands))

    def in_copy(c, slot):
        return pltpu.make_async_copy(a_hbm.at[_chunk_rows(start_ref, e, c)], abuf.at[slot],
                                     sem_in.at[slot])

    def out_copy(c, slot):
        return pltpu.make_async_copy(obuf.at[slot], y_hbm.at[_chunk_rows(start_ref, e, c)],
                                     sem_out.at[slot])

    def rows_times_weights(slot, cast_first):
        for b in range(n_bands):
            cs = slice(b * cw, (b + 1) * cw)
            if cast_first:
                await_bands((b,))
                wd_sc[:, cs] = w_stage[:, cs].astype(BF16)
                refill((b,))
            obuf[slot, :, cs] = jnp.dot(abuf[slot], wd_sc[:, cs], preferred_element_type=F32) + bd_ref[:, cs]

    def compute(slot, c):
        fresh = (c == 0) & has_weights

        @pl.when(fresh)
        def _():
            rows_times_weights(slot, True)

        @pl.when(jnp.logical_not(fresh))
        def _():
            rows_times_weights(slot, False)

    _expert_rows_pipeline(n_chunks, in_copy, out_copy, compute)


def _experts(xs, starts, counts, w_gate_up, b_gate_up, w_down, b_down):
    n_slots, d = xs.shape
    any_spec = pl.BlockSpec(memory_space=pl.ANY)
    dma_sems = pltpu.SemaphoreType.DMA((2,))
    bgu = b_gate_up.reshape(N_EXPERTS, 1, 2 * D_FF)
    n_groups = starts.shape[0]
    wi = lambda e: jnp.minimum(e, N_EXPERTS - 1)
    act = pl.pallas_call(
        _gate_up_kernel,
        grid_spec=pltpu.PrefetchScalarGridSpec(
            num_scalar_prefetch=2,
            grid=(n_groups,),
            in_specs=[
                any_spec,
                any_spec,
                pl.BlockSpec((None, 1, 2 * D_FF), lambda e, st, ct: (wi(e), 0, 0)),
            ],
            out_specs=any_spec,
            scratch_shapes=[pltpu.VMEM((d, 2 * D_FF), F32), pltpu.VMEM((d, 2 * D_FF), BF16),
                            pltpu.VMEM((2, MOE_CHUNK, d), xs.dtype), pltpu.VMEM((2, MOE_CHUNK, D_FF), BF16),
                            dma_sems, dma_sems, pltpu.SemaphoreType.DMA((2 * D_FF // GATE_UP_BAND,))],
        ),
        out_shape=jax.ShapeDtypeStruct((n_slots, D_FF), BF16),
        compiler_params=_cparams(("arbitrary",), "moe_gate_up"),
        name="moe_gate_up",
    )(starts, counts, xs, w_gate_up, bgu)
    bd = b_down.reshape(N_EXPERTS, 1, d)
    ys = pl.pallas_call(
        _down_kernel,
        grid_spec=pltpu.PrefetchScalarGridSpec(
            num_scalar_prefetch=2,
            grid=(n_groups,),
            in_specs=[
                any_spec,
                any_spec,
                pl.BlockSpec((None, 1, d), lambda e, st, ct: (wi(e), 0, 0)),
            ],
            out_specs=any_spec,
            scratch_shapes=[pltpu.VMEM((D_FF, d), F32), pltpu.VMEM((D_FF, d), BF16),
                            pltpu.VMEM((2, MOE_CHUNK, D_FF), BF16), pltpu.VMEM((2, MOE_CHUNK, d), F32),
                            dma_sems, dma_sems, pltpu.SemaphoreType.DMA((d // DOWN_BAND,))],
        ),
        out_shape=jax.ShapeDtypeStruct((n_slots, d), F32),
        compiler_params=_cparams(("arbitrary",), "moe_down"),
        name="moe_down",
    )(starts, counts, act, w_down, bd)
    return ys


def _combine_kernel(dest_ref, dest_next_ref, gate_ref, h_ref, g_ref, beta_ref, ys_ref, o_ref, buf_ref, sem,
                    *, alpha):
    i = pl.program_id(0)
    last = pl.num_programs(0) - 1
    slot = lax.rem(i, 2)
    n = dest_ref.shape[0]

    def gather(dref, sl):
        def start(g, _):
            base = pl.multiple_of(g * ROW_TILE, ROW_TILE)
            for u in range(ROW_TILE * TOP_K):
                r, k = divmod(u, TOP_K)
                pltpu.make_async_copy(ys_ref.at[pl.ds(dref[g * (ROW_TILE * TOP_K) + u], 1)],
                                      buf_ref.at[sl, k, pl.ds(base + r, 1)],
                                      sem.at[sl]).start(priority=u % 2)
            return 0
        lax.fori_loop(0, n // (ROW_TILE * TOP_K), start, 0)

    for parity in range(2):
        @pl.when((i == 0) & (slot == parity))
        def _():
            gather(dest_ref, parity)

        @pl.when((i < last) & (slot == parity))
        def _():
            gather(dest_next_ref, 1 - parity)

    pltpu.make_async_copy(buf_ref.at[slot], buf_ref.at[slot], sem.at[slot]).wait()
    gates = gate_ref[...]
    y = gates[:, 0:1] * buf_ref[slot, 0]
    for k in range(1, TOP_K):
        y = y + gates[:, k:k + 1] * buf_ref[slot, k]
    o_ref[...] = _layer_norm(alpha * h_ref[...] + y, g_ref[...], beta_ref[...])


def _combine(ys, dest_flat, gates, h, ln_g, ln_b, alpha):
    s, d = h.shape
    tm = min(128, s)
    n_steps = s // tm
    return pl.pallas_call(
        functools.partial(_combine_kernel, alpha=alpha),
        grid=(n_steps,),
        in_specs=[
            pl.BlockSpec((tm * TOP_K,), lambda i: (i,), memory_space=pltpu.SMEM),
            pl.BlockSpec((tm * TOP_K,), lambda i: (jnp.minimum(i + 1, n_steps - 1),), memory_space=pltpu.SMEM),
            pl.BlockSpec((tm, TOP_K), lambda i: (i, 0)),
            pl.BlockSpec((tm, d), lambda i: (i, 0)),
            pl.BlockSpec((1, d), lambda i: (0, 0)),
            pl.BlockSpec((1, d), lambda i: (0, 0)),
            pl.BlockSpec(memory_space=pl.ANY),
        ],
        out_specs=pl.BlockSpec((tm, d), lambda i: (i, 0)),
        out_shape=jax.ShapeDtypeStruct((s, d), F32),
        scratch_shapes=[pltpu.VMEM((2, TOP_K, tm, d), F32), pltpu.SemaphoreType.DMA((2,))],
        compiler_params=_cparams(("arbitrary",), "moe_combine"),
        name="moe_combine",
    )(dest_flat, dest_flat, gates, h, ln_g, ln_b, ys)


def _layer(x2d, w_in, b_forget, w_branch_a, w_branch_b, w_out, ln1_g, ln1_b, w_router, b_router,
           w_gate_up, b_gate_up, w_down, b_down, ln2_g, ln2_b, tables):
    s, d = x2d.shape
    alpha = (2.0 * DEPTH) ** 0.25
    scale = HEAD_DIM ** -0.5 * LOG2E

    w_bf, w_g = _prepare_w_in(w_in.T, scale)
    pbf, pg = _project(x2d, w_bf, tables, w_g)

    fox_tq = min(512, s)
    fox_tk = min(512, s)
    af_t = pg[:, COL_SMALL:COL_SMALL + FOX_HEADS].T
    c_t = _forget_cumsum(af_t, b_forget)
    c_tiles = c_t.reshape(FOX_HEADS, s // fox_tk, 1, fox_tk)
    a_out = _fox_attention(pbf, c_tiles, s, fox_tq, fox_tk, 4)

    b_out = _dsa_attention(pbf, pg, s, min(256, s), min(512, s))

    h, eidx, gates, pos, counts = _merge(
        a_out, b_out, pg, x2d, w_branch_a.astype(BF16), w_branch_b.astype(BF16), w_out.astype(BF16),
        ln1_g.reshape(1, d), ln1_b.reshape(1, d), w_router, b_router.reshape(1, N_EXPERTS), alpha)

    counts = counts.reshape(N_EXPERTS).astype(I32)
    aligned = (counts + MOE_ALIGN - 1) // MOE_ALIGN * MOE_ALIGN
    ends = jnp.cumsum(aligned).astype(I32)
    starts = ends - aligned
    bound = s * TOP_K + N_EXPERTS * MOE_ALIGN + 2 * MOE_CHUNK
    n_slots = (bound + MOE_CHUNK - 1) // MOE_CHUNK * MOE_CHUNK
    experts = jnp.arange(N_EXPERTS, dtype=I32)
    start_of = jnp.sum(jnp.where(eidx[..., None] == experts, starts, 0), axis=-1)
    dest = (start_of + pos).reshape(s * TOP_K).astype(I32)
    total = (ends[-1:] + MOE_CHUNK - 1) // MOE_CHUNK * MOE_CHUNK
    gap_len = aligned - counts + jnp.where(experts == N_EXPERTS - 1, total[0] - ends[-1], 0)
    group_starts = jnp.concatenate([starts, total])
    group_counts = jnp.concatenate([counts + jnp.where(experts == N_EXPERTS - 1, gap_len, 0), n_slots - total])

    xs = _dispatch(h, dest, starts + counts, gap_len, total, n_slots)
    ys = _experts(xs, group_starts, group_counts, w_gate_up, b_gate_up, w_down, b_down)
    return _combine(ys, dest, gates, h, ln2_g.reshape(1, d), ln2_b.reshape(1, d), alpha)


def kernel(x, w_in, b_forget, w_branch_a, w_branch_b, w_out, ln1_g, ln1_b, w_router, b_router,
           w_gate_up, b_gate_up, w_down, b_down, ln2_g, ln2_b):
    bsz, s, d = x.shape
    tables = _rope_tables(s)
    outs = []
    for bi in range(bsz):
        xb = x[bi]
        for l in range(DEPTH):
            xb = _layer(xb, w_in[l], b_forget[l], w_branch_a[l], w_branch_b[l], w_out[l], ln1_g[l], ln1_b[l],
                        w_router[l], b_router[l], w_gate_up[l], b_gate_up[l], w_down[l], b_down[l],
                        ln2_g[l], ln2_b[l], tables)
        outs.append(xb)
    return outs[0][None] if bsz == 1 else jnp.stack(outs)
```

```python
import functools

import numpy as np
import jax
import jax.numpy as jnp
from jax import lax
from jax.experimental import pallas as pl
from jax.experimental.pallas import tpu as pltpu

F32 = jnp.float32
BF16 = jnp.bfloat16
I32 = jnp.int32

D_MODEL = 2048
DEPTH = 1
CHUNK = 64
HEAD_DIM = 128
FOX_HEADS = 8
DSA_HEADS = 8
DSA_KV_HEADS = 2
DSA_GROUP = DSA_HEADS // DSA_KV_HEADS
IDX_HEADS = 16
IDX_DIM = 64
TOPK_MAX = 256
ROPE_THETA = 500000.0
ROT_FRACTION_DEN = 4
MIX_A = FOX_HEADS * HEAD_DIM
MIX_B = DSA_HEADS * HEAD_DIM
N_EXPERTS = 32
TOP_K = 4
D_FF = D_MODEL
SWIGLU_ALPHA = 1.702
SWIGLU_LIMIT = 7.0
LN_EPS = 1e-5

MIB = 1024 * 1024
NEG_BIG = -1e30
LOG2E = 1.4426950408889634
NT_DIMS = (((1,), (1,)), ((), ()))
INT_MIN = -(2 ** 31)

COL_AQ = 0
COL_AK = 1024
COL_AV = 2048
COL_BQ = 3072
COL_BK = 4096
COL_BV = 4352
COL_IQ = 4608
COL_IK = 5632
N_PBF = 5888
PROJ_TN = 256
ROPE_TABLE_W = 128
ROPE_HEAD_TILES = (12, 13, 14, 15, 16)
ROPE_IDX_TILES = (18, 19, 20, 21, 22)
COL_GA = 0
COL_GB = 2048
COL_SMALL = 4096
N_PG = 4224
PG_TN = 384

MOE_CHUNK = 256
MOE_ALIGN = 16
ROW_TILE = 8


V7X_VMEM_MIB = 64
VMEM_LIMIT_MIB = dict(
    w_in_prep=48, proj_bf16=60, proj_f32=60, forget_cumsum=32, fox_attention=56, dsa_attention=56,
    merge_ln_router=56, moe_dispatch=32, moe_gate_up=60, moe_down=56, moe_combine=32)
assert max(VMEM_LIMIT_MIB.values()) < V7X_VMEM_MIB


def _cparams(dims, name):
    return pltpu.CompilerParams(dimension_semantics=dims, vmem_limit_bytes=VMEM_LIMIT_MIB[name] * MIB)


IN_SIZES = (MIX_A, MIX_A, MIX_A, FOX_HEADS, MIX_B, DSA_KV_HEADS * HEAD_DIM, DSA_KV_HEADS * HEAD_DIM,
            IDX_HEADS * IDX_DIM, IDX_DIM, IDX_HEADS, D_MODEL, D_MODEL)
IN_OFF = dict(zip(("aq", "ak", "av", "af", "bq", "bk", "bv", "iq", "ik", "iw", "ga", "gb", "end"),
                  np.concatenate([[0], np.cumsum(IN_SIZES)]).tolist()))


def _prep_kernel(w_ref, wbf_ref, wg_ref, *, scale):
    o = IN_OFF
    width = w_ref.shape[1]

    def rows(a, b, mult=None):
        v = w_ref[a:b, :]
        return v if mult is None else v * mult

    wbf_ref[COL_AQ:COL_AK, :] = rows(o["aq"], o["ak"], scale).astype(BF16)
    wbf_ref[COL_AK:COL_BQ, :] = rows(o["ak"], o["af"]).astype(BF16)
    wbf_ref[COL_BQ:COL_BK, :] = rows(o["bq"], o["bk"], scale).astype(BF16)
    wbf_ref[COL_BK:COL_IK, :] = rows(o["bk"], o["ik"]).astype(BF16)
    wbf_ref[COL_IK:N_PBF, :] = jnp.concatenate(
        [rows(o["ik"], o["iw"]), jnp.zeros((N_PBF - COL_IK - IDX_DIM, width), F32)], axis=0).astype(BF16)
    wg_ref[COL_GA:COL_SMALL, :] = rows(o["ga"], o["end"]).astype(BF16)
    wg_ref[COL_SMALL:N_PG, :] = jnp.concatenate(
        [rows(o["af"], o["bq"]), rows(o["iw"], o["ga"]),
         jnp.zeros((N_PG - COL_SMALL - FOX_HEADS - IDX_HEADS, width), F32)], axis=0).astype(BF16)


def _prepare_w_in(w_in_t, scale):
    n_in, d = w_in_t.shape
    tc = 256
    return pl.pallas_call(
        functools.partial(_prep_kernel, scale=scale),
        grid=(d // tc,),
        in_specs=[pl.BlockSpec((n_in, tc), lambda i: (0, i))],
        out_specs=[pl.BlockSpec((N_PBF, tc), lambda i: (0, i)), pl.BlockSpec((N_PG, tc), lambda i: (0, i))],
        out_shape=[jax.ShapeDtypeStruct((N_PBF, d), BF16), jax.ShapeDtypeStruct((N_PG, d), BF16)],
        compiler_params=_cparams(("arbitrary",), "w_in_prep"),
        name="w_in_prep",
    )(w_in_t)


def _tile_in(j, tiles):
    cond = j == tiles[0]
    for t in tiles[1:]:
        cond = cond | (j == t)
    return cond


def _proj_rope_kernel(x_ref, w_ref, tab_ref, o_ref, xb_ref):
    j = pl.program_id(1)

    @pl.when(j == 0)
    def _():
        xb_ref[...] = x_ref[...].astype(BF16)

    acc = lax.dot_general(xb_ref[...], w_ref[...], NT_DIMS, preferred_element_type=F32)
    tn = acc.shape[1]
    is_head = _tile_in(j, ROPE_HEAD_TILES)
    is_idx = _tile_in(j, ROPE_IDX_TILES)

    def rope(shift):
        reps = tn // ROPE_TABLE_W
        c = jnp.tile(tab_ref[0, 0], (1, reps))
        s_prev = jnp.tile(tab_ref[0, 1], (1, reps))
        s_next = jnp.tile(tab_ref[0, 2], (1, reps))
        out = acc * c + pltpu.roll(acc, shift, 1) * s_prev + pltpu.roll(acc, tn - shift, 1) * s_next
        o_ref[...] = out.astype(o_ref.dtype)

    @pl.when(is_head)
    def _():
        rope(HEAD_DIM // ROT_FRACTION_DEN // 2)

    @pl.when(is_idx)
    def _():
        rope(IDX_DIM // ROT_FRACTION_DEN // 2)

    @pl.when(jnp.logical_not(is_head | is_idx))
    def _():
        o_ref[...] = acc.astype(o_ref.dtype)


def _proj_plain_kernel(x_ref, w_ref, o_ref, xb_ref):
    j = pl.program_id(1)

    @pl.when(j == 0)
    def _():
        xb_ref[...] = x_ref[...].astype(BF16)

    o_ref[...] = lax.dot_general(xb_ref[...], w_ref[...], NT_DIMS, preferred_element_type=F32).astype(o_ref.dtype)


def _rope_group(j):
    return jnp.where(_tile_in(j, ROPE_IDX_TILES), 1, 0)


def _project(x2d, w_bf, tables, w_g):
    s, d = x2d.shape
    tm = min(2048, s)
    pbf = pl.pallas_call(
        _proj_rope_kernel,
        grid=(s // tm, N_PBF // PROJ_TN),
        in_specs=[
            pl.BlockSpec((tm, d), lambda i, j: (i, 0)),
            pl.BlockSpec((PROJ_TN, d), lambda i, j: (j, 0)),
            pl.BlockSpec((1, 3, tm, ROPE_TABLE_W), lambda i, j: (_rope_group(j), 0, i, 0)),
        ],
        out_specs=pl.BlockSpec((tm, PROJ_TN), lambda i, j: (i, j)),
        out_shape=jax.ShapeDtypeStruct((s, N_PBF), BF16),
        scratch_shapes=[pltpu.VMEM((tm, d), BF16)],
        compiler_params=_cparams(("arbitrary", "arbitrary"), "proj_bf16"),
        name="proj_bf16",
    )(x2d, w_bf, tables)
    pg = pl.pallas_call(
        _proj_plain_kernel,
        grid=(s // tm, N_PG // PG_TN),
        in_specs=[
            pl.BlockSpec((tm, d), lambda i, j: (i, 0)),
            pl.BlockSpec((PG_TN, d), lambda i, j: (j, 0)),
        ],
        out_specs=pl.BlockSpec((tm, PG_TN), lambda i, j: (i, j)),
        out_shape=jax.ShapeDtypeStruct((s, N_PG), F32),
        scratch_shapes=[pltpu.VMEM((tm, d), BF16)],
        compiler_params=_cparams(("arbitrary", "arbitrary"), "proj_f32"),
        name="proj_f32",
    )(x2d, w_g)
    return pbf, pg


def _rope_tables(s):
    pos = jnp.arange(s, dtype=F32)

    def one(period):
        rot = period // ROT_FRACTION_DEN
        half = rot // 2
        inv = jnp.power(ROPE_THETA, -jnp.arange(0, rot, 2, dtype=F32) / rot)
        ang = pos[:, None] * inv[None, :]
        cos, sin = jnp.cos(ang), jnp.sin(ang)
        zero = jnp.zeros((s, period - rot), F32)
        c = jnp.concatenate([cos, cos, jnp.ones((s, period - rot), F32)], axis=1)
        s_prev = jnp.concatenate([jnp.zeros((s, half), F32), sin, zero], axis=1)
        s_next = jnp.concatenate([-sin, jnp.zeros((s, half), F32), zero], axis=1)
        reps = ROPE_TABLE_W // period
        return jnp.stack([jnp.tile(c, (1, reps)), jnp.tile(s_prev, (1, reps)), jnp.tile(s_next, (1, reps))])

    return jnp.stack([one(HEAD_DIM), one(IDX_DIM)])


def _cumsum_kernel(af_ref, bf_ref, c_ref, carry_ref):
    i = pl.program_id(0)

    @pl.when(i == 0)
    def _():
        carry_ref[...] = jnp.zeros_like(carry_ref)

    z = af_ref[...] + bf_ref[...]
    logf = jnp.minimum(z, 0.0) - jnp.log1p(jnp.exp(-jnp.abs(z)))
    t = z.shape[1]
    row = lax.broadcasted_iota(I32, (t, t), 0)
    col = lax.broadcasted_iota(I32, (t, t), 1)
    upper = (row <= col).astype(F32)
    c = jnp.dot(logf, upper, preferred_element_type=F32, precision=lax.Precision.HIGHEST) + carry_ref[...]
    c_ref[...] = c * LOG2E
    carry_ref[...] = c[:, t - 1:t]


def _forget_cumsum(af_t, b_forget):
    h, s = af_t.shape
    t = min(512, s)
    return pl.pallas_call(
        _cumsum_kernel,
        grid=(s // t,),
        in_specs=[pl.BlockSpec((h, t), lambda i: (0, i)), pl.BlockSpec((h, 1), lambda i: (0, 0))],
        out_specs=pl.BlockSpec((h, t), lambda i: (0, i)),
        out_shape=jax.ShapeDtypeStruct((h, s), F32),
        scratch_shapes=[pltpu.VMEM((h, 1), F32)],
        compiler_params=_cparams(("arbitrary",), "forget_cumsum"),
        name="forget_cumsum",
    )(af_t, b_forget.reshape(h, 1).astype(F32))


def _softmax_init(m_ref, l_ref, acc_ref):
    m_ref[...] = jnp.full(m_ref.shape, NEG_BIG, F32)
    l_ref[...] = jnp.zeros(l_ref.shape, F32)
    acc_ref[...] = jnp.zeros(acc_ref.shape, F32)


def _softmax_tile(s, v, m_ref, l_ref, acc_ref, i):
    reps = s.shape[1] // 128
    m_old = m_ref[i]
    m_new = jnp.maximum(m_old, jnp.max(s, axis=1, keepdims=True))
    p = jnp.exp2(s - jnp.tile(m_new, (1, reps)))
    alpha = jnp.exp2(m_old - m_new)
    psum = p[:, :128]
    for c in range(1, reps):
        psum = psum + p[:, c * 128:(c + 1) * 128]
    l_ref[i] = alpha * l_ref[i] + psum
    acc_ref[i] = alpha * acc_ref[i] + jnp.dot(p.astype(BF16), v, preferred_element_type=F32)
    m_ref[i] = m_new


def _softmax_result(l_ref, acc_ref, i):
    return acc_ref[i] / jnp.sum(l_ref[i], axis=1, keepdims=True)


def _fox_kernel(q_ref, k_ref, v_ref, c_ref, o_ref, m_ref, l_ref, acc_ref, *, tq, tk, nh):
    qi = pl.program_id(1)
    _softmax_init(m_ref, l_ref, acc_ref)

    def tile(kt, masked):
        start = pl.multiple_of(kt * tk, tk)
        if masked:
            row = qi * tq + lax.broadcasted_iota(I32, (tq, tk), 0)
            col = kt * tk + lax.broadcasted_iota(I32, (tq, tk), 1)
            causal = col <= row
        for h in range(nh):
            hs = slice(h * HEAD_DIM, (h + 1) * HEAD_DIM)
            k = k_ref[pl.ds(start, tk), hs]
            v = v_ref[pl.ds(start, tk), hs]
            s = lax.dot_general(q_ref[:, hs], k, (((1,), (1,)), ((), ())), preferred_element_type=F32)
            s = s - c_ref[h, kt]
            if masked:
                s = jnp.where(causal, s, NEG_BIG)
            _softmax_tile(s, v, m_ref, l_ref, acc_ref, h)

    n_full = (qi * tq) // tk

    def body(kt, _):
        tile(kt, False)
        return 0

    lax.fori_loop(0, n_full, body, 0)
    for t in range((tq + tk - 1) // tk):
        tile(n_full + t, True)
    for h in range(nh):
        o_ref[:, h * HEAD_DIM:(h + 1) * HEAD_DIM] = _softmax_result(l_ref, acc_ref, h).astype(o_ref.dtype)


def _fox_attention(pbf, c_tiles, s, tq, tk, nh):
    cb = nh * HEAD_DIM
    return pl.pallas_call(
        functools.partial(_fox_kernel, tq=tq, tk=tk, nh=nh),
        grid=(FOX_HEADS // nh, s // tq),
        in_specs=[
            pl.BlockSpec((tq, cb), lambda g, i: (i, COL_AQ // cb + g)),
            pl.BlockSpec((s, cb), lambda g, i: (0, COL_AK // cb + g)),
            pl.BlockSpec((s, cb), lambda g, i: (0, COL_AV // cb + g)),
            pl.BlockSpec((nh, s // tk, 1, tk), lambda g, i: (g, 0, 0, 0)),
        ],
        out_specs=pl.BlockSpec((tq, cb), lambda g, i: (i, g)),
        out_shape=jax.ShapeDtypeStruct((s, MIX_A), BF16),
        scratch_shapes=[
            pltpu.VMEM((nh, tq, 128), F32),
            pltpu.VMEM((nh, tq, 128), F32),
            pltpu.VMEM((nh, tq, HEAD_DIM), F32),
        ],
        compiler_params=_cparams(("arbitrary", "arbitrary"), "fox_attention"),
        name="fox_attention",
    )(pbf, pbf, pbf, c_tiles)


def _dsa_kernel(bq_ref, iq_lo_ref, iq_hi_ref, iw_ref, ik_ref, bk_ref, bv_ref, o_ref,
                key_ref, cnt_ref, qg_ref, m_ref, l_ref, acc_ref, *, tq, tk, n_sel):
    b = pl.program_id(0)
    n_tiles = (b * tq) // tk + 1
    row_g = b * tq + lax.broadcasted_iota(I32, (tq, tk), 0)
    adm_end = (row_g // CHUNK + 1) * CHUNK
    col_l = lax.broadcasted_iota(I32, (tq, tk), 1)

    idx_scale = (IDX_HEADS ** -0.5) * (IDX_DIM ** -0.5)
    iw = iw_ref[...][:, 8:8 + IDX_HEADS] * idx_scale
    iq = jnp.concatenate([iq_lo_ref[...], iq_hi_ref[...]], axis=1)

    def score_tile(kt, _):
        start = pl.multiple_of(kt * tk, tk)
        ik = ik_ref[pl.ds(start, tk), :][:, :IDX_DIM]
        acc = jnp.zeros((tq, tk), F32)
        for h in range(IDX_HEADS):
            a = iq[:, h * IDX_DIM:(h + 1) * IDX_DIM]
            rel = lax.dot_general(a, ik, (((1,), (1,)), ((), ())), preferred_element_type=F32)
            acc = acc + jnp.maximum(rel, 0.0) * iw[:, h:h + 1]
        key_ref[kt] = jnp.where(kt * tk + col_l < adm_end, acc, -jnp.inf)
        return 0

    lax.fori_loop(0, n_tiles, score_tile, 0)

    def float_of(code_u):
        code = code_u ^ INT_MIN
        return pltpu.bitcast(jnp.where(code >= 0, code, code ^ 0x7FFFFFFF), F32)

    def count(above, cand):
        cnt_ref[...] = jnp.zeros(cnt_ref.shape, I32)

        def body(kt, _):
            for r in range(tq // 128):
                rs = slice(r * 128, (r + 1) * 128)
                part = cnt_ref[rs, :]
                for c in range(tk // 128):
                    part = part + jnp.where(above(key_ref[kt, rs, c * 128:(c + 1) * 128], cand[rs]), 1, 0)
                cnt_ref[rs, :] = part
            return 0
        lax.fori_loop(0, n_tiles, body, 0)
        return jnp.sum(cnt_ref[...], axis=1, keepdims=True)

    def count_ge(cand):
        return count(lambda a, b: a >= b, cand)

    def search_cond(carry):
        i, _, _, pending = carry
        return (i < 32) & (pending > 0)

    def refine(i, t_u, hit):
        cand_u = t_u | lax.shift_left(jnp.int32(1), 31 - i)
        cnt = count_ge(float_of(cand_u))
        return jnp.where(cnt >= n_sel, cand_u, t_u), jnp.where(cnt == n_sel, 1, hit)

    def search_step(carry):
        i, t_u, hit, _ = carry
        t_u, hit = refine(i, t_u, hit)
        return i + 1, t_u, hit, jnp.sum(1 - hit)

    zeros = jnp.zeros((tq, 1), I32)
    lead = 12
    t_u, hit = lax.fori_loop(0, lead, lambda i, c: refine(i, *c), (zeros, zeros))
    _, t_u, hit, _ = lax.while_loop(search_cond, search_step, (jnp.int32(lead), t_u, hit, jnp.sum(1 - hit)))
    thr = jnp.where(t_u == 0, jnp.finfo(F32).min, float_of(t_u))
    tied_rows = jnp.sum(jnp.where((hit == 0) & (t_u != 0), 1, 0))

    for j in range(DSA_KV_HEADS):
        for g in range(DSA_GROUP):
            hd = j * DSA_GROUP + g
            qg_ref[j, g * tq:(g + 1) * tq, :] = bq_ref[:, hd * HEAD_DIM:(hd + 1) * HEAD_DIM]
    _softmax_init(m_ref, l_ref, acc_ref)

    def attend(kt, sel):
        start = pl.multiple_of(kt * tk, tk)
        for j in range(DSA_KV_HEADS):
            hs = slice(j * HEAD_DIM, (j + 1) * HEAD_DIM)
            k = bk_ref[pl.ds(start, tk), hs]
            v = bv_ref[pl.ds(start, tk), hs]
            s = lax.dot_general(qg_ref[j], k, (((1,), (1,)), ((), ())), preferred_element_type=F32)
            s = jnp.where(sel[None], s.reshape(DSA_GROUP, tq, tk), NEG_BIG).reshape(DSA_GROUP * tq, tk)
            _softmax_tile(s, v, m_ref, l_ref, acc_ref, j)

    @pl.when(tied_rows == 0)
    def _():
        def attn_tile(kt, _):
            attend(kt, key_ref[kt] >= thr)
            return 0
        lax.fori_loop(0, n_tiles, attn_tile, 0)

    @pl.when(tied_rows > 0)
    def _():
        ties_wanted = (n_sel - count(lambda a, b: a > b, thr)).astype(F32)
        r_i = lax.broadcasted_iota(I32, (tk, tk), 0)
        c_i = lax.broadcasted_iota(I32, (tk, tk), 1)
        tri = jnp.where(r_i <= c_i, 1.0, 0.0).astype(BF16)

        def attn_tile(kt, ties_seen):
            key = key_ref[kt]
            tie = key == thr
            rank = ties_seen + jnp.dot(jnp.where(tie, 1.0, 0.0).astype(BF16), tri, preferred_element_type=F32)
            attend(kt, (key > thr) | (tie & (rank <= ties_wanted)))
            return rank[:, tk - 1:tk]
        lax.fori_loop(0, n_tiles, attn_tile, jnp.zeros((tq, 1), F32))

    for j in range(DSA_KV_HEADS):
        o = _softmax_result(l_ref, acc_ref, j)
        for g in range(DSA_GROUP):
            hd = j * DSA_GROUP + g
            o_ref[:, hd * HEAD_DIM:(hd + 1) * HEAD_DIM] = o[g * tq:(g + 1) * tq].astype(o_ref.dtype)


def _dsa_attention(pbf, pg, s, tq, tk):
    n_sel = min(TOPK_MAX, s // 4)
    kvw = DSA_KV_HEADS * HEAD_DIM
    rows = DSA_GROUP * tq
    iqw = IDX_HEADS * IDX_DIM // 2
    return pl.pallas_call(
        functools.partial(_dsa_kernel, tq=tq, tk=tk, n_sel=n_sel),
        grid=(s // tq,),
        in_specs=[
            pl.BlockSpec((tq, MIX_B), lambda b: (b, COL_BQ // MIX_B)),
            pl.BlockSpec((tq, iqw), lambda b: (b, COL_IQ // iqw)),
            pl.BlockSpec((tq, iqw), lambda b: (b, COL_IQ // iqw + 1)),
            pl.BlockSpec((tq, 128), lambda b: (b, COL_SMALL // 128)),
            pl.BlockSpec((s, 128), lambda b: (0, COL_IK // 128)),
            pl.BlockSpec((s, kvw), lambda b: (0, COL_BK // kvw)),
            pl.BlockSpec((s, kvw), lambda b: (0, COL_BV // kvw)),
        ],
        out_specs=pl.BlockSpec((tq, MIX_B), lambda b: (b, 0)),
        out_shape=jax.ShapeDtypeStruct((s, MIX_B), BF16),
        scratch_shapes=[
            pltpu.VMEM((s // tk, tq, tk), F32),
            pltpu.VMEM((tq, 128), I32),
            pltpu.VMEM((DSA_KV_HEADS, rows, HEAD_DIM), BF16),
            pltpu.VMEM((DSA_KV_HEADS, rows, 128), F32),
            pltpu.VMEM((DSA_KV_HEADS, rows, 128), F32),
            pltpu.VMEM((DSA_KV_HEADS, rows, HEAD_DIM), F32),
        ],
        compiler_params=_cparams(("arbitrary",), "dsa_attention"),
        name="dsa_attention",
    )(pbf, pbf, pbf, pg, pbf, pbf, pbf)


def _layer_norm(z, g, b):
    mu = jnp.mean(z, axis=-1, keepdims=True)
    zc = z - mu
    var = jnp.mean(zc * zc, axis=-1, keepdims=True)
    return zc * lax.rsqrt(var + LN_EPS) * g + b


def _merge_kernel(a_ref, b_ref, ga_ref, gb_ref, x_ref, wa_ref, wb_ref, wo_ref, g_ref, beta_ref,
                  wr_ref, br_ref, h_ref, eidx_ref, gate_ref, pos_ref, cnt_ref, carry_ref, *, alpha):
    i = pl.program_id(0)

    @pl.when(i == 0)
    def _():
        carry_ref[...] = jnp.zeros_like(carry_ref)

    ma = jnp.dot(a_ref[...], wa_ref[...], preferred_element_type=F32)
    mb = jnp.dot(b_ref[...], wb_ref[...], preferred_element_type=F32)
    merged = jax.nn.sigmoid(ga_ref[...]) * ma + jax.nn.sigmoid(gb_ref[...]) * mb
    y = jnp.dot(merged.astype(BF16), wo_ref[...], preferred_element_type=F32)
    h = _layer_norm(alpha * x_ref[...] + y, g_ref[...], beta_ref[...])
    h_ref[...] = h

    wr = wr_ref[...]
    h_hi = h.astype(BF16)
    h_lo = (h - h_hi.astype(F32)).astype(BF16)
    wr_hi = wr.astype(BF16)
    wr_lo = (wr - wr_hi.astype(F32)).astype(BF16)
    logits = (jnp.dot(h_hi, wr_hi, preferred_element_type=F32) + jnp.dot(h_lo, wr_hi, preferred_element_type=F32)
              + jnp.dot(h_hi, wr_lo, preferred_element_type=F32) + br_ref[...])
    tm, ne = logits.shape
    lane = lax.broadcasted_iota(I32, (tm, ne), 1)
    lane_k = lax.broadcasted_iota(I32, (tm, TOP_K), 1)
    work = logits
    vals, sels = [], []
    eidx = jnp.zeros((tm, TOP_K), I32)
    onehot = jnp.zeros((tm, ne), F32)
    for k in range(TOP_K):
        mv = jnp.max(work, axis=1, keepdims=True)
        idx = jnp.min(jnp.where(work == mv, lane, ne), axis=1, keepdims=True)
        sel = lane == idx
        vals.append(mv)
        sels.append(sel)
        eidx = jnp.where(lane_k == k, idx, eidx)
        onehot = onehot + jnp.where(sel, 1.0, 0.0)
        work = jnp.where(sel, -jnp.inf, work)
    exps = [jnp.exp(v - vals[0]) for v in vals]
    denom = exps[0] + exps[1] + exps[2] + exps[3]
    gates = jnp.zeros((tm, TOP_K), F32)
    for k in range(TOP_K):
        gates = jnp.where(lane_k == k, exps[k] / denom, gates)

    r_i = lax.broadcasted_iota(I32, (tm, tm), 0)
    c_i = lax.broadcasted_iota(I32, (tm, tm), 1)
    lower = jnp.where(c_i < r_i, 1.0, 0.0).astype(BF16)
    rank = carry_ref[...] + jnp.dot(lower, onehot.astype(BF16), preferred_element_type=F32)
    pos = jnp.zeros((tm, TOP_K), I32)
    for k in range(TOP_K):
        pk = jnp.sum(jnp.where(sels[k], rank, 0.0), axis=1, keepdims=True).astype(I32)
        pos = jnp.where(lane_k == k, pk, pos)
    carry_ref[...] = carry_ref[...] + jnp.sum(onehot, axis=0, keepdims=True)

    eidx_ref[...] = eidx
    gate_ref[...] = gates
    pos_ref[...] = pos
    cnt_ref[...] = carry_ref[...].astype(I32)


def _merge(a_out, b_out, pg, x2d, wa, wb, wo, ln_g, ln_b, w_router, b_router, alpha):
    s, d = x2d.shape
    tm = min(256, s)
    full = lambda shape: pl.BlockSpec(shape, lambda i: (0,) * len(shape))
    return pl.pallas_call(
        functools.partial(_merge_kernel, alpha=alpha),
        grid=(s // tm,),
        in_specs=[
            pl.BlockSpec((tm, MIX_A), lambda i: (i, 0)),
            pl.BlockSpec((tm, MIX_B), lambda i: (i, 0)),
            pl.BlockSpec((tm, d), lambda i: (i, COL_GA // d)),
            pl.BlockSpec((tm, d), lambda i: (i, COL_GB // d)),
            pl.BlockSpec((tm, d), lambda i: (i, 0)),
            full((MIX_A, d)), full((MIX_B, d)), full((d, d)),
            full((1, d)), full((1, d)), full((d, N_EXPERTS)), full((1, N_EXPERTS)),
        ],
        out_specs=[
            pl.BlockSpec((tm, d), lambda i: (i, 0)),
            pl.BlockSpec((tm, TOP_K), lambda i: (i, 0)),
            pl.BlockSpec((tm, TOP_K), lambda i: (i, 0)),
            pl.BlockSpec((tm, TOP_K), lambda i: (i, 0)),
            full((1, N_EXPERTS)),
        ],
        out_shape=[
            jax.ShapeDtypeStruct((s, d), F32),
            jax.ShapeDtypeStruct((s, TOP_K), I32),
            jax.ShapeDtypeStruct((s, TOP_K), F32),
            jax.ShapeDtypeStruct((s, TOP_K), I32),
            jax.ShapeDtypeStruct((1, N_EXPERTS), I32),
        ],
        scratch_shapes=[pltpu.VMEM((1, N_EXPERTS), F32)],
        compiler_params=_cparams(("arbitrary",), "merge_ln_router"),
        name="merge_ln_router",
    )(a_out, b_out, pg, pg, x2d, wa, wb, wo, ln_g, ln_b, w_router, b_router)


def _dispatch_kernel(dest_ref, gap_start_ref, gap_len_ref, total_ref, h_ref, xs_ref, stage_ref, zero_ref,
                     sem, zsem, *, max_slack):
    i = pl.program_id(0)
    last = pl.num_programs(0) - 1
    slot = lax.rem(i, 2)
    n = dest_ref.shape[0]

    @pl.when(i == 0)
    def _():
        zero_ref[...] = jnp.zeros_like(zero_ref)

        def gap_copy(e, k):
            return pltpu.make_async_copy(zero_ref.at[pl.ds(0, 1)], xs_ref.at[pl.ds(gap_start_ref[e] + k, 1)], zsem)

        n_slots = xs_ref.shape[0]

        def slack_copy(k):
            row = pl.multiple_of(total_ref[0] + k * MOE_CHUNK, MOE_CHUNK)
            return pltpu.make_async_copy(zero_ref, xs_ref.at[pl.ds(row, MOE_CHUNK)], zsem)

        def slack(action):
            for k in range(max_slack):
                @pl.when(total_ref[0] + k * MOE_CHUNK < n_slots)
                def _():
                    action(slack_copy(k))

        def issue(e, _):
            lax.fori_loop(0, gap_len_ref[e], lambda k, c: (gap_copy(e, k).start(), c)[1], 0)
            return 0

        def settle(e, _):
            lax.fori_loop(0, gap_len_ref[e], lambda k, c: (gap_copy(e, k).wait(), c)[1], 0)
            return 0

        lax.fori_loop(0, N_EXPERTS, issue, 0)
        slack(lambda cp: cp.start())
        lax.fori_loop(0, N_EXPERTS, settle, 0)
        slack(lambda cp: cp.wait())

    def drain(sl):
        pltpu.make_async_copy(xs_ref.at[pl.ds(0, n)], xs_ref.at[pl.ds(0, n)], sem.at[sl]).wait()

    @pl.when(i >= 2)
    def _():
        drain(slot)

    stage_ref[slot] = h_ref[...]

    def scatter(sl):
        def start(g, _):
            base = pl.multiple_of(g * ROW_TILE, ROW_TILE)
            for u in range(ROW_TILE * TOP_K):
                r = u // TOP_K
                pltpu.make_async_copy(stage_ref.at[sl, pl.ds(base + r, 1)],
                                      xs_ref.at[pl.ds(dest_ref[g * (ROW_TILE * TOP_K) + u], 1)],
                                      sem.at[sl]).start(priority=u % 2)
            return 0
        lax.fori_loop(0, n // (ROW_TILE * TOP_K), start, 0)

    for parity in range(2):
        @pl.when(slot == parity)
        def _():
            scatter(parity)

    @pl.when(i == last)
    def _():
        drain(slot)

        @pl.when(i >= 1)
        def _():
            drain(1 - slot)


def _dispatch(h, dest_flat, gap_start, gap_len, total, n_slots):
    s, d = h.shape
    tm = min(128, s)
    smem = pl.BlockSpec(memory_space=pltpu.SMEM)
    max_slack = (n_slots - s * TOP_K + MOE_CHUNK - 1) // MOE_CHUNK
    return pl.pallas_call(
        functools.partial(_dispatch_kernel, max_slack=max_slack),
        grid=(s // tm,),
        in_specs=[
            pl.BlockSpec((tm * TOP_K,), lambda i: (i,), memory_space=pltpu.SMEM),
            smem, smem, smem,
            pl.BlockSpec((tm, d), lambda i: (i, 0)),
        ],
        out_specs=pl.BlockSpec(memory_space=pl.ANY),
        out_shape=jax.ShapeDtypeStruct((n_slots, d), h.dtype),
        scratch_shapes=[pltpu.VMEM((2, tm, d), h.dtype), pltpu.VMEM((MOE_CHUNK, d), h.dtype),
                        pltpu.SemaphoreType.DMA((2,)), pltpu.SemaphoreType.DMA(())],
        compiler_params=_cparams(("arbitrary",), "moe_dispatch"),
        name="moe_dispatch",
    )(dest_flat, gap_start, gap_len, total, h)


ROW_DMA_PRIORITY = 1


def _expert_rows_pipeline(n_chunks, in_copy, out_copy, compute):
    @pl.when(n_chunks > 0)
    def _():
        in_copy(0, 0).start(priority=ROW_DMA_PRIORITY)

        def body(c, _):
            slot = lax.rem(c, 2)
            in_copy(c, slot).wait()

            @pl.when(c + 1 < n_chunks)
            def _():
                in_copy(c + 1, 1 - slot).start(priority=ROW_DMA_PRIORITY)

            @pl.when(c >= 2)
            def _():
                out_copy(c - 2, slot).wait()

            compute(slot, c)
            out_copy(c, slot).start(priority=ROW_DMA_PRIORITY)
            return 0

        lax.fori_loop(0, n_chunks, body, 0)

        @pl.when(n_chunks >= 2)
        def _():
            out_copy(n_chunks - 2, lax.rem(n_chunks, 2)).wait()

        out_copy(n_chunks - 1, lax.rem(n_chunks - 1, 2)).wait()


GATE_UP_BAND = 256
DOWN_BAND = 512


def _weight_band_stream(w_hbm, w_stage, sem_w, e, band, n_bands):
    def band_copy(idx, b):
        cols = pl.ds(b * band, band)
        return pltpu.make_async_copy(w_hbm.at[idx, :, cols], w_stage.at[:, cols], sem_w.at[b])

    def start_bands(idx, bands):
        for b in bands:
            band_copy(idx, b).start(priority=b % 2)

    def await_bands(bands):
        for b in bands:
            band_copy(e, b).wait()

    def refill_bands(bands):
        @pl.when(e + 1 < N_EXPERTS)
        def _():
            start_bands(e + 1, bands)

    @pl.when(e == 0)
    def _():
        start_bands(0, range(n_bands))

    return await_bands, refill_bands


def _chunk_rows(start_ref, e, c):
    return pl.ds(pl.multiple_of(start_ref[e] + c * MOE_CHUNK, MOE_ALIGN), MOE_CHUNK)


def _gate_up_kernel(start_ref, cnt_ref, x_hbm, w_hbm, b_ref, act_hbm,
                    w_stage, w_sc, xbuf, obuf, sem_in, sem_out, sem_w):
    e = pl.program_id(0)
    n_chunks = (cnt_ref[e] + MOE_CHUNK - 1) // MOE_CHUNK
    d_ff = obuf.shape[2]

    cw = GATE_UP_BAND
    n_bands = 2 * d_ff // cw
    has_weights = e < N_EXPERTS
    await_bands, refill = _weight_band_stream(w_hbm, w_stage, sem_w, e, cw, n_bands)

    def in_copy(c, slot):
        return pltpu.make_async_copy(x_hbm.at[_chunk_rows(start_ref, e, c)], xbuf.at[slot], sem_in.at[slot])

    def out_copy(c, slot):
        return pltpu.make_async_copy(obuf.at[slot], act_hbm.at[_chunk_rows(start_ref, e, c)],
                                     sem_out.at[slot])

    def rows_times_weights(slot, cast_first):
        x = xbuf[slot].astype(BF16)
        for c in range(d_ff // cw):
            gs = slice(c * cw, (c + 1) * cw)
            us = slice(d_ff + c * cw, d_ff + (c + 1) * cw)
            if cast_first:
                bands = (c, d_ff // cw + c)
                await_bands(bands)
                w_sc[:, gs] = w_stage[:, gs].astype(BF16)
                w_sc[:, us] = w_stage[:, us].astype(BF16)
                refill(bands)
            g = jnp.dot(x, w_sc[:, gs], preferred_element_type=F32) + b_ref[:, gs]
            u = jnp.dot(x, w_sc[:, us], preferred_element_type=F32) + b_ref[:, us]
            g = jnp.minimum(g, SWIGLU_LIMIT)
            u = jnp.clip(u, -SWIGLU_LIMIT, SWIGLU_LIMIT)
            obuf[slot, :, gs] = (g * jax.nn.sigmoid(SWIGLU_ALPHA * g) * (u + 1.0)).astype(obuf.dtype)

    def compute(slot, c):
        fresh = (c == 0) & has_weights

        @pl.when(fresh)
        def _():
            rows_times_weights(slot, True)

        @pl.when(jnp.logical_not(fresh))
        def _():
            rows_times_weights(slot, False)

    @pl.when((n_chunks == 0) & has_weights)
    def _():
        await_bands(range(n_bands))
        refill(range(n_bands))

    _expert_rows_pipeline(n_chunks, in_copy, out_copy, compute)


def _down_kernel(start_ref, cnt_ref, a_hbm, w_hbm, bd_ref, y_hbm, w_stage, wd_sc, abuf, obuf, sem_in, sem_out,
                 sem_w):
    e = pl.program_id(0)
    n_chunks = (cnt_ref[e] + MOE_CHUNK - 1) // MOE_CHUNK

    cw = DOWN_BAND
    n_bands = obuf.shape[2] // cw
    has_weights = e < N_EXPERTS
    await_bands, refill = _weight_band_stream(w_hbm, w_stage, sem_w, e, cw, n_bands)

    @pl.when((n_chunks == 0) & has_weights)
    def _():
        await_bands(range(n_bands))
        refill(range(n_bands))

    def in_copy(c, slot):
        return pltpu.make_async_copy(a_hbm.at[_chunk_rows(start_ref, e, c)], abuf.at[slot],
                                     sem_in.at[slot])

    def out_copy(c, slot):
        return pltpu.make_async_copy(obuf.at[slot], y_hbm.at[_chunk_rows(start_ref, e, c)],
                                     sem_out.at[slot])

    def rows_times_weights(slot, cast_first):
        for b in range(n_bands):
            cs = slice(b * cw, (b + 1) * cw)
            if cast_first:
                await_bands((b,))
                wd_sc[:, cs] = w_stage[:, cs].astype(BF16)
                refill((b,))
            obuf[slot, :, cs] = jnp.dot(abuf[slot], wd_sc[:, cs], preferred_element_type=F32) + bd_ref[:, cs]

    def compute(slot, c):
        fresh = (c == 0) & has_weights

        @pl.when(fresh)
        def _():
            rows_times_weights(slot, True)

        @pl.when(jnp.logical_not(fresh))
        def _():
            rows_times_weights(slot, False)

    _expert_rows_pipeline(n_chunks, in_copy, out_copy, compute)


def _experts(xs, starts, counts, w_gate_up, b_gate_up, w_down, b_down):
    n_slots, d = xs.shape
    any_spec = pl.BlockSpec(memory_space=pl.ANY)
    dma_sems = pltpu.SemaphoreType.DMA((2,))
    bgu = b_gate_up.reshape(N_EXPERTS, 1, 2 * D_FF)
    n_groups = starts.shape[0]
    wi = lambda e: jnp.minimum(e, N_EXPERTS - 1)
    act = pl.pallas_call(
        _gate_up_kernel,
        grid_spec=pltpu.PrefetchScalarGridSpec(
            num_scalar_prefetch=2,
            grid=(n_groups,),
            in_specs=[
                any_spec,
                any_spec,
                pl.BlockSpec((None, 1, 2 * D_FF), lambda e, st, ct: (wi(e), 0, 0)),
            ],
            out_specs=any_spec,
            scratch_shapes=[pltpu.VMEM((d, 2 * D_FF), F32), pltpu.VMEM((d, 2 * D_FF), BF16),
                            pltpu.VMEM((2, MOE_CHUNK, d), xs.dtype), pltpu.VMEM((2, MOE_CHUNK, D_FF), BF16),
                            dma_sems, dma_sems, pltpu.SemaphoreType.DMA((2 * D_FF // GATE_UP_BAND,))],
        ),
        out_shape=jax.ShapeDtypeStruct((n_slots, D_FF), BF16),
        compiler_params=_cparams(("arbitrary",), "moe_gate_up"),
        name="moe_gate_up",
    )(starts, counts, xs, w_gate_up, bgu)
    bd = b_down.reshape(N_EXPERTS, 1, d)
    ys = pl.pallas_call(
        _down_kernel,
        grid_spec=pltpu.PrefetchScalarGridSpec(
            num_scalar_prefetch=2,
            grid=(n_groups,),
            in_specs=[
                any_spec,
                any_spec,
                pl.BlockSpec((None, 1, d), lambda e, st, ct: (wi(e), 0, 0)),
            ],
            out_specs=any_spec,
            scratch_shapes=[pltpu.VMEM((D_FF, d), F32), pltpu.VMEM((D_FF, d), BF16),
                            pltpu.VMEM((2, MOE_CHUNK, D_FF), BF16), pltpu.VMEM((2, MOE_CHUNK, d), F32),
                            dma_sems, dma_sems, pltpu.SemaphoreType.DMA((d // DOWN_BAND,))],
        ),
        out_shape=jax.ShapeDtypeStruct((n_slots, d), F32),
        compiler_params=_cparams(("arbitrary",), "moe_down"),
        name="moe_down",
    )(starts, counts, act, w_down, bd)
    return ys


def _combine_kernel(dest_ref, dest_next_ref, gate_ref, h_ref, g_ref, beta_ref, ys_ref, o_ref, buf_ref, sem,
                    *, alpha):
    i = pl.program_id(0)
    last = pl.num_programs(0) - 1
    slot = lax.rem(i, 2)
    n = dest_ref.shape[0]

    def gather(dref, sl):
        def start(g, _):
            base = pl.multiple_of(g * ROW_TILE, ROW_TILE)
            for u in range(ROW_TILE * TOP_K):
                r, k = divmod(u, TOP_K)
                pltpu.make_async_copy(ys_ref.at[pl.ds(dref[g * (ROW_TILE * TOP_K) + u], 1)],
                                      buf_ref.at[sl, k, pl.ds(base + r, 1)],
                                      sem.at[sl]).start(priority=u % 2)
            return 0
        lax.fori_loop(0, n // (ROW_TILE * TOP_K), start, 0)

    for parity in range(2):
        @pl.when((i == 0) & (slot == parity))
        def _():
            gather(dest_ref, parity)

        @pl.when((i < last) & (slot == parity))
        def _():
            gather(dest_next_ref, 1 - parity)

    pltpu.make_async_copy(buf_ref.at[slot], buf_ref.at[slot], sem.at[slot]).wait()
    gates = gate_ref[...]
    y = gates[:, 0:1] * buf_ref[slot, 0]
    for k in range(1, TOP_K):
        y = y + gates[:, k:k + 1] * buf_ref[slot, k]
    o_ref[...] = _layer_norm(alpha * h_ref[...] + y, g_ref[...], beta_ref[...])


def _combine(ys, dest_flat, gates, h, ln_g, ln_b, alpha):
    s, d = h.shape
    tm = min(128, s)
    n_steps = s // tm
    return pl.pallas_call(
        functools.partial(_combine_kernel, alpha=alpha),
        grid=(n_steps,),
        in_specs=[
            pl.BlockSpec((tm * TOP_K,), lambda i: (i,), memory_space=pltpu.SMEM),
            pl.BlockSpec((tm * TOP_K,), lambda i: (jnp.minimum(i + 1, n_steps - 1),), memory_space=pltpu.SMEM),
            pl.BlockSpec((tm, TOP_K), lambda i: (i, 0)),
            pl.BlockSpec((tm, d), lambda i: (i, 0)),
            pl.BlockSpec((1, d), lambda i: (0, 0)),
            pl.BlockSpec((1, d), lambda i: (0, 0)),
            pl.BlockSpec(memory_space=pl.ANY),
        ],
        out_specs=pl.BlockSpec((tm, d), lambda i: (i, 0)),
        out_shape=jax.ShapeDtypeStruct((s, d), F32),
        scratch_shapes=[pltpu.VMEM((2, TOP_K, tm, d), F32), pltpu.SemaphoreType.DMA((2,))],
        compiler_params=_cparams(("arbitrary",), "moe_combine"),
        name="moe_combine",
    )(dest_flat, dest_flat, gates, h, ln_g, ln_b, ys)


def _layer(x2d, w_in, b_forget, w_branch_a, w_branch_b, w_out, ln1_g, ln1_b, w_router, b_router,
           w_gate_up, b_gate_up, w_down, b_down, ln2_g, ln2_b, tables):
    s, d = x2d.shape
    alpha = (2.0 * DEPTH) ** 0.25
    scale = HEAD_DIM ** -0.5 * LOG2E

    w_bf, w_g = _prepare_w_in(w_in.T, scale)
    pbf, pg = _project(x2d, w_bf, tables, w_g)

    fox_tq = min(1024, s)
    fox_tk = min(512, s)
    af_t = pg[:, COL_SMALL:COL_SMALL + FOX_HEADS].T
    c_t = _forget_cumsum(af_t, b_forget)
    c_tiles = c_t.reshape(FOX_HEADS, s // fox_tk, 1, fox_tk)
    a_out = _fox_attention(pbf, c_tiles, s, fox_tq, fox_tk, 4)

    b_out = _dsa_attention(pbf, pg, s, min(256, s), min(512, s))

    h, eidx, gates, pos, counts = _merge(
        a_out, b_out, pg, x2d, w_branch_a.astype(BF16), w_branch_b.astype(BF16), w_out.astype(BF16),
        ln1_g.reshape(1, d), ln1_b.reshape(1, d), w_router, b_router.reshape(1, N_EXPERTS), alpha)

    counts = counts.reshape(N_EXPERTS).astype(I32)
    aligned = (counts + MOE_ALIGN - 1) // MOE_ALIGN * MOE_ALIGN
    ends = jnp.cumsum(aligned).astype(I32)
    starts = ends - aligned
    bound = s * TOP_K + N_EXPERTS * MOE_ALIGN + 2 * MOE_CHUNK
    n_slots = (bound + MOE_CHUNK - 1) // MOE_CHUNK * MOE_CHUNK
    experts = jnp.arange(N_EXPERTS, dtype=I32)
    start_of = jnp.sum(jnp.where(eidx[..., None] == experts, starts, 0), axis=-1)
    dest = (start_of + pos).reshape(s * TOP_K).astype(I32)
    total = (ends[-1:] + MOE_CHUNK - 1) // MOE_CHUNK * MOE_CHUNK
    gap_len = aligned - counts + jnp.where(experts == N_EXPERTS - 1, total[0] - ends[-1], 0)
    group_starts = jnp.concatenate([starts, total])
    group_counts = jnp.concatenate([counts + jnp.where(experts == N_EXPERTS - 1, gap_len, 0), n_slots - total])

    xs = _dispatch(h, dest, starts + counts, gap_len, total, n_slots)
    ys = _experts(xs, group_starts, group_counts, w_gate_up, b_gate_up, w_down, b_down)
    return _combine(ys, dest, gates, h, ln2_g.reshape(1, d), ln2_b.reshape(1, d), alpha)


def kernel(x, w_in, b_forget, w_branch_a, w_branch_b, w_out, ln1_g, ln1_b, w_router, b_router,
           w_gate_up, b_gate_up, w_down, b_down, ln2_g, ln2_b):
    bsz, s, d = x.shape
    tables = _rope_tables(s)
    outs = []
    for bi in range(bsz):
        xb = x[bi]
        for l in range(DEPTH):
            xb = _layer(xb, w_in[l], b_forget[l], w_branch_a[l], w_branch_b[l], w_out[l], ln1_g[l], ln1_b[l],
                        w_router[l], b_router[l], w_gate_up[l], b_gate_up[l], w_down[l], b_down[l],
                        ln2_g[l], ln2_b[l], tables)
        outs.append(xb)
    return outs[0][None] if bsz == 1 else jnp.stack(outs)
```
